```python
import numpy as np
import jax
import jax.numpy as jnp
from jax import lax

D_MODEL = 2048
BATCH = 4
SEQ = 2048
DEPTH = 1

H_A = 16
HD_A = 64
D_A = H_A * HD_A
DILATED_BRANCHES = ((128, 1), (512, 4), (2048, 16))
N_BUCKETS = 32
REL_MAX_DIST = 1024
H_B = 8
HD_B = 128
D_B = H_B * HD_B
CONV_W = 5
CHUNK = 64
IN_COLS = 3 * D_A + 4 * D_B + 4 * H_B
N_GROUPS = 4
EXPERTS_PER_GROUP = 16
N_EXPERTS = N_GROUPS * EXPERTS_PER_GROUP
TOP_K = 2
D_EXPERT = 1024
MOE_BLOCK = 128
EPS = 1e-6
NEG = -1e30

kernel_name = 'hybrid_dilated_gdn_hmoe_block'


def rmsnorm(x, g):
    xf = x.astype(jnp.float32)
    y = xf * lax.rsqrt(jnp.mean(xf * xf, axis=-1, keepdims=True) + EPS)
    return (y * g.astype(jnp.float32)).astype(x.dtype)


def l2norm(x):
    return x * lax.rsqrt(jnp.sum(x * x, axis=-1, keepdims=True) + EPS)


def t5_bucket(rel):
    half = N_BUCKETS // 2
    max_exact = half // 2
    n = np.abs(rel)
    large = max_exact + (np.log(np.maximum(n, 1) / max_exact) / np.log(REL_MAX_DIST / max_exact)
                         * (half - max_exact)).astype(np.int32)
    large = np.minimum(large, half - 1)
    return (np.where(rel > 0, half, 0) + np.where(n < max_exact, n, large)).astype(np.int32)


def dilated_window_branch(q, k, v, rel_bias, window, dilation):
    B, S, H, hd = q.shape
    n = window // (2 * dilation)
    Q = n
    L = S // dilation
    NB = -(-L // Q)
    Lp = NB * Q
    R = dilation
    qs, ks, vs = (t.reshape(B, L, R, H, hd) for t in (q, k, v))
    qb = jnp.pad(qs, ((0, 0), (0, Lp - L), (0, 0), (0, 0), (0, 0))).reshape(B, NB, Q, R, H, hd)

    def key_blocks(t):
        tp = jnp.pad(t, ((0, 0), (Q, Lp - L + Q), (0, 0), (0, 0), (0, 0))).reshape(B, NB + 2, Q, R, H, hd)
        return jnp.concatenate([tp[:, :-2], tp[:, 1:-1], tp[:, 2:]], axis=2)

    kb, vb = key_blocks(ks), key_blocks(vs)
    rel = np.arange(3 * Q)[None, :] - Q - np.arange(Q)[:, None]
    kpos = np.arange(NB)[:, None] * Q - Q + np.arange(3 * Q)[None, :]
    mask = (np.abs(rel) <= n)[None] & ((kpos >= 0) & (kpos < L))[:, None, :]
    bias = jnp.transpose(rel_bias[t5_bucket(rel * dilation)], (2, 0, 1)).astype(jnp.float32)
    s = jnp.einsum('bnqrhd,bnkrhd->bnrhqk', qb, kb, preferred_element_type=jnp.float32) * (hd ** -0.5) + bias
    s = jnp.where(mask[None, :, None, None], s, NEG)
    m = jnp.max(s, axis=-1, keepdims=True)
    p = jnp.exp(s - m)
    den = jnp.sum(p, axis=-1, keepdims=True)
    o = jnp.einsum('bnrhqk,bnkrhd->bnqrhd', p, vb.astype(jnp.float32))
    den_t = jnp.transpose(den[..., 0], (0, 1, 4, 2, 3))
    lse = jnp.transpose((m + jnp.log(den))[..., 0], (0, 1, 4, 2, 3))
    o = o / den_t[..., None]
    o = o.reshape(B, Lp, R, H, hd)[:, :L].reshape(B, S, H, hd)
    lse = lse.reshape(B, Lp, R, H)[:, :L].reshape(B, S, H)
    return o, lse


def dilated_attention(q, k, v, qn_g, kn_g, rel_bias):
    q = rmsnorm(q, qn_g)
    k = rmsnorm(k, kn_g)
    outs, lses = [], []
    for window, dilation in DILATED_BRANCHES:
        o, lse = dilated_window_branch(q, k, v, rel_bias, window, dilation)
        outs.append(o)
        lses.append(lse)
    wts = jax.nn.softmax(jnp.stack(lses), axis=0)
    o = jnp.sum(wts[..., None] * jnp.stack(outs), axis=0)
    return o.astype(v.dtype)


def gated_delta_chunked(q, k, v, g, beta):
    B, H, S, Dk = q.shape
    Dv = v.shape[-1]
    C = CHUNK
    N = S // C
    q = q * (Dk ** -0.5)
    kb = k * beta[..., None]
    vb = v * beta[..., None]
    resh = lambda t: t.reshape((B, H, N, C) + t.shape[3:])
    q, k, kb, vb, g = resh(q), resh(k), resh(kb), resh(vb), resh(g)
    gc = jnp.cumsum(g, axis=-1)
    lower = np.tril(np.ones((C, C), dtype=bool))
    strict = np.tril(np.ones((C, C), dtype=bool), -1)
    decay = jnp.exp(jnp.where(lower, gc[..., :, None] - gc[..., None, :], -jnp.inf))
    lmat = jnp.where(strict, jnp.einsum('bhnid,bhnjd->bhnij', kb, k) * decay, 0.0)
    rhs = jnp.concatenate([vb, kb * jnp.exp(gc)[..., None]], axis=-1)
    sol = lax.linalg.triangular_solve(lmat, rhs, left_side=True, lower=True, unit_diagonal=True)
    u, w = sol[..., :Dv], sol[..., Dv:]
    attn = jnp.where(lower, jnp.einsum('bhnid,bhnjd->bhnij', q, k) * decay, 0.0)

    def step(state, inp):
        q_i, k_i, u_i, w_i, a_i, gc_i = inp
        v_new = u_i - jnp.einsum('bhcd,bhde->bhce', w_i, state)
        o = (jnp.einsum('bhcd,bhde->bhce', q_i * jnp.exp(gc_i)[..., None], state)
             + jnp.einsum('bhcj,bhje->bhce', a_i, v_new))
        g_last = gc_i[..., -1]
        state = (state * jnp.exp(g_last)[..., None, None]
                 + jnp.einsum('bhcd,bhce->bhde', k_i * jnp.exp(g_last[..., None] - gc_i)[..., None], v_new))
        return state, o

    mv = lambda t: jnp.moveaxis(t, 2, 0)
    state0 = jnp.zeros((B, H, Dk, Dv), jnp.float32)
    _, o = lax.scan(step, state0, (mv(q), mv(k), mv(u), mv(w), mv(attn), mv(gc)))
    return jnp.moveaxis(o, 0, 2).reshape(B, H, S, Dv)


def gated_deltanet(q, k, v, z, a, b, conv_w, A_log, dt_bias, onorm_g):
    B, S, _ = q.shape
    qkv = jnp.concatenate([q, k, v], axis=-1)
    qkv = lax.conv_general_dilated(qkv, conv_w[:, None, :], (1,), [(CONV_W // 2, CONV_W // 2)],
                                   dimension_numbers=('NWC', 'WIO', 'NWC'), feature_group_count=3 * D_B)
    qkv = jax.nn.silu(qkv).astype(jnp.float32)
    q, k, v = jnp.split(qkv, 3, axis=-1)
    heads = lambda t: jnp.transpose(t.reshape(B, S, H_B, HD_B), (0, 2, 1, 3))
    q, k, v = l2norm(heads(q)), l2norm(heads(k)), heads(v)
    g = -jnp.exp(A_log.astype(jnp.float32)) * jax.nn.softplus(a.astype(jnp.float32) + dt_bias.astype(jnp.float32))
    beta = jax.nn.sigmoid(b.astype(jnp.float32))
    g = jnp.transpose(g, (2, 0, 3, 1))
    beta = jnp.transpose(beta, (2, 0, 3, 1))
    flip = lambda t: jnp.flip(t, axis=2)
    o_f = gated_delta_chunked(q, k, v, g[0], beta[0])
    o_b = flip(gated_delta_chunked(flip(q), flip(k), flip(v), flip(g[1]), flip(beta[1])))
    o = jnp.transpose(o_f + o_b, (0, 2, 1, 3))
    o = rmsnorm(o, onorm_g) * jax.nn.silu(z.astype(jnp.float32).reshape(B, S, H_B, HD_B))
    return o.reshape(B, S, D_B).astype(z.dtype)


def hierarchical_moe(h, w_rg, b_rg, w_re, b_re, w1, w3, w2):
    T, D = h.shape
    lg = jnp.dot(h, w_rg, preferred_element_type=jnp.float32) + b_rg.astype(jnp.float32)
    p_grp = jax.nn.softmax(lg, axis=-1)
    grp = jnp.argmax(lg, axis=-1)
    p_sel = jnp.take_along_axis(p_grp, grp[:, None], axis=-1)
    le = (jnp.dot(h, w_re, preferred_element_type=jnp.float32) + b_re.astype(jnp.float32))
    le = jnp.take_along_axis(le.reshape(T, N_GROUPS, EXPERTS_PER_GROUP), grp[:, None, None], axis=1)[:, 0]
    top_v, top_i = lax.top_k(le, TOP_K)
    gate = p_sel * jax.nn.softmax(top_v, axis=-1)
    eidx = grp[:, None] * EXPERTS_PER_GROUP + top_i
    A = T * TOP_K
    flat_e = eidx.reshape(-1)
    flat_t = jnp.repeat(jnp.arange(T, dtype=jnp.int32), TOP_K)
    flat_w = gate.reshape(-1)
    order = jnp.argsort(flat_e)
    se = flat_e[order]
    counts = jnp.bincount(flat_e, length=N_EXPERTS)
    padded = (counts + MOE_BLOCK - 1) // MOE_BLOCK * MOE_BLOCK
    cum_pad = jnp.cumsum(padded)
    start_pad = cum_pad - padded
    start = jnp.cumsum(counts) - counts
    dest = start_pad[se] + jnp.arange(A) - start[se]
    P = A + N_EXPERTS * MOE_BLOCK
    NBK = P // MOE_BLOCK
    row_tok = jnp.full((P,), T, jnp.int32).at[dest].set(flat_t[order])
    row_w = jnp.zeros((P,), jnp.float32).at[dest].set(flat_w[order])
    blk_e = jnp.minimum(jnp.searchsorted(cum_pad, jnp.arange(NBK) * MOE_BLOCK, side='right'), N_EXPERTS - 1)
    h_pad = jnp.concatenate([h, jnp.zeros((1, D), h.dtype)], axis=0)
    xb = h_pad[row_tok].reshape(NBK, MOE_BLOCK, D)

    def expert_block(args):
        xe, e = args
        hid = jax.nn.silu(xe @ w1[e]) * (xe @ w3[e])
        return hid @ w2[e]

    yb = lax.map(expert_block, (xb, blk_e)).reshape(P, D)
    y = jnp.zeros((T + 1, D), jnp.float32).at[row_tok].add(yb.astype(jnp.float32) * row_w[:, None])[:T]
    return y.astype(h.dtype)


def setup_inputs(seed: int = 0) -> dict:
    key = jax.random.key(seed)
    ks = jax.random.split(key, 24)
    D = D_MODEL
    f32 = jnp.float32
    nrm = lambda k, shape, scale: jax.random.normal(k, shape, f32) * scale
    dt = jnp.exp(jax.random.uniform(ks[12], (DEPTH, 2, H_B), f32, float(np.log(1e-3)), float(np.log(1e-1))))
    return {
        'x': nrm(ks[0], (BATCH, SEQ, D), 1.0),
        'c': nrm(ks[1], (BATCH, D), 1.0),
        'w_ada': nrm(ks[2], (DEPTH, D, 6 * D), 0.5 * D ** -0.5),
        'b_ada': nrm(ks[3], (DEPTH, 6 * D), 0.01),
        'norm1_g': 1.0 + nrm(ks[4], (DEPTH, D), 0.05),
        'norm2_g': 1.0 + nrm(ks[5], (DEPTH, D), 0.05),
        'w_in': nrm(ks[6], (DEPTH, D, IN_COLS), D ** -0.5),
        'qn_g': 1.0 + nrm(ks[7], (DEPTH, HD_A), 0.05),
        'kn_g': 1.0 + nrm(ks[8], (DEPTH, HD_A), 0.05),
        'rel_bias': nrm(ks[9], (N_BUCKETS, H_A), 0.5),
        'conv_w': nrm(ks[10], (DEPTH, CONV_W, 3 * D_B), CONV_W ** -0.5),
        'A_log': jnp.log(jax.random.uniform(ks[11], (DEPTH, 2, H_B), f32, 1.0, 16.0)),
        'dt_bias': dt + jnp.log(-jnp.expm1(-dt)),
        'onorm_g': 1.0 + nrm(ks[13], (DEPTH, HD_B), 0.05),
        'w_out': nrm(ks[14], (DEPTH, D, D), D ** -0.5),
        'w_rg': nrm(ks[15], (DEPTH, D, N_GROUPS), D ** -0.5),
        'b_rg': nrm(ks[16], (DEPTH, N_GROUPS), 0.01),
        'w_re': nrm(ks[17], (DEPTH, D, N_EXPERTS), D ** -0.5),
        'b_re': nrm(ks[18], (DEPTH, N_EXPERTS), 0.01),
        'w1': nrm(ks[19], (DEPTH, N_EXPERTS, D, D_EXPERT), D ** -0.5),
        'w3': nrm(ks[20], (DEPTH, N_EXPERTS, D, D_EXPERT), D ** -0.5),
        'w2': nrm(ks[21], (DEPTH, N_EXPERTS, D_EXPERT, D), D_EXPERT ** -0.5),
    }


def reference(x, c, w_ada, b_ada, norm1_g, norm2_g, w_in, qn_g, kn_g, rel_bias, conv_w, A_log, dt_bias,
              onorm_g, w_out, w_rg, b_rg, w_re, b_re, w1, w3, w2):
    B, S, D = x.shape
    splits = tuple(int(s) for s in np.cumsum([D_A, D_A, D_A, D_B, D_B, D_B, D_B, 2 * H_B]))
    for l in range(DEPTH):
        mod = jnp.dot(jax.nn.silu(c), w_ada[l]) + b_ada[l]
        sh1, sc1, gt1, sh2, sc2, gt2 = [m[:, None, :] for m in jnp.split(mod, 6, axis=-1)]
        h = rmsnorm(x, norm1_g[l]) * (1.0 + sc1) + sh1
        proj = h @ w_in[l]
        qa, ka, va, qb, kb, vb, zb, ab, bb = jnp.split(proj, splits, axis=-1)
        heads_a = lambda t: t.reshape(B, S, H_A, HD_A)
        oa = dilated_attention(heads_a(qa), heads_a(ka), heads_a(va), qn_g[l], kn_g[l], rel_bias).reshape(B, S, D_A)
        ob = gated_deltanet(qb, kb, vb, zb, ab.reshape(B, S, 2, H_B), bb.reshape(B, S, 2, H_B),
                            conv_w[l], A_log[l], dt_bias[l], onorm_g[l])
        y = jnp.concatenate([oa, ob], axis=-1) @ w_out[l]
        x = x + gt1 * y
        h = rmsnorm(x, norm2_g[l]) * (1.0 + sc2) + sh2
        y = hierarchical_moe(h.reshape(B * S, D), w_rg[l], b_rg[l], w_re[l], b_re[l],
                             w1[l], w3[l], w2[l]).reshape(B, S, D)
        x = x + gt2 * y
    return x
```

```python
import functools

import numpy as np
import jax
import jax.numpy as jnp
from jax import lax
from jax.experimental import pallas as pl
from jax.experimental.pallas import tpu as pltpu

F32 = jnp.float32
BF16 = jnp.bfloat16
I32 = jnp.int32
HIGHEST = lax.Precision.HIGHEST

EPS = 1e-6
NEG = -1e30
DILATED_BRANCHES = ((128, 1), (512, 4), (2048, 16))
REL_MAX_DIST = 1024
CHUNK = 64
TOP_K = 2

LANES = 128
MIB = 1 << 20

ATT_QB = 128
ATT_KW = 256
MOE_RB = 256
MOE_RMAX = 1024
MOE_CW = 256


def _cparams(sem, vmem_mib):
    return pltpu.CompilerParams(dimension_semantics=sem, vmem_limit_bytes=vmem_mib * MIB)


def _dot(a, b, **kw):
    return jnp.dot(a, b, preferred_element_type=F32, **kw)


def _dot_nt(a, b):
    return lax.dot_general(a, b, (((1,), (1,)), ((), ())), preferred_element_type=F32)


def _pick_tile(n, prefs):
    for t in prefs:
        if n % t == 0:
            return t
    return n


def _ada_kernel(c_ref, w_ref, b_ref, o_ref):
    c = c_ref[...]
    s = (c * jax.nn.sigmoid(c)).astype(BF16)
    o_ref[...] = _dot(s, w_ref[...].astype(BF16)) + b_ref[...]


def _ada(c, w_ada, b_ada):
    B, D = c.shape
    N = w_ada.shape[1]
    tn = _pick_tile(N, (1024, 512, 256, 128))
    return pl.pallas_call(
        _ada_kernel,
        grid=(N // tn,),
        in_specs=[pl.BlockSpec((B, D), lambda j: (0, 0)),
                  pl.BlockSpec((D, tn), lambda j: (0, j)),
                  pl.BlockSpec((1, tn), lambda j: (0, j))],
        out_specs=pl.BlockSpec((B, tn), lambda j: (0, j)),
        out_shape=jax.ShapeDtypeStruct((B, N), F32),
        compiler_params=_cparams(("arbitrary",), 40),
    )(c, w_ada, b_ada.reshape(1, N))


def _inproj_kernel(x_ref, g_ref, sc_ref, sh_ref, w_ref, wg_ref, o_ref, og_ref, h_scr):
    @pl.when(pl.program_id(1) == 0)
    def _():
        x = x_ref[...]
        y = x * lax.rsqrt(jnp.mean(x * x, axis=-1, keepdims=True) + EPS) * g_ref[...]
        h = y * (1.0 + sc_ref[...]) + sh_ref[...]
        h_scr[...] = h.astype(BF16)
        og_ref[...] = _dot(h, wg_ref[...], precision=HIGHEST)

    o_ref[...] = _dot(h_scr[...], w_ref[...]).astype(o_ref.dtype)


def _inproj(x2, g, sc, sh, w_main, w_gate, S):
    T, D = x2.shape
    NM = w_main.shape[1]
    tm = _pick_tile(S, (512, 256, 128))
    tn = _pick_tile(NM, (1024, 512, 256, 128))
    per_b = S // tm
    return pl.pallas_call(
        _inproj_kernel,
        grid=(T // tm, NM // tn),
        in_specs=[pl.BlockSpec((tm, D), lambda i, j: (i, 0)),
                  pl.BlockSpec((1, D), lambda i, j: (0, 0)),
                  pl.BlockSpec((None, 1, D), lambda i, j: (i // per_b, 0, 0)),
                  pl.BlockSpec((None, 1, D), lambda i, j: (i // per_b, 0, 0)),
                  pl.BlockSpec((D, tn), lambda i, j: (0, j)),
                  pl.BlockSpec((D, LANES), lambda i, j: (0, 0))],
        out_specs=[pl.BlockSpec((tm, tn), lambda i, j: (i, j)),
                   pl.BlockSpec((tm, LANES), lambda i, j: (i, 0))],
        out_shape=[jax.ShapeDtypeStruct((T, NM), BF16), jax.ShapeDtypeStruct((T, LANES), F32)],
        scratch_shapes=[pltpu.VMEM((tm, D), BF16)],
        compiler_params=_cparams(("arbitrary", "arbitrary"), 48),
    )(x2, g, sc, sh, w_main, w_gate)


def _gating_kernel(gat_ref, a_ref, dt_ref, o_ref, *, hb):
    gat = gat_ref[...]
    tm = gat.shape[0]
    gw = 4 * hb
    g = -jnp.exp(a_ref[...]) * jax.nn.softplus(gat + dt_ref[...])
    beta = jax.nn.sigmoid(gat)
    r = lax.broadcasted_iota(I32, (tm, tm), 0)
    c = lax.broadcasted_iota(I32, (tm, tm), 1)
    same = (r // CHUNK) == (c // CHUNK)
    pre = _dot(jnp.where(same & (c <= r), 1.0, 0.0), g, precision=HIGHEST)
    suf = _dot(jnp.where(same & (c >= r), 1.0, 0.0), g, precision=HIGHEST)
    tot = _dot(jnp.where(same, 1.0, 0.0), g, precision=HIGHEST)
    lane = lax.broadcasted_iota(I32, gat.shape, 1)
    grp = lane // gw
    sub = lane % gw
    gc = jnp.where(sub < hb, pre, suf)
    o_ref[...] = jnp.where(grp == 0, gc, jnp.where(grp == 1, tot - gc, jnp.where(grp == 2, tot, beta)))


def _gating(gat, a_row, dt_row, hb):
    T = gat.shape[0]
    tm = 256
    return pl.pallas_call(
        functools.partial(_gating_kernel, hb=hb),
        grid=(T // tm,),
        in_specs=[pl.BlockSpec((tm, LANES), lambda i: (i, 0)),
                  pl.BlockSpec((1, LANES), lambda i: (0, 0)),
                  pl.BlockSpec((1, LANES), lambda i: (0, 0))],
        out_specs=pl.BlockSpec((tm, LANES), lambda i: (i, 0)),
        out_shape=jax.ShapeDtypeStruct((T, LANES), F32),
        compiler_params=_cparams(("arbitrary",), 16),
    )(gat, a_row, dt_row)


def _t5_bucket(rel, n_buckets):
    half = n_buckets // 2
    max_exact = half // 2
    n = np.abs(rel)
    large = max_exact + (np.log(np.maximum(n, 1) / max_exact) / np.log(REL_MAX_DIST / max_exact)
                         * (half - max_exact)).astype(np.int32)
    large = np.minimum(large, half - 1)
    return (np.where(rel > 0, half, 0) + np.where(n < max_exact, n, large)).astype(np.int32)


def _attn_plan(S):
    plan, base = [], 0
    for window, dil in DILATED_BRANCHES:
        n = window // (2 * dil)
        L = S // dil
        assert L % ATT_QB == 0 and n * 2 == ATT_QB
        kw = min(ATT_KW, L)
        nbq = L // ATT_QB
        nvar = 1 if nbq == 1 else 3
        plan.append((dil, L, nbq, kw, base, nvar, n))
        base += nvar
    return tuple(plan), base


def _attn_bias_table(rel_bias, S):
    plan, nvar_total = _attn_plan(S)
    tabs = []
    for dil, L, nbq, kw, base, nvar, n in plan:
        offs = [0] if nvar == 1 else [0, -n, -(kw - ATT_QB)]
        for off in offs:
            rel = off + np.arange(ATT_KW)[None, :] - np.arange(ATT_QB)[:, None]
            band = (np.abs(rel) <= n) & (np.arange(ATT_KW)[None, :] < kw)
            bias = jnp.transpose(rel_bias[_t5_bucket(rel * dil, rel_bias.shape[0])], (2, 0, 1)).astype(F32)
            tabs.append(jnp.where(band[None], bias, NEG))
    return jnp.stack(tabs)


def _attn_kernel(q_ref, k_ref, v_ref, qg_ref, kg_ref, bias_ref, o_ref,
                 qn_scr, kn_scr, v_scr, ob_scr, mb_scr, db_scr, *, plan, hd):
    S = q_ref.shape[0]
    lane = lax.broadcasted_iota(I32, (1, LANES), 1)
    left = lane < hd

    def headnorm(x, g):
        x2 = x * x
        s_all = jnp.sum(x2, axis=-1, keepdims=True)
        s_left = jnp.sum(jnp.where(left, x2, 0.0), axis=-1, keepdims=True)
        ms = jnp.where(left, s_left, s_all - s_left) * (1.0 / hd)
        return x * lax.rsqrt(ms + EPS) * g

    qn_scr[...] = headnorm(q_ref[...].astype(F32), qg_ref[...]) * (hd ** -0.5)
    kn_scr[...] = headnorm(k_ref[...].astype(F32), kg_ref[...])
    v_scr[...] = v_ref[...].astype(F32)

    for bi, (dil, L, nbq, kw, base, nvar, n) in enumerate(plan):
        ones = jnp.ones((kw, LANES), BF16)

        def body(idx, carry, dil=dil, L=L, nbq=nbq, kw=kw, base=base, nvar=nvar, n=n, bi=bi, ones=ones):
            r = idx // nbq
            i = idx % nbq
            q0 = i * ATT_QB
            k0 = jnp.clip(q0 - n, 0, L - kw)
            var = base if nvar == 1 else base + jnp.where(i > 0, 1, 0) + jnp.where(i == nbq - 1, 1, 0)
            if dil == 1:
                qrows = pl.ds(pl.multiple_of(q0, ATT_QB), ATT_QB)
                krows = pl.ds(pl.multiple_of(k0, CHUNK), kw)
            else:
                qrows = pl.ds(r + q0 * dil, ATT_QB, stride=dil)
                krows = pl.ds(r + k0 * dil, kw, stride=dil)
            qb = qn_scr[qrows, :]
            kwin = kn_scr[krows, :].astype(BF16)
            vwin = v_scr[krows, :].astype(BF16)
            outs = []
            for hh in range(2):
                mask = left if hh == 0 else jnp.logical_not(left)
                qh = jnp.where(mask, qb, 0.0).astype(BF16)
                s = _dot_nt(qh, kwin) + bias_ref[var, hh][:, :kw]
                m = jnp.max(s, axis=-1, keepdims=True)
                p = jnp.exp(s - m).astype(BF16)
                outs.append((_dot(p, vwin), jnp.broadcast_to(m, (ATT_QB, LANES)), _dot(p, ones)))
            ob_scr[bi, qrows, :] = jnp.where(left, outs[0][0], outs[1][0])
            mb_scr[bi, qrows, :] = jnp.where(left, outs[0][1], outs[1][1])
            db_scr[bi, qrows, :] = jnp.where(left, outs[0][2], outs[1][2])
            return carry

        lax.fori_loop(0, dil * nbq, body, 0)

    nb = len(plan)
    mx = mb_scr[0]
    for bi in range(1, nb):
        mx = jnp.maximum(mx, mb_scr[bi])
    num = jnp.zeros((S, LANES), F32)
    den = jnp.zeros((S, LANES), F32)
    for bi in range(nb):
        w = jnp.exp(mb_scr[bi] - mx)
        num = num + w * ob_scr[bi]
        den = den + w * db_scr[bi]
    o_ref[...] = (num / den).astype(o_ref.dtype)


def _attention(proj, qg2, kg2, bias_tab, B, S, HA, hd):
    T = proj.shape[0]
    pairs = HA // 2
    da_blocks = HA * hd // LANES
    plan, nvar = _attn_plan(S)
    return pl.pallas_call(
        functools.partial(_attn_kernel, plan=plan, hd=hd),
        grid=(pairs, B),
        in_specs=[pl.BlockSpec((S, LANES), lambda p, b: (b, p)),
                  pl.BlockSpec((S, LANES), lambda p, b: (b, da_blocks + p)),
                  pl.BlockSpec((S, LANES), lambda p, b: (b, 2 * da_blocks + p)),
                  pl.BlockSpec((1, LANES), lambda p, b: (0, 0)),
                  pl.BlockSpec((1, LANES), lambda p, b: (0, 0)),
                  pl.BlockSpec((nvar, 2, ATT_QB, ATT_KW), lambda p, b: (0, p, 0, 0))],
        out_specs=pl.BlockSpec((S, LANES), lambda p, b: (b, p)),
        out_shape=jax.ShapeDtypeStruct((T, HA * hd), BF16),
        scratch_shapes=[pltpu.VMEM((S, LANES), F32)] * 3 + [pltpu.VMEM((len(plan), S, LANES), F32)] * 3,
        compiler_params=_cparams(("arbitrary", "arbitrary"), 40),
    )(proj, proj, proj, qg2, kg2, bias_tab)


def _gdn_kernel(q_ref, k_ref, v_ref, z_ref, cw_ref, pack_ref, og_ref, o_ref,
                q_scr, k_scr, v_scr, bc_scr, u_scr, wq_scr, at_scr, kdt_scr, et_scr, of_scr, obk_scr, *, hb):
    S = q_ref.shape[0]
    P2 = 2 * CHUNK
    npair = S // P2
    h = pl.program_id(1)
    gw = 4 * hb

    row = lax.broadcasted_iota(I32, (S, 1), 0)

    def conv_silu(x, which):
        acc = x * cw_ref[which, 2:3, :]
        for d in (-2, -1, 1, 2):
            xs = pltpu.roll(x, (-d) % S, 0)
            ok = (row + d >= 0) & (row + d < S)
            acc = acc + jnp.where(ok, xs, 0.0) * cw_ref[which, 2 + d:3 + d, :]
        return acc * jax.nn.sigmoid(acc)

    def l2n(x):
        return x * lax.rsqrt(jnp.sum(x * x, axis=-1, keepdims=True) + EPS)

    dk = q_ref.shape[1]
    q_scr[...] = l2n(conv_silu(q_ref[...].astype(F32), 0)) * (dk ** -0.5)
    k_scr[...] = l2n(conv_silu(k_ref[...].astype(F32), 1))
    v_scr[...] = conv_silu(v_ref[...].astype(F32), 2)

    pk = pack_ref[...]
    p1 = pk.astype(BF16)
    r1 = pk - p1.astype(F32)
    p2 = r1.astype(BF16)
    p3 = (r1 - p2.astype(F32)).astype(BF16)
    nsel = 8
    srow = lax.broadcasted_iota(I32, (LANES, nsel * LANES), 0)
    scol = lax.broadcasted_iota(I32, (LANES, nsel * LANES), 1) // LANES
    src = h + jnp.where(scol < 6, (scol // 2) * gw + (scol % 2) * hb, 3 * gw + 2 * hb + (scol - 6) * hb)
    sel = jnp.where(srow == src, 1.0, 0.0).astype(BF16)
    bc_scr[...] = _dot(p1, sel) + _dot(p2, sel) + _dot(p3, sel)

    ri = lax.broadcasted_iota(I32, (P2, P2), 0)
    ci = lax.broadcasted_iota(I32, (P2, P2), 1)
    same = (ri // CHUNK) == (ci // CHUNK)
    bd16 = (ri // 16) == (ci // 16)
    incl = (same & (ci <= ri), same & (ci >= ri))
    strict = (same & (ci < ri), same & (ci > ri))

    def mm(a, b):
        return _dot(a.astype(BF16), b.astype(BF16))

    def prep(m, carry):
        rows = pl.ds(pl.multiple_of(m * P2, P2), P2)
        kp = k_scr[rows, :]
        qp = q_scr[rows, :]
        vp = v_scr[rows, :]
        kp_b = kp.astype(BF16)
        for d in range(2):
            gc = bc_scr[rows, d * LANES:(d + 1) * LANES]
            rest = bc_scr[rows, (2 + d) * LANES:(3 + d) * LANES]
            tot = bc_scr[rows, (4 + d) * LANES:(5 + d) * LANES]
            beta = bc_scr[rows, (6 + d) * LANES:(7 + d) * LANES]
            egc = jnp.exp(gc)
            kb = kp * beta
            vb = vp * beta
            kbe = kb * egc
            qe = qp * egc
            kd = kp * jnp.exp(rest)
            dec = jnp.exp(jnp.where(incl[d], gc - gc.T, -jnp.inf))
            kk = _dot_nt(kb.astype(BF16), kp_b)
            qk = _dot_nt(qp.astype(BF16), kp_b)
            lm = jnp.where(strict[d], kk * dec, 0.0)
            attn = qk * dec
            lbd = jnp.where(bd16, lm, 0.0)
            loff = lm - lbd
            nn = -lbd
            pw = mm(lbd, lbd)
            nn = nn + pw + mm(nn, pw)
            pw = mm(pw, pw)
            nn = nn + pw + mm(nn, pw)
            pw = mm(pw, pw)
            nn = nn + pw + mm(nn, pw)
            mo = loff + mm(nn, loff)
            m2 = mm(mo, mo)
            xo = nn - mo - mm(mo, nn)
            toff = xo + m2 + mm(m2, xo)
            rhs = jnp.concatenate([vb, kbe], axis=1)
            uw = rhs + mm(toff, rhs)
            u_scr[d, rows, :] = uw[:, :LANES]
            wb = uw[:, LANES:].astype(BF16)
            qeb = qe.astype(BF16)
            wq_scr[d, pl.ds(pl.multiple_of(m * 2 * P2, 2 * P2), 2 * P2), :] = jnp.concatenate(
                [wb[:CHUNK], qeb[:CHUNK], wb[CHUNK:], qeb[CHUNK:]], axis=0)
            at_scr[d, rows, :] = attn.astype(BF16)
            kdt_scr[d, :, rows] = kd.T.astype(BF16)
            et_scr[d, rows, :] = jnp.exp(tot)
        return carry

    lax.fori_loop(0, npair, prep, 0)

    zero = jnp.zeros((CHUNK, LANES), F32)

    def scan(m, states):
        new_states = []
        for d in range(2):
            st = states[d]
            p = m if d == 0 else npair - 1 - m
            rows = pl.ds(pl.multiple_of(p * P2, P2), P2)
            u_pair = u_scr[d, rows, :]
            at_pair = at_scr[d, rows, :]
            kdt_pair = kdt_scr[d, :, rows]
            wq_pair = wq_scr[d, pl.ds(pl.multiple_of(p * 2 * P2, 2 * P2), 2 * P2), :]
            et_pair = et_scr[d, rows, :]
            outs = [None, None]
            for cpos in ((0, 1) if d == 0 else (1, 0)):
                lo = cpos * CHUNK
                rr = _dot(wq_pair[2 * lo:2 * lo + P2], st.astype(BF16))
                v_new = u_pair[lo:lo + CHUNK] - rr[:CHUNK]
                vn = jnp.concatenate([v_new, zero] if cpos == 0 else [zero, v_new], axis=0).astype(BF16)
                outs[cpos] = rr[CHUNK:] + _dot(at_pair[lo:lo + CHUNK], vn)
                st = st * et_pair[lo:lo + 1] + _dot(kdt_pair, vn)
            o_pair = jnp.concatenate(outs, axis=0)
            if d == 0:
                of_scr[rows, :] = o_pair
            else:
                obk_scr[rows, :] = o_pair
            new_states.append(st)
        return tuple(new_states)

    s0 = jnp.zeros((dk, LANES), F32)
    lax.fori_loop(0, npair, scan, (s0, s0))

    o = of_scr[...] + obk_scr[...]
    y = o * lax.rsqrt(jnp.mean(o * o, axis=-1, keepdims=True) + EPS) * og_ref[...]
    z = z_ref[...].astype(F32)
    o_ref[...] = (y * (z * jax.nn.sigmoid(z))).astype(o_ref.dtype)


def _gdn(proj, cw4, pack3, onorm_g, B, S, HB, base_blk):
    T = proj.shape[0]
    hdb = LANES
    return pl.pallas_call(
        functools.partial(_gdn_kernel, hb=HB),
        grid=(B, HB),
        in_specs=[pl.BlockSpec((S, LANES), lambda b, h: (b, base_blk + h)),
                  pl.BlockSpec((S, LANES), lambda b, h: (b, base_blk + HB + h)),
                  pl.BlockSpec((S, LANES), lambda b, h: (b, base_blk + 2 * HB + h)),
                  pl.BlockSpec((S, LANES), lambda b, h: (b, base_blk + 3 * HB + h)),
                  pl.BlockSpec((None, 3, cw4.shape[2], LANES), lambda b, h: (h, 0, 0, 0)),
                  pl.BlockSpec((None, S, LANES), lambda b, h: (b, 0, 0)),
                  pl.BlockSpec((1, LANES), lambda b, h: (0, 0))],
        out_specs=pl.BlockSpec((S, LANES), lambda b, h: (b, h)),
        out_shape=jax.ShapeDtypeStruct((T, HB * hdb), BF16),
        scratch_shapes=[pltpu.VMEM((S, LANES), F32)] * 3
        + [pltpu.VMEM((S, 8 * LANES), F32),
           pltpu.VMEM((2, S, LANES), F32),
           pltpu.VMEM((2, 2 * S, LANES), BF16),
           pltpu.VMEM((2, S, LANES), BF16),
           pltpu.VMEM((2, LANES, S), BF16),
           pltpu.VMEM((2, S, LANES), F32),
           pltpu.VMEM((S, LANES), F32),
           pltpu.VMEM((S, LANES), F32)],
        compiler_params=_cparams(("arbitrary", "arbitrary"), 48),
    )(proj, proj, proj, proj, cw4, pack3, onorm_g)


def _outproj_kernel(oa_ref, ob_ref, x_ref, gt_ref, w_ref, g_ref, sc_ref, sh_ref, wr_ref, br_ref,
                    x1_ref, h_ref, lg_ref):
    da = oa_ref.shape[1]
    y = _dot(oa_ref[...], w_ref[:da, :]) + _dot(ob_ref[...], w_ref[da:, :])
    x1 = x_ref[...] + gt_ref[...] * y
    x1_ref[...] = x1
    hn = x1 * lax.rsqrt(jnp.mean(x1 * x1, axis=-1, keepdims=True) + EPS) * g_ref[...]
    h = hn * (1.0 + sc_ref[...]) + sh_ref[...]
    h_ref[...] = h
    lg_ref[...] = _dot(h, wr_ref[...], precision=HIGHEST) + br_ref[...]


def _outproj(oa, ob, x2, gt1, w_out_b, g2, sc2, sh2, wr, br, S):
    T, D = x2.shape
    tm = 256
    per_b = S // tm
    bmap = lambda i: (i // per_b, 0, 0)
    return pl.pallas_call(
        _outproj_kernel,
        grid=(T // tm,),
        in_specs=[pl.BlockSpec((tm, oa.shape[1]), lambda i: (i, 0)),
                  pl.BlockSpec((tm, ob.shape[1]), lambda i: (i, 0)),
                  pl.BlockSpec((tm, D), lambda i: (i, 0)),
                  pl.BlockSpec((None, 1, D), bmap),
                  pl.BlockSpec((D, D), lambda i: (0, 0)),
                  pl.BlockSpec((1, D), lambda i: (0, 0)),
                  pl.BlockSpec((None, 1, D), bmap),
                  pl.BlockSpec((None, 1, D), bmap),
                  pl.BlockSpec((D, LANES), lambda i: (0, 0)),
                  pl.BlockSpec((1, LANES), lambda i: (0, 0))],
        out_specs=[pl.BlockSpec((tm, D), lambda i: (i, 0)),
                   pl.BlockSpec((tm, D), lambda i: (i, 0)),
                   pl.BlockSpec((tm, LANES), lambda i: (i, 0))],
        out_shape=[jax.ShapeDtypeStruct((T, D), F32), jax.ShapeDtypeStruct((T, D), F32),
                   jax.ShapeDtypeStruct((T, LANES), F32)],
        compiler_params=_cparams(("arbitrary",), 48),
    )(oa, ob, x2, gt1, w_out_b, g2, sc2, sh2, wr, br)


def _route_kernel(lg_ref, o_ref, cnt_ref, run_scr, *, ne, ng):
    @pl.when(pl.program_id(0) == 0)
    def _():
        run_scr[...] = jnp.zeros_like(run_scr)

    lg = lg_ref[...]
    tm = lg.shape[0]
    epg = ne // ng
    lane_i = lax.broadcasted_iota(I32, lg.shape, 1)
    lane = lane_i.astype(F32)
    big = float(2 * LANES)
    is_g = (lane_i >= ne) & (lane_i < ne + ng)
    gl = jnp.where(is_g, lg, -jnp.inf)
    gmax = jnp.max(gl, axis=-1, keepdims=True)
    gidx = jnp.min(jnp.where(gl == gmax, lane, big), axis=-1, keepdims=True) - ne
    psel = 1.0 / jnp.sum(jnp.where(is_g, jnp.exp(gl - gmax), 0.0), axis=-1, keepdims=True)
    in_grp = (lane_i // epg).astype(F32) == gidx
    el = jnp.where(in_grp & (lane_i < ne), lg, -jnp.inf)
    m1 = jnp.max(el, axis=-1, keepdims=True)
    i1 = jnp.min(jnp.where(el == m1, lane, big), axis=-1, keepdims=True)
    el2 = jnp.where(lane == i1, -jnp.inf, el)
    m2 = jnp.max(el2, axis=-1, keepdims=True)
    i2 = jnp.min(jnp.where(el2 == m2, lane, big), axis=-1, keepdims=True)
    e21 = jnp.exp(m2 - m1)
    g1 = psel / (1.0 + e21)
    g2 = psel * e21 / (1.0 + e21)
    o1 = jnp.where(lane == i1, 1.0, 0.0)
    o2 = jnp.where(lane == i2, 1.0, 0.0)
    cnt = o1 + o2
    r = lax.broadcasted_iota(I32, (tm, tm), 0)
    c = lax.broadcasted_iota(I32, (tm, tm), 1)
    before = _dot(jnp.where(c < r, 1.0, 0.0).astype(BF16), cnt.astype(BF16)) + run_scr[0:1, :]
    rank1 = jnp.sum(o1 * before, axis=-1, keepdims=True)
    rank2 = jnp.sum(o2 * before, axis=-1, keepdims=True)
    run_scr[...] = run_scr[...] + jnp.sum(cnt, axis=0, keepdims=True)
    cnt_ref[...] = run_scr[...]
    cols = (i1, i2, g1, g2, rank1, rank2)
    out = jnp.zeros(lg.shape, F32)
    for j, val in enumerate(cols):
        out = jnp.where(lane_i == j, val, out)
    o_ref[...] = out


def _route(logits, ne, ng):
    T = logits.shape[0]
    tm = 256
    return pl.pallas_call(
        functools.partial(_route_kernel, ne=ne, ng=ng),
        grid=(T // tm,),
        in_specs=[pl.BlockSpec((tm, LANES), lambda i: (i, 0))],
        out_specs=[pl.BlockSpec((tm, LANES), lambda i: (i, 0)),
                   pl.BlockSpec((8, LANES), lambda i: (0, 0))],
        out_shape=[jax.ShapeDtypeStruct((T, LANES), F32), jax.ShapeDtypeStruct((8, LANES), F32)],
        scratch_shapes=[pltpu.VMEM((8, LANES), F32)],
        compiler_params=_cparams(("arbitrary",), 16),
    )(logits)


def _moe_kernel(we_ref, ws_ref, wn_ref, code_ref, h_hbm, w1_ref, w3_ref, w2_ref, y_hbm,
                x32, xbf, yacc, w1b, w3b, w2b, sem_in, sem_out, *, n_tok):
    w = pl.program_id(0)
    c = pl.program_id(1)
    nc = pl.num_programs(1)
    nrows = wn_ref[w]
    start = ws_ref[w]
    nblk = nrows // MOE_RB

    def in_copy(tok, i):
        return pltpu.make_async_copy(h_hbm.at[pl.ds(tok, 1)], x32.at[pl.ds(i, 1)], sem_in)

    def out_copy(i, code):
        return pltpu.make_async_copy(yacc.at[pl.ds(i, 1)], y_hbm.at[pl.ds(code, 1)], sem_out)

    @pl.when(c == 0)
    def _gather():
        def issue(i, carry):
            code = code_ref[start + i]
            in_copy(jnp.where(code < TOP_K * n_tok, lax.rem(code, n_tok), 0), i).start()
            return carry

        lax.fori_loop(0, nrows, issue, 0)

        def wait(i, carry):
            in_copy(0, i).wait()
            return carry

        lax.fori_loop(0, nrows, wait, 0)

        def cast(rb, carry):
            rows = pl.ds(pl.multiple_of(rb * MOE_RB, MOE_RB), MOE_RB)
            xbf[rows, :] = x32[rows, :].astype(BF16)
            return carry

        lax.fori_loop(0, nblk, cast, 0)

    @pl.when(nrows > 0)
    def _compute():
        w1b[...] = w1_ref[...].astype(BF16)
        w3b[...] = w3_ref[...].astype(BF16)
        w2b[...] = w2_ref[...].astype(BF16)

        def blk(rb, carry):
            rows = pl.ds(pl.multiple_of(rb * MOE_RB, MOE_RB), MOE_RB)
            xb = xbf[rows, :]
            h1 = _dot(xb, w1b[...])
            h3 = _dot(xb, w3b[...])
            hid = (h1 * jax.nn.sigmoid(h1) * h3).astype(BF16)
            yc = _dot(hid, w2b[...])

            @pl.when(c == 0)
            def _():
                yacc[rows, :] = yc

            @pl.when(c > 0)
            def _():
                yacc[rows, :] = yacc[rows, :] + yc

            return carry

        lax.fori_loop(0, nblk, blk, 0)

    @pl.when(c == nc - 1)
    def _scatter():
        def issue(i, carry):
            code = code_ref[start + i]

            @pl.when(code < TOP_K * n_tok)
            def _():
                out_copy(i, code).start()

            return carry

        lax.fori_loop(0, nrows, issue, 0)

        def wait(i, carry):
            code = code_ref[start + i]

            @pl.when(code < TOP_K * n_tok)
            def _():
                out_copy(i, code).wait()

            return carry

        lax.fori_loop(0, nrows, wait, 0)


def _moe(we, ws, wn, row_code, h2, w1, w3, w2):
    T, D = h2.shape
    NE, _, DE = w1.shape
    cw = min(MOE_CW, DE)
    nc = DE // cw
    nw = we.shape[0]
    return pl.pallas_call(
        functools.partial(_moe_kernel, n_tok=T),
        grid_spec=pltpu.PrefetchScalarGridSpec(
            num_scalar_prefetch=4,
            grid=(nw, nc),
            in_specs=[pl.BlockSpec(memory_space=pl.ANY),
                      pl.BlockSpec((None, D, cw), lambda w, c, we, ws, wn, code: (we[w], 0, c)),
                      pl.BlockSpec((None, D, cw), lambda w, c, we, ws, wn, code: (we[w], 0, c)),
                      pl.BlockSpec((None, cw, D), lambda w, c, we, ws, wn, code: (we[w], c, 0))],
            out_specs=pl.BlockSpec(memory_space=pl.ANY),
            scratch_shapes=[pltpu.VMEM((MOE_RMAX, D), F32),
                            pltpu.VMEM((MOE_RMAX, D), BF16),
                            pltpu.VMEM((MOE_RMAX, D), F32),
                            pltpu.VMEM((D, cw), BF16),
                            pltpu.VMEM((D, cw), BF16),
                            pltpu.VMEM((cw, D), BF16),
                            pltpu.SemaphoreType.DMA(()),
                            pltpu.SemaphoreType.DMA(())]),
        out_shape=jax.ShapeDtypeStruct((TOP_K * T, D), F32),
        compiler_params=_cparams(("arbitrary", "arbitrary"), 56),
    )(we, ws, wn, row_code, h2, w1, w3, w2)


def _moe_schedule(route, counts, T, NE):
    e = route[:, 0:2].astype(I32)
    rank = route[:, 4:6].astype(I32)
    cnt = counts[0, :NE].astype(I32)
    padded = (cnt + MOE_RB - 1) // MOE_RB * MOE_RB
    cum_pad = jnp.cumsum(padded)
    start_pad = cum_pad - padded
    dest = start_pad[e] + rank
    codes = jnp.arange(T, dtype=I32)[:, None] + T * jnp.arange(TOP_K, dtype=I32)[None, :]
    a_rows = TOP_K * T
    p_rows = a_rows + NE * MOE_RB
    row_code = jnp.full((p_rows,), TOP_K * T, I32).at[dest.reshape(-1)].set(codes.reshape(-1))
    items = (padded + MOE_RMAX - 1) // MOE_RMAX
    cum_items = jnp.cumsum(items)
    n_items = cum_items[-1]
    nw = (p_rows + NE * (MOE_RMAX - MOE_RB)) // MOE_RMAX
    wi = jnp.arange(nw, dtype=I32)
    valid = wi < n_items
    wi_c = jnp.minimum(wi, jnp.maximum(n_items - 1, 0))
    we = jnp.minimum(jnp.searchsorted(cum_items, wi_c, side='right'), NE - 1).astype(I32)
    local = wi_c - (cum_items[we] - items[we])
    ws = (start_pad[we] + local * MOE_RMAX).astype(I32)
    wn = jnp.where(valid, jnp.clip(padded[we] - local * MOE_RMAX, 0, MOE_RMAX), 0).astype(I32)
    return we, ws, wn, row_code


def _combine_kernel(x1_ref, ya_ref, yb_ref, rt_ref, gt_ref, o_ref):
    rt = rt_ref[...]
    y = rt[:, 2:3] * ya_ref[...] + rt[:, 3:4] * yb_ref[...]
    o_ref[...] = x1_ref[...] + gt_ref[...] * y


def _combine(x1, y2, route, gt2, S):
    T, D = x1.shape
    tm = 256
    per_b = S // tm
    nt = T // tm
    return pl.pallas_call(
        _combine_kernel,
        grid=(nt,),
        in_specs=[pl.BlockSpec((tm, D), lambda i: (i, 0)),
                  pl.BlockSpec((tm, D), lambda i: (i, 0)),
                  pl.BlockSpec((tm, D), lambda i: (i + nt, 0)),
                  pl.BlockSpec((tm, LANES), lambda i: (i, 0)),
                  pl.BlockSpec((None, 1, D), lambda i: (i // per_b, 0, 0))],
        out_specs=pl.BlockSpec((tm, D), lambda i: (i, 0)),
        out_shape=jax.ShapeDtypeStruct((T, D), F32),
        compiler_params=_cparams(("arbitrary",), 32),
    )(x1, y2, y2, route, gt2)


def kernel(x, c, w_ada, b_ada, norm1_g, norm2_g, w_in, qn_g, kn_g, rel_bias, conv_w, A_log, dt_bias,
           onorm_g, w_out, w_rg, b_rg, w_re, b_re, w1, w3, w2):
    B, S, D = x.shape
    depth = w_ada.shape[0]
    HA, hda = rel_bias.shape[1], qn_g.shape[-1]
    HB, hdb = A_log.shape[-1], onorm_g.shape[-1]
    DA, DB = HA * hda, HB * hdb
    NG, NE = w_rg.shape[-1], w_re.shape[-1]
    T = B * S
    assert hdb == LANES and 2 * hda == LANES and DA + DB == D and 16 * HB <= LANES
    assert NE + NG <= LANES and conv_w.shape[1] == 5
    n_main = 3 * DA + 4 * DB

    bias_tab = _attn_bias_table(rel_bias, S)
    x2 = x.reshape(T, D)
    for l in range(depth):
        mod = _ada(c, w_ada[l], b_ada[l]).reshape(B, 6, 1, D)
        sh1, sc1, gt1, sh2, sc2, gt2 = (mod[:, i] for i in range(6))

        w_main = w_in[l][:, :n_main].astype(BF16)
        w_gate = jnp.tile(w_in[l][:, n_main:], (1, 4))
        w_gate = jnp.pad(w_gate, ((0, 0), (0, LANES - w_gate.shape[1])))
        proj, gat = _inproj(x2, norm1_g[l].reshape(1, D), sc1, sh1, w_main, w_gate, S)

        def gate_row(p):
            grp = jnp.concatenate([p.reshape(-1), jnp.zeros((2 * HB,), F32)])
            return jnp.pad(jnp.tile(grp, 4), (0, LANES - 16 * HB)).reshape(1, LANES)

        pack = _gating(gat, gate_row(A_log[l]), gate_row(dt_bias[l]), HB)

        oa = _attention(proj, jnp.tile(qn_g[l], 2).reshape(1, LANES), jnp.tile(kn_g[l], 2).reshape(1, LANES),
                        bias_tab, B, S, HA, hda)
        cw4 = jnp.transpose(conv_w[l].reshape(conv_w.shape[1], 3, HB, hdb), (2, 1, 0, 3))
        ob = _gdn(proj, cw4, pack.reshape(B, S, LANES), onorm_g[l].reshape(1, LANES), B, S, HB, 3 * DA // LANES)

        wr = jnp.pad(jnp.concatenate([w_re[l], w_rg[l]], axis=1), ((0, 0), (0, LANES - NE - NG)))
        br = jnp.pad(jnp.concatenate([b_re[l], b_rg[l]]), (0, LANES - NE - NG)).reshape(1, LANES)
        x1, h2, logits = _outproj(oa, ob, x2, gt1, w_out[l].astype(BF16), norm2_g[l].reshape(1, D),
                                  sc2, sh2, wr, br, S)
        route, counts = _route(logits, NE, NG)
        we, ws, wn, row_code = _moe_schedule(route, counts, T, NE)
        y2 = _moe(we, ws, wn, row_code, h2, w1[l], w3[l], w2[l])
        x2 = _combine(x1, y2, route, gt2, S)
    return x2.reshape(B, S, D)
```

```python
import functools

import numpy as np
import jax
import jax.numpy as jnp
from jax import lax
from jax.experimental import pallas as pl
from jax.experimental.pallas import tpu as pltpu

F32 = jnp.float32
BF16 = jnp.bfloat16
I32 = jnp.int32
HIGHEST = lax.Precision.HIGHEST

EPS = 1e-6
NEG = -1e30
DILATED_BRANCHES = ((128, 1), (512, 4), (2048, 16))
REL_MAX_DIST = 1024
CHUNK = 64
TOP_K = 2

LANES = 128
MIB = 1 << 20

ATT_QB = 128
ATT_KW = 256
ATT_UNROLL = 4
MOE_RB = 256
MOE_RMAX = 1024
MOE_CW = 256


def _cparams(sem, vmem_mib):
    return pltpu.CompilerParams(dimension_semantics=sem, vmem_limit_bytes=vmem_mib * MIB)


def _dot(a, b, **kw):
    return jnp.dot(a, b, preferred_element_type=F32, **kw)


def _dot_nt(a, b):
    return lax.dot_general(a, b, (((1,), (1,)), ((), ())), preferred_element_type=F32)


def _pick_tile(n, prefs):
    for t in prefs:
        if n % t == 0:
            return t
    return n


def _ada_kernel(c_ref, w_ref, b_ref, o_ref):
    c = c_ref[...]
    s = (c * jax.nn.sigmoid(c)).astype(BF16)
    o_ref[...] = _dot(s, w_ref[...].astype(BF16)) + b_ref[...]


def _ada(c, w_ada, b_ada):
    B, D = c.shape
    N = w_ada.shape[1]
    tn = _pick_tile(N, (1024, 512, 256, 128))
    return pl.pallas_call(
        _ada_kernel,
        grid=(N // tn,),
        in_specs=[pl.BlockSpec((B, D), lambda j: (0, 0)),
                  pl.BlockSpec((D, tn), lambda j: (0, j)),
                  pl.BlockSpec((1, tn), lambda j: (0, j))],
        out_specs=pl.BlockSpec((B, tn), lambda j: (0, j)),
        out_shape=jax.ShapeDtypeStruct((B, N), F32),
        compiler_params=_cparams(("arbitrary",), 40),
    )(c, w_ada, b_ada.reshape(1, N))


def _inproj_kernel(x_ref, g_ref, sc_ref, sh_ref, w_ref, wg_ref, o_ref, og_ref, h_scr):
    @pl.when(pl.program_id(1) == 0)
    def _():
        x = x_ref[...]
        y = x * lax.rsqrt(jnp.mean(x * x, axis=-1, keepdims=True) + EPS) * g_ref[...]
        h = y * (1.0 + sc_ref[...]) + sh_ref[...]
        h_scr[...] = h.astype(BF16)
        og_ref[...] = _dot(h, wg_ref[...], precision=HIGHEST)

    o_ref[...] = _dot(h_scr[...], w_ref[...]).astype(o_ref.dtype)


def _inproj(x2, g, sc, sh, w_main, w_gate, S):
    T, D = x2.shape
    NM = w_main.shape[1]
    tm = _pick_tile(S, (512, 256, 128))
    tn = _pick_tile(NM, (1024, 512, 256, 128))
    per_b = S // tm
    return pl.pallas_call(
        _inproj_kernel,
        grid=(T // tm, NM // tn),
        in_specs=[pl.BlockSpec((tm, D), lambda i, j: (i, 0)),
                  pl.BlockSpec((1, D), lambda i, j: (0, 0)),
                  pl.BlockSpec((None, 1, D), lambda i, j: (i // per_b, 0, 0)),
                  pl.BlockSpec((None, 1, D), lambda i, j: (i // per_b, 0, 0)),
                  pl.BlockSpec((D, tn), lambda i, j: (0, j)),
                  pl.BlockSpec((D, LANES), lambda i, j: (0, 0))],
        out_specs=[pl.BlockSpec((tm, tn), lambda i, j: (i, j)),
                   pl.BlockSpec((tm, LANES), lambda i, j: (i, 0))],
        out_shape=[jax.ShapeDtypeStruct((T, NM), BF16), jax.ShapeDtypeStruct((T, LANES), F32)],
        scratch_shapes=[pltpu.VMEM((tm, D), BF16)],
        compiler_params=_cparams(("arbitrary", "arbitrary"), 48),
    )(x2, g, sc, sh, w_main, w_gate)


def _gating_kernel(gat_ref, a_ref, dt_ref, o_ref, *, hb):
    gat = gat_ref[...]
    tm = gat.shape[0]
    gw = 4 * hb
    g = -jnp.exp(a_ref[...]) * jax.nn.softplus(gat + dt_ref[...])
    beta = jax.nn.sigmoid(gat)
    r = lax.broadcasted_iota(I32, (tm, tm), 0)
    c = lax.broadcasted_iota(I32, (tm, tm), 1)
    same = (r // CHUNK) == (c // CHUNK)
    pre = _dot(jnp.where(same & (c <= r), 1.0, 0.0), g, precision=HIGHEST)
    suf = _dot(jnp.where(same & (c >= r), 1.0, 0.0), g, precision=HIGHEST)
    tot = _dot(jnp.where(same, 1.0, 0.0), g, precision=HIGHEST)
    lane = lax.broadcasted_iota(I32, gat.shape, 1)
    grp = lane // gw
    sub = lane % gw
    gc = jnp.where(sub < hb, pre, suf)
    o_ref[...] = jnp.where(grp == 0, gc, jnp.where(grp == 1, tot - gc, jnp.where(grp == 2, tot, beta)))


def _gating(gat, a_row, dt_row, hb):
    T = gat.shape[0]
    tm = 256
    return pl.pallas_call(
        functools.partial(_gating_kernel, hb=hb),
        grid=(T // tm,),
        in_specs=[pl.BlockSpec((tm, LANES), lambda i: (i, 0)),
                  pl.BlockSpec((1, LANES), lambda i: (0, 0)),
                  pl.BlockSpec((1, LANES), lambda i: (0, 0))],
        out_specs=pl.BlockSpec((tm, LANES), lambda i: (i, 0)),
        out_shape=jax.ShapeDtypeStruct((T, LANES), F32),
        compiler_params=_cparams(("arbitrary",), 16),
    )(gat, a_row, dt_row)


def _t5_bucket(rel, n_buckets):
    half = n_buckets // 2
    max_exact = half // 2
    n = np.abs(rel)
    large = max_exact + (np.log(np.maximum(n, 1) / max_exact) / np.log(REL_MAX_DIST / max_exact)
                         * (half - max_exact)).astype(np.int32)
    large = np.minimum(large, half - 1)
    return (np.where(rel > 0, half, 0) + np.where(n < max_exact, n, large)).astype(np.int32)


def _attn_plan(S):
    plan, base = [], 0
    for window, dil in DILATED_BRANCHES:
        n = window // (2 * dil)
        L = S // dil
        assert L % ATT_QB == 0 and n * 2 == ATT_QB
        kw = min(ATT_KW, L)
        nbq = L // ATT_QB
        nvar = 1 if nbq == 1 else 3
        plan.append((dil, L, nbq, kw, base, nvar, n))
        base += nvar
    return tuple(plan), base


def _attn_bias_table(rel_bias, S):
    plan, nvar_total = _attn_plan(S)
    nbuckets, H = rel_bias.shape
    W = ATT_QB + ATT_KW
    u = np.arange(W) - (ATT_QB - 1)
    onehots, bands = [], []
    for dil, L, nbq, kw, base, nvar, n in plan:
        offs = [0] if nvar == 1 else [0, -n, -(kw - ATT_QB)]
        for off in offs:
            rel = off + u
            onehots.append(np.eye(nbuckets, dtype=np.float32)[_t5_bucket(rel * dil, nbuckets)])
            bands.append(np.abs(rel) <= n)
    onehot = jnp.asarray(np.stack(onehots))
    band = jnp.asarray(np.stack(bands))
    prof = jnp.einsum('vwn,nh->vhw', onehot, rel_bias.astype(F32), precision=HIGHEST)
    prof = jnp.where(band[:, None, :], prof, NEG)
    skew = jnp.tile(prof, (1, 1, ATT_QB))[:, :, :ATT_QB * (W - 1)].reshape(nvar_total, H, ATT_QB, W - 1)
    return skew[:, :, :, ATT_QB - 1:ATT_QB - 1 + ATT_KW]


def _attn_kernel(q_ref, k_ref, v_ref, qg_ref, kg_ref, bias_ref, o_ref,
                 qn_scr, kn_scr, v_scr, ob_scr, mb_scr, db_scr, *, plan, hd):
    S = q_ref.shape[0]
    lane = lax.broadcasted_iota(I32, (1, LANES), 1)
    left = lane < hd

    def headnorm(x, g):
        x2 = x * x
        s_all = jnp.sum(x2, axis=-1, keepdims=True)
        s_left = jnp.sum(jnp.where(left, x2, 0.0), axis=-1, keepdims=True)
        ms = jnp.where(left, s_left, s_all - s_left) * (1.0 / hd)
        return x * lax.rsqrt(ms + EPS) * g

    qn_scr[...] = headnorm(q_ref[...].astype(F32), qg_ref[...]) * (hd ** -0.5)
    kn_scr[...] = headnorm(k_ref[...].astype(F32), kg_ref[...])
    v_scr[...] = v_ref[...].astype(F32)

    for bi, (dil, L, nbq, kw, base, nvar, n) in enumerate(plan):
        ones = jnp.ones((kw, LANES), BF16)

        def body(idx, carry, dil=dil, L=L, nbq=nbq, kw=kw, base=base, nvar=nvar, n=n, bi=bi, ones=ones):
            r = idx // nbq
            i = idx % nbq
            q0 = i * ATT_QB
            k0 = jnp.clip(q0 - n, 0, L - kw)
            var = base if nvar == 1 else base + jnp.where(i > 0, 1, 0) + jnp.where(i == nbq - 1, 1, 0)
            if dil == 1:
                qrows = pl.ds(pl.multiple_of(q0, ATT_QB), ATT_QB)
                krows = pl.ds(pl.multiple_of(k0, CHUNK), kw)
            else:
                qrows = pl.ds(r + q0 * dil, ATT_QB, stride=dil)
                krows = pl.ds(r + k0 * dil, kw, stride=dil)
            qb = qn_scr[qrows, :]
            kwin = kn_scr[krows, :].astype(BF16)
            vwin = jnp.concatenate([v_scr[krows, :].astype(BF16), ones], axis=1)
            q2 = jnp.concatenate([jnp.where(left, qb, 0.0), jnp.where(left, 0.0, qb)], axis=0).astype(BF16)
            bias2 = jnp.concatenate([bias_ref[var, 0][:, :kw], bias_ref[var, 1][:, :kw]], axis=0)
            s = _dot_nt(q2, kwin) + bias2
            m = jnp.max(s, axis=-1, keepdims=True)
            p = jnp.exp(s - m).astype(BF16)
            od = _dot(p, vwin)
            mb = jnp.broadcast_to(m, (2 * ATT_QB, LANES))
            ob_scr[bi, qrows, :] = jnp.where(left, od[:ATT_QB, :LANES], od[ATT_QB:, :LANES])
            mb_scr[bi, qrows, :] = jnp.where(left, mb[:ATT_QB], mb[ATT_QB:])
            db_scr[bi, qrows, :] = jnp.where(left, od[:ATT_QB, LANES:], od[ATT_QB:, LANES:])
            return carry

        lax.fori_loop(0, dil * nbq, body, 0, unroll=ATT_UNROLL)

    nb = len(plan)
    mx = mb_scr[0]
    for bi in range(1, nb):
        mx = jnp.maximum(mx, mb_scr[bi])
    num = jnp.zeros((S, LANES), F32)
    den = jnp.zeros((S, LANES), F32)
    for bi in range(nb):
        w = jnp.exp(mb_scr[bi] - mx)
        num = num + w * ob_scr[bi]
        den = den + w * db_scr[bi]
    o_ref[...] = (num / den).astype(o_ref.dtype)


def _attention(proj, qg2, kg2, bias_tab, B, S, HA, hd):
    T = proj.shape[0]
    pairs = HA // 2
    da_blocks = HA * hd // LANES
    plan, nvar = _attn_plan(S)
    return pl.pallas_call(
        functools.partial(_attn_kernel, plan=plan, hd=hd),
        grid=(pairs, B),
        in_specs=[pl.BlockSpec((S, LANES), lambda p, b: (b, p)),
                  pl.BlockSpec((S, LANES), lambda p, b: (b, da_blocks + p)),
                  pl.BlockSpec((S, LANES), lambda p, b: (b, 2 * da_blocks + p)),
                  pl.BlockSpec((1, LANES), lambda p, b: (0, 0)),
                  pl.BlockSpec((1, LANES), lambda p, b: (0, 0)),
                  pl.BlockSpec((nvar, 2, ATT_QB, ATT_KW), lambda p, b: (0, p, 0, 0))],
        out_specs=pl.BlockSpec((S, LANES), lambda p, b: (b, p)),
        out_shape=jax.ShapeDtypeStruct((T, HA * hd), BF16),
        scratch_shapes=[pltpu.VMEM((S, LANES), F32)] * 3 + [pltpu.VMEM((len(plan), S, LANES), F32)] * 3,
        compiler_params=_cparams(("arbitrary", "arbitrary"), 40),
    )(proj, proj, proj, qg2, kg2, bias_tab)


def _gdn_kernel(q_ref, k_ref, v_ref, z_ref, cw_ref, pack_ref, og_ref, o_ref,
                q_scr, k_scr, v_scr, bc_scr, u_scr, wq_scr, at_scr, kdt_scr, et_scr, of_scr, obk_scr, *, hb):
    S = q_ref.shape[0]
    P2 = 2 * CHUNK
    npair = S // P2
    h = pl.program_id(1)
    gw = 4 * hb

    row = lax.broadcasted_iota(I32, (S, 1), 0)

    def conv_silu(x, which):
        acc = x * cw_ref[which, 2:3, :]
        for d in (-2, -1, 1, 2):
            xs = pltpu.roll(x, (-d) % S, 0)
            ok = (row + d >= 0) & (row + d < S)
            acc = acc + jnp.where(ok, xs, 0.0) * cw_ref[which, 2 + d:3 + d, :]
        return acc * jax.nn.sigmoid(acc)

    def l2n(x):
        return x * lax.rsqrt(jnp.sum(x * x, axis=-1, keepdims=True) + EPS)

    dk = q_ref.shape[1]
    q_scr[...] = l2n(conv_silu(q_ref[...].astype(F32), 0)) * (dk ** -0.5)
    k_scr[...] = l2n(conv_silu(k_ref[...].astype(F32), 1))
    v_scr[...] = conv_silu(v_ref[...].astype(F32), 2)

    pk = pack_ref[...]
    p1 = pk.astype(BF16)
    r1 = pk - p1.astype(F32)
    p2 = r1.astype(BF16)
    p3 = (r1 - p2.astype(F32)).astype(BF16)
    nsel = 8
    srow = lax.broadcasted_iota(I32, (LANES, nsel * LANES), 0)
    scol = lax.broadcasted_iota(I32, (LANES, nsel * LANES), 1) // LANES
    src = h + jnp.where(scol < 6, (scol // 2) * gw + (scol % 2) * hb, 3 * gw + 2 * hb + (scol - 6) * hb)
    sel = jnp.where(srow == src, 1.0, 0.0).astype(BF16)
    bc_scr[...] = _dot(p1, sel) + _dot(p2, sel) + _dot(p3, sel)

    ri = lax.broadcasted_iota(I32, (P2, P2), 0)
    ci = lax.broadcasted_iota(I32, (P2, P2), 1)
    same = (ri // CHUNK) == (ci // CHUNK)
    bd16 = (ri // 16) == (ci // 16)
    incl = (same & (ci <= ri), same & (ci >= ri))
    strict = (same & (ci < ri), same & (ci > ri))

    def mm(a, b):
        return _dot(a.astype(BF16), b.astype(BF16))

    def prep(m, carry):
        rows = pl.ds(pl.multiple_of(m * P2, P2), P2)
        kp = k_scr[rows, :]
        qp = q_scr[rows, :]
        vp = v_scr[rows, :]
        kp_b = kp.astype(BF16)
        for d in range(2):
            gc = bc_scr[rows, d * LANES:(d + 1) * LANES]
            rest = bc_scr[rows, (2 + d) * LANES:(3 + d) * LANES]
            tot = bc_scr[rows, (4 + d) * LANES:(5 + d) * LANES]
            beta = bc_scr[rows, (6 + d) * LANES:(7 + d) * LANES]
            egc = jnp.exp(gc)
            kb = kp * beta
            vb = vp * beta
            kbe = kb * egc
            qe = qp * egc
            kd = kp * jnp.exp(rest)
            dec = jnp.exp(jnp.where(incl[d], gc - gc.T, -jnp.inf))
            kk = _dot_nt(kb.astype(BF16), kp_b)
            qk = _dot_nt(qp.astype(BF16), kp_b)
            lm = jnp.where(strict[d], kk * dec, 0.0)
            attn = qk * dec
            lbd = jnp.where(bd16, lm, 0.0)
            loff = lm - lbd
            nn = -lbd
            pw = mm(lbd, lbd)
            nn = nn + pw + mm(nn, pw)
            pw = mm(pw, pw)
            nn = nn + pw + mm(nn, pw)
            pw = mm(pw, pw)
            nn = nn + pw + mm(nn, pw)
            mo = loff + mm(nn, loff)
            m2 = mm(mo, mo)
            xo = nn - mo - mm(mo, nn)
            toff = xo + m2 + mm(m2, xo)
            rhs = jnp.concatenate([vb, kbe], axis=1)
            uw = rhs + mm(toff, rhs)
            u_scr[d, rows, :] = uw[:, :LANES]
            wb = uw[:, LANES:].astype(BF16)
            qeb = qe.astype(BF16)
            wq_scr[d, pl.ds(pl.multiple_of(m * 2 * P2, 2 * P2), 2 * P2), :] = jnp.concatenate(
                [wb[:CHUNK], qeb[:CHUNK], wb[CHUNK:], qeb[CHUNK:]], axis=0)
            at_scr[d, rows, :] = attn.astype(BF16)
            kdt_scr[d, :, rows] = kd.T.astype(BF16)
            et_scr[d, rows, :] = jnp.exp(tot)
        return carry

    lax.fori_loop(0, npair, prep, 0)

    zero = jnp.zeros((CHUNK, LANES), F32)

    def scan(m, states):
        new_states = []
        for d in range(2):
            st = states[d]
            p = m if d == 0 else npair - 1 - m
            rows = pl.ds(pl.multiple_of(p * P2, P2), P2)
            u_pair = u_scr[d, rows, :]
            at_pair = at_scr[d, rows, :]
            kdt_pair = kdt_scr[d, :, rows]
            wq_pair = wq_scr[d, pl.ds(pl.multiple_of(p * 2 * P2, 2 * P2), 2 * P2), :]
            et_pair = et_scr[d, rows, :]
            outs = [None, None]
            for cpos in ((0, 1) if d == 0 else (1, 0)):
                lo = cpos * CHUNK
                rr = _dot(wq_pair[2 * lo:2 * lo + P2], st.astype(BF16))
                v_new = u_pair[lo:lo + CHUNK] - rr[:CHUNK]
                vn = jnp.concatenate([v_new, zero] if cpos == 0 else [zero, v_new], axis=0).astype(BF16)
                outs[cpos] = rr[CHUNK:] + _dot(at_pair[lo:lo + CHUNK], vn)
                st = st * et_pair[lo:lo + 1] + _dot(kdt_pair, vn)
            o_pair = jnp.concatenate(outs, axis=0)
            if d == 0:
                of_scr[rows, :] = o_pair
            else:
                obk_scr[rows, :] = o_pair
            new_states.append(st)
        return tuple(new_states)

    s0 = jnp.zeros((dk, LANES), F32)
    lax.fori_loop(0, npair, scan, (s0, s0))

    o = of_scr[...] + obk_scr[...]
    y = o * lax.rsqrt(jnp.mean(o * o, axis=-1, keepdims=True) + EPS) * og_ref[...]
    z = z_ref[...].astype(F32)
    o_ref[...] = (y * (z * jax.nn.sigmoid(z))).astype(o_ref.dtype)


def _gdn(proj, cw4, pack3, onorm_g, B, S, HB, base_blk):
    T = proj.shape[0]
    hdb = LANES
    return pl.pallas_call(
        functools.partial(_gdn_kernel, hb=HB),
        grid=(B, HB),
        in_specs=[pl.BlockSpec((S, LANES), lambda b, h: (b, base_blk + h)),
                  pl.BlockSpec((S, LANES), lambda b, h: (b, base_blk + HB + h)),
                  pl.BlockSpec((S, LANES), lambda b, h: (b, base_blk + 2 * HB + h)),
                  pl.BlockSpec((S, LANES), lambda b, h: (b, base_blk + 3 * HB + h)),
                  pl.BlockSpec((None, 3, cw4.shape[2], LANES), lambda b, h: (h, 0, 0, 0)),
                  pl.BlockSpec((None, S, LANES), lambda b, h: (b, 0, 0)),
                  pl.BlockSpec((1, LANES), lambda b, h: (0, 0))],
        out_specs=pl.BlockSpec((S, LANES), lambda b, h: (b, h)),
        out_shape=jax.ShapeDtypeStruct((T, HB * hdb), BF16),
        scratch_shapes=[pltpu.VMEM((S, LANES), F32)] * 3
        + [pltpu.VMEM((S, 8 * LANES), F32),
           pltpu.VMEM((2, S, LANES), F32),
           pltpu.VMEM((2, 2 * S, LANES), BF16),
           pltpu.VMEM((2, S, LANES), BF16),
           pltpu.VMEM((2, LANES, S), BF16),
           pltpu.VMEM((2, S, LANES), F32),
           pltpu.VMEM((S, LANES), F32),
           pltpu.VMEM((S, LANES), F32)],
        compiler_params=_cparams(("arbitrary", "arbitrary"), 48),
    )(proj, proj, proj, proj, cw4, pack3, onorm_g)


def _outproj_kernel(oa_ref, ob_ref, x_ref, gt_ref, w_ref, g_ref, sc_ref, sh_ref, wr_ref, br_ref,
                    x1_ref, h_ref, lg_ref):
    da = oa_ref.shape[1]
    y = _dot(oa_ref[...], w_ref[:da, :]) + _dot(ob_ref[...], w_ref[da:, :])
    x1 = x_ref[...] + gt_ref[...] * y
    x1_ref[...] = x1
    hn = x1 * lax.rsqrt(jnp.mean(x1 * x1, axis=-1, keepdims=True) + EPS) * g_ref[...]
    h = hn * (1.0 + sc_ref[...]) + sh_ref[...]
    h_ref[...] = h
    lg_ref[...] = _dot(h, wr_ref[...], precision=HIGHEST) + br_ref[...]


def _outproj(oa, ob, x2, gt1, w_out_b, g2, sc2, sh2, wr, br, S):
    T, D = x2.shape
    tm = 256
    per_b = S // tm
    bmap = lambda i: (i // per_b, 0, 0)
    return pl.pallas_call(
        _outproj_kernel,
        grid=(T // tm,),
        in_specs=[pl.BlockSpec((tm, oa.shape[1]), lambda i: (i, 0)),
                  pl.BlockSpec((tm, ob.shape[1]), lambda i: (i, 0)),
                  pl.BlockSpec((tm, D), lambda i: (i, 0)),
                  pl.BlockSpec((None, 1, D), bmap),
                  pl.BlockSpec((D, D), lambda i: (0, 0)),
                  pl.BlockSpec((1, D), lambda i: (0, 0)),
                  pl.BlockSpec((None, 1, D), bmap),
                  pl.BlockSpec((None, 1, D), bmap),
                  pl.BlockSpec((D, LANES), lambda i: (0, 0)),
                  pl.BlockSpec((1, LANES), lambda i: (0, 0))],
        out_specs=[pl.BlockSpec((tm, D), lambda i: (i, 0)),
                   pl.BlockSpec((tm, D), lambda i: (i, 0)),
                   pl.BlockSpec((tm, LANES), lambda i: (i, 0))],
        out_shape=[jax.ShapeDtypeStruct((T, D), F32), jax.ShapeDtypeStruct((T, D), F32),
                   jax.ShapeDtypeStruct((T, LANES), F32)],
        compiler_params=_cparams(("arbitrary",), 48),
    )(oa, ob, x2, gt1, w_out_b, g2, sc2, sh2, wr, br)


def _route_kernel(lg_ref, o_ref, cnt_ref, run_scr, *, ne, ng):
    @pl.when(pl.program_id(0) == 0)
    def _():
        run_scr[...] = jnp.zeros_like(run_scr)

    lg = lg_ref[...]
    tm = lg.shape[0]
    epg = ne // ng
    lane_i = lax.broadcasted_iota(I32, lg.shape, 1)
    lane = lane_i.astype(F32)
    big = float(2 * LANES)
    is_g = (lane_i >= ne) & (lane_i < ne + ng)
    gl = jnp.where(is_g, lg, -jnp.inf)
    gmax = jnp.max(gl, axis=-1, keepdims=True)
    gidx = jnp.min(jnp.where(gl == gmax, lane, big), axis=-1, keepdims=True) - ne
    psel = 1.0 / jnp.sum(jnp.where(is_g, jnp.exp(gl - gmax), 0.0), axis=-1, keepdims=True)
    in_grp = (lane_i // epg).astype(F32) == gidx
    el = jnp.where(in_grp & (lane_i < ne), lg, -jnp.inf)
    m1 = jnp.max(el, axis=-1, keepdims=True)
    i1 = jnp.min(jnp.where(el == m1, lane, big), axis=-1, keepdims=True)
    el2 = jnp.where(lane == i1, -jnp.inf, el)
    m2 = jnp.max(el2, axis=-1, keepdims=True)
    i2 = jnp.min(jnp.where(el2 == m2, lane, big), axis=-1, keepdims=True)
    e21 = jnp.exp(m2 - m1)
    g1 = psel / (1.0 + e21)
    g2 = psel * e21 / (1.0 + e21)
    o1 = jnp.where(lane == i1, 1.0, 0.0)
    o2 = jnp.where(lane == i2, 1.0, 0.0)
    cnt = o1 + o2
    r = lax.broadcasted_iota(I32, (tm, tm), 0)
    c = lax.broadcasted_iota(I32, (tm, tm), 1)
    before = _dot(jnp.where(c < r, 1.0, 0.0).astype(BF16), cnt.astype(BF16)) + run_scr[0:1, :]
    rank1 = jnp.sum(o1 * before, axis=-1, keepdims=True)
    rank2 = jnp.sum(o2 * before, axis=-1, keepdims=True)
    run_scr[...] = run_scr[...] + jnp.sum(cnt, axis=0, keepdims=True)
    cnt_ref[...] = run_scr[...]
    cols = (i1, i2, g1, g2, rank1, rank2)
    out = jnp.zeros(lg.shape, F32)
    for j, val in enumerate(cols):
        out = jnp.where(lane_i == j, val, out)
    o_ref[...] = out


def _route(logits, ne, ng):
    T = logits.shape[0]
    tm = 256
    return pl.pallas_call(
        functools.partial(_route_kernel, ne=ne, ng=ng),
        grid=(T // tm,),
        in_specs=[pl.BlockSpec((tm, LANES), lambda i: (i, 0))],
        out_specs=[pl.BlockSpec((tm, LANES), lambda i: (i, 0)),
                   pl.BlockSpec((8, LANES), lambda i: (0, 0))],
        out_shape=[jax.ShapeDtypeStruct((T, LANES), F32), jax.ShapeDtypeStruct((8, LANES), F32)],
        scratch_shapes=[pltpu.VMEM((8, LANES), F32)],
        compiler_params=_cparams(("arbitrary",), 16),
    )(logits)


def _moe_kernel(we_ref, ws_ref, wn_ref, code_ref, h_hbm, w1_ref, w3_ref, w2_ref, y_hbm,
                x32, xbf, yacc, w1b, w3b, w2b, sem_in, sem_out, *, n_tok):
    w = pl.program_id(0)
    c = pl.program_id(1)
    nc = pl.num_programs(1)
    nrows = wn_ref[w]
    start = ws_ref[w]
    nblk = nrows // MOE_RB

    def in_copy(tok, i):
        return pltpu.make_async_copy(h_hbm.at[pl.ds(tok, 1)], x32.at[pl.ds(i, 1)], sem_in)

    def out_copy(i, code):
        return pltpu.make_async_copy(yacc.at[pl.ds(i, 1)], y_hbm.at[pl.ds(code, 1)], sem_out)

    @pl.when(c == 0)
    def _gather():
        def issue(i, carry):
            code = code_ref[start + i]
            in_copy(jnp.where(code < TOP_K * n_tok, lax.rem(code, n_tok), 0), i).start()
            return carry

        lax.fori_loop(0, nrows, issue, 0)

        def wait(i, carry):
            in_copy(0, i).wait()
            return carry

        lax.fori_loop(0, nrows, wait, 0)

        def cast(rb, carry):
            rows = pl.ds(pl.multiple_of(rb * MOE_RB, MOE_RB), MOE_RB)
            xbf[rows, :] = x32[rows, :].astype(BF16)
            return carry

        lax.fori_loop(0, nblk, cast, 0)

    @pl.when(nrows > 0)
    def _compute():
        w1b[...] = w1_ref[...].astype(BF16)
        w3b[...] = w3_ref[...].astype(BF16)
        w2b[...] = w2_ref[...].astype(BF16)

        def blk(rb, carry):
            rows = pl.ds(pl.multiple_of(rb * MOE_RB, MOE_RB), MOE_RB)
            xb = xbf[rows, :]
            h1 = _dot(xb, w1b[...])
            h3 = _dot(xb, w3b[...])
            hid = (h1 * jax.nn.sigmoid(h1) * h3).astype(BF16)
            yc = _dot(hid, w2b[...])

            @pl.when(c == 0)
            def _():
                yacc[rows, :] = yc

            @pl.when(c > 0)
            def _():
                yacc[rows, :] = yacc[rows, :] + yc

            return carry

        lax.fori_loop(0, nblk, blk, 0)

    @pl.when(c == nc - 1)
    def _scatter():
        def issue(i, carry):
            code = code_ref[start + i]

            @pl.when(code < TOP_K * n_tok)
            def _():
                out_copy(i, code).start()

            return carry

        lax.fori_loop(0, nrows, issue, 0)

        def wait(i, carry):
            code = code_ref[start + i]

            @pl.when(code < TOP_K * n_tok)
            def _():
                out_copy(i, code).wait()

            return carry

        lax.fori_loop(0, nrows, wait, 0)


def _moe(we, ws, wn, row_code, h2, w1, w3, w2):
    T, D = h2.shape
    NE, _, DE = w1.shape
    cw = min(MOE_CW, DE)
    nc = DE // cw
    nw = we.shape[0]
    return pl.pallas_call(
        functools.partial(_moe_kernel, n_tok=T),
        grid_spec=pltpu.PrefetchScalarGridSpec(
            num_scalar_prefetch=4,
            grid=(nw, nc),
            in_specs=[pl.BlockSpec(memory_space=pl.ANY),
                      pl.BlockSpec((None, D, cw), lambda w, c, we, ws, wn, code: (we[w], 0, c)),
                      pl.BlockSpec((None, D, cw), lambda w, c, we, ws, wn, code: (we[w], 0, c)),
                      pl.BlockSpec((None, cw, D), lambda w, c, we, ws, wn, code: (we[w], c, 0))],
            out_specs=pl.BlockSpec(memory_space=pl.ANY),
            scratch_shapes=[pltpu.VMEM((MOE_RMAX, D), F32),
                            pltpu.VMEM((MOE_RMAX, D), BF16),
                            pltpu.VMEM((MOE_RMAX, D), F32),
                            pltpu.VMEM((D, cw), BF16),
                            pltpu.VMEM((D, cw), BF16),
                            pltpu.VMEM((cw, D), BF16),
                            pltpu.SemaphoreType.DMA(()),
                            pltpu.SemaphoreType.DMA(())]),
        out_shape=jax.ShapeDtypeStruct((TOP_K * T, D), F32),
        compiler_params=_cparams(("arbitrary", "arbitrary"), 56),
    )(we, ws, wn, row_code, h2, w1, w3, w2)


def _moe_schedule(route, counts, T, NE):
    e = route[:, 0:2].astype(I32)
    rank = route[:, 4:6].astype(I32)
    cnt = counts[0, :NE].astype(I32)
    padded = (cnt + MOE_RB - 1) // MOE_RB * MOE_RB
    cum_pad = jnp.cumsum(padded)
    start_pad = cum_pad - padded
    dest = start_pad[e] + rank
    codes = jnp.arange(T, dtype=I32)[:, None] + T * jnp.arange(TOP_K, dtype=I32)[None, :]
    a_rows = TOP_K * T
    p_rows = a_rows + NE * MOE_RB
    row_code = jnp.full((p_rows,), TOP_K * T, I32).at[dest.reshape(-1)].set(codes.reshape(-1))
    items = (padded + MOE_RMAX - 1) // MOE_RMAX
    cum_items = jnp.cumsum(items)
    n_items = cum_items[-1]
    nw = (p_rows + NE * (MOE_RMAX - MOE_RB)) // MOE_RMAX
    wi = jnp.arange(nw, dtype=I32)
    valid = wi < n_items
    wi_c = jnp.minimum(wi, jnp.maximum(n_items - 1, 0))
    we = jnp.minimum(jnp.searchsorted(cum_items, wi_c, side='right'), NE - 1).astype(I32)
    local = wi_c - (cum_items[we] - items[we])
    ws = (start_pad[we] + local * MOE_RMAX).astype(I32)
    wn = jnp.where(valid, jnp.clip(padded[we] - local * MOE_RMAX, 0, MOE_RMAX), 0).astype(I32)
    return we, ws, wn, row_code


def _combine_kernel(x1_ref, ya_ref, yb_ref, rt_ref, gt_ref, o_ref):
    rt = rt_ref[...]
    y = rt[:, 2:3] * ya_ref[...] + rt[:, 3:4] * yb_ref[...]
    o_ref[...] = x1_ref[...] + gt_ref[...] * y


def _combine(x1, y2, route, gt2, S):
    T, D = x1.shape
    tm = 256
    per_b = S // tm
    nt = T // tm
    return pl.pallas_call(
        _combine_kernel,
        grid=(nt,),
        in_specs=[pl.BlockSpec((tm, D), lambda i: (i, 0)),
                  pl.BlockSpec((tm, D), lambda i: (i, 0)),
                  pl.BlockSpec((tm, D), lambda i: (i + nt, 0)),
                  pl.BlockSpec((tm, LANES), lambda i: (i, 0)),
                  pl.BlockSpec((None, 1, D), lambda i: (i // per_b, 0, 0))],
        out_specs=pl.BlockSpec((tm, D), lambda i: (i, 0)),
        out_shape=jax.ShapeDtypeStruct((T, D), F32),
        compiler_params=_cparams(("arbitrary",), 32),
    )(x1, y2, y2, route, gt2)


def kernel(x, c, w_ada, b_ada, norm1_g, norm2_g, w_in, qn_g, kn_g, rel_bias, conv_w, A_log, dt_bias,
           onorm_g, w_out, w_rg, b_rg, w_re, b_re, w1, w3, w2):
    B, S, D = x.shape
    depth = w_ada.shape[0]
    HA, hda = rel_bias.shape[1], qn_g.shape[-1]
    HB, hdb = A_log.shape[-1], onorm_g.shape[-1]
    DA, DB = HA * hda, HB * hdb
    NG, NE = w_rg.shape[-1], w_re.shape[-1]
    T = B * S
    assert hdb == LANES and 2 * hda == LANES and DA + DB == D and 16 * HB <= LANES
    assert NE + NG <= LANES and conv_w.shape[1] == 5
    n_main = 3 * DA + 4 * DB

    bias_tab = _attn_bias_table(rel_bias, S)
    x2 = x.reshape(T, D)
    for l in range(depth):
        mod = _ada(c, w_ada[l], b_ada[l]).reshape(B, 6, 1, D)
        sh1, sc1, gt1, sh2, sc2, gt2 = (mod[:, i] for i in range(6))

        w_main = w_in[l][:, :n_main].astype(BF16)
        w_gate = jnp.tile(w_in[l][:, n_main:], (1, 4))
        w_gate = jnp.pad(w_gate, ((0, 0), (0, LANES - w_gate.shape[1])))
        proj, gat = _inproj(x2, norm1_g[l].reshape(1, D), sc1, sh1, w_main, w_gate, S)

        def gate_row(p):
            grp = jnp.concatenate([p.reshape(-1), jnp.zeros((2 * HB,), F32)])
            return jnp.pad(jnp.tile(grp, 4), (0, LANES - 16 * HB)).reshape(1, LANES)

        pack = _gating(gat, gate_row(A_log[l]), gate_row(dt_bias[l]), HB)

        oa = _attention(proj, jnp.tile(qn_g[l], 2).reshape(1, LANES), jnp.tile(kn_g[l], 2).reshape(1, LANES),
                        bias_tab, B, S, HA, hda)
        cw4 = jnp.transpose(conv_w[l].reshape(conv_w.shape[1], 3, HB, hdb), (2, 1, 0, 3))
        ob = _gdn(proj, cw4, pack.reshape(B, S, LANES), onorm_g[l].reshape(1, LANES), B, S, HB, 3 * DA // LANES)

        wr = jnp.pad(jnp.concatenate([w_re[l], w_rg[l]], axis=1), ((0, 0), (0, LANES - NE - NG)))
        br = jnp.pad(jnp.concatenate([b_re[l], b_rg[l]]), (0, LANES - NE - NG)).reshape(1, LANES)
        x1, h2, logits = _outproj(oa, ob, x2, gt1, w_out[l].astype(BF16), norm2_g[l].reshape(1, D),
                                  sc2, sh2, wr, br, S)
        route, counts = _route(logits, NE, NG)
        we, ws, wn, row_code = _moe_schedule(route, counts, T, NE)
        y2 = _moe(we, ws, wn, row_code, h2, w1[l], w3[l], w2[l])
        x2 = _combine(x1, y2, route, gt2, S)
    return x2.reshape(B, S, D)
```

```python
import functools

import numpy as np
import jax
import jax.numpy as jnp
from jax import lax
from jax.experimental import pallas as pl
from jax.experimental.pallas import tpu as pltpu

F32 = jnp.float32
BF16 = jnp.bfloat16
I32 = jnp.int32
U32 = jnp.uint32
HIGHEST = lax.Precision.HIGHEST

EPS = 1e-6
NEG = -1e30
DILATED_BRANCHES = ((128, 1), (512, 4), (2048, 16))
REL_MAX_DIST = 1024
CHUNK = 64
TOP_K = 2

LANES = 128
SUBLANES = 8
MIB = 1 << 20

ATT_QB = 128
ATT_KW = 256
ATT_UNROLL = 4
GDN_HP = 2
GDN_PREP_UNROLL = 4
GDN_SEG = 512
MOE_RB = 256
MOE_RMAX = 1024
MOE_CW = 256
ROW_TILE = 256
DMA_UNROLL = 8


def _cparams(sem, vmem_mib):
    return pltpu.CompilerParams(dimension_semantics=sem, vmem_limit_bytes=vmem_mib * MIB)


def _dot(a, b, **kw):
    return jnp.dot(a, b, preferred_element_type=F32, **kw)


def _dot_nt(a, b):
    return lax.dot_general(a, b, (((1,), (1,)), ((), ())), preferred_element_type=F32)


def _pick_tile(n, prefs):
    for t in prefs:
        if n % t == 0:
            return t
    return n


def _pack_halves(x):
    half = x.shape[1] // 2
    lo = lax.bitcast_convert_type(x[:, :half].astype(BF16).astype(F32), U32)
    hi = lax.bitcast_convert_type(x[:, half:].astype(BF16).astype(F32), U32)
    return lax.shift_right_logical(lo, jnp.uint32(16)) | (hi & jnp.uint32(0xFFFF0000))


def _unpack_halves(p):
    lo = lax.bitcast_convert_type(lax.shift_left(p, jnp.uint32(16)), F32)
    hi = lax.bitcast_convert_type(p & jnp.uint32(0xFFFF0000), F32)
    return lo, hi


def _ada_kernel(c_ref, w_ref, b_ref, o_ref):
    c = c_ref[...]
    s = (c * jax.nn.sigmoid(c)).astype(BF16)
    o_ref[...] = _dot(s, w_ref[...].astype(BF16)) + b_ref[...]


def _ada(c, w_ada, b_ada):
    B, D = c.shape
    N = w_ada.shape[1]
    tn = _pick_tile(N, (1024, 512, 256, 128))
    return pl.pallas_call(
        _ada_kernel,
        grid=(N // tn,),
        in_specs=[pl.BlockSpec((B, D), lambda j: (0, 0)),
                  pl.BlockSpec((D, tn), lambda j: (0, j)),
                  pl.BlockSpec((1, tn), lambda j: (0, j))],
        out_specs=pl.BlockSpec((B, tn), lambda j: (0, j)),
        out_shape=jax.ShapeDtypeStruct((B, N), F32),
        compiler_params=_cparams(("arbitrary",), 40),
    )(c, w_ada, b_ada.reshape(1, N))


def _inproj_kernel(x_ref, g_ref, sc_ref, sh_ref, w_ref, wg_ref, o_ref, og_ref, h_scr):
    @pl.when(pl.program_id(1) == 0)
    def _():
        x = x_ref[...]
        y = x * lax.rsqrt(jnp.mean(x * x, axis=-1, keepdims=True) + EPS) * g_ref[...]
        h = y * (1.0 + sc_ref[...]) + sh_ref[...]
        h_scr[...] = h.astype(BF16)
        og_ref[...] = _dot(h, wg_ref[...], precision=HIGHEST)

    o_ref[...] = _dot(h_scr[...], w_ref[...]).astype(o_ref.dtype)


def _inproj(x2, g, sc, sh, w_main, w_gate, S):
    T, D = x2.shape
    NM = w_main.shape[1]
    tm = _pick_tile(S, (512, 256, 128))
    tn = _pick_tile(NM, (1024, 512, 256, 128))
    per_b = S // tm
    return pl.pallas_call(
        _inproj_kernel,
        grid=(T // tm, NM // tn),
        in_specs=[pl.BlockSpec((tm, D), lambda i, j: (i, 0)),
                  pl.BlockSpec((1, D), lambda i, j: (0, 0)),
                  pl.BlockSpec((None, 1, D), lambda i, j: (i // per_b, 0, 0)),
                  pl.BlockSpec((None, 1, D), lambda i, j: (i // per_b, 0, 0)),
                  pl.BlockSpec((D, tn), lambda i, j: (0, j)),
                  pl.BlockSpec((D, LANES), lambda i, j: (0, 0))],
        out_specs=[pl.BlockSpec((tm, tn), lambda i, j: (i, j)),
                   pl.BlockSpec((tm, LANES), lambda i, j: (i, 0))],
        out_shape=[jax.ShapeDtypeStruct((T, NM), BF16), jax.ShapeDtypeStruct((T, LANES), F32)],
        scratch_shapes=[pltpu.VMEM((tm, D), BF16)],
        compiler_params=_cparams(("arbitrary", "arbitrary"), 48),
    )(x2, g, sc, sh, w_main, w_gate)


def _gating_kernel(gat_ref, a_ref, dt_ref, o_ref, *, hb):
    gat = gat_ref[...]
    tm = gat.shape[0]
    gw = 4 * hb
    g = -jnp.exp(a_ref[...]) * jax.nn.softplus(gat + dt_ref[...])
    beta = jax.nn.sigmoid(gat)
    r = lax.broadcasted_iota(I32, (tm, tm), 0)
    c = lax.broadcasted_iota(I32, (tm, tm), 1)
    same = (r // CHUNK) == (c // CHUNK)
    pre = _dot(jnp.where(same & (c <= r), 1.0, 0.0), g, precision=HIGHEST)
    suf = _dot(jnp.where(same & (c >= r), 1.0, 0.0), g, precision=HIGHEST)
    lane = lax.broadcasted_iota(I32, gat.shape, 1)
    o_ref[...] = jnp.where(lane // gw == 0, jnp.where(lane % gw < hb, pre, suf), beta)


def _gating(gat, a_row, dt_row, hb):
    T = gat.shape[0]
    tm = ROW_TILE
    return pl.pallas_call(
        functools.partial(_gating_kernel, hb=hb),
        grid=(T // tm,),
        in_specs=[pl.BlockSpec((tm, LANES), lambda i: (i, 0)),
                  pl.BlockSpec((1, LANES), lambda i: (0, 0)),
                  pl.BlockSpec((1, LANES), lambda i: (0, 0))],
        out_specs=pl.BlockSpec((tm, LANES), lambda i: (i, 0)),
        out_shape=jax.ShapeDtypeStruct((T, LANES), F32),
        compiler_params=_cparams(("arbitrary",), 16),
    )(gat, a_row, dt_row)


def _t5_bucket(rel, n_buckets):
    half = n_buckets // 2
    max_exact = half // 2
    n = np.abs(rel)
    large = max_exact + (np.log(np.maximum(n, 1) / max_exact) / np.log(REL_MAX_DIST / max_exact)
                         * (half - max_exact)).astype(np.int32)
    large = np.minimum(large, half - 1)
    return (np.where(rel > 0, half, 0) + np.where(n < max_exact, n, large)).astype(np.int32)


def _attn_plan(S):
    plan, base = [], 0
    for window, dil in DILATED_BRANCHES:
        n = window // (2 * dil)
        L = S // dil
        assert L % ATT_QB == 0 and n * 2 == ATT_QB
        kw = min(ATT_KW, L)
        nbq = L // ATT_QB
        nvar = 1 if nbq == 1 else 3
        plan.append((dil, L, nbq, kw, base, nvar, n))
        base += nvar
    return tuple(plan), base


def _attn_bias_table(rel_bias, S):
    plan, nvar_total = _attn_plan(S)
    nbuckets, H = rel_bias.shape
    W = ATT_QB + ATT_KW
    u = np.arange(W) - (ATT_QB - 1)
    onehots, bands = [], []
    for dil, L, nbq, kw, base, nvar, n in plan:
        offs = [0] if nvar == 1 else [0, -n, -(kw - ATT_QB)]
        for off in offs:
            rel = off + u
            onehots.append(np.eye(nbuckets, dtype=np.float32)[_t5_bucket(rel * dil, nbuckets)])
            bands.append(np.abs(rel) <= n)
    onehot = jnp.asarray(np.stack(onehots))
    band = jnp.asarray(np.stack(bands))
    prof = jnp.einsum('vwn,nh->vhw', onehot, rel_bias.astype(F32), precision=HIGHEST)
    prof = jnp.where(band[:, None, :], prof, NEG)
    skew = jnp.tile(prof, (1, 1, ATT_QB))[:, :, :ATT_QB * (W - 1)].reshape(nvar_total, H, ATT_QB, W - 1)
    return skew[:, :, :, ATT_QB - 1:ATT_QB - 1 + ATT_KW]


def _attn_kernel(q_ref, k_ref, v_ref, qg_ref, kg_ref, bias_ref, o_ref,
                 qn_scr, kn_scr, v_scr, ob_scr, mb_scr, db_scr, *, plan, hd):
    S = q_ref.shape[0]
    lane = lax.broadcasted_iota(I32, (1, LANES), 1)
    left = lane < hd

    def headnorm(x, g):
        x2 = x * x
        s_all = jnp.sum(x2, axis=-1, keepdims=True)
        s_left = jnp.sum(jnp.where(left, x2, 0.0), axis=-1, keepdims=True)
        ms = jnp.where(left, s_left, s_all - s_left) * (1.0 / hd)
        return x * lax.rsqrt(ms + EPS) * g

    qn_scr[...] = headnorm(q_ref[...].astype(F32), qg_ref[...]) * (hd ** -0.5)
    kn_scr[...] = headnorm(k_ref[...].astype(F32), kg_ref[...])
    v_scr[...] = v_ref[...].astype(F32)

    for bi, (dil, L, nbq, kw, base, nvar, n) in enumerate(plan):
        ones = jnp.ones((kw, LANES), BF16)

        def body(t, carry, dil=dil, L=L, nbq=nbq, kw=kw, base=base, nvar=nvar, n=n, bi=bi, ones=ones):
            blocks = []
            for uu in range(ATT_UNROLL):
                idx = t * ATT_UNROLL + uu
                r = idx // nbq
                i = idx % nbq
                q0 = i * ATT_QB
                k0 = jnp.clip(q0 - n, 0, L - kw)
                var = base if nvar == 1 else base + jnp.where(i > 0, 1, 0) + jnp.where(i == nbq - 1, 1, 0)
                if dil == 1:
                    qrows = pl.ds(pl.multiple_of(q0, ATT_QB), ATT_QB)
                    krows = pl.ds(pl.multiple_of(k0, CHUNK), kw)
                else:
                    qrows = pl.ds(r + q0 * dil, ATT_QB, stride=dil)
                    krows = pl.ds(r + k0 * dil, kw, stride=dil)
                qb = qn_scr[qrows, :]
                q2 = jnp.concatenate([jnp.where(left, qb, 0.0), jnp.where(left, 0.0, qb)], axis=0).astype(BF16)
                blocks.append((qrows, krows, var, q2))
            scores = [_dot_nt(q2, kn_scr[krows, :].astype(BF16)) for qrows, krows, var, q2 in blocks]
            probs, maxes = [], []
            for (qrows, krows, var, q2), s in zip(blocks, scores):
                s = s + jnp.concatenate([bias_ref[var, 0][:, :kw], bias_ref[var, 1][:, :kw]], axis=0)
                m = jnp.max(s, axis=-1, keepdims=True)
                probs.append(jnp.exp(s - m).astype(BF16))
                maxes.append(m)
            outs = [_dot(p, jnp.concatenate([v_scr[krows, :].astype(BF16), ones], axis=1))
                    for (qrows, krows, var, q2), p in zip(blocks, probs)]
            for (qrows, krows, var, q2), m, od in zip(blocks, maxes, outs):
                mb = jnp.broadcast_to(m, (2 * ATT_QB, LANES))
                ob_scr[bi, qrows, :] = jnp.where(left, od[:ATT_QB, :LANES], od[ATT_QB:, :LANES])
                mb_scr[bi, qrows, :] = jnp.where(left, mb[:ATT_QB], mb[ATT_QB:])
                db_scr[bi, qrows, :] = jnp.where(left, od[:ATT_QB, LANES:], od[ATT_QB:, LANES:])
            return carry

        assert (dil * nbq) % ATT_UNROLL == 0
        lax.fori_loop(0, dil * nbq // ATT_UNROLL, body, 0)

    nb = len(plan)
    mx = mb_scr[0]
    for bi in range(1, nb):
        mx = jnp.maximum(mx, mb_scr[bi])
    num = jnp.zeros((S, LANES), F32)
    den = jnp.zeros((S, LANES), F32)
    for bi in range(nb):
        w = jnp.exp(mb_scr[bi] - mx)
        num = num + w * ob_scr[bi]
        den = den + w * db_scr[bi]
    o_ref[...] = (num / den).astype(o_ref.dtype)


def _attention(proj, qg2, kg2, bias_tab, B, S, HA, hd):
    T = proj.shape[0]
    pairs = HA // 2
    da_blocks = HA * hd // LANES
    plan, nvar = _attn_plan(S)
    return pl.pallas_call(
        functools.partial(_attn_kernel, plan=plan, hd=hd),
        grid=(pairs, B),
        in_specs=[pl.BlockSpec((S, LANES), lambda p, b: (b, p)),
                  pl.BlockSpec((S, LANES), lambda p, b: (b, da_blocks + p)),
                  pl.BlockSpec((S, LANES), lambda p, b: (b, 2 * da_blocks + p)),
                  pl.BlockSpec((1, LANES), lambda p, b: (0, 0)),
                  pl.BlockSpec((1, LANES), lambda p, b: (0, 0)),
                  pl.BlockSpec((nvar, 2, ATT_QB, ATT_KW), lambda p, b: (0, p, 0, 0))],
        out_specs=pl.BlockSpec((S, LANES), lambda p, b: (b, p)),
        out_shape=jax.ShapeDtypeStruct((T, HA * hd), BF16),
        scratch_shapes=[pltpu.VMEM((S, LANES), F32)] * 3 + [pltpu.VMEM((len(plan), S, LANES), F32)] * 3,
        compiler_params=_cparams(("arbitrary", "arbitrary"), 40),
    )(proj, proj, proj, qg2, kg2, bias_tab)


def _gdn_kernel(q_ref, k_ref, v_ref, z_ref, cw_ref, pack_ref, og_ref, o_ref,
                q_scr, k_scr, v_scr, xpad_scr, pk2_scr, sel_scr, u_scr, wq_scr, at_scr, kdt_scr, et_scr, oacc_scr,
                *, hb, hp):
    S = q_ref.shape[0]
    P2 = 2 * CHUNK
    W2 = 2 * LANES
    npair = S // P2
    hg = pl.program_id(1)
    gw = 4 * hb
    dk = LANES

    pad = SUBLANES
    for slot in range(2):
        xpad_scr[slot, 0:pad, :] = jnp.zeros((pad, LANES), F32)
        xpad_scr[slot, pad + S:, :] = jnp.zeros((pad, LANES), F32)

    seg = GDN_SEG if S % GDN_SEG == 0 else S

    def conv_silu_to(src_ref, lanes_j, j, which, dst_scr, l2norm, scale):
        xp = xpad_scr.at[(3 * j + which) % 2]
        for s0 in range(0, S, seg):
            xp[pad + s0:pad + s0 + seg, :] = src_ref[s0:s0 + seg, lanes_j].astype(F32)
        for s0 in range(0, S, seg):
            acc = xp[pad + s0:pad + s0 + seg, :] * cw_ref[j, which, 2:3, :]
            for d in (-2, -1, 1, 2):
                acc = acc + xp[pad + s0 + d:pad + s0 + d + seg, :] * cw_ref[j, which, 2 + d:3 + d, :]
            y = acc * jax.nn.sigmoid(acc)
            if l2norm:
                y = y * (lax.rsqrt(jnp.sum(y * y, axis=-1, keepdims=True) + EPS) * scale)
            dst_scr[j, s0:s0 + seg, :] = y

    for s0 in range(0, S, seg):
        pk = pack_ref[s0:s0 + seg, :]
        p_hi = pk.astype(BF16)
        p_lo = (pk - p_hi.astype(F32)).astype(BF16)
        pk2_scr[s0:s0 + seg, :] = jnp.concatenate([p_hi, p_lo], axis=1)
    srow = lax.broadcasted_iota(I32, (W2, 4 * LANES), 0) % LANES
    scol = lax.broadcasted_iota(I32, (W2, 4 * LANES), 1) // LANES
    for j in range(hp):
        lanes_j = slice(j * LANES, (j + 1) * LANES)
        conv_silu_to(q_ref, lanes_j, j, 0, q_scr, True, dk ** -0.5)
        conv_silu_to(k_ref, lanes_j, j, 1, k_scr, True, 1.0)
        conv_silu_to(v_ref, lanes_j, j, 2, v_scr, False, 1.0)
        src = hg * hp + j + jnp.where(scol < 2, scol * hb, gw + 2 * hb + (scol - 2) * hb)
        sel_scr[j] = jnp.where(srow == src, 1.0, 0.0).astype(BF16)
        oacc_scr[j] = jnp.zeros((S, LANES), F32)

    ri = lax.broadcasted_iota(I32, (P2, W2), 0)
    cfull = lax.broadcasted_iota(I32, (P2, W2), 1)
    ci = cfull % LANES
    bwd = cfull >= LANES
    same = (ri // CHUNK) == (ci // CHUNK)
    bd16 = (ri // 16) == (ci // 16)
    ahead = jnp.where(bwd, ri - ci, ci - ri)
    incl = same & (ahead <= 0)
    strict = same & (ahead < 0)
    first = lax.broadcasted_iota(I32, (P2, 1), 0) < CHUNK
    zpair = jnp.zeros((P2, LANES), BF16)

    def bdiag(x):
        return jnp.concatenate([jnp.concatenate([x[:, :LANES], zpair], axis=1),
                                jnp.concatenate([zpair, x[:, LANES:]], axis=1)], axis=0)

    def mm(a, b):
        return _dot(a.astype(BF16), bdiag(b.astype(BF16)))

    U = GDN_PREP_UNROLL if npair % GDN_PREP_UNROLL == 0 else 1

    def prep(t, carry):
        cx = []
        chains = [(t * U + u, j) for u in range(U) for j in range(hp)]
        bcs = [_dot(pk2_scr[pl.ds(pl.multiple_of(m * P2, P2), P2), :], sel_scr[j]) for m, j in chains]
        for (m, j), bc in zip(chains, bcs):
            rows = pl.ds(pl.multiple_of(m * P2, P2), P2)
            kp = k_scr[j, rows, :]
            qp = q_scr[j, rows, :]
            vp = v_scr[j, rows, :]
            gc2 = bc[:, 0:W2]
            beta2 = bc[:, W2:2 * W2]
            gcf, gcb = gc2[:, :LANES], gc2[:, LANES:]
            tot2 = jnp.concatenate([jnp.where(first, gcf[CHUNK - 1:CHUNK], gcf[P2 - 1:P2]),
                                    jnp.where(first, gcb[0:1], gcb[CHUNK:CHUNK + 1])], axis=1)
            egc2 = jnp.exp(gc2)
            k2 = jnp.concatenate([kp, kp], axis=1)
            kb2 = k2 * beta2
            vb2 = jnp.concatenate([vp, vp], axis=1) * beta2
            kbe2 = kb2 * egc2
            cx.append(dict(
                m=m, j=j, rows=rows, vb2=vb2, kbe2=kbe2,
                qeb=(jnp.concatenate([qp, qp], axis=1) * egc2).astype(BF16),
                kd2=k2 * jnp.exp(tot2 - gc2),
                et=jnp.exp(tot2),
                dec2=jnp.exp(jnp.where(incl, gc2 - jnp.concatenate([gcf.T, gcb.T], axis=1), -jnp.inf)),
                stk=jnp.concatenate([kb2[:, :LANES], kb2[:, LANES:], qp], axis=0).astype(BF16),
                kpb=kp.astype(BF16)))

        g3s = [_dot_nt(c['stk'], c['kpb']) for c in cx]
        for c, g3 in zip(cx, g3s):
            lm = jnp.where(strict, jnp.concatenate([g3[:P2], g3[P2:2 * P2]], axis=1) * c['dec2'], 0.0)
            c['attn2'] = (jnp.concatenate([g3[2 * P2:], g3[2 * P2:]], axis=1) * c['dec2']).astype(BF16)
            c['lbd'] = jnp.where(bd16, lm, 0.0)
            c['loff'] = lm - c['lbd']
        nn = [-c['lbd'] for c in cx]
        pw = [mm(c['lbd'], c['lbd']) for c in cx]
        for rnd in range(3):
            prod = [mm(a, p) for a, p in zip(nn, pw)]
            nxt = [mm(p, p) for p in pw] if rnd < 2 else pw
            nn = [a + p + q for a, p, q in zip(nn, pw, prod)]
            pw = nxt
        mo = [c['loff'] + x for c, x in zip(cx, [mm(a, c['loff']) for a, c in zip(nn, cx)])]
        m2 = [mm(x, x) for x in mo]
        mn = [mm(x, a) for x, a in zip(mo, nn)]
        xo = [a - x - y for a, x, y in zip(nn, mo, mn)]
        mx = [mm(a, b) for a, b in zip(m2, xo)]
        toff = [a + b + q for a, b, q in zip(xo, m2, mx)]
        z2 = jnp.zeros((P2, W2), BF16)
        tws = []
        for c, tf in zip(cx, toff):
            vbb, kbb = c['vb2'].astype(BF16), c['kbe2'].astype(BF16)
            rhs = jnp.concatenate(
                [jnp.concatenate([vbb[:, :LANES], kbb[:, :LANES], z2], axis=1),
                 jnp.concatenate([z2, vbb[:, LANES:], kbb[:, LANES:]], axis=1)], axis=0)
            tws.append(_dot(tf.astype(BF16), rhs))
        for c, tw in zip(cx, tws):
            m, j, rows = c['m'], c['j'], c['rows']
            u2 = c['vb2'] + jnp.concatenate([tw[:, :LANES], tw[:, W2:W2 + LANES]], axis=1)
            w2 = (c['kbe2'] + jnp.concatenate([tw[:, LANES:W2], tw[:, W2 + LANES:]], axis=1)).astype(BF16)
            qeb = c['qeb']
            u_scr[j, rows, :] = u2
            wq_scr[j, pl.ds(pl.multiple_of(m * 2 * P2, 2 * P2), 2 * P2), :] = jnp.concatenate(
                [w2[:CHUNK], qeb[:CHUNK], w2[CHUNK:], qeb[CHUNK:]], axis=0)
            at_scr[j, rows, :] = c['attn2']
            kdt_scr[j, 0, :, rows] = c['kd2'][:, :LANES].T.astype(BF16)
            kdt_scr[j, 1, :, rows] = c['kd2'][:, LANES:].T.astype(BF16)
            et_scr[j, pl.ds(pl.multiple_of(m * 2 * SUBLANES, 2 * SUBLANES), 2 * SUBLANES), :] = jnp.concatenate(
                [c['et'][:SUBLANES], c['et'][CHUNK:CHUNK + SUBLANES]], axis=0)
        return carry

    lax.fori_loop(0, npair // U, prep, 0)

    zc = jnp.zeros((CHUNK, LANES), F32)
    zp = jnp.zeros((P2, LANES), F32)

    def place(v, cpos):
        return jnp.concatenate([v, zc] if cpos == 0 else [zc, v], axis=0)

    def scan(m, states):
        pf = m
        pb = npair - 1 - m
        rows_f = pl.ds(pl.multiple_of(pf * P2, P2), P2)
        rows_b = pl.ds(pl.multiple_of(pb * P2, P2), P2)
        hx = []
        for j in range(hp):
            hx.append(dict(
                u_f=u_scr[j, rows_f, :LANES], u_b=u_scr[j, rows_b, LANES:],
                at_f=at_scr[j, rows_f, :LANES], at_b=at_scr[j, rows_b, LANES:],
                kdt=jnp.concatenate([kdt_scr[j, 0, :, rows_f], kdt_scr[j, 1, :, rows_b]], axis=1),
                wq_f=wq_scr[j, pl.ds(pl.multiple_of(pf * 2 * P2, 2 * P2), 2 * P2), :LANES],
                wq_b=wq_scr[j, pl.ds(pl.multiple_of(pb * 2 * P2, 2 * P2), 2 * P2), LANES:],
                et_f=et_scr[j, pl.ds(pl.multiple_of(pf * 2 * SUBLANES, 2 * SUBLANES), 2 * SUBLANES), :LANES],
                et_b=et_scr[j, pl.ds(pl.multiple_of(pb * 2 * SUBLANES, 2 * SUBLANES), 2 * SUBLANES), LANES:]))
        sts = list(states)
        for step in range(2):
            cf, cb = step, 1 - step
            rrs = [_dot(jnp.concatenate([c['wq_f'][cf * P2:(cf + 1) * P2], c['wq_b'][cb * P2:(cb + 1) * P2]], axis=1),
                        bdiag(st.astype(BF16))) for c, st in zip(hx, sts)]
            ress = []
            for c, rr in zip(hx, rrs):
                u2 = jnp.concatenate([c['u_f'][cf * CHUNK:(cf + 1) * CHUNK],
                                      c['u_b'][cb * CHUNK:(cb + 1) * CHUNK]], axis=1)
                v_new = u2 - rr[:CHUNK]
                rhs = jnp.concatenate(
                    [jnp.concatenate([place(v_new[:, :LANES], cf), zp], axis=1),
                     jnp.concatenate([zp, place(v_new[:, LANES:], cb)], axis=1)], axis=0).astype(BF16)
                lhs = jnp.concatenate(
                    [jnp.concatenate([c['at_f'][cf * CHUNK:(cf + 1) * CHUNK],
                                      c['at_b'][cb * CHUNK:(cb + 1) * CHUNK]], axis=1),
                     c['kdt']], axis=0)
                ress.append(_dot(lhs, rhs))
            for j, (c, rr, res) in enumerate(zip(hx, rrs, ress)):
                o2 = rr[CHUNK:] + res[:CHUNK]
                et2 = jnp.concatenate([c['et_f'][cf * SUBLANES:cf * SUBLANES + 1],
                                       c['et_b'][cb * SUBLANES:cb * SUBLANES + 1]], axis=1)
                sts[j] = sts[j] * et2 + res[CHUNK:]
                of_rows = pl.ds(pl.multiple_of(pf * P2 + cf * CHUNK, CHUNK), CHUNK)
                ob_rows = pl.ds(pl.multiple_of(pb * P2 + cb * CHUNK, CHUNK), CHUNK)
                oacc_scr[j, of_rows, :] = oacc_scr[j, of_rows, :] + o2[:, :LANES]
                oacc_scr[j, ob_rows, :] = oacc_scr[j, ob_rows, :] + o2[:, LANES:]
        return tuple(sts)

    s0 = jnp.zeros((dk, W2), F32)
    lax.fori_loop(0, npair, scan, (s0,) * hp)

    for j in range(hp):
        lanes_j = slice(j * LANES, (j + 1) * LANES)
        for s0 in range(0, S, seg):
            o = oacc_scr[j, s0:s0 + seg, :]
            y = o * lax.rsqrt(jnp.mean(o * o, axis=-1, keepdims=True) + EPS) * og_ref[...]
            z = z_ref[s0:s0 + seg, lanes_j].astype(F32)
            o_ref[s0:s0 + seg, lanes_j] = (y * (z * jax.nn.sigmoid(z))).astype(o_ref.dtype)


def _gdn(proj, cw4, pack3, onorm_g, B, S, HB, base_blk):
    T = proj.shape[0]
    hp = GDN_HP if (HB % GDN_HP == 0 and base_blk % GDN_HP == 0) else 1
    wblk = hp * LANES
    npair = S // (2 * CHUNK)

    def col(k):
        off = (base_blk + k * HB) // hp
        return lambda b, h: (b, off + h)

    return pl.pallas_call(
        functools.partial(_gdn_kernel, hb=HB, hp=hp),
        grid=(B, HB // hp),
        in_specs=[pl.BlockSpec((S, wblk), col(0)),
                  pl.BlockSpec((S, wblk), col(1)),
                  pl.BlockSpec((S, wblk), col(2)),
                  pl.BlockSpec((S, wblk), col(3)),
                  pl.BlockSpec((hp, 3, cw4.shape[2], LANES), lambda b, h: (h, 0, 0, 0)),
                  pl.BlockSpec((None, S, LANES), lambda b, h: (b, 0, 0)),
                  pl.BlockSpec((1, LANES), lambda b, h: (0, 0))],
        out_specs=pl.BlockSpec((S, wblk), lambda b, h: (b, h)),
        out_shape=jax.ShapeDtypeStruct((T, HB * LANES), BF16),
        scratch_shapes=[pltpu.VMEM((hp, S, LANES), F32)] * 3
        + [pltpu.VMEM((2, S + 2 * SUBLANES, LANES), F32),
           pltpu.VMEM((S, 2 * LANES), BF16),
           pltpu.VMEM((hp, 2 * LANES, 4 * LANES), BF16),
           pltpu.VMEM((hp, S, 2 * LANES), F32),
           pltpu.VMEM((hp, 2 * S, 2 * LANES), BF16),
           pltpu.VMEM((hp, S, 2 * LANES), BF16),
           pltpu.VMEM((hp, 2, LANES, S), BF16),
           pltpu.VMEM((hp, npair * 2 * SUBLANES, 2 * LANES), F32),
           pltpu.VMEM((hp, S, LANES), F32)],
        compiler_params=_cparams(("arbitrary", "arbitrary"), 56),
    )(proj, proj, proj, proj, cw4, pack3, onorm_g)


def _outproj_kernel(oa_ref, ob_ref, x_ref, gt_ref, w_ref, g_ref, sc_ref, sh_ref, wr_ref, br_ref,
                    x1_ref, hp_ref, lg_ref):
    da = oa_ref.shape[1]
    y = _dot(oa_ref[...], w_ref[:da, :]) + _dot(ob_ref[...], w_ref[da:, :])
    x1 = x_ref[...] + gt_ref[...] * y
    x1_ref[...] = x1
    hn = x1 * lax.rsqrt(jnp.mean(x1 * x1, axis=-1, keepdims=True) + EPS) * g_ref[...]
    h = hn * (1.0 + sc_ref[...]) + sh_ref[...]
    hp_ref[...] = _pack_halves(h)
    lg_ref[...] = _dot(h, wr_ref[...], precision=HIGHEST) + br_ref[...]


def _outproj(oa, ob, x2, gt1, w_out_b, g2, sc2, sh2, wr, br, S):
    T, D = x2.shape
    tm = ROW_TILE
    per_b = S // tm
    bmap = lambda i: (i // per_b, 0, 0)
    return pl.pallas_call(
        _outproj_kernel,
        grid=(T // tm,),
        in_specs=[pl.BlockSpec((tm, oa.shape[1]), lambda i: (i, 0)),
                  pl.BlockSpec((tm, ob.shape[1]), lambda i: (i, 0)),
                  pl.BlockSpec((tm, D), lambda i: (i, 0)),
                  pl.BlockSpec((None, 1, D), bmap),
                  pl.BlockSpec((D, D), lambda i: (0, 0)),
                  pl.BlockSpec((1, D), lambda i: (0, 0)),
                  pl.BlockSpec((None, 1, D), bmap),
                  pl.BlockSpec((None, 1, D), bmap),
                  pl.BlockSpec((D, LANES), lambda i: (0, 0)),
                  pl.BlockSpec((1, LANES), lambda i: (0, 0))],
        out_specs=[pl.BlockSpec((tm, D), lambda i: (i, 0)),
                   pl.BlockSpec((tm, D // 2), lambda i: (i, 0)),
                   pl.BlockSpec((tm, LANES), lambda i: (i, 0))],
        out_shape=[jax.ShapeDtypeStruct((T, D), F32), jax.ShapeDtypeStruct((T, D // 2), U32),
                   jax.ShapeDtypeStruct((T, LANES), F32)],
        compiler_params=_cparams(("arbitrary",), 48),
    )(oa, ob, x2, gt1, w_out_b, g2, sc2, sh2, wr, br)


def _route_kernel(lg_ref, o_ref, info_ref, run_scr, *, ne, ng):
    ph = pl.program_id(0)
    i = pl.program_id(1)

    @pl.when((ph == 0) & (i == 0))
    def _():
        run_scr[...] = jnp.zeros_like(run_scr)

    @pl.when((ph == 1) & (i == 0))
    def _():
        cnt = run_scr[...]
        padded = jnp.ceil(cnt * (1.0 / MOE_RB)) * MOE_RB
        k = lax.broadcasted_iota(I32, (LANES, LANES), 0)
        e = lax.broadcasted_iota(I32, (LANES, LANES), 1)
        start = _dot(padded, jnp.where(k < e, 1.0, 0.0), precision=HIGHEST)
        rowi = lax.broadcasted_iota(I32, cnt.shape, 0)
        info_ref[...] = jnp.where(rowi == 0, cnt, jnp.where(rowi == 1, padded, start))
        run_scr[...] = start

    lg = lg_ref[...]
    tm = lg.shape[0]
    epg = ne // ng
    lane_i = lax.broadcasted_iota(I32, lg.shape, 1)
    lane = lane_i.astype(F32)
    big = float(2 * LANES)
    is_g = (lane_i >= ne) & (lane_i < ne + ng)
    gl = jnp.where(is_g, lg, -jnp.inf)
    gmax = jnp.max(gl, axis=-1, keepdims=True)
    gidx = jnp.min(jnp.where(gl == gmax, lane, big), axis=-1, keepdims=True) - ne
    psel = 1.0 / jnp.sum(jnp.where(is_g, jnp.exp(gl - gmax), 0.0), axis=-1, keepdims=True)
    in_grp = (lane_i // epg).astype(F32) == gidx
    el = jnp.where(in_grp & (lane_i < ne), lg, -jnp.inf)
    m1 = jnp.max(el, axis=-1, keepdims=True)
    i1 = jnp.min(jnp.where(el == m1, lane, big), axis=-1, keepdims=True)
    el2 = jnp.where(lane == i1, -jnp.inf, el)
    m2 = jnp.max(el2, axis=-1, keepdims=True)
    i2 = jnp.min(jnp.where(el2 == m2, lane, big), axis=-1, keepdims=True)
    e21 = jnp.exp(m2 - m1)
    g1 = psel / (1.0 + e21)
    g2 = psel * e21 / (1.0 + e21)
    o1 = jnp.where(lane == i1, 1.0, 0.0)
    o2 = jnp.where(lane == i2, 1.0, 0.0)
    cnt = o1 + o2
    r = lax.broadcasted_iota(I32, (tm, tm), 0)
    c = lax.broadcasted_iota(I32, (tm, tm), 1)
    before = _dot(jnp.where(c < r, 1.0, 0.0).astype(BF16), cnt.astype(BF16)) + run_scr[0:1, :]
    d1 = jnp.sum(o1 * before, axis=-1, keepdims=True)
    d2 = jnp.sum(o2 * before, axis=-1, keepdims=True)
    run_scr[...] = run_scr[...] + jnp.sum(cnt, axis=0, keepdims=True)

    @pl.when(ph == 1)
    def _():
        out = jnp.zeros(lg.shape, F32)
        for j, val in enumerate((i1, i2, g1, g2, d1, d2)):
            out = jnp.where(lane_i == j, val, out)
        o_ref[...] = out


def _route(logits, ne, ng):
    T = logits.shape[0]
    tm = ROW_TILE
    return pl.pallas_call(
        functools.partial(_route_kernel, ne=ne, ng=ng),
        grid=(2, T // tm),
        in_specs=[pl.BlockSpec((tm, LANES), lambda p, i: (i, 0))],
        out_specs=[pl.BlockSpec((tm, LANES), lambda p, i: (i * p, 0)),
                   pl.BlockSpec((SUBLANES, LANES), lambda p, i: (0, 0))],
        out_shape=[jax.ShapeDtypeStruct((T, LANES), F32), jax.ShapeDtypeStruct((SUBLANES, LANES), F32)],
        scratch_shapes=[pltpu.VMEM((SUBLANES, LANES), F32)],
        compiler_params=_cparams(("arbitrary", "arbitrary"), 16),
    )(logits)


def _dispatch_kernel(dest_ref, h_hbm, xs_in, xs_hbm, sem):
    del xs_in
    tm = ROW_TILE
    base = pl.program_id(0) * tm

    def copy(tok, a):
        return pltpu.make_async_copy(h_hbm.at[pl.ds(tok, 1)], xs_hbm.at[pl.ds(dest_ref[a], 1)], sem)

    def issue(t, carry):
        for k in range(TOP_K):
            copy(base + t, (base + t) * TOP_K + k).start()
        return carry

    lax.fori_loop(0, tm, issue, 0, unroll=DMA_UNROLL)

    def wait(t, carry):
        for k in range(TOP_K):
            copy(base + t, (base + t) * TOP_K + k).wait()
        return carry

    lax.fori_loop(0, tm, wait, 0, unroll=DMA_UNROLL)


def _dispatch(dest, hpk, p_rows):
    T, dh = hpk.shape
    return pl.pallas_call(
        _dispatch_kernel,
        grid_spec=pltpu.PrefetchScalarGridSpec(
            num_scalar_prefetch=1,
            grid=(T // ROW_TILE,),
            in_specs=[pl.BlockSpec(memory_space=pl.ANY), pl.BlockSpec(memory_space=pl.ANY)],
            out_specs=pl.BlockSpec(memory_space=pl.ANY),
            scratch_shapes=[pltpu.SemaphoreType.DMA(())]),
        out_shape=jax.ShapeDtypeStruct((p_rows, dh), U32),
        input_output_aliases={2: 0},
        compiler_params=_cparams(("arbitrary",), 16),
    )(dest, hpk, jnp.zeros((p_rows, dh), U32))


def _moe_kernel(we_ref, ws_ref, wn_ref, wt_ref, xs_hbm, w1_ref, w3_ref, w2_ref, y_hbm,
                xbuf, xlo, xhi, yacc, ypk, w1b, w3b, w2b, sem_in, sem_out):
    w = pl.program_id(0)
    c = pl.program_id(1)
    nw = pl.num_programs(0)
    nc = pl.num_programs(1)
    nrows = wn_ref[w]
    start = ws_ref[w]
    nblk = nrows // MOE_RB
    dh = xbuf.shape[1]

    def blk_rows(rb):
        return pl.ds(pl.multiple_of(rb * MOE_RB, MOE_RB), MOE_RB)

    def hbm_rows(item_start, rb):
        return pl.ds(pl.multiple_of(item_start + rb * MOE_RB, MOE_RB), MOE_RB)

    def in_copy(item_start, rb):
        return pltpu.make_async_copy(xs_hbm.at[hbm_rows(item_start, rb)], xbuf.at[blk_rows(rb)], sem_in)

    def out_copy(item_start, rb):
        return pltpu.make_async_copy(ypk.at[blk_rows(rb)], y_hbm.at[hbm_rows(item_start, rb)], sem_out)

    def each_block(n, fn):
        def body(rb, carry):
            fn(rb)
            return carry
        lax.fori_loop(0, n, body, 0)

    @pl.when(c == 0)
    def _load():
        @pl.when(w == 0)
        def _():
            each_block(nblk, lambda rb: in_copy(start, rb).start())

        each_block(nblk, lambda rb: in_copy(start, rb).wait())

        def unpack(rb):
            lo, hi = _unpack_halves(xbuf[blk_rows(rb), :])
            xlo[blk_rows(rb), :] = lo.astype(BF16)
            xhi[blk_rows(rb), :] = hi.astype(BF16)
            yacc[blk_rows(rb), :] = jnp.zeros((MOE_RB, 2 * dh), F32)

        each_block(nblk, unpack)

        @pl.when(w + 1 < nw)
        def _():
            nxt = ws_ref[w + 1]
            each_block(wn_ref[w + 1] // MOE_RB, lambda rb: in_copy(nxt, rb).start())

    @pl.when(nrows > 0)
    def _compute():
        w1b[...] = w1_ref[...].astype(BF16)
        w3b[...] = w3_ref[...].astype(BF16)
        w2b[...] = w2_ref[...].astype(BF16)

        def blk(rb):
            rows = blk_rows(rb)
            xl = xlo[rows, :]
            xh = xhi[rows, :]
            h1 = _dot(xl, w1b[:dh, :]) + _dot(xh, w1b[dh:, :])
            h3 = _dot(xl, w3b[:dh, :]) + _dot(xh, w3b[dh:, :])
            hid = (h1 * jax.nn.sigmoid(h1) * h3).astype(BF16)
            yacc[rows, :] = yacc[rows, :] + _dot(hid, w2b[...])

        each_block(nblk, blk)

    @pl.when(c == nc - 1)
    def _store():
        @pl.when(w > 0)
        def _():
            prev = ws_ref[w - 1]
            each_block(wn_ref[w - 1] // MOE_RB, lambda rb: out_copy(prev, rb).wait())

        def pack(rb):
            ypk[blk_rows(rb), :] = _pack_halves(yacc[blk_rows(rb), :])

        each_block(nblk, pack)
        each_block(nblk, lambda rb: out_copy(start, rb).start())

        @pl.when(w == nw - 1)
        def _():
            each_block(nblk, lambda rb: out_copy(start, rb).wait())

    @pl.when((w == nw - 1) & (c == nc - 1))
    def _zero_tail():
        ypk[blk_rows(0), :] = jnp.zeros((MOE_RB, dh), U32)
        first = wt_ref[0] // MOE_RB

        def tail_copy(b):
            return pltpu.make_async_copy(ypk.at[blk_rows(0)],
                                         y_hbm.at[pl.ds(pl.multiple_of(b * MOE_RB, MOE_RB), MOE_RB)], sem_out)

        def start(b, carry):
            tail_copy(b).start()
            return carry

        def wait(b, carry):
            tail_copy(b).wait()
            return carry

        lax.fori_loop(first, y_hbm.shape[0] // MOE_RB, start, 0)
        lax.fori_loop(first, y_hbm.shape[0] // MOE_RB, wait, 0)


def _moe(we, ws, wn, wt, xs, w1, w3, w2):
    P, dh = xs.shape
    D = 2 * dh
    NE, _, DE = w1.shape
    cw = min(MOE_CW, DE)
    nc = DE // cw
    nw = we.shape[0]
    return pl.pallas_call(
        _moe_kernel,
        grid_spec=pltpu.PrefetchScalarGridSpec(
            num_scalar_prefetch=4,
            grid=(nw, nc),
            in_specs=[pl.BlockSpec(memory_space=pl.ANY),
                      pl.BlockSpec((None, D, cw), lambda w, c, we, ws, wn, wt: (we[w],0, c)),
                      pl.BlockSpec((None, D, cw), lambda w, c, we, ws, wn, wt: (we[w],0, c)),
                      pl.BlockSpec((None, cw, D), lambda w, c, we, ws, wn, wt: (we[w],c, 0))],
            out_specs=pl.BlockSpec(memory_space=pl.ANY),
            scratch_shapes=[pltpu.VMEM((MOE_RMAX, dh), U32),
                            pltpu.VMEM((MOE_RMAX, dh), BF16),
                            pltpu.VMEM((MOE_RMAX, dh), BF16),
                            pltpu.VMEM((MOE_RMAX, D), F32),
                            pltpu.VMEM((MOE_RMAX, dh), U32),
                            pltpu.VMEM((D, cw), BF16),
                            pltpu.VMEM((D, cw), BF16),
                            pltpu.VMEM((cw, D), BF16),
                            pltpu.SemaphoreType.DMA(()),
                            pltpu.SemaphoreType.DMA(())]),
        out_shape=jax.ShapeDtypeStruct((P, dh), U32),
        compiler_params=_cparams(("arbitrary", "arbitrary"), 56),
    )(we, ws, wn, wt, xs, w1, w3, w2)


def _moe_schedule(info, NE, p_rows):
    padded = info[1, :NE].astype(I32)
    start_pad = info[2, :NE].astype(I32)
    items = (padded + MOE_RMAX - 1) // MOE_RMAX
    cum_items = jnp.cumsum(items)
    n_items = cum_items[-1]
    nw = (p_rows + NE * (MOE_RMAX - MOE_RB)) // MOE_RMAX
    wi = jnp.arange(nw, dtype=I32)
    valid = wi < n_items
    wi_c = jnp.minimum(wi, jnp.maximum(n_items - 1, 0))
    we = jnp.minimum(jnp.searchsorted(cum_items, wi_c, side='right'), NE - 1).astype(I32)
    local = wi_c - (cum_items[we] - items[we])
    ws = (start_pad[we] + local * MOE_RMAX).astype(I32)
    wn = jnp.where(valid, jnp.clip(padded[we] - local * MOE_RMAX, 0, MOE_RMAX), 0).astype(I32)
    wt = (start_pad[NE - 1] + padded[NE - 1]).reshape(1)
    return we, ws, wn, wt


def _combine_kernel(dest_ref, x1_ref, rt_ref, gt_ref, y_hbm, o_ref, ybuf, sems):
    tm = ROW_TILE
    i = pl.program_id(0)
    n = pl.num_programs(0)
    dh = ybuf.shape[-1]

    def copy(tile, slot, t, k):
        a = (tile * tm + t) * TOP_K + k
        return pltpu.make_async_copy(y_hbm.at[pl.ds(dest_ref[a], 1)], ybuf.at[slot, k, pl.ds(t, 1)], sems.at[slot])

    def issue_tile(tile, slot):
        def body(t, carry):
            for k in range(TOP_K):
                copy(tile, slot, t, k).start()
            return carry
        lax.fori_loop(0, tm, body, 0, unroll=DMA_UNROLL)

    @pl.when(i == 0)
    def _():
        issue_tile(0, 0)

    @pl.when(i + 1 < n)
    def _():
        issue_tile(i + 1, (i + 1) % 2)

    slot = i % 2

    def wait_body(t, carry):
        for k in range(TOP_K):
            copy(i, slot, t, k).wait()
        return carry

    lax.fori_loop(0, tm, wait_body, 0, unroll=DMA_UNROLL)

    rt = rt_ref[...]
    g1 = rt[:, 2:3]
    g2 = rt[:, 3:4]
    lo1, hi1 = _unpack_halves(ybuf[slot, 0])
    lo2, hi2 = _unpack_halves(ybuf[slot, 1])
    o_ref[:, :dh] = x1_ref[:, :dh] + gt_ref[:, :dh] * (g1 * lo1 + g2 * lo2)
    o_ref[:, dh:] = x1_ref[:, dh:] + gt_ref[:, dh:] * (g1 * hi1 + g2 * hi2)


def _combine(dest, x1, route, gt2, ypk, S):
    T, D = x1.shape
    tm = ROW_TILE
    per_b = S // tm
    return pl.pallas_call(
        _combine_kernel,
        grid_spec=pltpu.PrefetchScalarGridSpec(
            num_scalar_prefetch=1,
            grid=(T // tm,),
            in_specs=[pl.BlockSpec((tm, D), lambda i, d: (i, 0)),
                      pl.BlockSpec((tm, LANES), lambda i, d: (i, 0)),
                      pl.BlockSpec((None, 1, D), lambda i, d: (i // per_b, 0, 0)),
                      pl.BlockSpec(memory_space=pl.ANY)],
            out_specs=pl.BlockSpec((tm, D), lambda i, d: (i, 0)),
            scratch_shapes=[pltpu.VMEM((2, TOP_K, tm, D // 2), U32),
                            pltpu.SemaphoreType.DMA((2,))]),
        out_shape=jax.ShapeDtypeStruct((T, D), F32),
        compiler_params=_cparams(("arbitrary",), 32),
    )(dest, x1, route, gt2, ypk)


def kernel(x, c, w_ada, b_ada, norm1_g, norm2_g, w_in, qn_g, kn_g, rel_bias, conv_w, A_log, dt_bias,
           onorm_g, w_out, w_rg, b_rg, w_re, b_re, w1, w3, w2):
    B, S, D = x.shape
    depth = w_ada.shape[0]
    HA, hda = rel_bias.shape[1], qn_g.shape[-1]
    HB, hdb = A_log.shape[-1], onorm_g.shape[-1]
    DA, DB = HA * hda, HB * hdb
    NG, NE = w_rg.shape[-1], w_re.shape[-1]
    T = B * S
    assert hdb == LANES and 2 * hda == LANES and DA + DB == D and 16 * HB <= LANES
    assert NE + NG <= LANES and conv_w.shape[1] == 5 and S % ROW_TILE == 0
    n_main = 3 * DA + 4 * DB
    p_rows = TOP_K * T + NE * MOE_RB

    bias_tab = _attn_bias_table(rel_bias, S)
    x2 = x.reshape(T, D)
    for l in range(depth):
        mod = _ada(c, w_ada[l], b_ada[l]).reshape(B, 6, 1, D)
        sh1, sc1, gt1, sh2, sc2, gt2 = (mod[:, i] for i in range(6))

        w_main = w_in[l][:, :n_main].astype(BF16)
        w_gate = jnp.tile(w_in[l][:, n_main:], (1, 4))
        w_gate = jnp.pad(w_gate, ((0, 0), (0, LANES - w_gate.shape[1])))
        proj, gat = _inproj(x2, norm1_g[l].reshape(1, D), sc1, sh1, w_main, w_gate, S)

        def gate_row(p):
            grp = jnp.concatenate([p.reshape(-1), jnp.zeros((2 * HB,), F32)])
            return jnp.pad(jnp.tile(grp, 4), (0, LANES - 16 * HB)).reshape(1, LANES)

        pack = _gating(gat, gate_row(A_log[l]), gate_row(dt_bias[l]), HB)

        oa = _attention(proj, jnp.tile(qn_g[l], 2).reshape(1, LANES), jnp.tile(kn_g[l], 2).reshape(1, LANES),
                        bias_tab, B, S, HA, hda)
        cw4 = jnp.transpose(conv_w[l].reshape(conv_w.shape[1], 3, HB, hdb), (2, 1, 0, 3))
        ob = _gdn(proj, cw4, pack.reshape(B, S, LANES), onorm_g[l].reshape(1, LANES), B, S, HB, 3 * DA // LANES)

        wr = jnp.pad(jnp.concatenate([w_re[l], w_rg[l]], axis=1), ((0, 0), (0, LANES - NE - NG)))
        br = jnp.pad(jnp.concatenate([b_re[l], b_rg[l]]), (0, LANES - NE - NG)).reshape(1, LANES)
        x1, hpk, logits = _outproj(oa, ob, x2, gt1, w_out[l].astype(BF16), norm2_g[l].reshape(1, D),
                                   sc2, sh2, wr, br, S)
        route, info = _route(logits, NE, NG)
        dest = route[:, 4:4 + TOP_K].astype(I32).reshape(TOP_K * T)
        we, ws, wn, wt = _moe_schedule(info, NE, p_rows)
        xs = _dispatch(dest, hpk, p_rows)
        ypk = _moe(we, ws, wn, wt, xs, w1[l], w3[l], w2[l])
        x2 = _combine(dest, x1, route, gt2, ypk, S)
    return x2.reshape(B, S, D)
```

```python
import functools

import numpy as np
import jax
import jax.numpy as jnp
from jax import lax
from jax.experimental import pallas as pl
from jax.experimental.pallas import tpu as pltpu

F32 = jnp.float32
BF16 = jnp.bfloat16
I32 = jnp.int32
U32 = jnp.uint32
HIGHEST = lax.Precision.HIGHEST

EPS = 1e-6
NEG = -1e30
DILATED_BRANCHES = ((128, 1), (512, 4), (2048, 16))
REL_MAX_DIST = 1024
CHUNK = 64
TOP_K = 2

LANES = 128
SUBLANES = 8
MIB = 1 << 20

ATT_QB = 128
ATT_KW = 256
ATT_UNROLL = 4
ATT_PROF_W = 512
GDN_HP = 2
GDN_PREP_UNROLL = 4
GDN_SEG = 512
MOE_RB = 256
MOE_RMAX = 1024
MOE_CW = 256
ROW_TILE = 256
DMA_UNROLL = 8


def _cparams(sem, vmem_mib):
    return pltpu.CompilerParams(dimension_semantics=sem, vmem_limit_bytes=vmem_mib * MIB)


def _dot(a, b, **kw):
    return jnp.dot(a, b, preferred_element_type=F32, **kw)


def _dot_nt(a, b):
    return lax.dot_general(a, b, (((1,), (1,)), ((), ())), preferred_element_type=F32)


def _pick_tile(n, prefs):
    for t in prefs:
        if n % t == 0:
            return t
    return n


def _pack_halves(x):
    half = x.shape[1] // 2
    lo = lax.bitcast_convert_type(x[:, :half].astype(BF16).astype(F32), U32)
    hi = lax.bitcast_convert_type(x[:, half:].astype(BF16).astype(F32), U32)
    return lax.shift_right_logical(lo, jnp.uint32(16)) | (hi & jnp.uint32(0xFFFF0000))


def _hilo_weights(w):
    hi = w.astype(BF16)
    lo = (w - hi.astype(F32)).astype(BF16)
    return jnp.concatenate([hi, lo], axis=1)


def _dot_hilo(x, w2):
    m, n = x.shape[0], w2.shape[1] // 2
    hi = x.astype(BF16)
    lo = (x - hi.astype(F32)).astype(BF16)
    r = _dot(jnp.concatenate([hi, lo], axis=0), w2)
    return r[:m, :n] + (r[:m, n:] + r[m:, :n])


def _unpack_halves(p):
    lo = lax.bitcast_convert_type(lax.shift_left(p, jnp.uint32(16)), F32)
    hi = lax.bitcast_convert_type(p & jnp.uint32(0xFFFF0000), F32)
    return lo, hi


def _ada_kernel(c_ref, w_ref, b_ref, o_ref):
    c = c_ref[...]
    s = (c * jax.nn.sigmoid(c)).astype(BF16)
    o_ref[...] = _dot(s, w_ref[...].astype(BF16)) + b_ref[...]


def _ada(c, w_ada, b_ada):
    B, D = c.shape
    N = w_ada.shape[1]
    tn = _pick_tile(N, (1024, 512, 256, 128))
    return pl.pallas_call(
        _ada_kernel,
        grid=(N // tn,),
        in_specs=[pl.BlockSpec((B, D), lambda j: (0, 0)),
                  pl.BlockSpec((D, tn), lambda j: (0, j)),
                  pl.BlockSpec((1, tn), lambda j: (0, j))],
        out_specs=pl.BlockSpec((B, tn), lambda j: (0, j)),
        out_shape=jax.ShapeDtypeStruct((B, N), F32),
        compiler_params=_cparams(("arbitrary",), 40),
    )(c, w_ada, b_ada.reshape(1, N))


def _inproj_kernel(x_ref, g_ref, sc_ref, sh_ref, w_ref, wg_ref, o_ref, og_ref, h_scr):
    @pl.when(pl.program_id(1) == 0)
    def _():
        x = x_ref[...]
        y = x * lax.rsqrt(jnp.mean(x * x, axis=-1, keepdims=True) + EPS) * g_ref[...]
        h = y * (1.0 + sc_ref[...]) + sh_ref[...]
        h_scr[...] = h.astype(BF16)
        og_ref[...] = _dot_hilo(h, wg_ref[...])

    o_ref[...] = _dot(h_scr[...], w_ref[...]).astype(o_ref.dtype)


def _inproj(x2, g, sc, sh, w_all, n_main, w_gate, S):
    T, D = x2.shape
    NM = n_main
    tm = _pick_tile(S, (512, 256, 128))
    tn = _pick_tile(NM, (1024, 512, 256, 128))
    per_b = S // tm
    return pl.pallas_call(
        _inproj_kernel,
        grid=(T // tm, NM // tn),
        in_specs=[pl.BlockSpec((tm, D), lambda i, j: (i, 0)),
                  pl.BlockSpec((1, D), lambda i, j: (0, 0)),
                  pl.BlockSpec((None, 1, D), lambda i, j: (i // per_b, 0, 0)),
                  pl.BlockSpec((None, 1, D), lambda i, j: (i // per_b, 0, 0)),
                  pl.BlockSpec((D, tn), lambda i, j: (0, j)),
                  pl.BlockSpec((D, 2 * LANES), lambda i, j: (0, 0))],
        out_specs=[pl.BlockSpec((tm, tn), lambda i, j: (i, j)),
                   pl.BlockSpec((tm, LANES), lambda i, j: (i, 0))],
        out_shape=[jax.ShapeDtypeStruct((T, NM), BF16), jax.ShapeDtypeStruct((T, LANES), F32)],
        scratch_shapes=[pltpu.VMEM((tm, D), BF16)],
        compiler_params=_cparams(("arbitrary", "arbitrary"), 48),
    )(x2, g, sc, sh, w_all, w_gate)


def _gating_kernel(gat_ref, a_ref, dt_ref, o_ref, *, hb):
    gat = gat_ref[...]
    tm = gat.shape[0]
    gw = 4 * hb
    g = -jnp.exp(a_ref[...]) * jax.nn.softplus(gat + dt_ref[...])
    beta = jax.nn.sigmoid(gat)
    r = lax.broadcasted_iota(I32, (tm, tm), 0)
    c = lax.broadcasted_iota(I32, (tm, tm), 1)
    same = (r // CHUNK) == (c // CHUNK)
    pre = _dot(jnp.where(same & (c <= r), 1.0, 0.0), g, precision=HIGHEST)
    suf = _dot(jnp.where(same & (c >= r), 1.0, 0.0), g, precision=HIGHEST)
    lane = lax.broadcasted_iota(I32, gat.shape, 1)
    o_ref[...] = jnp.where(lane // gw == 0, jnp.where(lane % gw < hb, pre, suf), beta)


def _gating(gat, a_row, dt_row, hb):
    T = gat.shape[0]
    tm = ROW_TILE
    return pl.pallas_call(
        functools.partial(_gating_kernel, hb=hb),
        grid=(T // tm,),
        in_specs=[pl.BlockSpec((tm, LANES), lambda i: (i, 0)),
                  pl.BlockSpec((1, LANES), lambda i: (0, 0)),
                  pl.BlockSpec((1, LANES), lambda i: (0, 0))],
        out_specs=pl.BlockSpec((tm, LANES), lambda i: (i, 0)),
        out_shape=jax.ShapeDtypeStruct((T, LANES), F32),
        compiler_params=_cparams(("arbitrary",), 16),
    )(gat, a_row, dt_row)


def _t5_bucket(rel, n_buckets):
    half = n_buckets // 2
    max_exact = half // 2
    n = np.abs(rel)
    large = max_exact + (np.log(np.maximum(n, 1) / max_exact) / np.log(REL_MAX_DIST / max_exact)
                         * (half - max_exact)).astype(np.int32)
    large = np.minimum(large, half - 1)
    return (np.where(rel > 0, half, 0) + np.where(n < max_exact, n, large)).astype(np.int32)


def _attn_plan(S):
    plan, base = [], 0
    for window, dil in DILATED_BRANCHES:
        n = window // (2 * dil)
        L = S // dil
        assert L % ATT_QB == 0 and n * 2 == ATT_QB
        kw = min(ATT_KW, L)
        nbq = L // ATT_QB
        nvar = 1 if nbq == 1 else 3
        plan.append((dil, L, nbq, kw, base, nvar, n))
        base += nvar
    return tuple(plan), base


def _attn_bias_profiles(rel_bias, S):
    plan, nvar_total = _attn_plan(S)
    nbuckets, H = rel_bias.shape
    u = np.arange(ATT_PROF_W) - ATT_QB
    onehots, bands = [], []
    for dil, L, nbq, kw, base, nvar, n in plan:
        offs = [0] if nvar == 1 else [0, -n, -(kw - ATT_QB)]
        for off in offs:
            rel = off + u
            onehots.append(np.eye(nbuckets, dtype=np.float32)[_t5_bucket(rel * dil, nbuckets)])
            bands.append(np.abs(rel) <= n)
    onehot = jnp.asarray(np.stack(onehots))
    band = jnp.asarray(np.stack(bands))
    prof = jnp.einsum('vwn,nh->hvw', onehot, rel_bias.astype(F32), precision=HIGHEST)
    prof = jnp.where(band[None], prof, NEG)
    prof = prof.reshape(H // 2, 2, nvar_total, ATT_PROF_W)
    return jnp.transpose(prof, (0, 2, 1, 3)).reshape(H // 2, 2 * nvar_total, ATT_PROF_W)


def _attn_kernel(q_ref, k_ref, v_ref, qg_ref, kg_ref, prof_ref, o_ref,
                 qn_scr, kn_scr, v_scr, ob_scr, mb_scr, db_scr, bias_ref, *, plan, hd):
    S = q_ref.shape[0]
    lane = lax.broadcasted_iota(I32, (1, LANES), 1)
    left = lane < hd

    @pl.when(pl.program_id(1) == 0)
    def _():
        for row in range(prof_ref.shape[0]):
            rep = jnp.broadcast_to(prof_ref[row:row + 1, :], (ATT_QB, ATT_PROF_W))
            skew = pltpu.roll(rep, 0, 1, stride=1, stride_axis=0)
            bias_ref[row // 2, row % 2] = skew[:, ATT_QB:ATT_QB + ATT_KW]

    def headnorm(x, g):
        x2 = x * x
        s_all = jnp.sum(x2, axis=-1, keepdims=True)
        s_left = jnp.sum(jnp.where(left, x2, 0.0), axis=-1, keepdims=True)
        ms = jnp.where(left, s_left, s_all - s_left) * (1.0 / hd)
        return x * lax.rsqrt(ms + EPS) * g

    qn_scr[...] = headnorm(q_ref[...].astype(F32), qg_ref[...]) * (hd ** -0.5)
    kn_scr[...] = headnorm(k_ref[...].astype(F32), kg_ref[...])
    v_scr[...] = v_ref[...].astype(F32)

    for bi, (dil, L, nbq, kw, base, nvar, n) in enumerate(plan):
        ones = jnp.ones((kw, LANES), BF16)

        def body(t, carry, dil=dil, L=L, nbq=nbq, kw=kw, base=base, nvar=nvar, n=n, bi=bi, ones=ones):
            blocks = []
            for uu in range(ATT_UNROLL):
                idx = t * ATT_UNROLL + uu
                r = idx // nbq
                i = idx % nbq
                q0 = i * ATT_QB
                k0 = jnp.clip(q0 - n, 0, L - kw)
                var = base if nvar == 1 else base + jnp.where(i > 0, 1, 0) + jnp.where(i == nbq - 1, 1, 0)
                if dil == 1:
                    qrows = pl.ds(pl.multiple_of(q0, ATT_QB), ATT_QB)
                    krows = pl.ds(pl.multiple_of(k0, CHUNK), kw)
                else:
                    qrows = pl.ds(r + q0 * dil, ATT_QB, stride=dil)
                    krows = pl.ds(r + k0 * dil, kw, stride=dil)
                qb = qn_scr[qrows, :]
                q2 = jnp.concatenate([jnp.where(left, qb, 0.0), jnp.where(left, 0.0, qb)], axis=0).astype(BF16)
                blocks.append((qrows, krows, var, q2))
            scores = [_dot_nt(q2, kn_scr[krows, :].astype(BF16)) for qrows, krows, var, q2 in blocks]
            probs, maxes = [], []
            for (qrows, krows, var, q2), s in zip(blocks, scores):
                s = s + jnp.concatenate([bias_ref[var, 0][:, :kw], bias_ref[var, 1][:, :kw]], axis=0)
                m = jnp.max(s, axis=-1, keepdims=True)
                probs.append(jnp.exp(s - m).astype(BF16))
                maxes.append(m)
            outs = [_dot(p, jnp.concatenate([v_scr[krows, :].astype(BF16), ones], axis=1))
                    for (qrows, krows, var, q2), p in zip(blocks, probs)]
            for (qrows, krows, var, q2), m, od in zip(blocks, maxes, outs):
                mb = jnp.broadcast_to(m, (2 * ATT_QB, LANES))
                ob_scr[bi, qrows, :] = jnp.where(left, od[:ATT_QB, :LANES], od[ATT_QB:, :LANES])
                mb_scr[bi, qrows, :] = jnp.where(left, mb[:ATT_QB], mb[ATT_QB:])
                db_scr[bi, qrows, :] = jnp.where(left, od[:ATT_QB, LANES:], od[ATT_QB:, LANES:])
            return carry

        assert (dil * nbq) % ATT_UNROLL == 0
        lax.fori_loop(0, dil * nbq // ATT_UNROLL, body, 0)

    nb = len(plan)
    mx = mb_scr[0]
    for bi in range(1, nb):
        mx = jnp.maximum(mx, mb_scr[bi])
    num = jnp.zeros((S, LANES), F32)
    den = jnp.zeros((S, LANES), F32)
    for bi in range(nb):
        w = jnp.exp(mb_scr[bi] - mx)
        num = num + w * ob_scr[bi]
        den = den + w * db_scr[bi]
    o_ref[...] = (num / den).astype(o_ref.dtype)


def _attention(proj, qg2, kg2, profiles, B, S, HA, hd):
    T = proj.shape[0]
    pairs = HA // 2
    da_blocks = HA * hd // LANES
    plan, nvar = _attn_plan(S)
    return pl.pallas_call(
        functools.partial(_attn_kernel, plan=plan, hd=hd),
        grid=(pairs, B),
        in_specs=[pl.BlockSpec((S, LANES), lambda p, b: (b, p)),
                  pl.BlockSpec((S, LANES), lambda p, b: (b, da_blocks + p)),
                  pl.BlockSpec((S, LANES), lambda p, b: (b, 2 * da_blocks + p)),
                  pl.BlockSpec((1, LANES), lambda p, b: (0, 0)),
                  pl.BlockSpec((1, LANES), lambda p, b: (0, 0)),
                  pl.BlockSpec((None, 2 * nvar, ATT_PROF_W), lambda p, b: (p, 0, 0))],
        out_specs=pl.BlockSpec((S, LANES), lambda p, b: (b, p)),
        out_shape=jax.ShapeDtypeStruct((T, HA * hd), BF16),
        scratch_shapes=[pltpu.VMEM((S, LANES), F32)] * 3 + [pltpu.VMEM((len(plan), S, LANES), F32)] * 3
        + [pltpu.VMEM((nvar, 2, ATT_QB, ATT_KW), F32)],
        compiler_params=_cparams(("arbitrary", "arbitrary"), 40),
    )(proj, proj, proj, qg2, kg2, profiles)


def _gdn_kernel(q_ref, k_ref, v_ref, z_ref, cw_ref, pack_ref, og_ref, o_ref,
                q_scr, k_scr, v_scr, xpad_scr, pk2_scr, sel_scr, u_scr, wq_scr, at_scr, kdt_scr, et_scr, oacc_scr,
                *, hb, hp):
    S = q_ref.shape[0]
    P2 = 2 * CHUNK
    W2 = 2 * LANES
    npair = S // P2
    hg = pl.program_id(1)
    gw = 4 * hb
    dk = LANES

    pad = SUBLANES
    for slot in range(2):
        xpad_scr[slot, 0:pad, :] = jnp.zeros((pad, LANES), F32)
        xpad_scr[slot, pad + S:, :] = jnp.zeros((pad, LANES), F32)

    seg = GDN_SEG if S % GDN_SEG == 0 else S

    def conv_silu_to(src_ref, lanes_j, j, which, dst_scr, l2norm, scale):
        xp = xpad_scr.at[(3 * j + which) % 2]
        for s0 in range(0, S, seg):
            xp[pad + s0:pad + s0 + seg, :] = src_ref[s0:s0 + seg, lanes_j].astype(F32)
        for s0 in range(0, S, seg):
            acc = xp[pad + s0:pad + s0 + seg, :] * cw_ref[j, which, 2:3, :]
            for d in (-2, -1, 1, 2):
                acc = acc + xp[pad + s0 + d:pad + s0 + d + seg, :] * cw_ref[j, which, 2 + d:3 + d, :]
            y = acc * jax.nn.sigmoid(acc)
            if l2norm:
                y = y * (lax.rsqrt(jnp.sum(y * y, axis=-1, keepdims=True) + EPS) * scale)
            dst_scr[j, s0:s0 + seg, :] = y

    for s0 in range(0, S, seg):
        pk = pack_ref[s0:s0 + seg, :]
        p_hi = pk.astype(BF16)
        p_lo = (pk - p_hi.astype(F32)).astype(BF16)
        pk2_scr[s0:s0 + seg, :] = jnp.concatenate([p_hi, p_lo], axis=1)
    srow = lax.broadcasted_iota(I32, (W2, 4 * LANES), 0) % LANES
    scol = lax.broadcasted_iota(I32, (W2, 4 * LANES), 1) // LANES
    for j in range(hp):
        lanes_j = slice(j * LANES, (j + 1) * LANES)
        conv_silu_to(q_ref, lanes_j, j, 0, q_scr, True, dk ** -0.5)
        conv_silu_to(k_ref, lanes_j, j, 1, k_scr, True, 1.0)
        conv_silu_to(v_ref, lanes_j, j, 2, v_scr, False, 1.0)
        src = hg * hp + j + jnp.where(scol < 2, scol * hb, gw + 2 * hb + (scol - 2) * hb)
        sel_scr[j] = jnp.where(srow == src, 1.0, 0.0).astype(BF16)
        oacc_scr[j] = jnp.zeros((S, LANES), F32)

    ri = lax.broadcasted_iota(I32, (P2, W2), 0)
    cfull = lax.broadcasted_iota(I32, (P2, W2), 1)
    ci = cfull % LANES
    bwd = cfull >= LANES
    same = (ri // CHUNK) == (ci // CHUNK)
    bd16 = (ri // 16) == (ci // 16)
    ahead = jnp.where(bwd, ri - ci, ci - ri)
    incl = same & (ahead <= 0)
    strict = same & (ahead < 0)
    first = lax.broadcasted_iota(I32, (P2, 1), 0) < CHUNK
    zpair = jnp.zeros((P2, LANES), BF16)

    def bdiag(x):
        return jnp.concatenate([jnp.concatenate([x[:, :LANES], zpair], axis=1),
                                jnp.concatenate([zpair, x[:, LANES:]], axis=1)], axis=0)

    def mm(a, b):
        return _dot(a.astype(BF16), bdiag(b.astype(BF16)))

    U = GDN_PREP_UNROLL if npair % GDN_PREP_UNROLL == 0 else 1

    def prep(t, carry):
        cx = []
        chains = [(t * U + u, j) for u in range(U) for j in range(hp)]
        bcs = [_dot(pk2_scr[pl.ds(pl.multiple_of(m * P2, P2), P2), :], sel_scr[j]) for m, j in chains]
        for (m, j), bc in zip(chains, bcs):
            rows = pl.ds(pl.multiple_of(m * P2, P2), P2)
            kp = k_scr[j, rows, :]
            qp = q_scr[j, rows, :]
            vp = v_scr[j, rows, :]
            gc2 = bc[:, 0:W2]
            beta2 = bc[:, W2:2 * W2]
            gcf, gcb = gc2[:, :LANES], gc2[:, LANES:]
            tot2 = jnp.concatenate([jnp.where(first, gcf[CHUNK - 1:CHUNK], gcf[P2 - 1:P2]),
                                    jnp.where(first, gcb[0:1], gcb[CHUNK:CHUNK + 1])], axis=1)
            egc2 = jnp.exp(gc2)
            k2 = jnp.concatenate([kp, kp], axis=1)
            kb2 = k2 * beta2
            vb2 = jnp.concatenate([vp, vp], axis=1) * beta2
            kbe2 = kb2 * egc2
            cx.append(dict(
                m=m, j=j, rows=rows, vb2=vb2, kbe2=kbe2,
                qeb=(jnp.concatenate([qp, qp], axis=1) * egc2).astype(BF16),
                kd2=k2 * jnp.exp(tot2 - gc2),
                et=jnp.exp(tot2),
                dec2=jnp.exp(jnp.where(incl, gc2 - jnp.concatenate([gcf.T, gcb.T], axis=1), -jnp.inf)),
                stk=jnp.concatenate([kb2[:, :LANES], kb2[:, LANES:], qp], axis=0).astype(BF16),
                kpb=kp.astype(BF16)))

        g3s = [_dot_nt(c['stk'], c['kpb']) for c in cx]
        for c, g3 in zip(cx, g3s):
            lm = jnp.where(strict, jnp.concatenate([g3[:P2], g3[P2:2 * P2]], axis=1) * c['dec2'], 0.0)
            c['attn2'] = (jnp.concatenate([g3[2 * P2:], g3[2 * P2:]], axis=1) * c['dec2']).astype(BF16)
            c['lbd'] = jnp.where(bd16, lm, 0.0)
            c['loff'] = lm - c['lbd']
        nn = [-c['lbd'] for c in cx]
        pw = [mm(c['lbd'], c['lbd']) for c in cx]
        for rnd in range(3):
            prod = [mm(a, p) for a, p in zip(nn, pw)]
            nxt = [mm(p, p) for p in pw] if rnd < 2 else pw
            nn = [a + p + q for a, p, q in zip(nn, pw, prod)]
            pw = nxt
        mo = [c['loff'] + x for c, x in zip(cx, [mm(a, c['loff']) for a, c in zip(nn, cx)])]
        m2 = [mm(x, x) for x in mo]
        mn = [mm(x, a) for x, a in zip(mo, nn)]
        xo = [a - x - y for a, x, y in zip(nn, mo, mn)]
        mx = [mm(a, b) for a, b in zip(m2, xo)]
        toff = [a + b + q for a, b, q in zip(xo, m2, mx)]
        z2 = jnp.zeros((P2, W2), BF16)
        tws = []
        for c, tf in zip(cx, toff):
            vbb, kbb = c['vb2'].astype(BF16), c['kbe2'].astype(BF16)
            rhs = jnp.concatenate(
                [jnp.concatenate([vbb[:, :LANES], kbb[:, :LANES], z2], axis=1),
                 jnp.concatenate([z2, vbb[:, LANES:], kbb[:, LANES:]], axis=1)], axis=0)
            tws.append(_dot(tf.astype(BF16), rhs))
        for c, tw in zip(cx, tws):
            m, j, rows = c['m'], c['j'], c['rows']
            u2 = c['vb2'] + jnp.concatenate([tw[:, :LANES], tw[:, W2:W2 + LANES]], axis=1)
            w2 = (c['kbe2'] + jnp.concatenate([tw[:, LANES:W2], tw[:, W2 + LANES:]], axis=1)).astype(BF16)
            qeb = c['qeb']
            u_scr[j, rows, :] = u2
            wq_scr[j, pl.ds(pl.multiple_of(m * 2 * P2, 2 * P2), 2 * P2), :] = jnp.concatenate(
                [w2[:CHUNK], qeb[:CHUNK], w2[CHUNK:], qeb[CHUNK:]], axis=0)
            at_scr[j, rows, :] = c['attn2']
            kdt_scr[j, 0, :, rows] = c['kd2'][:, :LANES].T.astype(BF16)
            kdt_scr[j, 1, :, rows] = c['kd2'][:, LANES:].T.astype(BF16)
            et_scr[j, pl.ds(pl.multiple_of(m * 2 * SUBLANES, 2 * SUBLANES), 2 * SUBLANES), :] = jnp.concatenate(
                [c['et'][:SUBLANES], c['et'][CHUNK:CHUNK + SUBLANES]], axis=0)
        return carry

    lax.fori_loop(0, npair // U, prep, 0)

    zc = jnp.zeros((CHUNK, LANES), F32)
    zp = jnp.zeros((P2, LANES), F32)

    def place(v, cpos):
        return jnp.concatenate([v, zc] if cpos == 0 else [zc, v], axis=0)

    def scan(m, states):
        pf = m
        pb = npair - 1 - m
        rows_f = pl.ds(pl.multiple_of(pf * P2, P2), P2)
        rows_b = pl.ds(pl.multiple_of(pb * P2, P2), P2)
        hx = []
        for j in range(hp):
            hx.append(dict(
                u_f=u_scr[j, rows_f, :LANES], u_b=u_scr[j, rows_b, LANES:],
                at_f=at_scr[j, rows_f, :LANES], at_b=at_scr[j, rows_b, LANES:],
                kdt=jnp.concatenate([kdt_scr[j, 0, :, rows_f], kdt_scr[j, 1, :, rows_b]], axis=1),
                wq_f=wq_scr[j, pl.ds(pl.multiple_of(pf * 2 * P2, 2 * P2), 2 * P2), :LANES],
                wq_b=wq_scr[j, pl.ds(pl.multiple_of(pb * 2 * P2, 2 * P2), 2 * P2), LANES:],
                et_f=et_scr[j, pl.ds(pl.multiple_of(pf * 2 * SUBLANES, 2 * SUBLANES), 2 * SUBLANES), :LANES],
                et_b=et_scr[j, pl.ds(pl.multiple_of(pb * 2 * SUBLANES, 2 * SUBLANES), 2 * SUBLANES), LANES:]))
        sts = list(states)
        for step in range(2):
            cf, cb = step, 1 - step
            rrs = [_dot(jnp.concatenate([c['wq_f'][cf * P2:(cf + 1) * P2], c['wq_b'][cb * P2:(cb + 1) * P2]], axis=1),
                        bdiag(st.astype(BF16))) for c, st in zip(hx, sts)]
            ress = []
            for c, rr in zip(hx, rrs):
                u2 = jnp.concatenate([c['u_f'][cf * CHUNK:(cf + 1) * CHUNK],
                                      c['u_b'][cb * CHUNK:(cb + 1) * CHUNK]], axis=1)
                v_new = u2 - rr[:CHUNK]
                rhs = jnp.concatenate(
                    [jnp.concatenate([place(v_new[:, :LANES], cf), zp], axis=1),
                     jnp.concatenate([zp, place(v_new[:, LANES:], cb)], axis=1)], axis=0).astype(BF16)
                lhs = jnp.concatenate(
                    [jnp.concatenate([c['at_f'][cf * CHUNK:(cf + 1) * CHUNK],
                                      c['at_b'][cb * CHUNK:(cb + 1) * CHUNK]], axis=1),
                     c['kdt']], axis=0)
                ress.append(_dot(lhs, rhs))
            for j, (c, rr, res) in enumerate(zip(hx, rrs, ress)):
                o2 = rr[CHUNK:] + res[:CHUNK]
                et2 = jnp.concatenate([c['et_f'][cf * SUBLANES:cf * SUBLANES + 1],
                                       c['et_b'][cb * SUBLANES:cb * SUBLANES + 1]], axis=1)
                sts[j] = sts[j] * et2 + res[CHUNK:]
                of_rows = pl.ds(pl.multiple_of(pf * P2 + cf * CHUNK, CHUNK), CHUNK)
                ob_rows = pl.ds(pl.multiple_of(pb * P2 + cb * CHUNK, CHUNK), CHUNK)
                oacc_scr[j, of_rows, :] = oacc_scr[j, of_rows, :] + o2[:, :LANES]
                oacc_scr[j, ob_rows, :] = oacc_scr[j, ob_rows, :] + o2[:, LANES:]
        return tuple(sts)

    s0 = jnp.zeros((dk, W2), F32)
    lax.fori_loop(0, npair, scan, (s0,) * hp)

    for j in range(hp):
        lanes_j = slice(j * LANES, (j + 1) * LANES)
        for s0 in range(0, S, seg):
            o = oacc_scr[j, s0:s0 + seg, :]
            y = o * lax.rsqrt(jnp.mean(o * o, axis=-1, keepdims=True) + EPS) * og_ref[...]
            z = z_ref[s0:s0 + seg, lanes_j].astype(F32)
            o_ref[s0:s0 + seg, lanes_j] = (y * (z * jax.nn.sigmoid(z))).astype(o_ref.dtype)


def _gdn(proj, cw4, pack3, onorm_g, B, S, HB, base_blk):
    T = proj.shape[0]
    hp = GDN_HP if (HB % GDN_HP == 0 and base_blk % GDN_HP == 0) else 1
    wblk = hp * LANES
    npair = S // (2 * CHUNK)

    def col(k):
        off = (base_blk + k * HB) // hp
        return lambda b, h: (b, off + h)

    return pl.pallas_call(
        functools.partial(_gdn_kernel, hb=HB, hp=hp),
        grid=(B, HB // hp),
        in_specs=[pl.BlockSpec((S, wblk), col(0)),
                  pl.BlockSpec((S, wblk), col(1)),
                  pl.BlockSpec((S, wblk), col(2)),
                  pl.BlockSpec((S, wblk), col(3)),
                  pl.BlockSpec((hp, 3, cw4.shape[2], LANES), lambda b, h: (h, 0, 0, 0)),
                  pl.BlockSpec((None, S, LANES), lambda b, h: (b, 0, 0)),
                  pl.BlockSpec((1, LANES), lambda b, h: (0, 0))],
        out_specs=pl.BlockSpec((S, wblk), lambda b, h: (b, h)),
        out_shape=jax.ShapeDtypeStruct((T, HB * LANES), BF16),
        scratch_shapes=[pltpu.VMEM((hp, S, LANES), F32)] * 3
        + [pltpu.VMEM((2, S + 2 * SUBLANES, LANES), F32),
           pltpu.VMEM((S, 2 * LANES), BF16),
           pltpu.VMEM((hp, 2 * LANES, 4 * LANES), BF16),
           pltpu.VMEM((hp, S, 2 * LANES), F32),
           pltpu.VMEM((hp, 2 * S, 2 * LANES), BF16),
           pltpu.VMEM((hp, S, 2 * LANES), BF16),
           pltpu.VMEM((hp, 2, LANES, S), BF16),
           pltpu.VMEM((hp, npair * 2 * SUBLANES, 2 * LANES), F32),
           pltpu.VMEM((hp, S, LANES), F32)],
        compiler_params=_cparams(("arbitrary", "arbitrary"), 56),
    )(proj, proj, proj, proj, cw4, pack3, onorm_g)


def _outproj_kernel(oa_ref, ob_ref, x_ref, gt_ref, w_ref, g_ref, sc_ref, sh_ref, wr_ref, br_ref,
                    x1_ref, hp_ref, lg_ref, h_scr):
    da = oa_ref.shape[1]
    y = _dot(oa_ref[...], w_ref[:da, :]) + _dot(ob_ref[...], w_ref[da:, :])
    x1 = x_ref[...] + gt_ref[...] * y
    x1_ref[...] = x1
    hn = x1 * lax.rsqrt(jnp.mean(x1 * x1, axis=-1, keepdims=True) + EPS) * g_ref[...]
    h = hn * (1.0 + sc_ref[...]) + sh_ref[...]
    hp_ref[...] = _pack_halves(h)
    h_scr[...] = h

    @pl.when(pl.program_id(0) < pl.num_programs(0))
    def _():
        lg_ref[...] = _dot_hilo(h_scr[...], wr_ref[...]) + br_ref[...]


def _outproj(oa, ob, x2, gt1, w_out_b, g2, sc2, sh2, wr, br, S):
    T, D = x2.shape
    tm = ROW_TILE
    per_b = S // tm
    bmap = lambda i: (i // per_b, 0, 0)
    return pl.pallas_call(
        _outproj_kernel,
        grid=(T // tm,),
        in_specs=[pl.BlockSpec((tm, oa.shape[1]), lambda i: (i, 0)),
                  pl.BlockSpec((tm, ob.shape[1]), lambda i: (i, 0)),
                  pl.BlockSpec((tm, D), lambda i: (i, 0)),
                  pl.BlockSpec((None, 1, D), bmap),
                  pl.BlockSpec((D, D), lambda i: (0, 0)),
                  pl.BlockSpec((1, D), lambda i: (0, 0)),
                  pl.BlockSpec((None, 1, D), bmap),
                  pl.BlockSpec((None, 1, D), bmap),
                  pl.BlockSpec((D, 2 * LANES), lambda i: (0, 0)),
                  pl.BlockSpec((1, LANES), lambda i: (0, 0))],
        out_specs=[pl.BlockSpec((tm, D), lambda i: (i, 0)),
                   pl.BlockSpec((tm, D // 2), lambda i: (i, 0)),
                   pl.BlockSpec((tm, LANES), lambda i: (i, 0))],
        out_shape=[jax.ShapeDtypeStruct((T, D), F32), jax.ShapeDtypeStruct((T, D // 2), U32),
                   jax.ShapeDtypeStruct((T, LANES), F32)],
        scratch_shapes=[pltpu.VMEM((tm, D), F32)],
        compiler_params=_cparams(("arbitrary",), 48),
    )(oa, ob, x2, gt1, w_out_b, g2, sc2, sh2, wr, br)


def _route_kernel(lg_ref, o_ref, info_ref, run_scr, *, ne, ng):
    ph = pl.program_id(0)
    i = pl.program_id(1)

    @pl.when((ph == 0) & (i == 0))
    def _():
        run_scr[...] = jnp.zeros_like(run_scr)

    @pl.when((ph == 1) & (i == 0))
    def _():
        cnt = run_scr[...]
        padded = jnp.ceil(cnt * (1.0 / MOE_RB)) * MOE_RB
        k = lax.broadcasted_iota(I32, (LANES, LANES), 0)
        e = lax.broadcasted_iota(I32, (LANES, LANES), 1)
        start = _dot(padded, jnp.where(k < e, 1.0, 0.0), precision=HIGHEST)
        rowi = lax.broadcasted_iota(I32, cnt.shape, 0)
        info_ref[...] = jnp.where(rowi == 0, cnt, jnp.where(rowi == 1, padded, start))
        run_scr[...] = start

    lg = lg_ref[...]
    tm = lg.shape[0]
    epg = ne // ng
    lane_i = lax.broadcasted_iota(I32, lg.shape, 1)
    lane = lane_i.astype(F32)
    big = float(2 * LANES)
    is_g = (lane_i >= ne) & (lane_i < ne + ng)
    gl = jnp.where(is_g, lg, -jnp.inf)
    gmax = jnp.max(gl, axis=-1, keepdims=True)
    gidx = jnp.min(jnp.where(gl == gmax, lane, big), axis=-1, keepdims=True) - ne
    psel = 1.0 / jnp.sum(jnp.where(is_g, jnp.exp(gl - gmax), 0.0), axis=-1, keepdims=True)
    in_grp = (lane_i // epg).astype(F32) == gidx
    el = jnp.where(in_grp & (lane_i < ne), lg, -jnp.inf)
    m1 = jnp.max(el, axis=-1, keepdims=True)
    i1 = jnp.min(jnp.where(el == m1, lane, big), axis=-1, keepdims=True)
    el2 = jnp.where(lane == i1, -jnp.inf, el)
    m2 = jnp.max(el2, axis=-1, keepdims=True)
    i2 = jnp.min(jnp.where(el2 == m2, lane, big), axis=-1, keepdims=True)
    e21 = jnp.exp(m2 - m1)
    g1 = psel / (1.0 + e21)
    g2 = psel * e21 / (1.0 + e21)
    o1 = jnp.where(lane == i1, 1.0, 0.0)
    o2 = jnp.where(lane == i2, 1.0, 0.0)
    cnt = o1 + o2
    r = lax.broadcasted_iota(I32, (tm, tm), 0)
    c = lax.broadcasted_iota(I32, (tm, tm), 1)
    before = _dot(jnp.where(c < r, 1.0, 0.0).astype(BF16), cnt.astype(BF16)) + run_scr[0:1, :]
    d1 = jnp.sum(o1 * before, axis=-1, keepdims=True)
    d2 = jnp.sum(o2 * before, axis=-1, keepdims=True)
    run_scr[...] = run_scr[...] + jnp.sum(cnt, axis=0, keepdims=True)

    @pl.when(ph == 1)
    def _():
        out = jnp.zeros(lg.shape, F32)
        for j, val in enumerate((i1, i2, g1, g2, d1, d2)):
            out = jnp.where(lane_i == j, val, out)
        o_ref[...] = out


def _route(logits, ne, ng):
    T = logits.shape[0]
    tm = _pick_tile(T, (1024, 512, 256))
    return pl.pallas_call(
        functools.partial(_route_kernel, ne=ne, ng=ng),
        grid=(2, T // tm),
        in_specs=[pl.BlockSpec((tm, LANES), lambda p, i: (i, 0))],
        out_specs=[pl.BlockSpec((tm, LANES), lambda p, i: (i * p, 0)),
                   pl.BlockSpec((SUBLANES, LANES), lambda p, i: (0, 0))],
        out_shape=[jax.ShapeDtypeStruct((T, LANES), F32), jax.ShapeDtypeStruct((SUBLANES, LANES), F32)],
        scratch_shapes=[pltpu.VMEM((SUBLANES, LANES), F32)],
        compiler_params=_cparams(("arbitrary", "arbitrary"), 16),
    )(logits)


def _dispatch_kernel(dest_ref, h_hbm, xs_in, xs_hbm, sem):
    del xs_in
    tm = ROW_TILE
    base = pl.program_id(0) * tm

    def copy(tok, a):
        return pltpu.make_async_copy(h_hbm.at[pl.ds(tok, 1)], xs_hbm.at[pl.ds(dest_ref[a], 1)], sem)

    def issue(t, carry):
        for k in range(TOP_K):
            copy(base + t, (base + t) * TOP_K + k).start()
        return carry

    lax.fori_loop(0, tm, issue, 0, unroll=DMA_UNROLL)

    def wait(t, carry):
        for k in range(TOP_K):
            copy(base + t, (base + t) * TOP_K + k).wait()
        return carry

    lax.fori_loop(0, tm, wait, 0, unroll=DMA_UNROLL)


def _dispatch(dest, hpk, p_rows):
    T, dh = hpk.shape
    return pl.pallas_call(
        _dispatch_kernel,
        grid_spec=pltpu.PrefetchScalarGridSpec(
            num_scalar_prefetch=1,
            grid=(T // ROW_TILE,),
            in_specs=[pl.BlockSpec(memory_space=pl.ANY), pl.BlockSpec(memory_space=pl.ANY)],
            out_specs=pl.BlockSpec(memory_space=pl.ANY),
            scratch_shapes=[pltpu.SemaphoreType.DMA(())]),
        out_shape=jax.ShapeDtypeStruct((p_rows, dh), U32),
        input_output_aliases={2: 0},
        compiler_params=_cparams(("arbitrary",), 16),
    )(dest, hpk, jnp.zeros((p_rows, dh), U32))


def _moe_kernel(we_ref, ws_ref, wn_ref, wt_ref, xs_hbm, w1_ref, w3_ref, w2_ref, y_hbm,
                xbuf, xlo, xhi, yacc, ypk, sem_in, sem_out):
    w = pl.program_id(0)
    c = pl.program_id(1)
    nw = pl.num_programs(0)
    nc = pl.num_programs(1)
    nrows = wn_ref[w]
    start = ws_ref[w]
    nblk = nrows // MOE_RB
    dh = xbuf.shape[1]

    def blk_rows(rb):
        return pl.ds(pl.multiple_of(rb * MOE_RB, MOE_RB), MOE_RB)

    def hbm_rows(item_start, rb):
        return pl.ds(pl.multiple_of(item_start + rb * MOE_RB, MOE_RB), MOE_RB)

    def in_copy(item_start, rb):
        return pltpu.make_async_copy(xs_hbm.at[hbm_rows(item_start, rb)], xbuf.at[blk_rows(rb)], sem_in)

    def out_copy(item_start, rb):
        return pltpu.make_async_copy(ypk.at[blk_rows(rb)], y_hbm.at[hbm_rows(item_start, rb)], sem_out)

    def each_block(n, fn):
        def body(rb, carry):
            fn(rb)
            return carry
        lax.fori_loop(0, n, body, 0)

    @pl.when(c == 0)
    def _load():
        @pl.when(w == 0)
        def _():
            each_block(nblk, lambda rb: in_copy(start, rb).start())

        each_block(nblk, lambda rb: in_copy(start, rb).wait())

        def unpack(rb):
            lo, hi = _unpack_halves(xbuf[blk_rows(rb), :])
            xlo[blk_rows(rb), :] = lo.astype(BF16)
            xhi[blk_rows(rb), :] = hi.astype(BF16)
            yacc[blk_rows(rb), :] = jnp.zeros((MOE_RB, 2 * dh), F32)

        each_block(nblk, unpack)

        @pl.when(w + 1 < nw)
        def _():
            nxt = ws_ref[w + 1]
            each_block(wn_ref[w + 1] // MOE_RB, lambda rb: in_copy(nxt, rb).start())

    @pl.when(nrows > 0)
    def _compute():
        def blk(rb):
            rows = blk_rows(rb)
            xl = xlo[rows, :]
            xh = xhi[rows, :]
            h1 = _dot(xl, w1_ref[:dh, :].astype(BF16)) + _dot(xh, w1_ref[dh:, :].astype(BF16))
            h3 = _dot(xl, w3_ref[:dh, :].astype(BF16)) + _dot(xh, w3_ref[dh:, :].astype(BF16))
            hid = (h1 * jax.nn.sigmoid(h1) * h3).astype(BF16)
            yacc[rows, :] = yacc[rows, :] + _dot(hid, w2_ref[...].astype(BF16))

        each_block(nblk, blk)

    @pl.when(c == nc - 1)
    def _store():
        @pl.when(w > 0)
        def _():
            prev = ws_ref[w - 1]
            each_block(wn_ref[w - 1] // MOE_RB, lambda rb: out_copy(prev, rb).wait())

        def pack(rb):
            ypk[blk_rows(rb), :] = _pack_halves(yacc[blk_rows(rb), :])

        each_block(nblk, pack)
        each_block(nblk, lambda rb: out_copy(start, rb).start())

        @pl.when(w == nw - 1)
        def _():
            each_block(nblk, lambda rb: out_copy(start, rb).wait())

    @pl.when((w == nw - 1) & (c == nc - 1))
    def _zero_tail():
        ypk[blk_rows(0), :] = jnp.zeros((MOE_RB, dh), U32)
        first = wt_ref[0] // MOE_RB

        def tail_copy(b):
            return pltpu.make_async_copy(ypk.at[blk_rows(0)],
                                         y_hbm.at[pl.ds(pl.multiple_of(b * MOE_RB, MOE_RB), MOE_RB)], sem_out)

        def start(b, carry):
            tail_copy(b).start()
            return carry

        def wait(b, carry):
            tail_copy(b).wait()
            return carry

        lax.fori_loop(first, y_hbm.shape[0] // MOE_RB, start, 0)
        lax.fori_loop(first, y_hbm.shape[0] // MOE_RB, wait, 0)


def _moe(we, ws, wn, wt, xs, w1, w3, w2):
    P, dh = xs.shape
    D = 2 * dh
    NE, _, DE = w1.shape
    cw = min(MOE_CW, DE)
    nc = DE // cw
    nw = we.shape[0]
    return pl.pallas_call(
        _moe_kernel,
        grid_spec=pltpu.PrefetchScalarGridSpec(
            num_scalar_prefetch=4,
            grid=(nw, nc),
            in_specs=[pl.BlockSpec(memory_space=pl.ANY),
                      pl.BlockSpec((None, D, cw), lambda w, c, we, ws, wn, wt: (we[w],0, c)),
                      pl.BlockSpec((None, D, cw), lambda w, c, we, ws, wn, wt: (we[w],0, c)),
                      pl.BlockSpec((None, cw, D), lambda w, c, we, ws, wn, wt: (we[w],c, 0))],
            out_specs=pl.BlockSpec(memory_space=pl.ANY),
            scratch_shapes=[pltpu.VMEM((MOE_RMAX, dh), U32),
                            pltpu.VMEM((MOE_RMAX, dh), BF16),
                            pltpu.VMEM((MOE_RMAX, dh), BF16),
                            pltpu.VMEM((MOE_RMAX, D), F32),
                            pltpu.VMEM((MOE_RMAX, dh), U32),
                            pltpu.SemaphoreType.DMA(()),
                            pltpu.SemaphoreType.DMA(())]),
        out_shape=jax.ShapeDtypeStruct((P, dh), U32),
        compiler_params=_cparams(("arbitrary", "arbitrary"), 56),
    )(we, ws, wn, wt, xs, w1, w3, w2)


def _moe_schedule(info, NE, p_rows):
    padded = info[1, :NE].astype(I32)
    start_pad = info[2, :NE].astype(I32)
    items = (padded + MOE_RMAX - 1) // MOE_RMAX
    cum_items = jnp.cumsum(items)
    n_items = cum_items[-1]
    nw = (p_rows + NE * (MOE_RMAX - MOE_RB)) // MOE_RMAX
    wi = jnp.arange(nw, dtype=I32)
    valid = wi < n_items
    wi_c = jnp.minimum(wi, jnp.maximum(n_items - 1, 0))
    we = jnp.minimum(jnp.searchsorted(cum_items, wi_c, side='right'), NE - 1).astype(I32)
    local = wi_c - (cum_items[we] - items[we])
    ws = (start_pad[we] + local * MOE_RMAX).astype(I32)
    wn = jnp.where(valid, jnp.clip(padded[we] - local * MOE_RMAX, 0, MOE_RMAX), 0).astype(I32)
    wt = (start_pad[NE - 1] + padded[NE - 1]).reshape(1)
    return we, ws, wn, wt


def _combine_kernel(dest_ref, x1_ref, rt_ref, gt_ref, y_hbm, o_ref, ybuf, sems):
    tm = ROW_TILE
    i = pl.program_id(0)
    n = pl.num_programs(0)
    dh = ybuf.shape[-1]

    def copy(tile, slot, t, k):
        a = (tile * tm + t) * TOP_K + k
        return pltpu.make_async_copy(y_hbm.at[pl.ds(dest_ref[a], 1)], ybuf.at[slot, k, pl.ds(t, 1)], sems.at[slot])

    def issue_tile(tile, slot):
        def body(t, carry):
            for k in range(TOP_K):
                copy(tile, slot, t, k).start()
            return carry
        lax.fori_loop(0, tm, body, 0, unroll=DMA_UNROLL)

    @pl.when(i == 0)
    def _():
        issue_tile(0, 0)

    @pl.when(i + 1 < n)
    def _():
        issue_tile(i + 1, (i + 1) % 2)

    slot = i % 2

    def wait_body(t, carry):
        for k in range(TOP_K):
            copy(i, slot, t, k).wait()
        return carry

    lax.fori_loop(0, tm, wait_body, 0, unroll=DMA_UNROLL)

    rt = rt_ref[...]
    g1 = rt[:, 2:3]
    g2 = rt[:, 3:4]
    lo1, hi1 = _unpack_halves(ybuf[slot, 0])
    lo2, hi2 = _unpack_halves(ybuf[slot, 1])
    o_ref[:, :dh] = x1_ref[:, :dh] + gt_ref[:, :dh] * (g1 * lo1 + g2 * lo2)
    o_ref[:, dh:] = x1_ref[:, dh:] + gt_ref[:, dh:] * (g1 * hi1 + g2 * hi2)


def _combine(dest, x1, route, gt2, ypk, S):
    T, D = x1.shape
    tm = ROW_TILE
    per_b = S // tm
    return pl.pallas_call(
        _combine_kernel,
        grid_spec=pltpu.PrefetchScalarGridSpec(
            num_scalar_prefetch=1,
            grid=(T // tm,),
            in_specs=[pl.BlockSpec((tm, D), lambda i, d: (i, 0)),
                      pl.BlockSpec((tm, LANES), lambda i, d: (i, 0)),
                      pl.BlockSpec((None, 1, D), lambda i, d: (i // per_b, 0, 0)),
                      pl.BlockSpec(memory_space=pl.ANY)],
            out_specs=pl.BlockSpec((tm, D), lambda i, d: (i, 0)),
            scratch_shapes=[pltpu.VMEM((2, TOP_K, tm, D // 2), U32),
                            pltpu.SemaphoreType.DMA((2,))]),
        out_shape=jax.ShapeDtypeStruct((T, D), F32),
        compiler_params=_cparams(("arbitrary",), 32),
    )(dest, x1, route, gt2, ypk)


def kernel(x, c, w_ada, b_ada, norm1_g, norm2_g, w_in, qn_g, kn_g, rel_bias, conv_w, A_log, dt_bias,
           onorm_g, w_out, w_rg, b_rg, w_re, b_re, w1, w3, w2):
    B, S, D = x.shape
    depth = w_ada.shape[0]
    HA, hda = rel_bias.shape[1], qn_g.shape[-1]
    HB, hdb = A_log.shape[-1], onorm_g.shape[-1]
    DA, DB = HA * hda, HB * hdb
    NG, NE = w_rg.shape[-1], w_re.shape[-1]
    T = B * S
    assert hdb == LANES and 2 * hda == LANES and DA + DB == D and 16 * HB <= LANES
    assert NE + NG <= LANES and conv_w.shape[1] == 5 and S % ROW_TILE == 0
    n_main = 3 * DA + 4 * DB
    p_rows = TOP_K * T + NE * MOE_RB

    bias_prof = _attn_bias_profiles(rel_bias, S)
    w_in_b = w_in.astype(BF16)
    x2 = x.reshape(T, D)
    for l in range(depth):
        mod = _ada(c, w_ada[l], b_ada[l]).reshape(B, 6, 1, D)
        sh1, sc1, gt1, sh2, sc2, gt2 = (mod[:, i] for i in range(6))

        w_gate = jnp.tile(w_in[l][:, n_main:], (1, 4))
        w_gate = _hilo_weights(jnp.pad(w_gate, ((0, 0), (0, LANES - w_gate.shape[1]))))
        proj, gat = _inproj(x2, norm1_g[l].reshape(1, D), sc1, sh1, w_in_b[l], n_main, w_gate, S)

        def gate_row(p):
            grp = jnp.concatenate([p.reshape(-1), jnp.zeros((2 * HB,), F32)])
            return jnp.pad(jnp.tile(grp, 4), (0, LANES - 16 * HB)).reshape(1, LANES)

        pack = _gating(gat, gate_row(A_log[l]), gate_row(dt_bias[l]), HB)

        oa = _attention(proj, jnp.tile(qn_g[l], 2).reshape(1, LANES), jnp.tile(kn_g[l], 2).reshape(1, LANES),
                        bias_prof, B, S, HA, hda)
        cw4 = jnp.transpose(conv_w[l].reshape(conv_w.shape[1], 3, HB, hdb), (2, 1, 0, 3))
        ob = _gdn(proj, cw4, pack.reshape(B, S, LANES), onorm_g[l].reshape(1, LANES), B, S, HB, 3 * DA // LANES)

        wr = _hilo_weights(jnp.pad(jnp.concatenate([w_re[l], w_rg[l]], axis=1), ((0, 0), (0, LANES - NE - NG))))
        br = jnp.pad(jnp.concatenate([b_re[l], b_rg[l]]), (0, LANES - NE - NG)).reshape(1, LANES)
        x1, hpk, logits = _outproj(oa, ob, x2, gt1, w_out[l].astype(BF16), norm2_g[l].reshape(1, D),
                                   sc2, sh2, wr, br, S)
        route, info = _route(logits, NE, NG)
        dest = route[:, 4:4 + TOP_K].astype(I32).reshape(TOP_K * T)
        we, ws, wn, wt = _moe_schedule(info, NE, p_rows)
        xs = _dispatch(dest, hpk, p_rows)
        ypk = _moe(we, ws, wn, wt, xs, w1[l], w3[l], w2[l])
        x2 = _combine(dest, x1, route, gt2, ypk, S)
    return x2.reshape(B, S, D)
```

```python
import functools

import numpy as np
import jax
import jax.numpy as jnp
from jax import lax
from jax.experimental import pallas as pl
from jax.experimental.pallas import tpu as pltpu

F32 = jnp.float32
BF16 = jnp.bfloat16
I32 = jnp.int32
U32 = jnp.uint32
HIGHEST = lax.Precision.HIGHEST

EPS = 1e-6
NEG = -1e30
DILATED_BRANCHES = ((128, 1), (512, 4), (2048, 16))
REL_MAX_DIST = 1024
CHUNK = 64
TOP_K = 2

LANES = 128
SUBLANES = 8
MIB = 1 << 20

ATT_QB = 128
ATT_KW = 256
ATT_UNROLL = 4
ATT_PROF_W = 512
GDN_HP = 2
GDN_PREP_UNROLL = 4
GDN_SEG = 512
MOE_RB = 256
MOE_RMAX = 1024
MOE_CW = 512
ROW_TILE = 256
DMA_UNROLL = 8


def _cparams(sem, vmem_mib):
    return pltpu.CompilerParams(dimension_semantics=sem, vmem_limit_bytes=vmem_mib * MIB)


def _dot(a, b, **kw):
    return jnp.dot(a, b, preferred_element_type=F32, **kw)


def _dot_nt(a, b):
    return lax.dot_general(a, b, (((1,), (1,)), ((), ())), preferred_element_type=F32)


def _pick_tile(n, prefs):
    for t in prefs:
        if n % t == 0:
            return t
    return n


def _pack_halves(x):
    half = x.shape[1] // 2
    lo = lax.bitcast_convert_type(x[:, :half].astype(BF16).astype(F32), U32)
    hi = lax.bitcast_convert_type(x[:, half:].astype(BF16).astype(F32), U32)
    return lax.shift_right_logical(lo, jnp.uint32(16)) | (hi & jnp.uint32(0xFFFF0000))


def _hilo_weights(w):
    hi = w.astype(BF16)
    lo = (w - hi.astype(F32)).astype(BF16)
    return jnp.concatenate([hi, lo], axis=1)


def _dot_hilo(x, w2):
    m, n = x.shape[0], w2.shape[1] // 2
    hi = x.astype(BF16)
    lo = (x - hi.astype(F32)).astype(BF16)
    r = _dot(jnp.concatenate([hi, lo], axis=0), w2)
    return r[:m, :n] + (r[:m, n:] + r[m:, :n])


def _unpack_halves(p):
    lo = lax.bitcast_convert_type(lax.shift_left(p, jnp.uint32(16)), F32)
    hi = lax.bitcast_convert_type(p & jnp.uint32(0xFFFF0000), F32)
    return lo, hi


def _ada_kernel(c_ref, w_ref, b_ref, o_ref):
    c = c_ref[...]
    s = (c * jax.nn.sigmoid(c)).astype(BF16)
    o_ref[...] = _dot(s, w_ref[...].astype(BF16)) + b_ref[...]


def _ada(c, w_ada, b_ada):
    B, D = c.shape
    N = w_ada.shape[1]
    tn = _pick_tile(N, (1024, 512, 256, 128))
    return pl.pallas_call(
        _ada_kernel,
        grid=(N // tn,),
        in_specs=[pl.BlockSpec((B, D), lambda j: (0, 0)),
                  pl.BlockSpec((D, tn), lambda j: (0, j)),
                  pl.BlockSpec((1, tn), lambda j: (0, j))],
        out_specs=pl.BlockSpec((B, tn), lambda j: (0, j)),
        out_shape=jax.ShapeDtypeStruct((B, N), F32),
        compiler_params=_cparams(("arbitrary",), 40),
    )(c, w_ada, b_ada.reshape(1, N))


def _inproj_kernel(x_ref, g_ref, sc_ref, sh_ref, w_ref, wg_ref, o_ref, og_ref, h_scr):
    @pl.when(pl.program_id(1) == 0)
    def _():
        x = x_ref[...]
        y = x * lax.rsqrt(jnp.mean(x * x, axis=-1, keepdims=True) + EPS) * g_ref[...]
        h = y * (1.0 + sc_ref[...]) + sh_ref[...]
        h_scr[...] = h.astype(BF16)
        og_ref[...] = _dot_hilo(h, wg_ref[...])

    o_ref[...] = _dot(h_scr[...], w_ref[...]).astype(o_ref.dtype)


def _inproj(x2, g, sc, sh, w_all, n_main, w_gate, S):
    T, D = x2.shape
    NM = n_main
    tm = _pick_tile(S, (1024, 512, 256, 128))
    tn = _pick_tile(NM, (1024, 512, 256, 128))
    per_b = S // tm
    return pl.pallas_call(
        _inproj_kernel,
        grid=(T // tm, NM // tn),
        in_specs=[pl.BlockSpec((tm, D), lambda i, j: (i, 0)),
                  pl.BlockSpec((1, D), lambda i, j: (0, 0)),
                  pl.BlockSpec((None, 1, D), lambda i, j: (i // per_b, 0, 0)),
                  pl.BlockSpec((None, 1, D), lambda i, j: (i // per_b, 0, 0)),
                  pl.BlockSpec((D, tn), lambda i, j: (0, j)),
                  pl.BlockSpec((D, 2 * LANES), lambda i, j: (0, 0))],
        out_specs=[pl.BlockSpec((tm, tn), lambda i, j: (i, j)),
                   pl.BlockSpec((tm, LANES), lambda i, j: (i, 0))],
        out_shape=[jax.ShapeDtypeStruct((T, NM), BF16), jax.ShapeDtypeStruct((T, LANES), F32)],
        scratch_shapes=[pltpu.VMEM((tm, D), BF16)],
        compiler_params=_cparams(("arbitrary", "arbitrary"), 56),
    )(x2, g, sc, sh, w_all, w_gate)


def _gating_kernel(gat_ref, a_ref, dt_ref, o_ref, *, hb):
    gat = gat_ref[...]
    tm = gat.shape[0]
    gw = 4 * hb
    g = -jnp.exp(a_ref[...]) * jax.nn.softplus(gat + dt_ref[...])
    beta = jax.nn.sigmoid(gat)
    r = lax.broadcasted_iota(I32, (tm, tm), 0)
    c = lax.broadcasted_iota(I32, (tm, tm), 1)
    same = (r // CHUNK) == (c // CHUNK)
    pre = _dot(jnp.where(same & (c <= r), 1.0, 0.0), g, precision=HIGHEST)
    suf = _dot(jnp.where(same & (c >= r), 1.0, 0.0), g, precision=HIGHEST)
    lane = lax.broadcasted_iota(I32, gat.shape, 1)
    o_ref[...] = jnp.where(lane // gw == 0, jnp.where(lane % gw < hb, pre, suf), beta)


def _gating(gat, a_row, dt_row, hb):
    T = gat.shape[0]
    tm = ROW_TILE
    return pl.pallas_call(
        functools.partial(_gating_kernel, hb=hb),
        grid=(T // tm,),
        in_specs=[pl.BlockSpec((tm, LANES), lambda i: (i, 0)),
                  pl.BlockSpec((1, LANES), lambda i: (0, 0)),
                  pl.BlockSpec((1, LANES), lambda i: (0, 0))],
        out_specs=pl.BlockSpec((tm, LANES), lambda i: (i, 0)),
        out_shape=jax.ShapeDtypeStruct((T, LANES), F32),
        compiler_params=_cparams(("arbitrary",), 16),
    )(gat, a_row, dt_row)


def _t5_bucket(rel, n_buckets):
    half = n_buckets // 2
    max_exact = half // 2
    n = np.abs(rel)
    large = max_exact + (np.log(np.maximum(n, 1) / max_exact) / np.log(REL_MAX_DIST / max_exact)
                         * (half - max_exact)).astype(np.int32)
    large = np.minimum(large, half - 1)
    return (np.where(rel > 0, half, 0) + np.where(n < max_exact, n, large)).astype(np.int32)


def _attn_plan(S):
    plan, base = [], 0
    for window, dil in DILATED_BRANCHES:
        n = window // (2 * dil)
        L = S // dil
        assert L % ATT_QB == 0 and n * 2 == ATT_QB
        kw = min(ATT_KW, L)
        nbq = L // ATT_QB
        nvar = 1 if nbq == 1 else 3
        plan.append((dil, L, nbq, kw, base, nvar, n))
        base += nvar
    return tuple(plan), base


def _attn_bias_profiles(rel_bias, S):
    plan, nvar_total = _attn_plan(S)
    nbuckets, H = rel_bias.shape
    u = np.arange(ATT_PROF_W) - ATT_QB
    onehots, bands = [], []
    for dil, L, nbq, kw, base, nvar, n in plan:
        offs = [0] if nvar == 1 else [0, -n, -(kw - ATT_QB)]
        for off in offs:
            rel = off + u
            onehots.append(np.eye(nbuckets, dtype=np.float32)[_t5_bucket(rel * dil, nbuckets)])
            bands.append(np.abs(rel) <= n)
    onehot = jnp.asarray(np.stack(onehots))
    band = jnp.asarray(np.stack(bands))
    prof = jnp.einsum('vwn,nh->hvw', onehot, rel_bias.astype(F32), precision=HIGHEST)
    prof = jnp.where(band[None], prof, NEG)
    prof = prof.reshape(H // 2, 2, nvar_total, ATT_PROF_W)
    return jnp.transpose(prof, (0, 2, 1, 3)).reshape(H // 2, 2 * nvar_total, ATT_PROF_W)


def _attn_kernel(q_ref, k_ref, v_ref, qg_ref, kg_ref, prof_ref, o_ref,
                 qn_scr, kn_scr, v_scr, ob_scr, mb_scr, db_scr, bias_ref, *, plan, hd):
    S = q_ref.shape[0]
    lane = lax.broadcasted_iota(I32, (1, LANES), 1)
    left = lane < hd

    @pl.when(pl.program_id(1) == 0)
    def _():
        for row in range(prof_ref.shape[0]):
            rep = jnp.broadcast_to(prof_ref[row:row + 1, :], (ATT_QB, ATT_PROF_W))
            skew = pltpu.roll(rep, 0, 1, stride=1, stride_axis=0)
            bias_ref[row // 2, row % 2] = skew[:, ATT_QB:ATT_QB + ATT_KW]

    def headnorm(x, g):
        x2 = x * x
        s_all = jnp.sum(x2, axis=-1, keepdims=True)
        s_left = jnp.sum(jnp.where(left, x2, 0.0), axis=-1, keepdims=True)
        ms = jnp.where(left, s_left, s_all - s_left) * (1.0 / hd)
        return x * lax.rsqrt(ms + EPS) * g

    qn_scr[...] = headnorm(q_ref[...].astype(F32), qg_ref[...]) * (hd ** -0.5)
    kn_scr[...] = headnorm(k_ref[...].astype(F32), kg_ref[...])
    v_scr[...] = v_ref[...].astype(F32)

    for bi, (dil, L, nbq, kw, base, nvar, n) in enumerate(plan):
        ones = jnp.ones((kw, LANES), BF16)

        def body(t, carry, dil=dil, L=L, nbq=nbq, kw=kw, base=base, nvar=nvar, n=n, bi=bi, ones=ones):
            blocks = []
            for uu in range(ATT_UNROLL):
                idx = t * ATT_UNROLL + uu
                r = idx // nbq
                i = idx % nbq
                q0 = i * ATT_QB
                k0 = jnp.clip(q0 - n, 0, L - kw)
                var = base if nvar == 1 else base + jnp.where(i > 0, 1, 0) + jnp.where(i == nbq - 1, 1, 0)
                if dil == 1:
                    qrows = pl.ds(pl.multiple_of(q0, ATT_QB), ATT_QB)
                    krows = pl.ds(pl.multiple_of(k0, CHUNK), kw)
                else:
                    qrows = pl.ds(r + q0 * dil, ATT_QB, stride=dil)
                    krows = pl.ds(r + k0 * dil, kw, stride=dil)
                qb = qn_scr[qrows, :]
                q2 = jnp.concatenate([jnp.where(left, qb, 0.0), jnp.where(left, 0.0, qb)], axis=0).astype(BF16)
                blocks.append((qrows, krows, var, q2))
            scores = [_dot_nt(q2, kn_scr[krows, :].astype(BF16)) for qrows, krows, var, q2 in blocks]
            probs, maxes = [], []
            for (qrows, krows, var, q2), s in zip(blocks, scores):
                s = s + jnp.concatenate([bias_ref[var, 0][:, :kw], bias_ref[var, 1][:, :kw]], axis=0)
                m = jnp.max(s, axis=-1, keepdims=True)
                probs.append(jnp.exp(s - m).astype(BF16))
                maxes.append(m)
            outs = [_dot(p, jnp.concatenate([v_scr[krows, :].astype(BF16), ones], axis=1))
                    for (qrows, krows, var, q2), p in zip(blocks, probs)]
            for (qrows, krows, var, q2), m, od in zip(blocks, maxes, outs):
                mb = jnp.broadcast_to(m, (2 * ATT_QB, LANES))
                ob_scr[bi, qrows, :] = jnp.where(left, od[:ATT_QB, :LANES], od[ATT_QB:, :LANES])
                mb_scr[bi, qrows, :] = jnp.where(left, mb[:ATT_QB], mb[ATT_QB:])
                db_scr[bi, qrows, :] = jnp.where(left, od[:ATT_QB, LANES:], od[ATT_QB:, LANES:])
            return carry

        assert (dil * nbq) % ATT_UNROLL == 0
        lax.fori_loop(0, dil * nbq // ATT_UNROLL, body, 0)

    nb = len(plan)
    mx = mb_scr[0]
    for bi in range(1, nb):
        mx = jnp.maximum(mx, mb_scr[bi])
    num = jnp.zeros((S, LANES), F32)
    den = jnp.zeros((S, LANES), F32)
    for bi in range(nb):
        w = jnp.exp(mb_scr[bi] - mx)
        num = num + w * ob_scr[bi]
        den = den + w * db_scr[bi]
    o_ref[...] = (num / den).astype(o_ref.dtype)


def _attention(proj, qg2, kg2, profiles, B, S, HA, hd):
    T = proj.shape[0]
    pairs = HA // 2
    da_blocks = HA * hd // LANES
    plan, nvar = _attn_plan(S)
    return pl.pallas_call(
        functools.partial(_attn_kernel, plan=plan, hd=hd),
        grid=(pairs, B),
        in_specs=[pl.BlockSpec((S, LANES), lambda p, b: (b, p)),
                  pl.BlockSpec((S, LANES), lambda p, b: (b, da_blocks + p)),
                  pl.BlockSpec((S, LANES), lambda p, b: (b, 2 * da_blocks + p)),
                  pl.BlockSpec((1, LANES), lambda p, b: (0, 0)),
                  pl.BlockSpec((1, LANES), lambda p, b: (0, 0)),
                  pl.BlockSpec((None, 2 * nvar, ATT_PROF_W), lambda p, b: (p, 0, 0))],
        out_specs=pl.BlockSpec((S, LANES), lambda p, b: (b, p)),
        out_shape=jax.ShapeDtypeStruct((T, HA * hd), BF16),
        scratch_shapes=[pltpu.VMEM((S, LANES), F32)] * 3 + [pltpu.VMEM((len(plan), S, LANES), F32)] * 3
        + [pltpu.VMEM((nvar, 2, ATT_QB, ATT_KW), F32)],
        compiler_params=_cparams(("arbitrary", "arbitrary"), 40),
    )(proj, proj, proj, qg2, kg2, profiles)


def _gdn_kernel(q_ref, k_ref, v_ref, z_ref, cw_ref, pack_ref, og_ref, o_ref,
                q_scr, k_scr, v_scr, xpad_scr, pk2_scr, sel_scr, u_scr, wq_scr, at_scr, kdt_scr, et_scr, oacc_scr,
                *, hb, hp):
    S = q_ref.shape[0]
    P2 = 2 * CHUNK
    W2 = 2 * LANES
    npair = S // P2
    hg = pl.program_id(1)
    gw = 4 * hb
    dk = LANES

    pad = SUBLANES
    for slot in range(2):
        xpad_scr[slot, 0:pad, :] = jnp.zeros((pad, LANES), F32)
        xpad_scr[slot, pad + S:, :] = jnp.zeros((pad, LANES), F32)

    seg = GDN_SEG if S % GDN_SEG == 0 else S

    def conv_silu_to(src_ref, lanes_j, j, which, dst_scr, l2norm, scale):
        xp = xpad_scr.at[(3 * j + which) % 2]
        for s0 in range(0, S, seg):
            xp[pad + s0:pad + s0 + seg, :] = src_ref[s0:s0 + seg, lanes_j].astype(F32)
        for s0 in range(0, S, seg):
            acc = xp[pad + s0:pad + s0 + seg, :] * cw_ref[j, which, 2:3, :]
            for d in (-2, -1, 1, 2):
                acc = acc + xp[pad + s0 + d:pad + s0 + d + seg, :] * cw_ref[j, which, 2 + d:3 + d, :]
            y = acc * jax.nn.sigmoid(acc)
            if l2norm:
                y = y * (lax.rsqrt(jnp.sum(y * y, axis=-1, keepdims=True) + EPS) * scale)
            dst_scr[j, s0:s0 + seg, :] = y

    for s0 in range(0, S, seg):
        pk = pack_ref[s0:s0 + seg, :]
        p_hi = pk.astype(BF16)
        p_lo = (pk - p_hi.astype(F32)).astype(BF16)
        pk2_scr[s0:s0 + seg, :] = jnp.concatenate([p_hi, p_lo], axis=1)
    srow = lax.broadcasted_iota(I32, (W2, 4 * LANES), 0) % LANES
    scol = lax.broadcasted_iota(I32, (W2, 4 * LANES), 1) // LANES
    for j in range(hp):
        lanes_j = slice(j * LANES, (j + 1) * LANES)
        conv_silu_to(q_ref, lanes_j, j, 0, q_scr, True, dk ** -0.5)
        conv_silu_to(k_ref, lanes_j, j, 1, k_scr, True, 1.0)
        conv_silu_to(v_ref, lanes_j, j, 2, v_scr, False, 1.0)
        src = hg * hp + j + jnp.where(scol < 2, scol * hb, gw + 2 * hb + (scol - 2) * hb)
        sel_scr[j] = jnp.where(srow == src, 1.0, 0.0).astype(BF16)
        oacc_scr[j] = jnp.zeros((S, LANES), F32)

    r4 = lax.broadcasted_iota(I32, (CHUNK, W2), 0)
    l4 = lax.broadcasted_iota(I32, (CHUNK, W2), 1)
    c4 = l4 % CHUNK
    blk4 = l4 // CHUNK
    lo_half = (l4 % LANES) < CHUNK
    ahead = jnp.where(l4 >= LANES, r4 - c4, c4 - r4)
    incl = ahead <= 0
    strict = ahead < 0
    bd16 = (r4 // 16) == (c4 // 16)

    def squeeze(x):
        return jnp.where(lo_half, x[:CHUNK], x[CHUNK:])

    def unsqueeze(x):
        return jnp.concatenate([jnp.where(lo_half, x, jnp.zeros_like(x)),
                                jnp.where(lo_half, jnp.zeros_like(x), x)], axis=0)

    def mm4(a, b):
        rhs = jnp.concatenate([jnp.where(blk4 == g, b, 0.0) for g in range(4)], axis=0)
        return _dot(a.astype(BF16), rhs.astype(BF16))

    first = lax.broadcasted_iota(I32, (P2, 1), 0) < CHUNK
    zpair = jnp.zeros((P2, LANES), BF16)

    def bdiag(x):
        return jnp.concatenate([jnp.concatenate([x[:, :LANES], zpair], axis=1),
                                jnp.concatenate([zpair, x[:, LANES:]], axis=1)], axis=0)

    U = GDN_PREP_UNROLL if npair % GDN_PREP_UNROLL == 0 else 1

    def prep(t, carry):
        cx = []
        chains = [(t * U + u, j) for u in range(U) for j in range(hp)]
        bcs = [_dot(pk2_scr[pl.ds(pl.multiple_of(m * P2, P2), P2), :], sel_scr[j]) for m, j in chains]
        for (m, j), bc in zip(chains, bcs):
            rows = pl.ds(pl.multiple_of(m * P2, P2), P2)
            kp = k_scr[j, rows, :]
            qp = q_scr[j, rows, :]
            vp = v_scr[j, rows, :]
            gc2 = bc[:, 0:W2]
            beta2 = bc[:, W2:2 * W2]
            gcf, gcb = gc2[:, :LANES], gc2[:, LANES:]
            tot2 = jnp.concatenate([jnp.where(first, gcf[CHUNK - 1:CHUNK], gcf[P2 - 1:P2]),
                                    jnp.where(first, gcb[0:1], gcb[CHUNK:CHUNK + 1])], axis=1)
            egc2 = jnp.exp(gc2)
            k2 = jnp.concatenate([kp, kp], axis=1)
            kb2 = k2 * beta2
            vb2 = jnp.concatenate([vp, vp], axis=1) * beta2
            kbe2 = kb2 * egc2
            cx.append(dict(
                m=m, j=j, rows=rows, vb2=vb2, kbe2=kbe2,
                qeb=(jnp.concatenate([qp, qp], axis=1) * egc2).astype(BF16),
                kd2=k2 * jnp.exp(tot2 - gc2),
                et=jnp.exp(tot2),
                dec=jnp.exp(jnp.where(
                    incl, squeeze(gc2) - squeeze(jnp.concatenate([gcf.T, gcb.T], axis=1)), -jnp.inf)),
                stk=jnp.concatenate([kb2[:, :LANES], kb2[:, LANES:], qp], axis=0).astype(BF16),
                kpb=kp.astype(BF16)))

        g3s = [_dot_nt(c['stk'], c['kpb']) for c in cx]
        for c, g3 in zip(cx, g3s):
            lm = jnp.where(strict, squeeze(jnp.concatenate([g3[:P2], g3[P2:2 * P2]], axis=1)) * c['dec'], 0.0)
            attn = squeeze(jnp.concatenate([g3[2 * P2:], g3[2 * P2:]], axis=1)) * c['dec']
            c['attn2'] = unsqueeze(attn).astype(BF16)
            c['lbd'] = jnp.where(bd16, lm, 0.0)
            c['loff'] = lm - c['lbd']
        nn = [-c['lbd'] for c in cx]
        pw = [mm4(c['lbd'], c['lbd']) for c in cx]
        for rnd in range(3):
            prod = [mm4(a, p) for a, p in zip(nn, pw)]
            nxt = [mm4(p, p) for p in pw] if rnd < 2 else pw
            nn = [a + p + q for a, p, q in zip(nn, pw, prod)]
            pw = nxt
        mo = [c['loff'] + x for c, x in zip(cx, [mm4(a, c['loff']) for a, c in zip(nn, cx)])]
        m2 = [mm4(x, x) for x in mo]
        mn = [mm4(x, a) for x, a in zip(mo, nn)]
        xo = [a - x - y for a, x, y in zip(nn, mo, mn)]
        mx = [mm4(a, b) for a, b in zip(m2, xo)]
        toff = [a + b + q for a, b, q in zip(xo, m2, mx)]
        z4 = jnp.zeros((CHUNK, W2), BF16)
        tws = []
        for c, tf in zip(cx, toff):
            vbb, kbb = c['vb2'].astype(BF16), c['kbe2'].astype(BF16)
            blocks = []
            for g in range(4):
                rr = slice((g % 2) * CHUNK, (g % 2 + 1) * CHUNK)
                ll = slice((g // 2) * LANES, (g // 2 + 1) * LANES)
                blocks.append(jnp.concatenate([z4] * g + [vbb[rr, ll], kbb[rr, ll]] + [z4] * (3 - g), axis=1))
            tws.append(_dot(tf.astype(BF16), jnp.concatenate(blocks, axis=0)))
        for c, tw in zip(cx, tws):
            m, j, rows = c['m'], c['j'], c['rows']
            def pair_layout(off):
                piece = lambda g: tw[:, 2 * g * LANES + off:2 * g * LANES + off + LANES]
                return jnp.concatenate([jnp.concatenate([piece(0), piece(2)], axis=1),
                                        jnp.concatenate([piece(1), piece(3)], axis=1)], axis=0)

            u2 = c['vb2'] + pair_layout(0)
            w2 = (c['kbe2'] + pair_layout(LANES)).astype(BF16)
            qeb = c['qeb']
            u_scr[j, rows, :] = u2
            wq_scr[j, pl.ds(pl.multiple_of(m * 2 * P2, 2 * P2), 2 * P2), :] = jnp.concatenate(
                [w2[:CHUNK], qeb[:CHUNK], w2[CHUNK:], qeb[CHUNK:]], axis=0)
            at_scr[j, rows, :] = c['attn2']
            kdt_scr[j, 0, :, rows] = c['kd2'][:, :LANES].T.astype(BF16)
            kdt_scr[j, 1, :, rows] = c['kd2'][:, LANES:].T.astype(BF16)
            et_scr[j, pl.ds(pl.multiple_of(m * 2 * SUBLANES, 2 * SUBLANES), 2 * SUBLANES), :] = jnp.concatenate(
                [c['et'][:SUBLANES], c['et'][CHUNK:CHUNK + SUBLANES]], axis=0)
        return carry

    lax.fori_loop(0, npair // U, prep, 0)

    zc = jnp.zeros((CHUNK, LANES), F32)
    zp = jnp.zeros((P2, LANES), F32)

    def place(v, cpos):
        return jnp.concatenate([v, zc] if cpos == 0 else [zc, v], axis=0)

    def scan(m, states):
        pf = m
        pb = npair - 1 - m
        rows_f = pl.ds(pl.multiple_of(pf * P2, P2), P2)
        rows_b = pl.ds(pl.multiple_of(pb * P2, P2), P2)
        hx = []
        for j in range(hp):
            hx.append(dict(
                u_f=u_scr[j, rows_f, :LANES], u_b=u_scr[j, rows_b, LANES:],
                at_f=at_scr[j, rows_f, :LANES], at_b=at_scr[j, rows_b, LANES:],
                kdt=jnp.concatenate([kdt_scr[j, 0, :, rows_f], kdt_scr[j, 1, :, rows_b]], axis=1),
                wq_f=wq_scr[j, pl.ds(pl.multiple_of(pf * 2 * P2, 2 * P2), 2 * P2), :LANES],
                wq_b=wq_scr[j, pl.ds(pl.multiple_of(pb * 2 * P2, 2 * P2), 2 * P2), LANES:],
                et_f=et_scr[j, pl.ds(pl.multiple_of(pf * 2 * SUBLANES, 2 * SUBLANES), 2 * SUBLANES), :LANES],
                et_b=et_scr[j, pl.ds(pl.multiple_of(pb * 2 * SUBLANES, 2 * SUBLANES), 2 * SUBLANES), LANES:]))
        sts = list(states)
        for step in range(2):
            cf, cb = step, 1 - step
            rrs = [_dot(jnp.concatenate([c['wq_f'][cf * P2:(cf + 1) * P2], c['wq_b'][cb * P2:(cb + 1) * P2]], axis=1),
                        bdiag(st.astype(BF16))) for c, st in zip(hx, sts)]
            ress = []
            for c, rr in zip(hx, rrs):
                u2 = jnp.concatenate([c['u_f'][cf * CHUNK:(cf + 1) * CHUNK],
                                      c['u_b'][cb * CHUNK:(cb + 1) * CHUNK]], axis=1)
                v_new = u2 - rr[:CHUNK]
                rhs = jnp.concatenate(
                    [jnp.concatenate([place(v_new[:, :LANES], cf), zp], axis=1),
                     jnp.concatenate([zp, place(v_new[:, LANES:], cb)], axis=1)], axis=0).astype(BF16)
                lhs = jnp.concatenate(
                    [jnp.concatenate([c['at_f'][cf * CHUNK:(cf + 1) * CHUNK],
                                      c['at_b'][cb * CHUNK:(cb + 1) * CHUNK]], axis=1),
                     c['kdt']], axis=0)
                ress.append(_dot(lhs, rhs))
            for j, (c, rr, res) in enumerate(zip(hx, rrs, ress)):
                o2 = rr[CHUNK:] + res[:CHUNK]
                et2 = jnp.concatenate([c['et_f'][cf * SUBLANES:cf * SUBLANES + 1],
                                       c['et_b'][cb * SUBLANES:cb * SUBLANES + 1]], axis=1)
                sts[j] = sts[j] * et2 + res[CHUNK:]
                of_rows = pl.ds(pl.multiple_of(pf * P2 + cf * CHUNK, CHUNK), CHUNK)
                ob_rows = pl.ds(pl.multiple_of(pb * P2 + cb * CHUNK, CHUNK), CHUNK)
                oacc_scr[j, of_rows, :] = oacc_scr[j, of_rows, :] + o2[:, :LANES]
                oacc_scr[j, ob_rows, :] = oacc_scr[j, ob_rows, :] + o2[:, LANES:]
        return tuple(sts)

    s0 = jnp.zeros((dk, W2), F32)
    lax.fori_loop(0, npair, scan, (s0,) * hp)

    for j in range(hp):
        lanes_j = slice(j * LANES, (j + 1) * LANES)
        for s0 in range(0, S, seg):
            o = oacc_scr[j, s0:s0 + seg, :]
            y = o * lax.rsqrt(jnp.mean(o * o, axis=-1, keepdims=True) + EPS) * og_ref[...]
            z = z_ref[s0:s0 + seg, lanes_j].astype(F32)
            o_ref[s0:s0 + seg, lanes_j] = (y * (z * jax.nn.sigmoid(z))).astype(o_ref.dtype)


def _gdn(proj, cw4, pack3, onorm_g, B, S, HB, base_blk):
    T = proj.shape[0]
    hp = GDN_HP if (HB % GDN_HP == 0 and base_blk % GDN_HP == 0) else 1
    wblk = hp * LANES
    npair = S // (2 * CHUNK)

    def col(k):
        off = (base_blk + k * HB) // hp
        return lambda b, h: (b, off + h)

    return pl.pallas_call(
        functools.partial(_gdn_kernel, hb=HB, hp=hp),
        grid=(B, HB // hp),
        in_specs=[pl.BlockSpec((S, wblk), col(0)),
                  pl.BlockSpec((S, wblk), col(1)),
                  pl.BlockSpec((S, wblk), col(2)),
                  pl.BlockSpec((S, wblk), col(3)),
                  pl.BlockSpec((hp, 3, cw4.shape[2], LANES), lambda b, h: (h, 0, 0, 0)),
                  pl.BlockSpec((None, S, LANES), lambda b, h: (b, 0, 0)),
                  pl.BlockSpec((1, LANES), lambda b, h: (0, 0))],
        out_specs=pl.BlockSpec((S, wblk), lambda b, h: (b, h)),
        out_shape=jax.ShapeDtypeStruct((T, HB * LANES), BF16),
        scratch_shapes=[pltpu.VMEM((hp, S, LANES), F32)] * 3
        + [pltpu.VMEM((2, S + 2 * SUBLANES, LANES), F32),
           pltpu.VMEM((S, 2 * LANES), BF16),
           pltpu.VMEM((hp, 2 * LANES, 4 * LANES), BF16),
           pltpu.VMEM((hp, S, 2 * LANES), F32),
           pltpu.VMEM((hp, 2 * S, 2 * LANES), BF16),
           pltpu.VMEM((hp, S, 2 * LANES), BF16),
           pltpu.VMEM((hp, 2, LANES, S), BF16),
           pltpu.VMEM((hp, npair * 2 * SUBLANES, 2 * LANES), F32),
           pltpu.VMEM((hp, S, LANES), F32)],
        compiler_params=_cparams(("arbitrary", "arbitrary"), 56),
    )(proj, proj, proj, proj, cw4, pack3, onorm_g)


def _outproj_kernel(oa_ref, ob_ref, x_ref, gt_ref, w_ref, g_ref, sc_ref, sh_ref, wr_ref, br_ref,
                    x1_ref, hp_ref, lg_ref, h_scr):
    da = oa_ref.shape[1]
    y = _dot(oa_ref[...], w_ref[:da, :]) + _dot(ob_ref[...], w_ref[da:, :])
    x1 = x_ref[...] + gt_ref[...] * y
    x1_ref[...] = x1
    hn = x1 * lax.rsqrt(jnp.mean(x1 * x1, axis=-1, keepdims=True) + EPS) * g_ref[...]
    h = hn * (1.0 + sc_ref[...]) + sh_ref[...]
    hp_ref[...] = _pack_halves(h)
    h_scr[...] = h

    @pl.when(pl.program_id(0) < pl.num_programs(0))
    def _():
        lg_ref[...] = _dot_hilo(h_scr[...], wr_ref[...]) + br_ref[...]


def _outproj(oa, ob, x2, gt1, w_out_b, g2, sc2, sh2, wr, br, S):
    T, D = x2.shape
    tm = ROW_TILE
    per_b = S // tm
    bmap = lambda i: (i // per_b, 0, 0)
    return pl.pallas_call(
        _outproj_kernel,
        grid=(T // tm,),
        in_specs=[pl.BlockSpec((tm, oa.shape[1]), lambda i: (i, 0)),
                  pl.BlockSpec((tm, ob.shape[1]), lambda i: (i, 0)),
                  pl.BlockSpec((tm, D), lambda i: (i, 0)),
                  pl.BlockSpec((None, 1, D), bmap),
                  pl.BlockSpec((D, D), lambda i: (0, 0)),
                  pl.BlockSpec((1, D), lambda i: (0, 0)),
                  pl.BlockSpec((None, 1, D), bmap),
                  pl.BlockSpec((None, 1, D), bmap),
                  pl.BlockSpec((D, 2 * LANES), lambda i: (0, 0)),
                  pl.BlockSpec((1, LANES), lambda i: (0, 0))],
        out_specs=[pl.BlockSpec((tm, D), lambda i: (i, 0)),
                   pl.BlockSpec((tm, D // 2), lambda i: (i, 0)),
                   pl.BlockSpec((tm, LANES), lambda i: (i, 0))],
        out_shape=[jax.ShapeDtypeStruct((T, D), F32), jax.ShapeDtypeStruct((T, D // 2), U32),
                   jax.ShapeDtypeStruct((T, LANES), F32)],
        scratch_shapes=[pltpu.VMEM((tm, D), F32)],
        compiler_params=_cparams(("arbitrary",), 48),
    )(oa, ob, x2, gt1, w_out_b, g2, sc2, sh2, wr, br)


def _route_kernel(lg_ref, o_ref, info_ref, run_scr, *, ne, ng):
    ph = pl.program_id(0)
    i = pl.program_id(1)

    @pl.when((ph == 0) & (i == 0))
    def _():
        run_scr[...] = jnp.zeros_like(run_scr)

    @pl.when((ph == 1) & (i == 0))
    def _():
        cnt = run_scr[...]
        padded = jnp.ceil(cnt * (1.0 / MOE_RB)) * MOE_RB
        k = lax.broadcasted_iota(I32, (LANES, LANES), 0)
        e = lax.broadcasted_iota(I32, (LANES, LANES), 1)
        start = _dot(padded, jnp.where(k < e, 1.0, 0.0), precision=HIGHEST)
        rowi = lax.broadcasted_iota(I32, cnt.shape, 0)
        info_ref[...] = jnp.where(rowi == 0, cnt, jnp.where(rowi == 1, padded, start))
        run_scr[...] = start

    lg = lg_ref[...]
    tm = lg.shape[0]
    epg = ne // ng
    lane_i = lax.broadcasted_iota(I32, lg.shape, 1)
    lane = lane_i.astype(F32)
    big = float(2 * LANES)
    is_g = (lane_i >= ne) & (lane_i < ne + ng)
    gl = jnp.where(is_g, lg, -jnp.inf)
    gmax = jnp.max(gl, axis=-1, keepdims=True)
    gidx = jnp.min(jnp.where(gl == gmax, lane, big), axis=-1, keepdims=True) - ne
    psel = 1.0 / jnp.sum(jnp.where(is_g, jnp.exp(gl - gmax), 0.0), axis=-1, keepdims=True)
    in_grp = (lane_i // epg).astype(F32) == gidx
    el = jnp.where(in_grp & (lane_i < ne), lg, -jnp.inf)
    m1 = jnp.max(el, axis=-1, keepdims=True)
    i1 = jnp.min(jnp.where(el == m1, lane, big), axis=-1, keepdims=True)
    el2 = jnp.where(lane == i1, -jnp.inf, el)
    m2 = jnp.max(el2, axis=-1, keepdims=True)
    i2 = jnp.min(jnp.where(el2 == m2, lane, big), axis=-1, keepdims=True)
    e21 = jnp.exp(m2 - m1)
    g1 = psel / (1.0 + e21)
    g2 = psel * e21 / (1.0 + e21)
    o1 = jnp.where(lane == i1, 1.0, 0.0)
    o2 = jnp.where(lane == i2, 1.0, 0.0)
    cnt = o1 + o2
    r = lax.broadcasted_iota(I32, (tm, tm), 0)
    c = lax.broadcasted_iota(I32, (tm, tm), 1)
    before = _dot(jnp.where(c < r, 1.0, 0.0).astype(BF16), cnt.astype(BF16)) + run_scr[0:1, :]
    d1 = jnp.sum(o1 * before, axis=-1, keepdims=True)
    d2 = jnp.sum(o2 * before, axis=-1, keepdims=True)
    run_scr[...] = run_scr[...] + jnp.sum(cnt, axis=0, keepdims=True)

    @pl.when(ph == 1)
    def _():
        out = jnp.zeros(lg.shape, F32)
        for j, val in enumerate((i1, i2, g1, g2, d1, d2)):
            out = jnp.where(lane_i == j, val, out)
        o_ref[...] = out


def _route(logits, ne, ng):
    T = logits.shape[0]
    tm = _pick_tile(T, (1024, 512, 256))
    return pl.pallas_call(
        functools.partial(_route_kernel, ne=ne, ng=ng),
        grid=(2, T // tm),
        in_specs=[pl.BlockSpec((tm, LANES), lambda p, i: (i, 0))],
        out_specs=[pl.BlockSpec((tm, LANES), lambda p, i: (i * p, 0)),
                   pl.BlockSpec((SUBLANES, LANES), lambda p, i: (0, 0))],
        out_shape=[jax.ShapeDtypeStruct((T, LANES), F32), jax.ShapeDtypeStruct((SUBLANES, LANES), F32)],
        scratch_shapes=[pltpu.VMEM((SUBLANES, LANES), F32)],
        compiler_params=_cparams(("arbitrary", "arbitrary"), 16),
    )(logits)


def _dispatch_kernel(dest_ref, h_hbm, xs_in, xs_hbm, sem):
    del xs_in
    tm = ROW_TILE
    base = pl.program_id(0) * tm

    def copy(tok, a):
        return pltpu.make_async_copy(h_hbm.at[pl.ds(tok, 1)], xs_hbm.at[pl.ds(dest_ref[a], 1)], sem)

    def issue(t, carry):
        for k in range(TOP_K):
            copy(base + t, (base + t) * TOP_K + k).start()
        return carry

    lax.fori_loop(0, tm, issue, 0, unroll=DMA_UNROLL)

    for k in range(TOP_K):
        pltpu.make_async_copy(h_hbm.at[pl.ds(0, tm)], xs_hbm.at[pl.ds(0, tm)], sem).wait()


def _dispatch(dest, hpk, p_rows):
    T, dh = hpk.shape
    return pl.pallas_call(
        _dispatch_kernel,
        grid_spec=pltpu.PrefetchScalarGridSpec(
            num_scalar_prefetch=1,
            grid=(T // ROW_TILE,),
            in_specs=[pl.BlockSpec(memory_space=pl.ANY), pl.BlockSpec(memory_space=pl.ANY)],
            out_specs=pl.BlockSpec(memory_space=pl.ANY),
            scratch_shapes=[pltpu.SemaphoreType.DMA(())]),
        out_shape=jax.ShapeDtypeStruct((p_rows, dh), U32),
        input_output_aliases={2: 0},
        compiler_params=_cparams(("arbitrary",), 16),
    )(dest, hpk, jnp.zeros((p_rows, dh), U32))


def _moe_kernel(we_ref, ws_ref, wn_ref, wt_ref, xs_hbm, w1_ref, w3_ref, w2_ref, y_hbm,
                xbuf, xlo, xhi, yacc, ypk, sem_in, sem_out):
    w = pl.program_id(0)
    c = pl.program_id(1)
    nw = pl.num_programs(0)
    nc = pl.num_programs(1)
    nrows = wn_ref[w]
    start = ws_ref[w]
    nblk = nrows // MOE_RB
    dh = xbuf.shape[1]

    def blk_rows(rb):
        return pl.ds(pl.multiple_of(rb * MOE_RB, MOE_RB), MOE_RB)

    def hbm_rows(item_start, rb):
        return pl.ds(pl.multiple_of(item_start + rb * MOE_RB, MOE_RB), MOE_RB)

    def in_copy(item_start, rb):
        return pltpu.make_async_copy(xs_hbm.at[hbm_rows(item_start, rb)], xbuf.at[blk_rows(rb)], sem_in)

    def out_copy(item_start, rb):
        return pltpu.make_async_copy(ypk.at[blk_rows(rb)], y_hbm.at[hbm_rows(item_start, rb)], sem_out)

    def each_block(n, fn):
        def body(rb, carry):
            fn(rb)
            return carry
        lax.fori_loop(0, n, body, 0)

    @pl.when(c == 0)
    def _load():
        @pl.when(w == 0)
        def _():
            each_block(nblk, lambda rb: in_copy(start, rb).start())

        each_block(nblk, lambda rb: in_copy(start, rb).wait())

        def unpack(rb):
            lo, hi = _unpack_halves(xbuf[blk_rows(rb), :])
            xlo[blk_rows(rb), :] = lo.astype(BF16)
            xhi[blk_rows(rb), :] = hi.astype(BF16)
            yacc[blk_rows(rb), :] = jnp.zeros((MOE_RB, 2 * dh), F32)

        each_block(nblk, unpack)

        @pl.when(w + 1 < nw)
        def _():
            nxt = ws_ref[w + 1]
            each_block(wn_ref[w + 1] // MOE_RB, lambda rb: in_copy(nxt, rb).start())

    @pl.when(nrows > 0)
    def _compute():
        def blk(rb):
            rows = blk_rows(rb)
            xl = xlo[rows, :]
            xh = xhi[rows, :]
            h1 = _dot(xl, w1_ref[:dh, :].astype(BF16)) + _dot(xh, w1_ref[dh:, :].astype(BF16))
            h3 = _dot(xl, w3_ref[:dh, :].astype(BF16)) + _dot(xh, w3_ref[dh:, :].astype(BF16))
            hid = (h1 * jax.nn.sigmoid(h1) * h3).astype(BF16)
            yacc[rows, :] = yacc[rows, :] + _dot(hid, w2_ref[...].astype(BF16))

        each_block(nblk, blk)

    @pl.when(c == nc - 1)
    def _store():
        @pl.when(w > 0)
        def _():
            prev = ws_ref[w - 1]
            each_block(wn_ref[w - 1] // MOE_RB, lambda rb: out_copy(prev, rb).wait())

        def pack(rb):
            ypk[blk_rows(rb), :] = _pack_halves(yacc[blk_rows(rb), :])

        each_block(nblk, pack)
        each_block(nblk, lambda rb: out_copy(start, rb).start())

        @pl.when(w == nw - 1)
        def _():
            each_block(nblk, lambda rb: out_copy(start, rb).wait())

    @pl.when((w == nw - 1) & (c == nc - 1))
    def _zero_tail():
        ypk[blk_rows(0), :] = jnp.zeros((MOE_RB, dh), U32)
        first = wt_ref[0] // MOE_RB

        def tail_copy(b):
            return pltpu.make_async_copy(ypk.at[blk_rows(0)],
                                         y_hbm.at[pl.ds(pl.multiple_of(b * MOE_RB, MOE_RB), MOE_RB)], sem_out)

        def start(b, carry):
            tail_copy(b).start()
            return carry

        def wait(b, carry):
            tail_copy(b).wait()
            return carry

        lax.fori_loop(first, y_hbm.shape[0] // MOE_RB, start, 0)
        lax.fori_loop(first, y_hbm.shape[0] // MOE_RB, wait, 0)


def _moe(we, ws, wn, wt, xs, w1, w3, w2):
    P, dh = xs.shape
    D = 2 * dh
    NE, _, DE = w1.shape
    cw = min(MOE_CW, DE)
    nc = DE // cw
    nw = we.shape[0]
    return pl.pallas_call(
        _moe_kernel,
        grid_spec=pltpu.PrefetchScalarGridSpec(
            num_scalar_prefetch=4,
            grid=(nw, nc),
            in_specs=[pl.BlockSpec(memory_space=pl.ANY),
                      pl.BlockSpec((None, D, cw), lambda w, c, we, ws, wn, wt: (we[w],0, c)),
                      pl.BlockSpec((None, D, cw), lambda w, c, we, ws, wn, wt: (we[w],0, c)),
                      pl.BlockSpec((None, cw, D), lambda w, c, we, ws, wn, wt: (we[w],c, 0))],
            out_specs=pl.BlockSpec(memory_space=pl.ANY),
            scratch_shapes=[pltpu.VMEM((MOE_RMAX, dh), U32),
                            pltpu.VMEM((MOE_RMAX, dh), BF16),
                            pltpu.VMEM((MOE_RMAX, dh), BF16),
                            pltpu.VMEM((MOE_RMAX, D), F32),
                            pltpu.VMEM((MOE_RMAX, dh), U32),
                            pltpu.SemaphoreType.DMA(()),
                            pltpu.SemaphoreType.DMA(())]),
        out_shape=jax.ShapeDtypeStruct((P, dh), U32),
        compiler_params=_cparams(("arbitrary", "arbitrary"), 56),
    )(we, ws, wn, wt, xs, w1, w3, w2)


def _moe_schedule(info, NE, p_rows):
    padded = info[1, :NE].astype(I32)
    start_pad = info[2, :NE].astype(I32)
    items = (padded + MOE_RMAX - 1) // MOE_RMAX
    cum_items = jnp.cumsum(items)
    n_items = cum_items[-1]
    nw = (p_rows + NE * (MOE_RMAX - MOE_RB)) // MOE_RMAX
    wi = jnp.arange(nw, dtype=I32)
    valid = wi < n_items
    wi_c = jnp.minimum(wi, jnp.maximum(n_items - 1, 0))
    we = jnp.minimum(jnp.searchsorted(cum_items, wi_c, side='right'), NE - 1).astype(I32)
    local = wi_c - (cum_items[we] - items[we])
    ws = (start_pad[we] + local * MOE_RMAX).astype(I32)
    wn = jnp.where(valid, jnp.clip(padded[we] - local * MOE_RMAX, 0, MOE_RMAX), 0).astype(I32)
    wt = (start_pad[NE - 1] + padded[NE - 1]).reshape(1)
    return we, ws, wn, wt


def _combine_kernel(dest_ref, x1_ref, rt_ref, gt_ref, y_hbm, o_ref, ybuf, sems):
    tm = ROW_TILE
    i = pl.program_id(0)
    n = pl.num_programs(0)
    dh = ybuf.shape[-1]

    def copy(tile, slot, t, k):
        a = (tile * tm + t) * TOP_K + k
        return pltpu.make_async_copy(y_hbm.at[pl.ds(dest_ref[a], 1)], ybuf.at[slot, k, pl.ds(t, 1)], sems.at[slot])

    def issue_tile(tile, slot):
        def body(t, carry):
            for k in range(TOP_K):
                copy(tile, slot, t, k).start()
            return carry
        lax.fori_loop(0, tm, body, 0, unroll=DMA_UNROLL)

    @pl.when(i == 0)
    def _():
        issue_tile(0, 0)

    @pl.when(i + 1 < n)
    def _():
        issue_tile(i + 1, (i + 1) % 2)

    slot = i % 2

    for k in range(TOP_K):
        pltpu.make_async_copy(y_hbm.at[pl.ds(0, tm)], ybuf.at[slot, k], sems.at[slot]).wait()

    rt = rt_ref[...]
    g1 = rt[:, 2:3]
    g2 = rt[:, 3:4]
    lo1, hi1 = _unpack_halves(ybuf[slot, 0])
    lo2, hi2 = _unpack_halves(ybuf[slot, 1])
    o_ref[:, :dh] = x1_ref[:, :dh] + gt_ref[:, :dh] * (g1 * lo1 + g2 * lo2)
    o_ref[:, dh:] = x1_ref[:, dh:] + gt_ref[:, dh:] * (g1 * hi1 + g2 * hi2)


def _combine(dest, x1, route, gt2, ypk, S):
    T, D = x1.shape
    tm = ROW_TILE
    per_b = S // tm
    return pl.pallas_call(
        _combine_kernel,
        grid_spec=pltpu.PrefetchScalarGridSpec(
            num_scalar_prefetch=1,
            grid=(T // tm,),
            in_specs=[pl.BlockSpec((tm, D), lambda i, d: (i, 0)),
                      pl.BlockSpec((tm, LANES), lambda i, d: (i, 0)),
                      pl.BlockSpec((None, 1, D), lambda i, d: (i // per_b, 0, 0)),
                      pl.BlockSpec(memory_space=pl.ANY)],
            out_specs=pl.BlockSpec((tm, D), lambda i, d: (i, 0)),
            scratch_shapes=[pltpu.VMEM((2, TOP_K, tm, D // 2), U32),
                            pltpu.SemaphoreType.DMA((2,))]),
        out_shape=jax.ShapeDtypeStruct((T, D), F32),
        compiler_params=_cparams(("arbitrary",), 32),
    )(dest, x1, route, gt2, ypk)


def kernel(x, c, w_ada, b_ada, norm1_g, norm2_g, w_in, qn_g, kn_g, rel_bias, conv_w, A_log, dt_bias,
           onorm_g, w_out, w_rg, b_rg, w_re, b_re, w1, w3, w2):
    B, S, D = x.shape
    depth = w_ada.shape[0]
    HA, hda = rel_bias.shape[1], qn_g.shape[-1]
    HB, hdb = A_log.shape[-1], onorm_g.shape[-1]
    DA, DB = HA * hda, HB * hdb
    NG, NE = w_rg.shape[-1], w_re.shape[-1]
    T = B * S
    assert hdb == LANES and 2 * hda == LANES and DA + DB == D and 16 * HB <= LANES
    assert NE + NG <= LANES and conv_w.shape[1] == 5 and S % ROW_TILE == 0
    n_main = 3 * DA + 4 * DB
    p_rows = TOP_K * T + NE * MOE_RB

    bias_prof = _attn_bias_profiles(rel_bias, S)
    w_in_b = w_in.astype(BF16)
    x2 = x.reshape(T, D)
    for l in range(depth):
        mod = _ada(c, w_ada[l], b_ada[l]).reshape(B, 6, 1, D)
        sh1, sc1, gt1, sh2, sc2, gt2 = (mod[:, i] for i in range(6))

        w_gate = jnp.tile(w_in[l][:, n_main:], (1, 4))
        w_gate = _hilo_weights(jnp.pad(w_gate, ((0, 0), (0, LANES - w_gate.shape[1]))))
        proj, gat = _inproj(x2, norm1_g[l].reshape(1, D), sc1, sh1, w_in_b[l], n_main, w_gate, S)

        def gate_row(p):
            grp = jnp.concatenate([p.reshape(-1), jnp.zeros((2 * HB,), F32)])
            return jnp.pad(jnp.tile(grp, 4), (0, LANES - 16 * HB)).reshape(1, LANES)

        pack = _gating(gat, gate_row(A_log[l]), gate_row(dt_bias[l]), HB)

        oa = _attention(proj, jnp.tile(qn_g[l], 2).reshape(1, LANES), jnp.tile(kn_g[l], 2).reshape(1, LANES),
                        bias_prof, B, S, HA, hda)
        cw4 = jnp.transpose(conv_w[l].reshape(conv_w.shape[1], 3, HB, hdb), (2, 1, 0, 3))
        ob = _gdn(proj, cw4, pack.reshape(B, S, LANES), onorm_g[l].reshape(1, LANES), B, S, HB, 3 * DA // LANES)

        wr = _hilo_weights(jnp.pad(jnp.concatenate([w_re[l], w_rg[l]], axis=1), ((0, 0), (0, LANES - NE - NG))))
        br = jnp.pad(jnp.concatenate([b_re[l], b_rg[l]]), (0, LANES - NE - NG)).reshape(1, LANES)
        x1, hpk, logits = _outproj(oa, ob, x2, gt1, w_out[l].astype(BF16), norm2_g[l].reshape(1, D),
                                   sc2, sh2, wr, br, S)
        route, info = _route(logits, NE, NG)
        dest = route[:, 4:4 + TOP_K].astype(I32).reshape(TOP_K * T)
        we, ws, wn, wt = _moe_schedule(info, NE, p_rows)
        xs = _dispatch(dest, hpk, p_rows)
        ypk = _moe(we, ws, wn, wt, xs, w1[l], w3[l], w2[l])
        x2 = _combine(dest, x1, route, gt2, ypk, S)
    return x2.reshape(B, S, D)
```

```python
import functools

import numpy as np
import jax
import jax.numpy as jnp
from jax import lax
from jax.experimental import pallas as pl
from jax.experimental.pallas import tpu as pltpu

F32 = jnp.float32
BF16 = jnp.bfloat16
I32 = jnp.int32
U32 = jnp.uint32
HIGHEST = lax.Precision.HIGHEST

EPS = 1e-6
NEG = -1e30
DILATED_BRANCHES = ((128, 1), (512, 4), (2048, 16))
REL_MAX_DIST = 1024
CHUNK = 64
TOP_K = 2

LANES = 128
SUBLANES = 8
MIB = 1 << 20

ATT_QB = 128
ATT_KW = 256
ATT_UNROLL = 8
ATT_PROF_W = 512
GDN_HP = 2
GDN_PREP_UNROLL = 4
GDN_SEG = 512
MOE_RB = 256
MOE_RMAX = 1024
MOE_CW = 512
ROW_TILE = 256
DMA_UNROLL = 8


def _cparams(sem, vmem_mib):
    return pltpu.CompilerParams(dimension_semantics=sem, vmem_limit_bytes=vmem_mib * MIB)


def _dot(a, b, **kw):
    return jnp.dot(a, b, preferred_element_type=F32, **kw)


def _dot_nt(a, b):
    return lax.dot_general(a, b, (((1,), (1,)), ((), ())), preferred_element_type=F32)


def _pick_tile(n, prefs):
    for t in prefs:
        if n % t == 0:
            return t
    return n


def _pack_halves(x):
    half = x.shape[1] // 2
    lo = lax.bitcast_convert_type(x[:, :half].astype(BF16).astype(F32), U32)
    hi = lax.bitcast_convert_type(x[:, half:].astype(BF16).astype(F32), U32)
    return lax.shift_right_logical(lo, jnp.uint32(16)) | (hi & jnp.uint32(0xFFFF0000))


def _hilo_weights(w):
    hi = w.astype(BF16)
    lo = (w - hi.astype(F32)).astype(BF16)
    return jnp.concatenate([hi, lo], axis=1)


def _dot_hilo(x, w2):
    m, n = x.shape[0], w2.shape[1] // 2
    hi = x.astype(BF16)
    lo = (x - hi.astype(F32)).astype(BF16)
    r = _dot(jnp.concatenate([hi, lo], axis=0), w2)
    return r[:m, :n] + (r[:m, n:] + r[m:, :n])


def _unpack_halves(p):
    lo = lax.bitcast_convert_type(lax.shift_left(p, jnp.uint32(16)), F32)
    hi = lax.bitcast_convert_type(p & jnp.uint32(0xFFFF0000), F32)
    return lo, hi


def _ada_kernel(c_ref, w_ref, b_ref, o_ref):
    c = c_ref[...]
    s = (c * jax.nn.sigmoid(c)).astype(BF16)
    o_ref[...] = _dot(s, w_ref[...].astype(BF16)) + b_ref[...]


def _ada(c, w_ada, b_ada):
    B, D = c.shape
    N = w_ada.shape[1]
    tn = _pick_tile(N, (1024, 512, 256, 128))
    return pl.pallas_call(
        _ada_kernel,
        grid=(N // tn,),
        in_specs=[pl.BlockSpec((B, D), lambda j: (0, 0)),
                  pl.BlockSpec((D, tn), lambda j: (0, j)),
                  pl.BlockSpec((1, tn), lambda j: (0, j))],
        out_specs=pl.BlockSpec((B, tn), lambda j: (0, j)),
        out_shape=jax.ShapeDtypeStruct((B, N), F32),
        compiler_params=_cparams(("arbitrary",), 40),
    )(c, w_ada, b_ada.reshape(1, N))


def _inproj_kernel(x_ref, g_ref, sc_ref, sh_ref, w_ref, wg_ref, o_ref, og_ref, h_scr):
    @pl.when(pl.program_id(1) == 0)
    def _():
        x = x_ref[...]
        y = x * lax.rsqrt(jnp.mean(x * x, axis=-1, keepdims=True) + EPS) * g_ref[...]
        h = y * (1.0 + sc_ref[...]) + sh_ref[...]
        h_scr[...] = h.astype(BF16)
        og_ref[...] = _dot_hilo(h, wg_ref[...])

    o_ref[...] = _dot(h_scr[...], w_ref[...]).astype(o_ref.dtype)


def _inproj(x2, g, sc, sh, w_all, n_main, w_gate, S):
    T, D = x2.shape
    NM = n_main
    tm = _pick_tile(S, (1024, 512, 256, 128))
    tn = _pick_tile(NM, (1024, 512, 256, 128))
    per_b = S // tm
    return pl.pallas_call(
        _inproj_kernel,
        grid=(T // tm, NM // tn),
        in_specs=[pl.BlockSpec((tm, D), lambda i, j: (i, 0)),
                  pl.BlockSpec((1, D), lambda i, j: (0, 0)),
                  pl.BlockSpec((None, 1, D), lambda i, j: (i // per_b, 0, 0)),
                  pl.BlockSpec((None, 1, D), lambda i, j: (i // per_b, 0, 0)),
                  pl.BlockSpec((D, tn), lambda i, j: (0, j)),
                  pl.BlockSpec((D, 2 * LANES), lambda i, j: (0, 0))],
        out_specs=[pl.BlockSpec((tm, tn), lambda i, j: (i, j)),
                   pl.BlockSpec((tm, LANES), lambda i, j: (i, 0))],
        out_shape=[jax.ShapeDtypeStruct((T, NM), BF16), jax.ShapeDtypeStruct((T, LANES), F32)],
        scratch_shapes=[pltpu.VMEM((tm, D), BF16)],
        compiler_params=_cparams(("arbitrary", "arbitrary"), 56),
    )(x2, g, sc, sh, w_all, w_gate)


def _gating_kernel(gat_ref, a_ref, dt_ref, o_ref, *, hb):
    gat = gat_ref[...]
    tm = gat.shape[0]
    gw = 4 * hb
    g = -jnp.exp(a_ref[...]) * jax.nn.softplus(gat + dt_ref[...])
    beta = jax.nn.sigmoid(gat)
    r = lax.broadcasted_iota(I32, (tm, tm), 0)
    c = lax.broadcasted_iota(I32, (tm, tm), 1)
    same = (r // CHUNK) == (c // CHUNK)
    pre = _dot(jnp.where(same & (c <= r), 1.0, 0.0), g, precision=HIGHEST)
    suf = _dot(jnp.where(same & (c >= r), 1.0, 0.0), g, precision=HIGHEST)
    lane = lax.broadcasted_iota(I32, gat.shape, 1)
    o_ref[...] = jnp.where(lane // gw == 0, jnp.where(lane % gw < hb, pre, suf), beta)


def _gating(gat, a_row, dt_row, hb):
    T = gat.shape[0]
    tm = ROW_TILE
    return pl.pallas_call(
        functools.partial(_gating_kernel, hb=hb),
        grid=(T // tm,),
        in_specs=[pl.BlockSpec((tm, LANES), lambda i: (i, 0)),
                  pl.BlockSpec((1, LANES), lambda i: (0, 0)),
                  pl.BlockSpec((1, LANES), lambda i: (0, 0))],
        out_specs=pl.BlockSpec((tm, LANES), lambda i: (i, 0)),
        out_shape=jax.ShapeDtypeStruct((T, LANES), F32),
        compiler_params=_cparams(("arbitrary",), 16),
    )(gat, a_row, dt_row)


def _t5_bucket(rel, n_buckets):
    half = n_buckets // 2
    max_exact = half // 2
    n = np.abs(rel)
    large = max_exact + (np.log(np.maximum(n, 1) / max_exact) / np.log(REL_MAX_DIST / max_exact)
                         * (half - max_exact)).astype(np.int32)
    large = np.minimum(large, half - 1)
    return (np.where(rel > 0, half, 0) + np.where(n < max_exact, n, large)).astype(np.int32)


def _attn_plan(S):
    plan, base = [], 0
    for window, dil in DILATED_BRANCHES:
        n = window // (2 * dil)
        L = S // dil
        assert L % ATT_QB == 0 and n * 2 == ATT_QB
        kw = min(ATT_KW, L)
        nbq = L // ATT_QB
        nvar = 1 if nbq == 1 else 3
        plan.append((dil, L, nbq, kw, base, nvar, n))
        base += nvar
    return tuple(plan), base


def _attn_bias_profiles(rel_bias, S):
    plan, nvar_total = _attn_plan(S)
    nbuckets, H = rel_bias.shape
    u = np.arange(ATT_PROF_W) - ATT_QB
    onehots, bands = [], []
    for dil, L, nbq, kw, base, nvar, n in plan:
        offs = [0] if nvar == 1 else [0, -n, -(kw - ATT_QB)]
        for off in offs:
            rel = off + u
            onehots.append(np.eye(nbuckets, dtype=np.float32)[_t5_bucket(rel * dil, nbuckets)])
            bands.append(np.abs(rel) <= n)
    onehot = jnp.asarray(np.stack(onehots))
    band = jnp.asarray(np.stack(bands))
    prof = jnp.einsum('vwn,nh->hvw', onehot, rel_bias.astype(F32), precision=HIGHEST)
    prof = jnp.where(band[None], prof, NEG)
    prof = prof.reshape(H // 2, 2, nvar_total, ATT_PROF_W)
    return jnp.transpose(prof, (0, 2, 1, 3)).reshape(H // 2, 2 * nvar_total, ATT_PROF_W)


def _attn_kernel(q_ref, k_ref, v_ref, qg_ref, kg_ref, prof_ref, o_ref,
                 qn_scr, kn_scr, v_scr, ob_scr, mb_scr, db_scr, bias_ref, *, plan, hd):
    S = q_ref.shape[0]
    lane = lax.broadcasted_iota(I32, (1, LANES), 1)
    left = lane < hd

    @pl.when(pl.program_id(1) == 0)
    def _():
        for row in range(prof_ref.shape[0]):
            rep = jnp.broadcast_to(prof_ref[row:row + 1, :], (ATT_QB, ATT_PROF_W))
            skew = pltpu.roll(rep, 0, 1, stride=1, stride_axis=0)
            bias_ref[row // 2, row % 2] = skew[:, ATT_QB:ATT_QB + ATT_KW]

    def headnorm(x, g):
        x2 = x * x
        s_all = jnp.sum(x2, axis=-1, keepdims=True)
        s_left = jnp.sum(jnp.where(left, x2, 0.0), axis=-1, keepdims=True)
        ms = jnp.where(left, s_left, s_all - s_left) * (1.0 / hd)
        return x * lax.rsqrt(ms + EPS) * g

    qn_scr[...] = headnorm(q_ref[...].astype(F32), qg_ref[...]) * (hd ** -0.5)
    kn_scr[...] = headnorm(k_ref[...].astype(F32), kg_ref[...])
    v_scr[...] = v_ref[...].astype(F32)

    for bi, (dil, L, nbq, kw, base, nvar, n) in enumerate(plan):
        ones = jnp.ones((kw, LANES), BF16)

        def body(t, carry, dil=dil, L=L, nbq=nbq, kw=kw, base=base, nvar=nvar, n=n, bi=bi, ones=ones):
            blocks = []
            for uu in range(ATT_UNROLL):
                idx = t * ATT_UNROLL + uu
                r = idx // nbq
                i = idx % nbq
                q0 = i * ATT_QB
                k0 = jnp.clip(q0 - n, 0, L - kw)
                var = base if nvar == 1 else base + jnp.where(i > 0, 1, 0) + jnp.where(i == nbq - 1, 1, 0)
                if dil == 1:
                    qrows = pl.ds(pl.multiple_of(q0, ATT_QB), ATT_QB)
                    krows = pl.ds(pl.multiple_of(k0, CHUNK), kw)
                else:
                    qrows = pl.ds(r + q0 * dil, ATT_QB, stride=dil)
                    krows = pl.ds(r + k0 * dil, kw, stride=dil)
                qb = qn_scr[qrows, :]
                q2 = jnp.concatenate([jnp.where(left, qb, 0.0), jnp.where(left, 0.0, qb)], axis=0).astype(BF16)
                blocks.append((qrows, krows, var, q2))
            scores = [_dot_nt(q2, kn_scr[krows, :].astype(BF16)) for qrows, krows, var, q2 in blocks]
            probs, maxes = [], []
            for (qrows, krows, var, q2), s in zip(blocks, scores):
                s = s + jnp.concatenate([bias_ref[var, 0][:, :kw], bias_ref[var, 1][:, :kw]], axis=0)
                m = jnp.max(s, axis=-1, keepdims=True)
                probs.append(jnp.exp(s - m).astype(BF16))
                maxes.append(m)
            outs = [_dot(p, jnp.concatenate([v_scr[krows, :].astype(BF16), ones], axis=1))
                    for (qrows, krows, var, q2), p in zip(blocks, probs)]
            for (qrows, krows, var, q2), m, od in zip(blocks, maxes, outs):
                mb = jnp.broadcast_to(m, (2 * ATT_QB, LANES))
                ob_scr[bi, qrows, :] = jnp.where(left, od[:ATT_QB, :LANES], od[ATT_QB:, :LANES])
                mb_scr[bi, qrows, :] = jnp.where(left, mb[:ATT_QB], mb[ATT_QB:])
                db_scr[bi, qrows, :] = jnp.where(left, od[:ATT_QB, LANES:], od[ATT_QB:, LANES:])
            return carry

        assert (dil * nbq) % ATT_UNROLL == 0
        lax.fori_loop(0, dil * nbq // ATT_UNROLL, body, 0)

    nb = len(plan)
    mx = mb_scr[0]
    for bi in range(1, nb):
        mx = jnp.maximum(mx, mb_scr[bi])
    num = jnp.zeros((S, LANES), F32)
    den = jnp.zeros((S, LANES), F32)
    for bi in range(nb):
        w = jnp.exp(mb_scr[bi] - mx)
        num = num + w * ob_scr[bi]
        den = den + w * db_scr[bi]
    o_ref[...] = (num / den).astype(o_ref.dtype)


def _attention(proj, qg2, kg2, profiles, B, S, HA, hd):
    T = proj.shape[0]
    pairs = HA // 2
    da_blocks = HA * hd // LANES
    plan, nvar = _attn_plan(S)
    return pl.pallas_call(
        functools.partial(_attn_kernel, plan=plan, hd=hd),
        grid=(pairs, B),
        in_specs=[pl.BlockSpec((S, LANES), lambda p, b: (b, p)),
                  pl.BlockSpec((S, LANES), lambda p, b: (b, da_blocks + p)),
                  pl.BlockSpec((S, LANES), lambda p, b: (b, 2 * da_blocks + p)),
                  pl.BlockSpec((1, LANES), lambda p, b: (0, 0)),
                  pl.BlockSpec((1, LANES), lambda p, b: (0, 0)),
                  pl.BlockSpec((None, 2 * nvar, ATT_PROF_W), lambda p, b: (p, 0, 0))],
        out_specs=pl.BlockSpec((S, LANES), lambda p, b: (b, p)),
        out_shape=jax.ShapeDtypeStruct((T, HA * hd), BF16),
        scratch_shapes=[pltpu.VMEM((S, LANES), F32)] * 3 + [pltpu.VMEM((len(plan), S, LANES), F32)] * 3
        + [pltpu.VMEM((nvar, 2, ATT_QB, ATT_KW), F32)],
        compiler_params=_cparams(("arbitrary", "arbitrary"), 40),
    )(proj, proj, proj, qg2, kg2, profiles)


def _gdn_kernel(q_ref, k_ref, v_ref, z_ref, cw_ref, pack_ref, og_ref, o_ref,
                q_scr, k_scr, v_scr, xpad_scr, pk2_scr, sel_scr, u_scr, wq_scr, at_scr, kdt_scr, et_scr, oacc_scr,
                *, hb, hp):
    S = q_ref.shape[0]
    P2 = 2 * CHUNK
    W2 = 2 * LANES
    npair = S // P2
    hg = pl.program_id(1)
    gw = 4 * hb
    dk = LANES

    pad = SUBLANES
    for slot in range(2):
        xpad_scr[slot, 0:pad, :] = jnp.zeros((pad, LANES), F32)
        xpad_scr[slot, pad + S:, :] = jnp.zeros((pad, LANES), F32)

    seg = GDN_SEG if S % GDN_SEG == 0 else S

    def conv_silu_to(src_ref, lanes_j, j, which, dst_scr, l2norm, scale):
        xp = xpad_scr.at[(3 * j + which) % 2]
        for s0 in range(0, S, seg):
            xp[pad + s0:pad + s0 + seg, :] = src_ref[s0:s0 + seg, lanes_j].astype(F32)
        for s0 in range(0, S, seg):
            acc = xp[pad + s0:pad + s0 + seg, :] * cw_ref[j, which, 2:3, :]
            for d in (-2, -1, 1, 2):
                acc = acc + xp[pad + s0 + d:pad + s0 + d + seg, :] * cw_ref[j, which, 2 + d:3 + d, :]
            y = acc * jax.nn.sigmoid(acc)
            if l2norm:
                y = y * (lax.rsqrt(jnp.sum(y * y, axis=-1, keepdims=True) + EPS) * scale)
            dst_scr[j, s0:s0 + seg, :] = y

    for s0 in range(0, S, seg):
        pk = pack_ref[s0:s0 + seg, :]
        p_hi = pk.astype(BF16)
        p_lo = (pk - p_hi.astype(F32)).astype(BF16)
        pk2_scr[s0:s0 + seg, :] = jnp.concatenate([p_hi, p_lo], axis=1)
    srow = lax.broadcasted_iota(I32, (W2, 4 * LANES), 0) % LANES
    scol = lax.broadcasted_iota(I32, (W2, 4 * LANES), 1) // LANES
    for j in range(hp):
        lanes_j = slice(j * LANES, (j + 1) * LANES)
        conv_silu_to(q_ref, lanes_j, j, 0, q_scr, True, dk ** -0.5)
        conv_silu_to(k_ref, lanes_j, j, 1, k_scr, True, 1.0)
        conv_silu_to(v_ref, lanes_j, j, 2, v_scr, False, 1.0)
        src = hg * hp + j + jnp.where(scol < 2, scol * hb, gw + 2 * hb + (scol - 2) * hb)
        sel_scr[j] = jnp.where(srow == src, 1.0, 0.0).astype(BF16)
        oacc_scr[j] = jnp.zeros((S, LANES), F32)

    r4 = lax.broadcasted_iota(I32, (CHUNK, W2), 0)
    l4 = lax.broadcasted_iota(I32, (CHUNK, W2), 1)
    c4 = l4 % CHUNK
    blk4 = l4 // CHUNK
    lo_half = (l4 % LANES) < CHUNK
    ahead = jnp.where(l4 >= LANES, r4 - c4, c4 - r4)
    incl = ahead <= 0
    strict = ahead < 0
    bd16 = (r4 // 16) == (c4 // 16)

    def squeeze(x):
        return jnp.where(lo_half, x[:CHUNK], x[CHUNK:])

    def unsqueeze(x):
        return jnp.concatenate([jnp.where(lo_half, x, jnp.zeros_like(x)),
                                jnp.where(lo_half, jnp.zeros_like(x), x)], axis=0)

    def mm4(a, b):
        rhs = jnp.concatenate([jnp.where(blk4 == g, b, 0.0) for g in range(4)], axis=0)
        return _dot(a.astype(BF16), rhs.astype(BF16))

    first = lax.broadcasted_iota(I32, (P2, 1), 0) < CHUNK
    zpair = jnp.zeros((P2, LANES), BF16)

    def bdiag(x):
        return jnp.concatenate([jnp.concatenate([x[:, :LANES], zpair], axis=1),
                                jnp.concatenate([zpair, x[:, LANES:]], axis=1)], axis=0)

    U = GDN_PREP_UNROLL if npair % GDN_PREP_UNROLL == 0 else 1

    def prep(t, carry):
        cx = []
        chains = [(t * U + u, j) for u in range(U) for j in range(hp)]
        bcs = [_dot(pk2_scr[pl.ds(pl.multiple_of(m * P2, P2), P2), :], sel_scr[j]) for m, j in chains]
        for (m, j), bc in zip(chains, bcs):
            rows = pl.ds(pl.multiple_of(m * P2, P2), P2)
            kp = k_scr[j, rows, :]
            qp = q_scr[j, rows, :]
            vp = v_scr[j, rows, :]
            gc2 = bc[:, 0:W2]
            beta2 = bc[:, W2:2 * W2]
            gcf, gcb = gc2[:, :LANES], gc2[:, LANES:]
            tot2 = jnp.concatenate([jnp.where(first, gcf[CHUNK - 1:CHUNK], gcf[P2 - 1:P2]),
                                    jnp.where(first, gcb[0:1], gcb[CHUNK:CHUNK + 1])], axis=1)
            egc2 = jnp.exp(gc2)
            k2 = jnp.concatenate([kp, kp], axis=1)
            kb2 = k2 * beta2
            vb2 = jnp.concatenate([vp, vp], axis=1) * beta2
            kbe2 = kb2 * egc2
            cx.append(dict(
                m=m, j=j, rows=rows, vb2=vb2, kbe2=kbe2,
                qeb=(jnp.concatenate([qp, qp], axis=1) * egc2).astype(BF16),
                kd2=k2 * jnp.exp(tot2 - gc2),
                et=jnp.exp(tot2),
                dec=jnp.exp(jnp.where(
                    incl, squeeze(gc2) - squeeze(jnp.concatenate([gcf.T, gcb.T], axis=1)), -jnp.inf)),
                stk=jnp.concatenate([kb2[:, :LANES], kb2[:, LANES:], qp], axis=0).astype(BF16),
                kpb=kp.astype(BF16)))

        g3s = [_dot_nt(c['stk'], c['kpb']) for c in cx]
        for c, g3 in zip(cx, g3s):
            lm = jnp.where(strict, squeeze(jnp.concatenate([g3[:P2], g3[P2:2 * P2]], axis=1)) * c['dec'], 0.0)
            attn = squeeze(jnp.concatenate([g3[2 * P2:], g3[2 * P2:]], axis=1)) * c['dec']
            c['attn2'] = unsqueeze(attn).astype(BF16)
            c['lbd'] = jnp.where(bd16, lm, 0.0)
            c['loff'] = lm - c['lbd']
        nn = [-c['lbd'] for c in cx]
        pw = [mm4(c['lbd'], c['lbd']) for c in cx]
        for rnd in range(3):
            prod = [mm4(a, p) for a, p in zip(nn, pw)]
            nxt = [mm4(p, p) for p in pw] if rnd < 2 else pw
            nn = [a + p + q for a, p, q in zip(nn, pw, prod)]
            pw = nxt
        mo = [c['loff'] + x for c, x in zip(cx, [mm4(a, c['loff']) for a, c in zip(nn, cx)])]
        m2 = [mm4(x, x) for x in mo]
        mn = [mm4(x, a) for x, a in zip(mo, nn)]
        xo = [a - x - y for a, x, y in zip(nn, mo, mn)]
        mx = [mm4(a, b) for a, b in zip(m2, xo)]
        toff = [a + b + q for a, b, q in zip(xo, m2, mx)]
        z4 = jnp.zeros((CHUNK, W2), BF16)
        tws = []
        for c, tf in zip(cx, toff):
            vbb, kbb = c['vb2'].astype(BF16), c['kbe2'].astype(BF16)
            blocks = []
            for g in range(4):
                rr = slice((g % 2) * CHUNK, (g % 2 + 1) * CHUNK)
                ll = slice((g // 2) * LANES, (g // 2 + 1) * LANES)
                blocks.append(jnp.concatenate([z4] * g + [vbb[rr, ll], kbb[rr, ll]] + [z4] * (3 - g), axis=1))
            tws.append(_dot(tf.astype(BF16), jnp.concatenate(blocks, axis=0)))
        for c, tw in zip(cx, tws):
            m, j, rows = c['m'], c['j'], c['rows']
            def pair_layout(off):
                piece = lambda g: tw[:, 2 * g * LANES + off:2 * g * LANES + off + LANES]
                return jnp.concatenate([jnp.concatenate([piece(0), piece(2)], axis=1),
                                        jnp.concatenate([piece(1), piece(3)], axis=1)], axis=0)

            u2 = c['vb2'] + pair_layout(0)
            w2 = (c['kbe2'] + pair_layout(LANES)).astype(BF16)
            qeb = c['qeb']
            u_scr[j, rows, :] = u2
            wq_scr[j, pl.ds(pl.multiple_of(m * 2 * P2, 2 * P2), 2 * P2), :] = jnp.concatenate(
                [w2[:CHUNK], qeb[:CHUNK], w2[CHUNK:], qeb[CHUNK:]], axis=0)
            at_scr[j, rows, :] = c['attn2']
            kdt_scr[j, 0, :, rows] = c['kd2'][:, :LANES].T.astype(BF16)
            kdt_scr[j, 1, :, rows] = c['kd2'][:, LANES:].T.astype(BF16)
            et_scr[j, pl.ds(pl.multiple_of(m * 2 * SUBLANES, 2 * SUBLANES), 2 * SUBLANES), :] = jnp.concatenate(
                [c['et'][:SUBLANES], c['et'][CHUNK:CHUNK + SUBLANES]], axis=0)
        return carry

    lax.fori_loop(0, npair // U, prep, 0)

    zc = jnp.zeros((CHUNK, LANES), F32)
    zp = jnp.zeros((P2, LANES), F32)

    def place(v, cpos):
        return jnp.concatenate([v, zc] if cpos == 0 else [zc, v], axis=0)

    def scan(m, states):
        pf = m
        pb = npair - 1 - m
        rows_f = pl.ds(pl.multiple_of(pf * P2, P2), P2)
        rows_b = pl.ds(pl.multiple_of(pb * P2, P2), P2)
        hx = []
        for j in range(hp):
            hx.append(dict(
                u_f=u_scr[j, rows_f, :LANES], u_b=u_scr[j, rows_b, LANES:],
                at_f=at_scr[j, rows_f, :LANES], at_b=at_scr[j, rows_b, LANES:],
                kdt=jnp.concatenate([kdt_scr[j, 0, :, rows_f], kdt_scr[j, 1, :, rows_b]], axis=1),
                wq_f=wq_scr[j, pl.ds(pl.multiple_of(pf * 2 * P2, 2 * P2), 2 * P2), :LANES],
                wq_b=wq_scr[j, pl.ds(pl.multiple_of(pb * 2 * P2, 2 * P2), 2 * P2), LANES:],
                et_f=et_scr[j, pl.ds(pl.multiple_of(pf * 2 * SUBLANES, 2 * SUBLANES), 2 * SUBLANES), :LANES],
                et_b=et_scr[j, pl.ds(pl.multiple_of(pb * 2 * SUBLANES, 2 * SUBLANES), 2 * SUBLANES), LANES:]))
        sts = list(states)
        for step in range(2):
            cf, cb = step, 1 - step
            rrs = [_dot(jnp.concatenate([c['wq_f'][cf * P2:(cf + 1) * P2], c['wq_b'][cb * P2:(cb + 1) * P2]], axis=1),
                        bdiag(st.astype(BF16))) for c, st in zip(hx, sts)]
            ress = []
            for c, rr in zip(hx, rrs):
                u2 = jnp.concatenate([c['u_f'][cf * CHUNK:(cf + 1) * CHUNK],
                                      c['u_b'][cb * CHUNK:(cb + 1) * CHUNK]], axis=1)
                v_new = u2 - rr[:CHUNK]
                rhs = jnp.concatenate(
                    [jnp.concatenate([place(v_new[:, :LANES], cf), zp], axis=1),
                     jnp.concatenate([zp, place(v_new[:, LANES:], cb)], axis=1)], axis=0).astype(BF16)
                lhs = jnp.concatenate(
                    [jnp.concatenate([c['at_f'][cf * CHUNK:(cf + 1) * CHUNK],
                                      c['at_b'][cb * CHUNK:(cb + 1) * CHUNK]], axis=1),
                     c['kdt']], axis=0)
                ress.append(_dot(lhs, rhs))
            for j, (c, rr, res) in enumerate(zip(hx, rrs, ress)):
                o2 = rr[CHUNK:] + res[:CHUNK]
                et2 = jnp.concatenate([c['et_f'][cf * SUBLANES:cf * SUBLANES + 1],
                                       c['et_b'][cb * SUBLANES:cb * SUBLANES + 1]], axis=1)
                sts[j] = sts[j] * et2 + res[CHUNK:]
                of_rows = pl.ds(pl.multiple_of(pf * P2 + cf * CHUNK, CHUNK), CHUNK)
                ob_rows = pl.ds(pl.multiple_of(pb * P2 + cb * CHUNK, CHUNK), CHUNK)
                oacc_scr[j, of_rows, :] = oacc_scr[j, of_rows, :] + o2[:, :LANES]
                oacc_scr[j, ob_rows, :] = oacc_scr[j, ob_rows, :] + o2[:, LANES:]
        return tuple(sts)

    s0 = jnp.zeros((dk, W2), F32)
    lax.fori_loop(0, npair, scan, (s0,) * hp)

    for j in range(hp):
        lanes_j = slice(j * LANES, (j + 1) * LANES)
        for s0 in range(0, S, seg):
            o = oacc_scr[j, s0:s0 + seg, :]
            y = o * lax.rsqrt(jnp.mean(o * o, axis=-1, keepdims=True) + EPS) * og_ref[...]
            z = z_ref[s0:s0 + seg, lanes_j].astype(F32)
            o_ref[s0:s0 + seg, lanes_j] = (y * (z * jax.nn.sigmoid(z))).astype(o_ref.dtype)


def _gdn(proj, cw4, pack3, onorm_g, B, S, HB, base_blk):
    T = proj.shape[0]
    hp = GDN_HP if (HB % GDN_HP == 0 and base_blk % GDN_HP == 0) else 1
    wblk = hp * LANES
    npair = S // (2 * CHUNK)

    def col(k):
        off = (base_blk + k * HB) // hp
        return lambda b, h: (b, off + h)

    return pl.pallas_call(
        functools.partial(_gdn_kernel, hb=HB, hp=hp),
        grid=(B, HB // hp),
        in_specs=[pl.BlockSpec((S, wblk), col(0)),
                  pl.BlockSpec((S, wblk), col(1)),
                  pl.BlockSpec((S, wblk), col(2)),
                  pl.BlockSpec((S, wblk), col(3)),
                  pl.BlockSpec((hp, 3, cw4.shape[2], LANES), lambda b, h: (h, 0, 0, 0)),
                  pl.BlockSpec((None, S, LANES), lambda b, h: (b, 0, 0)),
                  pl.BlockSpec((1, LANES), lambda b, h: (0, 0))],
        out_specs=pl.BlockSpec((S, wblk), lambda b, h: (b, h)),
        out_shape=jax.ShapeDtypeStruct((T, HB * LANES), BF16),
        scratch_shapes=[pltpu.VMEM((hp, S, LANES), F32)] * 3
        + [pltpu.VMEM((2, S + 2 * SUBLANES, LANES), F32),
           pltpu.VMEM((S, 2 * LANES), BF16),
           pltpu.VMEM((hp, 2 * LANES, 4 * LANES), BF16),
           pltpu.VMEM((hp, S, 2 * LANES), F32),
           pltpu.VMEM((hp, 2 * S, 2 * LANES), BF16),
           pltpu.VMEM((hp, S, 2 * LANES), BF16),
           pltpu.VMEM((hp, 2, LANES, S), BF16),
           pltpu.VMEM((hp, npair * 2 * SUBLANES, 2 * LANES), F32),
           pltpu.VMEM((hp, S, LANES), F32)],
        compiler_params=_cparams(("arbitrary", "arbitrary"), 56),
    )(proj, proj, proj, proj, cw4, pack3, onorm_g)


def _outproj_kernel(oa_ref, ob_ref, x_ref, gt_ref, w_ref, g_ref, sc_ref, sh_ref, wr_ref, br_ref,
                    x1_ref, hp_ref, lg_ref, h_scr):
    da = oa_ref.shape[1]
    y = _dot(oa_ref[...], w_ref[:da, :]) + _dot(ob_ref[...], w_ref[da:, :])
    x1 = x_ref[...] + gt_ref[...] * y
    x1_ref[...] = x1
    hn = x1 * lax.rsqrt(jnp.mean(x1 * x1, axis=-1, keepdims=True) + EPS) * g_ref[...]
    h = hn * (1.0 + sc_ref[...]) + sh_ref[...]
    hp_ref[...] = _pack_halves(h)
    h_scr[...] = h

    @pl.when(pl.program_id(0) < pl.num_programs(0))
    def _():
        lg_ref[...] = _dot_hilo(h_scr[...], wr_ref[...]) + br_ref[...]


def _outproj(oa, ob, x2, gt1, w_out_b, g2, sc2, sh2, wr, br, S):
    T, D = x2.shape
    tm = ROW_TILE
    per_b = S // tm
    bmap = lambda i: (i // per_b, 0, 0)
    return pl.pallas_call(
        _outproj_kernel,
        grid=(T // tm,),
        in_specs=[pl.BlockSpec((tm, oa.shape[1]), lambda i: (i, 0)),
                  pl.BlockSpec((tm, ob.shape[1]), lambda i: (i, 0)),
                  pl.BlockSpec((tm, D), lambda i: (i, 0)),
                  pl.BlockSpec((None, 1, D), bmap),
                  pl.BlockSpec((D, D), lambda i: (0, 0)),
                  pl.BlockSpec((1, D), lambda i: (0, 0)),
                  pl.BlockSpec((None, 1, D), bmap),
                  pl.BlockSpec((None, 1, D), bmap),
                  pl.BlockSpec((D, 2 * LANES), lambda i: (0, 0)),
                  pl.BlockSpec((1, LANES), lambda i: (0, 0))],
        out_specs=[pl.BlockSpec((tm, D), lambda i: (i, 0)),
                   pl.BlockSpec((tm, D // 2), lambda i: (i, 0)),
                   pl.BlockSpec((tm, LANES), lambda i: (i, 0))],
        out_shape=[jax.ShapeDtypeStruct((T, D), F32), jax.ShapeDtypeStruct((T, D // 2), U32),
                   jax.ShapeDtypeStruct((T, LANES), F32)],
        scratch_shapes=[pltpu.VMEM((tm, D), F32)],
        compiler_params=_cparams(("arbitrary",), 48),
    )(oa, ob, x2, gt1, w_out_b, g2, sc2, sh2, wr, br)


def _route_kernel(lg_ref, o_ref, info_ref, run_scr, *, ne, ng):
    ph = pl.program_id(0)
    i = pl.program_id(1)

    @pl.when((ph == 0) & (i == 0))
    def _():
        run_scr[...] = jnp.zeros_like(run_scr)

    @pl.when((ph == 1) & (i == 0))
    def _():
        cnt = run_scr[...]
        padded = jnp.ceil(cnt * (1.0 / MOE_RB)) * MOE_RB
        k = lax.broadcasted_iota(I32, (LANES, LANES), 0)
        e = lax.broadcasted_iota(I32, (LANES, LANES), 1)
        start = _dot(padded, jnp.where(k < e, 1.0, 0.0), precision=HIGHEST)
        rowi = lax.broadcasted_iota(I32, cnt.shape, 0)
        info_ref[...] = jnp.where(rowi == 0, cnt, jnp.where(rowi == 1, padded, start))
        run_scr[...] = start

    lg = lg_ref[...]
    tm = lg.shape[0]
    epg = ne // ng
    lane_i = lax.broadcasted_iota(I32, lg.shape, 1)
    lane = lane_i.astype(F32)
    big = float(2 * LANES)
    is_g = (lane_i >= ne) & (lane_i < ne + ng)
    gl = jnp.where(is_g, lg, -jnp.inf)
    gmax = jnp.max(gl, axis=-1, keepdims=True)
    gidx = jnp.min(jnp.where(gl == gmax, lane, big), axis=-1, keepdims=True) - ne
    psel = 1.0 / jnp.sum(jnp.where(is_g, jnp.exp(gl - gmax), 0.0), axis=-1, keepdims=True)
    in_grp = (lane_i // epg).astype(F32) == gidx
    el = jnp.where(in_grp & (lane_i < ne), lg, -jnp.inf)
    m1 = jnp.max(el, axis=-1, keepdims=True)
    i1 = jnp.min(jnp.where(el == m1, lane, big), axis=-1, keepdims=True)
    el2 = jnp.where(lane == i1, -jnp.inf, el)
    m2 = jnp.max(el2, axis=-1, keepdims=True)
    i2 = jnp.min(jnp.where(el2 == m2, lane, big), axis=-1, keepdims=True)
    e21 = jnp.exp(m2 - m1)
    g1 = psel / (1.0 + e21)
    g2 = psel * e21 / (1.0 + e21)
    o1 = jnp.where(lane == i1, 1.0, 0.0)
    o2 = jnp.where(lane == i2, 1.0, 0.0)
    cnt = o1 + o2
    r = lax.broadcasted_iota(I32, (tm, tm), 0)
    c = lax.broadcasted_iota(I32, (tm, tm), 1)
    before = _dot(jnp.where(c < r, 1.0, 0.0).astype(BF16), cnt.astype(BF16)) + run_scr[0:1, :]
    d1 = jnp.sum(o1 * before, axis=-1, keepdims=True)
    d2 = jnp.sum(o2 * before, axis=-1, keepdims=True)
    run_scr[...] = run_scr[...] + jnp.sum(cnt, axis=0, keepdims=True)

    @pl.when(ph == 1)
    def _():
        out = jnp.zeros(lg.shape, F32)
        for j, val in enumerate((i1, i2, g1, g2, d1, d2)):
            out = jnp.where(lane_i == j, val, out)
        o_ref[...] = out


def _route(logits, ne, ng):
    T = logits.shape[0]
    tm = _pick_tile(T, (1024, 512, 256))
    return pl.pallas_call(
        functools.partial(_route_kernel, ne=ne, ng=ng),
        grid=(2, T // tm),
        in_specs=[pl.BlockSpec((tm, LANES), lambda p, i: (i, 0))],
        out_specs=[pl.BlockSpec((tm, LANES), lambda p, i: (i * p, 0)),
                   pl.BlockSpec((SUBLANES, LANES), lambda p, i: (0, 0))],
        out_shape=[jax.ShapeDtypeStruct((T, LANES), F32), jax.ShapeDtypeStruct((SUBLANES, LANES), F32)],
        scratch_shapes=[pltpu.VMEM((SUBLANES, LANES), F32)],
        compiler_params=_cparams(("arbitrary", "arbitrary"), 16),
    )(logits)


def _dispatch_kernel(dest_ref, h_hbm, xs_in, xs_hbm, sem):
    del xs_in
    tm = ROW_TILE
    base = pl.program_id(0) * tm

    def copy(tok, a):
        return pltpu.make_async_copy(h_hbm.at[pl.ds(tok, 1)], xs_hbm.at[pl.ds(dest_ref[a], 1)], sem)

    def issue(t, carry):
        for k in range(TOP_K):
            copy(base + t, (base + t) * TOP_K + k).start()
        return carry

    lax.fori_loop(0, tm, issue, 0, unroll=DMA_UNROLL)

    for k in range(TOP_K):
        pltpu.make_async_copy(h_hbm.at[pl.ds(0, tm)], xs_hbm.at[pl.ds(0, tm)], sem).wait()


def _dispatch(dest, hpk, p_rows):
    T, dh = hpk.shape
    return pl.pallas_call(
        _dispatch_kernel,
        grid_spec=pltpu.PrefetchScalarGridSpec(
            num_scalar_prefetch=1,
            grid=(T // ROW_TILE,),
            in_specs=[pl.BlockSpec(memory_space=pl.ANY), pl.BlockSpec(memory_space=pl.ANY)],
            out_specs=pl.BlockSpec(memory_space=pl.ANY),
            scratch_shapes=[pltpu.SemaphoreType.DMA(())]),
        out_shape=jax.ShapeDtypeStruct((p_rows, dh), U32),
        input_output_aliases={2: 0},
        compiler_params=_cparams(("arbitrary",), 16),
    )(dest, hpk, jnp.zeros((p_rows, dh), U32))


def _moe_kernel(we_ref, ws_ref, wn_ref, wt_ref, xs_hbm, w1_ref, w3_ref, w2_ref, y_hbm,
                xbuf, hacc, yacc, ypk, sem_in, sem_out):
    w = pl.program_id(0)
    s = pl.program_id(1)
    nw = pl.num_programs(0)
    ns = pl.num_programs(1)
    nrows = wn_ref[w]
    start = ws_ref[w]
    nblk = nrows // MOE_RB
    dh = xbuf.shape[1]
    ncd, dcw = hacc.shape[1], hacc.shape[-1]

    def blk_rows(rb):
        return pl.ds(pl.multiple_of(rb * MOE_RB, MOE_RB), MOE_RB)

    def hbm_rows(item_start, rb):
        return pl.ds(pl.multiple_of(item_start + rb * MOE_RB, MOE_RB), MOE_RB)

    def in_copy(item_start, rb):
        return pltpu.make_async_copy(xs_hbm.at[hbm_rows(item_start, rb)], xbuf.at[blk_rows(rb)], sem_in)

    def out_copy(item_start, rb):
        return pltpu.make_async_copy(ypk.at[blk_rows(rb)], y_hbm.at[hbm_rows(item_start, rb)], sem_out)

    def each_block(n, fn):
        def body(rb, carry):
            fn(rb)
            return carry
        lax.fori_loop(0, n, body, 0)

    @pl.when(s == 0)
    def _load():
        @pl.when(w == 0)
        def _():
            each_block(nblk, lambda rb: in_copy(start, rb).start())

        each_block(nblk, lambda rb: in_copy(start, rb).wait())

        def clear(rb):
            for a in range(2):
                for cc in range(ncd):
                    hacc[a, cc, blk_rows(rb), :] = jnp.zeros((MOE_RB, dcw), F32)
            yacc[blk_rows(rb), :] = jnp.zeros((MOE_RB, 2 * dh), F32)

        each_block(nblk, clear)

    @pl.when((s < 2) & (nrows > 0))
    def _up():
        def blk(rb):
            rows = blk_rows(rb)
            lo, hi = _unpack_halves(xbuf[rows, :])
            xk = jnp.where(s == 0, lo, hi).astype(BF16)
            for a, w_ref in enumerate((w1_ref, w3_ref)):
                h = _dot(xk, w_ref[...].astype(BF16))
                for cc in range(ncd):
                    hacc[a, cc, rows, :] = hacc[a, cc, rows, :] + h[:, cc * dcw:(cc + 1) * dcw]

        each_block(nblk, blk)

    @pl.when((s == 1) & (w + 1 < nw))
    def _prefetch():
        nxt = ws_ref[w + 1]
        each_block(wn_ref[w + 1] // MOE_RB, lambda rb: in_copy(nxt, rb).start())

    @pl.when((s >= 2) & (nrows > 0))
    def _down():
        cc = s - 2

        def blk(rb):
            rows = blk_rows(rb)
            h1 = hacc[0, cc, rows, :]
            h3 = hacc[1, cc, rows, :]
            hid = (h1 * jax.nn.sigmoid(h1) * h3).astype(BF16)
            yacc[rows, :] = yacc[rows, :] + _dot(hid, w2_ref[...].astype(BF16))

        each_block(nblk, blk)

    @pl.when(s == ns - 1)
    def _store():
        @pl.when(w > 0)
        def _():
            prev = ws_ref[w - 1]
            each_block(wn_ref[w - 1] // MOE_RB, lambda rb: out_copy(prev, rb).wait())

        def pack(rb):
            ypk[blk_rows(rb), :] = _pack_halves(yacc[blk_rows(rb), :])

        each_block(nblk, pack)
        each_block(nblk, lambda rb: out_copy(start, rb).start())

        @pl.when(w == nw - 1)
        def _():
            each_block(nblk, lambda rb: out_copy(start, rb).wait())

    @pl.when((w == nw - 1) & (s == ns - 1))
    def _zero_tail():
        ypk[blk_rows(0), :] = jnp.zeros((MOE_RB, dh), U32)
        first = wt_ref[0] // MOE_RB

        def tail_copy(b):
            return pltpu.make_async_copy(ypk.at[blk_rows(0)],
                                         y_hbm.at[pl.ds(pl.multiple_of(b * MOE_RB, MOE_RB), MOE_RB)], sem_out)

        def start(b, carry):
            tail_copy(b).start()
            return carry

        def wait(b, carry):
            tail_copy(b).wait()
            return carry

        lax.fori_loop(first, y_hbm.shape[0] // MOE_RB, start, 0)
        lax.fori_loop(first, y_hbm.shape[0] // MOE_RB, wait, 0)


def _moe(we, ws, wn, wt, xs, w1, w3, w2):
    P, dh = xs.shape
    D = 2 * dh
    NE, _, DE = w1.shape
    cw = min(MOE_CW, DE)
    ncd = DE // cw
    nw = we.shape[0]
    up_map = lambda w, s, we, ws, wn, wt: (we[w], jnp.minimum(s, 1), 0)
    down_map = lambda w, s, we, ws, wn, wt: (we[w], jnp.maximum(s - 2, 0), 0)
    return pl.pallas_call(
        _moe_kernel,
        grid_spec=pltpu.PrefetchScalarGridSpec(
            num_scalar_prefetch=4,
            grid=(nw, 2 + ncd),
            in_specs=[pl.BlockSpec(memory_space=pl.ANY),
                      pl.BlockSpec((None, dh, DE), up_map),
                      pl.BlockSpec((None, dh, DE), up_map),
                      pl.BlockSpec((None, cw, D), down_map)],
            out_specs=pl.BlockSpec(memory_space=pl.ANY),
            scratch_shapes=[pltpu.VMEM((MOE_RMAX, dh), U32),
                            pltpu.VMEM((2, ncd, MOE_RMAX, cw), F32),
                            pltpu.VMEM((MOE_RMAX, D), F32),
                            pltpu.VMEM((MOE_RMAX, dh), U32),
                            pltpu.SemaphoreType.DMA(()),
                            pltpu.SemaphoreType.DMA(())]),
        out_shape=jax.ShapeDtypeStruct((P, dh), U32),
        compiler_params=_cparams(("arbitrary", "arbitrary"), 56),
    )(we, ws, wn, wt, xs, w1, w3, w2)


def _moe_schedule(info, NE, p_rows):
    padded = info[1, :NE].astype(I32)
    start_pad = info[2, :NE].astype(I32)
    items = (padded + MOE_RMAX - 1) // MOE_RMAX
    cum_items = jnp.cumsum(items)
    n_items = cum_items[-1]
    nw = (p_rows + NE * (MOE_RMAX - MOE_RB)) // MOE_RMAX
    wi = jnp.arange(nw, dtype=I32)
    valid = wi < n_items
    wi_c = jnp.minimum(wi, jnp.maximum(n_items - 1, 0))
    we = jnp.minimum(jnp.searchsorted(cum_items, wi_c, side='right'), NE - 1).astype(I32)
    local = wi_c - (cum_items[we] - items[we])
    ws = (start_pad[we] + local * MOE_RMAX).astype(I32)
    wn = jnp.where(valid, jnp.clip(padded[we] - local * MOE_RMAX, 0, MOE_RMAX), 0).astype(I32)
    wt = (start_pad[NE - 1] + padded[NE - 1]).reshape(1)
    return we, ws, wn, wt


def _combine_kernel(dest_ref, x1_ref, rt_ref, gt_ref, y_hbm, o_ref, ybuf, sems):
    tm = ROW_TILE
    i = pl.program_id(0)
    n = pl.num_programs(0)
    dh = ybuf.shape[-1]

    def copy(tile, slot, t, k):
        a = (tile * tm + t) * TOP_K + k
        return pltpu.make_async_copy(y_hbm.at[pl.ds(dest_ref[a], 1)], ybuf.at[slot, k, pl.ds(t, 1)], sems.at[slot])

    def issue_tile(tile, slot):
        def body(t, carry):
            for k in range(TOP_K):
                copy(tile, slot, t, k).start()
            return carry
        lax.fori_loop(0, tm, body, 0, unroll=DMA_UNROLL)

    @pl.when(i == 0)
    def _():
        issue_tile(0, 0)

    @pl.when(i + 1 < n)
    def _():
        issue_tile(i + 1, (i + 1) % 2)

    slot = i % 2

    for k in range(TOP_K):
        pltpu.make_async_copy(y_hbm.at[pl.ds(0, tm)], ybuf.at[slot, k], sems.at[slot]).wait()

    rt = rt_ref[...]
    g1 = rt[:, 2:3]
    g2 = rt[:, 3:4]
    lo1, hi1 = _unpack_halves(ybuf[slot, 0])
    lo2, hi2 = _unpack_halves(ybuf[slot, 1])
    o_ref[:, :dh] = x1_ref[:, :dh] + gt_ref[:, :dh] * (g1 * lo1 + g2 * lo2)
    o_ref[:, dh:] = x1_ref[:, dh:] + gt_ref[:, dh:] * (g1 * hi1 + g2 * hi2)


def _combine(dest, x1, route, gt2, ypk, S):
    T, D = x1.shape
    tm = ROW_TILE
    per_b = S // tm
    return pl.pallas_call(
        _combine_kernel,
        grid_spec=pltpu.PrefetchScalarGridSpec(
            num_scalar_prefetch=1,
            grid=(T // tm,),
            in_specs=[pl.BlockSpec((tm, D), lambda i, d: (i, 0)),
                      pl.BlockSpec((tm, LANES), lambda i, d: (i, 0)),
                      pl.BlockSpec((None, 1, D), lambda i, d: (i // per_b, 0, 0)),
                      pl.BlockSpec(memory_space=pl.ANY)],
            out_specs=pl.BlockSpec((tm, D), lambda i, d: (i, 0)),
            scratch_shapes=[pltpu.VMEM((2, TOP_K, tm, D // 2), U32),
                            pltpu.SemaphoreType.DMA((2,))]),
        out_shape=jax.ShapeDtypeStruct((T, D), F32),
        compiler_params=_cparams(("arbitrary",), 32),
    )(dest, x1, route, gt2, ypk)


def kernel(x, c, w_ada, b_ada, norm1_g, norm2_g, w_in, qn_g, kn_g, rel_bias, conv_w, A_log, dt_bias,
           onorm_g, w_out, w_rg, b_rg, w_re, b_re, w1, w3, w2):
    B, S, D = x.shape
    depth = w_ada.shape[0]
    HA, hda = rel_bias.shape[1], qn_g.shape[-1]
    HB, hdb = A_log.shape[-1], onorm_g.shape[-1]
    DA, DB = HA * hda, HB * hdb
    NG, NE = w_rg.shape[-1], w_re.shape[-1]
    T = B * S
    assert hdb == LANES and 2 * hda == LANES and DA + DB == D and 16 * HB <= LANES
    assert NE + NG <= LANES and conv_w.shape[1] == 5 and S % ROW_TILE == 0
    n_main = 3 * DA + 4 * DB
    p_rows = TOP_K * T + NE * MOE_RB

    bias_prof = _attn_bias_profiles(rel_bias, S)
    w_in_b = w_in.astype(BF16)
    x2 = x.reshape(T, D)
    for l in range(depth):
        mod = _ada(c, w_ada[l], b_ada[l]).reshape(B, 6, 1, D)
        sh1, sc1, gt1, sh2, sc2, gt2 = (mod[:, i] for i in range(6))

        w_gate = jnp.tile(w_in[l][:, n_main:], (1, 4))
        w_gate = _hilo_weights(jnp.pad(w_gate, ((0, 0), (0, LANES - w_gate.shape[1]))))
        proj, gat = _inproj(x2, norm1_g[l].reshape(1, D), sc1, sh1, w_in_b[l], n_main, w_gate, S)

        def gate_row(p):
            grp = jnp.concatenate([p.reshape(-1), jnp.zeros((2 * HB,), F32)])
            return jnp.pad(jnp.tile(grp, 4), (0, LANES - 16 * HB)).reshape(1, LANES)

        pack = _gating(gat, gate_row(A_log[l]), gate_row(dt_bias[l]), HB)

        oa = _attention(proj, jnp.tile(qn_g[l], 2).reshape(1, LANES), jnp.tile(kn_g[l], 2).reshape(1, LANES),
                        bias_prof, B, S, HA, hda)
        cw4 = jnp.transpose(conv_w[l].reshape(conv_w.shape[1], 3, HB, hdb), (2, 1, 0, 3))
        ob = _gdn(proj, cw4, pack.reshape(B, S, LANES), onorm_g[l].reshape(1, LANES), B, S, HB, 3 * DA // LANES)

        wr = _hilo_weights(jnp.pad(jnp.concatenate([w_re[l], w_rg[l]], axis=1), ((0, 0), (0, LANES - NE - NG))))
        br = jnp.pad(jnp.concatenate([b_re[l], b_rg[l]]), (0, LANES - NE - NG)).reshape(1, LANES)
        x1, hpk, logits = _outproj(oa, ob, x2, gt1, w_out[l].astype(BF16), norm2_g[l].reshape(1, D),
                                   sc2, sh2, wr, br, S)
        route, info = _route(logits, NE, NG)
        dest = route[:, 4:4 + TOP_K].astype(I32).reshape(TOP_K * T)
        we, ws, wn, wt = _moe_schedule(info, NE, p_rows)
        xs = _dispatch(dest, hpk, p_rows)
        ypk = _moe(we, ws, wn, wt, xs, w1[l], w3[l], w2[l])
        x2 = _combine(dest, x1, route, gt2, ypk, S)
    return x2.reshape(B, S, D)
```

```python
import functools

import numpy as np
import jax
import jax.numpy as jnp
from jax import lax
from jax.experimental import pallas as pl
from jax.experimental.pallas import tpu as pltpu

F32 = jnp.float32
BF16 = jnp.bfloat16
I32 = jnp.int32
U32 = jnp.uint32
HIGHEST = lax.Precision.HIGHEST

EPS = 1e-6
NEG = -1e30
DILATED_BRANCHES = ((128, 1), (512, 4), (2048, 16))
REL_MAX_DIST = 1024
CHUNK = 64
TOP_K = 2

LANES = 128
SUBLANES = 8
MIB = 1 << 20

ATT_QB = 128
ATT_KW = 256
ATT_UNROLL = 8
ATT_PROF_W = 512
GDN_HP = 2
GDN_PREP_UNROLL = 4
GDN_SEG = 512
MOE_RB = 128
MOE_TALL = 512
MOE_RMAX = 1024
MOE_CW = 512
ROW_TILE = 256
DMA_UNROLL = 8


def _cparams(sem, vmem_mib):
    return pltpu.CompilerParams(dimension_semantics=sem, vmem_limit_bytes=vmem_mib * MIB)


def _dot(a, b, **kw):
    return jnp.dot(a, b, preferred_element_type=F32, **kw)


def _dot_nt(a, b):
    return lax.dot_general(a, b, (((1,), (1,)), ((), ())), preferred_element_type=F32)


def _pick_tile(n, prefs):
    for t in prefs:
        if n % t == 0:
            return t
    return n


def _pack_halves(x):
    half = x.shape[1] // 2
    lo = lax.bitcast_convert_type(x[:, :half].astype(BF16).astype(F32), U32)
    hi = lax.bitcast_convert_type(x[:, half:].astype(BF16).astype(F32), U32)
    return lax.shift_right_logical(lo, jnp.uint32(16)) | (hi & jnp.uint32(0xFFFF0000))


def _hilo_weights(w):
    hi = w.astype(BF16)
    lo = (w - hi.astype(F32)).astype(BF16)
    return jnp.concatenate([hi, lo], axis=1)


def _dot_hilo(x, w2):
    m, n = x.shape[0], w2.shape[1] // 2
    hi = x.astype(BF16)
    lo = (x - hi.astype(F32)).astype(BF16)
    r = _dot(jnp.concatenate([hi, lo], axis=0), w2)
    return r[:m, :n] + (r[:m, n:] + r[m:, :n])


def _unpack_halves(p):
    lo = lax.bitcast_convert_type(lax.shift_left(p, jnp.uint32(16)), F32)
    hi = lax.bitcast_convert_type(p & jnp.uint32(0xFFFF0000), F32)
    return lo, hi


def _ada_kernel(c_ref, w_ref, b_ref, o_ref):
    c = c_ref[...]
    s = (c * jax.nn.sigmoid(c)).astype(BF16)
    o_ref[...] = _dot(s, w_ref[...].astype(BF16)) + b_ref[...]


def _ada(c, w_ada, b_ada):
    B, D = c.shape
    N = w_ada.shape[1]
    tn = _pick_tile(N, (1024, 512, 256, 128))
    return pl.pallas_call(
        _ada_kernel,
        grid=(N // tn,),
        in_specs=[pl.BlockSpec((B, D), lambda j: (0, 0)),
                  pl.BlockSpec((D, tn), lambda j: (0, j)),
                  pl.BlockSpec((1, tn), lambda j: (0, j))],
        out_specs=pl.BlockSpec((B, tn), lambda j: (0, j)),
        out_shape=jax.ShapeDtypeStruct((B, N), F32),
        compiler_params=_cparams(("arbitrary",), 40),
    )(c, w_ada, b_ada.reshape(1, N))


def _inproj_kernel(x_ref, g_ref, sc_ref, sh_ref, w_ref, wg_ref, o_ref, og_ref, h_scr):
    @pl.when(pl.program_id(1) == 0)
    def _():
        x = x_ref[...]
        y = x * lax.rsqrt(jnp.mean(x * x, axis=-1, keepdims=True) + EPS) * g_ref[...]
        h = y * (1.0 + sc_ref[...]) + sh_ref[...]
        h_scr[...] = h.astype(BF16)
        og_ref[...] = _dot_hilo(h, wg_ref[...])

    o_ref[...] = _dot(h_scr[...], w_ref[...]).astype(o_ref.dtype)


def _inproj(x2, g, sc, sh, w_all, n_main, w_gate, S):
    T, D = x2.shape
    NM = n_main
    tm = _pick_tile(S, (1024, 512, 256, 128))
    tn = _pick_tile(NM, (1024, 512, 256, 128))
    per_b = S // tm
    return pl.pallas_call(
        _inproj_kernel,
        grid=(T // tm, NM // tn),
        in_specs=[pl.BlockSpec((tm, D), lambda i, j: (i, 0)),
                  pl.BlockSpec((1, D), lambda i, j: (0, 0)),
                  pl.BlockSpec((None, 1, D), lambda i, j: (i // per_b, 0, 0)),
                  pl.BlockSpec((None, 1, D), lambda i, j: (i // per_b, 0, 0)),
                  pl.BlockSpec((D, tn), lambda i, j: (0, j)),
                  pl.BlockSpec((D, 2 * LANES), lambda i, j: (0, 0))],
        out_specs=[pl.BlockSpec((tm, tn), lambda i, j: (i, j)),
                   pl.BlockSpec((tm, LANES), lambda i, j: (i, 0))],
        out_shape=[jax.ShapeDtypeStruct((T, NM), BF16), jax.ShapeDtypeStruct((T, LANES), F32)],
        scratch_shapes=[pltpu.VMEM((tm, D), BF16)],
        compiler_params=_cparams(("arbitrary", "arbitrary"), 56),
    )(x2, g, sc, sh, w_all, w_gate)


def _gating_kernel(gat_ref, a_ref, dt_ref, o_ref, *, hb):
    gat = gat_ref[...]
    tm = gat.shape[0]
    gw = 4 * hb
    g = -jnp.exp(a_ref[...]) * jax.nn.softplus(gat + dt_ref[...])
    beta = jax.nn.sigmoid(gat)
    r = lax.broadcasted_iota(I32, (tm, tm), 0)
    c = lax.broadcasted_iota(I32, (tm, tm), 1)
    same = (r // CHUNK) == (c // CHUNK)
    pre = _dot(jnp.where(same & (c <= r), 1.0, 0.0), g, precision=HIGHEST)
    suf = _dot(jnp.where(same & (c >= r), 1.0, 0.0), g, precision=HIGHEST)
    lane = lax.broadcasted_iota(I32, gat.shape, 1)
    o_ref[...] = jnp.where(lane // gw == 0, jnp.where(lane % gw < hb, pre, suf), beta)


def _gating(gat, a_row, dt_row, hb):
    T = gat.shape[0]
    tm = ROW_TILE
    return pl.pallas_call(
        functools.partial(_gating_kernel, hb=hb),
        grid=(T // tm,),
        in_specs=[pl.BlockSpec((tm, LANES), lambda i: (i, 0)),
                  pl.BlockSpec((1, LANES), lambda i: (0, 0)),
                  pl.BlockSpec((1, LANES), lambda i: (0, 0))],
        out_specs=pl.BlockSpec((tm, LANES), lambda i: (i, 0)),
        out_shape=jax.ShapeDtypeStruct((T, LANES), F32),
        compiler_params=_cparams(("arbitrary",), 16),
    )(gat, a_row, dt_row)


def _t5_bucket(rel, n_buckets):
    half = n_buckets // 2
    max_exact = half // 2
    n = np.abs(rel)
    large = max_exact + (np.log(np.maximum(n, 1) / max_exact) / np.log(REL_MAX_DIST / max_exact)
                         * (half - max_exact)).astype(np.int32)
    large = np.minimum(large, half - 1)
    return (np.where(rel > 0, half, 0) + np.where(n < max_exact, n, large)).astype(np.int32)


def _attn_plan(S):
    plan, base = [], 0
    for window, dil in DILATED_BRANCHES:
        n = window // (2 * dil)
        L = S // dil
        assert L % ATT_QB == 0 and n * 2 == ATT_QB
        kw = min(ATT_KW, L)
        nbq = L // ATT_QB
        nvar = 1 if nbq == 1 else 3
        plan.append((dil, L, nbq, kw, base, nvar, n))
        base += nvar
    return tuple(plan), base


def _attn_bias_profiles(rel_bias, S):
    plan, nvar_total = _attn_plan(S)
    nbuckets, H = rel_bias.shape
    u = np.arange(ATT_PROF_W) - ATT_QB
    onehots, bands = [], []
    for dil, L, nbq, kw, base, nvar, n in plan:
        offs = [0] if nvar == 1 else [0, -n, -(kw - ATT_QB)]
        for off in offs:
            rel = off + u
            onehots.append(np.eye(nbuckets, dtype=np.float32)[_t5_bucket(rel * dil, nbuckets)])
            bands.append(np.abs(rel) <= n)
    onehot = jnp.asarray(np.stack(onehots))
    band = jnp.asarray(np.stack(bands))
    prof = jnp.einsum('vwn,nh->hvw', onehot, rel_bias.astype(F32), precision=HIGHEST)
    prof = jnp.where(band[None], prof, NEG)
    prof = prof.reshape(H // 2, 2, nvar_total, ATT_PROF_W)
    return jnp.transpose(prof, (0, 2, 1, 3)).reshape(H // 2, 2 * nvar_total, ATT_PROF_W)


def _attn_kernel(q_ref, k_ref, v_ref, qg_ref, kg_ref, prof_ref, o_ref,
                 qn_scr, kn_scr, v_scr, ob_scr, mb_scr, db_scr, bias_ref, *, plan, hd):
    S = q_ref.shape[0]
    lane = lax.broadcasted_iota(I32, (1, LANES), 1)
    left = lane < hd

    @pl.when(pl.program_id(1) == 0)
    def _():
        for row in range(prof_ref.shape[0]):
            rep = jnp.broadcast_to(prof_ref[row:row + 1, :], (ATT_QB, ATT_PROF_W))
            skew = pltpu.roll(rep, 0, 1, stride=1, stride_axis=0)
            bias_ref[row // 2, row % 2] = skew[:, ATT_QB:ATT_QB + ATT_KW]

    def headnorm(x, g):
        x2 = x * x
        s_all = jnp.sum(x2, axis=-1, keepdims=True)
        s_left = jnp.sum(jnp.where(left, x2, 0.0), axis=-1, keepdims=True)
        ms = jnp.where(left, s_left, s_all - s_left) * (1.0 / hd)
        return x * lax.rsqrt(ms + EPS) * g

    qn_scr[...] = headnorm(q_ref[...].astype(F32), qg_ref[...]) * (hd ** -0.5)
    kn_scr[...] = headnorm(k_ref[...].astype(F32), kg_ref[...])
    v_scr[...] = v_ref[...].astype(F32)

    for bi, (dil, L, nbq, kw, base, nvar, n) in enumerate(plan):
        ones = jnp.ones((kw, LANES), BF16)

        def body(t, carry, dil=dil, L=L, nbq=nbq, kw=kw, base=base, nvar=nvar, n=n, bi=bi, ones=ones):
            blocks = []
            for uu in range(ATT_UNROLL):
                idx = t * ATT_UNROLL + uu
                r = idx // nbq
                i = idx % nbq
                q0 = i * ATT_QB
                k0 = jnp.clip(q0 - n, 0, L - kw)
                var = base if nvar == 1 else base + jnp.where(i > 0, 1, 0) + jnp.where(i == nbq - 1, 1, 0)
                if dil == 1:
                    qrows = pl.ds(pl.multiple_of(q0, ATT_QB), ATT_QB)
                    krows = pl.ds(pl.multiple_of(k0, CHUNK), kw)
                else:
                    qrows = pl.ds(r + q0 * dil, ATT_QB, stride=dil)
                    krows = pl.ds(r + k0 * dil, kw, stride=dil)
                qb = qn_scr[qrows, :]
                q2 = jnp.concatenate([jnp.where(left, qb, 0.0), jnp.where(left, 0.0, qb)], axis=0).astype(BF16)
                blocks.append((qrows, krows, var, q2))
            scores = [_dot_nt(q2, kn_scr[krows, :].astype(BF16)) for qrows, krows, var, q2 in blocks]
            probs, maxes = [], []
            for (qrows, krows, var, q2), s in zip(blocks, scores):
                s = s + jnp.concatenate([bias_ref[var, 0][:, :kw], bias_ref[var, 1][:, :kw]], axis=0)
                m = jnp.max(s, axis=-1, keepdims=True)
                probs.append(jnp.exp(s - m).astype(BF16))
                maxes.append(m)
            outs = [_dot(p, jnp.concatenate([v_scr[krows, :].astype(BF16), ones], axis=1))
                    for (qrows, krows, var, q2), p in zip(blocks, probs)]
            for (qrows, krows, var, q2), m, od in zip(blocks, maxes, outs):
                mb = jnp.broadcast_to(m, (2 * ATT_QB, LANES))
                ob_scr[bi, qrows, :] = jnp.where(left, od[:ATT_QB, :LANES], od[ATT_QB:, :LANES])
                mb_scr[bi, qrows, :] = jnp.where(left, mb[:ATT_QB], mb[ATT_QB:])
                db_scr[bi, qrows, :] = jnp.where(left, od[:ATT_QB, LANES:], od[ATT_QB:, LANES:])
            return carry

        assert (dil * nbq) % ATT_UNROLL == 0
        lax.fori_loop(0, dil * nbq // ATT_UNROLL, body, 0)

    nb = len(plan)
    mx = mb_scr[0]
    for bi in range(1, nb):
        mx = jnp.maximum(mx, mb_scr[bi])
    num = jnp.zeros((S, LANES), F32)
    den = jnp.zeros((S, LANES), F32)
    for bi in range(nb):
        w = jnp.exp(mb_scr[bi] - mx)
        num = num + w * ob_scr[bi]
        den = den + w * db_scr[bi]
    o_ref[...] = (num / den).astype(o_ref.dtype)


def _attention(proj, qg2, kg2, profiles, B, S, HA, hd):
    T = proj.shape[0]
    pairs = HA // 2
    da_blocks = HA * hd // LANES
    plan, nvar = _attn_plan(S)
    return pl.pallas_call(
        functools.partial(_attn_kernel, plan=plan, hd=hd),
        grid=(pairs, B),
        in_specs=[pl.BlockSpec((S, LANES), lambda p, b: (b, p)),
                  pl.BlockSpec((S, LANES), lambda p, b: (b, da_blocks + p)),
                  pl.BlockSpec((S, LANES), lambda p, b: (b, 2 * da_blocks + p)),
                  pl.BlockSpec((1, LANES), lambda p, b: (0, 0)),
                  pl.BlockSpec((1, LANES), lambda p, b: (0, 0)),
                  pl.BlockSpec((None, 2 * nvar, ATT_PROF_W), lambda p, b: (p, 0, 0))],
        out_specs=pl.BlockSpec((S, LANES), lambda p, b: (b, p)),
        out_shape=jax.ShapeDtypeStruct((T, HA * hd), BF16),
        scratch_shapes=[pltpu.VMEM((S, LANES), F32)] * 3 + [pltpu.VMEM((len(plan), S, LANES), F32)] * 3
        + [pltpu.VMEM((nvar, 2, ATT_QB, ATT_KW), F32)],
        compiler_params=_cparams(("arbitrary", "arbitrary"), 40),
    )(proj, proj, proj, qg2, kg2, profiles)


def _gdn_kernel(q_ref, k_ref, v_ref, z_ref, cw_ref, pack_ref, og_ref, o_ref,
                q_scr, k_scr, v_scr, xpad_scr, pk2_scr, sel_scr, u_scr, wq_scr, at_scr, kdt_scr, et_scr, oacc_scr,
                *, hb, hp):
    S = q_ref.shape[0]
    P2 = 2 * CHUNK
    W2 = 2 * LANES
    npair = S // P2
    hg = pl.program_id(1)
    gw = 4 * hb
    dk = LANES

    pad = SUBLANES
    for slot in range(2):
        xpad_scr[slot, 0:pad, :] = jnp.zeros((pad, LANES), F32)
        xpad_scr[slot, pad + S:, :] = jnp.zeros((pad, LANES), F32)

    seg = GDN_SEG if S % GDN_SEG == 0 else S

    def conv_silu_to(src_ref, lanes_j, j, which, dst_scr, l2norm, scale):
        xp = xpad_scr.at[(3 * j + which) % 2]
        for s0 in range(0, S, seg):
            xp[pad + s0:pad + s0 + seg, :] = src_ref[s0:s0 + seg, lanes_j].astype(F32)
        for s0 in range(0, S, seg):
            acc = xp[pad + s0:pad + s0 + seg, :] * cw_ref[j, which, 2:3, :]
            for d in (-2, -1, 1, 2):
                acc = acc + xp[pad + s0 + d:pad + s0 + d + seg, :] * cw_ref[j, which, 2 + d:3 + d, :]
            y = acc * jax.nn.sigmoid(acc)
            if l2norm:
                y = y * (lax.rsqrt(jnp.sum(y * y, axis=-1, keepdims=True) + EPS) * scale)
            dst_scr[j, s0:s0 + seg, :] = y

    for s0 in range(0, S, seg):
        pk = pack_ref[s0:s0 + seg, :]
        p_hi = pk.astype(BF16)
        p_lo = (pk - p_hi.astype(F32)).astype(BF16)
        pk2_scr[s0:s0 + seg, :] = jnp.concatenate([p_hi, p_lo], axis=1)
    srow = lax.broadcasted_iota(I32, (W2, 4 * LANES), 0) % LANES
    scol = lax.broadcasted_iota(I32, (W2, 4 * LANES), 1) // LANES
    for j in range(hp):
        lanes_j = slice(j * LANES, (j + 1) * LANES)
        conv_silu_to(q_ref, lanes_j, j, 0, q_scr, True, dk ** -0.5)
        conv_silu_to(k_ref, lanes_j, j, 1, k_scr, True, 1.0)
        conv_silu_to(v_ref, lanes_j, j, 2, v_scr, False, 1.0)
        src = hg * hp + j + jnp.where(scol < 2, scol * hb, gw + 2 * hb + (scol - 2) * hb)
        sel_scr[j] = jnp.where(srow == src, 1.0, 0.0).astype(BF16)
        oacc_scr[j] = jnp.zeros((S, LANES), F32)

    r4 = lax.broadcasted_iota(I32, (CHUNK, W2), 0)
    l4 = lax.broadcasted_iota(I32, (CHUNK, W2), 1)
    c4 = l4 % CHUNK
    blk4 = l4 // CHUNK
    lo_half = (l4 % LANES) < CHUNK
    ahead = jnp.where(l4 >= LANES, r4 - c4, c4 - r4)
    incl = ahead <= 0
    strict = ahead < 0
    bd16 = (r4 // 16) == (c4 // 16)

    def squeeze(x):
        return jnp.where(lo_half, x[:CHUNK], x[CHUNK:])

    def unsqueeze(x):
        return jnp.concatenate([jnp.where(lo_half, x, jnp.zeros_like(x)),
                                jnp.where(lo_half, jnp.zeros_like(x), x)], axis=0)

    def mm4(a, b):
        rhs = jnp.concatenate([jnp.where(blk4 == g, b, 0.0) for g in range(4)], axis=0)
        return _dot(a.astype(BF16), rhs.astype(BF16))

    first = lax.broadcasted_iota(I32, (P2, 1), 0) < CHUNK
    zpair = jnp.zeros((P2, LANES), BF16)

    def bdiag(x):
        return jnp.concatenate([jnp.concatenate([x[:, :LANES], zpair], axis=1),
                                jnp.concatenate([zpair, x[:, LANES:]], axis=1)], axis=0)

    U = GDN_PREP_UNROLL if npair % GDN_PREP_UNROLL == 0 else 1

    def prep(t, carry):
        cx = []
        chains = [(t * U + u, j) for u in range(U) for j in range(hp)]
        bcs = [_dot(pk2_scr[pl.ds(pl.multiple_of(m * P2, P2), P2), :], sel_scr[j]) for m, j in chains]
        for (m, j), bc in zip(chains, bcs):
            rows = pl.ds(pl.multiple_of(m * P2, P2), P2)
            kp = k_scr[j, rows, :]
            qp = q_scr[j, rows, :]
            vp = v_scr[j, rows, :]
            gc2 = bc[:, 0:W2]
            beta2 = bc[:, W2:2 * W2]
            gcf, gcb = gc2[:, :LANES], gc2[:, LANES:]
            tot2 = jnp.concatenate([jnp.where(first, gcf[CHUNK - 1:CHUNK], gcf[P2 - 1:P2]),
                                    jnp.where(first, gcb[0:1], gcb[CHUNK:CHUNK + 1])], axis=1)
            egc2 = jnp.exp(gc2)
            k2 = jnp.concatenate([kp, kp], axis=1)
            kb2 = k2 * beta2
            vb2 = jnp.concatenate([vp, vp], axis=1) * beta2
            kbe2 = kb2 * egc2
            cx.append(dict(
                m=m, j=j, rows=rows, vb2=vb2, kbe2=kbe2,
                qeb=(jnp.concatenate([qp, qp], axis=1) * egc2).astype(BF16),
                kd2=k2 * jnp.exp(tot2 - gc2),
                et=jnp.exp(tot2),
                dec=jnp.exp(jnp.where(
                    incl, squeeze(gc2) - squeeze(jnp.concatenate([gcf.T, gcb.T], axis=1)), -jnp.inf)),
                stk=jnp.concatenate([kb2[:, :LANES], kb2[:, LANES:], qp], axis=0).astype(BF16),
                kpb=kp.astype(BF16)))

        g3s = [_dot_nt(c['stk'], c['kpb']) for c in cx]
        for c, g3 in zip(cx, g3s):
            lm = jnp.where(strict, squeeze(jnp.concatenate([g3[:P2], g3[P2:2 * P2]], axis=1)) * c['dec'], 0.0)
            attn = squeeze(jnp.concatenate([g3[2 * P2:], g3[2 * P2:]], axis=1)) * c['dec']
            c['attn2'] = unsqueeze(attn).astype(BF16)
            c['lbd'] = jnp.where(bd16, lm, 0.0)
            c['loff'] = lm - c['lbd']
        nn = [-c['lbd'] for c in cx]
        pw = [mm4(c['lbd'], c['lbd']) for c in cx]
        for rnd in range(3):
            prod = [mm4(a, p) for a, p in zip(nn, pw)]
            nxt = [mm4(p, p) for p in pw] if rnd < 2 else pw
            nn = [a + p + q for a, p, q in zip(nn, pw, prod)]
            pw = nxt
        mo = [c['loff'] + x for c, x in zip(cx, [mm4(a, c['loff']) for a, c in zip(nn, cx)])]
        m2 = [mm4(x, x) for x in mo]
        mn = [mm4(x, a) for x, a in zip(mo, nn)]
        xo = [a - x - y for a, x, y in zip(nn, mo, mn)]
        mx = [mm4(a, b) for a, b in zip(m2, xo)]
        toff = [a + b + q for a, b, q in zip(xo, m2, mx)]
        z4 = jnp.zeros((CHUNK, W2), BF16)
        tws = []
        for c, tf in zip(cx, toff):
            vbb, kbb = c['vb2'].astype(BF16), c['kbe2'].astype(BF16)
            blocks = []
            for g in range(4):
                rr = slice((g % 2) * CHUNK, (g % 2 + 1) * CHUNK)
                ll = slice((g // 2) * LANES, (g // 2 + 1) * LANES)
                blocks.append(jnp.concatenate([z4] * g + [vbb[rr, ll], kbb[rr, ll]] + [z4] * (3 - g), axis=1))
            tws.append(_dot(tf.astype(BF16), jnp.concatenate(blocks, axis=0)))
        for c, tw in zip(cx, tws):
            m, j, rows = c['m'], c['j'], c['rows']
            def pair_layout(off):
                piece = lambda g: tw[:, 2 * g * LANES + off:2 * g * LANES + off + LANES]
                return jnp.concatenate([jnp.concatenate([piece(0), piece(2)], axis=1),
                                        jnp.concatenate([piece(1), piece(3)], axis=1)], axis=0)

            u2 = c['vb2'] + pair_layout(0)
            w2 = (c['kbe2'] + pair_layout(LANES)).astype(BF16)
            qeb = c['qeb']
            u_scr[j, rows, :] = u2
            wq_scr[j, pl.ds(pl.multiple_of(m * 2 * P2, 2 * P2), 2 * P2), :] = jnp.concatenate(
                [w2[:CHUNK], qeb[:CHUNK], w2[CHUNK:], qeb[CHUNK:]], axis=0)
            at_scr[j, rows, :] = c['attn2']
            kdt_scr[j, 0, :, rows] = c['kd2'][:, :LANES].T.astype(BF16)
            kdt_scr[j, 1, :, rows] = c['kd2'][:, LANES:].T.astype(BF16)
            et_scr[j, pl.ds(pl.multiple_of(m * 2 * SUBLANES, 2 * SUBLANES), 2 * SUBLANES), :] = jnp.concatenate(
                [c['et'][:SUBLANES], c['et'][CHUNK:CHUNK + SUBLANES]], axis=0)
        return carry

    lax.fori_loop(0, npair // U, prep, 0)

    zc = jnp.zeros((CHUNK, LANES), F32)
    zp = jnp.zeros((P2, LANES), F32)

    def place(v, cpos):
        return jnp.concatenate([v, zc] if cpos == 0 else [zc, v], axis=0)

    def scan(m, states):
        pf = m
        pb = npair - 1 - m
        rows_f = pl.ds(pl.multiple_of(pf * P2, P2), P2)
        rows_b = pl.ds(pl.multiple_of(pb * P2, P2), P2)
        hx = []
        for j in range(hp):
            hx.append(dict(
                u_f=u_scr[j, rows_f, :LANES], u_b=u_scr[j, rows_b, LANES:],
                at_f=at_scr[j, rows_f, :LANES], at_b=at_scr[j, rows_b, LANES:],
                kdt=jnp.concatenate([kdt_scr[j, 0, :, rows_f], kdt_scr[j, 1, :, rows_b]], axis=1),
                wq_f=wq_scr[j, pl.ds(pl.multiple_of(pf * 2 * P2, 2 * P2), 2 * P2), :LANES],
                wq_b=wq_scr[j, pl.ds(pl.multiple_of(pb * 2 * P2, 2 * P2), 2 * P2), LANES:],
                et_f=et_scr[j, pl.ds(pl.multiple_of(pf * 2 * SUBLANES, 2 * SUBLANES), 2 * SUBLANES), :LANES],
                et_b=et_scr[j, pl.ds(pl.multiple_of(pb * 2 * SUBLANES, 2 * SUBLANES), 2 * SUBLANES), LANES:]))
        sts = list(states)
        for step in range(2):
            cf, cb = step, 1 - step
            rrs = [_dot(jnp.concatenate([c['wq_f'][cf * P2:(cf + 1) * P2], c['wq_b'][cb * P2:(cb + 1) * P2]], axis=1),
                        bdiag(st.astype(BF16))) for c, st in zip(hx, sts)]
            ress = []
            for c, rr in zip(hx, rrs):
                u2 = jnp.concatenate([c['u_f'][cf * CHUNK:(cf + 1) * CHUNK],
                                      c['u_b'][cb * CHUNK:(cb + 1) * CHUNK]], axis=1)
                v_new = u2 - rr[:CHUNK]
                rhs = jnp.concatenate(
                    [jnp.concatenate([place(v_new[:, :LANES], cf), zp], axis=1),
                     jnp.concatenate([zp, place(v_new[:, LANES:], cb)], axis=1)], axis=0).astype(BF16)
                lhs = jnp.concatenate(
                    [jnp.concatenate([c['at_f'][cf * CHUNK:(cf + 1) * CHUNK],
                                      c['at_b'][cb * CHUNK:(cb + 1) * CHUNK]], axis=1),
                     c['kdt']], axis=0)
                ress.append(_dot(lhs, rhs))
            for j, (c, rr, res) in enumerate(zip(hx, rrs, ress)):
                o2 = rr[CHUNK:] + res[:CHUNK]
                et2 = jnp.concatenate([c['et_f'][cf * SUBLANES:cf * SUBLANES + 1],
                                       c['et_b'][cb * SUBLANES:cb * SUBLANES + 1]], axis=1)
                sts[j] = sts[j] * et2 + res[CHUNK:]
                of_rows = pl.ds(pl.multiple_of(pf * P2 + cf * CHUNK, CHUNK), CHUNK)
                ob_rows = pl.ds(pl.multiple_of(pb * P2 + cb * CHUNK, CHUNK), CHUNK)
                oacc_scr[j, of_rows, :] = oacc_scr[j, of_rows, :] + o2[:, :LANES]
                oacc_scr[j, ob_rows, :] = oacc_scr[j, ob_rows, :] + o2[:, LANES:]
        return tuple(sts)

    s0 = jnp.zeros((dk, W2), F32)
    lax.fori_loop(0, npair, scan, (s0,) * hp)

    for j in range(hp):
        lanes_j = slice(j * LANES, (j + 1) * LANES)
        for s0 in range(0, S, seg):
            o = oacc_scr[j, s0:s0 + seg, :]
            y = o * lax.rsqrt(jnp.mean(o * o, axis=-1, keepdims=True) + EPS) * og_ref[...]
            z = z_ref[s0:s0 + seg, lanes_j].astype(F32)
            o_ref[s0:s0 + seg, lanes_j] = (y * (z * jax.nn.sigmoid(z))).astype(o_ref.dtype)


def _gdn(proj, cw4, pack3, onorm_g, B, S, HB, base_blk):
    T = proj.shape[0]
    hp = GDN_HP if (HB % GDN_HP == 0 and base_blk % GDN_HP == 0) else 1
    wblk = hp * LANES
    npair = S // (2 * CHUNK)

    def col(k):
        off = (base_blk + k * HB) // hp
        return lambda b, h: (b, off + h)

    return pl.pallas_call(
        functools.partial(_gdn_kernel, hb=HB, hp=hp),
        grid=(B, HB // hp),
        in_specs=[pl.BlockSpec((S, wblk), col(0)),
                  pl.BlockSpec((S, wblk), col(1)),
                  pl.BlockSpec((S, wblk), col(2)),
                  pl.BlockSpec((S, wblk), col(3)),
                  pl.BlockSpec((hp, 3, cw4.shape[2], LANES), lambda b, h: (h, 0, 0, 0)),
                  pl.BlockSpec((None, S, LANES), lambda b, h: (b, 0, 0)),
                  pl.BlockSpec((1, LANES), lambda b, h: (0, 0))],
        out_specs=pl.BlockSpec((S, wblk), lambda b, h: (b, h)),
        out_shape=jax.ShapeDtypeStruct((T, HB * LANES), BF16),
        scratch_shapes=[pltpu.VMEM((hp, S, LANES), F32)] * 3
        + [pltpu.VMEM((2, S + 2 * SUBLANES, LANES), F32),
           pltpu.VMEM((S, 2 * LANES), BF16),
           pltpu.VMEM((hp, 2 * LANES, 4 * LANES), BF16),
           pltpu.VMEM((hp, S, 2 * LANES), F32),
           pltpu.VMEM((hp, 2 * S, 2 * LANES), BF16),
           pltpu.VMEM((hp, S, 2 * LANES), BF16),
           pltpu.VMEM((hp, 2, LANES, S), BF16),
           pltpu.VMEM((hp, npair * 2 * SUBLANES, 2 * LANES), F32),
           pltpu.VMEM((hp, S, LANES), F32)],
        compiler_params=_cparams(("arbitrary", "arbitrary"), 56),
    )(proj, proj, proj, proj, cw4, pack3, onorm_g)


def _outproj_kernel(oa_ref, ob_ref, x_ref, gt_ref, w_ref, g_ref, sc_ref, sh_ref, wr_ref, br_ref,
                    x1_ref, hp_ref, lg_ref, h_scr):
    da = oa_ref.shape[1]
    y = _dot(oa_ref[...], w_ref[:da, :]) + _dot(ob_ref[...], w_ref[da:, :])
    x1 = x_ref[...] + gt_ref[...] * y
    x1_ref[...] = x1
    hn = x1 * lax.rsqrt(jnp.mean(x1 * x1, axis=-1, keepdims=True) + EPS) * g_ref[...]
    h = hn * (1.0 + sc_ref[...]) + sh_ref[...]
    hp_ref[...] = _pack_halves(h)
    h_scr[...] = h

    @pl.when(pl.program_id(0) < pl.num_programs(0))
    def _():
        lg_ref[...] = _dot_hilo(h_scr[...], wr_ref[...]) + br_ref[...]


def _outproj(oa, ob, x2, gt1, w_out_b, g2, sc2, sh2, wr, br, S):
    T, D = x2.shape
    tm = ROW_TILE
    per_b = S // tm
    bmap = lambda i: (i // per_b, 0, 0)
    return pl.pallas_call(
        _outproj_kernel,
        grid=(T // tm,),
        in_specs=[pl.BlockSpec((tm, oa.shape[1]), lambda i: (i, 0)),
                  pl.BlockSpec((tm, ob.shape[1]), lambda i: (i, 0)),
                  pl.BlockSpec((tm, D), lambda i: (i, 0)),
                  pl.BlockSpec((None, 1, D), bmap),
                  pl.BlockSpec((D, D), lambda i: (0, 0)),
                  pl.BlockSpec((1, D), lambda i: (0, 0)),
                  pl.BlockSpec((None, 1, D), bmap),
                  pl.BlockSpec((None, 1, D), bmap),
                  pl.BlockSpec((D, 2 * LANES), lambda i: (0, 0)),
                  pl.BlockSpec((1, LANES), lambda i: (0, 0))],
        out_specs=[pl.BlockSpec((tm, D), lambda i: (i, 0)),
                   pl.BlockSpec((tm, D // 2), lambda i: (i, 0)),
                   pl.BlockSpec((tm, LANES), lambda i: (i, 0))],
        out_shape=[jax.ShapeDtypeStruct((T, D), F32), jax.ShapeDtypeStruct((T, D // 2), U32),
                   jax.ShapeDtypeStruct((T, LANES), F32)],
        scratch_shapes=[pltpu.VMEM((tm, D), F32)],
        compiler_params=_cparams(("arbitrary",), 48),
    )(oa, ob, x2, gt1, w_out_b, g2, sc2, sh2, wr, br)


def _route_kernel(lg_ref, o_ref, info_ref, run_scr, *, ne, ng):
    ph = pl.program_id(0)
    i = pl.program_id(1)

    @pl.when((ph == 0) & (i == 0))
    def _():
        run_scr[...] = jnp.zeros_like(run_scr)

    @pl.when((ph == 1) & (i == 0))
    def _():
        cnt = run_scr[...]
        padded = jnp.ceil(cnt * (1.0 / MOE_RB)) * MOE_RB
        k = lax.broadcasted_iota(I32, (LANES, LANES), 0)
        e = lax.broadcasted_iota(I32, (LANES, LANES), 1)
        start = _dot(padded, jnp.where(k < e, 1.0, 0.0), precision=HIGHEST)
        rowi = lax.broadcasted_iota(I32, cnt.shape, 0)
        info_ref[...] = jnp.where(rowi == 0, cnt, jnp.where(rowi == 1, padded, start))
        run_scr[...] = start

    lg = lg_ref[...]
    tm = lg.shape[0]
    epg = ne // ng
    lane_i = lax.broadcasted_iota(I32, lg.shape, 1)
    lane = lane_i.astype(F32)
    big = float(2 * LANES)
    is_g = (lane_i >= ne) & (lane_i < ne + ng)
    gl = jnp.where(is_g, lg, -jnp.inf)
    gmax = jnp.max(gl, axis=-1, keepdims=True)
    gidx = jnp.min(jnp.where(gl == gmax, lane, big), axis=-1, keepdims=True) - ne
    psel = 1.0 / jnp.sum(jnp.where(is_g, jnp.exp(gl - gmax), 0.0), axis=-1, keepdims=True)
    in_grp = (lane_i // epg).astype(F32) == gidx
    el = jnp.where(in_grp & (lane_i < ne), lg, -jnp.inf)
    m1 = jnp.max(el, axis=-1, keepdims=True)
    i1 = jnp.min(jnp.where(el == m1, lane, big), axis=-1, keepdims=True)
    el2 = jnp.where(lane == i1, -jnp.inf, el)
    m2 = jnp.max(el2, axis=-1, keepdims=True)
    i2 = jnp.min(jnp.where(el2 == m2, lane, big), axis=-1, keepdims=True)
    e21 = jnp.exp(m2 - m1)
    g1 = psel / (1.0 + e21)
    g2 = psel * e21 / (1.0 + e21)
    o1 = jnp.where(lane == i1, 1.0, 0.0)
    o2 = jnp.where(lane == i2, 1.0, 0.0)
    cnt = o1 + o2
    r = lax.broadcasted_iota(I32, (tm, tm), 0)
    c = lax.broadcasted_iota(I32, (tm, tm), 1)
    before = _dot(jnp.where(c < r, 1.0, 0.0).astype(BF16), cnt.astype(BF16)) + run_scr[0:1, :]
    d1 = jnp.sum(o1 * before, axis=-1, keepdims=True)
    d2 = jnp.sum(o2 * before, axis=-1, keepdims=True)
    run_scr[...] = run_scr[...] + jnp.sum(cnt, axis=0, keepdims=True)

    @pl.when(ph == 1)
    def _():
        out = jnp.zeros(lg.shape, F32)
        for j, val in enumerate((i1, i2, g1, g2, d1, d2)):
            out = jnp.where(lane_i == j, val, out)
        o_ref[...] = out


def _route(logits, ne, ng):
    T = logits.shape[0]
    tm = _pick_tile(T, (1024, 512, 256))
    return pl.pallas_call(
        functools.partial(_route_kernel, ne=ne, ng=ng),
        grid=(2, T // tm),
        in_specs=[pl.BlockSpec((tm, LANES), lambda p, i: (i, 0))],
        out_specs=[pl.BlockSpec((tm, LANES), lambda p, i: (i * p, 0)),
                   pl.BlockSpec((SUBLANES, LANES), lambda p, i: (0, 0))],
        out_shape=[jax.ShapeDtypeStruct((T, LANES), F32), jax.ShapeDtypeStruct((SUBLANES, LANES), F32)],
        scratch_shapes=[pltpu.VMEM((SUBLANES, LANES), F32)],
        compiler_params=_cparams(("arbitrary", "arbitrary"), 16),
    )(logits)


def _dispatch_kernel(dest_ref, h_hbm, xs_in, xs_hbm, sem):
    del xs_in
    tm = ROW_TILE
    base = pl.program_id(0) * tm

    def copy(tok, a):
        return pltpu.make_async_copy(h_hbm.at[pl.ds(tok, 1)], xs_hbm.at[pl.ds(dest_ref[a], 1)], sem)

    def issue(t, carry):
        for k in range(TOP_K):
            copy(base + t, (base + t) * TOP_K + k).start()
        return carry

    lax.fori_loop(0, tm, issue, 0, unroll=DMA_UNROLL)

    for k in range(TOP_K):
        pltpu.make_async_copy(h_hbm.at[pl.ds(0, tm)], xs_hbm.at[pl.ds(0, tm)], sem).wait()


def _dispatch(dest, hpk, p_rows):
    T, dh = hpk.shape
    return pl.pallas_call(
        _dispatch_kernel,
        grid_spec=pltpu.PrefetchScalarGridSpec(
            num_scalar_prefetch=1,
            grid=(T // ROW_TILE,),
            in_specs=[pl.BlockSpec(memory_space=pl.ANY), pl.BlockSpec(memory_space=pl.ANY)],
            out_specs=pl.BlockSpec(memory_space=pl.ANY),
            scratch_shapes=[pltpu.SemaphoreType.DMA(())]),
        out_shape=jax.ShapeDtypeStruct((p_rows, dh), U32),
        input_output_aliases={2: 0},
        compiler_params=_cparams(("arbitrary",), 16),
    )(dest, hpk, jnp.zeros((p_rows, dh), U32))


def _moe_kernel(we_ref, ws_ref, wn_ref, wt_ref, xs_hbm, w1_ref, w3_ref, w2_ref, y_hbm,
                xbuf, xlo, xhi, yacc, ypk, sem_in, sem_out):
    w = pl.program_id(0)
    c = pl.program_id(1)
    nw = pl.num_programs(0)
    nc = pl.num_programs(1)
    nrows = wn_ref[w]
    start = ws_ref[w]
    nblk = nrows // MOE_RB
    dh = xbuf.shape[1]

    def blk_rows(rb):
        return pl.ds(pl.multiple_of(rb * MOE_RB, MOE_RB), MOE_RB)

    def hbm_rows(item_start, rb):
        return pl.ds(pl.multiple_of(item_start + rb * MOE_RB, MOE_RB), MOE_RB)

    def in_copy(item_start, rb):
        return pltpu.make_async_copy(xs_hbm.at[hbm_rows(item_start, rb)], xbuf.at[blk_rows(rb)], sem_in)

    def out_copy(item_start, rb):
        return pltpu.make_async_copy(ypk.at[blk_rows(rb)], y_hbm.at[hbm_rows(item_start, rb)], sem_out)

    def each_block(n, fn):
        def body(rb, carry):
            fn(rb)
            return carry
        lax.fori_loop(0, n, body, 0)

    @pl.when(c == 0)
    def _load():
        @pl.when(w == 0)
        def _():
            each_block(nblk, lambda rb: in_copy(start, rb).start())

        each_block(nblk, lambda rb: in_copy(start, rb).wait())

        def unpack(rb):
            lo, hi = _unpack_halves(xbuf[blk_rows(rb), :])
            xlo[blk_rows(rb), :] = lo.astype(BF16)
            xhi[blk_rows(rb), :] = hi.astype(BF16)
            yacc[blk_rows(rb), :] = jnp.zeros((MOE_RB, 2 * dh), F32)

        each_block(nblk, unpack)

        @pl.when(w + 1 < nw)
        def _():
            nxt = ws_ref[w + 1]
            each_block(wn_ref[w + 1] // MOE_RB, lambda rb: in_copy(nxt, rb).start())

    @pl.when(nrows > 0)
    def _compute():
        def rows_block(row0, nr):
            rows = pl.ds(row0, nr)
            xl = xlo[rows, :]
            xh = xhi[rows, :]
            h1 = _dot(xl, w1_ref[:dh, :].astype(BF16)) + _dot(xh, w1_ref[dh:, :].astype(BF16))
            h3 = _dot(xl, w3_ref[:dh, :].astype(BF16)) + _dot(xh, w3_ref[dh:, :].astype(BF16))
            hid = (h1 * jax.nn.sigmoid(h1) * h3).astype(BF16)
            yacc[rows, :] = yacc[rows, :] + _dot(hid, w2_ref[...].astype(BF16))

        ntall = nrows // MOE_TALL
        each_block(ntall, lambda i: rows_block(pl.multiple_of(i * MOE_TALL, MOE_TALL), MOE_TALL))

        rem = nrows - ntall * MOE_TALL
        for nr in range(MOE_RB, MOE_TALL, MOE_RB):
            @pl.when(rem == nr)
            def _(nr=nr):
                rows_block(pl.multiple_of(ntall * MOE_TALL, MOE_TALL), nr)

    @pl.when(c == nc - 1)
    def _store():
        @pl.when(w > 0)
        def _():
            prev = ws_ref[w - 1]
            each_block(wn_ref[w - 1] // MOE_RB, lambda rb: out_copy(prev, rb).wait())

        def pack(rb):
            ypk[blk_rows(rb), :] = _pack_halves(yacc[blk_rows(rb), :])

        each_block(nblk, pack)
        each_block(nblk, lambda rb: out_copy(start, rb).start())

        @pl.when(w == nw - 1)
        def _():
            each_block(nblk, lambda rb: out_copy(start, rb).wait())

    @pl.when((w == nw - 1) & (c == nc - 1))
    def _zero_tail():
        ypk[blk_rows(0), :] = jnp.zeros((MOE_RB, dh), U32)
        first = wt_ref[0] // MOE_RB

        def tail_copy(b):
            return pltpu.make_async_copy(ypk.at[blk_rows(0)],
                                         y_hbm.at[pl.ds(pl.multiple_of(b * MOE_RB, MOE_RB), MOE_RB)], sem_out)

        def start(b, carry):
            tail_copy(b).start()
            return carry

        def wait(b, carry):
            tail_copy(b).wait()
            return carry

        lax.fori_loop(first, y_hbm.shape[0] // MOE_RB, start, 0)
        lax.fori_loop(first, y_hbm.shape[0] // MOE_RB, wait, 0)


def _moe(we, ws, wn, wt, xs, w1, w3, w2):
    P, dh = xs.shape
    D = 2 * dh
    NE, _, DE = w1.shape
    cw = min(MOE_CW, DE)
    nc = DE // cw
    nw = we.shape[0]
    return pl.pallas_call(
        _moe_kernel,
        grid_spec=pltpu.PrefetchScalarGridSpec(
            num_scalar_prefetch=4,
            grid=(nw, nc),
            in_specs=[pl.BlockSpec(memory_space=pl.ANY),
                      pl.BlockSpec((None, D, cw), lambda w, c, we, ws, wn, wt: (we[w], 0, c)),
                      pl.BlockSpec((None, D, cw), lambda w, c, we, ws, wn, wt: (we[w], 0, c)),
                      pl.BlockSpec((None, cw, D), lambda w, c, we, ws, wn, wt: (we[w], c, 0))],
            out_specs=pl.BlockSpec(memory_space=pl.ANY),
            scratch_shapes=[pltpu.VMEM((MOE_RMAX, dh), U32),
                            pltpu.VMEM((MOE_RMAX, dh), BF16),
                            pltpu.VMEM((MOE_RMAX, dh), BF16),
                            pltpu.VMEM((MOE_RMAX, D), F32),
                            pltpu.VMEM((MOE_RMAX, dh), U32),
                            pltpu.SemaphoreType.DMA(()),
                            pltpu.SemaphoreType.DMA(())]),
        out_shape=jax.ShapeDtypeStruct((P, dh), U32),
        compiler_params=_cparams(("arbitrary", "arbitrary"), 56),
    )(we, ws, wn, wt, xs, w1, w3, w2)


def _moe_schedule(info, NE, p_rows):
    padded = info[1, :NE].astype(I32)
    start_pad = info[2, :NE].astype(I32)
    items = (padded + MOE_RMAX - 1) // MOE_RMAX
    cum_items = jnp.cumsum(items)
    n_items = cum_items[-1]
    nw = (p_rows + NE * (MOE_RMAX - MOE_RB)) // MOE_RMAX
    wi = jnp.arange(nw, dtype=I32)
    valid = wi < n_items
    wi_c = jnp.minimum(wi, jnp.maximum(n_items - 1, 0))
    we = jnp.minimum(jnp.searchsorted(cum_items, wi_c, side='right'), NE - 1).astype(I32)
    local = wi_c - (cum_items[we] - items[we])
    ws = (start_pad[we] + local * MOE_RMAX).astype(I32)
    wn = jnp.where(valid, jnp.clip(padded[we] - local * MOE_RMAX, 0, MOE_RMAX), 0).astype(I32)
    wt = (start_pad[NE - 1] + padded[NE - 1]).reshape(1)
    return we, ws, wn, wt


def _combine_kernel(dest_ref, x1_ref, rt_ref, gt_ref, y_hbm, o_ref, ybuf, sems):
    tm = ROW_TILE
    i = pl.program_id(0)
    n = pl.num_programs(0)
    dh = ybuf.shape[-1]

    def copy(tile, slot, t, k):
        a = (tile * tm + t) * TOP_K + k
        return pltpu.make_async_copy(y_hbm.at[pl.ds(dest_ref[a], 1)], ybuf.at[slot, k, pl.ds(t, 1)], sems.at[slot])

    def issue_tile(tile, slot):
        def body(t, carry):
            for k in range(TOP_K):
                copy(tile, slot, t, k).start()
            return carry
        lax.fori_loop(0, tm, body, 0, unroll=DMA_UNROLL)

    @pl.when(i == 0)
    def _():
        issue_tile(0, 0)

    @pl.when(i + 1 < n)
    def _():
        issue_tile(i + 1, (i + 1) % 2)

    slot = i % 2

    for k in range(TOP_K):
        pltpu.make_async_copy(y_hbm.at[pl.ds(0, tm)], ybuf.at[slot, k], sems.at[slot]).wait()

    rt = rt_ref[...]
    g1 = rt[:, 2:3]
    g2 = rt[:, 3:4]
    lo1, hi1 = _unpack_halves(ybuf[slot, 0])
    lo2, hi2 = _unpack_halves(ybuf[slot, 1])
    o_ref[:, :dh] = x1_ref[:, :dh] + gt_ref[:, :dh] * (g1 * lo1 + g2 * lo2)
    o_ref[:, dh:] = x1_ref[:, dh:] + gt_ref[:, dh:] * (g1 * hi1 + g2 * hi2)


def _combine(dest, x1, route, gt2, ypk, S):
    T, D = x1.shape
    tm = ROW_TILE
    per_b = S // tm
    return pl.pallas_call(
        _combine_kernel,
        grid_spec=pltpu.PrefetchScalarGridSpec(
            num_scalar_prefetch=1,
            grid=(T // tm,),
            in_specs=[pl.BlockSpec((tm, D), lambda i, d: (i, 0)),
                      pl.BlockSpec((tm, LANES), lambda i, d: (i, 0)),
                      pl.BlockSpec((None, 1, D), lambda i, d: (i // per_b, 0, 0)),
                      pl.BlockSpec(memory_space=pl.ANY)],
            out_specs=pl.BlockSpec((tm, D), lambda i, d: (i, 0)),
            scratch_shapes=[pltpu.VMEM((2, TOP_K, tm, D // 2), U32),
                            pltpu.SemaphoreType.DMA((2,))]),
        out_shape=jax.ShapeDtypeStruct((T, D), F32),
        compiler_params=_cparams(("arbitrary",), 32),
    )(dest, x1, route, gt2, ypk)


def kernel(x, c, w_ada, b_ada, norm1_g, norm2_g, w_in, qn_g, kn_g, rel_bias, conv_w, A_log, dt_bias,
           onorm_g, w_out, w_rg, b_rg, w_re, b_re, w1, w3, w2):
    B, S, D = x.shape
    depth = w_ada.shape[0]
    HA, hda = rel_bias.shape[1], qn_g.shape[-1]
    HB, hdb = A_log.shape[-1], onorm_g.shape[-1]
    DA, DB = HA * hda, HB * hdb
    NG, NE = w_rg.shape[-1], w_re.shape[-1]
    T = B * S
    assert hdb == LANES and 2 * hda == LANES and DA + DB == D and 16 * HB <= LANES
    assert NE + NG <= LANES and conv_w.shape[1] == 5 and S % ROW_TILE == 0
    n_main = 3 * DA + 4 * DB
    p_rows = TOP_K * T + NE * MOE_RB

    bias_prof = _attn_bias_profiles(rel_bias, S)
    w_in_b = w_in.astype(BF16)
    x2 = x.reshape(T, D)
    for l in range(depth):
        mod = _ada(c, w_ada[l], b_ada[l]).reshape(B, 6, 1, D)
        sh1, sc1, gt1, sh2, sc2, gt2 = (mod[:, i] for i in range(6))

        w_gate = jnp.tile(w_in[l][:, n_main:], (1, 4))
        w_gate = _hilo_weights(jnp.pad(w_gate, ((0, 0), (0, LANES - w_gate.shape[1]))))
        proj, gat = _inproj(x2, norm1_g[l].reshape(1, D), sc1, sh1, w_in_b[l], n_main, w_gate, S)

        def gate_row(p):
            grp = jnp.concatenate([p.reshape(-1), jnp.zeros((2 * HB,), F32)])
            return jnp.pad(jnp.tile(grp, 4), (0, LANES - 16 * HB)).reshape(1, LANES)

        pack = _gating(gat, gate_row(A_log[l]), gate_row(dt_bias[l]), HB)

        oa = _attention(proj, jnp.tile(qn_g[l], 2).reshape(1, LANES), jnp.tile(kn_g[l], 2).reshape(1, LANES),
                        bias_prof, B, S, HA, hda)
        cw4 = jnp.transpose(conv_w[l].reshape(conv_w.shape[1], 3, HB, hdb), (2, 1, 0, 3))
        ob = _gdn(proj, cw4, pack.reshape(B, S, LANES), onorm_g[l].reshape(1, LANES), B, S, HB, 3 * DA // LANES)

        wr = _hilo_weights(jnp.pad(jnp.concatenate([w_re[l], w_rg[l]], axis=1), ((0, 0), (0, LANES - NE - NG))))
        br = jnp.pad(jnp.concatenate([b_re[l], b_rg[l]]), (0, LANES - NE - NG)).reshape(1, LANES)
        x1, hpk, logits = _outproj(oa, ob, x2, gt1, w_out[l].astype(BF16), norm2_g[l].reshape(1, D),
                                   sc2, sh2, wr, br, S)
        route, info = _route(logits, NE, NG)
        dest = route[:, 4:4 + TOP_K].astype(I32).reshape(TOP_K * T)
        we, ws, wn, wt = _moe_schedule(info, NE, p_rows)
        xs = _dispatch(dest, hpk, p_rows)
        ypk = _moe(we, ws, wn, wt, xs, w1[l], w3[l], w2[l])
        x2 = _combine(dest, x1, route, gt2, ypk, S)
    return x2.reshape(B, S, D)
```

```python
import functools

import numpy as np
import jax
import jax.numpy as jnp
from jax import lax
from jax.experimental import pallas as pl
from jax.experimental.pallas import tpu as pltpu

F32 = jnp.float32
BF16 = jnp.bfloat16
I32 = jnp.int32
U32 = jnp.uint32
HIGHEST = lax.Precision.HIGHEST

EPS = 1e-6
NEG = -1e30
DILATED_BRANCHES = ((128, 1), (512, 4), (2048, 16))
REL_MAX_DIST = 1024
CHUNK = 64
TOP_K = 2

LANES = 128
SUBLANES = 8
MIB = 1 << 20

ATT_QB = 128
ATT_KW = 256
ATT_UNROLL = 8
ATT_PROF_W = 512
GDN_HP = 2
GDN_PREP_UNROLL = 4
GDN_SEG = 512
MOE_RB = 128
MOE_TALL = 512
MOE_RMAX = 512
MOE_WSLOTS = 3
MOE_CW = 512
ROW_TILE = 256
DMA_UNROLL = 8


def _cparams(sem, vmem_mib):
    return pltpu.CompilerParams(dimension_semantics=sem, vmem_limit_bytes=vmem_mib * MIB)


def _dot(a, b, **kw):
    return jnp.dot(a, b, preferred_element_type=F32, **kw)


def _dot_nt(a, b):
    return lax.dot_general(a, b, (((1,), (1,)), ((), ())), preferred_element_type=F32)


def _pick_tile(n, prefs):
    for t in prefs:
        if n % t == 0:
            return t
    return n


def _pack_halves(x):
    half = x.shape[1] // 2
    lo = lax.bitcast_convert_type(x[:, :half].astype(BF16).astype(F32), U32)
    hi = lax.bitcast_convert_type(x[:, half:].astype(BF16).astype(F32), U32)
    return lax.shift_right_logical(lo, jnp.uint32(16)) | (hi & jnp.uint32(0xFFFF0000))


def _hilo_weights(w):
    hi = w.astype(BF16)
    lo = (w - hi.astype(F32)).astype(BF16)
    return jnp.concatenate([hi, lo], axis=1)


def _dot_hilo(x, w2):
    m, n = x.shape[0], w2.shape[1] // 2
    hi = x.astype(BF16)
    lo = (x - hi.astype(F32)).astype(BF16)
    r = _dot(jnp.concatenate([hi, lo], axis=0), w2)
    return r[:m, :n] + (r[:m, n:] + r[m:, :n])


def _unpack_halves(p):
    lo = lax.bitcast_convert_type(lax.shift_left(p, jnp.uint32(16)), F32)
    hi = lax.bitcast_convert_type(p & jnp.uint32(0xFFFF0000), F32)
    return lo, hi


def _ada_kernel(c_ref, w_ref, b_ref, o_ref):
    c = c_ref[...]
    s = (c * jax.nn.sigmoid(c)).astype(BF16)
    o_ref[...] = _dot(s, w_ref[...].astype(BF16)) + b_ref[...]


def _ada(c, w_ada, b_ada):
    B, D = c.shape
    N = w_ada.shape[1]
    tn = _pick_tile(N, (1024, 512, 256, 128))
    return pl.pallas_call(
        _ada_kernel,
        grid=(N // tn,),
        in_specs=[pl.BlockSpec((B, D), lambda j: (0, 0)),
                  pl.BlockSpec((D, tn), lambda j: (0, j)),
                  pl.BlockSpec((1, tn), lambda j: (0, j))],
        out_specs=pl.BlockSpec((B, tn), lambda j: (0, j)),
        out_shape=jax.ShapeDtypeStruct((B, N), F32),
        compiler_params=_cparams(("arbitrary",), 40),
    )(c, w_ada, b_ada.reshape(1, N))


def _inproj_kernel(x_ref, g_ref, sc_ref, sh_ref, w_ref, wg_ref, o_ref, og_ref, h_scr):
    @pl.when(pl.program_id(1) == 0)
    def _():
        x = x_ref[...]
        y = x * lax.rsqrt(jnp.mean(x * x, axis=-1, keepdims=True) + EPS) * g_ref[...]
        h = y * (1.0 + sc_ref[...]) + sh_ref[...]
        h_scr[...] = h.astype(BF16)
        og_ref[...] = _dot_hilo(h, wg_ref[...])

    o_ref[...] = _dot(h_scr[...], w_ref[...]).astype(o_ref.dtype)


def _inproj(x2, g, sc, sh, w_all, n_main, w_gate, S):
    T, D = x2.shape
    NM = n_main
    tm = _pick_tile(S, (1024, 512, 256, 128))
    tn = _pick_tile(NM, (1024, 512, 256, 128))
    per_b = S // tm
    return pl.pallas_call(
        _inproj_kernel,
        grid=(T // tm, NM // tn),
        in_specs=[pl.BlockSpec((tm, D), lambda i, j: (i, 0)),
                  pl.BlockSpec((1, D), lambda i, j: (0, 0)),
                  pl.BlockSpec((None, 1, D), lambda i, j: (i // per_b, 0, 0)),
                  pl.BlockSpec((None, 1, D), lambda i, j: (i // per_b, 0, 0)),
                  pl.BlockSpec((D, tn), lambda i, j: (0, j)),
                  pl.BlockSpec((D, 2 * LANES), lambda i, j: (0, 0))],
        out_specs=[pl.BlockSpec((tm, tn), lambda i, j: (i, j)),
                   pl.BlockSpec((tm, LANES), lambda i, j: (i, 0))],
        out_shape=[jax.ShapeDtypeStruct((T, NM), BF16), jax.ShapeDtypeStruct((T, LANES), F32)],
        scratch_shapes=[pltpu.VMEM((tm, D), BF16)],
        compiler_params=_cparams(("arbitrary", "arbitrary"), 56),
    )(x2, g, sc, sh, w_all, w_gate)


def _gating_kernel(gat_ref, a_ref, dt_ref, o_ref, *, hb):
    gat = gat_ref[...]
    tm = gat.shape[0]
    gw = 4 * hb
    g = -jnp.exp(a_ref[...]) * jax.nn.softplus(gat + dt_ref[...])
    beta = jax.nn.sigmoid(gat)
    r = lax.broadcasted_iota(I32, (tm, tm), 0)
    c = lax.broadcasted_iota(I32, (tm, tm), 1)
    same = (r // CHUNK) == (c // CHUNK)
    pre = _dot(jnp.where(same & (c <= r), 1.0, 0.0), g, precision=HIGHEST)
    suf = _dot(jnp.where(same & (c >= r), 1.0, 0.0), g, precision=HIGHEST)
    lane = lax.broadcasted_iota(I32, gat.shape, 1)
    o_ref[...] = jnp.where(lane // gw == 0, jnp.where(lane % gw < hb, pre, suf), beta)


def _gating(gat, a_row, dt_row, hb):
    T = gat.shape[0]
    tm = ROW_TILE
    return pl.pallas_call(
        functools.partial(_gating_kernel, hb=hb),
        grid=(T // tm,),
        in_specs=[pl.BlockSpec((tm, LANES), lambda i: (i, 0)),
                  pl.BlockSpec((1, LANES), lambda i: (0, 0)),
                  pl.BlockSpec((1, LANES), lambda i: (0, 0))],
        out_specs=pl.BlockSpec((tm, LANES), lambda i: (i, 0)),
        out_shape=jax.ShapeDtypeStruct((T, LANES), F32),
        compiler_params=_cparams(("arbitrary",), 16),
    )(gat, a_row, dt_row)


def _t5_bucket(rel, n_buckets):
    half = n_buckets // 2
    max_exact = half // 2
    n = np.abs(rel)
    large = max_exact + (np.log(np.maximum(n, 1) / max_exact) / np.log(REL_MAX_DIST / max_exact)
                         * (half - max_exact)).astype(np.int32)
    large = np.minimum(large, half - 1)
    return (np.where(rel > 0, half, 0) + np.where(n < max_exact, n, large)).astype(np.int32)


def _attn_plan(S):
    plan, base = [], 0
    for window, dil in DILATED_BRANCHES:
        n = window // (2 * dil)
        L = S // dil
        assert L % ATT_QB == 0 and n * 2 == ATT_QB
        kw = min(ATT_KW, L)
        nbq = L // ATT_QB
        nvar = 1 if nbq == 1 else 3
        plan.append((dil, L, nbq, kw, base, nvar, n))
        base += nvar
    return tuple(plan), base


def _attn_bias_profiles(rel_bias, S):
    plan, nvar_total = _attn_plan(S)
    nbuckets, H = rel_bias.shape
    u = np.arange(ATT_PROF_W) - ATT_QB
    onehots, bands = [], []
    for dil, L, nbq, kw, base, nvar, n in plan:
        offs = [0] if nvar == 1 else [0, -n, -(kw - ATT_QB)]
        for off in offs:
            rel = off + u
            onehots.append(np.eye(nbuckets, dtype=np.float32)[_t5_bucket(rel * dil, nbuckets)])
            bands.append(np.abs(rel) <= n)
    onehot = jnp.asarray(np.stack(onehots))
    band = jnp.asarray(np.stack(bands))
    prof = jnp.einsum('vwn,nh->hvw', onehot, rel_bias.astype(F32), precision=HIGHEST)
    prof = jnp.where(band[None], prof, NEG)
    prof = prof.reshape(H // 2, 2, nvar_total, ATT_PROF_W)
    return jnp.transpose(prof, (0, 2, 1, 3)).reshape(H // 2, 2 * nvar_total, ATT_PROF_W)


def _attn_kernel(q_ref, k_ref, v_ref, qg_ref, kg_ref, prof_ref, o_ref,
                 qn_scr, kn_scr, v_scr, ob_scr, mb_scr, db_scr, bias_ref, *, plan, hd):
    S = q_ref.shape[0]
    lane = lax.broadcasted_iota(I32, (1, LANES), 1)
    left = lane < hd

    @pl.when(pl.program_id(1) == 0)
    def _():
        for row in range(prof_ref.shape[0]):
            rep = jnp.broadcast_to(prof_ref[row:row + 1, :], (ATT_QB, ATT_PROF_W))
            skew = pltpu.roll(rep, 0, 1, stride=1, stride_axis=0)
            bias_ref[row // 2, row % 2] = skew[:, ATT_QB:ATT_QB + ATT_KW]

    def headnorm(x, g):
        x2 = x * x
        s_all = jnp.sum(x2, axis=-1, keepdims=True)
        s_left = jnp.sum(jnp.where(left, x2, 0.0), axis=-1, keepdims=True)
        ms = jnp.where(left, s_left, s_all - s_left) * (1.0 / hd)
        return x * lax.rsqrt(ms + EPS) * g

    qn_scr[...] = headnorm(q_ref[...].astype(F32), qg_ref[...]) * (hd ** -0.5)
    kn_scr[...] = headnorm(k_ref[...].astype(F32), kg_ref[...])
    v_scr[...] = v_ref[...].astype(F32)

    for bi, (dil, L, nbq, kw, base, nvar, n) in enumerate(plan):
        ones = jnp.ones((kw, LANES), BF16)

        def body(t, carry, dil=dil, L=L, nbq=nbq, kw=kw, base=base, nvar=nvar, n=n, bi=bi, ones=ones):
            blocks = []
            for uu in range(ATT_UNROLL):
                idx = t * ATT_UNROLL + uu
                r = idx // nbq
                i = idx % nbq
                q0 = i * ATT_QB
                k0 = jnp.clip(q0 - n, 0, L - kw)
                var = base if nvar == 1 else base + jnp.where(i > 0, 1, 0) + jnp.where(i == nbq - 1, 1, 0)
                if dil == 1:
                    qrows = pl.ds(pl.multiple_of(q0, ATT_QB), ATT_QB)
                    krows = pl.ds(pl.multiple_of(k0, CHUNK), kw)
                else:
                    qrows = pl.ds(r + q0 * dil, ATT_QB, stride=dil)
                    krows = pl.ds(r + k0 * dil, kw, stride=dil)
                qb = qn_scr[qrows, :]
                q2 = jnp.concatenate([jnp.where(left, qb, 0.0), jnp.where(left, 0.0, qb)], axis=0).astype(BF16)
                blocks.append((qrows, krows, var, q2))
            scores = [_dot_nt(q2, kn_scr[krows, :].astype(BF16)) for qrows, krows, var, q2 in blocks]
            probs, maxes = [], []
            for (qrows, krows, var, q2), s in zip(blocks, scores):
                s = s + jnp.concatenate([bias_ref[var, 0][:, :kw], bias_ref[var, 1][:, :kw]], axis=0)
                m = jnp.max(s, axis=-1, keepdims=True)
                probs.append(jnp.exp(s - m).astype(BF16))
                maxes.append(m)
            outs = [_dot(p, jnp.concatenate([v_scr[krows, :].astype(BF16), ones], axis=1))
                    for (qrows, krows, var, q2), p in zip(blocks, probs)]
            for (qrows, krows, var, q2), m, od in zip(blocks, maxes, outs):
                mb = jnp.broadcast_to(m, (2 * ATT_QB, LANES))
                ob_scr[bi, qrows, :] = jnp.where(left, od[:ATT_QB, :LANES], od[ATT_QB:, :LANES])
                mb_scr[bi, qrows, :] = jnp.where(left, mb[:ATT_QB], mb[ATT_QB:])
                db_scr[bi, qrows, :] = jnp.where(left, od[:ATT_QB, LANES:], od[ATT_QB:, LANES:])
            return carry

        assert (dil * nbq) % ATT_UNROLL == 0
        lax.fori_loop(0, dil * nbq // ATT_UNROLL, body, 0)

    nb = len(plan)
    mx = mb_scr[0]
    for bi in range(1, nb):
        mx = jnp.maximum(mx, mb_scr[bi])
    num = jnp.zeros((S, LANES), F32)
    den = jnp.zeros((S, LANES), F32)
    for bi in range(nb):
        w = jnp.exp(mb_scr[bi] - mx)
        num = num + w * ob_scr[bi]
        den = den + w * db_scr[bi]
    o_ref[...] = (num / den).astype(o_ref.dtype)


def _attention(proj, qg2, kg2, profiles, B, S, HA, hd):
    T = proj.shape[0]
    pairs = HA // 2
    da_blocks = HA * hd // LANES
    plan, nvar = _attn_plan(S)
    return pl.pallas_call(
        functools.partial(_attn_kernel, plan=plan, hd=hd),
        grid=(pairs, B),
        in_specs=[pl.BlockSpec((S, LANES), lambda p, b: (b, p)),
                  pl.BlockSpec((S, LANES), lambda p, b: (b, da_blocks + p)),
                  pl.BlockSpec((S, LANES), lambda p, b: (b, 2 * da_blocks + p)),
                  pl.BlockSpec((1, LANES), lambda p, b: (0, 0)),
                  pl.BlockSpec((1, LANES), lambda p, b: (0, 0)),
                  pl.BlockSpec((None, 2 * nvar, ATT_PROF_W), lambda p, b: (p, 0, 0))],
        out_specs=pl.BlockSpec((S, LANES), lambda p, b: (b, p)),
        out_shape=jax.ShapeDtypeStruct((T, HA * hd), BF16),
        scratch_shapes=[pltpu.VMEM((S, LANES), F32)] * 3 + [pltpu.VMEM((len(plan), S, LANES), F32)] * 3
        + [pltpu.VMEM((nvar, 2, ATT_QB, ATT_KW), F32)],
        compiler_params=_cparams(("arbitrary", "arbitrary"), 40),
    )(proj, proj, proj, qg2, kg2, profiles)


def _gdn_kernel(q_ref, k_ref, v_ref, z_ref, cw_ref, pack_ref, og_ref, o_ref,
                q_scr, k_scr, v_scr, xpad_scr, pk2_scr, sel_scr, u_scr, wq_scr, at_scr, kdt_scr, et_scr, oacc_scr,
                *, hb, hp):
    S = q_ref.shape[0]
    P2 = 2 * CHUNK
    W2 = 2 * LANES
    npair = S // P2
    hg = pl.program_id(1)
    gw = 4 * hb
    dk = LANES

    pad = SUBLANES
    for slot in range(2):
        xpad_scr[slot, 0:pad, :] = jnp.zeros((pad, LANES), F32)
        xpad_scr[slot, pad + S:, :] = jnp.zeros((pad, LANES), F32)

    seg = GDN_SEG if S % GDN_SEG == 0 else S

    def conv_silu_to(src_ref, lanes_j, j, which, dst_scr, l2norm, scale):
        xp = xpad_scr.at[(3 * j + which) % 2]
        for s0 in range(0, S, seg):
            xp[pad + s0:pad + s0 + seg, :] = src_ref[s0:s0 + seg, lanes_j].astype(F32)
        for s0 in range(0, S, seg):
            acc = xp[pad + s0:pad + s0 + seg, :] * cw_ref[j, which, 2:3, :]
            for d in (-2, -1, 1, 2):
                acc = acc + xp[pad + s0 + d:pad + s0 + d + seg, :] * cw_ref[j, which, 2 + d:3 + d, :]
            y = acc * jax.nn.sigmoid(acc)
            if l2norm:
                y = y * (lax.rsqrt(jnp.sum(y * y, axis=-1, keepdims=True) + EPS) * scale)
            dst_scr[j, s0:s0 + seg, :] = y

    for s0 in range(0, S, seg):
        pk = pack_ref[s0:s0 + seg, :]
        p_hi = pk.astype(BF16)
        p_lo = (pk - p_hi.astype(F32)).astype(BF16)
        pk2_scr[s0:s0 + seg, :] = jnp.concatenate([p_hi, p_lo], axis=1)
    srow = lax.broadcasted_iota(I32, (W2, 4 * LANES), 0) % LANES
    scol = lax.broadcasted_iota(I32, (W2, 4 * LANES), 1) // LANES
    for j in range(hp):
        lanes_j = slice(j * LANES, (j + 1) * LANES)
        conv_silu_to(q_ref, lanes_j, j, 0, q_scr, True, dk ** -0.5)
        conv_silu_to(k_ref, lanes_j, j, 1, k_scr, True, 1.0)
        conv_silu_to(v_ref, lanes_j, j, 2, v_scr, False, 1.0)
        src = hg * hp + j + jnp.where(scol < 2, scol * hb, gw + 2 * hb + (scol - 2) * hb)
        sel_scr[j] = jnp.where(srow == src, 1.0, 0.0).astype(BF16)
        oacc_scr[j] = jnp.zeros((S, LANES), F32)

    r4 = lax.broadcasted_iota(I32, (CHUNK, W2), 0)
    l4 = lax.broadcasted_iota(I32, (CHUNK, W2), 1)
    c4 = l4 % CHUNK
    blk4 = l4 // CHUNK
    lo_half = (l4 % LANES) < CHUNK
    ahead = jnp.where(l4 >= LANES, r4 - c4, c4 - r4)
    incl = ahead <= 0
    strict = ahead < 0
    bd16 = (r4 // 16) == (c4 // 16)

    def squeeze(x):
        return jnp.where(lo_half, x[:CHUNK], x[CHUNK:])

    def unsqueeze(x):
        return jnp.concatenate([jnp.where(lo_half, x, jnp.zeros_like(x)),
                                jnp.where(lo_half, jnp.zeros_like(x), x)], axis=0)

    def mm4(a, b):
        rhs = jnp.concatenate([jnp.where(blk4 == g, b, 0.0) for g in range(4)], axis=0)
        return _dot(a.astype(BF16), rhs.astype(BF16))

    first = lax.broadcasted_iota(I32, (P2, 1), 0) < CHUNK
    zpair = jnp.zeros((P2, LANES), BF16)

    def bdiag(x):
        return jnp.concatenate([jnp.concatenate([x[:, :LANES], zpair], axis=1),
                                jnp.concatenate([zpair, x[:, LANES:]], axis=1)], axis=0)

    U = GDN_PREP_UNROLL if npair % GDN_PREP_UNROLL == 0 else 1

    def prep(t, carry):
        cx = []
        chains = [(t * U + u, j) for u in range(U) for j in range(hp)]
        bcs = [_dot(pk2_scr[pl.ds(pl.multiple_of(m * P2, P2), P2), :], sel_scr[j]) for m, j in chains]
        for (m, j), bc in zip(chains, bcs):
            rows = pl.ds(pl.multiple_of(m * P2, P2), P2)
            kp = k_scr[j, rows, :]
            qp = q_scr[j, rows, :]
            vp = v_scr[j, rows, :]
            gc2 = bc[:, 0:W2]
            beta2 = bc[:, W2:2 * W2]
            gcf, gcb = gc2[:, :LANES], gc2[:, LANES:]
            tot2 = jnp.concatenate([jnp.where(first, gcf[CHUNK - 1:CHUNK], gcf[P2 - 1:P2]),
                                    jnp.where(first, gcb[0:1], gcb[CHUNK:CHUNK + 1])], axis=1)
            egc2 = jnp.exp(gc2)
            k2 = jnp.concatenate([kp, kp], axis=1)
            kb2 = k2 * beta2
            vb2 = jnp.concatenate([vp, vp], axis=1) * beta2
            kbe2 = kb2 * egc2
            cx.append(dict(
                m=m, j=j, rows=rows, vb2=vb2, kbe2=kbe2,
                qeb=(jnp.concatenate([qp, qp], axis=1) * egc2).astype(BF16),
                kd2=k2 * jnp.exp(tot2 - gc2),
                et=jnp.exp(tot2),
                dec=jnp.exp(jnp.where(
                    incl, squeeze(gc2) - squeeze(jnp.concatenate([gcf.T, gcb.T], axis=1)), -jnp.inf)),
                stk=jnp.concatenate([kb2[:, :LANES], kb2[:, LANES:], qp], axis=0).astype(BF16),
                kpb=kp.astype(BF16)))

        g3s = [_dot_nt(c['stk'], c['kpb']) for c in cx]
        for c, g3 in zip(cx, g3s):
            lm = jnp.where(strict, squeeze(jnp.concatenate([g3[:P2], g3[P2:2 * P2]], axis=1)) * c['dec'], 0.0)
            attn = squeeze(jnp.concatenate([g3[2 * P2:], g3[2 * P2:]], axis=1)) * c['dec']
            c['attn2'] = unsqueeze(attn).astype(BF16)
            c['lbd'] = jnp.where(bd16, lm, 0.0)
            c['loff'] = lm - c['lbd']
        nn = [-c['lbd'] for c in cx]
        pw = [mm4(c['lbd'], c['lbd']) for c in cx]
        for rnd in range(3):
            prod = [mm4(a, p) for a, p in zip(nn, pw)]
            nxt = [mm4(p, p) for p in pw] if rnd < 2 else pw
            nn = [a + p + q for a, p, q in zip(nn, pw, prod)]
            pw = nxt
        mo = [c['loff'] + x for c, x in zip(cx, [mm4(a, c['loff']) for a, c in zip(nn, cx)])]
        m2 = [mm4(x, x) for x in mo]
        mn = [mm4(x, a) for x, a in zip(mo, nn)]
        xo = [a - x - y for a, x, y in zip(nn, mo, mn)]
        mx = [mm4(a, b) for a, b in zip(m2, xo)]
        toff = [a + b + q for a, b, q in zip(xo, m2, mx)]
        z4 = jnp.zeros((CHUNK, W2), BF16)
        tws = []
        for c, tf in zip(cx, toff):
            vbb, kbb = c['vb2'].astype(BF16), c['kbe2'].astype(BF16)
            blocks = []
            for g in range(4):
                rr = slice((g % 2) * CHUNK, (g % 2 + 1) * CHUNK)
                ll = slice((g // 2) * LANES, (g // 2 + 1) * LANES)
                blocks.append(jnp.concatenate([z4] * g + [vbb[rr, ll], kbb[rr, ll]] + [z4] * (3 - g), axis=1))
            tws.append(_dot(tf.astype(BF16), jnp.concatenate(blocks, axis=0)))
        for c, tw in zip(cx, tws):
            m, j, rows = c['m'], c['j'], c['rows']
            def pair_layout(off):
                piece = lambda g: tw[:, 2 * g * LANES + off:2 * g * LANES + off + LANES]
                return jnp.concatenate([jnp.concatenate([piece(0), piece(2)], axis=1),
                                        jnp.concatenate([piece(1), piece(3)], axis=1)], axis=0)

            u2 = c['vb2'] + pair_layout(0)
            w2 = (c['kbe2'] + pair_layout(LANES)).astype(BF16)
            qeb = c['qeb']
            u_scr[j, rows, :] = u2
            wq_scr[j, pl.ds(pl.multiple_of(m * 2 * P2, 2 * P2), 2 * P2), :] = jnp.concatenate(
                [w2[:CHUNK], qeb[:CHUNK], w2[CHUNK:], qeb[CHUNK:]], axis=0)
            at_scr[j, rows, :] = c['attn2']
            kdt_scr[j, 0, :, rows] = c['kd2'][:, :LANES].T.astype(BF16)
            kdt_scr[j, 1, :, rows] = c['kd2'][:, LANES:].T.astype(BF16)
            et_scr[j, pl.ds(pl.multiple_of(m * 2 * SUBLANES, 2 * SUBLANES), 2 * SUBLANES), :] = jnp.concatenate(
                [c['et'][:SUBLANES], c['et'][CHUNK:CHUNK + SUBLANES]], axis=0)
        return carry

    lax.fori_loop(0, npair // U, prep, 0)

    zc = jnp.zeros((CHUNK, LANES), F32)
    zp = jnp.zeros((P2, LANES), F32)

    def place(v, cpos):
        return jnp.concatenate([v, zc] if cpos == 0 else [zc, v], axis=0)

    def scan(m, states):
        pf = m
        pb = npair - 1 - m
        rows_f = pl.ds(pl.multiple_of(pf * P2, P2), P2)
        rows_b = pl.ds(pl.multiple_of(pb * P2, P2), P2)
        hx = []
        for j in range(hp):
            hx.append(dict(
                u_f=u_scr[j, rows_f, :LANES], u_b=u_scr[j, rows_b, LANES:],
                at_f=at_scr[j, rows_f, :LANES], at_b=at_scr[j, rows_b, LANES:],
                kdt=jnp.concatenate([kdt_scr[j, 0, :, rows_f], kdt_scr[j, 1, :, rows_b]], axis=1),
                wq_f=wq_scr[j, pl.ds(pl.multiple_of(pf * 2 * P2, 2 * P2), 2 * P2), :LANES],
                wq_b=wq_scr[j, pl.ds(pl.multiple_of(pb * 2 * P2, 2 * P2), 2 * P2), LANES:],
                et_f=et_scr[j, pl.ds(pl.multiple_of(pf * 2 * SUBLANES, 2 * SUBLANES), 2 * SUBLANES), :LANES],
                et_b=et_scr[j, pl.ds(pl.multiple_of(pb * 2 * SUBLANES, 2 * SUBLANES), 2 * SUBLANES), LANES:]))
        sts = list(states)
        for step in range(2):
            cf, cb = step, 1 - step
            rrs = [_dot(jnp.concatenate([c['wq_f'][cf * P2:(cf + 1) * P2], c['wq_b'][cb * P2:(cb + 1) * P2]], axis=1),
                        bdiag(st.astype(BF16))) for c, st in zip(hx, sts)]
            ress = []
            for c, rr in zip(hx, rrs):
                u2 = jnp.concatenate([c['u_f'][cf * CHUNK:(cf + 1) * CHUNK],
                                      c['u_b'][cb * CHUNK:(cb + 1) * CHUNK]], axis=1)
                v_new = u2 - rr[:CHUNK]
                rhs = jnp.concatenate(
                    [jnp.concatenate([place(v_new[:, :LANES], cf), zp], axis=1),
                     jnp.concatenate([zp, place(v_new[:, LANES:], cb)], axis=1)], axis=0).astype(BF16)
                lhs = jnp.concatenate(
                    [jnp.concatenate([c['at_f'][cf * CHUNK:(cf + 1) * CHUNK],
                                      c['at_b'][cb * CHUNK:(cb + 1) * CHUNK]], axis=1),
                     c['kdt']], axis=0)
                ress.append(_dot(lhs, rhs))
            for j, (c, rr, res) in enumerate(zip(hx, rrs, ress)):
                o2 = rr[CHUNK:] + res[:CHUNK]
                et2 = jnp.concatenate([c['et_f'][cf * SUBLANES:cf * SUBLANES + 1],
                                       c['et_b'][cb * SUBLANES:cb * SUBLANES + 1]], axis=1)
                sts[j] = sts[j] * et2 + res[CHUNK:]
                of_rows = pl.ds(pl.multiple_of(pf * P2 + cf * CHUNK, CHUNK), CHUNK)
                ob_rows = pl.ds(pl.multiple_of(pb * P2 + cb * CHUNK, CHUNK), CHUNK)
                oacc_scr[j, of_rows, :] = oacc_scr[j, of_rows, :] + o2[:, :LANES]
                oacc_scr[j, ob_rows, :] = oacc_scr[j, ob_rows, :] + o2[:, LANES:]
        return tuple(sts)

    s0 = jnp.zeros((dk, W2), F32)
    lax.fori_loop(0, npair, scan, (s0,) * hp)

    for j in range(hp):
        lanes_j = slice(j * LANES, (j + 1) * LANES)
        for s0 in range(0, S, seg):
            o = oacc_scr[j, s0:s0 + seg, :]
            y = o * lax.rsqrt(jnp.mean(o * o, axis=-1, keepdims=True) + EPS) * og_ref[...]
            z = z_ref[s0:s0 + seg, lanes_j].astype(F32)
            o_ref[s0:s0 + seg, lanes_j] = (y * (z * jax.nn.sigmoid(z))).astype(o_ref.dtype)


def _gdn(proj, cw4, pack3, onorm_g, B, S, HB, base_blk):
    T = proj.shape[0]
    hp = GDN_HP if (HB % GDN_HP == 0 and base_blk % GDN_HP == 0) else 1
    wblk = hp * LANES
    npair = S // (2 * CHUNK)

    def col(k):
        off = (base_blk + k * HB) // hp
        return lambda b, h: (b, off + h)

    return pl.pallas_call(
        functools.partial(_gdn_kernel, hb=HB, hp=hp),
        grid=(B, HB // hp),
        in_specs=[pl.BlockSpec((S, wblk), col(0)),
                  pl.BlockSpec((S, wblk), col(1)),
                  pl.BlockSpec((S, wblk), col(2)),
                  pl.BlockSpec((S, wblk), col(3)),
                  pl.BlockSpec((hp, 3, cw4.shape[2], LANES), lambda b, h: (h, 0, 0, 0)),
                  pl.BlockSpec((None, S, LANES), lambda b, h: (b, 0, 0)),
                  pl.BlockSpec((1, LANES), lambda b, h: (0, 0))],
        out_specs=pl.BlockSpec((S, wblk), lambda b, h: (b, h)),
        out_shape=jax.ShapeDtypeStruct((T, HB * LANES), BF16),
        scratch_shapes=[pltpu.VMEM((hp, S, LANES), F32)] * 3
        + [pltpu.VMEM((2, S + 2 * SUBLANES, LANES), F32),
           pltpu.VMEM((S, 2 * LANES), BF16),
           pltpu.VMEM((hp, 2 * LANES, 4 * LANES), BF16),
           pltpu.VMEM((hp, S, 2 * LANES), F32),
           pltpu.VMEM((hp, 2 * S, 2 * LANES), BF16),
           pltpu.VMEM((hp, S, 2 * LANES), BF16),
           pltpu.VMEM((hp, 2, LANES, S), BF16),
           pltpu.VMEM((hp, npair * 2 * SUBLANES, 2 * LANES), F32),
           pltpu.VMEM((hp, S, LANES), F32)],
        compiler_params=_cparams(("arbitrary", "arbitrary"), 56),
    )(proj, proj, proj, proj, cw4, pack3, onorm_g)


def _outproj_kernel(oa_ref, ob_ref, x_ref, gt_ref, w_ref, g_ref, sc_ref, sh_ref, wr_ref, br_ref,
                    x1_ref, hp_ref, lg_ref, h_scr):
    da = oa_ref.shape[1]
    y = _dot(oa_ref[...], w_ref[:da, :]) + _dot(ob_ref[...], w_ref[da:, :])
    x1 = x_ref[...] + gt_ref[...] * y
    x1_ref[...] = x1
    hn = x1 * lax.rsqrt(jnp.mean(x1 * x1, axis=-1, keepdims=True) + EPS) * g_ref[...]
    h = hn * (1.0 + sc_ref[...]) + sh_ref[...]
    hp_ref[...] = _pack_halves(h)
    h_scr[...] = h

    @pl.when(pl.program_id(0) < pl.num_programs(0))
    def _():
        lg_ref[...] = _dot_hilo(h_scr[...], wr_ref[...]) + br_ref[...]


def _outproj(oa, ob, x2, gt1, w_out_b, g2, sc2, sh2, wr, br, S):
    T, D = x2.shape
    tm = ROW_TILE
    per_b = S // tm
    bmap = lambda i: (i // per_b, 0, 0)
    return pl.pallas_call(
        _outproj_kernel,
        grid=(T // tm,),
        in_specs=[pl.BlockSpec((tm, oa.shape[1]), lambda i: (i, 0)),
                  pl.BlockSpec((tm, ob.shape[1]), lambda i: (i, 0)),
                  pl.BlockSpec((tm, D), lambda i: (i, 0)),
                  pl.BlockSpec((None, 1, D), bmap),
                  pl.BlockSpec((D, D), lambda i: (0, 0)),
                  pl.BlockSpec((1, D), lambda i: (0, 0)),
                  pl.BlockSpec((None, 1, D), bmap),
                  pl.BlockSpec((None, 1, D), bmap),
                  pl.BlockSpec((D, 2 * LANES), lambda i: (0, 0)),
                  pl.BlockSpec((1, LANES), lambda i: (0, 0))],
        out_specs=[pl.BlockSpec((tm, D), lambda i: (i, 0)),
                   pl.BlockSpec((tm, D // 2), lambda i: (i, 0)),
                   pl.BlockSpec((tm, LANES), lambda i: (i, 0))],
        out_shape=[jax.ShapeDtypeStruct((T, D), F32), jax.ShapeDtypeStruct((T, D // 2), U32),
                   jax.ShapeDtypeStruct((T, LANES), F32)],
        scratch_shapes=[pltpu.VMEM((tm, D), F32)],
        compiler_params=_cparams(("arbitrary",), 48),
    )(oa, ob, x2, gt1, w_out_b, g2, sc2, sh2, wr, br)


def _route_kernel(lg_ref, o_ref, info_ref, run_scr, *, ne, ng):
    ph = pl.program_id(0)
    i = pl.program_id(1)

    @pl.when((ph == 0) & (i == 0))
    def _():
        run_scr[...] = jnp.zeros_like(run_scr)

    @pl.when((ph == 1) & (i == 0))
    def _():
        cnt = run_scr[...]
        padded = jnp.ceil(cnt * (1.0 / MOE_RB)) * MOE_RB
        k = lax.broadcasted_iota(I32, (LANES, LANES), 0)
        e = lax.broadcasted_iota(I32, (LANES, LANES), 1)
        start = _dot(padded, jnp.where(k < e, 1.0, 0.0), precision=HIGHEST)
        rowi = lax.broadcasted_iota(I32, cnt.shape, 0)
        info_ref[...] = jnp.where(rowi == 0, cnt, jnp.where(rowi == 1, padded, start))
        run_scr[...] = start

    lg = lg_ref[...]
    tm = lg.shape[0]
    epg = ne // ng
    lane_i = lax.broadcasted_iota(I32, lg.shape, 1)
    lane = lane_i.astype(F32)
    big = float(2 * LANES)
    is_g = (lane_i >= ne) & (lane_i < ne + ng)
    gl = jnp.where(is_g, lg, -jnp.inf)
    gmax = jnp.max(gl, axis=-1, keepdims=True)
    gidx = jnp.min(jnp.where(gl == gmax, lane, big), axis=-1, keepdims=True) - ne
    psel = 1.0 / jnp.sum(jnp.where(is_g, jnp.exp(gl - gmax), 0.0), axis=-1, keepdims=True)
    in_grp = (lane_i // epg).astype(F32) == gidx
    el = jnp.where(in_grp & (lane_i < ne), lg, -jnp.inf)
    m1 = jnp.max(el, axis=-1, keepdims=True)
    i1 = jnp.min(jnp.where(el == m1, lane, big), axis=-1, keepdims=True)
    el2 = jnp.where(lane == i1, -jnp.inf, el)
    m2 = jnp.max(el2, axis=-1, keepdims=True)
    i2 = jnp.min(jnp.where(el2 == m2, lane, big), axis=-1, keepdims=True)
    e21 = jnp.exp(m2 - m1)
    g1 = psel / (1.0 + e21)
    g2 = psel * e21 / (1.0 + e21)
    o1 = jnp.where(lane == i1, 1.0, 0.0)
    o2 = jnp.where(lane == i2, 1.0, 0.0)
    cnt = o1 + o2
    r = lax.broadcasted_iota(I32, (tm, tm), 0)
    c = lax.broadcasted_iota(I32, (tm, tm), 1)
    before = _dot(jnp.where(c < r, 1.0, 0.0).astype(BF16), cnt.astype(BF16)) + run_scr[0:1, :]
    d1 = jnp.sum(o1 * before, axis=-1, keepdims=True)
    d2 = jnp.sum(o2 * before, axis=-1, keepdims=True)
    run_scr[...] = run_scr[...] + jnp.sum(cnt, axis=0, keepdims=True)

    @pl.when(ph == 1)
    def _():
        out = jnp.zeros(lg.shape, F32)
        for j, val in enumerate((i1, i2, g1, g2, d1, d2)):
            out = jnp.where(lane_i == j, val, out)
        o_ref[...] = out


def _route(logits, ne, ng):
    T = logits.shape[0]
    tm = _pick_tile(T, (1024, 512, 256))
    return pl.pallas_call(
        functools.partial(_route_kernel, ne=ne, ng=ng),
        grid=(2, T // tm),
        in_specs=[pl.BlockSpec((tm, LANES), lambda p, i: (i, 0))],
        out_specs=[pl.BlockSpec((tm, LANES), lambda p, i: (i * p, 0)),
                   pl.BlockSpec((SUBLANES, LANES), lambda p, i: (0, 0))],
        out_shape=[jax.ShapeDtypeStruct((T, LANES), F32), jax.ShapeDtypeStruct((SUBLANES, LANES), F32)],
        scratch_shapes=[pltpu.VMEM((SUBLANES, LANES), F32)],
        compiler_params=_cparams(("arbitrary", "arbitrary"), 16),
    )(logits)


def _dispatch_kernel(dest_ref, h_hbm, xs_in, xs_hbm, sem):
    del xs_in
    tm = ROW_TILE
    base = pl.program_id(0) * tm

    def copy(tok, a):
        return pltpu.make_async_copy(h_hbm.at[pl.ds(tok, 1)], xs_hbm.at[pl.ds(dest_ref[a], 1)], sem)

    def issue(t, carry):
        for k in range(TOP_K):
            copy(base + t, (base + t) * TOP_K + k).start()
        return carry

    lax.fori_loop(0, tm, issue, 0, unroll=DMA_UNROLL)

    for k in range(TOP_K):
        pltpu.make_async_copy(h_hbm.at[pl.ds(0, tm)], xs_hbm.at[pl.ds(0, tm)], sem).wait()


def _dispatch(dest, hpk, p_rows):
    T, dh = hpk.shape
    return pl.pallas_call(
        _dispatch_kernel,
        grid_spec=pltpu.PrefetchScalarGridSpec(
            num_scalar_prefetch=1,
            grid=(T // ROW_TILE,),
            in_specs=[pl.BlockSpec(memory_space=pl.ANY), pl.BlockSpec(memory_space=pl.ANY)],
            out_specs=pl.BlockSpec(memory_space=pl.ANY),
            scratch_shapes=[pltpu.SemaphoreType.DMA(())]),
        out_shape=jax.ShapeDtypeStruct((p_rows, dh), U32),
        input_output_aliases={2: 0},
        compiler_params=_cparams(("arbitrary",), 16),
    )(dest, hpk, jnp.zeros((p_rows, dh), U32))


def _moe_kernel(we_ref, ws_ref, wn_ref, wt_ref, xs_hbm, w1_hbm, w3_hbm, w2_hbm, y_hbm,
                xbuf, xlo, xhi, yacc, ypk, w1buf, w3buf, w2buf, sem_in, sem_out, sem_w):
    w = pl.program_id(0)
    c = pl.program_id(1)
    nw = pl.num_programs(0)
    nc = pl.num_programs(1)
    nrows = wn_ref[w]
    start = ws_ref[w]
    nblk = nrows // MOE_RB
    dh = xbuf.shape[1]
    cw = w1buf.shape[-1]

    g = w * nc + c

    def w_copies(step, do):
        item = step // nc
        chunk = step % nc

        @pl.when((item < nw) & (wn_ref[jnp.minimum(item, nw - 1)] > 0))
        def _():
            e = we_ref[item]
            cols = pl.ds(pl.multiple_of(chunk * cw, cw), cw)
            slot = step % MOE_WSLOTS
            do(pltpu.make_async_copy(w1_hbm.at[e, :, cols], w1buf.at[slot], sem_w.at[slot]))
            do(pltpu.make_async_copy(w3_hbm.at[e, :, cols], w3buf.at[slot], sem_w.at[slot]))
            do(pltpu.make_async_copy(w2_hbm.at[e, cols, :], w2buf.at[slot], sem_w.at[slot]))

    @pl.when(g == 0)
    def _():
        for ahead in range(MOE_WSLOTS - 1):
            w_copies(ahead, lambda cp: cp.start())

    w_copies(g + MOE_WSLOTS - 1, lambda cp: cp.start())
    w_copies(g, lambda cp: cp.wait())
    wslot = g % MOE_WSLOTS
    w1_ref, w3_ref, w2_ref = w1buf.at[wslot], w3buf.at[wslot], w2buf.at[wslot]

    def blk_rows(rb):
        return pl.ds(pl.multiple_of(rb * MOE_RB, MOE_RB), MOE_RB)

    def hbm_rows(item_start, rb):
        return pl.ds(pl.multiple_of(item_start + rb * MOE_RB, MOE_RB), MOE_RB)

    def in_copy(item_start, rb):
        return pltpu.make_async_copy(xs_hbm.at[hbm_rows(item_start, rb)], xbuf.at[blk_rows(rb)], sem_in)

    def out_copy(item_start, rb):
        return pltpu.make_async_copy(ypk.at[blk_rows(rb)], y_hbm.at[hbm_rows(item_start, rb)], sem_out)

    def each_block(n, fn):
        def body(rb, carry):
            fn(rb)
            return carry
        lax.fori_loop(0, n, body, 0)

    @pl.when(c == 0)
    def _load():
        @pl.when(w == 0)
        def _():
            each_block(nblk, lambda rb: in_copy(start, rb).start())

        each_block(nblk, lambda rb: in_copy(start, rb).wait())

        def unpack(rb):
            lo, hi = _unpack_halves(xbuf[blk_rows(rb), :])
            xlo[blk_rows(rb), :] = lo.astype(BF16)
            xhi[blk_rows(rb), :] = hi.astype(BF16)
            yacc[blk_rows(rb), :] = jnp.zeros((MOE_RB, 2 * dh), F32)

        each_block(nblk, unpack)

        @pl.when(w + 1 < nw)
        def _():
            nxt = ws_ref[w + 1]
            each_block(wn_ref[w + 1] // MOE_RB, lambda rb: in_copy(nxt, rb).start())

    @pl.when(nrows > 0)
    def _compute():
        def rows_block(row0, nr):
            rows = pl.ds(row0, nr)
            xl = xlo[rows, :]
            xh = xhi[rows, :]
            h1 = _dot(xl, w1_ref[:dh, :].astype(BF16)) + _dot(xh, w1_ref[dh:, :].astype(BF16))
            h3 = _dot(xl, w3_ref[:dh, :].astype(BF16)) + _dot(xh, w3_ref[dh:, :].astype(BF16))
            hid = (h1 * jax.nn.sigmoid(h1) * h3).astype(BF16)
            yacc[rows, :] = yacc[rows, :] + _dot(hid, w2_ref[...].astype(BF16))

        ntall = nrows // MOE_TALL
        each_block(ntall, lambda i: rows_block(pl.multiple_of(i * MOE_TALL, MOE_TALL), MOE_TALL))

        rem = nrows - ntall * MOE_TALL
        for nr in range(MOE_RB, MOE_TALL, MOE_RB):
            @pl.when(rem == nr)
            def _(nr=nr):
                rows_block(pl.multiple_of(ntall * MOE_TALL, MOE_TALL), nr)

    @pl.when(c == nc - 1)
    def _store():
        @pl.when(w > 0)
        def _():
            prev = ws_ref[w - 1]
            each_block(wn_ref[w - 1] // MOE_RB, lambda rb: out_copy(prev, rb).wait())

        def pack(rb):
            ypk[blk_rows(rb), :] = _pack_halves(yacc[blk_rows(rb), :])

        each_block(nblk, pack)
        each_block(nblk, lambda rb: out_copy(start, rb).start())

        @pl.when(w == nw - 1)
        def _():
            each_block(nblk, lambda rb: out_copy(start, rb).wait())

    @pl.when((w == nw - 1) & (c == nc - 1))
    def _zero_tail():
        ypk[blk_rows(0), :] = jnp.zeros((MOE_RB, dh), U32)
        first = wt_ref[0] // MOE_RB

        def tail_copy(b):
            return pltpu.make_async_copy(ypk.at[blk_rows(0)],
                                         y_hbm.at[pl.ds(pl.multiple_of(b * MOE_RB, MOE_RB), MOE_RB)], sem_out)

        def start(b, carry):
            tail_copy(b).start()
            return carry

        def wait(b, carry):
            tail_copy(b).wait()
            return carry

        lax.fori_loop(first, y_hbm.shape[0] // MOE_RB, start, 0)
        lax.fori_loop(first, y_hbm.shape[0] // MOE_RB, wait, 0)


def _moe(we, ws, wn, wt, xs, w1, w3, w2):
    P, dh = xs.shape
    D = 2 * dh
    NE, _, DE = w1.shape
    cw = min(MOE_CW, DE)
    nc = DE // cw
    nw = we.shape[0]
    return pl.pallas_call(
        _moe_kernel,
        grid_spec=pltpu.PrefetchScalarGridSpec(
            num_scalar_prefetch=4,
            grid=(nw, nc),
            in_specs=[pl.BlockSpec(memory_space=pl.ANY)] * 4,
            out_specs=pl.BlockSpec(memory_space=pl.ANY),
            scratch_shapes=[pltpu.VMEM((MOE_RMAX, dh), U32),
                            pltpu.VMEM((MOE_RMAX, dh), BF16),
                            pltpu.VMEM((MOE_RMAX, dh), BF16),
                            pltpu.VMEM((MOE_RMAX, D), F32),
                            pltpu.VMEM((MOE_RMAX, dh), U32),
                            pltpu.VMEM((MOE_WSLOTS, D, cw), F32),
                            pltpu.VMEM((MOE_WSLOTS, D, cw), F32),
                            pltpu.VMEM((MOE_WSLOTS, cw, D), F32),
                            pltpu.SemaphoreType.DMA(()),
                            pltpu.SemaphoreType.DMA(()),
                            pltpu.SemaphoreType.DMA((MOE_WSLOTS,))]),
        out_shape=jax.ShapeDtypeStruct((P, dh), U32),
        compiler_params=_cparams(("arbitrary", "arbitrary"), 56),
    )(we, ws, wn, wt, xs, w1, w3, w2)


def _moe_schedule(info, NE, p_rows):
    padded = info[1, :NE].astype(I32)
    start_pad = info[2, :NE].astype(I32)
    items = (padded + MOE_RMAX - 1) // MOE_RMAX
    cum_items = jnp.cumsum(items)
    n_items = cum_items[-1]
    nw = (p_rows + NE * (MOE_RMAX - MOE_RB)) // MOE_RMAX
    wi = jnp.arange(nw, dtype=I32)
    valid = wi < n_items
    wi_c = jnp.minimum(wi, jnp.maximum(n_items - 1, 0))
    we = jnp.minimum(jnp.searchsorted(cum_items, wi_c, side='right'), NE - 1).astype(I32)
    local = wi_c - (cum_items[we] - items[we])
    ws = (start_pad[we] + local * MOE_RMAX).astype(I32)
    wn = jnp.where(valid, jnp.clip(padded[we] - local * MOE_RMAX, 0, MOE_RMAX), 0).astype(I32)
    wt = (start_pad[NE - 1] + padded[NE - 1]).reshape(1)
    return we, ws, wn, wt


def _combine_kernel(dest_ref, x1_ref, rt_ref, gt_ref, y_hbm, o_ref, ybuf, sems):
    tm = ROW_TILE
    i = pl.program_id(0)
    n = pl.num_programs(0)
    dh = ybuf.shape[-1]

    def copy(tile, slot, t, k):
        a = (tile * tm + t) * TOP_K + k
        return pltpu.make_async_copy(y_hbm.at[pl.ds(dest_ref[a], 1)], ybuf.at[slot, k, pl.ds(t, 1)], sems.at[slot])

    def issue_tile(tile, slot):
        def body(t, carry):
            for k in range(TOP_K):
                copy(tile, slot, t, k).start()
            return carry
        lax.fori_loop(0, tm, body, 0, unroll=DMA_UNROLL)

    @pl.when(i == 0)
    def _():
        issue_tile(0, 0)

    @pl.when(i + 1 < n)
    def _():
        issue_tile(i + 1, (i + 1) % 2)

    slot = i % 2

    for k in range(TOP_K):
        pltpu.make_async_copy(y_hbm.at[pl.ds(0, tm)], ybuf.at[slot, k], sems.at[slot]).wait()

    rt = rt_ref[...]
    g1 = rt[:, 2:3]
    g2 = rt[:, 3:4]
    lo1, hi1 = _unpack_halves(ybuf[slot, 0])
    lo2, hi2 = _unpack_halves(ybuf[slot, 1])
    o_ref[:, :dh] = x1_ref[:, :dh] + gt_ref[:, :dh] * (g1 * lo1 + g2 * lo2)
    o_ref[:, dh:] = x1_ref[:, dh:] + gt_ref[:, dh:] * (g1 * hi1 + g2 * hi2)


def _combine(dest, x1, route, gt2, ypk, S):
    T, D = x1.shape
    tm = ROW_TILE
    per_b = S // tm
    return pl.pallas_call(
        _combine_kernel,
        grid_spec=pltpu.PrefetchScalarGridSpec(
            num_scalar_prefetch=1,
            grid=(T // tm,),
            in_specs=[pl.BlockSpec((tm, D), lambda i, d: (i, 0)),
                      pl.BlockSpec((tm, LANES), lambda i, d: (i, 0)),
                      pl.BlockSpec((None, 1, D), lambda i, d: (i // per_b, 0, 0)),
                      pl.BlockSpec(memory_space=pl.ANY)],
            out_specs=pl.BlockSpec((tm, D), lambda i, d: (i, 0)),
            scratch_shapes=[pltpu.VMEM((2, TOP_K, tm, D // 2), U32),
                            pltpu.SemaphoreType.DMA((2,))]),
        out_shape=jax.ShapeDtypeStruct((T, D), F32),
        compiler_params=_cparams(("arbitrary",), 32),
    )(dest, x1, route, gt2, ypk)


def kernel(x, c, w_ada, b_ada, norm1_g, norm2_g, w_in, qn_g, kn_g, rel_bias, conv_w, A_log, dt_bias,
           onorm_g, w_out, w_rg, b_rg, w_re, b_re, w1, w3, w2):
    B, S, D = x.shape
    depth = w_ada.shape[0]
    HA, hda = rel_bias.shape[1], qn_g.shape[-1]
    HB, hdb = A_log.shape[-1], onorm_g.shape[-1]
    DA, DB = HA * hda, HB * hdb
    NG, NE = w_rg.shape[-1], w_re.shape[-1]
    T = B * S
    assert hdb == LANES and 2 * hda == LANES and DA + DB == D and 16 * HB <= LANES
    assert NE + NG <= LANES and conv_w.shape[1] == 5 and S % ROW_TILE == 0
    n_main = 3 * DA + 4 * DB
    p_rows = TOP_K * T + NE * MOE_RB

    bias_prof = _attn_bias_profiles(rel_bias, S)
    w_in_b = w_in.astype(BF16)
    x2 = x.reshape(T, D)
    for l in range(depth):
        mod = _ada(c, w_ada[l], b_ada[l]).reshape(B, 6, 1, D)
        sh1, sc1, gt1, sh2, sc2, gt2 = (mod[:, i] for i in range(6))

        w_gate = jnp.tile(w_in[l][:, n_main:], (1, 4))
        w_gate = _hilo_weights(jnp.pad(w_gate, ((0, 0), (0, LANES - w_gate.shape[1]))))
        proj, gat = _inproj(x2, norm1_g[l].reshape(1, D), sc1, sh1, w_in_b[l], n_main, w_gate, S)

        def gate_row(p):
            grp = jnp.concatenate([p.reshape(-1), jnp.zeros((2 * HB,), F32)])
            return jnp.pad(jnp.tile(grp, 4), (0, LANES - 16 * HB)).reshape(1, LANES)

        pack = _gating(gat, gate_row(A_log[l]), gate_row(dt_bias[l]), HB)

        oa = _attention(proj, jnp.tile(qn_g[l], 2).reshape(1, LANES), jnp.tile(kn_g[l], 2).reshape(1, LANES),
                        bias_prof, B, S, HA, hda)
        cw4 = jnp.transpose(conv_w[l].reshape(conv_w.shape[1], 3, HB, hdb), (2, 1, 0, 3))
        ob = _gdn(proj, cw4, pack.reshape(B, S, LANES), onorm_g[l].reshape(1, LANES), B, S, HB, 3 * DA // LANES)

        wr = _hilo_weights(jnp.pad(jnp.concatenate([w_re[l], w_rg[l]], axis=1), ((0, 0), (0, LANES - NE - NG))))
        br = jnp.pad(jnp.concatenate([b_re[l], b_rg[l]]), (0, LANES - NE - NG)).reshape(1, LANES)
        x1, hpk, logits = _outproj(oa, ob, x2, gt1, w_out[l].astype(BF16), norm2_g[l].reshape(1, D),
                                   sc2, sh2, wr, br, S)
        route, info = _route(logits, NE, NG)
        dest = route[:, 4:4 + TOP_K].astype(I32).reshape(TOP_K * T)
        we, ws, wn, wt = _moe_schedule(info, NE, p_rows)
        xs = _dispatch(dest, hpk, p_rows)
        ypk = _moe(we, ws, wn, wt, xs, w1[l], w3[l], w2[l])
        x2 = _combine(dest, x1, route, gt2, ypk, S)
    return x2.reshape(B, S, D)
```

```python
import functools

import numpy as np
import jax
import jax.numpy as jnp
from jax import lax
from jax.experimental import pallas as pl
from jax.experimental.pallas import tpu as pltpu

F32 = jnp.float32
BF16 = jnp.bfloat16
I32 = jnp.int32
U32 = jnp.uint32
HIGHEST = lax.Precision.HIGHEST

EPS = 1e-6
NEG = -1e30
DILATED_BRANCHES = ((128, 1), (512, 4), (2048, 16))
REL_MAX_DIST = 1024
CHUNK = 64
TOP_K = 2

LANES = 128
SUBLANES = 8
MIB = 1 << 20

ATT_QB = 128
ATT_KW = 256
ATT_UNROLL = 8
ATT_PROF_W = 512
GDN_HP = 2
GDN_PREP_UNROLL = 4
GDN_SEG = 512
MOE_RB = 128
MOE_TALL = 512
MOE_RMAX = 512
MOE_WSLOTS = 3
MOE_CW = 512
ROW_TILE = 256
DMA_UNROLL = 8


def _cparams(sem, vmem_mib):
    return pltpu.CompilerParams(dimension_semantics=sem, vmem_limit_bytes=vmem_mib * MIB)


def _dot(a, b, **kw):
    return jnp.dot(a, b, preferred_element_type=F32, **kw)


def _dot_nt(a, b):
    return lax.dot_general(a, b, (((1,), (1,)), ((), ())), preferred_element_type=F32)


def _pick_tile(n, prefs):
    for t in prefs:
        if n % t == 0:
            return t
    return n


def _pack_halves(x):
    half = x.shape[1] // 2
    lo = lax.bitcast_convert_type(x[:, :half].astype(BF16).astype(F32), U32)
    hi = lax.bitcast_convert_type(x[:, half:].astype(BF16).astype(F32), U32)
    return lax.shift_right_logical(lo, jnp.uint32(16)) | (hi & jnp.uint32(0xFFFF0000))


def _silu(x):
    h = 0.5 * x
    return h + h * jnp.tanh(h)


def _hilo_weights(w):
    hi = w.astype(BF16)
    lo = (w - hi.astype(F32)).astype(BF16)
    return jnp.concatenate([hi, lo], axis=1)


def _dot_hilo(x, w2):
    m, n = x.shape[0], w2.shape[1] // 2
    hi = x.astype(BF16)
    lo = (x - hi.astype(F32)).astype(BF16)
    r = _dot(jnp.concatenate([hi, lo], axis=0), w2)
    return r[:m, :n] + (r[:m, n:] + r[m:, :n])


def _unpack_halves(p):
    lo = lax.bitcast_convert_type(lax.shift_left(p, jnp.uint32(16)), F32)
    hi = lax.bitcast_convert_type(p & jnp.uint32(0xFFFF0000), F32)
    return lo, hi


def _ada_kernel(c_ref, w_ref, b_ref, o_ref):
    c = c_ref[...]
    s = _silu(c).astype(BF16)
    o_ref[...] = _dot(s, w_ref[...].astype(BF16)) + b_ref[...]


def _ada(c, w_ada, b_ada):
    B, D = c.shape
    N = w_ada.shape[1]
    tn = _pick_tile(N, (1024, 512, 256, 128))
    return pl.pallas_call(
        _ada_kernel,
        grid=(N // tn,),
        in_specs=[pl.BlockSpec((B, D), lambda j: (0, 0)),
                  pl.BlockSpec((D, tn), lambda j: (0, j)),
                  pl.BlockSpec((1, tn), lambda j: (0, j))],
        out_specs=pl.BlockSpec((B, tn), lambda j: (0, j)),
        out_shape=jax.ShapeDtypeStruct((B, N), F32),
        compiler_params=_cparams(("arbitrary",), 40),
    )(c, w_ada, b_ada.reshape(1, N))


def _inproj_kernel(x_ref, g_ref, sc_ref, sh_ref, w_ref, wg_ref, o_ref, og_ref, h_scr):
    @pl.when(pl.program_id(1) == 0)
    def _():
        x = x_ref[...]
        y = x * lax.rsqrt(jnp.mean(x * x, axis=-1, keepdims=True) + EPS) * g_ref[...]
        h = y * (1.0 + sc_ref[...]) + sh_ref[...]
        h_scr[...] = h.astype(BF16)
        og_ref[...] = _dot_hilo(h, wg_ref[...])

    o_ref[...] = _dot(h_scr[...], w_ref[...]).astype(o_ref.dtype)


def _inproj(x2, g, sc, sh, w_all, n_main, w_gate, S):
    T, D = x2.shape
    NM = n_main
    tm = _pick_tile(S, (1024, 512, 256, 128))
    tn = _pick_tile(NM, (1024, 512, 256, 128))
    per_b = S // tm
    return pl.pallas_call(
        _inproj_kernel,
        grid=(T // tm, NM // tn),
        in_specs=[pl.BlockSpec((tm, D), lambda i, j: (i, 0)),
                  pl.BlockSpec((1, D), lambda i, j: (0, 0)),
                  pl.BlockSpec((None, 1, D), lambda i, j: (i // per_b, 0, 0)),
                  pl.BlockSpec((None, 1, D), lambda i, j: (i // per_b, 0, 0)),
                  pl.BlockSpec((D, tn), lambda i, j: (0, j)),
                  pl.BlockSpec((D, 2 * LANES), lambda i, j: (0, 0))],
        out_specs=[pl.BlockSpec((tm, tn), lambda i, j: (i, j)),
                   pl.BlockSpec((tm, LANES), lambda i, j: (i, 0))],
        out_shape=[jax.ShapeDtypeStruct((T, NM), BF16), jax.ShapeDtypeStruct((T, LANES), F32)],
        scratch_shapes=[pltpu.VMEM((tm, D), BF16)],
        compiler_params=_cparams(("arbitrary", "arbitrary"), 56),
    )(x2, g, sc, sh, w_all, w_gate)


def _gating_kernel(gat_ref, a_ref, dt_ref, o_ref, *, hb):
    gat = gat_ref[...]
    tm = gat.shape[0]
    gw = 4 * hb
    g = -jnp.exp(a_ref[...]) * jax.nn.softplus(gat + dt_ref[...])
    beta = jax.nn.sigmoid(gat)
    r = lax.broadcasted_iota(I32, (tm, tm), 0)
    c = lax.broadcasted_iota(I32, (tm, tm), 1)
    same = (r // CHUNK) == (c // CHUNK)
    pre = _dot(jnp.where(same & (c <= r), 1.0, 0.0), g, precision=HIGHEST)
    suf = _dot(jnp.where(same & (c >= r), 1.0, 0.0), g, precision=HIGHEST)
    lane = lax.broadcasted_iota(I32, gat.shape, 1)
    o_ref[...] = jnp.where(lane // gw == 0, jnp.where(lane % gw < hb, pre, suf), beta)


def _gating(gat, a_row, dt_row, hb):
    T = gat.shape[0]
    tm = ROW_TILE
    return pl.pallas_call(
        functools.partial(_gating_kernel, hb=hb),
        grid=(T // tm,),
        in_specs=[pl.BlockSpec((tm, LANES), lambda i: (i, 0)),
                  pl.BlockSpec((1, LANES), lambda i: (0, 0)),
                  pl.BlockSpec((1, LANES), lambda i: (0, 0))],
        out_specs=pl.BlockSpec((tm, LANES), lambda i: (i, 0)),
        out_shape=jax.ShapeDtypeStruct((T, LANES), F32),
        compiler_params=_cparams(("arbitrary",), 16),
    )(gat, a_row, dt_row)


def _t5_bucket(rel, n_buckets):
    half = n_buckets // 2
    max_exact = half // 2
    n = np.abs(rel)
    large = max_exact + (np.log(np.maximum(n, 1) / max_exact) / np.log(REL_MAX_DIST / max_exact)
                         * (half - max_exact)).astype(np.int32)
    large = np.minimum(large, half - 1)
    return (np.where(rel > 0, half, 0) + np.where(n < max_exact, n, large)).astype(np.int32)


def _attn_plan(S):
    plan, base = [], 0
    for window, dil in DILATED_BRANCHES:
        n = window // (2 * dil)
        L = S // dil
        assert L % ATT_QB == 0 and n * 2 == ATT_QB
        kw = min(ATT_KW, L)
        nbq = L // ATT_QB
        nvar = 1 if nbq == 1 else 3
        plan.append((dil, L, nbq, kw, base, nvar, n))
        base += nvar
    return tuple(plan), base


def _attn_bias_profiles(rel_bias, S):
    plan, nvar_total = _attn_plan(S)
    nbuckets, H = rel_bias.shape
    u = np.arange(ATT_PROF_W) - ATT_QB
    onehots, bands = [], []
    for dil, L, nbq, kw, base, nvar, n in plan:
        offs = [0] if nvar == 1 else [0, -n, -(kw - ATT_QB)]
        for off in offs:
            rel = off + u
            onehots.append(np.eye(nbuckets, dtype=np.float32)[_t5_bucket(rel * dil, nbuckets)])
            bands.append(np.abs(rel) <= n)
    onehot = jnp.asarray(np.stack(onehots))
    band = jnp.asarray(np.stack(bands))
    prof = jnp.einsum('vwn,nh->hvw', onehot, rel_bias.astype(F32), precision=HIGHEST)
    prof = jnp.where(band[None], prof, NEG)
    prof = prof.reshape(H // 2, 2, nvar_total, ATT_PROF_W)
    return jnp.transpose(prof, (0, 2, 1, 3)).reshape(H // 2, 2 * nvar_total, ATT_PROF_W)


def _attn_kernel(q_ref, k_ref, v_ref, qg_ref, kg_ref, prof_ref, o_ref,
                 qn_scr, kn_scr, v_scr, ob_scr, mb_scr, db_scr, bias_ref, *, plan, hd):
    S = q_ref.shape[0]
    lane = lax.broadcasted_iota(I32, (1, LANES), 1)
    left = lane < hd

    @pl.when(pl.program_id(1) == 0)
    def _():
        for row in range(prof_ref.shape[0]):
            rep = jnp.broadcast_to(prof_ref[row:row + 1, :], (ATT_QB, ATT_PROF_W))
            skew = pltpu.roll(rep, 0, 1, stride=1, stride_axis=0)
            bias_ref[row // 2, row % 2] = skew[:, ATT_QB:ATT_QB + ATT_KW]

    def headnorm(x, g):
        x2 = x * x
        s_all = jnp.sum(x2, axis=-1, keepdims=True)
        s_left = jnp.sum(jnp.where(left, x2, 0.0), axis=-1, keepdims=True)
        ms = jnp.where(left, s_left, s_all - s_left) * (1.0 / hd)
        return x * lax.rsqrt(ms + EPS) * g

    qn_scr[...] = headnorm(q_ref[...].astype(F32), qg_ref[...]) * (hd ** -0.5)
    kn_scr[...] = headnorm(k_ref[...].astype(F32), kg_ref[...])
    v_scr[...] = v_ref[...].astype(F32)

    for bi, (dil, L, nbq, kw, base, nvar, n) in enumerate(plan):
        ones = jnp.ones((kw, LANES), BF16)

        def body(t, carry, dil=dil, L=L, nbq=nbq, kw=kw, base=base, nvar=nvar, n=n, bi=bi, ones=ones):
            blocks = []
            for uu in range(ATT_UNROLL):
                idx = t * ATT_UNROLL + uu
                r = idx // nbq
                i = idx % nbq
                q0 = i * ATT_QB
                k0 = jnp.clip(q0 - n, 0, L - kw)
                var = base if nvar == 1 else base + jnp.where(i > 0, 1, 0) + jnp.where(i == nbq - 1, 1, 0)
                if dil == 1:
                    qrows = pl.ds(pl.multiple_of(q0, ATT_QB), ATT_QB)
                    krows = pl.ds(pl.multiple_of(k0, CHUNK), kw)
                else:
                    qrows = pl.ds(r + q0 * dil, ATT_QB, stride=dil)
                    krows = pl.ds(r + k0 * dil, kw, stride=dil)
                qb = qn_scr[qrows, :]
                q2 = jnp.concatenate([jnp.where(left, qb, 0.0), jnp.where(left, 0.0, qb)], axis=0).astype(BF16)
                blocks.append((qrows, krows, var, q2))
            def score(group):
                return [_dot_nt(q2, kn_scr[krows, :].astype(BF16)) for qrows, krows, var, q2 in group]

            def softmax(group, scores):
                probs, maxes = [], []
                for (qrows, krows, var, q2), s in zip(group, scores):
                    s = s + jnp.concatenate([bias_ref[var, 0][:, :kw], bias_ref[var, 1][:, :kw]], axis=0)
                    m = jnp.max(s, axis=-1, keepdims=True)
                    probs.append(jnp.exp(s - m).astype(BF16))
                    maxes.append(m)
                return probs, maxes

            def values(group, probs):
                return [_dot(p, jnp.concatenate([v_scr[krows, :].astype(BF16), ones], axis=1))
                        for (qrows, krows, var, q2), p in zip(group, probs)]

            def finish(group, maxes, outs):
                for (qrows, krows, var, q2), m, od in zip(group, maxes, outs):
                    mb = jnp.broadcast_to(m, (2 * ATT_QB, LANES))
                    ob_scr[bi, qrows, :] = jnp.where(left, od[:ATT_QB, :LANES], od[ATT_QB:, :LANES])
                    mb_scr[bi, qrows, :] = jnp.where(left, mb[:ATT_QB], mb[ATT_QB:])
                    db_scr[bi, qrows, :] = jnp.where(left, od[:ATT_QB, LANES:], od[ATT_QB:, LANES:])

            ga, gb = blocks[:ATT_UNROLL // 2], blocks[ATT_UNROLL // 2:]
            sa = score(ga)
            sb = score(gb)
            pa, ma = softmax(ga, sa)
            oa = values(ga, pa)
            pb, mbx = softmax(gb, sb)
            ob = values(gb, pb)
            finish(ga, ma, oa)
            finish(gb, mbx, ob)
            return carry

        assert (dil * nbq) % ATT_UNROLL == 0
        lax.fori_loop(0, dil * nbq // ATT_UNROLL, body, 0)

    nb = len(plan)
    mx = mb_scr[0]
    for bi in range(1, nb):
        mx = jnp.maximum(mx, mb_scr[bi])
    num = jnp.zeros((S, LANES), F32)
    den = jnp.zeros((S, LANES), F32)
    for bi in range(nb):
        w = jnp.exp(mb_scr[bi] - mx)
        num = num + w * ob_scr[bi]
        den = den + w * db_scr[bi]
    o_ref[...] = (num / den).astype(o_ref.dtype)


def _attention(proj, qg2, kg2, profiles, B, S, HA, hd):
    T = proj.shape[0]
    pairs = HA // 2
    da_blocks = HA * hd // LANES
    plan, nvar = _attn_plan(S)
    return pl.pallas_call(
        functools.partial(_attn_kernel, plan=plan, hd=hd),
        grid=(pairs, B),
        in_specs=[pl.BlockSpec((S, LANES), lambda p, b: (b, p)),
                  pl.BlockSpec((S, LANES), lambda p, b: (b, da_blocks + p)),
                  pl.BlockSpec((S, LANES), lambda p, b: (b, 2 * da_blocks + p)),
                  pl.BlockSpec((1, LANES), lambda p, b: (0, 0)),
                  pl.BlockSpec((1, LANES), lambda p, b: (0, 0)),
                  pl.BlockSpec((None, 2 * nvar, ATT_PROF_W), lambda p, b: (p, 0, 0))],
        out_specs=pl.BlockSpec((S, LANES), lambda p, b: (b, p)),
        out_shape=jax.ShapeDtypeStruct((T, HA * hd), BF16),
        scratch_shapes=[pltpu.VMEM((S, LANES), F32)] * 3 + [pltpu.VMEM((len(plan), S, LANES), F32)] * 3
        + [pltpu.VMEM((nvar, 2, ATT_QB, ATT_KW), F32)],
        compiler_params=_cparams(("arbitrary", "arbitrary"), 40),
    )(proj, proj, proj, qg2, kg2, profiles)


def _gdn_kernel(q_ref, k_ref, v_ref, z_ref, cw_ref, pack_ref, og_ref, o_ref,
                q_scr, k_scr, v_scr, xpad_scr, pk2_scr, sel_scr, u_scr, wq_scr, at_scr, kdt_scr, et_scr, oacc_scr,
                *, hb, hp):
    S = q_ref.shape[0]
    P2 = 2 * CHUNK
    W2 = 2 * LANES
    npair = S // P2
    hg = pl.program_id(1)
    gw = 4 * hb
    dk = LANES

    pad = SUBLANES
    for slot in range(2):
        xpad_scr[slot, 0:pad, :] = jnp.zeros((pad, LANES), F32)
        xpad_scr[slot, pad + S:, :] = jnp.zeros((pad, LANES), F32)

    seg = GDN_SEG if S % GDN_SEG == 0 else S

    def conv_silu_to(src_ref, lanes_j, j, which, dst_scr, l2norm, scale):
        xp = xpad_scr.at[(3 * j + which) % 2]
        for s0 in range(0, S, seg):
            xp[pad + s0:pad + s0 + seg, :] = src_ref[s0:s0 + seg, lanes_j].astype(F32)
        for s0 in range(0, S, seg):
            acc = xp[pad + s0:pad + s0 + seg, :] * cw_ref[j, which, 2:3, :]
            for d in (-2, -1, 1, 2):
                acc = acc + xp[pad + s0 + d:pad + s0 + d + seg, :] * cw_ref[j, which, 2 + d:3 + d, :]
            y = _silu(acc)
            if l2norm:
                y = y * (lax.rsqrt(jnp.sum(y * y, axis=-1, keepdims=True) + EPS) * scale)
            dst_scr[j, s0:s0 + seg, :] = y

    for s0 in range(0, S, seg):
        pk = pack_ref[s0:s0 + seg, :]
        p_hi = pk.astype(BF16)
        p_lo = (pk - p_hi.astype(F32)).astype(BF16)
        pk2_scr[s0:s0 + seg, :] = jnp.concatenate([p_hi, p_lo], axis=1)
    srow = lax.broadcasted_iota(I32, (W2, 4 * LANES), 0) % LANES
    scol = lax.broadcasted_iota(I32, (W2, 4 * LANES), 1) // LANES
    for j in range(hp):
        lanes_j = slice(j * LANES, (j + 1) * LANES)
        conv_silu_to(q_ref, lanes_j, j, 0, q_scr, True, dk ** -0.5)
        conv_silu_to(k_ref, lanes_j, j, 1, k_scr, True, 1.0)
        conv_silu_to(v_ref, lanes_j, j, 2, v_scr, False, 1.0)
        src = hg * hp + j + jnp.where(scol < 2, scol * hb, gw + 2 * hb + (scol - 2) * hb)
        sel_scr[j] = jnp.where(srow == src, 1.0, 0.0).astype(BF16)
        oacc_scr[j] = jnp.zeros((S, LANES), F32)

    r4 = lax.broadcasted_iota(I32, (CHUNK, W2), 0)
    l4 = lax.broadcasted_iota(I32, (CHUNK, W2), 1)
    c4 = l4 % CHUNK
    blk4 = l4 // CHUNK
    lo_half = (l4 % LANES) < CHUNK
    ahead = jnp.where(l4 >= LANES, r4 - c4, c4 - r4)
    incl = ahead <= 0
    strict = ahead < 0
    bd16 = (r4 // 16) == (c4 // 16)

    def squeeze(x):
        return jnp.where(lo_half, x[:CHUNK], x[CHUNK:])

    def unsqueeze(x):
        return jnp.concatenate([jnp.where(lo_half, x, jnp.zeros_like(x)),
                                jnp.where(lo_half, jnp.zeros_like(x), x)], axis=0)

    def mm4(a, b):
        rhs = jnp.concatenate([jnp.where(blk4 == g, b, 0.0) for g in range(4)], axis=0)
        return _dot(a.astype(BF16), rhs.astype(BF16))

    first = lax.broadcasted_iota(I32, (P2, 1), 0) < CHUNK
    zpair = jnp.zeros((P2, LANES), BF16)

    def bdiag(x):
        return jnp.concatenate([jnp.concatenate([x[:, :LANES], zpair], axis=1),
                                jnp.concatenate([zpair, x[:, LANES:]], axis=1)], axis=0)

    U = GDN_PREP_UNROLL if npair % GDN_PREP_UNROLL == 0 else 1

    def prep(t, carry):
        cx = []
        chains = [(t * U + u, j) for u in range(U) for j in range(hp)]
        bcs = [_dot(pk2_scr[pl.ds(pl.multiple_of(m * P2, P2), P2), :], sel_scr[j]) for m, j in chains]
        for (m, j), bc in zip(chains, bcs):
            rows = pl.ds(pl.multiple_of(m * P2, P2), P2)
            kp = k_scr[j, rows, :]
            qp = q_scr[j, rows, :]
            vp = v_scr[j, rows, :]
            gc2 = bc[:, 0:W2]
            beta2 = bc[:, W2:2 * W2]
            gcf, gcb = gc2[:, :LANES], gc2[:, LANES:]
            tot2 = jnp.concatenate([jnp.where(first, gcf[CHUNK - 1:CHUNK], gcf[P2 - 1:P2]),
                                    jnp.where(first, gcb[0:1], gcb[CHUNK:CHUNK + 1])], axis=1)
            egc2 = jnp.exp(gc2)
            k2 = jnp.concatenate([kp, kp], axis=1)
            kb2 = k2 * beta2
            vb2 = jnp.concatenate([vp, vp], axis=1) * beta2
            kbe2 = kb2 * egc2
            cx.append(dict(
                m=m, j=j, rows=rows, vb2=vb2, kbe2=kbe2,
                qeb=(jnp.concatenate([qp, qp], axis=1) * egc2).astype(BF16),
                kd2=k2 * jnp.exp(tot2 - gc2),
                et=jnp.exp(tot2),
                dec=jnp.exp(jnp.where(
                    incl, squeeze(gc2) - squeeze(jnp.concatenate([gcf.T, gcb.T], axis=1)), -jnp.inf)),
                stk=jnp.concatenate([kb2[:, :LANES], kb2[:, LANES:], qp], axis=0).astype(BF16),
                kpb=kp.astype(BF16)))

        g3s = [_dot_nt(c['stk'], c['kpb']) for c in cx]
        for c, g3 in zip(cx, g3s):
            lm = jnp.where(strict, squeeze(jnp.concatenate([g3[:P2], g3[P2:2 * P2]], axis=1)) * c['dec'], 0.0)
            attn = squeeze(jnp.concatenate([g3[2 * P2:], g3[2 * P2:]], axis=1)) * c['dec']
            c['attn2'] = unsqueeze(attn).astype(BF16)
            c['lbd'] = jnp.where(bd16, lm, 0.0)
            c['loff'] = lm - c['lbd']
        nn = [-c['lbd'] for c in cx]
        pw = [mm4(c['lbd'], c['lbd']) for c in cx]
        for rnd in range(3):
            prod = [mm4(a, p) for a, p in zip(nn, pw)]
            nxt = [mm4(p, p) for p in pw] if rnd < 2 else pw
            nn = [a + p + q for a, p, q in zip(nn, pw, prod)]
            pw = nxt
        mo = [c['loff'] + x for c, x in zip(cx, [mm4(a, c['loff']) for a, c in zip(nn, cx)])]
        m2 = [mm4(x, x) for x in mo]
        mn = [mm4(x, a) for x, a in zip(mo, nn)]
        xo = [a - x - y for a, x, y in zip(nn, mo, mn)]
        mx = [mm4(a, b) for a, b in zip(m2, xo)]
        toff = [a + b + q for a, b, q in zip(xo, m2, mx)]
        z4 = jnp.zeros((CHUNK, W2), BF16)
        tws = []
        for c, tf in zip(cx, toff):
            vbb, kbb = c['vb2'].astype(BF16), c['kbe2'].astype(BF16)
            blocks = []
            for g in range(4):
                rr = slice((g % 2) * CHUNK, (g % 2 + 1) * CHUNK)
                ll = slice((g // 2) * LANES, (g // 2 + 1) * LANES)
                blocks.append(jnp.concatenate([z4] * g + [vbb[rr, ll], kbb[rr, ll]] + [z4] * (3 - g), axis=1))
            tws.append(_dot(tf.astype(BF16), jnp.concatenate(blocks, axis=0)))
        for c, tw in zip(cx, tws):
            m, j, rows = c['m'], c['j'], c['rows']
            def pair_layout(off):
                piece = lambda g: tw[:, 2 * g * LANES + off:2 * g * LANES + off + LANES]
                return jnp.concatenate([jnp.concatenate([piece(0), piece(2)], axis=1),
                                        jnp.concatenate([piece(1), piece(3)], axis=1)], axis=0)

            u2 = c['vb2'] + pair_layout(0)
            w2 = (c['kbe2'] + pair_layout(LANES)).astype(BF16)
            qeb = c['qeb']
            u_scr[j, rows, :] = u2
            wq_scr[j, pl.ds(pl.multiple_of(m * 2 * P2, 2 * P2), 2 * P2), :] = jnp.concatenate(
                [w2[:CHUNK], qeb[:CHUNK], w2[CHUNK:], qeb[CHUNK:]], axis=0)
            at_scr[j, rows, :] = c['attn2']
            kdt_scr[j, 0, :, rows] = c['kd2'][:, :LANES].T.astype(BF16)
            kdt_scr[j, 1, :, rows] = c['kd2'][:, LANES:].T.astype(BF16)
            et_scr[j, pl.ds(pl.multiple_of(m * 2 * SUBLANES, 2 * SUBLANES), 2 * SUBLANES), :] = jnp.concatenate(
                [c['et'][:SUBLANES], c['et'][CHUNK:CHUNK + SUBLANES]], axis=0)
        return carry

    lax.fori_loop(0, npair // U, prep, 0)

    zc = jnp.zeros((CHUNK, LANES), F32)
    zp = jnp.zeros((P2, LANES), F32)

    def place(v, cpos):
        return jnp.concatenate([v, zc] if cpos == 0 else [zc, v], axis=0)

    def scan(m, states):
        pf = m
        pb = npair - 1 - m
        rows_f = pl.ds(pl.multiple_of(pf * P2, P2), P2)
        rows_b = pl.ds(pl.multiple_of(pb * P2, P2), P2)
        hx = []
        for j in range(hp):
            hx.append(dict(
                u_f=u_scr[j, rows_f, :LANES], u_b=u_scr[j, rows_b, LANES:],
                at_f=at_scr[j, rows_f, :LANES], at_b=at_scr[j, rows_b, LANES:],
                kdt=jnp.concatenate([kdt_scr[j, 0, :, rows_f], kdt_scr[j, 1, :, rows_b]], axis=1),
                wq_f=wq_scr[j, pl.ds(pl.multiple_of(pf * 2 * P2, 2 * P2), 2 * P2), :LANES],
                wq_b=wq_scr[j, pl.ds(pl.multiple_of(pb * 2 * P2, 2 * P2), 2 * P2), LANES:],
                et_f=et_scr[j, pl.ds(pl.multiple_of(pf * 2 * SUBLANES, 2 * SUBLANES), 2 * SUBLANES), :LANES],
                et_b=et_scr[j, pl.ds(pl.multiple_of(pb * 2 * SUBLANES, 2 * SUBLANES), 2 * SUBLANES), LANES:]))
        sts = list(states)
        for step in range(2):
            cf, cb = step, 1 - step
            rrs = [_dot(jnp.concatenate([c['wq_f'][cf * P2:(cf + 1) * P2], c['wq_b'][cb * P2:(cb + 1) * P2]], axis=1),
                        bdiag(st.astype(BF16))) for c, st in zip(hx, sts)]
            ress = []
            for c, rr in zip(hx, rrs):
                u2 = jnp.concatenate([c['u_f'][cf * CHUNK:(cf + 1) * CHUNK],
                                      c['u_b'][cb * CHUNK:(cb + 1) * CHUNK]], axis=1)
                v_new = u2 - rr[:CHUNK]
                rhs = jnp.concatenate(
                    [jnp.concatenate([place(v_new[:, :LANES], cf), zp], axis=1),
                     jnp.concatenate([zp, place(v_new[:, LANES:], cb)], axis=1)], axis=0).astype(BF16)
                lhs = jnp.concatenate(
                    [jnp.concatenate([c['at_f'][cf * CHUNK:(cf + 1) * CHUNK],
                                      c['at_b'][cb * CHUNK:(cb + 1) * CHUNK]], axis=1),
                     c['kdt']], axis=0)
                ress.append(_dot(lhs, rhs))
            for j, (c, rr, res) in enumerate(zip(hx, rrs, ress)):
                o2 = rr[CHUNK:] + res[:CHUNK]
                et2 = jnp.concatenate([c['et_f'][cf * SUBLANES:cf * SUBLANES + 1],
                                       c['et_b'][cb * SUBLANES:cb * SUBLANES + 1]], axis=1)
                sts[j] = sts[j] * et2 + res[CHUNK:]
                of_rows = pl.ds(pl.multiple_of(pf * P2 + cf * CHUNK, CHUNK), CHUNK)
                ob_rows = pl.ds(pl.multiple_of(pb * P2 + cb * CHUNK, CHUNK), CHUNK)
                oacc_scr[j, of_rows, :] = oacc_scr[j, of_rows, :] + o2[:, :LANES]
                oacc_scr[j, ob_rows, :] = oacc_scr[j, ob_rows, :] + o2[:, LANES:]
        return tuple(sts)

    s0 = jnp.zeros((dk, W2), F32)
    lax.fori_loop(0, npair, scan, (s0,) * hp)

    for j in range(hp):
        lanes_j = slice(j * LANES, (j + 1) * LANES)
        for s0 in range(0, S, seg):
            o = oacc_scr[j, s0:s0 + seg, :]
            y = o * lax.rsqrt(jnp.mean(o * o, axis=-1, keepdims=True) + EPS) * og_ref[...]
            z = z_ref[s0:s0 + seg, lanes_j].astype(F32)
            o_ref[s0:s0 + seg, lanes_j] = (y * _silu(z)).astype(o_ref.dtype)


def _gdn(proj, cw4, pack3, onorm_g, B, S, HB, base_blk):
    T = proj.shape[0]
    hp = GDN_HP if (HB % GDN_HP == 0 and base_blk % GDN_HP == 0) else 1
    wblk = hp * LANES
    npair = S // (2 * CHUNK)

    def col(k):
        off = (base_blk + k * HB) // hp
        return lambda b, h: (b, off + h)

    return pl.pallas_call(
        functools.partial(_gdn_kernel, hb=HB, hp=hp),
        grid=(B, HB // hp),
        in_specs=[pl.BlockSpec((S, wblk), col(0)),
                  pl.BlockSpec((S, wblk), col(1)),
                  pl.BlockSpec((S, wblk), col(2)),
                  pl.BlockSpec((S, wblk), col(3)),
                  pl.BlockSpec((hp, 3, cw4.shape[2], LANES), lambda b, h: (h, 0, 0, 0)),
                  pl.BlockSpec((None, S, LANES), lambda b, h: (b, 0, 0)),
                  pl.BlockSpec((1, LANES), lambda b, h: (0, 0))],
        out_specs=pl.BlockSpec((S, wblk), lambda b, h: (b, h)),
        out_shape=jax.ShapeDtypeStruct((T, HB * LANES), BF16),
        scratch_shapes=[pltpu.VMEM((hp, S, LANES), F32)] * 3
        + [pltpu.VMEM((2, S + 2 * SUBLANES, LANES), F32),
           pltpu.VMEM((S, 2 * LANES), BF16),
           pltpu.VMEM((hp, 2 * LANES, 4 * LANES), BF16),
           pltpu.VMEM((hp, S, 2 * LANES), F32),
           pltpu.VMEM((hp, 2 * S, 2 * LANES), BF16),
           pltpu.VMEM((hp, S, 2 * LANES), BF16),
           pltpu.VMEM((hp, 2, LANES, S), BF16),
           pltpu.VMEM((hp, npair * 2 * SUBLANES, 2 * LANES), F32),
           pltpu.VMEM((hp, S, LANES), F32)],
        compiler_params=_cparams(("arbitrary", "arbitrary"), 56),
    )(proj, proj, proj, proj, cw4, pack3, onorm_g)


def _outproj_kernel(oa_ref, ob_ref, x_ref, gt_ref, w_ref, g_ref, sc_ref, sh_ref, wr_ref, br_ref,
                    x1_ref, hp_ref, lg_ref, h_scr):
    da = oa_ref.shape[1]
    y = _dot(oa_ref[...], w_ref[:da, :]) + _dot(ob_ref[...], w_ref[da:, :])
    x1 = x_ref[...] + gt_ref[...] * y
    x1_ref[...] = x1
    hn = x1 * lax.rsqrt(jnp.mean(x1 * x1, axis=-1, keepdims=True) + EPS) * g_ref[...]
    h = hn * (1.0 + sc_ref[...]) + sh_ref[...]
    hp_ref[...] = _pack_halves(h)
    h_scr[...] = h

    @pl.when(pl.program_id(0) < pl.num_programs(0))
    def _():
        lg_ref[...] = _dot_hilo(h_scr[...], wr_ref[...]) + br_ref[...]


def _outproj(oa, ob, x2, gt1, w_out_b, g2, sc2, sh2, wr, br, S):
    T, D = x2.shape
    tm = ROW_TILE
    per_b = S // tm
    bmap = lambda i: (i // per_b, 0, 0)
    return pl.pallas_call(
        _outproj_kernel,
        grid=(T // tm,),
        in_specs=[pl.BlockSpec((tm, oa.shape[1]), lambda i: (i, 0)),
                  pl.BlockSpec((tm, ob.shape[1]), lambda i: (i, 0)),
                  pl.BlockSpec((tm, D), lambda i: (i, 0)),
                  pl.BlockSpec((None, 1, D), bmap),
                  pl.BlockSpec((D, D), lambda i: (0, 0)),
                  pl.BlockSpec((1, D), lambda i: (0, 0)),
                  pl.BlockSpec((None, 1, D), bmap),
                  pl.BlockSpec((None, 1, D), bmap),
                  pl.BlockSpec((D, 2 * LANES), lambda i: (0, 0)),
                  pl.BlockSpec((1, LANES), lambda i: (0, 0))],
        out_specs=[pl.BlockSpec((tm, D), lambda i: (i, 0)),
                   pl.BlockSpec((tm, D // 2), lambda i: (i, 0)),
                   pl.BlockSpec((tm, LANES), lambda i: (i, 0))],
        out_shape=[jax.ShapeDtypeStruct((T, D), F32), jax.ShapeDtypeStruct((T, D // 2), U32),
                   jax.ShapeDtypeStruct((T, LANES), F32)],
        scratch_shapes=[pltpu.VMEM((tm, D), F32)],
        compiler_params=_cparams(("arbitrary",), 48),
    )(oa, ob, x2, gt1, w_out_b, g2, sc2, sh2, wr, br)


def _route_kernel(lg_ref, o_ref, info_ref, run_scr, *, ne, ng):
    ph = pl.program_id(0)
    i = pl.program_id(1)

    @pl.when((ph == 0) & (i == 0))
    def _():
        run_scr[...] = jnp.zeros_like(run_scr)

    @pl.when((ph == 1) & (i == 0))
    def _():
        cnt = run_scr[...]
        padded = jnp.ceil(cnt * (1.0 / MOE_RB)) * MOE_RB
        k = lax.broadcasted_iota(I32, (LANES, LANES), 0)
        e = lax.broadcasted_iota(I32, (LANES, LANES), 1)
        start = _dot(padded, jnp.where(k < e, 1.0, 0.0), precision=HIGHEST)
        rowi = lax.broadcasted_iota(I32, cnt.shape, 0)
        info_ref[...] = jnp.where(rowi == 0, cnt, jnp.where(rowi == 1, padded, start))
        run_scr[...] = start

    lg = lg_ref[...]
    tm = lg.shape[0]
    epg = ne // ng
    lane_i = lax.broadcasted_iota(I32, lg.shape, 1)
    lane = lane_i.astype(F32)
    big = float(2 * LANES)
    is_g = (lane_i >= ne) & (lane_i < ne + ng)
    gl = jnp.where(is_g, lg, -jnp.inf)
    gmax = jnp.max(gl, axis=-1, keepdims=True)
    gidx = jnp.min(jnp.where(gl == gmax, lane, big), axis=-1, keepdims=True) - ne
    psel = 1.0 / jnp.sum(jnp.where(is_g, jnp.exp(gl - gmax), 0.0), axis=-1, keepdims=True)
    in_grp = (lane_i // epg).astype(F32) == gidx
    el = jnp.where(in_grp & (lane_i < ne), lg, -jnp.inf)
    m1 = jnp.max(el, axis=-1, keepdims=True)
    i1 = jnp.min(jnp.where(el == m1, lane, big), axis=-1, keepdims=True)
    el2 = jnp.where(lane == i1, -jnp.inf, el)
    m2 = jnp.max(el2, axis=-1, keepdims=True)
    i2 = jnp.min(jnp.where(el2 == m2, lane, big), axis=-1, keepdims=True)
    e21 = jnp.exp(m2 - m1)
    g1 = psel / (1.0 + e21)
    g2 = psel * e21 / (1.0 + e21)
    o1 = jnp.where(lane == i1, 1.0, 0.0)
    o2 = jnp.where(lane == i2, 1.0, 0.0)
    cnt = o1 + o2
    r = lax.broadcasted_iota(I32, (tm, tm), 0)
    c = lax.broadcasted_iota(I32, (tm, tm), 1)
    before = _dot(jnp.where(c < r, 1.0, 0.0).astype(BF16), cnt.astype(BF16)) + run_scr[0:1, :]
    d1 = jnp.sum(o1 * before, axis=-1, keepdims=True)
    d2 = jnp.sum(o2 * before, axis=-1, keepdims=True)
    run_scr[...] = run_scr[...] + jnp.sum(cnt, axis=0, keepdims=True)

    @pl.when(ph == 1)
    def _():
        out = jnp.zeros(lg.shape, F32)
        for j, val in enumerate((i1, i2, g1, g2, d1, d2)):
            out = jnp.where(lane_i == j, val, out)
        o_ref[...] = out


def _route(logits, ne, ng):
    T = logits.shape[0]
    tm = _pick_tile(T, (1024, 512, 256))
    return pl.pallas_call(
        functools.partial(_route_kernel, ne=ne, ng=ng),
        grid=(2, T // tm),
        in_specs=[pl.BlockSpec((tm, LANES), lambda p, i: (i, 0))],
        out_specs=[pl.BlockSpec((tm, LANES), lambda p, i: (i * p, 0)),
                   pl.BlockSpec((SUBLANES, LANES), lambda p, i: (0, 0))],
        out_shape=[jax.ShapeDtypeStruct((T, LANES), F32), jax.ShapeDtypeStruct((SUBLANES, LANES), F32)],
        scratch_shapes=[pltpu.VMEM((SUBLANES, LANES), F32)],
        compiler_params=_cparams(("arbitrary", "arbitrary"), 16),
    )(logits)


def _dispatch_kernel(dest_ref, h_hbm, xs_in, xs_hbm, sem):
    del xs_in
    tm = ROW_TILE
    base = pl.program_id(0) * tm

    def copy(tok, a):
        return pltpu.make_async_copy(h_hbm.at[pl.ds(tok, 1)], xs_hbm.at[pl.ds(dest_ref[a], 1)], sem)

    def issue(t, carry):
        for k in range(TOP_K):
            copy(base + t, (base + t) * TOP_K + k).start()
        return carry

    lax.fori_loop(0, tm, issue, 0, unroll=DMA_UNROLL)

    for k in range(TOP_K):
        pltpu.make_async_copy(h_hbm.at[pl.ds(0, tm)], xs_hbm.at[pl.ds(0, tm)], sem).wait()


def _dispatch(dest, hpk, p_rows):
    T, dh = hpk.shape
    return pl.pallas_call(
        _dispatch_kernel,
        grid_spec=pltpu.PrefetchScalarGridSpec(
            num_scalar_prefetch=1,
            grid=(T // ROW_TILE,),
            in_specs=[pl.BlockSpec(memory_space=pl.ANY), pl.BlockSpec(memory_space=pl.ANY)],
            out_specs=pl.BlockSpec(memory_space=pl.ANY),
            scratch_shapes=[pltpu.SemaphoreType.DMA(())]),
        out_shape=jax.ShapeDtypeStruct((p_rows, dh), U32),
        input_output_aliases={2: 0},
        compiler_params=_cparams(("arbitrary",), 16),
    )(dest, hpk, jnp.zeros((p_rows, dh), U32))


def _moe_kernel(we_ref, ws_ref, wn_ref, wt_ref, xs_hbm, w1_hbm, w3_hbm, w2_hbm, y_hbm,
                xbuf, xlo, xhi, yacc, ypk, w1buf, w3buf, w2buf, sem_in, sem_out, sem_w):
    w = pl.program_id(0)
    c = pl.program_id(1)
    nw = pl.num_programs(0)
    nc = pl.num_programs(1)
    nrows = wn_ref[w]
    start = ws_ref[w]
    nblk = nrows // MOE_RB
    dh = xbuf.shape[1]
    cw = w1buf.shape[-1]

    g = w * nc + c

    def w_copies(step, do):
        item = step // nc
        chunk = step % nc

        @pl.when((item < nw) & (wn_ref[jnp.minimum(item, nw - 1)] > 0))
        def _():
            e = we_ref[item]
            cols = pl.ds(pl.multiple_of(chunk * cw, cw), cw)
            slot = step % MOE_WSLOTS
            do(pltpu.make_async_copy(w1_hbm.at[e, :, cols], w1buf.at[slot], sem_w.at[slot]))
            do(pltpu.make_async_copy(w3_hbm.at[e, :, cols], w3buf.at[slot], sem_w.at[slot]))
            do(pltpu.make_async_copy(w2_hbm.at[e, cols, :], w2buf.at[slot], sem_w.at[slot]))

    @pl.when(g == 0)
    def _():
        for ahead in range(MOE_WSLOTS - 1):
            w_copies(ahead, lambda cp: cp.start())

    w_copies(g + MOE_WSLOTS - 1, lambda cp: cp.start())
    w_copies(g, lambda cp: cp.wait())
    wslot = g % MOE_WSLOTS
    w1_ref, w3_ref, w2_ref = w1buf.at[wslot], w3buf.at[wslot], w2buf.at[wslot]

    def blk_rows(rb):
        return pl.ds(pl.multiple_of(rb * MOE_RB, MOE_RB), MOE_RB)

    def hbm_rows(item_start, rb):
        return pl.ds(pl.multiple_of(item_start + rb * MOE_RB, MOE_RB), MOE_RB)

    def in_copy(item_start, rb):
        return pltpu.make_async_copy(xs_hbm.at[hbm_rows(item_start, rb)], xbuf.at[blk_rows(rb)], sem_in)

    def out_copy(item_start, rb):
        return pltpu.make_async_copy(ypk.at[blk_rows(rb)], y_hbm.at[hbm_rows(item_start, rb)], sem_out)

    def each_block(n, fn):
        def body(rb, carry):
            fn(rb)
            return carry
        lax.fori_loop(0, n, body, 0)

    @pl.when(c == 0)
    def _load():
        @pl.when(w == 0)
        def _():
            each_block(nblk, lambda rb: in_copy(start, rb).start())

        each_block(nblk, lambda rb: in_copy(start, rb).wait())

        def unpack(rb):
            lo, hi = _unpack_halves(xbuf[blk_rows(rb), :])
            xlo[blk_rows(rb), :] = lo.astype(BF16)
            xhi[blk_rows(rb), :] = hi.astype(BF16)
            yacc[blk_rows(rb), :] = jnp.zeros((MOE_RB, 2 * dh), F32)

        each_block(nblk, unpack)

        @pl.when(w + 1 < nw)
        def _():
            nxt = ws_ref[w + 1]
            each_block(wn_ref[w + 1] // MOE_RB, lambda rb: in_copy(nxt, rb).start())

    @pl.when(nrows > 0)
    def _compute():
        def rows_block(row0, nr):
            rows = pl.ds(row0, nr)
            xl = xlo[rows, :]
            xh = xhi[rows, :]
            h1 = _dot(xl, w1_ref[:dh, :].astype(BF16)) + _dot(xh, w1_ref[dh:, :].astype(BF16))
            h3 = _dot(xl, w3_ref[:dh, :].astype(BF16)) + _dot(xh, w3_ref[dh:, :].astype(BF16))
            hid = (_silu(h1) * h3).astype(BF16)
            yacc[rows, :] = yacc[rows, :] + _dot(hid, w2_ref[...].astype(BF16))

        ntall = nrows // MOE_TALL
        each_block(ntall, lambda i: rows_block(pl.multiple_of(i * MOE_TALL, MOE_TALL), MOE_TALL))

        rem = nrows - ntall * MOE_TALL
        for nr in range(MOE_RB, MOE_TALL, MOE_RB):
            @pl.when(rem == nr)
            def _(nr=nr):
                rows_block(pl.multiple_of(ntall * MOE_TALL, MOE_TALL), nr)

    @pl.when(c == nc - 1)
    def _store():
        @pl.when(w > 0)
        def _():
            prev = ws_ref[w - 1]
            each_block(wn_ref[w - 1] // MOE_RB, lambda rb: out_copy(prev, rb).wait())

        def pack(rb):
            ypk[blk_rows(rb), :] = _pack_halves(yacc[blk_rows(rb), :])

        each_block(nblk, pack)
        each_block(nblk, lambda rb: out_copy(start, rb).start())

        @pl.when(w == nw - 1)
        def _():
            each_block(nblk, lambda rb: out_copy(start, rb).wait())

    @pl.when((w == nw - 1) & (c == nc - 1))
    def _zero_tail():
        ypk[blk_rows(0), :] = jnp.zeros((MOE_RB, dh), U32)
        first = wt_ref[0] // MOE_RB

        def tail_copy(b):
            return pltpu.make_async_copy(ypk.at[blk_rows(0)],
                                         y_hbm.at[pl.ds(pl.multiple_of(b * MOE_RB, MOE_RB), MOE_RB)], sem_out)

        def start(b, carry):
            tail_copy(b).start()
            return carry

        def wait(b, carry):
            tail_copy(b).wait()
            return carry

        lax.fori_loop(first, y_hbm.shape[0] // MOE_RB, start, 0)
        lax.fori_loop(first, y_hbm.shape[0] // MOE_RB, wait, 0)


def _moe(we, ws, wn, wt, xs, w1, w3, w2):
    P, dh = xs.shape
    D = 2 * dh
    NE, _, DE = w1.shape
    cw = min(MOE_CW, DE)
    nc = DE // cw
    nw = we.shape[0]
    return pl.pallas_call(
        _moe_kernel,
        grid_spec=pltpu.PrefetchScalarGridSpec(
            num_scalar_prefetch=4,
            grid=(nw, nc),
            in_specs=[pl.BlockSpec(memory_space=pl.ANY)] * 4,
            out_specs=pl.BlockSpec(memory_space=pl.ANY),
            scratch_shapes=[pltpu.VMEM((MOE_RMAX, dh), U32),
                            pltpu.VMEM((MOE_RMAX, dh), BF16),
                            pltpu.VMEM((MOE_RMAX, dh), BF16),
                            pltpu.VMEM((MOE_RMAX, D), F32),
                            pltpu.VMEM((MOE_RMAX, dh), U32),
                            pltpu.VMEM((MOE_WSLOTS, D, cw), F32),
                            pltpu.VMEM((MOE_WSLOTS, D, cw), F32),
                            pltpu.VMEM((MOE_WSLOTS, cw, D), F32),
                            pltpu.SemaphoreType.DMA(()),
                            pltpu.SemaphoreType.DMA(()),
                            pltpu.SemaphoreType.DMA((MOE_WSLOTS,))]),
        out_shape=jax.ShapeDtypeStruct((P, dh), U32),
        compiler_params=_cparams(("arbitrary", "arbitrary"), 56),
    )(we, ws, wn, wt, xs, w1, w3, w2)


def _moe_schedule(info, NE, p_rows):
    padded = info[1, :NE].astype(I32)
    start_pad = info[2, :NE].astype(I32)
    items = (padded + MOE_RMAX - 1) // MOE_RMAX
    cum_items = jnp.cumsum(items)
    n_items = cum_items[-1]
    nw = (p_rows + NE * (MOE_RMAX - MOE_RB)) // MOE_RMAX
    wi = jnp.arange(nw, dtype=I32)
    valid = wi < n_items
    wi_c = jnp.minimum(wi, jnp.maximum(n_items - 1, 0))
    we = jnp.minimum(jnp.searchsorted(cum_items, wi_c, side='right'), NE - 1).astype(I32)
    local = wi_c - (cum_items[we] - items[we])
    ws = (start_pad[we] + local * MOE_RMAX).astype(I32)
    wn = jnp.where(valid, jnp.clip(padded[we] - local * MOE_RMAX, 0, MOE_RMAX), 0).astype(I32)
    wt = (start_pad[NE - 1] + padded[NE - 1]).reshape(1)
    return we, ws, wn, wt


def _combine_kernel(dest_ref, x1_ref, rt_ref, gt_ref, y_hbm, o_ref, ybuf, sems):
    tm = ROW_TILE
    i = pl.program_id(0)
    n = pl.num_programs(0)
    dh = ybuf.shape[-1]

    def copy(tile, slot, t, k):
        a = (tile * tm + t) * TOP_K + k
        return pltpu.make_async_copy(y_hbm.at[pl.ds(dest_ref[a], 1)], ybuf.at[slot, k, pl.ds(t, 1)], sems.at[slot])

    def issue_tile(tile, slot):
        def body(t, carry):
            for k in range(TOP_K):
                copy(tile, slot, t, k).start()
            return carry
        lax.fori_loop(0, tm, body, 0, unroll=DMA_UNROLL)

    @pl.when(i == 0)
    def _():
        issue_tile(0, 0)

    @pl.when(i + 1 < n)
    def _():
        issue_tile(i + 1, (i + 1) % 2)

    slot = i % 2

    for k in range(TOP_K):
        pltpu.make_async_copy(y_hbm.at[pl.ds(0, tm)], ybuf.at[slot, k], sems.at[slot]).wait()

    rt = rt_ref[...]
    g1 = rt[:, 2:3]
    g2 = rt[:, 3:4]
    lo1, hi1 = _unpack_halves(ybuf[slot, 0])
    lo2, hi2 = _unpack_halves(ybuf[slot, 1])
    o_ref[:, :dh] = x1_ref[:, :dh] + gt_ref[:, :dh] * (g1 * lo1 + g2 * lo2)
    o_ref[:, dh:] = x1_ref[:, dh:] + gt_ref[:, dh:] * (g1 * hi1 + g2 * hi2)


def _combine(dest, x1, route, gt2, ypk, S):
    T, D = x1.shape
    tm = ROW_TILE
    per_b = S // tm
    return pl.pallas_call(
        _combine_kernel,
        grid_spec=pltpu.PrefetchScalarGridSpec(
            num_scalar_prefetch=1,
            grid=(T // tm,),
            in_specs=[pl.BlockSpec((tm, D), lambda i, d: (i, 0)),
                      pl.BlockSpec((tm, LANES), lambda i, d: (i, 0)),
                      pl.BlockSpec((None, 1, D), lambda i, d: (i // per_b, 0, 0)),
                      pl.BlockSpec(memory_space=pl.ANY)],
            out_specs=pl.BlockSpec((tm, D), lambda i, d: (i, 0)),
            scratch_shapes=[pltpu.VMEM((2, TOP_K, tm, D // 2), U32),
                            pltpu.SemaphoreType.DMA((2,))]),
        out_shape=jax.ShapeDtypeStruct((T, D), F32),
        compiler_params=_cparams(("arbitrary",), 32),
    )(dest, x1, route, gt2, ypk)


def kernel(x, c, w_ada, b_ada, norm1_g, norm2_g, w_in, qn_g, kn_g, rel_bias, conv_w, A_log, dt_bias,
           onorm_g, w_out, w_rg, b_rg, w_re, b_re, w1, w3, w2):
    B, S, D = x.shape
    depth = w_ada.shape[0]
    HA, hda = rel_bias.shape[1], qn_g.shape[-1]
    HB, hdb = A_log.shape[-1], onorm_g.shape[-1]
    DA, DB = HA * hda, HB * hdb
    NG, NE = w_rg.shape[-1], w_re.shape[-1]
    T = B * S
    assert hdb == LANES and 2 * hda == LANES and DA + DB == D and 16 * HB <= LANES
    assert NE + NG <= LANES and conv_w.shape[1] == 5 and S % ROW_TILE == 0
    n_main = 3 * DA + 4 * DB
    p_rows = TOP_K * T + NE * MOE_RB

    bias_prof = _attn_bias_profiles(rel_bias, S)
    w_in_b = w_in.astype(BF16)
    x2 = x.reshape(T, D)
    for l in range(depth):
        mod = _ada(c, w_ada[l], b_ada[l]).reshape(B, 6, 1, D)
        sh1, sc1, gt1, sh2, sc2, gt2 = (mod[:, i] for i in range(6))

        w_gate = jnp.tile(w_in[l][:, n_main:], (1, 4))
        w_gate = _hilo_weights(jnp.pad(w_gate, ((0, 0), (0, LANES - w_gate.shape[1]))))
        proj, gat = _inproj(x2, norm1_g[l].reshape(1, D), sc1, sh1, w_in_b[l], n_main, w_gate, S)

        def gate_row(p):
            grp = jnp.concatenate([p.reshape(-1), jnp.zeros((2 * HB,), F32)])
            return jnp.pad(jnp.tile(grp, 4), (0, LANES - 16 * HB)).reshape(1, LANES)

        pack = _gating(gat, gate_row(A_log[l]), gate_row(dt_bias[l]), HB)

        oa = _attention(proj, jnp.tile(qn_g[l], 2).reshape(1, LANES), jnp.tile(kn_g[l], 2).reshape(1, LANES),
                        bias_prof, B, S, HA, hda)
        cw4 = jnp.transpose(conv_w[l].reshape(conv_w.shape[1], 3, HB, hdb), (2, 1, 0, 3))
        ob = _gdn(proj, cw4, pack.reshape(B, S, LANES), onorm_g[l].reshape(1, LANES), B, S, HB, 3 * DA // LANES)

        wr = _hilo_weights(jnp.pad(jnp.concatenate([w_re[l], w_rg[l]], axis=1), ((0, 0), (0, LANES - NE - NG))))
        br = jnp.pad(jnp.concatenate([b_re[l], b_rg[l]]), (0, LANES - NE - NG)).reshape(1, LANES)
        x1, hpk, logits = _outproj(oa, ob, x2, gt1, w_out[l].astype(BF16), norm2_g[l].reshape(1, D),
                                   sc2, sh2, wr, br, S)
        route, info = _route(logits, NE, NG)
        dest = route[:, 4:4 + TOP_K].astype(I32).reshape(TOP_K * T)
        we, ws, wn, wt = _moe_schedule(info, NE, p_rows)
        xs = _dispatch(dest, hpk, p_rows)
        ypk = _moe(we, ws, wn, wt, xs, w1[l], w3[l], w2[l])
        x2 = _combine(dest, x1, route, gt2, ypk, S)
    return x2.reshape(B, S, D)
```

```python
import functools

import numpy as np
import jax
import jax.numpy as jnp
from jax import lax
from jax.experimental import pallas as pl
from jax.experimental.pallas import tpu as pltpu

F32 = jnp.float32
BF16 = jnp.bfloat16
I32 = jnp.int32
U32 = jnp.uint32
HIGHEST = lax.Precision.HIGHEST

EPS = 1e-6
NEG = -1e30
DILATED_BRANCHES = ((128, 1), (512, 4), (2048, 16))
REL_MAX_DIST = 1024
CHUNK = 64
TOP_K = 2

LANES = 128
SUBLANES = 8
MIB = 1 << 20

ATT_QB = 128
ATT_KW = 256
ATT_UNROLL = 8
ATT_PROF_W = 512
GDN_HP = 2
GDN_PREP_UNROLL = 4
GDN_SEG = 512
MOE_RB = 128
MOE_TALL = 512
MOE_RMAX = 512
MOE_WSLOTS = 3
MOE_CW = 512
ROW_TILE = 256
DMA_UNROLL = 8


def _cparams(sem, vmem_mib):
    return pltpu.CompilerParams(dimension_semantics=sem, vmem_limit_bytes=vmem_mib * MIB)


def _dot(a, b, **kw):
    return jnp.dot(a, b, preferred_element_type=F32, **kw)


def _dot_nt(a, b):
    return lax.dot_general(a, b, (((1,), (1,)), ((), ())), preferred_element_type=F32)


def _pick_tile(n, prefs):
    for t in prefs:
        if n % t == 0:
            return t
    return n


def _pack_halves(x):
    half = x.shape[1] // 2
    lo = lax.bitcast_convert_type(x[:, :half].astype(BF16).astype(F32), U32)
    hi = lax.bitcast_convert_type(x[:, half:].astype(BF16).astype(F32), U32)
    return lax.shift_right_logical(lo, jnp.uint32(16)) | (hi & jnp.uint32(0xFFFF0000))


def _silu(x):
    h = 0.5 * x
    return h + h * jnp.tanh(h)


def _hilo_weights(w):
    hi = w.astype(BF16)
    lo = (w - hi.astype(F32)).astype(BF16)
    return jnp.concatenate([hi, lo], axis=1)


def _dot_hilo(x, w2):
    m, n = x.shape[0], w2.shape[1] // 2
    hi = x.astype(BF16)
    lo = (x - hi.astype(F32)).astype(BF16)
    r = _dot(jnp.concatenate([hi, lo], axis=0), w2)
    return r[:m, :n] + (r[:m, n:] + r[m:, :n])


def _unpack_halves(p):
    lo = lax.bitcast_convert_type(lax.shift_left(p, jnp.uint32(16)), F32)
    hi = lax.bitcast_convert_type(p & jnp.uint32(0xFFFF0000), F32)
    return lo, hi


def _ada_kernel(c_ref, w_ref, b_ref, o_ref):
    c = c_ref[...]
    s = _silu(c).astype(BF16)
    o_ref[...] = _dot(s, w_ref[...].astype(BF16)) + b_ref[...]


def _ada(c, w_ada, b_ada):
    B, D = c.shape
    N = w_ada.shape[1]
    tn = _pick_tile(N, (1024, 512, 256, 128))
    return pl.pallas_call(
        _ada_kernel,
        grid=(N // tn,),
        in_specs=[pl.BlockSpec((B, D), lambda j: (0, 0)),
                  pl.BlockSpec((D, tn), lambda j: (0, j)),
                  pl.BlockSpec((1, tn), lambda j: (0, j))],
        out_specs=pl.BlockSpec((B, tn), lambda j: (0, j)),
        out_shape=jax.ShapeDtypeStruct((B, N), F32),
        compiler_params=_cparams(("arbitrary",), 40),
    )(c, w_ada, b_ada.reshape(1, N))


def _inproj_kernel(x_ref, g_ref, sc_ref, sh_ref, w_ref, wg_ref, o_ref, og_ref, h_scr):
    @pl.when(pl.program_id(1) == 0)
    def _():
        x = x_ref[...]
        y = x * lax.rsqrt(jnp.mean(x * x, axis=-1, keepdims=True) + EPS) * g_ref[...]
        h = y * (1.0 + sc_ref[...]) + sh_ref[...]
        h_scr[...] = h.astype(BF16)
        og_ref[...] = _dot_hilo(h, wg_ref[...])

    o_ref[...] = _dot_nt(h_scr[...], w_ref[...].astype(BF16)).astype(o_ref.dtype)


def _inproj(x2, g, sc, sh, w_all, n_main, w_gate, S):
    T, D = x2.shape
    NM = n_main
    tm = _pick_tile(S, (1024, 512, 256, 128))
    tn = _pick_tile(NM, (1024, 512, 256, 128))
    per_b = S // tm
    return pl.pallas_call(
        _inproj_kernel,
        grid=(T // tm, NM // tn),
        in_specs=[pl.BlockSpec((tm, D), lambda i, j: (i, 0)),
                  pl.BlockSpec((1, D), lambda i, j: (0, 0)),
                  pl.BlockSpec((None, 1, D), lambda i, j: (i // per_b, 0, 0)),
                  pl.BlockSpec((None, 1, D), lambda i, j: (i // per_b, 0, 0)),
                  pl.BlockSpec((tn, D), lambda i, j: (j, 0)),
                  pl.BlockSpec((D, 2 * LANES), lambda i, j: (0, 0))],
        out_specs=[pl.BlockSpec((tm, tn), lambda i, j: (i, j)),
                   pl.BlockSpec((tm, LANES), lambda i, j: (i, 0))],
        out_shape=[jax.ShapeDtypeStruct((T, NM), BF16), jax.ShapeDtypeStruct((T, LANES), F32)],
        scratch_shapes=[pltpu.VMEM((tm, D), BF16)],
        compiler_params=_cparams(("arbitrary", "arbitrary"), 56),
    )(x2, g, sc, sh, w_all, w_gate)


def _gating_kernel(gat_ref, a_ref, dt_ref, o_ref, *, hb):
    gat = gat_ref[...]
    tm = gat.shape[0]
    gw = 4 * hb
    g = -jnp.exp(a_ref[...]) * jax.nn.softplus(gat + dt_ref[...])
    beta = jax.nn.sigmoid(gat)
    r = lax.broadcasted_iota(I32, (tm, tm), 0)
    c = lax.broadcasted_iota(I32, (tm, tm), 1)
    same = (r // CHUNK) == (c // CHUNK)
    pre = _dot(jnp.where(same & (c <= r), 1.0, 0.0), g, precision=HIGHEST)
    suf = _dot(jnp.where(same & (c >= r), 1.0, 0.0), g, precision=HIGHEST)
    lane = lax.broadcasted_iota(I32, gat.shape, 1)
    o_ref[...] = jnp.where(lane // gw == 0, jnp.where(lane % gw < hb, pre, suf), beta)


def _gating(gat, a_row, dt_row, hb):
    T = gat.shape[0]
    tm = ROW_TILE
    return pl.pallas_call(
        functools.partial(_gating_kernel, hb=hb),
        grid=(T // tm,),
        in_specs=[pl.BlockSpec((tm, LANES), lambda i: (i, 0)),
                  pl.BlockSpec((1, LANES), lambda i: (0, 0)),
                  pl.BlockSpec((1, LANES), lambda i: (0, 0))],
        out_specs=pl.BlockSpec((tm, LANES), lambda i: (i, 0)),
        out_shape=jax.ShapeDtypeStruct((T, LANES), F32),
        compiler_params=_cparams(("arbitrary",), 16),
    )(gat, a_row, dt_row)


def _t5_bucket(rel, n_buckets):
    half = n_buckets // 2
    max_exact = half // 2
    n = np.abs(rel)
    large = max_exact + (np.log(np.maximum(n, 1) / max_exact) / np.log(REL_MAX_DIST / max_exact)
                         * (half - max_exact)).astype(np.int32)
    large = np.minimum(large, half - 1)
    return (np.where(rel > 0, half, 0) + np.where(n < max_exact, n, large)).astype(np.int32)


def _attn_plan(S):
    plan, base = [], 0
    for window, dil in DILATED_BRANCHES:
        n = window // (2 * dil)
        L = S // dil
        assert L % ATT_QB == 0 and n * 2 == ATT_QB
        kw = min(ATT_KW, L)
        nbq = L // ATT_QB
        nvar = 1 if nbq == 1 else 3
        plan.append((dil, L, nbq, kw, base, nvar, n))
        base += nvar
    return tuple(plan), base


def _attn_bias_profiles(rel_bias, S):
    plan, nvar_total = _attn_plan(S)
    nbuckets, H = rel_bias.shape
    u = np.arange(ATT_PROF_W) - ATT_QB
    onehots, bands = [], []
    for dil, L, nbq, kw, base, nvar, n in plan:
        offs = [0] if nvar == 1 else [0, -n, -(kw - ATT_QB)]
        for off in offs:
            rel = off + u
            onehots.append(np.eye(nbuckets, dtype=np.float32)[_t5_bucket(rel * dil, nbuckets)])
            bands.append(np.abs(rel) <= n)
    onehot = jnp.asarray(np.stack(onehots))
    band = jnp.asarray(np.stack(bands))
    prof = jnp.einsum('vwn,nh->hvw', onehot, rel_bias.astype(F32), precision=HIGHEST)
    prof = jnp.where(band[None], prof, NEG)
    prof = prof.reshape(H // 2, 2, nvar_total, ATT_PROF_W)
    return jnp.transpose(prof, (0, 2, 1, 3)).reshape(H // 2, 2 * nvar_total, ATT_PROF_W)


def _attn_kernel(q_ref, k_ref, v_ref, qg_ref, kg_ref, prof_ref, o_ref,
                 qn_scr, kn_scr, v_scr, ob_scr, mb_scr, db_scr, bias_ref, *, plan, hd):
    S = q_ref.shape[0]
    lane = lax.broadcasted_iota(I32, (1, LANES), 1)
    left = lane < hd

    @pl.when(pl.program_id(1) == 0)
    def _():
        for row in range(prof_ref.shape[0]):
            rep = jnp.broadcast_to(prof_ref[row:row + 1, :], (ATT_QB, ATT_PROF_W))
            skew = pltpu.roll(rep, 0, 1, stride=1, stride_axis=0)
            bias_ref[row // 2, row % 2] = skew[:, ATT_QB:ATT_QB + ATT_KW]

    def headnorm(x, g):
        x2 = x * x
        s_all = jnp.sum(x2, axis=-1, keepdims=True)
        s_left = jnp.sum(jnp.where(left, x2, 0.0), axis=-1, keepdims=True)
        ms = jnp.where(left, s_left, s_all - s_left) * (1.0 / hd)
        return x * lax.rsqrt(ms + EPS) * g

    qn_scr[...] = headnorm(q_ref[...].astype(F32), qg_ref[...]) * (hd ** -0.5)
    kn_scr[...] = headnorm(k_ref[...].astype(F32), kg_ref[...])
    v_scr[...] = v_ref[...].astype(F32)

    for bi, (dil, L, nbq, kw, base, nvar, n) in enumerate(plan):
        ones = jnp.ones((kw, LANES), BF16)

        def body(t, carry, dil=dil, L=L, nbq=nbq, kw=kw, base=base, nvar=nvar, n=n, bi=bi, ones=ones):
            blocks = []
            for uu in range(ATT_UNROLL):
                idx = t * ATT_UNROLL + uu
                r = idx // nbq
                i = idx % nbq
                q0 = i * ATT_QB
                k0 = jnp.clip(q0 - n, 0, L - kw)
                var = base if nvar == 1 else base + jnp.where(i > 0, 1, 0) + jnp.where(i == nbq - 1, 1, 0)
                if dil == 1:
                    qrows = pl.ds(pl.multiple_of(q0, ATT_QB), ATT_QB)
                    krows = pl.ds(pl.multiple_of(k0, CHUNK), kw)
                else:
                    qrows = pl.ds(r + q0 * dil, ATT_QB, stride=dil)
                    krows = pl.ds(r + k0 * dil, kw, stride=dil)
                qb = qn_scr[qrows, :]
                q2 = jnp.concatenate([jnp.where(left, qb, 0.0), jnp.where(left, 0.0, qb)], axis=0).astype(BF16)
                blocks.append((qrows, krows, var, q2))
            def score(group):
                return [_dot_nt(q2, kn_scr[krows, :].astype(BF16)) for qrows, krows, var, q2 in group]

            def softmax(group, scores):
                probs, maxes = [], []
                for (qrows, krows, var, q2), s in zip(group, scores):
                    s = s + jnp.concatenate([bias_ref[var, 0][:, :kw], bias_ref[var, 1][:, :kw]], axis=0)
                    m = jnp.max(s, axis=-1, keepdims=True)
                    probs.append(jnp.exp(s - m).astype(BF16))
                    maxes.append(m)
                return probs, maxes

            def values(group, probs):
                return [_dot(p, jnp.concatenate([v_scr[krows, :].astype(BF16), ones], axis=1))
                        for (qrows, krows, var, q2), p in zip(group, probs)]

            def finish(group, maxes, outs):
                for (qrows, krows, var, q2), m, od in zip(group, maxes, outs):
                    mb = jnp.broadcast_to(m, (2 * ATT_QB, LANES))
                    ob_scr[bi, qrows, :] = jnp.where(left, od[:ATT_QB, :LANES], od[ATT_QB:, :LANES])
                    mb_scr[bi, qrows, :] = jnp.where(left, mb[:ATT_QB], mb[ATT_QB:])
                    db_scr[bi, qrows, :] = jnp.where(left, od[:ATT_QB, LANES:], od[ATT_QB:, LANES:])

            ga, gb = blocks[:ATT_UNROLL // 2], blocks[ATT_UNROLL // 2:]
            sa = score(ga)
            sb = score(gb)
            pa, ma = softmax(ga, sa)
            oa = values(ga, pa)
            pb, mbx = softmax(gb, sb)
            ob = values(gb, pb)
            finish(ga, ma, oa)
            finish(gb, mbx, ob)
            return carry

        assert (dil * nbq) % ATT_UNROLL == 0
        lax.fori_loop(0, dil * nbq // ATT_UNROLL, body, 0)

    nb = len(plan)
    mx = mb_scr[0]
    for bi in range(1, nb):
        mx = jnp.maximum(mx, mb_scr[bi])
    num = jnp.zeros((S, LANES), F32)
    den = jnp.zeros((S, LANES), F32)
    for bi in range(nb):
        w = jnp.exp(mb_scr[bi] - mx)
        num = num + w * ob_scr[bi]
        den = den + w * db_scr[bi]
    o_ref[...] = (num / den).astype(o_ref.dtype)


def _attention(proj, qg2, kg2, profiles, B, S, HA, hd):
    T = proj.shape[0]
    pairs = HA // 2
    da_blocks = HA * hd // LANES
    plan, nvar = _attn_plan(S)
    return pl.pallas_call(
        functools.partial(_attn_kernel, plan=plan, hd=hd),
        grid=(pairs, B),
        in_specs=[pl.BlockSpec((S, LANES), lambda p, b: (b, p)),
                  pl.BlockSpec((S, LANES), lambda p, b: (b, da_blocks + p)),
                  pl.BlockSpec((S, LANES), lambda p, b: (b, 2 * da_blocks + p)),
                  pl.BlockSpec((1, LANES), lambda p, b: (0, 0)),
                  pl.BlockSpec((1, LANES), lambda p, b: (0, 0)),
                  pl.BlockSpec((None, 2 * nvar, ATT_PROF_W), lambda p, b: (p, 0, 0))],
        out_specs=pl.BlockSpec((S, LANES), lambda p, b: (b, p)),
        out_shape=jax.ShapeDtypeStruct((T, HA * hd), BF16),
        scratch_shapes=[pltpu.VMEM((S, LANES), F32)] * 3 + [pltpu.VMEM((len(plan), S, LANES), F32)] * 3
        + [pltpu.VMEM((nvar, 2, ATT_QB, ATT_KW), F32)],
        compiler_params=_cparams(("arbitrary", "arbitrary"), 40),
    )(proj, proj, proj, qg2, kg2, profiles)


def _gdn_kernel(q_ref, k_ref, v_ref, z_ref, cw_ref, pack_ref, og_ref, o_ref,
                q_scr, k_scr, v_scr, xpad_scr, pk2_scr, sel_scr, u_scr, wq_scr, at_scr, kdt_scr, et_scr, oacc_scr,
                *, hb, hp):
    S = q_ref.shape[0]
    P2 = 2 * CHUNK
    W2 = 2 * LANES
    npair = S // P2
    hg = pl.program_id(1)
    gw = 4 * hb
    dk = LANES

    pad = SUBLANES
    for slot in range(2):
        xpad_scr[slot, 0:pad, :] = jnp.zeros((pad, LANES), F32)
        xpad_scr[slot, pad + S:, :] = jnp.zeros((pad, LANES), F32)

    seg = GDN_SEG if S % GDN_SEG == 0 else S

    def conv_silu_to(src_ref, lanes_j, j, which, dst_scr, l2norm, scale):
        xp = xpad_scr.at[(3 * j + which) % 2]
        for s0 in range(0, S, seg):
            xp[pad + s0:pad + s0 + seg, :] = src_ref[s0:s0 + seg, lanes_j].astype(F32)
        for s0 in range(0, S, seg):
            acc = xp[pad + s0:pad + s0 + seg, :] * cw_ref[j, which, 2:3, :]
            for d in (-2, -1, 1, 2):
                acc = acc + xp[pad + s0 + d:pad + s0 + d + seg, :] * cw_ref[j, which, 2 + d:3 + d, :]
            y = _silu(acc)
            if l2norm:
                y = y * (lax.rsqrt(jnp.sum(y * y, axis=-1, keepdims=True) + EPS) * scale)
            dst_scr[j, s0:s0 + seg, :] = y

    for s0 in range(0, S, seg):
        pk = pack_ref[s0:s0 + seg, :]
        p_hi = pk.astype(BF16)
        p_lo = (pk - p_hi.astype(F32)).astype(BF16)
        pk2_scr[s0:s0 + seg, :] = jnp.concatenate([p_hi, p_lo], axis=1)
    srow = lax.broadcasted_iota(I32, (W2, 4 * LANES), 0) % LANES
    scol = lax.broadcasted_iota(I32, (W2, 4 * LANES), 1) // LANES
    for j in range(hp):
        lanes_j = slice(j * LANES, (j + 1) * LANES)
        conv_silu_to(q_ref, lanes_j, j, 0, q_scr, True, dk ** -0.5)
        conv_silu_to(k_ref, lanes_j, j, 1, k_scr, True, 1.0)
        conv_silu_to(v_ref, lanes_j, j, 2, v_scr, False, 1.0)
        src = hg * hp + j + jnp.where(scol < 2, scol * hb, gw + 2 * hb + (scol - 2) * hb)
        sel_scr[j] = jnp.where(srow == src, 1.0, 0.0).astype(BF16)
        oacc_scr[j] = jnp.zeros((S, LANES), F32)

    r4 = lax.broadcasted_iota(I32, (CHUNK, W2), 0)
    l4 = lax.broadcasted_iota(I32, (CHUNK, W2), 1)
    c4 = l4 % CHUNK
    blk4 = l4 // CHUNK
    lo_half = (l4 % LANES) < CHUNK
    ahead = jnp.where(l4 >= LANES, r4 - c4, c4 - r4)
    incl = ahead <= 0
    strict = ahead < 0
    bd16 = (r4 // 16) == (c4 // 16)

    def squeeze(x):
        return jnp.where(lo_half, x[:CHUNK], x[CHUNK:])

    def unsqueeze(x):
        return jnp.concatenate([jnp.where(lo_half, x, jnp.zeros_like(x)),
                                jnp.where(lo_half, jnp.zeros_like(x), x)], axis=0)

    def mm4(a, b):
        rhs = jnp.concatenate([jnp.where(blk4 == g, b, 0.0) for g in range(4)], axis=0)
        return _dot(a.astype(BF16), rhs.astype(BF16))

    first = lax.broadcasted_iota(I32, (P2, 1), 0) < CHUNK
    zpair = jnp.zeros((P2, LANES), BF16)

    def bdiag(x):
        return jnp.concatenate([jnp.concatenate([x[:, :LANES], zpair], axis=1),
                                jnp.concatenate([zpair, x[:, LANES:]], axis=1)], axis=0)

    U = GDN_PREP_UNROLL if npair % GDN_PREP_UNROLL == 0 else 1

    def prep(t, carry):
        cx = []
        chains = [(t * U + u, j) for u in range(U) for j in range(hp)]
        bcs = [_dot(pk2_scr[pl.ds(pl.multiple_of(m * P2, P2), P2), :], sel_scr[j]) for m, j in chains]
        for (m, j), bc in zip(chains, bcs):
            rows = pl.ds(pl.multiple_of(m * P2, P2), P2)
            kp = k_scr[j, rows, :]
            qp = q_scr[j, rows, :]
            vp = v_scr[j, rows, :]
            gc2 = bc[:, 0:W2]
            beta2 = bc[:, W2:2 * W2]
            gcf, gcb = gc2[:, :LANES], gc2[:, LANES:]
            tot2 = jnp.concatenate([jnp.where(first, gcf[CHUNK - 1:CHUNK], gcf[P2 - 1:P2]),
                                    jnp.where(first, gcb[0:1], gcb[CHUNK:CHUNK + 1])], axis=1)
            egc2 = jnp.exp(gc2)
            k2 = jnp.concatenate([kp, kp], axis=1)
            kb2 = k2 * beta2
            vb2 = jnp.concatenate([vp, vp], axis=1) * beta2
            kbe2 = kb2 * egc2
            cx.append(dict(
                m=m, j=j, rows=rows, vb2=vb2, kbe2=kbe2,
                qeb=(jnp.concatenate([qp, qp], axis=1) * egc2).astype(BF16),
                kd2=k2 * jnp.exp(tot2 - gc2),
                et=jnp.exp(tot2),
                dec=jnp.exp(jnp.where(
                    incl, squeeze(gc2) - squeeze(jnp.concatenate([gcf.T, gcb.T], axis=1)), -jnp.inf)),
                stk=jnp.concatenate([kb2[:, :LANES], kb2[:, LANES:], qp], axis=0).astype(BF16),
                kpb=kp.astype(BF16)))

        g3s = [_dot_nt(c['stk'], c['kpb']) for c in cx]
        for c, g3 in zip(cx, g3s):
            lm = jnp.where(strict, squeeze(jnp.concatenate([g3[:P2], g3[P2:2 * P2]], axis=1)) * c['dec'], 0.0)
            attn = squeeze(jnp.concatenate([g3[2 * P2:], g3[2 * P2:]], axis=1)) * c['dec']
            c['attn2'] = unsqueeze(attn).astype(BF16)
            c['lbd'] = jnp.where(bd16, lm, 0.0)
            c['loff'] = lm - c['lbd']
        nn = [-c['lbd'] for c in cx]
        pw = [mm4(c['lbd'], c['lbd']) for c in cx]
        for rnd in range(3):
            prod = [mm4(a, p) for a, p in zip(nn, pw)]
            nxt = [mm4(p, p) for p in pw] if rnd < 2 else pw
            nn = [a + p + q for a, p, q in zip(nn, pw, prod)]
            pw = nxt
        mo = [c['loff'] + x for c, x in zip(cx, [mm4(a, c['loff']) for a, c in zip(nn, cx)])]
        m2 = [mm4(x, x) for x in mo]
        mn = [mm4(x, a) for x, a in zip(mo, nn)]
        xo = [a - x - y for a, x, y in zip(nn, mo, mn)]
        mx = [mm4(a, b) for a, b in zip(m2, xo)]
        toff = [a + b + q for a, b, q in zip(xo, m2, mx)]
        z4 = jnp.zeros((CHUNK, W2), BF16)
        tws = []
        for c, tf in zip(cx, toff):
            vbb, kbb = c['vb2'].astype(BF16), c['kbe2'].astype(BF16)
            blocks = []
            for g in range(4):
                rr = slice((g % 2) * CHUNK, (g % 2 + 1) * CHUNK)
                ll = slice((g // 2) * LANES, (g // 2 + 1) * LANES)
                blocks.append(jnp.concatenate([z4] * g + [vbb[rr, ll], kbb[rr, ll]] + [z4] * (3 - g), axis=1))
            tws.append(_dot(tf.astype(BF16), jnp.concatenate(blocks, axis=0)))
        for c, tw in zip(cx, tws):
            m, j, rows = c['m'], c['j'], c['rows']
            def pair_layout(off):
                piece = lambda g: tw[:, 2 * g * LANES + off:2 * g * LANES + off + LANES]
                return jnp.concatenate([jnp.concatenate([piece(0), piece(2)], axis=1),
                                        jnp.concatenate([piece(1), piece(3)], axis=1)], axis=0)

            u2 = c['vb2'] + pair_layout(0)
            w2 = (c['kbe2'] + pair_layout(LANES)).astype(BF16)
            qeb = c['qeb']
            u_scr[j, rows, :] = u2
            wq_scr[j, pl.ds(pl.multiple_of(m * 2 * P2, 2 * P2), 2 * P2), :] = jnp.concatenate(
                [w2[:CHUNK], qeb[:CHUNK], w2[CHUNK:], qeb[CHUNK:]], axis=0)
            at_scr[j, rows, :] = c['attn2']
            kdt_scr[j, 0, :, rows] = c['kd2'][:, :LANES].T.astype(BF16)
            kdt_scr[j, 1, :, rows] = c['kd2'][:, LANES:].T.astype(BF16)
            et_scr[j, pl.ds(pl.multiple_of(m * 2 * SUBLANES, 2 * SUBLANES), 2 * SUBLANES), :] = jnp.concatenate(
                [c['et'][:SUBLANES], c['et'][CHUNK:CHUNK + SUBLANES]], axis=0)
        return carry

    lax.fori_loop(0, npair // U, prep, 0)

    zc = jnp.zeros((CHUNK, LANES), F32)
    zp = jnp.zeros((P2, LANES), F32)

    def place(v, cpos):
        return jnp.concatenate([v, zc] if cpos == 0 else [zc, v], axis=0)

    def scan(m, states):
        pf = m
        pb = npair - 1 - m
        rows_f = pl.ds(pl.multiple_of(pf * P2, P2), P2)
        rows_b = pl.ds(pl.multiple_of(pb * P2, P2), P2)
        hx = []
        for j in range(hp):
            hx.append(dict(
                u_f=u_scr[j, rows_f, :LANES], u_b=u_scr[j, rows_b, LANES:],
                at_f=at_scr[j, rows_f, :LANES], at_b=at_scr[j, rows_b, LANES:],
                kdt=jnp.concatenate([kdt_scr[j, 0, :, rows_f], kdt_scr[j, 1, :, rows_b]], axis=1),
                wq_f=wq_scr[j, pl.ds(pl.multiple_of(pf * 2 * P2, 2 * P2), 2 * P2), :LANES],
                wq_b=wq_scr[j, pl.ds(pl.multiple_of(pb * 2 * P2, 2 * P2), 2 * P2), LANES:],
                et_f=et_scr[j, pl.ds(pl.multiple_of(pf * 2 * SUBLANES, 2 * SUBLANES), 2 * SUBLANES), :LANES],
                et_b=et_scr[j, pl.ds(pl.multiple_of(pb * 2 * SUBLANES, 2 * SUBLANES), 2 * SUBLANES), LANES:]))
        sts = list(states)
        for step in range(2):
            cf, cb = step, 1 - step
            rrs = [_dot(jnp.concatenate([c['wq_f'][cf * P2:(cf + 1) * P2], c['wq_b'][cb * P2:(cb + 1) * P2]], axis=1),
                        bdiag(st.astype(BF16))) for c, st in zip(hx, sts)]
            ress = []
            for c, rr in zip(hx, rrs):
                u2 = jnp.concatenate([c['u_f'][cf * CHUNK:(cf + 1) * CHUNK],
                                      c['u_b'][cb * CHUNK:(cb + 1) * CHUNK]], axis=1)
                v_new = u2 - rr[:CHUNK]
                rhs = jnp.concatenate(
                    [jnp.concatenate([place(v_new[:, :LANES], cf), zp], axis=1),
                     jnp.concatenate([zp, place(v_new[:, LANES:], cb)], axis=1)], axis=0).astype(BF16)
                lhs = jnp.concatenate(
                    [jnp.concatenate([c['at_f'][cf * CHUNK:(cf + 1) * CHUNK],
                                      c['at_b'][cb * CHUNK:(cb + 1) * CHUNK]], axis=1),
                     c['kdt']], axis=0)
                ress.append(_dot(lhs, rhs))
            for j, (c, rr, res) in enumerate(zip(hx, rrs, ress)):
                o2 = rr[CHUNK:] + res[:CHUNK]
                et2 = jnp.concatenate([c['et_f'][cf * SUBLANES:cf * SUBLANES + 1],
                                       c['et_b'][cb * SUBLANES:cb * SUBLANES + 1]], axis=1)
                sts[j] = sts[j] * et2 + res[CHUNK:]
                of_rows = pl.ds(pl.multiple_of(pf * P2 + cf * CHUNK, CHUNK), CHUNK)
                ob_rows = pl.ds(pl.multiple_of(pb * P2 + cb * CHUNK, CHUNK), CHUNK)
                oacc_scr[j, of_rows, :] = oacc_scr[j, of_rows, :] + o2[:, :LANES]
                oacc_scr[j, ob_rows, :] = oacc_scr[j, ob_rows, :] + o2[:, LANES:]
        return tuple(sts)

    s0 = jnp.zeros((dk, W2), F32)
    lax.fori_loop(0, npair, scan, (s0,) * hp)

    for j in range(hp):
        lanes_j = slice(j * LANES, (j + 1) * LANES)
        for s0 in range(0, S, seg):
            o = oacc_scr[j, s0:s0 + seg, :]
            y = o * lax.rsqrt(jnp.mean(o * o, axis=-1, keepdims=True) + EPS) * og_ref[...]
            z = z_ref[s0:s0 + seg, lanes_j].astype(F32)
            o_ref[s0:s0 + seg, lanes_j] = (y * _silu(z)).astype(o_ref.dtype)


def _gdn(proj, cw4, pack3, onorm_g, B, S, HB, base_blk):
    T = proj.shape[0]
    hp = GDN_HP if (HB % GDN_HP == 0 and base_blk % GDN_HP == 0) else 1
    wblk = hp * LANES
    npair = S // (2 * CHUNK)

    def col(k):
        off = (base_blk + k * HB) // hp
        return lambda b, h: (b, off + h)

    return pl.pallas_call(
        functools.partial(_gdn_kernel, hb=HB, hp=hp),
        grid=(B, HB // hp),
        in_specs=[pl.BlockSpec((S, wblk), col(0)),
                  pl.BlockSpec((S, wblk), col(1)),
                  pl.BlockSpec((S, wblk), col(2)),
                  pl.BlockSpec((S, wblk), col(3)),
                  pl.BlockSpec((hp, 3, cw4.shape[2], LANES), lambda b, h: (h, 0, 0, 0)),
                  pl.BlockSpec((None, S, LANES), lambda b, h: (b, 0, 0)),
                  pl.BlockSpec((1, LANES), lambda b, h: (0, 0))],
        out_specs=pl.BlockSpec((S, wblk), lambda b, h: (b, h)),
        out_shape=jax.ShapeDtypeStruct((T, HB * LANES), BF16),
        scratch_shapes=[pltpu.VMEM((hp, S, LANES), F32)] * 3
        + [pltpu.VMEM((2, S + 2 * SUBLANES, LANES), F32),
           pltpu.VMEM((S, 2 * LANES), BF16),
           pltpu.VMEM((hp, 2 * LANES, 4 * LANES), BF16),
           pltpu.VMEM((hp, S, 2 * LANES), F32),
           pltpu.VMEM((hp, 2 * S, 2 * LANES), BF16),
           pltpu.VMEM((hp, S, 2 * LANES), BF16),
           pltpu.VMEM((hp, 2, LANES, S), BF16),
           pltpu.VMEM((hp, npair * 2 * SUBLANES, 2 * LANES), F32),
           pltpu.VMEM((hp, S, LANES), F32)],
        compiler_params=_cparams(("arbitrary", "arbitrary"), 56),
    )(proj, proj, proj, proj, cw4, pack3, onorm_g)


def _outproj_kernel(oa_ref, ob_ref, x_ref, gt_ref, w_ref, g_ref, sc_ref, sh_ref, wr_ref, br_ref,
                    x1_ref, hp_ref, lg_ref, h_scr):
    da = oa_ref.shape[1]
    y = _dot(oa_ref[...], w_ref[:da, :]) + _dot(ob_ref[...], w_ref[da:, :])
    x1 = x_ref[...] + gt_ref[...] * y
    x1_ref[...] = x1
    hn = x1 * lax.rsqrt(jnp.mean(x1 * x1, axis=-1, keepdims=True) + EPS) * g_ref[...]
    h = hn * (1.0 + sc_ref[...]) + sh_ref[...]
    hp_ref[...] = _pack_halves(h)
    h_scr[...] = h

    @pl.when(pl.program_id(0) < pl.num_programs(0))
    def _():
        lg_ref[...] = _dot_hilo(h_scr[...], wr_ref[...]) + br_ref[...]


def _outproj(oa, ob, x2, gt1, w_out_b, g2, sc2, sh2, wr, br, S):
    T, D = x2.shape
    tm = ROW_TILE
    per_b = S // tm
    bmap = lambda i: (i // per_b, 0, 0)
    return pl.pallas_call(
        _outproj_kernel,
        grid=(T // tm,),
        in_specs=[pl.BlockSpec((tm, oa.shape[1]), lambda i: (i, 0)),
                  pl.BlockSpec((tm, ob.shape[1]), lambda i: (i, 0)),
                  pl.BlockSpec((tm, D), lambda i: (i, 0)),
                  pl.BlockSpec((None, 1, D), bmap),
                  pl.BlockSpec((D, D), lambda i: (0, 0)),
                  pl.BlockSpec((1, D), lambda i: (0, 0)),
                  pl.BlockSpec((None, 1, D), bmap),
                  pl.BlockSpec((None, 1, D), bmap),
                  pl.BlockSpec((D, 2 * LANES), lambda i: (0, 0)),
                  pl.BlockSpec((1, LANES), lambda i: (0, 0))],
        out_specs=[pl.BlockSpec((tm, D), lambda i: (i, 0)),
                   pl.BlockSpec((tm, D // 2), lambda i: (i, 0)),
                   pl.BlockSpec((tm, LANES), lambda i: (i, 0))],
        out_shape=[jax.ShapeDtypeStruct((T, D), F32), jax.ShapeDtypeStruct((T, D // 2), U32),
                   jax.ShapeDtypeStruct((T, LANES), F32)],
        scratch_shapes=[pltpu.VMEM((tm, D), F32)],
        compiler_params=_cparams(("arbitrary",), 48),
    )(oa, ob, x2, gt1, w_out_b, g2, sc2, sh2, wr, br)


def _route_kernel(lg_ref, o_ref, info_ref, run_scr, *, ne, ng):
    ph = pl.program_id(0)
    i = pl.program_id(1)

    @pl.when((ph == 0) & (i == 0))
    def _():
        run_scr[...] = jnp.zeros_like(run_scr)

    @pl.when((ph == 1) & (i == 0))
    def _():
        cnt = run_scr[...]
        padded = jnp.ceil(cnt * (1.0 / MOE_RB)) * MOE_RB
        k = lax.broadcasted_iota(I32, (LANES, LANES), 0)
        e = lax.broadcasted_iota(I32, (LANES, LANES), 1)
        start = _dot(padded, jnp.where(k < e, 1.0, 0.0), precision=HIGHEST)
        rowi = lax.broadcasted_iota(I32, cnt.shape, 0)
        info_ref[...] = jnp.where(rowi == 0, cnt, jnp.where(rowi == 1, padded, start))
        run_scr[...] = start

    lg = lg_ref[...]
    tm = lg.shape[0]
    epg = ne // ng
    lane_i = lax.broadcasted_iota(I32, lg.shape, 1)
    lane = lane_i.astype(F32)
    big = float(2 * LANES)
    is_g = (lane_i >= ne) & (lane_i < ne + ng)
    gl = jnp.where(is_g, lg, -jnp.inf)
    gmax = jnp.max(gl, axis=-1, keepdims=True)
    gidx = jnp.min(jnp.where(gl == gmax, lane, big), axis=-1, keepdims=True) - ne
    psel = 1.0 / jnp.sum(jnp.where(is_g, jnp.exp(gl - gmax), 0.0), axis=-1, keepdims=True)
    in_grp = (lane_i // epg).astype(F32) == gidx
    el = jnp.where(in_grp & (lane_i < ne), lg, -jnp.inf)
    m1 = jnp.max(el, axis=-1, keepdims=True)
    i1 = jnp.min(jnp.where(el == m1, lane, big), axis=-1, keepdims=True)
    el2 = jnp.where(lane == i1, -jnp.inf, el)
    m2 = jnp.max(el2, axis=-1, keepdims=True)
    i2 = jnp.min(jnp.where(el2 == m2, lane, big), axis=-1, keepdims=True)
    e21 = jnp.exp(m2 - m1)
    g1 = psel / (1.0 + e21)
    g2 = psel * e21 / (1.0 + e21)
    o1 = jnp.where(lane == i1, 1.0, 0.0)
    o2 = jnp.where(lane == i2, 1.0, 0.0)
    cnt = o1 + o2
    r = lax.broadcasted_iota(I32, (tm, tm), 0)
    c = lax.broadcasted_iota(I32, (tm, tm), 1)
    before = _dot(jnp.where(c < r, 1.0, 0.0).astype(BF16), cnt.astype(BF16)) + run_scr[0:1, :]
    d1 = jnp.sum(o1 * before, axis=-1, keepdims=True)
    d2 = jnp.sum(o2 * before, axis=-1, keepdims=True)
    run_scr[...] = run_scr[...] + jnp.sum(cnt, axis=0, keepdims=True)

    @pl.when(ph == 1)
    def _():
        out = jnp.zeros(lg.shape, F32)
        for j, val in enumerate((i1, i2, g1, g2, d1, d2)):
            out = jnp.where(lane_i == j, val, out)
        o_ref[...] = out


def _route(logits, ne, ng):
    T = logits.shape[0]
    tm = _pick_tile(T, (1024, 512, 256))
    return pl.pallas_call(
        functools.partial(_route_kernel, ne=ne, ng=ng),
        grid=(2, T // tm),
        in_specs=[pl.BlockSpec((tm, LANES), lambda p, i: (i, 0))],
        out_specs=[pl.BlockSpec((tm, LANES), lambda p, i: (i * p, 0)),
                   pl.BlockSpec((SUBLANES, LANES), lambda p, i: (0, 0))],
        out_shape=[jax.ShapeDtypeStruct((T, LANES), F32), jax.ShapeDtypeStruct((SUBLANES, LANES), F32)],
        scratch_shapes=[pltpu.VMEM((SUBLANES, LANES), F32)],
        compiler_params=_cparams(("arbitrary", "arbitrary"), 16),
    )(logits)


def _dispatch_kernel(dest_ref, h_hbm, xs_in, xs_hbm, sem):
    del xs_in
    tm = ROW_TILE
    base = pl.program_id(0) * tm

    def copy(tok, a):
        return pltpu.make_async_copy(h_hbm.at[pl.ds(tok, 1)], xs_hbm.at[pl.ds(dest_ref[a], 1)], sem)

    def issue(t, carry):
        for k in range(TOP_K):
            copy(base + t, (base + t) * TOP_K + k).start()
        return carry

    lax.fori_loop(0, tm, issue, 0, unroll=DMA_UNROLL)

    for k in range(TOP_K):
        pltpu.make_async_copy(h_hbm.at[pl.ds(0, tm)], xs_hbm.at[pl.ds(0, tm)], sem).wait()


def _dispatch(dest, hpk, p_rows):
    T, dh = hpk.shape
    return pl.pallas_call(
        _dispatch_kernel,
        grid_spec=pltpu.PrefetchScalarGridSpec(
            num_scalar_prefetch=1,
            grid=(T // ROW_TILE,),
            in_specs=[pl.BlockSpec(memory_space=pl.ANY), pl.BlockSpec(memory_space=pl.ANY)],
            out_specs=pl.BlockSpec(memory_space=pl.ANY),
            scratch_shapes=[pltpu.SemaphoreType.DMA(())]),
        out_shape=jax.ShapeDtypeStruct((p_rows, dh), U32),
        input_output_aliases={2: 0},
        compiler_params=_cparams(("arbitrary",), 16),
    )(dest, hpk, jnp.zeros((p_rows, dh), U32))


def _moe_kernel(we_ref, ws_ref, wn_ref, wt_ref, xs_hbm, w1_hbm, w3_hbm, w2_hbm, y_hbm,
                xbuf, xlo, xhi, yacc, ypk, w1buf, w3buf, w2buf, sem_in, sem_out, sem_w):
    w = pl.program_id(0)
    c = pl.program_id(1)
    nw = pl.num_programs(0)
    nc = pl.num_programs(1)
    nrows = wn_ref[w]
    start = ws_ref[w]
    nblk = nrows // MOE_RB
    dh = xbuf.shape[1]
    cw = w1buf.shape[-1]

    g = w * nc + c

    def w_copies(step, do):
        item = step // nc
        chunk = step % nc

        @pl.when((item < nw) & (wn_ref[jnp.minimum(item, nw - 1)] > 0))
        def _():
            e = we_ref[item]
            cols = pl.ds(pl.multiple_of(chunk * cw, cw), cw)
            slot = step % MOE_WSLOTS
            do(pltpu.make_async_copy(w1_hbm.at[e, :, cols], w1buf.at[slot], sem_w.at[slot]))
            do(pltpu.make_async_copy(w3_hbm.at[e, :, cols], w3buf.at[slot], sem_w.at[slot]))
            do(pltpu.make_async_copy(w2_hbm.at[e, cols, :], w2buf.at[slot], sem_w.at[slot]))

    @pl.when(g == 0)
    def _():
        for ahead in range(MOE_WSLOTS - 1):
            w_copies(ahead, lambda cp: cp.start())

    w_copies(g + MOE_WSLOTS - 1, lambda cp: cp.start())
    w_copies(g, lambda cp: cp.wait())
    wslot = g % MOE_WSLOTS
    w1_ref, w3_ref, w2_ref = w1buf.at[wslot], w3buf.at[wslot], w2buf.at[wslot]

    def blk_rows(rb):
        return pl.ds(pl.multiple_of(rb * MOE_RB, MOE_RB), MOE_RB)

    def hbm_rows(item_start, rb):
        return pl.ds(pl.multiple_of(item_start + rb * MOE_RB, MOE_RB), MOE_RB)

    def in_copy(item_start, rb):
        return pltpu.make_async_copy(xs_hbm.at[hbm_rows(item_start, rb)], xbuf.at[blk_rows(rb)], sem_in)

    def out_copy(item_start, rb):
        return pltpu.make_async_copy(ypk.at[blk_rows(rb)], y_hbm.at[hbm_rows(item_start, rb)], sem_out)

    def each_block(n, fn):
        def body(rb, carry):
            fn(rb)
            return carry
        lax.fori_loop(0, n, body, 0)

    @pl.when(c == 0)
    def _load():
        @pl.when(w == 0)
        def _():
            each_block(nblk, lambda rb: in_copy(start, rb).start())

        each_block(nblk, lambda rb: in_copy(start, rb).wait())

        def unpack(rb):
            lo, hi = _unpack_halves(xbuf[blk_rows(rb), :])
            xlo[blk_rows(rb), :] = lo.astype(BF16)
            xhi[blk_rows(rb), :] = hi.astype(BF16)
            yacc[blk_rows(rb), :] = jnp.zeros((MOE_RB, 2 * dh), F32)

        each_block(nblk, unpack)

        @pl.when(w + 1 < nw)
        def _():
            nxt = ws_ref[w + 1]
            each_block(wn_ref[w + 1] // MOE_RB, lambda rb: in_copy(nxt, rb).start())

    @pl.when(nrows > 0)
    def _compute():
        def rows_block(row0, nr):
            rows = pl.ds(row0, nr)
            xl = xlo[rows, :]
            xh = xhi[rows, :]
            h1 = _dot(xl, w1_ref[:dh, :].astype(BF16)) + _dot(xh, w1_ref[dh:, :].astype(BF16))
            h3 = _dot(xl, w3_ref[:dh, :].astype(BF16)) + _dot(xh, w3_ref[dh:, :].astype(BF16))
            hid = (_silu(h1) * h3).astype(BF16)
            yacc[rows, :] = yacc[rows, :] + _dot(hid, w2_ref[...].astype(BF16))

        ntall = nrows // MOE_TALL
        each_block(ntall, lambda i: rows_block(pl.multiple_of(i * MOE_TALL, MOE_TALL), MOE_TALL))

        rem = nrows - ntall * MOE_TALL
        for nr in range(MOE_RB, MOE_TALL, MOE_RB):
            @pl.when(rem == nr)
            def _(nr=nr):
                rows_block(pl.multiple_of(ntall * MOE_TALL, MOE_TALL), nr)

    @pl.when(c == nc - 1)
    def _store():
        @pl.when(w > 0)
        def _():
            prev = ws_ref[w - 1]
            each_block(wn_ref[w - 1] // MOE_RB, lambda rb: out_copy(prev, rb).wait())

        def pack(rb):
            ypk[blk_rows(rb), :] = _pack_halves(yacc[blk_rows(rb), :])

        each_block(nblk, pack)
        each_block(nblk, lambda rb: out_copy(start, rb).start())

        @pl.when(w == nw - 1)
        def _():
            each_block(nblk, lambda rb: out_copy(start, rb).wait())

    @pl.when((w == nw - 1) & (c == nc - 1))
    def _zero_tail():
        ypk[blk_rows(0), :] = jnp.zeros((MOE_RB, dh), U32)
        first = wt_ref[0] // MOE_RB

        def tail_copy(b):
            return pltpu.make_async_copy(ypk.at[blk_rows(0)],
                                         y_hbm.at[pl.ds(pl.multiple_of(b * MOE_RB, MOE_RB), MOE_RB)], sem_out)

        def start(b, carry):
            tail_copy(b).start()
            return carry

        def wait(b, carry):
            tail_copy(b).wait()
            return carry

        lax.fori_loop(first, y_hbm.shape[0] // MOE_RB, start, 0)
        lax.fori_loop(first, y_hbm.shape[0] // MOE_RB, wait, 0)


def _moe(we, ws, wn, wt, xs, w1, w3, w2):
    P, dh = xs.shape
    D = 2 * dh
    NE, _, DE = w1.shape
    cw = min(MOE_CW, DE)
    nc = DE // cw
    nw = we.shape[0]
    return pl.pallas_call(
        _moe_kernel,
        grid_spec=pltpu.PrefetchScalarGridSpec(
            num_scalar_prefetch=4,
            grid=(nw, nc),
            in_specs=[pl.BlockSpec(memory_space=pl.ANY)] * 4,
            out_specs=pl.BlockSpec(memory_space=pl.ANY),
            scratch_shapes=[pltpu.VMEM((MOE_RMAX, dh), U32),
                            pltpu.VMEM((MOE_RMAX, dh), BF16),
                            pltpu.VMEM((MOE_RMAX, dh), BF16),
                            pltpu.VMEM((MOE_RMAX, D), F32),
                            pltpu.VMEM((MOE_RMAX, dh), U32),
                            pltpu.VMEM((MOE_WSLOTS, D, cw), F32),
                            pltpu.VMEM((MOE_WSLOTS, D, cw), F32),
                            pltpu.VMEM((MOE_WSLOTS, cw, D), F32),
                            pltpu.SemaphoreType.DMA(()),
                            pltpu.SemaphoreType.DMA(()),
                            pltpu.SemaphoreType.DMA((MOE_WSLOTS,))]),
        out_shape=jax.ShapeDtypeStruct((P, dh), U32),
        compiler_params=_cparams(("arbitrary", "arbitrary"), 56),
    )(we, ws, wn, wt, xs, w1, w3, w2)


def _moe_schedule(info, NE, p_rows):
    padded = info[1, :NE].astype(I32)
    start_pad = info[2, :NE].astype(I32)
    items = (padded + MOE_RMAX - 1) // MOE_RMAX
    cum_items = jnp.cumsum(items)
    n_items = cum_items[-1]
    nw = (p_rows + NE * (MOE_RMAX - MOE_RB)) // MOE_RMAX
    wi = jnp.arange(nw, dtype=I32)
    valid = wi < n_items
    wi_c = jnp.minimum(wi, jnp.maximum(n_items - 1, 0))
    we = jnp.minimum(jnp.searchsorted(cum_items, wi_c, side='right'), NE - 1).astype(I32)
    local = wi_c - (cum_items[we] - items[we])
    ws = (start_pad[we] + local * MOE_RMAX).astype(I32)
    wn = jnp.where(valid, jnp.clip(padded[we] - local * MOE_RMAX, 0, MOE_RMAX), 0).astype(I32)
    wt = (start_pad[NE - 1] + padded[NE - 1]).reshape(1)
    return we, ws, wn, wt


def _combine_kernel(dest_ref, x1_ref, rt_ref, gt_ref, y_hbm, o_ref, ybuf, sems):
    tm = ROW_TILE
    i = pl.program_id(0)
    n = pl.num_programs(0)
    dh = ybuf.shape[-1]

    def copy(tile, slot, t, k):
        a = (tile * tm + t) * TOP_K + k
        return pltpu.make_async_copy(y_hbm.at[pl.ds(dest_ref[a], 1)], ybuf.at[slot, k, pl.ds(t, 1)], sems.at[slot])

    def issue_tile(tile, slot):
        def body(t, carry):
            for k in range(TOP_K):
                copy(tile, slot, t, k).start()
            return carry
        lax.fori_loop(0, tm, body, 0, unroll=DMA_UNROLL)

    @pl.when(i == 0)
    def _():
        issue_tile(0, 0)

    @pl.when(i + 1 < n)
    def _():
        issue_tile(i + 1, (i + 1) % 2)

    slot = i % 2

    for k in range(TOP_K):
        pltpu.make_async_copy(y_hbm.at[pl.ds(0, tm)], ybuf.at[slot, k], sems.at[slot]).wait()

    rt = rt_ref[...]
    g1 = rt[:, 2:3]
    g2 = rt[:, 3:4]
    lo1, hi1 = _unpack_halves(ybuf[slot, 0])
    lo2, hi2 = _unpack_halves(ybuf[slot, 1])
    o_ref[:, :dh] = x1_ref[:, :dh] + gt_ref[:, :dh] * (g1 * lo1 + g2 * lo2)
    o_ref[:, dh:] = x1_ref[:, dh:] + gt_ref[:, dh:] * (g1 * hi1 + g2 * hi2)


def _combine(dest, x1, route, gt2, ypk, S):
    T, D = x1.shape
    tm = ROW_TILE
    per_b = S // tm
    return pl.pallas_call(
        _combine_kernel,
        grid_spec=pltpu.PrefetchScalarGridSpec(
            num_scalar_prefetch=1,
            grid=(T // tm,),
            in_specs=[pl.BlockSpec((tm, D), lambda i, d: (i, 0)),
                      pl.BlockSpec((tm, LANES), lambda i, d: (i, 0)),
                      pl.BlockSpec((None, 1, D), lambda i, d: (i // per_b, 0, 0)),
                      pl.BlockSpec(memory_space=pl.ANY)],
            out_specs=pl.BlockSpec((tm, D), lambda i, d: (i, 0)),
            scratch_shapes=[pltpu.VMEM((2, TOP_K, tm, D // 2), U32),
                            pltpu.SemaphoreType.DMA((2,))]),
        out_shape=jax.ShapeDtypeStruct((T, D), F32),
        compiler_params=_cparams(("arbitrary",), 32),
    )(dest, x1, route, gt2, ypk)


def kernel(x, c, w_ada, b_ada, norm1_g, norm2_g, w_in, qn_g, kn_g, rel_bias, conv_w, A_log, dt_bias,
           onorm_g, w_out, w_rg, b_rg, w_re, b_re, w1, w3, w2):
    B, S, D = x.shape
    depth = w_ada.shape[0]
    HA, hda = rel_bias.shape[1], qn_g.shape[-1]
    HB, hdb = A_log.shape[-1], onorm_g.shape[-1]
    DA, DB = HA * hda, HB * hdb
    NG, NE = w_rg.shape[-1], w_re.shape[-1]
    T = B * S
    assert hdb == LANES and 2 * hda == LANES and DA + DB == D and 16 * HB <= LANES
    assert NE + NG <= LANES and conv_w.shape[1] == 5 and S % ROW_TILE == 0
    n_main = 3 * DA + 4 * DB
    p_rows = TOP_K * T + NE * MOE_RB

    bias_prof = _attn_bias_profiles(rel_bias, S)
    w_in_t = jnp.swapaxes(w_in, 1, 2)
    x2 = x.reshape(T, D)
    for l in range(depth):
        mod = _ada(c, w_ada[l], b_ada[l]).reshape(B, 6, 1, D)
        sh1, sc1, gt1, sh2, sc2, gt2 = (mod[:, i] for i in range(6))

        w_gate = jnp.tile(w_in_t[l][n_main:, :].T, (1, 4))
        w_gate = _hilo_weights(jnp.pad(w_gate, ((0, 0), (0, LANES - w_gate.shape[1]))))
        proj, gat = _inproj(x2, norm1_g[l].reshape(1, D), sc1, sh1, w_in_t[l], n_main, w_gate, S)

        def gate_row(p):
            grp = jnp.concatenate([p.reshape(-1), jnp.zeros((2 * HB,), F32)])
            return jnp.pad(jnp.tile(grp, 4), (0, LANES - 16 * HB)).reshape(1, LANES)

        pack = _gating(gat, gate_row(A_log[l]), gate_row(dt_bias[l]), HB)

        oa = _attention(proj, jnp.tile(qn_g[l], 2).reshape(1, LANES), jnp.tile(kn_g[l], 2).reshape(1, LANES),
                        bias_prof, B, S, HA, hda)
        cw4 = jnp.transpose(conv_w[l].reshape(conv_w.shape[1], 3, HB, hdb), (2, 1, 0, 3))
        ob = _gdn(proj, cw4, pack.reshape(B, S, LANES), onorm_g[l].reshape(1, LANES), B, S, HB, 3 * DA // LANES)

        wr = _hilo_weights(jnp.pad(jnp.concatenate([w_re[l], w_rg[l]], axis=1), ((0, 0), (0, LANES - NE - NG))))
        br = jnp.pad(jnp.concatenate([b_re[l], b_rg[l]]), (0, LANES - NE - NG)).reshape(1, LANES)
        x1, hpk, logits = _outproj(oa, ob, x2, gt1, w_out[l].astype(BF16), norm2_g[l].reshape(1, D),
                                   sc2, sh2, wr, br, S)
        route, info = _route(logits, NE, NG)
        dest = route[:, 4:4 + TOP_K].astype(I32).reshape(TOP_K * T)
        we, ws, wn, wt = _moe_schedule(info, NE, p_rows)
        xs = _dispatch(dest, hpk, p_rows)
        ypk = _moe(we, ws, wn, wt, xs, w1[l], w3[l], w2[l])
        x2 = _combine(dest, x1, route, gt2, ypk, S)
    return x2.reshape(B, S, D)
```

```python
import functools

import numpy as np
import jax
import jax.numpy as jnp
from jax import lax
from jax.experimental import pallas as pl
from jax.experimental.pallas import tpu as pltpu

F32 = jnp.float32
BF16 = jnp.bfloat16
I32 = jnp.int32
U32 = jnp.uint32
HIGHEST = lax.Precision.HIGHEST

EPS = 1e-6
NEG = -1e30
DILATED_BRANCHES = ((128, 1), (512, 4), (2048, 16))
REL_MAX_DIST = 1024
CHUNK = 64
TOP_K = 2

LANES = 128
SUBLANES = 8
MIB = 1 << 20

ATT_QB = 128
ATT_KW = 256
ATT_UNROLL = 8
ATT_PROF_W = 512
GDN_HP = 2
GDN_PREP_UNROLL = 4
GDN_SEG = 512
MOE_RB = 128
MOE_TALL = 512
MOE_RMAX = 512
MOE_WSLOTS = 3
MOE_CW = 512
ROW_TILE = 256
DMA_UNROLL = 8


def _cparams(sem, vmem_mib):
    return pltpu.CompilerParams(dimension_semantics=sem, vmem_limit_bytes=vmem_mib * MIB)


def _dot(a, b, **kw):
    return jnp.dot(a, b, preferred_element_type=F32, **kw)


def _dot_nt(a, b):
    return lax.dot_general(a, b, (((1,), (1,)), ((), ())), preferred_element_type=F32)


def _pick_tile(n, prefs):
    for t in prefs:
        if n % t == 0:
            return t
    return n


def _pack_halves(x):
    half = x.shape[1] // 2
    lo = lax.bitcast_convert_type(x[:, :half].astype(BF16).astype(F32), U32)
    hi = lax.bitcast_convert_type(x[:, half:].astype(BF16).astype(F32), U32)
    return lax.shift_right_logical(lo, jnp.uint32(16)) | (hi & jnp.uint32(0xFFFF0000))


def _silu(x):
    h = 0.5 * x
    return h + h * jnp.tanh(h)


def _hilo_weights(w):
    hi = w.astype(BF16)
    lo = (w - hi.astype(F32)).astype(BF16)
    return jnp.concatenate([hi, lo], axis=1)


def _dot_hilo(x, w2):
    m, n = x.shape[0], w2.shape[1] // 2
    hi = x.astype(BF16)
    lo = (x - hi.astype(F32)).astype(BF16)
    r = _dot(jnp.concatenate([hi, lo], axis=0), w2)
    return r[:m, :n] + (r[:m, n:] + r[m:, :n])


def _unpack_halves(p):
    lo = lax.bitcast_convert_type(lax.shift_left(p, jnp.uint32(16)), F32)
    hi = lax.bitcast_convert_type(p & jnp.uint32(0xFFFF0000), F32)
    return lo, hi


def _ada_kernel(c_ref, w_ref, b_ref, o_ref):
    c = c_ref[...]
    s = _silu(c).astype(BF16)
    o_ref[...] = _dot(s, w_ref[...].astype(BF16)) + b_ref[...]


def _ada(c, w_ada, b_ada):
    B, D = c.shape
    N = w_ada.shape[1]
    tn = _pick_tile(N, (1024, 512, 256, 128))
    return pl.pallas_call(
        _ada_kernel,
        grid=(N // tn,),
        in_specs=[pl.BlockSpec((B, D), lambda j: (0, 0)),
                  pl.BlockSpec((D, tn), lambda j: (0, j)),
                  pl.BlockSpec((1, tn), lambda j: (0, j))],
        out_specs=pl.BlockSpec((B, tn), lambda j: (0, j)),
        out_shape=jax.ShapeDtypeStruct((B, N), F32),
        compiler_params=_cparams(("arbitrary",), 40),
    )(c, w_ada, b_ada.reshape(1, N))


def _inproj_kernel(x_ref, g_ref, sc_ref, sh_ref, w_ref, wg_ref, o_ref, og_ref, h_scr):
    @pl.when(pl.program_id(1) == 0)
    def _():
        x = x_ref[...]
        y = x * lax.rsqrt(jnp.mean(x * x, axis=-1, keepdims=True) + EPS) * g_ref[...]
        h = y * (1.0 + sc_ref[...]) + sh_ref[...]
        h_scr[...] = h.astype(BF16)
        og_ref[...] = _dot_hilo(h, wg_ref[...])

    o_ref[...] = _dot_nt(h_scr[...], w_ref[...].astype(BF16)).astype(o_ref.dtype)


def _inproj(x2, g, sc, sh, w_all, n_main, w_gate, S):
    T, D = x2.shape
    NM = n_main
    tm = _pick_tile(S, (1024, 512, 256, 128))
    tn = _pick_tile(NM, (1024, 512, 256, 128))
    per_b = S // tm
    return pl.pallas_call(
        _inproj_kernel,
        grid=(T // tm, NM // tn),
        in_specs=[pl.BlockSpec((tm, D), lambda i, j: (i, 0)),
                  pl.BlockSpec((1, D), lambda i, j: (0, 0)),
                  pl.BlockSpec((None, 1, D), lambda i, j: (i // per_b, 0, 0)),
                  pl.BlockSpec((None, 1, D), lambda i, j: (i // per_b, 0, 0)),
                  pl.BlockSpec((tn, D), lambda i, j: (j, 0)),
                  pl.BlockSpec((D, 2 * LANES), lambda i, j: (0, 0))],
        out_specs=[pl.BlockSpec((tm, tn), lambda i, j: (i, j)),
                   pl.BlockSpec((tm, LANES), lambda i, j: (i, 0))],
        out_shape=[jax.ShapeDtypeStruct((T, NM), BF16), jax.ShapeDtypeStruct((T, LANES), F32)],
        scratch_shapes=[pltpu.VMEM((tm, D), BF16)],
        compiler_params=_cparams(("arbitrary", "arbitrary"), 56),
    )(x2, g, sc, sh, w_all, w_gate)


def _gating_kernel(gat_ref, a_ref, dt_ref, o_ref, *, hb):
    gat = gat_ref[...]
    tm = gat.shape[0]
    gw = 4 * hb
    g = -jnp.exp(a_ref[...]) * jax.nn.softplus(gat + dt_ref[...])
    beta = jax.nn.sigmoid(gat)
    r = lax.broadcasted_iota(I32, (tm, tm), 0)
    c = lax.broadcasted_iota(I32, (tm, tm), 1)
    same = (r // CHUNK) == (c // CHUNK)
    pre = _dot(jnp.where(same & (c <= r), 1.0, 0.0), g, precision=HIGHEST)
    suf = _dot(jnp.where(same & (c >= r), 1.0, 0.0), g, precision=HIGHEST)
    lane = lax.broadcasted_iota(I32, gat.shape, 1)
    o_ref[...] = jnp.where(lane // gw == 0, jnp.where(lane % gw < hb, pre, suf), beta)


def _gating(gat, a_row, dt_row, hb):
    T = gat.shape[0]
    tm = ROW_TILE
    return pl.pallas_call(
        functools.partial(_gating_kernel, hb=hb),
        grid=(T // tm,),
        in_specs=[pl.BlockSpec((tm, LANES), lambda i: (i, 0)),
                  pl.BlockSpec((1, LANES), lambda i: (0, 0)),
                  pl.BlockSpec((1, LANES), lambda i: (0, 0))],
        out_specs=pl.BlockSpec((tm, LANES), lambda i: (i, 0)),
        out_shape=jax.ShapeDtypeStruct((T, LANES), F32),
        compiler_params=_cparams(("arbitrary",), 16),
    )(gat, a_row, dt_row)


def _t5_bucket(rel, n_buckets):
    half = n_buckets // 2
    max_exact = half // 2
    n = np.abs(rel)
    large = max_exact + (np.log(np.maximum(n, 1) / max_exact) / np.log(REL_MAX_DIST / max_exact)
                         * (half - max_exact)).astype(np.int32)
    large = np.minimum(large, half - 1)
    return (np.where(rel > 0, half, 0) + np.where(n < max_exact, n, large)).astype(np.int32)


def _attn_plan(S):
    plan, base = [], 0
    for window, dil in DILATED_BRANCHES:
        n = window // (2 * dil)
        L = S // dil
        assert L % ATT_QB == 0 and n * 2 == ATT_QB
        kw = min(ATT_KW, L)
        nbq = L // ATT_QB
        nvar = 1 if nbq == 1 else 3
        plan.append((dil, L, nbq, kw, base, nvar, n))
        base += nvar
    return tuple(plan), base


def _attn_bias_profiles(rel_bias, S):
    plan, nvar_total = _attn_plan(S)
    nbuckets, H = rel_bias.shape
    u = np.arange(ATT_PROF_W) - ATT_QB
    onehots, bands = [], []
    for dil, L, nbq, kw, base, nvar, n in plan:
        offs = [0] if nvar == 1 else [0, -n, -(kw - ATT_QB)]
        for off in offs:
            rel = off + u
            onehots.append(np.eye(nbuckets, dtype=np.float32)[_t5_bucket(rel * dil, nbuckets)])
            bands.append(np.abs(rel) <= n)
    onehot = jnp.asarray(np.stack(onehots))
    band = jnp.asarray(np.stack(bands))
    prof = jnp.einsum('vwn,nh->hvw', onehot, rel_bias.astype(F32), precision=HIGHEST)
    prof = jnp.where(band[None], prof, NEG)
    prof = prof.reshape(H // 2, 2, nvar_total, ATT_PROF_W)
    return jnp.transpose(prof, (0, 2, 1, 3)).reshape(H // 2, 2 * nvar_total, ATT_PROF_W)


def _attn_kernel(q_ref, k_ref, v_ref, qg_ref, kg_ref, prof_ref, o_ref,
                 qn_scr, kn_scr, v_scr, ob_scr, mb_scr, db_scr, bias_ref, *, plan, hd):
    S = q_ref.shape[0]
    lane = lax.broadcasted_iota(I32, (1, LANES), 1)
    left = lane < hd

    @pl.when(pl.program_id(1) == 0)
    def _():
        for row in range(prof_ref.shape[0]):
            rep = jnp.broadcast_to(prof_ref[row:row + 1, :], (ATT_QB, ATT_PROF_W))
            skew = pltpu.roll(rep, 0, 1, stride=1, stride_axis=0)
            bias_ref[row // 2, row % 2] = skew[:, ATT_QB:ATT_QB + ATT_KW]

    same_head = (lax.broadcasted_iota(I32, (2 * LANES, LANES), 0) % LANES) // hd == \
        lax.broadcasted_iota(I32, (2 * LANES, LANES), 1) // hd
    avg = jnp.where(same_head, 1.0 / hd, 0.0).astype(BF16)

    def headnorm(x, g):
        x2 = x * x
        hi = x2.astype(BF16)
        lo = (x2 - hi.astype(F32)).astype(BF16)
        ms = _dot(jnp.concatenate([hi, lo], axis=1), avg)
        return x * lax.rsqrt(ms + EPS) * g

    qn_scr[...] = headnorm(q_ref[...].astype(F32), qg_ref[...]) * (hd ** -0.5)
    kn_scr[...] = headnorm(k_ref[...].astype(F32), kg_ref[...])
    v_scr[...] = v_ref[...].astype(F32)

    for bi, (dil, L, nbq, kw, base, nvar, n) in enumerate(plan):
        ones = jnp.ones((kw, LANES), BF16)

        def body(t, carry, dil=dil, L=L, nbq=nbq, kw=kw, base=base, nvar=nvar, n=n, bi=bi, ones=ones):
            blocks = []
            for uu in range(ATT_UNROLL):
                idx = t * ATT_UNROLL + uu
                r = idx // nbq
                i = idx % nbq
                q0 = i * ATT_QB
                k0 = jnp.clip(q0 - n, 0, L - kw)
                var = base if nvar == 1 else base + jnp.where(i > 0, 1, 0) + jnp.where(i == nbq - 1, 1, 0)
                if dil == 1:
                    qrows = pl.ds(pl.multiple_of(q0, ATT_QB), ATT_QB)
                    krows = pl.ds(pl.multiple_of(k0, CHUNK), kw)
                else:
                    qrows = pl.ds(r + q0 * dil, ATT_QB, stride=dil)
                    krows = pl.ds(r + k0 * dil, kw, stride=dil)
                qb = qn_scr[qrows, :]
                q2 = jnp.concatenate([jnp.where(left, qb, 0.0), jnp.where(left, 0.0, qb)], axis=0).astype(BF16)
                blocks.append((qrows, krows, var, q2))
            def score(group):
                return [_dot_nt(q2, kn_scr[krows, :].astype(BF16)) for qrows, krows, var, q2 in group]

            def softmax(group, scores):
                probs, maxes = [], []
                for (qrows, krows, var, q2), s in zip(group, scores):
                    s = s + jnp.concatenate([bias_ref[var, 0][:, :kw], bias_ref[var, 1][:, :kw]], axis=0)
                    m = jnp.max(s, axis=-1, keepdims=True)
                    probs.append(jnp.exp(s - m).astype(BF16))
                    maxes.append(m)
                return probs, maxes

            def values(group, probs):
                return [_dot(p, jnp.concatenate([v_scr[krows, :].astype(BF16), ones], axis=1))
                        for (qrows, krows, var, q2), p in zip(group, probs)]

            def finish(group, maxes, outs):
                for (qrows, krows, var, q2), m, od in zip(group, maxes, outs):
                    mb = jnp.broadcast_to(m, (2 * ATT_QB, LANES))
                    ob_scr[bi, qrows, :] = jnp.where(left, od[:ATT_QB, :LANES], od[ATT_QB:, :LANES])
                    mb_scr[bi, qrows, :] = jnp.where(left, mb[:ATT_QB], mb[ATT_QB:])
                    db_scr[bi, qrows, :] = jnp.where(left, od[:ATT_QB, LANES:], od[ATT_QB:, LANES:])

            ga, gb = blocks[:ATT_UNROLL // 2], blocks[ATT_UNROLL // 2:]
            sa = score(ga)
            sb = score(gb)
            pa, ma = softmax(ga, sa)
            oa = values(ga, pa)
            pb, mbx = softmax(gb, sb)
            ob = values(gb, pb)
            finish(ga, ma, oa)
            finish(gb, mbx, ob)
            return carry

        assert (dil * nbq) % ATT_UNROLL == 0
        lax.fori_loop(0, dil * nbq // ATT_UNROLL, body, 0)

    nb = len(plan)
    mx = mb_scr[0]
    for bi in range(1, nb):
        mx = jnp.maximum(mx, mb_scr[bi])
    num = jnp.zeros((S, LANES), F32)
    den = jnp.zeros((S, LANES), F32)
    for bi in range(nb):
        w = jnp.exp(mb_scr[bi] - mx)
        num = num + w * ob_scr[bi]
        den = den + w * db_scr[bi]
    o_ref[...] = (num / den).astype(o_ref.dtype)


def _attention(proj, qg2, kg2, profiles, B, S, HA, hd):
    T = proj.shape[0]
    pairs = HA // 2
    da_blocks = HA * hd // LANES
    plan, nvar = _attn_plan(S)
    return pl.pallas_call(
        functools.partial(_attn_kernel, plan=plan, hd=hd),
        grid=(pairs, B),
        in_specs=[pl.BlockSpec((S, LANES), lambda p, b: (b, p)),
                  pl.BlockSpec((S, LANES), lambda p, b: (b, da_blocks + p)),
                  pl.BlockSpec((S, LANES), lambda p, b: (b, 2 * da_blocks + p)),
                  pl.BlockSpec((1, LANES), lambda p, b: (0, 0)),
                  pl.BlockSpec((1, LANES), lambda p, b: (0, 0)),
                  pl.BlockSpec((None, 2 * nvar, ATT_PROF_W), lambda p, b: (p, 0, 0))],
        out_specs=pl.BlockSpec((S, LANES), lambda p, b: (b, p)),
        out_shape=jax.ShapeDtypeStruct((T, HA * hd), BF16),
        scratch_shapes=[pltpu.VMEM((S, LANES), F32)] * 3 + [pltpu.VMEM((len(plan), S, LANES), F32)] * 3
        + [pltpu.VMEM((nvar, 2, ATT_QB, ATT_KW), F32)],
        compiler_params=_cparams(("arbitrary", "arbitrary"), 40),
    )(proj, proj, proj, qg2, kg2, profiles)


def _gdn_kernel(q_ref, k_ref, v_ref, z_ref, cw_ref, pack_ref, og_ref, o_ref,
                q_scr, k_scr, v_scr, xpad_scr, pk2_scr, sel_scr, u_scr, wq_scr, at_scr, kdt_scr, et_scr, oacc_scr,
                *, hb, hp):
    S = q_ref.shape[0]
    P2 = 2 * CHUNK
    W2 = 2 * LANES
    npair = S // P2
    hg = pl.program_id(1)
    gw = 4 * hb
    dk = LANES

    pad = SUBLANES
    for slot in range(2):
        xpad_scr[slot, 0:pad, :] = jnp.zeros((pad, LANES), F32)
        xpad_scr[slot, pad + S:, :] = jnp.zeros((pad, LANES), F32)

    seg = GDN_SEG if S % GDN_SEG == 0 else S

    def conv_silu_to(src_ref, lanes_j, j, which, dst_scr, l2norm, scale):
        xp = xpad_scr.at[(3 * j + which) % 2]
        for s0 in range(0, S, seg):
            xp[pad + s0:pad + s0 + seg, :] = src_ref[s0:s0 + seg, lanes_j].astype(F32)
        for s0 in range(0, S, seg):
            acc = xp[pad + s0:pad + s0 + seg, :] * cw_ref[j, which, 2:3, :]
            for d in (-2, -1, 1, 2):
                acc = acc + xp[pad + s0 + d:pad + s0 + d + seg, :] * cw_ref[j, which, 2 + d:3 + d, :]
            y = _silu(acc)
            if l2norm:
                y = y * (lax.rsqrt(jnp.sum(y * y, axis=-1, keepdims=True) + EPS) * scale)
            dst_scr[j, s0:s0 + seg, :] = y

    for s0 in range(0, S, seg):
        pk = pack_ref[s0:s0 + seg, :]
        p_hi = pk.astype(BF16)
        p_lo = (pk - p_hi.astype(F32)).astype(BF16)
        pk2_scr[s0:s0 + seg, :] = jnp.concatenate([p_hi, p_lo], axis=1)
    srow = lax.broadcasted_iota(I32, (W2, 4 * LANES), 0) % LANES
    scol = lax.broadcasted_iota(I32, (W2, 4 * LANES), 1) // LANES
    for j in range(hp):
        lanes_j = slice(j * LANES, (j + 1) * LANES)
        conv_silu_to(q_ref, lanes_j, j, 0, q_scr, True, dk ** -0.5)
        conv_silu_to(k_ref, lanes_j, j, 1, k_scr, True, 1.0)
        conv_silu_to(v_ref, lanes_j, j, 2, v_scr, False, 1.0)
        src = hg * hp + j + jnp.where(scol < 2, scol * hb, gw + 2 * hb + (scol - 2) * hb)
        sel_scr[j] = jnp.where(srow == src, 1.0, 0.0).astype(BF16)
        oacc_scr[j] = jnp.zeros((S, LANES), F32)

    r4 = lax.broadcasted_iota(I32, (CHUNK, W2), 0)
    l4 = lax.broadcasted_iota(I32, (CHUNK, W2), 1)
    c4 = l4 % CHUNK
    blk4 = l4 // CHUNK
    lo_half = (l4 % LANES) < CHUNK
    ahead = jnp.where(l4 >= LANES, r4 - c4, c4 - r4)
    incl = ahead <= 0
    strict = ahead < 0
    bd16 = (r4 // 16) == (c4 // 16)

    def squeeze(x):
        return jnp.where(lo_half, x[:CHUNK], x[CHUNK:])

    def unsqueeze(x):
        return jnp.concatenate([jnp.where(lo_half, x, jnp.zeros_like(x)),
                                jnp.where(lo_half, jnp.zeros_like(x), x)], axis=0)

    def mm4(a, b):
        rhs = jnp.concatenate([jnp.where(blk4 == g, b, 0.0) for g in range(4)], axis=0)
        return _dot(a.astype(BF16), rhs.astype(BF16))

    first = lax.broadcasted_iota(I32, (P2, 1), 0) < CHUNK
    zpair = jnp.zeros((P2, LANES), BF16)

    def bdiag(x):
        return jnp.concatenate([jnp.concatenate([x[:, :LANES], zpair], axis=1),
                                jnp.concatenate([zpair, x[:, LANES:]], axis=1)], axis=0)

    U = GDN_PREP_UNROLL if npair % GDN_PREP_UNROLL == 0 else 1

    def prep(t, carry):
        cx = []
        chains = [(t * U + u, j) for u in range(U) for j in range(hp)]
        bcs = [_dot(pk2_scr[pl.ds(pl.multiple_of(m * P2, P2), P2), :], sel_scr[j]) for m, j in chains]
        for (m, j), bc in zip(chains, bcs):
            rows = pl.ds(pl.multiple_of(m * P2, P2), P2)
            kp = k_scr[j, rows, :]
            qp = q_scr[j, rows, :]
            vp = v_scr[j, rows, :]
            gc2 = bc[:, 0:W2]
            beta2 = bc[:, W2:2 * W2]
            gcf, gcb = gc2[:, :LANES], gc2[:, LANES:]
            tot2 = jnp.concatenate([jnp.where(first, gcf[CHUNK - 1:CHUNK], gcf[P2 - 1:P2]),
                                    jnp.where(first, gcb[0:1], gcb[CHUNK:CHUNK + 1])], axis=1)
            egc2 = jnp.exp(gc2)
            k2 = jnp.concatenate([kp, kp], axis=1)
            kb2 = k2 * beta2
            vb2 = jnp.concatenate([vp, vp], axis=1) * beta2
            kbe2 = kb2 * egc2
            cx.append(dict(
                m=m, j=j, rows=rows, vb2=vb2, kbe2=kbe2,
                qeb=(jnp.concatenate([qp, qp], axis=1) * egc2).astype(BF16),
                kd2=k2 * jnp.exp(tot2 - gc2),
                et=jnp.exp(tot2),
                dec=jnp.exp(jnp.where(
                    incl, squeeze(gc2) - squeeze(jnp.concatenate([gcf.T, gcb.T], axis=1)), -jnp.inf)),
                stk=jnp.concatenate([kb2[:, :LANES], kb2[:, LANES:], qp], axis=0).astype(BF16),
                kpb=kp.astype(BF16)))

        g3s = [_dot_nt(c['stk'], c['kpb']) for c in cx]
        for c, g3 in zip(cx, g3s):
            lm = jnp.where(strict, squeeze(jnp.concatenate([g3[:P2], g3[P2:2 * P2]], axis=1)) * c['dec'], 0.0)
            attn = squeeze(jnp.concatenate([g3[2 * P2:], g3[2 * P2:]], axis=1)) * c['dec']
            c['attn2'] = unsqueeze(attn).astype(BF16)
            c['lbd'] = jnp.where(bd16, lm, 0.0)
            c['loff'] = lm - c['lbd']
        nn = [-c['lbd'] for c in cx]
        pw = [mm4(c['lbd'], c['lbd']) for c in cx]
        for rnd in range(3):
            prod = [mm4(a, p) for a, p in zip(nn, pw)]
            nxt = [mm4(p, p) for p in pw] if rnd < 2 else pw
            nn = [a + p + q for a, p, q in zip(nn, pw, prod)]
            pw = nxt
        mo = [c['loff'] + x for c, x in zip(cx, [mm4(a, c['loff']) for a, c in zip(nn, cx)])]
        m2 = [mm4(x, x) for x in mo]
        mn = [mm4(x, a) for x, a in zip(mo, nn)]
        xo = [a - x - y for a, x, y in zip(nn, mo, mn)]
        mx = [mm4(a, b) for a, b in zip(m2, xo)]
        toff = [a + b + q for a, b, q in zip(xo, m2, mx)]
        z4 = jnp.zeros((CHUNK, W2), BF16)
        tws = []
        for c, tf in zip(cx, toff):
            vbb, kbb = c['vb2'].astype(BF16), c['kbe2'].astype(BF16)
            blocks = []
            for g in range(4):
                rr = slice((g % 2) * CHUNK, (g % 2 + 1) * CHUNK)
                ll = slice((g // 2) * LANES, (g // 2 + 1) * LANES)
                blocks.append(jnp.concatenate([z4] * g + [vbb[rr, ll], kbb[rr, ll]] + [z4] * (3 - g), axis=1))
            tws.append(_dot(tf.astype(BF16), jnp.concatenate(blocks, axis=0)))
        for c, tw in zip(cx, tws):
            m, j, rows = c['m'], c['j'], c['rows']
            def pair_layout(off):
                piece = lambda g: tw[:, 2 * g * LANES + off:2 * g * LANES + off + LANES]
                return jnp.concatenate([jnp.concatenate([piece(0), piece(2)], axis=1),
                                        jnp.concatenate([piece(1), piece(3)], axis=1)], axis=0)

            u2 = c['vb2'] + pair_layout(0)
            w2 = (c['kbe2'] + pair_layout(LANES)).astype(BF16)
            qeb = c['qeb']
            u_scr[j, rows, :] = u2
            wq_scr[j, pl.ds(pl.multiple_of(m * 2 * P2, 2 * P2), 2 * P2), :] = jnp.concatenate(
                [w2[:CHUNK], qeb[:CHUNK], w2[CHUNK:], qeb[CHUNK:]], axis=0)
            at_scr[j, rows, :] = c['attn2']
            kdt_scr[j, 0, :, rows] = c['kd2'][:, :LANES].T.astype(BF16)
            kdt_scr[j, 1, :, rows] = c['kd2'][:, LANES:].T.astype(BF16)
            et_scr[j, pl.ds(pl.multiple_of(m * 2 * SUBLANES, 2 * SUBLANES), 2 * SUBLANES), :] = jnp.concatenate(
                [c['et'][:SUBLANES], c['et'][CHUNK:CHUNK + SUBLANES]], axis=0)
        return carry

    lax.fori_loop(0, npair // U, prep, 0)

    zc = jnp.zeros((CHUNK, LANES), F32)
    zp = jnp.zeros((P2, LANES), F32)

    def place(v, cpos):
        return jnp.concatenate([v, zc] if cpos == 0 else [zc, v], axis=0)

    def scan(m, states):
        pf = m
        pb = npair - 1 - m
        rows_f = pl.ds(pl.multiple_of(pf * P2, P2), P2)
        rows_b = pl.ds(pl.multiple_of(pb * P2, P2), P2)
        hx = []
        for j in range(hp):
            hx.append(dict(
                u_f=u_scr[j, rows_f, :LANES], u_b=u_scr[j, rows_b, LANES:],
                at_f=at_scr[j, rows_f, :LANES], at_b=at_scr[j, rows_b, LANES:],
                kdt=jnp.concatenate([kdt_scr[j, 0, :, rows_f], kdt_scr[j, 1, :, rows_b]], axis=1),
                wq_f=wq_scr[j, pl.ds(pl.multiple_of(pf * 2 * P2, 2 * P2), 2 * P2), :LANES],
                wq_b=wq_scr[j, pl.ds(pl.multiple_of(pb * 2 * P2, 2 * P2), 2 * P2), LANES:],
                et_f=et_scr[j, pl.ds(pl.multiple_of(pf * 2 * SUBLANES, 2 * SUBLANES), 2 * SUBLANES), :LANES],
                et_b=et_scr[j, pl.ds(pl.multiple_of(pb * 2 * SUBLANES, 2 * SUBLANES), 2 * SUBLANES), LANES:]))
        sts = list(states)
        for step in range(2):
            cf, cb = step, 1 - step
            rrs = [_dot(jnp.concatenate([c['wq_f'][cf * P2:(cf + 1) * P2], c['wq_b'][cb * P2:(cb + 1) * P2]], axis=1),
                        bdiag(st.astype(BF16))) for c, st in zip(hx, sts)]
            ress = []
            for c, rr in zip(hx, rrs):
                u2 = jnp.concatenate([c['u_f'][cf * CHUNK:(cf + 1) * CHUNK],
                                      c['u_b'][cb * CHUNK:(cb + 1) * CHUNK]], axis=1)
                v_new = u2 - rr[:CHUNK]
                rhs = jnp.concatenate(
                    [jnp.concatenate([place(v_new[:, :LANES], cf), zp], axis=1),
                     jnp.concatenate([zp, place(v_new[:, LANES:], cb)], axis=1)], axis=0).astype(BF16)
                lhs = jnp.concatenate(
                    [jnp.concatenate([c['at_f'][cf * CHUNK:(cf + 1) * CHUNK],
                                      c['at_b'][cb * CHUNK:(cb + 1) * CHUNK]], axis=1),
                     c['kdt']], axis=0)
                ress.append(_dot(lhs, rhs))
            for j, (c, rr, res) in enumerate(zip(hx, rrs, ress)):
                o2 = rr[CHUNK:] + res[:CHUNK]
                et2 = jnp.concatenate([c['et_f'][cf * SUBLANES:cf * SUBLANES + 1],
                                       c['et_b'][cb * SUBLANES:cb * SUBLANES + 1]], axis=1)
                sts[j] = sts[j] * et2 + res[CHUNK:]
                of_rows = pl.ds(pl.multiple_of(pf * P2 + cf * CHUNK, CHUNK), CHUNK)
                ob_rows = pl.ds(pl.multiple_of(pb * P2 + cb * CHUNK, CHUNK), CHUNK)
                oacc_scr[j, of_rows, :] = oacc_scr[j, of_rows, :] + o2[:, :LANES]
                oacc_scr[j, ob_rows, :] = oacc_scr[j, ob_rows, :] + o2[:, LANES:]
        return tuple(sts)

    s0 = jnp.zeros((dk, W2), F32)
    lax.fori_loop(0, npair, scan, (s0,) * hp)

    for j in range(hp):
        lanes_j = slice(j * LANES, (j + 1) * LANES)
        for s0 in range(0, S, seg):
            o = oacc_scr[j, s0:s0 + seg, :]
            y = o * lax.rsqrt(jnp.mean(o * o, axis=-1, keepdims=True) + EPS) * og_ref[...]
            z = z_ref[s0:s0 + seg, lanes_j].astype(F32)
            o_ref[s0:s0 + seg, lanes_j] = (y * _silu(z)).astype(o_ref.dtype)


def _gdn(proj, cw4, pack3, onorm_g, B, S, HB, base_blk):
    T = proj.shape[0]
    hp = GDN_HP if (HB % GDN_HP == 0 and base_blk % GDN_HP == 0) else 1
    wblk = hp * LANES
    npair = S // (2 * CHUNK)

    def col(k):
        off = (base_blk + k * HB) // hp
        return lambda b, h: (b, off + h)

    return pl.pallas_call(
        functools.partial(_gdn_kernel, hb=HB, hp=hp),
        grid=(B, HB // hp),
        in_specs=[pl.BlockSpec((S, wblk), col(0)),
                  pl.BlockSpec((S, wblk), col(1)),
                  pl.BlockSpec((S, wblk), col(2)),
                  pl.BlockSpec((S, wblk), col(3)),
                  pl.BlockSpec((hp, 3, cw4.shape[2], LANES), lambda b, h: (h, 0, 0, 0)),
                  pl.BlockSpec((None, S, LANES), lambda b, h: (b, 0, 0)),
                  pl.BlockSpec((1, LANES), lambda b, h: (0, 0))],
        out_specs=pl.BlockSpec((S, wblk), lambda b, h: (b, h)),
        out_shape=jax.ShapeDtypeStruct((T, HB * LANES), BF16),
        scratch_shapes=[pltpu.VMEM((hp, S, LANES), F32)] * 3
        + [pltpu.VMEM((2, S + 2 * SUBLANES, LANES), F32),
           pltpu.VMEM((S, 2 * LANES), BF16),
           pltpu.VMEM((hp, 2 * LANES, 4 * LANES), BF16),
           pltpu.VMEM((hp, S, 2 * LANES), F32),
           pltpu.VMEM((hp, 2 * S, 2 * LANES), BF16),
           pltpu.VMEM((hp, S, 2 * LANES), BF16),
           pltpu.VMEM((hp, 2, LANES, S), BF16),
           pltpu.VMEM((hp, npair * 2 * SUBLANES, 2 * LANES), F32),
           pltpu.VMEM((hp, S, LANES), F32)],
        compiler_params=_cparams(("arbitrary", "arbitrary"), 56),
    )(proj, proj, proj, proj, cw4, pack3, onorm_g)


def _outproj_kernel(oa_ref, ob_ref, x_ref, gt_ref, w_ref, g_ref, sc_ref, sh_ref, wr_ref, br_ref,
                    x1_ref, hp_ref, lg_ref, h_scr):
    da = oa_ref.shape[1]
    y = _dot(oa_ref[...], w_ref[:da, :]) + _dot(ob_ref[...], w_ref[da:, :])
    x1 = x_ref[...] + gt_ref[...] * y
    x1_ref[...] = x1
    hn = x1 * lax.rsqrt(jnp.mean(x1 * x1, axis=-1, keepdims=True) + EPS) * g_ref[...]
    h = hn * (1.0 + sc_ref[...]) + sh_ref[...]
    hp_ref[...] = _pack_halves(h)
    h_scr[...] = h

    @pl.when(pl.program_id(0) < pl.num_programs(0))
    def _():
        lg_ref[...] = _dot_hilo(h_scr[...], wr_ref[...]) + br_ref[...]


def _outproj(oa, ob, x2, gt1, w_out_b, g2, sc2, sh2, wr, br, S):
    T, D = x2.shape
    tm = ROW_TILE
    per_b = S // tm
    bmap = lambda i: (i // per_b, 0, 0)
    return pl.pallas_call(
        _outproj_kernel,
        grid=(T // tm,),
        in_specs=[pl.BlockSpec((tm, oa.shape[1]), lambda i: (i, 0)),
                  pl.BlockSpec((tm, ob.shape[1]), lambda i: (i, 0)),
                  pl.BlockSpec((tm, D), lambda i: (i, 0)),
                  pl.BlockSpec((None, 1, D), bmap),
                  pl.BlockSpec((D, D), lambda i: (0, 0)),
                  pl.BlockSpec((1, D), lambda i: (0, 0)),
                  pl.BlockSpec((None, 1, D), bmap),
                  pl.BlockSpec((None, 1, D), bmap),
                  pl.BlockSpec((D, 2 * LANES), lambda i: (0, 0)),
                  pl.BlockSpec((1, LANES), lambda i: (0, 0))],
        out_specs=[pl.BlockSpec((tm, D), lambda i: (i, 0)),
                   pl.BlockSpec((tm, D // 2), lambda i: (i, 0)),
                   pl.BlockSpec((tm, LANES), lambda i: (i, 0))],
        out_shape=[jax.ShapeDtypeStruct((T, D), F32), jax.ShapeDtypeStruct((T, D // 2), U32),
                   jax.ShapeDtypeStruct((T, LANES), F32)],
        scratch_shapes=[pltpu.VMEM((tm, D), F32)],
        compiler_params=_cparams(("arbitrary",), 48),
    )(oa, ob, x2, gt1, w_out_b, g2, sc2, sh2, wr, br)


def _route_kernel(lg_ref, o_ref, info_ref, run_scr, *, ne, ng):
    ph = pl.program_id(0)
    i = pl.program_id(1)

    @pl.when((ph == 0) & (i == 0))
    def _():
        run_scr[...] = jnp.zeros_like(run_scr)

    @pl.when((ph == 1) & (i == 0))
    def _():
        cnt = run_scr[...]
        padded = jnp.ceil(cnt * (1.0 / MOE_RB)) * MOE_RB
        k = lax.broadcasted_iota(I32, (LANES, LANES), 0)
        e = lax.broadcasted_iota(I32, (LANES, LANES), 1)
        start = _dot(padded, jnp.where(k < e, 1.0, 0.0), precision=HIGHEST)
        rowi = lax.broadcasted_iota(I32, cnt.shape, 0)
        info_ref[...] = jnp.where(rowi == 0, cnt, jnp.where(rowi == 1, padded, start))
        run_scr[...] = start

    lg = lg_ref[...]
    tm = lg.shape[0]
    epg = ne // ng
    lane_i = lax.broadcasted_iota(I32, lg.shape, 1)
    lane = lane_i.astype(F32)
    big = float(2 * LANES)
    is_g = (lane_i >= ne) & (lane_i < ne + ng)
    gl = jnp.where(is_g, lg, -jnp.inf)
    gmax = jnp.max(gl, axis=-1, keepdims=True)
    gidx = jnp.min(jnp.where(gl == gmax, lane, big), axis=-1, keepdims=True) - ne
    psel = 1.0 / jnp.sum(jnp.where(is_g, jnp.exp(gl - gmax), 0.0), axis=-1, keepdims=True)
    in_grp = (lane_i // epg).astype(F32) == gidx
    el = jnp.where(in_grp & (lane_i < ne), lg, -jnp.inf)
    m1 = jnp.max(el, axis=-1, keepdims=True)
    i1 = jnp.min(jnp.where(el == m1, lane, big), axis=-1, keepdims=True)
    el2 = jnp.where(lane == i1, -jnp.inf, el)
    m2 = jnp.max(el2, axis=-1, keepdims=True)
    i2 = jnp.min(jnp.where(el2 == m2, lane, big), axis=-1, keepdims=True)
    e21 = jnp.exp(m2 - m1)
    g1 = psel / (1.0 + e21)
    g2 = psel * e21 / (1.0 + e21)
    o1 = jnp.where(lane == i1, 1.0, 0.0)
    o2 = jnp.where(lane == i2, 1.0, 0.0)
    cnt = o1 + o2
    r = lax.broadcasted_iota(I32, (tm, tm), 0)
    c = lax.broadcasted_iota(I32, (tm, tm), 1)
    before = _dot(jnp.where(c < r, 1.0, 0.0).astype(BF16), cnt.astype(BF16)) + run_scr[0:1, :]
    d1 = jnp.sum(o1 * before, axis=-1, keepdims=True)
    d2 = jnp.sum(o2 * before, axis=-1, keepdims=True)
    run_scr[...] = run_scr[...] + jnp.sum(cnt, axis=0, keepdims=True)

    @pl.when(ph == 1)
    def _():
        out = jnp.zeros(lg.shape, F32)
        for j, val in enumerate((i1, i2, g1, g2, d1, d2)):
            out = jnp.where(lane_i == j, val, out)
        o_ref[...] = out


def _route(logits, ne, ng):
    T = logits.shape[0]
    tm = _pick_tile(T, (1024, 512, 256))
    return pl.pallas_call(
        functools.partial(_route_kernel, ne=ne, ng=ng),
        grid=(2, T // tm),
        in_specs=[pl.BlockSpec((tm, LANES), lambda p, i: (i, 0))],
        out_specs=[pl.BlockSpec((tm, LANES), lambda p, i: (i * p, 0)),
                   pl.BlockSpec((SUBLANES, LANES), lambda p, i: (0, 0))],
        out_shape=[jax.ShapeDtypeStruct((T, LANES), F32), jax.ShapeDtypeStruct((SUBLANES, LANES), F32)],
        scratch_shapes=[pltpu.VMEM((SUBLANES, LANES), F32)],
        compiler_params=_cparams(("arbitrary", "arbitrary"), 16),
    )(logits)


def _dispatch_kernel(dest_ref, h_hbm, xs_in, xs_hbm, sem):
    del xs_in
    tm = ROW_TILE
    base = pl.program_id(0) * tm

    def copy(tok, a):
        return pltpu.make_async_copy(h_hbm.at[pl.ds(tok, 1)], xs_hbm.at[pl.ds(dest_ref[a], 1)], sem)

    def issue(t, carry):
        for k in range(TOP_K):
            copy(base + t, (base + t) * TOP_K + k).start()
        return carry

    lax.fori_loop(0, tm, issue, 0, unroll=DMA_UNROLL)

    for k in range(TOP_K):
        pltpu.make_async_copy(h_hbm.at[pl.ds(0, tm)], xs_hbm.at[pl.ds(0, tm)], sem).wait()


def _dispatch(dest, hpk, p_rows):
    T, dh = hpk.shape
    return pl.pallas_call(
        _dispatch_kernel,
        grid_spec=pltpu.PrefetchScalarGridSpec(
            num_scalar_prefetch=1,
            grid=(T // ROW_TILE,),
            in_specs=[pl.BlockSpec(memory_space=pl.ANY), pl.BlockSpec(memory_space=pl.ANY)],
            out_specs=pl.BlockSpec(memory_space=pl.ANY),
            scratch_shapes=[pltpu.SemaphoreType.DMA(())]),
        out_shape=jax.ShapeDtypeStruct((p_rows, dh), U32),
        input_output_aliases={2: 0},
        compiler_params=_cparams(("arbitrary",), 16),
    )(dest, hpk, jnp.zeros((p_rows, dh), U32))


def _moe_kernel(we_ref, ws_ref, wn_ref, wt_ref, xs_hbm, w1_hbm, w3_hbm, w2_hbm, y_hbm,
                xbuf, xlo, xhi, yacc, ypk, w1buf, w3buf, w2buf, sem_in, sem_out, sem_w):
    w = pl.program_id(0)
    c = pl.program_id(1)
    nw = pl.num_programs(0)
    nc = pl.num_programs(1)
    nrows = wn_ref[w]
    start = ws_ref[w]
    nblk = nrows // MOE_RB
    dh = xbuf.shape[1]
    cw = w1buf.shape[-1]

    g = w * nc + c

    def w_copies(step, do):
        item = step // nc
        chunk = step % nc

        @pl.when((item < nw) & (wn_ref[jnp.minimum(item, nw - 1)] > 0))
        def _():
            e = we_ref[item]
            cols = pl.ds(pl.multiple_of(chunk * cw, cw), cw)
            slot = step % MOE_WSLOTS
            do(pltpu.make_async_copy(w1_hbm.at[e, :, cols], w1buf.at[slot], sem_w.at[slot]))
            do(pltpu.make_async_copy(w3_hbm.at[e, :, cols], w3buf.at[slot], sem_w.at[slot]))
            do(pltpu.make_async_copy(w2_hbm.at[e, cols, :], w2buf.at[slot], sem_w.at[slot]))

    @pl.when(g == 0)
    def _():
        for ahead in range(MOE_WSLOTS - 1):
            w_copies(ahead, lambda cp: cp.start())

    w_copies(g + MOE_WSLOTS - 1, lambda cp: cp.start())
    w_copies(g, lambda cp: cp.wait())
    wslot = g % MOE_WSLOTS
    w1_ref, w3_ref, w2_ref = w1buf.at[wslot], w3buf.at[wslot], w2buf.at[wslot]

    def blk_rows(rb):
        return pl.ds(pl.multiple_of(rb * MOE_RB, MOE_RB), MOE_RB)

    def hbm_rows(item_start, rb):
        return pl.ds(pl.multiple_of(item_start + rb * MOE_RB, MOE_RB), MOE_RB)

    def in_copy(item_start, rb):
        return pltpu.make_async_copy(xs_hbm.at[hbm_rows(item_start, rb)], xbuf.at[blk_rows(rb)], sem_in)

    def out_copy(item_start, rb):
        return pltpu.make_async_copy(ypk.at[blk_rows(rb)], y_hbm.at[hbm_rows(item_start, rb)], sem_out)

    def each_block(n, fn):
        def body(rb, carry):
            fn(rb)
            return carry
        lax.fori_loop(0, n, body, 0)

    @pl.when(c == 0)
    def _load():
        @pl.when(w == 0)
        def _():
            each_block(nblk, lambda rb: in_copy(start, rb).start())

        each_block(nblk, lambda rb: in_copy(start, rb).wait())

        def unpack(rb):
            lo, hi = _unpack_halves(xbuf[blk_rows(rb), :])
            xlo[blk_rows(rb), :] = lo.astype(BF16)
            xhi[blk_rows(rb), :] = hi.astype(BF16)
            yacc[blk_rows(rb), :] = jnp.zeros((MOE_RB, 2 * dh), F32)

        each_block(nblk, unpack)

        @pl.when(w + 1 < nw)
        def _():
            nxt = ws_ref[w + 1]
            each_block(wn_ref[w + 1] // MOE_RB, lambda rb: in_copy(nxt, rb).start())

    @pl.when(nrows > 0)
    def _compute():
        def rows_block(row0, nr):
            rows = pl.ds(row0, nr)
            xl = xlo[rows, :]
            xh = xhi[rows, :]
            h1 = _dot(xl, w1_ref[:dh, :].astype(BF16)) + _dot(xh, w1_ref[dh:, :].astype(BF16))
            h3 = _dot(xl, w3_ref[:dh, :].astype(BF16)) + _dot(xh, w3_ref[dh:, :].astype(BF16))
            hid = (_silu(h1) * h3).astype(BF16)
            yacc[rows, :] = yacc[rows, :] + _dot(hid, w2_ref[...].astype(BF16))

        ntall = nrows // MOE_TALL
        each_block(ntall, lambda i: rows_block(pl.multiple_of(i * MOE_TALL, MOE_TALL), MOE_TALL))

        rem = nrows - ntall * MOE_TALL
        for nr in range(MOE_RB, MOE_TALL, MOE_RB):
            @pl.when(rem == nr)
            def _(nr=nr):
                rows_block(pl.multiple_of(ntall * MOE_TALL, MOE_TALL), nr)

    @pl.when(c == nc - 1)
    def _store():
        @pl.when(w > 0)
        def _():
            prev = ws_ref[w - 1]
            each_block(wn_ref[w - 1] // MOE_RB, lambda rb: out_copy(prev, rb).wait())

        def pack(rb):
            ypk[blk_rows(rb), :] = _pack_halves(yacc[blk_rows(rb), :])

        each_block(nblk, pack)
        each_block(nblk, lambda rb: out_copy(start, rb).start())

        @pl.when(w == nw - 1)
        def _():
            each_block(nblk, lambda rb: out_copy(start, rb).wait())

    @pl.when((w == nw - 1) & (c == nc - 1))
    def _zero_tail():
        ypk[blk_rows(0), :] = jnp.zeros((MOE_RB, dh), U32)
        first = wt_ref[0] // MOE_RB

        def tail_copy(b):
            return pltpu.make_async_copy(ypk.at[blk_rows(0)],
                                         y_hbm.at[pl.ds(pl.multiple_of(b * MOE_RB, MOE_RB), MOE_RB)], sem_out)

        def start(b, carry):
            tail_copy(b).start()
            return carry

        def wait(b, carry):
            tail_copy(b).wait()
            return carry

        lax.fori_loop(first, y_hbm.shape[0] // MOE_RB, start, 0)
        lax.fori_loop(first, y_hbm.shape[0] // MOE_RB, wait, 0)


def _moe(we, ws, wn, wt, xs, w1, w3, w2):
    P, dh = xs.shape
    D = 2 * dh
    NE, _, DE = w1.shape
    cw = min(MOE_CW, DE)
    nc = DE // cw
    nw = we.shape[0]
    return pl.pallas_call(
        _moe_kernel,
        grid_spec=pltpu.PrefetchScalarGridSpec(
            num_scalar_prefetch=4,
            grid=(nw, nc),
            in_specs=[pl.BlockSpec(memory_space=pl.ANY)] * 4,
            out_specs=pl.BlockSpec(memory_space=pl.ANY),
            scratch_shapes=[pltpu.VMEM((MOE_RMAX, dh), U32),
                            pltpu.VMEM((MOE_RMAX, dh), BF16),
                            pltpu.VMEM((MOE_RMAX, dh), BF16),
                            pltpu.VMEM((MOE_RMAX, D), F32),
                            pltpu.VMEM((MOE_RMAX, dh), U32),
                            pltpu.VMEM((MOE_WSLOTS, D, cw), F32),
                            pltpu.VMEM((MOE_WSLOTS, D, cw), F32),
                            pltpu.VMEM((MOE_WSLOTS, cw, D), F32),
                            pltpu.SemaphoreType.DMA(()),
                            pltpu.SemaphoreType.DMA(()),
                            pltpu.SemaphoreType.DMA((MOE_WSLOTS,))]),
        out_shape=jax.ShapeDtypeStruct((P, dh), U32),
        compiler_params=_cparams(("arbitrary", "arbitrary"), 56),
    )(we, ws, wn, wt, xs, w1, w3, w2)


def _moe_schedule(info, NE, p_rows):
    padded = info[1, :NE].astype(I32)
    start_pad = info[2, :NE].astype(I32)
    items = (padded + MOE_RMAX - 1) // MOE_RMAX
    cum_items = jnp.cumsum(items)
    n_items = cum_items[-1]
    nw = (p_rows + NE * (MOE_RMAX - MOE_RB)) // MOE_RMAX
    wi = jnp.arange(nw, dtype=I32)
    valid = wi < n_items
    wi_c = jnp.minimum(wi, jnp.maximum(n_items - 1, 0))
    we = jnp.minimum(jnp.searchsorted(cum_items, wi_c, side='right'), NE - 1).astype(I32)
    local = wi_c - (cum_items[we] - items[we])
    ws = (start_pad[we] + local * MOE_RMAX).astype(I32)
    wn = jnp.where(valid, jnp.clip(padded[we] - local * MOE_RMAX, 0, MOE_RMAX), 0).astype(I32)
    wt = (start_pad[NE - 1] + padded[NE - 1]).reshape(1)
    return we, ws, wn, wt


def _combine_kernel(dest_ref, x1_ref, rt_ref, gt_ref, y_hbm, o_ref, ybuf, sems):
    tm = ROW_TILE
    i = pl.program_id(0)
    n = pl.num_programs(0)
    dh = ybuf.shape[-1]

    def copy(tile, slot, t, k):
        a = (tile * tm + t) * TOP_K + k
        return pltpu.make_async_copy(y_hbm.at[pl.ds(dest_ref[a], 1)], ybuf.at[slot, k, pl.ds(t, 1)], sems.at[slot])

    def issue_tile(tile, slot):
        def body(t, carry):
            for k in range(TOP_K):
                copy(tile, slot, t, k).start()
            return carry
        lax.fori_loop(0, tm, body, 0, unroll=DMA_UNROLL)

    @pl.when(i == 0)
    def _():
        issue_tile(0, 0)

    @pl.when(i + 1 < n)
    def _():
        issue_tile(i + 1, (i + 1) % 2)

    slot = i % 2

    for k in range(TOP_K):
        pltpu.make_async_copy(y_hbm.at[pl.ds(0, tm)], ybuf.at[slot, k], sems.at[slot]).wait()

    rt = rt_ref[...]
    g1 = rt[:, 2:3]
    g2 = rt[:, 3:4]
    lo1, hi1 = _unpack_halves(ybuf[slot, 0])
    lo2, hi2 = _unpack_halves(ybuf[slot, 1])
    o_ref[:, :dh] = x1_ref[:, :dh] + gt_ref[:, :dh] * (g1 * lo1 + g2 * lo2)
    o_ref[:, dh:] = x1_ref[:, dh:] + gt_ref[:, dh:] * (g1 * hi1 + g2 * hi2)


def _combine(dest, x1, route, gt2, ypk, S):
    T, D = x1.shape
    tm = ROW_TILE
    per_b = S // tm
    return pl.pallas_call(
        _combine_kernel,
        grid_spec=pltpu.PrefetchScalarGridSpec(
            num_scalar_prefetch=1,
            grid=(T // tm,),
            in_specs=[pl.BlockSpec((tm, D), lambda i, d: (i, 0)),
                      pl.BlockSpec((tm, LANES), lambda i, d: (i, 0)),
                      pl.BlockSpec((None, 1, D), lambda i, d: (i // per_b, 0, 0)),
                      pl.BlockSpec(memory_space=pl.ANY)],
            out_specs=pl.BlockSpec((tm, D), lambda i, d: (i, 0)),
            scratch_shapes=[pltpu.VMEM((2, TOP_K, tm, D // 2), U32),
                            pltpu.SemaphoreType.DMA((2,))]),
        out_shape=jax.ShapeDtypeStruct((T, D), F32),
        compiler_params=_cparams(("arbitrary",), 32),
    )(dest, x1, route, gt2, ypk)


def kernel(x, c, w_ada, b_ada, norm1_g, norm2_g, w_in, qn_g, kn_g, rel_bias, conv_w, A_log, dt_bias,
           onorm_g, w_out, w_rg, b_rg, w_re, b_re, w1, w3, w2):
    B, S, D = x.shape
    depth = w_ada.shape[0]
    HA, hda = rel_bias.shape[1], qn_g.shape[-1]
    HB, hdb = A_log.shape[-1], onorm_g.shape[-1]
    DA, DB = HA * hda, HB * hdb
    NG, NE = w_rg.shape[-1], w_re.shape[-1]
    T = B * S
    assert hdb == LANES and 2 * hda == LANES and DA + DB == D and 16 * HB <= LANES
    assert NE + NG <= LANES and conv_w.shape[1] == 5 and S % ROW_TILE == 0
    n_main = 3 * DA + 4 * DB
    p_rows = TOP_K * T + NE * MOE_RB

    bias_prof = _attn_bias_profiles(rel_bias, S)
    w_in_t = jnp.swapaxes(w_in, 1, 2)
    x2 = x.reshape(T, D)
    for l in range(depth):
        mod = _ada(c, w_ada[l], b_ada[l]).reshape(B, 6, 1, D)
        sh1, sc1, gt1, sh2, sc2, gt2 = (mod[:, i] for i in range(6))

        w_gate = jnp.tile(w_in_t[l][n_main:, :].T, (1, 4))
        w_gate = _hilo_weights(jnp.pad(w_gate, ((0, 0), (0, LANES - w_gate.shape[1]))))
        proj, gat = _inproj(x2, norm1_g[l].reshape(1, D), sc1, sh1, w_in_t[l], n_main, w_gate, S)

        def gate_row(p):
            grp = jnp.concatenate([p.reshape(-1), jnp.zeros((2 * HB,), F32)])
            return jnp.pad(jnp.tile(grp, 4), (0, LANES - 16 * HB)).reshape(1, LANES)

        pack = _gating(gat, gate_row(A_log[l]), gate_row(dt_bias[l]), HB)

        oa = _attention(proj, jnp.tile(qn_g[l], 2).reshape(1, LANES), jnp.tile(kn_g[l], 2).reshape(1, LANES),
                        bias_prof, B, S, HA, hda)
        cw4 = jnp.transpose(conv_w[l].reshape(conv_w.shape[1], 3, HB, hdb), (2, 1, 0, 3))
        ob = _gdn(proj, cw4, pack.reshape(B, S, LANES), onorm_g[l].reshape(1, LANES), B, S, HB, 3 * DA // LANES)

        wr = _hilo_weights(jnp.pad(jnp.concatenate([w_re[l], w_rg[l]], axis=1), ((0, 0), (0, LANES - NE - NG))))
        br = jnp.pad(jnp.concatenate([b_re[l], b_rg[l]]), (0, LANES - NE - NG)).reshape(1, LANES)
        x1, hpk, logits = _outproj(oa, ob, x2, gt1, w_out[l].astype(BF16), norm2_g[l].reshape(1, D),
                                   sc2, sh2, wr, br, S)
        route, info = _route(logits, NE, NG)
        dest = route[:, 4:4 + TOP_K].astype(I32).reshape(TOP_K * T)
        we, ws, wn, wt = _moe_schedule(info, NE, p_rows)
        xs = _dispatch(dest, hpk, p_rows)
        ypk = _moe(we, ws, wn, wt, xs, w1[l], w3[l], w2[l])
        x2 = _combine(dest, x1, route, gt2, ypk, S)
    return x2.reshape(B, S, D)
```

```python
import functools

import numpy as np
import jax
import jax.numpy as jnp
from jax import lax
from jax.experimental import pallas as pl
from jax.experimental.pallas import tpu as pltpu

F32 = jnp.float32
BF16 = jnp.bfloat16
I32 = jnp.int32
U32 = jnp.uint32
HIGHEST = lax.Precision.HIGHEST

EPS = 1e-6
NEG = -1e30
DILATED_BRANCHES = ((128, 1), (512, 4), (2048, 16))
REL_MAX_DIST = 1024
CHUNK = 64
TOP_K = 2

LANES = 128
SUBLANES = 8
MIB = 1 << 20

ATT_QB = 128
ATT_KW = 256
ATT_UNROLL = 8
ATT_PROF_W = 512
GDN_HP = 2
GDN_PREP_UNROLL = 4
GDN_SEG = 512
MOE_RB = 128
MOE_TALL = 512
MOE_RMAX = 512
MOE_WSLOTS = 3
MOE_CW = 512
ROW_TILE = 256
DMA_UNROLL = 8


def _cparams(sem, vmem_mib):
    return pltpu.CompilerParams(dimension_semantics=sem, vmem_limit_bytes=vmem_mib * MIB)


def _dot(a, b, **kw):
    return jnp.dot(a, b, preferred_element_type=F32, **kw)


def _dot_nt(a, b):
    return lax.dot_general(a, b, (((1,), (1,)), ((), ())), preferred_element_type=F32)


def _pick_tile(n, prefs):
    for t in prefs:
        if n % t == 0:
            return t
    return n


def _pack_halves(x):
    half = x.shape[1] // 2
    lo = lax.bitcast_convert_type(x[:, :half].astype(BF16).astype(F32), U32)
    hi = lax.bitcast_convert_type(x[:, half:].astype(BF16).astype(F32), U32)
    return lax.shift_right_logical(lo, jnp.uint32(16)) | (hi & jnp.uint32(0xFFFF0000))


def _silu(x):
    h = 0.5 * x
    return h + h * jnp.tanh(h)


def _hilo_weights(w):
    hi = w.astype(BF16)
    lo = (w - hi.astype(F32)).astype(BF16)
    return jnp.concatenate([hi, lo], axis=1)


def _dot_hilo(x, w2):
    m, n = x.shape[0], w2.shape[1] // 2
    hi = x.astype(BF16)
    lo = (x - hi.astype(F32)).astype(BF16)
    r = _dot(jnp.concatenate([hi, lo], axis=0), w2)
    return r[:m, :n] + (r[:m, n:] + r[m:, :n])


def _unpack_halves(p):
    lo = lax.bitcast_convert_type(lax.shift_left(p, jnp.uint32(16)), F32)
    hi = lax.bitcast_convert_type(p & jnp.uint32(0xFFFF0000), F32)
    return lo, hi


def _ada_kernel(c_ref, w_ref, b_ref, o_ref):
    c = c_ref[...]
    s = _silu(c).astype(BF16)
    o_ref[...] = _dot(s, w_ref[...].astype(BF16)) + b_ref[...]


def _ada(c, w_ada, b_ada):
    B, D = c.shape
    N = w_ada.shape[1]
    tn = _pick_tile(N, (1024, 512, 256, 128))
    return pl.pallas_call(
        _ada_kernel,
        grid=(N // tn,),
        in_specs=[pl.BlockSpec((B, D), lambda j: (0, 0)),
                  pl.BlockSpec((D, tn), lambda j: (0, j)),
                  pl.BlockSpec((1, tn), lambda j: (0, j))],
        out_specs=pl.BlockSpec((B, tn), lambda j: (0, j)),
        out_shape=jax.ShapeDtypeStruct((B, N), F32),
        compiler_params=_cparams(("arbitrary",), 40),
    )(c, w_ada, b_ada.reshape(1, N))


def _inproj_kernel(x_ref, g_ref, sc_ref, sh_ref, w_ref, wg_ref, o_ref, og_ref, h_scr):
    @pl.when(pl.program_id(1) == 0)
    def _():
        x = x_ref[...]
        y = x * lax.rsqrt(jnp.mean(x * x, axis=-1, keepdims=True) + EPS) * g_ref[...]
        h = y * (1.0 + sc_ref[...]) + sh_ref[...]
        h_scr[...] = h.astype(BF16)
        og_ref[...] = _dot_hilo(h, wg_ref[...])

    o_ref[...] = _dot_nt(h_scr[...], w_ref[...].astype(BF16)).astype(o_ref.dtype)


def _inproj(x2, g, sc, sh, w_all, n_main, w_gate, S):
    T, D = x2.shape
    NM = n_main
    tm = _pick_tile(S, (1024, 512, 256, 128))
    tn = _pick_tile(NM, (1024, 512, 256, 128))
    per_b = S // tm
    return pl.pallas_call(
        _inproj_kernel,
        grid=(T // tm, NM // tn),
        in_specs=[pl.BlockSpec((tm, D), lambda i, j: (i, 0)),
                  pl.BlockSpec((1, D), lambda i, j: (0, 0)),
                  pl.BlockSpec((None, 1, D), lambda i, j: (i // per_b, 0, 0)),
                  pl.BlockSpec((None, 1, D), lambda i, j: (i // per_b, 0, 0)),
                  pl.BlockSpec((tn, D), lambda i, j: (j, 0)),
                  pl.BlockSpec((D, 2 * LANES), lambda i, j: (0, 0))],
        out_specs=[pl.BlockSpec((tm, tn), lambda i, j: (i, j)),
                   pl.BlockSpec((tm, LANES), lambda i, j: (i, 0))],
        out_shape=[jax.ShapeDtypeStruct((T, NM), BF16), jax.ShapeDtypeStruct((T, LANES), F32)],
        scratch_shapes=[pltpu.VMEM((tm, D), BF16)],
        compiler_params=_cparams(("arbitrary", "arbitrary"), 56),
    )(x2, g, sc, sh, w_all, w_gate)


def _gating_kernel(gat_ref, a_ref, dt_ref, o_ref, *, hb):
    gat = gat_ref[...]
    tm = gat.shape[0]
    gw = 4 * hb
    g = -jnp.exp(a_ref[...]) * jax.nn.softplus(gat + dt_ref[...])
    beta = jax.nn.sigmoid(gat)
    r = lax.broadcasted_iota(I32, (tm, tm), 0)
    c = lax.broadcasted_iota(I32, (tm, tm), 1)
    same = (r // CHUNK) == (c // CHUNK)
    pre = _dot(jnp.where(same & (c <= r), 1.0, 0.0), g, precision=HIGHEST)
    suf = _dot(jnp.where(same & (c >= r), 1.0, 0.0), g, precision=HIGHEST)
    lane = lax.broadcasted_iota(I32, gat.shape, 1)
    o_ref[...] = jnp.where(lane // gw == 0, jnp.where(lane % gw < hb, pre, suf), beta)


def _gating(gat, a_row, dt_row, hb):
    T = gat.shape[0]
    tm = ROW_TILE
    return pl.pallas_call(
        functools.partial(_gating_kernel, hb=hb),
        grid=(T // tm,),
        in_specs=[pl.BlockSpec((tm, LANES), lambda i: (i, 0)),
                  pl.BlockSpec((1, LANES), lambda i: (0, 0)),
                  pl.BlockSpec((1, LANES), lambda i: (0, 0))],
        out_specs=pl.BlockSpec((tm, LANES), lambda i: (i, 0)),
        out_shape=jax.ShapeDtypeStruct((T, LANES), F32),
        compiler_params=_cparams(("arbitrary",), 16),
    )(gat, a_row, dt_row)


def _t5_bucket(rel, n_buckets):
    half = n_buckets // 2
    max_exact = half // 2
    n = np.abs(rel)
    large = max_exact + (np.log(np.maximum(n, 1) / max_exact) / np.log(REL_MAX_DIST / max_exact)
                         * (half - max_exact)).astype(np.int32)
    large = np.minimum(large, half - 1)
    return (np.where(rel > 0, half, 0) + np.where(n < max_exact, n, large)).astype(np.int32)


def _attn_plan(S):
    plan, base = [], 0
    for window, dil in DILATED_BRANCHES:
        n = window // (2 * dil)
        L = S // dil
        assert L % ATT_QB == 0 and n * 2 == ATT_QB
        kw = min(ATT_KW, L)
        nbq = L // ATT_QB
        nvar = 1 if nbq == 1 else 3
        plan.append((dil, L, nbq, kw, base, nvar, n))
        base += nvar
    return tuple(plan), base


def _attn_bias_profiles(rel_bias, S):
    plan, nvar_total = _attn_plan(S)
    nbuckets, H = rel_bias.shape
    u = np.arange(ATT_PROF_W) - ATT_QB
    onehots, bands = [], []
    for dil, L, nbq, kw, base, nvar, n in plan:
        offs = [0] if nvar == 1 else [0, -n, -(kw - ATT_QB)]
        for off in offs:
            rel = off + u
            onehots.append(np.eye(nbuckets, dtype=np.float32)[_t5_bucket(rel * dil, nbuckets)])
            bands.append(np.abs(rel) <= n)
    onehot = jnp.asarray(np.stack(onehots))
    band = jnp.asarray(np.stack(bands))
    prof = jnp.einsum('vwn,nh->hvw', onehot, rel_bias.astype(F32), precision=HIGHEST)
    prof = jnp.where(band[None], prof, NEG)
    prof = prof.reshape(H // 2, 2, nvar_total, ATT_PROF_W)
    return jnp.transpose(prof, (0, 2, 1, 3)).reshape(H // 2, 2 * nvar_total, ATT_PROF_W)


def _attn_kernel(q_ref, k_ref, v_ref, qg_ref, kg_ref, prof_ref, o_ref,
                 qn_scr, kn_scr, v_scr, ob_scr, mb_scr, db_scr, bias_ref, *, plan, hd):
    S = q_ref.shape[0]
    lane = lax.broadcasted_iota(I32, (1, LANES), 1)
    left = lane < hd

    @pl.when(pl.program_id(1) == 0)
    def _():
        for row in range(prof_ref.shape[0]):
            rep = jnp.broadcast_to(prof_ref[row:row + 1, :], (ATT_QB, ATT_PROF_W))
            skew = pltpu.roll(rep, 0, 1, stride=1, stride_axis=0)
            bias_ref[row // 2, row % 2] = skew[:, ATT_QB:ATT_QB + ATT_KW]

    same_head = (lax.broadcasted_iota(I32, (2 * LANES, LANES), 0) % LANES) // hd == \
        lax.broadcasted_iota(I32, (2 * LANES, LANES), 1) // hd
    avg = jnp.where(same_head, 1.0 / hd, 0.0).astype(BF16)

    def headnorm(x, g):
        x2 = x * x
        hi = x2.astype(BF16)
        lo = (x2 - hi.astype(F32)).astype(BF16)
        ms = _dot(jnp.concatenate([hi, lo], axis=1), avg)
        return x * lax.rsqrt(ms + EPS) * g

    qn_scr[...] = headnorm(q_ref[...].astype(F32), qg_ref[...]) * (hd ** -0.5)
    kn_scr[...] = headnorm(k_ref[...].astype(F32), kg_ref[...])
    v_scr[...] = v_ref[...].astype(F32)

    for bi, (dil, L, nbq, kw, base, nvar, n) in enumerate(plan):
        ones = jnp.ones((kw, LANES), BF16)

        def body(t, carry, dil=dil, L=L, nbq=nbq, kw=kw, base=base, nvar=nvar, n=n, bi=bi, ones=ones):
            blocks = []
            for uu in range(ATT_UNROLL):
                idx = t * ATT_UNROLL + uu
                r = idx // nbq
                i = idx % nbq
                q0 = i * ATT_QB
                k0 = jnp.clip(q0 - n, 0, L - kw)
                var = base if nvar == 1 else base + jnp.where(i > 0, 1, 0) + jnp.where(i == nbq - 1, 1, 0)
                if dil == 1:
                    qrows = pl.ds(pl.multiple_of(q0, ATT_QB), ATT_QB)
                    krows = pl.ds(pl.multiple_of(k0, CHUNK), kw)
                else:
                    qrows = pl.ds(r + q0 * dil, ATT_QB, stride=dil)
                    krows = pl.ds(r + k0 * dil, kw, stride=dil)
                qb = qn_scr[qrows, :]
                q2 = jnp.concatenate([jnp.where(left, qb, 0.0), jnp.where(left, 0.0, qb)], axis=0).astype(BF16)
                blocks.append((qrows, krows, var, q2))
            def score(group):
                return [_dot_nt(q2, kn_scr[krows, :].astype(BF16)) for qrows, krows, var, q2 in group]

            def softmax(group, scores):
                probs, maxes = [], []
                for (qrows, krows, var, q2), s in zip(group, scores):
                    s = s + jnp.concatenate([bias_ref[var, 0][:, :kw], bias_ref[var, 1][:, :kw]], axis=0)
                    m = jnp.max(s, axis=-1, keepdims=True)
                    probs.append(jnp.exp(s - m).astype(BF16))
                    maxes.append(m)
                return probs, maxes

            def values(group, probs):
                return [_dot(p, jnp.concatenate([v_scr[krows, :].astype(BF16), ones], axis=1))
                        for (qrows, krows, var, q2), p in zip(group, probs)]

            def finish(group, maxes, outs):
                for (qrows, krows, var, q2), m, od in zip(group, maxes, outs):
                    mb = jnp.broadcast_to(m, (2 * ATT_QB, LANES))
                    ob_scr[bi, qrows, :] = jnp.where(left, od[:ATT_QB, :LANES], od[ATT_QB:, :LANES])
                    mb_scr[bi, qrows, :] = jnp.where(left, mb[:ATT_QB], mb[ATT_QB:])
                    db_scr[bi, qrows, :] = jnp.where(left, od[:ATT_QB, LANES:], od[ATT_QB:, LANES:])

            ga, gb = blocks[:ATT_UNROLL // 2], blocks[ATT_UNROLL // 2:]
            sa = score(ga)
            sb = score(gb)
            pa, ma = softmax(ga, sa)
            oa = values(ga, pa)
            pb, mbx = softmax(gb, sb)
            ob = values(gb, pb)
            finish(ga, ma, oa)
            finish(gb, mbx, ob)
            return carry

        assert (dil * nbq) % ATT_UNROLL == 0
        lax.fori_loop(0, dil * nbq // ATT_UNROLL, body, 0)

    nb = len(plan)
    mx = mb_scr[0]
    for bi in range(1, nb):
        mx = jnp.maximum(mx, mb_scr[bi])
    num = jnp.zeros((S, LANES), F32)
    den = jnp.zeros((S, LANES), F32)
    for bi in range(nb):
        w = jnp.exp(mb_scr[bi] - mx)
        num = num + w * ob_scr[bi]
        den = den + w * db_scr[bi]
    o_ref[...] = (num / den).astype(o_ref.dtype)


def _attention(proj, qg2, kg2, profiles, B, S, HA, hd):
    T = proj.shape[0]
    pairs = HA // 2
    da_blocks = HA * hd // LANES
    plan, nvar = _attn_plan(S)
    return pl.pallas_call(
        functools.partial(_attn_kernel, plan=plan, hd=hd),
        grid=(pairs, B),
        in_specs=[pl.BlockSpec((S, LANES), lambda p, b: (b, p)),
                  pl.BlockSpec((S, LANES), lambda p, b: (b, da_blocks + p)),
                  pl.BlockSpec((S, LANES), lambda p, b: (b, 2 * da_blocks + p)),
                  pl.BlockSpec((1, LANES), lambda p, b: (0, 0)),
                  pl.BlockSpec((1, LANES), lambda p, b: (0, 0)),
                  pl.BlockSpec((None, 2 * nvar, ATT_PROF_W), lambda p, b: (p, 0, 0))],
        out_specs=pl.BlockSpec((S, LANES), lambda p, b: (b, p)),
        out_shape=jax.ShapeDtypeStruct((T, HA * hd), BF16),
        scratch_shapes=[pltpu.VMEM((S, LANES), F32)] * 3 + [pltpu.VMEM((len(plan), S, LANES), F32)] * 3
        + [pltpu.VMEM((nvar, 2, ATT_QB, ATT_KW), F32)],
        compiler_params=_cparams(("arbitrary", "arbitrary"), 40),
    )(proj, proj, proj, qg2, kg2, profiles)


def _gdn_kernel(q_ref, k_ref, v_ref, z_ref, cw_ref, pack_ref, og_ref, o_ref,
                q_scr, k_scr, v_scr, xpad_scr, pk2_scr, sel_scr, u_scr, wq_scr, at_scr, kdt_scr, et_scr, oacc_scr,
                *, hb, hp):
    S = q_ref.shape[0]
    P2 = 2 * CHUNK
    W2 = 2 * LANES
    npair = S // P2
    hg = pl.program_id(1)
    gw = 4 * hb
    dk = LANES

    pad = SUBLANES
    for slot in range(2):
        xpad_scr[slot, 0:pad, :] = jnp.zeros((pad, LANES), F32)
        xpad_scr[slot, pad + S:, :] = jnp.zeros((pad, LANES), F32)

    seg = GDN_SEG if S % GDN_SEG == 0 else S

    def conv_silu_to(src_ref, lanes_j, j, which, dst_scr, l2norm, scale):
        xp = xpad_scr.at[(3 * j + which) % 2]
        for s0 in range(0, S, seg):
            xp[pad + s0:pad + s0 + seg, :] = src_ref[s0:s0 + seg, lanes_j].astype(F32)
        for s0 in range(0, S, seg):
            acc = xp[pad + s0:pad + s0 + seg, :] * cw_ref[j, which, 2:3, :]
            for d in (-2, -1, 1, 2):
                acc = acc + xp[pad + s0 + d:pad + s0 + d + seg, :] * cw_ref[j, which, 2 + d:3 + d, :]
            y = _silu(acc)
            if l2norm:
                y = y * (lax.rsqrt(jnp.sum(y * y, axis=-1, keepdims=True) + EPS) * scale)
            dst_scr[j, s0:s0 + seg, :] = y

    for s0 in range(0, S, seg):
        pk = pack_ref[s0:s0 + seg, :]
        p_hi = pk.astype(BF16)
        p_lo = (pk - p_hi.astype(F32)).astype(BF16)
        pk2_scr[s0:s0 + seg, :] = jnp.concatenate([p_hi, p_lo], axis=1)
    srow = lax.broadcasted_iota(I32, (W2, 4 * LANES), 0) % LANES
    scol = lax.broadcasted_iota(I32, (W2, 4 * LANES), 1) // LANES
    for j in range(hp):
        lanes_j = slice(j * LANES, (j + 1) * LANES)
        conv_silu_to(q_ref, lanes_j, j, 0, q_scr, True, dk ** -0.5)
        conv_silu_to(k_ref, lanes_j, j, 1, k_scr, True, 1.0)
        conv_silu_to(v_ref, lanes_j, j, 2, v_scr, False, 1.0)
        src = hg * hp + j + jnp.where(scol < 2, scol * hb, gw + 2 * hb + (scol - 2) * hb)
        sel_scr[j] = jnp.where(srow == src, 1.0, 0.0).astype(BF16)
        oacc_scr[j] = jnp.zeros((S, LANES), F32)

    r4 = lax.broadcasted_iota(I32, (CHUNK, W2), 0)
    l4 = lax.broadcasted_iota(I32, (CHUNK, W2), 1)
    c4 = l4 % CHUNK
    blk4 = l4 // CHUNK
    lo_half = (l4 % LANES) < CHUNK
    ahead = jnp.where(l4 >= LANES, r4 - c4, c4 - r4)
    incl = ahead <= 0
    strict = ahead < 0
    bd16 = (r4 // 16) == (c4 // 16)

    def squeeze(x):
        return jnp.where(lo_half, x[:CHUNK], x[CHUNK:])

    def unsqueeze(x):
        return jnp.concatenate([jnp.where(lo_half, x, jnp.zeros_like(x)),
                                jnp.where(lo_half, jnp.zeros_like(x), x)], axis=0)

    def mm4(a, b):
        rhs = jnp.concatenate([jnp.where(blk4 == g, b, 0.0) for g in range(4)], axis=0)
        return _dot(a.astype(BF16), rhs.astype(BF16))

    first = lax.broadcasted_iota(I32, (P2, 1), 0) < CHUNK
    zpair = jnp.zeros((P2, LANES), BF16)

    def bdiag(x):
        return jnp.concatenate([jnp.concatenate([x[:, :LANES], zpair], axis=1),
                                jnp.concatenate([zpair, x[:, LANES:]], axis=1)], axis=0)

    U = GDN_PREP_UNROLL if npair % GDN_PREP_UNROLL == 0 else 1

    def prep(t, carry):
        cx = []
        chains = [(t * U + u, j) for u in range(U) for j in range(hp)]
        bcs = [_dot(pk2_scr[pl.ds(pl.multiple_of(m * P2, P2), P2), :], sel_scr[j]) for m, j in chains]
        for (m, j), bc in zip(chains, bcs):
            rows = pl.ds(pl.multiple_of(m * P2, P2), P2)
            kp = k_scr[j, rows, :]
            qp = q_scr[j, rows, :]
            vp = v_scr[j, rows, :]
            gc2 = bc[:, 0:W2]
            beta2 = bc[:, W2:2 * W2]
            gcf, gcb = gc2[:, :LANES], gc2[:, LANES:]
            tot2 = jnp.concatenate([jnp.where(first, gcf[CHUNK - 1:CHUNK], gcf[P2 - 1:P2]),
                                    jnp.where(first, gcb[0:1], gcb[CHUNK:CHUNK + 1])], axis=1)
            egc2 = jnp.exp(gc2)
            k2 = jnp.concatenate([kp, kp], axis=1)
            kb2 = k2 * beta2
            vb2 = jnp.concatenate([vp, vp], axis=1) * beta2
            kbe2 = kb2 * egc2
            cx.append(dict(
                m=m, j=j, rows=rows, vb2=vb2, kbe2=kbe2,
                qeb=(jnp.concatenate([qp, qp], axis=1) * egc2).astype(BF16),
                kd2=k2 * jnp.exp(tot2 - gc2),
                et=jnp.exp(tot2),
                dec=jnp.exp(jnp.where(
                    incl, squeeze(gc2) - squeeze(jnp.concatenate([gcf.T, gcb.T], axis=1)), -jnp.inf)),
                stk=jnp.concatenate([kb2[:, :LANES], kb2[:, LANES:], qp], axis=0).astype(BF16),
                kpb=kp.astype(BF16)))

        g3s = [_dot_nt(c['stk'], c['kpb']) for c in cx]
        for c, g3 in zip(cx, g3s):
            lm = jnp.where(strict, squeeze(jnp.concatenate([g3[:P2], g3[P2:2 * P2]], axis=1)) * c['dec'], 0.0)
            attn = squeeze(jnp.concatenate([g3[2 * P2:], g3[2 * P2:]], axis=1)) * c['dec']
            c['attn2'] = unsqueeze(attn).astype(BF16)
            c['lbd'] = jnp.where(bd16, lm, 0.0)
            c['loff'] = lm - c['lbd']
        nn = [-c['lbd'] for c in cx]
        pw = [mm4(c['lbd'], c['lbd']) for c in cx]
        for rnd in range(3):
            prod = [mm4(a, p) for a, p in zip(nn, pw)]
            nxt = [mm4(p, p) for p in pw] if rnd < 2 else pw
            nn = [a + p + q for a, p, q in zip(nn, pw, prod)]
            pw = nxt
        mo = [c['loff'] + x for c, x in zip(cx, [mm4(a, c['loff']) for a, c in zip(nn, cx)])]
        m2 = [mm4(x, x) for x in mo]
        mn = [mm4(x, a) for x, a in zip(mo, nn)]
        xo = [a - x - y for a, x, y in zip(nn, mo, mn)]
        mx = [mm4(a, b) for a, b in zip(m2, xo)]
        toff = [a + b + q for a, b, q in zip(xo, m2, mx)]
        z4 = jnp.zeros((CHUNK, W2), BF16)
        tws = []
        for c, tf in zip(cx, toff):
            vbb, kbb = c['vb2'].astype(BF16), c['kbe2'].astype(BF16)
            blocks = []
            for g in range(4):
                rr = slice((g % 2) * CHUNK, (g % 2 + 1) * CHUNK)
                ll = slice((g // 2) * LANES, (g // 2 + 1) * LANES)
                blocks.append(jnp.concatenate([z4] * g + [vbb[rr, ll], kbb[rr, ll]] + [z4] * (3 - g), axis=1))
            tws.append(_dot(tf.astype(BF16), jnp.concatenate(blocks, axis=0)))
        for c, tw in zip(cx, tws):
            m, j, rows = c['m'], c['j'], c['rows']
            def pair_layout(off):
                piece = lambda g: tw[:, 2 * g * LANES + off:2 * g * LANES + off + LANES]
                return jnp.concatenate([jnp.concatenate([piece(0), piece(2)], axis=1),
                                        jnp.concatenate([piece(1), piece(3)], axis=1)], axis=0)

            u2 = c['vb2'] + pair_layout(0)
            w2 = (c['kbe2'] + pair_layout(LANES)).astype(BF16)
            qeb = c['qeb']
            u_scr[j, rows, :] = u2
            wq_scr[j, pl.ds(pl.multiple_of(m * 2 * P2, 2 * P2), 2 * P2), :] = jnp.concatenate(
                [w2[:CHUNK], qeb[:CHUNK], w2[CHUNK:], qeb[CHUNK:]], axis=0)
            at_scr[j, rows, :] = c['attn2']
            kdt_scr[j, 0, :, rows] = c['kd2'][:, :LANES].T.astype(BF16)
            kdt_scr[j, 1, :, rows] = c['kd2'][:, LANES:].T.astype(BF16)
            et_scr[j, pl.ds(pl.multiple_of(m * 2 * SUBLANES, 2 * SUBLANES), 2 * SUBLANES), :] = jnp.concatenate(
                [c['et'][:SUBLANES], c['et'][CHUNK:CHUNK + SUBLANES]], axis=0)
        return carry

    lax.fori_loop(0, npair // U, prep, 0)

    zc = jnp.zeros((CHUNK, LANES), F32)
    zp = jnp.zeros((P2, LANES), F32)

    def place(v, cpos):
        return jnp.concatenate([v, zc] if cpos == 0 else [zc, v], axis=0)

    def scan(m, states):
        pf = m
        pb = npair - 1 - m
        rows_f = pl.ds(pl.multiple_of(pf * P2, P2), P2)
        rows_b = pl.ds(pl.multiple_of(pb * P2, P2), P2)
        hx = []
        for j in range(hp):
            hx.append(dict(
                u_f=u_scr[j, rows_f, :LANES], u_b=u_scr[j, rows_b, LANES:],
                at_f=at_scr[j, rows_f, :LANES], at_b=at_scr[j, rows_b, LANES:],
                kdt=jnp.concatenate([kdt_scr[j, 0, :, rows_f], kdt_scr[j, 1, :, rows_b]], axis=1),
                wq_f=wq_scr[j, pl.ds(pl.multiple_of(pf * 2 * P2, 2 * P2), 2 * P2), :LANES],
                wq_b=wq_scr[j, pl.ds(pl.multiple_of(pb * 2 * P2, 2 * P2), 2 * P2), LANES:],
                et_f=et_scr[j, pl.ds(pl.multiple_of(pf * 2 * SUBLANES, 2 * SUBLANES), 2 * SUBLANES), :LANES],
                et_b=et_scr[j, pl.ds(pl.multiple_of(pb * 2 * SUBLANES, 2 * SUBLANES), 2 * SUBLANES), LANES:]))
        sts = list(states)
        for step in range(2):
            cf, cb = step, 1 - step
            rrs = [_dot(jnp.concatenate([c['wq_f'][cf * P2:(cf + 1) * P2], c['wq_b'][cb * P2:(cb + 1) * P2]], axis=1),
                        bdiag(st.astype(BF16))) for c, st in zip(hx, sts)]
            ress = []
            for c, rr in zip(hx, rrs):
                u2 = jnp.concatenate([c['u_f'][cf * CHUNK:(cf + 1) * CHUNK],
                                      c['u_b'][cb * CHUNK:(cb + 1) * CHUNK]], axis=1)
                v_new = u2 - rr[:CHUNK]
                rhs = jnp.concatenate(
                    [jnp.concatenate([place(v_new[:, :LANES], cf), zp], axis=1),
                     jnp.concatenate([zp, place(v_new[:, LANES:], cb)], axis=1)], axis=0).astype(BF16)
                lhs = jnp.concatenate(
                    [jnp.concatenate([c['at_f'][cf * CHUNK:(cf + 1) * CHUNK],
                                      c['at_b'][cb * CHUNK:(cb + 1) * CHUNK]], axis=1),
                     c['kdt']], axis=0)
                ress.append(_dot(lhs, rhs))
            for j, (c, rr, res) in enumerate(zip(hx, rrs, ress)):
                o2 = rr[CHUNK:] + res[:CHUNK]
                et2 = jnp.concatenate([c['et_f'][cf * SUBLANES:cf * SUBLANES + 1],
                                       c['et_b'][cb * SUBLANES:cb * SUBLANES + 1]], axis=1)
                sts[j] = sts[j] * et2 + res[CHUNK:]
                of_rows = pl.ds(pl.multiple_of(pf * P2 + cf * CHUNK, CHUNK), CHUNK)
                ob_rows = pl.ds(pl.multiple_of(pb * P2 + cb * CHUNK, CHUNK), CHUNK)
                oacc_scr[j, of_rows, :] = oacc_scr[j, of_rows, :] + o2[:, :LANES]
                oacc_scr[j, ob_rows, :] = oacc_scr[j, ob_rows, :] + o2[:, LANES:]
        return tuple(sts)

    s0 = jnp.zeros((dk, W2), F32)
    lax.fori_loop(0, npair, scan, (s0,) * hp)

    for j in range(hp):
        lanes_j = slice(j * LANES, (j + 1) * LANES)
        for s0 in range(0, S, seg):
            o = oacc_scr[j, s0:s0 + seg, :]
            y = o * lax.rsqrt(jnp.mean(o * o, axis=-1, keepdims=True) + EPS) * og_ref[...]
            z = z_ref[s0:s0 + seg, lanes_j].astype(F32)
            o_ref[s0:s0 + seg, lanes_j] = (y * _silu(z)).astype(o_ref.dtype)


def _gdn(proj, cw4, pack3, onorm_g, B, S, HB, base_blk):
    T = proj.shape[0]
    hp = GDN_HP if (HB % GDN_HP == 0 and base_blk % GDN_HP == 0) else 1
    wblk = hp * LANES
    npair = S // (2 * CHUNK)

    def col(k):
        off = (base_blk + k * HB) // hp
        return lambda b, h: (b, off + h)

    return pl.pallas_call(
        functools.partial(_gdn_kernel, hb=HB, hp=hp),
        grid=(B, HB // hp),
        in_specs=[pl.BlockSpec((S, wblk), col(0)),
                  pl.BlockSpec((S, wblk), col(1)),
                  pl.BlockSpec((S, wblk), col(2)),
                  pl.BlockSpec((S, wblk), col(3)),
                  pl.BlockSpec((hp, 3, cw4.shape[2], LANES), lambda b, h: (h, 0, 0, 0)),
                  pl.BlockSpec((None, S, LANES), lambda b, h: (b, 0, 0)),
                  pl.BlockSpec((1, LANES), lambda b, h: (0, 0))],
        out_specs=pl.BlockSpec((S, wblk), lambda b, h: (b, h)),
        out_shape=jax.ShapeDtypeStruct((T, HB * LANES), BF16),
        scratch_shapes=[pltpu.VMEM((hp, S, LANES), F32)] * 3
        + [pltpu.VMEM((2, S + 2 * SUBLANES, LANES), F32),
           pltpu.VMEM((S, 2 * LANES), BF16),
           pltpu.VMEM((hp, 2 * LANES, 4 * LANES), BF16),
           pltpu.VMEM((hp, S, 2 * LANES), F32),
           pltpu.VMEM((hp, 2 * S, 2 * LANES), BF16),
           pltpu.VMEM((hp, S, 2 * LANES), BF16),
           pltpu.VMEM((hp, 2, LANES, S), BF16),
           pltpu.VMEM((hp, npair * 2 * SUBLANES, 2 * LANES), F32),
           pltpu.VMEM((hp, S, LANES), F32)],
        compiler_params=_cparams(("arbitrary", "arbitrary"), 56),
    )(proj, proj, proj, proj, cw4, pack3, onorm_g)


def _outproj_kernel(oa_ref, ob_ref, x_ref, gt_ref, w_ref, g_ref, sc_ref, sh_ref, wr_ref, br_ref,
                    x1_ref, hp_ref, lg_ref, z_ref, h_scr):
    z_ref[...] = jnp.zeros(z_ref.shape, U32)
    da = oa_ref.shape[1]
    y = _dot(oa_ref[...], w_ref[:da, :]) + _dot(ob_ref[...], w_ref[da:, :])
    x1 = x_ref[...] + gt_ref[...] * y
    x1_ref[...] = x1
    hn = x1 * lax.rsqrt(jnp.mean(x1 * x1, axis=-1, keepdims=True) + EPS) * g_ref[...]
    h = hn * (1.0 + sc_ref[...]) + sh_ref[...]
    hp_ref[...] = _pack_halves(h)
    h_scr[...] = h

    @pl.when(pl.program_id(0) < pl.num_programs(0))
    def _():
        lg_ref[...] = _dot_hilo(h_scr[...], wr_ref[...]) + br_ref[...]


def _outproj(oa, ob, x2, gt1, w_out_b, g2, sc2, sh2, wr, br, S, p_rows):
    T, D = x2.shape
    tm = ROW_TILE
    per_b = S // tm
    bmap = lambda i: (i // per_b, 0, 0)
    zrows = p_rows // (T // tm)
    assert zrows * (T // tm) == p_rows and zrows % SUBLANES == 0
    return pl.pallas_call(
        _outproj_kernel,
        grid=(T // tm,),
        in_specs=[pl.BlockSpec((tm, oa.shape[1]), lambda i: (i, 0)),
                  pl.BlockSpec((tm, ob.shape[1]), lambda i: (i, 0)),
                  pl.BlockSpec((tm, D), lambda i: (i, 0)),
                  pl.BlockSpec((None, 1, D), bmap),
                  pl.BlockSpec((D, D), lambda i: (0, 0)),
                  pl.BlockSpec((1, D), lambda i: (0, 0)),
                  pl.BlockSpec((None, 1, D), bmap),
                  pl.BlockSpec((None, 1, D), bmap),
                  pl.BlockSpec((D, 2 * LANES), lambda i: (0, 0)),
                  pl.BlockSpec((1, LANES), lambda i: (0, 0))],
        out_specs=[pl.BlockSpec((tm, D), lambda i: (i, 0)),
                   pl.BlockSpec((tm, D // 2), lambda i: (i, 0)),
                   pl.BlockSpec((tm, LANES), lambda i: (i, 0)),
                   pl.BlockSpec((zrows, D // 2), lambda i: (i, 0))],
        out_shape=[jax.ShapeDtypeStruct((T, D), F32), jax.ShapeDtypeStruct((T, D // 2), U32),
                   jax.ShapeDtypeStruct((T, LANES), F32), jax.ShapeDtypeStruct((p_rows, D // 2), U32)],
        scratch_shapes=[pltpu.VMEM((tm, D), F32)],
        compiler_params=_cparams(("arbitrary",), 48),
    )(oa, ob, x2, gt1, w_out_b, g2, sc2, sh2, wr, br)


def _route_kernel(lg_ref, o_ref, info_ref, run_scr, *, ne, ng):
    ph = pl.program_id(0)
    i = pl.program_id(1)

    @pl.when((ph == 0) & (i == 0))
    def _():
        run_scr[...] = jnp.zeros_like(run_scr)

    @pl.when((ph == 1) & (i == 0))
    def _():
        cnt = run_scr[...]
        padded = jnp.ceil(cnt * (1.0 / MOE_RB)) * MOE_RB
        k = lax.broadcasted_iota(I32, (LANES, LANES), 0)
        e = lax.broadcasted_iota(I32, (LANES, LANES), 1)
        start = _dot(padded, jnp.where(k < e, 1.0, 0.0), precision=HIGHEST)
        rowi = lax.broadcasted_iota(I32, cnt.shape, 0)
        info_ref[...] = jnp.where(rowi == 0, cnt, jnp.where(rowi == 1, padded, start))
        run_scr[...] = start

    lg = lg_ref[...]
    tm = lg.shape[0]
    epg = ne // ng
    lane_i = lax.broadcasted_iota(I32, lg.shape, 1)
    lane = lane_i.astype(F32)
    big = float(2 * LANES)
    is_g = (lane_i >= ne) & (lane_i < ne + ng)
    gl = jnp.where(is_g, lg, -jnp.inf)
    gmax = jnp.max(gl, axis=-1, keepdims=True)
    gidx = jnp.min(jnp.where(gl == gmax, lane, big), axis=-1, keepdims=True) - ne
    psel = 1.0 / jnp.sum(jnp.where(is_g, jnp.exp(gl - gmax), 0.0), axis=-1, keepdims=True)
    in_grp = (lane_i // epg).astype(F32) == gidx
    el = jnp.where(in_grp & (lane_i < ne), lg, -jnp.inf)
    m1 = jnp.max(el, axis=-1, keepdims=True)
    i1 = jnp.min(jnp.where(el == m1, lane, big), axis=-1, keepdims=True)
    el2 = jnp.where(lane == i1, -jnp.inf, el)
    m2 = jnp.max(el2, axis=-1, keepdims=True)
    i2 = jnp.min(jnp.where(el2 == m2, lane, big), axis=-1, keepdims=True)
    e21 = jnp.exp(m2 - m1)
    g1 = psel / (1.0 + e21)
    g2 = psel * e21 / (1.0 + e21)
    o1 = jnp.where(lane == i1, 1.0, 0.0)
    o2 = jnp.where(lane == i2, 1.0, 0.0)
    cnt = o1 + o2
    r = lax.broadcasted_iota(I32, (tm, tm), 0)
    c = lax.broadcasted_iota(I32, (tm, tm), 1)
    before = _dot(jnp.where(c < r, 1.0, 0.0).astype(BF16), cnt.astype(BF16)) + run_scr[0:1, :]
    d1 = jnp.sum(o1 * before, axis=-1, keepdims=True)
    d2 = jnp.sum(o2 * before, axis=-1, keepdims=True)
    run_scr[...] = run_scr[...] + jnp.sum(cnt, axis=0, keepdims=True)

    @pl.when(ph == 1)
    def _():
        out = jnp.zeros(lg.shape, F32)
        for j, val in enumerate((i1, i2, g1, g2, d1, d2)):
            out = jnp.where(lane_i == j, val, out)
        o_ref[...] = out


def _route(logits, ne, ng):
    T = logits.shape[0]
    tm = _pick_tile(T, (1024, 512, 256))
    return pl.pallas_call(
        functools.partial(_route_kernel, ne=ne, ng=ng),
        grid=(2, T // tm),
        in_specs=[pl.BlockSpec((tm, LANES), lambda p, i: (i, 0))],
        out_specs=[pl.BlockSpec((tm, LANES), lambda p, i: (i * p, 0)),
                   pl.BlockSpec((SUBLANES, LANES), lambda p, i: (0, 0))],
        out_shape=[jax.ShapeDtypeStruct((T, LANES), F32), jax.ShapeDtypeStruct((SUBLANES, LANES), F32)],
        scratch_shapes=[pltpu.VMEM((SUBLANES, LANES), F32)],
        compiler_params=_cparams(("arbitrary", "arbitrary"), 16),
    )(logits)


def _dispatch_kernel(dest_ref, h_hbm, xs_in, xs_hbm, sem):
    del xs_in
    tm = ROW_TILE
    base = pl.program_id(0) * tm

    def copy(tok, a):
        return pltpu.make_async_copy(h_hbm.at[pl.ds(tok, 1)], xs_hbm.at[pl.ds(dest_ref[a], 1)], sem)

    def issue(t, carry):
        for k in range(TOP_K):
            copy(base + t, (base + t) * TOP_K + k).start()
        return carry

    lax.fori_loop(0, tm, issue, 0, unroll=DMA_UNROLL)

    for k in range(TOP_K):
        pltpu.make_async_copy(h_hbm.at[pl.ds(0, tm)], xs_hbm.at[pl.ds(0, tm)], sem).wait()


def _dispatch(dest, hpk, xs_zero):
    T, dh = hpk.shape
    p_rows = xs_zero.shape[0]
    return pl.pallas_call(
        _dispatch_kernel,
        grid_spec=pltpu.PrefetchScalarGridSpec(
            num_scalar_prefetch=1,
            grid=(T // ROW_TILE,),
            in_specs=[pl.BlockSpec(memory_space=pl.ANY), pl.BlockSpec(memory_space=pl.ANY)],
            out_specs=pl.BlockSpec(memory_space=pl.ANY),
            scratch_shapes=[pltpu.SemaphoreType.DMA(())]),
        out_shape=jax.ShapeDtypeStruct((p_rows, dh), U32),
        input_output_aliases={2: 0},
        compiler_params=_cparams(("arbitrary",), 16),
    )(dest, hpk, xs_zero)


def _moe_kernel(we_ref, ws_ref, wn_ref, wt_ref, xs_hbm, w1_hbm, w3_hbm, w2_hbm, y_hbm,
                xbuf, xlo, xhi, yacc, ypk, w1buf, w3buf, w2buf, sem_in, sem_out, sem_w):
    w = pl.program_id(0)
    c = pl.program_id(1)
    nw = pl.num_programs(0)
    nc = pl.num_programs(1)
    nrows = wn_ref[w]
    start = ws_ref[w]
    nblk = nrows // MOE_RB
    dh = xbuf.shape[1]
    cw = w1buf.shape[-1]

    g = w * nc + c

    def w_copies(step, do):
        item = step // nc
        chunk = step % nc

        @pl.when((item < nw) & (wn_ref[jnp.minimum(item, nw - 1)] > 0))
        def _():
            e = we_ref[item]
            cols = pl.ds(pl.multiple_of(chunk * cw, cw), cw)
            slot = step % MOE_WSLOTS
            do(pltpu.make_async_copy(w1_hbm.at[e, :, cols], w1buf.at[slot], sem_w.at[slot]))
            do(pltpu.make_async_copy(w3_hbm.at[e, :, cols], w3buf.at[slot], sem_w.at[slot]))
            do(pltpu.make_async_copy(w2_hbm.at[e, cols, :], w2buf.at[slot], sem_w.at[slot]))

    @pl.when(g == 0)
    def _():
        for ahead in range(MOE_WSLOTS - 1):
            w_copies(ahead, lambda cp: cp.start())

    w_copies(g + MOE_WSLOTS - 1, lambda cp: cp.start())
    w_copies(g, lambda cp: cp.wait())
    wslot = g % MOE_WSLOTS
    w1_ref, w3_ref, w2_ref = w1buf.at[wslot], w3buf.at[wslot], w2buf.at[wslot]

    def blk_rows(rb):
        return pl.ds(pl.multiple_of(rb * MOE_RB, MOE_RB), MOE_RB)

    def hbm_rows(item_start, rb):
        return pl.ds(pl.multiple_of(item_start + rb * MOE_RB, MOE_RB), MOE_RB)

    def in_copy(item_start, rb):
        return pltpu.make_async_copy(xs_hbm.at[hbm_rows(item_start, rb)], xbuf.at[blk_rows(rb)], sem_in)

    def out_copy(item_start, rb):
        return pltpu.make_async_copy(ypk.at[blk_rows(rb)], y_hbm.at[hbm_rows(item_start, rb)], sem_out)

    def each_block(n, fn):
        def body(rb, carry):
            fn(rb)
            return carry
        lax.fori_loop(0, n, body, 0)

    @pl.when(c == 0)
    def _load():
        @pl.when(w == 0)
        def _():
            each_block(nblk, lambda rb: in_copy(start, rb).start())

        each_block(nblk, lambda rb: in_copy(start, rb).wait())

        def unpack(rb):
            lo, hi = _unpack_halves(xbuf[blk_rows(rb), :])
            xlo[blk_rows(rb), :] = lo.astype(BF16)
            xhi[blk_rows(rb), :] = hi.astype(BF16)
            yacc[blk_rows(rb), :] = jnp.zeros((MOE_RB, 2 * dh), F32)

        each_block(nblk, unpack)

        @pl.when(w + 1 < nw)
        def _():
            nxt = ws_ref[w + 1]
            each_block(wn_ref[w + 1] // MOE_RB, lambda rb: in_copy(nxt, rb).start())

    @pl.when(nrows > 0)
    def _compute():
        def rows_block(row0, nr):
            rows = pl.ds(row0, nr)
            xl = xlo[rows, :]
            xh = xhi[rows, :]
            h1 = _dot(xl, w1_ref[:dh, :].astype(BF16)) + _dot(xh, w1_ref[dh:, :].astype(BF16))
            h3 = _dot(xl, w3_ref[:dh, :].astype(BF16)) + _dot(xh, w3_ref[dh:, :].astype(BF16))
            hid = (_silu(h1) * h3).astype(BF16)
            yacc[rows, :] = yacc[rows, :] + _dot(hid, w2_ref[...].astype(BF16))

        ntall = nrows // MOE_TALL
        each_block(ntall, lambda i: rows_block(pl.multiple_of(i * MOE_TALL, MOE_TALL), MOE_TALL))

        rem = nrows - ntall * MOE_TALL
        for nr in range(MOE_RB, MOE_TALL, MOE_RB):
            @pl.when(rem == nr)
            def _(nr=nr):
                rows_block(pl.multiple_of(ntall * MOE_TALL, MOE_TALL), nr)

    @pl.when(c == nc - 1)
    def _store():
        @pl.when(w > 0)
        def _():
            prev = ws_ref[w - 1]
            each_block(wn_ref[w - 1] // MOE_RB, lambda rb: out_copy(prev, rb).wait())

        def pack(rb):
            ypk[blk_rows(rb), :] = _pack_halves(yacc[blk_rows(rb), :])

        each_block(nblk, pack)
        each_block(nblk, lambda rb: out_copy(start, rb).start())

        @pl.when(w == nw - 1)
        def _():
            each_block(nblk, lambda rb: out_copy(start, rb).wait())

    @pl.when((w == nw - 1) & (c == nc - 1))
    def _zero_tail():
        ypk[blk_rows(0), :] = jnp.zeros((MOE_RB, dh), U32)
        first = wt_ref[0] // MOE_RB

        def tail_copy(b):
            return pltpu.make_async_copy(ypk.at[blk_rows(0)],
                                         y_hbm.at[pl.ds(pl.multiple_of(b * MOE_RB, MOE_RB), MOE_RB)], sem_out)

        def start(b, carry):
            tail_copy(b).start()
            return carry

        def wait(b, carry):
            tail_copy(b).wait()
            return carry

        lax.fori_loop(first, y_hbm.shape[0] // MOE_RB, start, 0)
        lax.fori_loop(first, y_hbm.shape[0] // MOE_RB, wait, 0)


def _moe(we, ws, wn, wt, xs, w1, w3, w2):
    P, dh = xs.shape
    D = 2 * dh
    NE, _, DE = w1.shape
    cw = min(MOE_CW, DE)
    nc = DE // cw
    nw = we.shape[0]
    return pl.pallas_call(
        _moe_kernel,
        grid_spec=pltpu.PrefetchScalarGridSpec(
            num_scalar_prefetch=4,
            grid=(nw, nc),
            in_specs=[pl.BlockSpec(memory_space=pl.ANY)] * 4,
            out_specs=pl.BlockSpec(memory_space=pl.ANY),
            scratch_shapes=[pltpu.VMEM((MOE_RMAX, dh), U32),
                            pltpu.VMEM((MOE_RMAX, dh), BF16),
                            pltpu.VMEM((MOE_RMAX, dh), BF16),
                            pltpu.VMEM((MOE_RMAX, D), F32),
                            pltpu.VMEM((MOE_RMAX, dh), U32),
                            pltpu.VMEM((MOE_WSLOTS, D, cw), F32),
                            pltpu.VMEM((MOE_WSLOTS, D, cw), F32),
                            pltpu.VMEM((MOE_WSLOTS, cw, D), F32),
                            pltpu.SemaphoreType.DMA(()),
                            pltpu.SemaphoreType.DMA(()),
                            pltpu.SemaphoreType.DMA((MOE_WSLOTS,))]),
        out_shape=jax.ShapeDtypeStruct((P, dh), U32),
        compiler_params=_cparams(("arbitrary", "arbitrary"), 56),
    )(we, ws, wn, wt, xs, w1, w3, w2)


def _moe_schedule(info, NE, p_rows):
    padded = info[1, :NE].astype(I32)
    start_pad = info[2, :NE].astype(I32)
    items = (padded + MOE_RMAX - 1) // MOE_RMAX
    cum_items = jnp.cumsum(items)
    n_items = cum_items[-1]
    nw = (p_rows + NE * (MOE_RMAX - MOE_RB)) // MOE_RMAX
    wi = jnp.arange(nw, dtype=I32)
    valid = wi < n_items
    wi_c = jnp.minimum(wi, jnp.maximum(n_items - 1, 0))
    we = jnp.minimum(jnp.sum(cum_items[None, :] <= wi_c[:, None], axis=1), NE - 1).astype(I32)
    onehot = we[:, None] == jnp.arange(NE, dtype=I32)[None, :]
    pick = lambda v: jnp.sum(jnp.where(onehot, v[None, :], 0), axis=1)
    local = wi_c - (pick(cum_items) - pick(items))
    ws = (pick(start_pad) + local * MOE_RMAX).astype(I32)
    wn = jnp.where(valid, jnp.clip(pick(padded) - local * MOE_RMAX, 0, MOE_RMAX), 0).astype(I32)
    wt = (start_pad[NE - 1] + padded[NE - 1]).reshape(1)
    return we, ws, wn, wt


def _combine_kernel(dest_ref, x1_ref, rt_ref, gt_ref, y_hbm, o_ref, ybuf, sems):
    tm = ROW_TILE
    i = pl.program_id(0)
    n = pl.num_programs(0)
    dh = ybuf.shape[-1]

    def copy(tile, slot, t, k):
        a = (tile * tm + t) * TOP_K + k
        return pltpu.make_async_copy(y_hbm.at[pl.ds(dest_ref[a], 1)], ybuf.at[slot, k, pl.ds(t, 1)], sems.at[slot])

    def issue_tile(tile, slot):
        def body(t, carry):
            for k in range(TOP_K):
                copy(tile, slot, t, k).start()
            return carry
        lax.fori_loop(0, tm, body, 0, unroll=DMA_UNROLL)

    @pl.when(i == 0)
    def _():
        issue_tile(0, 0)

    @pl.when(i + 1 < n)
    def _():
        issue_tile(i + 1, (i + 1) % 2)

    slot = i % 2

    for k in range(TOP_K):
        pltpu.make_async_copy(y_hbm.at[pl.ds(0, tm)], ybuf.at[slot, k], sems.at[slot]).wait()

    rt = rt_ref[...]
    g1 = rt[:, 2:3]
    g2 = rt[:, 3:4]
    lo1, hi1 = _unpack_halves(ybuf[slot, 0])
    lo2, hi2 = _unpack_halves(ybuf[slot, 1])
    o_ref[:, :dh] = x1_ref[:, :dh] + gt_ref[:, :dh] * (g1 * lo1 + g2 * lo2)
    o_ref[:, dh:] = x1_ref[:, dh:] + gt_ref[:, dh:] * (g1 * hi1 + g2 * hi2)


def _combine(dest, x1, route, gt2, ypk, S):
    T, D = x1.shape
    tm = ROW_TILE
    per_b = S // tm
    return pl.pallas_call(
        _combine_kernel,
        grid_spec=pltpu.PrefetchScalarGridSpec(
            num_scalar_prefetch=1,
            grid=(T // tm,),
            in_specs=[pl.BlockSpec((tm, D), lambda i, d: (i, 0)),
                      pl.BlockSpec((tm, LANES), lambda i, d: (i, 0)),
                      pl.BlockSpec((None, 1, D), lambda i, d: (i // per_b, 0, 0)),
                      pl.BlockSpec(memory_space=pl.ANY)],
            out_specs=pl.BlockSpec((tm, D), lambda i, d: (i, 0)),
            scratch_shapes=[pltpu.VMEM((2, TOP_K, tm, D // 2), U32),
                            pltpu.SemaphoreType.DMA((2,))]),
        out_shape=jax.ShapeDtypeStruct((T, D), F32),
        compiler_params=_cparams(("arbitrary",), 32),
    )(dest, x1, route, gt2, ypk)


def kernel(x, c, w_ada, b_ada, norm1_g, norm2_g, w_in, qn_g, kn_g, rel_bias, conv_w, A_log, dt_bias,
           onorm_g, w_out, w_rg, b_rg, w_re, b_re, w1, w3, w2):
    B, S, D = x.shape
    depth = w_ada.shape[0]
    HA, hda = rel_bias.shape[1], qn_g.shape[-1]
    HB, hdb = A_log.shape[-1], onorm_g.shape[-1]
    DA, DB = HA * hda, HB * hdb
    NG, NE = w_rg.shape[-1], w_re.shape[-1]
    T = B * S
    assert hdb == LANES and 2 * hda == LANES and DA + DB == D and 16 * HB <= LANES
    assert NE + NG <= LANES and conv_w.shape[1] == 5 and S % ROW_TILE == 0
    n_main = 3 * DA + 4 * DB
    p_rows = TOP_K * T + NE * MOE_RB

    bias_prof = _attn_bias_profiles(rel_bias, S)
    w_in_t = jnp.swapaxes(w_in, 1, 2)
    x2 = x.reshape(T, D)
    for l in range(depth):
        mod = _ada(c, w_ada[l], b_ada[l]).reshape(B, 6, 1, D)
        sh1, sc1, gt1, sh2, sc2, gt2 = (mod[:, i] for i in range(6))

        w_gate = jnp.tile(w_in_t[l][n_main:, :].T, (1, 4))
        w_gate = _hilo_weights(jnp.pad(w_gate, ((0, 0), (0, LANES - w_gate.shape[1]))))
        proj, gat = _inproj(x2, norm1_g[l].reshape(1, D), sc1, sh1, w_in_t[l], n_main, w_gate, S)

        def gate_row(p):
            grp = jnp.concatenate([p.reshape(-1), jnp.zeros((2 * HB,), F32)])
            return jnp.pad(jnp.tile(grp, 4), (0, LANES - 16 * HB)).reshape(1, LANES)

        pack = _gating(gat, gate_row(A_log[l]), gate_row(dt_bias[l]), HB)

        oa = _attention(proj, jnp.tile(qn_g[l], 2).reshape(1, LANES), jnp.tile(kn_g[l], 2).reshape(1, LANES),
                        bias_prof, B, S, HA, hda)
        cw4 = jnp.transpose(conv_w[l].reshape(conv_w.shape[1], 3, HB, hdb), (2, 1, 0, 3))
        ob = _gdn(proj, cw4, pack.reshape(B, S, LANES), onorm_g[l].reshape(1, LANES), B, S, HB, 3 * DA // LANES)

        wr = _hilo_weights(jnp.pad(jnp.concatenate([w_re[l], w_rg[l]], axis=1), ((0, 0), (0, LANES - NE - NG))))
        br = jnp.pad(jnp.concatenate([b_re[l], b_rg[l]]), (0, LANES - NE - NG)).reshape(1, LANES)
        x1, hpk, logits, xs_zero = _outproj(oa, ob, x2, gt1, w_out[l].astype(BF16), norm2_g[l].reshape(1, D),
                                            sc2, sh2, wr, br, S, p_rows)
        route, info = _route(logits, NE, NG)
        dest = route[:, 4:4 + TOP_K].astype(I32).reshape(TOP_K * T)
        we, ws, wn, wt = _moe_schedule(info, NE, p_rows)
        xs = _dispatch(dest, hpk, xs_zero)
        ypk = _moe(we, ws, wn, wt, xs, w1[l], w3[l], w2[l])
        x2 = _combine(dest, x1, route, gt2, ypk, S)
    return x2.reshape(B, S, D)
```

```python
import functools

import numpy as np
import jax
import jax.numpy as jnp
from jax import lax
from jax.experimental import pallas as pl
from jax.experimental.pallas import tpu as pltpu

F32 = jnp.float32
BF16 = jnp.bfloat16
I32 = jnp.int32
U32 = jnp.uint32
HIGHEST = lax.Precision.HIGHEST

EPS = 1e-6
NEG = -1e30
DILATED_BRANCHES = ((128, 1), (512, 4), (2048, 16))
REL_MAX_DIST = 1024
CHUNK = 64
TOP_K = 2

LANES = 128
SUBLANES = 8
MIB = 1 << 20

ATT_QB = 128
ATT_KW = 256
ATT_UNROLL = 8
ATT_PROF_W = 512
GDN_HP = 2
GDN_PREP_UNROLL = 8
GDN_SEG = 512
MOE_RB = 128
MOE_TALL = 512
MOE_RMAX = 512
MOE_WSLOTS = 3
MOE_CW = 512
ROW_TILE = 256
DMA_UNROLL = 8


def _cparams(sem, vmem_mib):
    return pltpu.CompilerParams(dimension_semantics=sem, vmem_limit_bytes=vmem_mib * MIB)


def _dot(a, b, **kw):
    return jnp.dot(a, b, preferred_element_type=F32, **kw)


def _dot_nt(a, b):
    return lax.dot_general(a, b, (((1,), (1,)), ((), ())), preferred_element_type=F32)


def _pick_tile(n, prefs):
    for t in prefs:
        if n % t == 0:
            return t
    return n


def _pack_halves(x):
    half = x.shape[1] // 2
    lo = lax.bitcast_convert_type(x[:, :half].astype(BF16).astype(F32), U32)
    hi = lax.bitcast_convert_type(x[:, half:].astype(BF16).astype(F32), U32)
    return lax.shift_right_logical(lo, jnp.uint32(16)) | (hi & jnp.uint32(0xFFFF0000))


def _silu(x):
    h = 0.5 * x
    return h + h * jnp.tanh(h)


def _hilo_weights(w):
    hi = w.astype(BF16)
    lo = (w - hi.astype(F32)).astype(BF16)
    return jnp.concatenate([hi, lo], axis=1)


def _dot_hilo(x, w2):
    m, n = x.shape[0], w2.shape[1] // 2
    hi = x.astype(BF16)
    lo = (x - hi.astype(F32)).astype(BF16)
    r = _dot(jnp.concatenate([hi, lo], axis=0), w2)
    return r[:m, :n] + (r[:m, n:] + r[m:, :n])


def _unpack_halves(p):
    lo = lax.bitcast_convert_type(lax.shift_left(p, jnp.uint32(16)), F32)
    hi = lax.bitcast_convert_type(p & jnp.uint32(0xFFFF0000), F32)
    return lo, hi


def _ada_kernel(c_ref, w_ref, b_ref, o_ref):
    c = c_ref[...]
    s = _silu(c).astype(BF16)
    o_ref[...] = _dot(s, w_ref[...].astype(BF16)) + b_ref[...]


def _ada(c, w_ada, b_ada):
    B, D = c.shape
    N = w_ada.shape[1]
    tn = _pick_tile(N, (1024, 512, 256, 128))
    return pl.pallas_call(
        _ada_kernel,
        grid=(N // tn,),
        in_specs=[pl.BlockSpec((B, D), lambda j: (0, 0)),
                  pl.BlockSpec((D, tn), lambda j: (0, j)),
                  pl.BlockSpec((1, tn), lambda j: (0, j))],
        out_specs=pl.BlockSpec((B, tn), lambda j: (0, j)),
        out_shape=jax.ShapeDtypeStruct((B, N), F32),
        compiler_params=_cparams(("arbitrary",), 40),
    )(c, w_ada, b_ada.reshape(1, N))


def _inproj_kernel(x_ref, g_ref, sc_ref, sh_ref, w_ref, wg_ref, o_ref, og_ref, h_scr):
    @pl.when(pl.program_id(1) == 0)
    def _():
        x = x_ref[...]
        y = x * lax.rsqrt(jnp.mean(x * x, axis=-1, keepdims=True) + EPS) * g_ref[...]
        h = y * (1.0 + sc_ref[...]) + sh_ref[...]
        h_scr[...] = h.astype(BF16)
        og_ref[...] = _dot_hilo(h, wg_ref[...])

    o_ref[...] = _dot_nt(h_scr[...], w_ref[...].astype(BF16)).astype(o_ref.dtype)


def _inproj(x2, g, sc, sh, w_all, n_main, w_gate, S):
    T, D = x2.shape
    NM = n_main
    tm = _pick_tile(S, (1024, 512, 256, 128))
    tn = _pick_tile(NM, (1024, 512, 256, 128))
    per_b = S // tm
    return pl.pallas_call(
        _inproj_kernel,
        grid=(T // tm, NM // tn),
        in_specs=[pl.BlockSpec((tm, D), lambda i, j: (i, 0)),
                  pl.BlockSpec((1, D), lambda i, j: (0, 0)),
                  pl.BlockSpec((None, 1, D), lambda i, j: (i // per_b, 0, 0)),
                  pl.BlockSpec((None, 1, D), lambda i, j: (i // per_b, 0, 0)),
                  pl.BlockSpec((tn, D), lambda i, j: (j, 0)),
                  pl.BlockSpec((D, 2 * LANES), lambda i, j: (0, 0))],
        out_specs=[pl.BlockSpec((tm, tn), lambda i, j: (i, j)),
                   pl.BlockSpec((tm, LANES), lambda i, j: (i, 0))],
        out_shape=[jax.ShapeDtypeStruct((T, NM), BF16), jax.ShapeDtypeStruct((T, LANES), F32)],
        scratch_shapes=[pltpu.VMEM((tm, D), BF16)],
        compiler_params=_cparams(("arbitrary", "arbitrary"), 56),
    )(x2, g, sc, sh, w_all, w_gate)


def _gating_kernel(gat_ref, a_ref, dt_ref, o_ref, *, hb):
    gat = gat_ref[...]
    tm = gat.shape[0]
    gw = 4 * hb
    g = -jnp.exp(a_ref[...]) * jax.nn.softplus(gat + dt_ref[...])
    beta = jax.nn.sigmoid(gat)
    r = lax.broadcasted_iota(I32, (tm, tm), 0)
    c = lax.broadcasted_iota(I32, (tm, tm), 1)
    same = (r // CHUNK) == (c // CHUNK)
    pre = _dot(jnp.where(same & (c <= r), 1.0, 0.0), g, precision=HIGHEST)
    suf = _dot(jnp.where(same & (c >= r), 1.0, 0.0), g, precision=HIGHEST)
    lane = lax.broadcasted_iota(I32, gat.shape, 1)
    o_ref[...] = jnp.where(lane // gw == 0, jnp.where(lane % gw < hb, pre, suf), beta)


def _gating(gat, a_row, dt_row, hb):
    T = gat.shape[0]
    tm = ROW_TILE
    return pl.pallas_call(
        functools.partial(_gating_kernel, hb=hb),
        grid=(T // tm,),
        in_specs=[pl.BlockSpec((tm, LANES), lambda i: (i, 0)),
                  pl.BlockSpec((1, LANES), lambda i: (0, 0)),
                  pl.BlockSpec((1, LANES), lambda i: (0, 0))],
        out_specs=pl.BlockSpec((tm, LANES), lambda i: (i, 0)),
        out_shape=jax.ShapeDtypeStruct((T, LANES), F32),
        compiler_params=_cparams(("arbitrary",), 16),
    )(gat, a_row, dt_row)


def _t5_bucket(rel, n_buckets):
    half = n_buckets // 2
    max_exact = half // 2
    n = np.abs(rel)
    large = max_exact + (np.log(np.maximum(n, 1) / max_exact) / np.log(REL_MAX_DIST / max_exact)
                         * (half - max_exact)).astype(np.int32)
    large = np.minimum(large, half - 1)
    return (np.where(rel > 0, half, 0) + np.where(n < max_exact, n, large)).astype(np.int32)


def _attn_plan(S):
    plan, base = [], 0
    for window, dil in DILATED_BRANCHES:
        n = window // (2 * dil)
        L = S // dil
        assert L % ATT_QB == 0 and n * 2 == ATT_QB
        kw = min(ATT_KW, L)
        nbq = L // ATT_QB
        nvar = 1 if nbq == 1 else 3
        plan.append((dil, L, nbq, kw, base, nvar, n))
        base += nvar
    return tuple(plan), base


def _attn_bias_profiles(rel_bias, S):
    plan, nvar_total = _attn_plan(S)
    nbuckets, H = rel_bias.shape
    u = np.arange(ATT_PROF_W) - ATT_QB
    onehots, bands = [], []
    for dil, L, nbq, kw, base, nvar, n in plan:
        offs = [0] if nvar == 1 else [0, -n, -(kw - ATT_QB)]
        for off in offs:
            rel = off + u
            onehots.append(np.eye(nbuckets, dtype=np.float32)[_t5_bucket(rel * dil, nbuckets)])
            bands.append(np.abs(rel) <= n)
    onehot = jnp.asarray(np.stack(onehots))
    band = jnp.asarray(np.stack(bands))
    prof = jnp.einsum('vwn,nh->hvw', onehot, rel_bias.astype(F32), precision=HIGHEST)
    prof = jnp.where(band[None], prof, NEG)
    prof = prof.reshape(H // 2, 2, nvar_total, ATT_PROF_W)
    return jnp.transpose(prof, (0, 2, 1, 3)).reshape(H // 2, 2 * nvar_total, ATT_PROF_W)


def _attn_kernel(q_ref, k_ref, v_ref, qg_ref, kg_ref, prof_ref, o_ref,
                 qn_scr, kn_scr, v_scr, ob_scr, mb_scr, db_scr, bias_ref, *, plan, hd):
    S = q_ref.shape[0]
    lane = lax.broadcasted_iota(I32, (1, LANES), 1)
    left = lane < hd

    @pl.when(pl.program_id(1) == 0)
    def _():
        for row in range(prof_ref.shape[0]):
            rep = jnp.broadcast_to(prof_ref[row:row + 1, :], (ATT_QB, ATT_PROF_W))
            skew = pltpu.roll(rep, 0, 1, stride=1, stride_axis=0)
            bias_ref[row // 2, row % 2] = skew[:, ATT_QB:ATT_QB + ATT_KW]

    same_head = (lax.broadcasted_iota(I32, (2 * LANES, LANES), 0) % LANES) // hd == \
        lax.broadcasted_iota(I32, (2 * LANES, LANES), 1) // hd
    avg = jnp.where(same_head, 1.0 / hd, 0.0).astype(BF16)

    def headnorm(x, g):
        x2 = x * x
        hi = x2.astype(BF16)
        lo = (x2 - hi.astype(F32)).astype(BF16)
        ms = _dot(jnp.concatenate([hi, lo], axis=1), avg)
        return x * lax.rsqrt(ms + EPS) * g

    qn_scr[...] = headnorm(q_ref[...].astype(F32), qg_ref[...]) * (hd ** -0.5)
    kn_scr[...] = headnorm(k_ref[...].astype(F32), kg_ref[...])
    v_scr[...] = v_ref[...].astype(F32)

    for bi, (dil, L, nbq, kw, base, nvar, n) in enumerate(plan):
        ones = jnp.ones((kw, LANES), BF16)

        def body(t, carry, dil=dil, L=L, nbq=nbq, kw=kw, base=base, nvar=nvar, n=n, bi=bi, ones=ones):
            blocks = []
            for uu in range(ATT_UNROLL):
                idx = t * ATT_UNROLL + uu
                r = idx // nbq
                i = idx % nbq
                q0 = i * ATT_QB
                k0 = jnp.clip(q0 - n, 0, L - kw)
                var = base if nvar == 1 else base + jnp.where(i > 0, 1, 0) + jnp.where(i == nbq - 1, 1, 0)
                if dil == 1:
                    qrows = pl.ds(pl.multiple_of(q0, ATT_QB), ATT_QB)
                    krows = pl.ds(pl.multiple_of(k0, CHUNK), kw)
                else:
                    qrows = pl.ds(r + q0 * dil, ATT_QB, stride=dil)
                    krows = pl.ds(r + k0 * dil, kw, stride=dil)
                qb = qn_scr[qrows, :]
                q2 = jnp.concatenate([jnp.where(left, qb, 0.0), jnp.where(left, 0.0, qb)], axis=0).astype(BF16)
                blocks.append((qrows, krows, var, q2))
            def score(group):
                return [_dot_nt(q2, kn_scr[krows, :].astype(BF16)) for qrows, krows, var, q2 in group]

            def softmax(group, scores):
                probs, maxes = [], []
                for (qrows, krows, var, q2), s in zip(group, scores):
                    s = s + jnp.concatenate([bias_ref[var, 0][:, :kw], bias_ref[var, 1][:, :kw]], axis=0)
                    m = jnp.max(s, axis=-1, keepdims=True)
                    probs.append(jnp.exp(s - m).astype(BF16))
                    maxes.append(m)
                return probs, maxes

            def values(group, probs):
                return [_dot(p, jnp.concatenate([v_scr[krows, :].astype(BF16), ones], axis=1))
                        for (qrows, krows, var, q2), p in zip(group, probs)]

            def finish(group, maxes, outs):
                for (qrows, krows, var, q2), m, od in zip(group, maxes, outs):
                    mb = jnp.broadcast_to(m, (2 * ATT_QB, LANES))
                    ob_scr[bi, qrows, :] = jnp.where(left, od[:ATT_QB, :LANES], od[ATT_QB:, :LANES])
                    mb_scr[bi, qrows, :] = jnp.where(left, mb[:ATT_QB], mb[ATT_QB:])
                    db_scr[bi, qrows, :] = jnp.where(left, od[:ATT_QB, LANES:], od[ATT_QB:, LANES:])

            ga, gb = blocks[:ATT_UNROLL // 2], blocks[ATT_UNROLL // 2:]
            sa = score(ga)
            sb = score(gb)
            pa, ma = softmax(ga, sa)
            oa = values(ga, pa)
            pb, mbx = softmax(gb, sb)
            ob = values(gb, pb)
            finish(ga, ma, oa)
            finish(gb, mbx, ob)
            return carry

        assert (dil * nbq) % ATT_UNROLL == 0
        lax.fori_loop(0, dil * nbq // ATT_UNROLL, body, 0)

    nb = len(plan)
    mx = mb_scr[0]
    for bi in range(1, nb):
        mx = jnp.maximum(mx, mb_scr[bi])
    num = jnp.zeros((S, LANES), F32)
    den = jnp.zeros((S, LANES), F32)
    for bi in range(nb):
        w = jnp.exp(mb_scr[bi] - mx)
        num = num + w * ob_scr[bi]
        den = den + w * db_scr[bi]
    o_ref[...] = (num / den).astype(o_ref.dtype)


def _attention(proj, qg2, kg2, profiles, B, S, HA, hd):
    T = proj.shape[0]
    pairs = HA // 2
    da_blocks = HA * hd // LANES
    plan, nvar = _attn_plan(S)
    return pl.pallas_call(
        functools.partial(_attn_kernel, plan=plan, hd=hd),
        grid=(pairs, B),
        in_specs=[pl.BlockSpec((S, LANES), lambda p, b: (b, p)),
                  pl.BlockSpec((S, LANES), lambda p, b: (b, da_blocks + p)),
                  pl.BlockSpec((S, LANES), lambda p, b: (b, 2 * da_blocks + p)),
                  pl.BlockSpec((1, LANES), lambda p, b: (0, 0)),
                  pl.BlockSpec((1, LANES), lambda p, b: (0, 0)),
                  pl.BlockSpec((None, 2 * nvar, ATT_PROF_W), lambda p, b: (p, 0, 0))],
        out_specs=pl.BlockSpec((S, LANES), lambda p, b: (b, p)),
        out_shape=jax.ShapeDtypeStruct((T, HA * hd), BF16),
        scratch_shapes=[pltpu.VMEM((S, LANES), F32)] * 3 + [pltpu.VMEM((len(plan), S, LANES), F32)] * 3
        + [pltpu.VMEM((nvar, 2, ATT_QB, ATT_KW), F32)],
        compiler_params=_cparams(("arbitrary", "arbitrary"), 40),
    )(proj, proj, proj, qg2, kg2, profiles)


def _gdn_kernel(q_ref, k_ref, v_ref, z_ref, cw_ref, pack_ref, og_ref, o_ref,
                q_scr, k_scr, v_scr, xpad_scr, pk2_scr, sel_scr, u_scr, wq_scr, at_scr, kdt_scr, et_scr, oacc_scr,
                *, hb, hp):
    S = q_ref.shape[0]
    P2 = 2 * CHUNK
    W2 = 2 * LANES
    npair = S // P2
    hg = pl.program_id(1)
    gw = 4 * hb
    dk = LANES

    pad = SUBLANES
    for slot in range(2):
        xpad_scr[slot, 0:pad, :] = jnp.zeros((pad, LANES), F32)
        xpad_scr[slot, pad + S:, :] = jnp.zeros((pad, LANES), F32)

    seg = GDN_SEG if S % GDN_SEG == 0 else S

    def conv_silu_to(src_ref, lanes_j, j, which, dst_scr, l2norm, scale):
        xp = xpad_scr.at[(3 * j + which) % 2]
        for s0 in range(0, S, seg):
            xp[pad + s0:pad + s0 + seg, :] = src_ref[s0:s0 + seg, lanes_j].astype(F32)
        for s0 in range(0, S, seg):
            acc = xp[pad + s0:pad + s0 + seg, :] * cw_ref[j, which, 2:3, :]
            for d in (-2, -1, 1, 2):
                acc = acc + xp[pad + s0 + d:pad + s0 + d + seg, :] * cw_ref[j, which, 2 + d:3 + d, :]
            y = _silu(acc)
            if l2norm:
                y = y * (lax.rsqrt(jnp.sum(y * y, axis=-1, keepdims=True) + EPS) * scale)
            dst_scr[j, s0:s0 + seg, :] = y

    for s0 in range(0, S, seg):
        pk = pack_ref[s0:s0 + seg, :]
        p_hi = pk.astype(BF16)
        p_lo = (pk - p_hi.astype(F32)).astype(BF16)
        pk2_scr[s0:s0 + seg, :] = jnp.concatenate([p_hi, p_lo], axis=1)
    srow = lax.broadcasted_iota(I32, (W2, 4 * LANES), 0) % LANES
    scol = lax.broadcasted_iota(I32, (W2, 4 * LANES), 1) // LANES
    for j in range(hp):
        lanes_j = slice(j * LANES, (j + 1) * LANES)
        conv_silu_to(q_ref, lanes_j, j, 0, q_scr, True, dk ** -0.5)
        conv_silu_to(k_ref, lanes_j, j, 1, k_scr, True, 1.0)
        conv_silu_to(v_ref, lanes_j, j, 2, v_scr, False, 1.0)
        src = hg * hp + j + jnp.where(scol < 2, scol * hb, gw + 2 * hb + (scol - 2) * hb)
        sel_scr[j] = jnp.where(srow == src, 1.0, 0.0).astype(BF16)
        oacc_scr[j] = jnp.zeros((S, LANES), F32)

    r4 = lax.broadcasted_iota(I32, (CHUNK, W2), 0)
    l4 = lax.broadcasted_iota(I32, (CHUNK, W2), 1)
    c4 = l4 % CHUNK
    blk4 = l4 // CHUNK
    lo_half = (l4 % LANES) < CHUNK
    ahead = jnp.where(l4 >= LANES, r4 - c4, c4 - r4)
    incl = ahead <= 0
    strict = ahead < 0
    bd16 = (r4 // 16) == (c4 // 16)

    def squeeze(x):
        return jnp.where(lo_half, x[:CHUNK], x[CHUNK:])

    def unsqueeze(x):
        return jnp.concatenate([jnp.where(lo_half, x, jnp.zeros_like(x)),
                                jnp.where(lo_half, jnp.zeros_like(x), x)], axis=0)

    def mm4(a, b):
        rhs = jnp.concatenate([jnp.where(blk4 == g, b, 0.0) for g in range(4)], axis=0)
        return _dot(a.astype(BF16), rhs.astype(BF16))

    first = lax.broadcasted_iota(I32, (P2, 1), 0) < CHUNK
    zpair = jnp.zeros((P2, LANES), BF16)

    def bdiag(x):
        return jnp.concatenate([jnp.concatenate([x[:, :LANES], zpair], axis=1),
                                jnp.concatenate([zpair, x[:, LANES:]], axis=1)], axis=0)

    U = GDN_PREP_UNROLL if npair % GDN_PREP_UNROLL == 0 else 1

    def prep(t, carry):
        cx = []
        chains = [(t * U + u, j) for u in range(U) for j in range(hp)]
        bcs = [_dot(pk2_scr[pl.ds(pl.multiple_of(m * P2, P2), P2), :], sel_scr[j]) for m, j in chains]
        for (m, j), bc in zip(chains, bcs):
            rows = pl.ds(pl.multiple_of(m * P2, P2), P2)
            kp = k_scr[j, rows, :]
            qp = q_scr[j, rows, :]
            vp = v_scr[j, rows, :]
            gc2 = bc[:, 0:W2]
            beta2 = bc[:, W2:2 * W2]
            gcf, gcb = gc2[:, :LANES], gc2[:, LANES:]
            tot2 = jnp.concatenate([jnp.where(first, gcf[CHUNK - 1:CHUNK], gcf[P2 - 1:P2]),
                                    jnp.where(first, gcb[0:1], gcb[CHUNK:CHUNK + 1])], axis=1)
            egc2 = jnp.exp(gc2)
            k2 = jnp.concatenate([kp, kp], axis=1)
            kb2 = k2 * beta2
            vb2 = jnp.concatenate([vp, vp], axis=1) * beta2
            kbe2 = kb2 * egc2
            cx.append(dict(
                m=m, j=j, rows=rows, vb2=vb2, kbe2=kbe2,
                qeb=(jnp.concatenate([qp, qp], axis=1) * egc2).astype(BF16),
                kd2=k2 * jnp.exp(tot2 - gc2),
                et=jnp.exp(tot2),
                dec=jnp.exp(jnp.where(
                    incl, squeeze(gc2) - squeeze(jnp.concatenate([gcf.T, gcb.T], axis=1)), -jnp.inf)),
                stk=jnp.concatenate([kb2[:, :LANES], kb2[:, LANES:], qp], axis=0).astype(BF16),
                kpb=kp.astype(BF16)))

        g3s = [_dot_nt(c['stk'], c['kpb']) for c in cx]
        for c, g3 in zip(cx, g3s):
            lm = jnp.where(strict, squeeze(jnp.concatenate([g3[:P2], g3[P2:2 * P2]], axis=1)) * c['dec'], 0.0)
            attn = squeeze(jnp.concatenate([g3[2 * P2:], g3[2 * P2:]], axis=1)) * c['dec']
            c['attn2'] = unsqueeze(attn).astype(BF16)
            c['lbd'] = jnp.where(bd16, lm, 0.0)
            c['loff'] = lm - c['lbd']
        nn = [-c['lbd'] for c in cx]
        pw = [mm4(c['lbd'], c['lbd']) for c in cx]
        for rnd in range(3):
            prod = [mm4(a, p) for a, p in zip(nn, pw)]
            nxt = [mm4(p, p) for p in pw] if rnd < 2 else pw
            nn = [a + p + q for a, p, q in zip(nn, pw, prod)]
            pw = nxt
        mo = [c['loff'] + x for c, x in zip(cx, [mm4(a, c['loff']) for a, c in zip(nn, cx)])]
        m2 = [mm4(x, x) for x in mo]
        mn = [mm4(x, a) for x, a in zip(mo, nn)]
        xo = [a - x - y for a, x, y in zip(nn, mo, mn)]
        mx = [mm4(a, b) for a, b in zip(m2, xo)]
        toff = [a + b + q for a, b, q in zip(xo, m2, mx)]
        z4 = jnp.zeros((CHUNK, W2), BF16)
        tws = []
        for c, tf in zip(cx, toff):
            vbb, kbb = c['vb2'].astype(BF16), c['kbe2'].astype(BF16)
            blocks = []
            for g in range(4):
                rr = slice((g % 2) * CHUNK, (g % 2 + 1) * CHUNK)
                ll = slice((g // 2) * LANES, (g // 2 + 1) * LANES)
                blocks.append(jnp.concatenate([z4] * g + [vbb[rr, ll], kbb[rr, ll]] + [z4] * (3 - g), axis=1))
            tws.append(_dot(tf.astype(BF16), jnp.concatenate(blocks, axis=0)))
        for c, tw in zip(cx, tws):
            m, j, rows = c['m'], c['j'], c['rows']
            def pair_layout(off):
                piece = lambda g: tw[:, 2 * g * LANES + off:2 * g * LANES + off + LANES]
                return jnp.concatenate([jnp.concatenate([piece(0), piece(2)], axis=1),
                                        jnp.concatenate([piece(1), piece(3)], axis=1)], axis=0)

            u2 = c['vb2'] + pair_layout(0)
            w2 = (c['kbe2'] + pair_layout(LANES)).astype(BF16)
            qeb = c['qeb']
            u_scr[j, rows, :] = u2
            wq_scr[j, pl.ds(pl.multiple_of(m * 2 * P2, 2 * P2), 2 * P2), :] = jnp.concatenate(
                [w2[:CHUNK], qeb[:CHUNK], w2[CHUNK:], qeb[CHUNK:]], axis=0)
            at_scr[j, rows, :] = c['attn2']
            kdt_scr[j, 0, :, rows] = c['kd2'][:, :LANES].T.astype(BF16)
            kdt_scr[j, 1, :, rows] = c['kd2'][:, LANES:].T.astype(BF16)
            et_scr[j, pl.ds(pl.multiple_of(m * 2 * SUBLANES, 2 * SUBLANES), 2 * SUBLANES), :] = jnp.concatenate(
                [c['et'][:SUBLANES], c['et'][CHUNK:CHUNK + SUBLANES]], axis=0)
        return carry

    lax.fori_loop(0, npair // U, prep, 0)

    zc = jnp.zeros((CHUNK, LANES), F32)
    zp = jnp.zeros((P2, LANES), F32)

    def place(v, cpos):
        return jnp.concatenate([v, zc] if cpos == 0 else [zc, v], axis=0)

    def scan(m, states):
        pf = m
        pb = npair - 1 - m
        rows_f = pl.ds(pl.multiple_of(pf * P2, P2), P2)
        rows_b = pl.ds(pl.multiple_of(pb * P2, P2), P2)
        hx = []
        for j in range(hp):
            hx.append(dict(
                u_f=u_scr[j, rows_f, :LANES], u_b=u_scr[j, rows_b, LANES:],
                at_f=at_scr[j, rows_f, :LANES], at_b=at_scr[j, rows_b, LANES:],
                kdt=jnp.concatenate([kdt_scr[j, 0, :, rows_f], kdt_scr[j, 1, :, rows_b]], axis=1),
                wq_f=wq_scr[j, pl.ds(pl.multiple_of(pf * 2 * P2, 2 * P2), 2 * P2), :LANES],
                wq_b=wq_scr[j, pl.ds(pl.multiple_of(pb * 2 * P2, 2 * P2), 2 * P2), LANES:],
                et_f=et_scr[j, pl.ds(pl.multiple_of(pf * 2 * SUBLANES, 2 * SUBLANES), 2 * SUBLANES), :LANES],
                et_b=et_scr[j, pl.ds(pl.multiple_of(pb * 2 * SUBLANES, 2 * SUBLANES), 2 * SUBLANES), LANES:]))
        sts = list(states)
        for step in range(2):
            cf, cb = step, 1 - step
            rrs = [_dot(jnp.concatenate([c['wq_f'][cf * P2:(cf + 1) * P2], c['wq_b'][cb * P2:(cb + 1) * P2]], axis=1),
                        bdiag(st.astype(BF16))) for c, st in zip(hx, sts)]
            ress = []
            for c, rr in zip(hx, rrs):
                u2 = jnp.concatenate([c['u_f'][cf * CHUNK:(cf + 1) * CHUNK],
                                      c['u_b'][cb * CHUNK:(cb + 1) * CHUNK]], axis=1)
                v_new = u2 - rr[:CHUNK]
                rhs = jnp.concatenate(
                    [jnp.concatenate([place(v_new[:, :LANES], cf), zp], axis=1),
                     jnp.concatenate([zp, place(v_new[:, LANES:], cb)], axis=1)], axis=0).astype(BF16)
                lhs = jnp.concatenate(
                    [jnp.concatenate([c['at_f'][cf * CHUNK:(cf + 1) * CHUNK],
                                      c['at_b'][cb * CHUNK:(cb + 1) * CHUNK]], axis=1),
                     c['kdt']], axis=0)
                ress.append(_dot(lhs, rhs))
            for j, (c, rr, res) in enumerate(zip(hx, rrs, ress)):
                o2 = rr[CHUNK:] + res[:CHUNK]
                et2 = jnp.concatenate([c['et_f'][cf * SUBLANES:cf * SUBLANES + 1],
                                       c['et_b'][cb * SUBLANES:cb * SUBLANES + 1]], axis=1)
                sts[j] = sts[j] * et2 + res[CHUNK:]
                of_rows = pl.ds(pl.multiple_of(pf * P2 + cf * CHUNK, CHUNK), CHUNK)
                ob_rows = pl.ds(pl.multiple_of(pb * P2 + cb * CHUNK, CHUNK), CHUNK)
                oacc_scr[j, of_rows, :] = oacc_scr[j, of_rows, :] + o2[:, :LANES]
                oacc_scr[j, ob_rows, :] = oacc_scr[j, ob_rows, :] + o2[:, LANES:]
        return tuple(sts)

    s0 = jnp.zeros((dk, W2), F32)
    lax.fori_loop(0, npair, scan, (s0,) * hp)

    for j in range(hp):
        lanes_j = slice(j * LANES, (j + 1) * LANES)
        for s0 in range(0, S, seg):
            o = oacc_scr[j, s0:s0 + seg, :]
            y = o * lax.rsqrt(jnp.mean(o * o, axis=-1, keepdims=True) + EPS) * og_ref[...]
            z = z_ref[s0:s0 + seg, lanes_j].astype(F32)
            o_ref[s0:s0 + seg, lanes_j] = (y * _silu(z)).astype(o_ref.dtype)


def _gdn(proj, cw4, pack3, onorm_g, B, S, HB, base_blk):
    T = proj.shape[0]
    hp = GDN_HP if (HB % GDN_HP == 0 and base_blk % GDN_HP == 0) else 1
    wblk = hp * LANES
    npair = S // (2 * CHUNK)

    def col(k):
        off = (base_blk + k * HB) // hp
        return lambda b, h: (b, off + h)

    return pl.pallas_call(
        functools.partial(_gdn_kernel, hb=HB, hp=hp),
        grid=(B, HB // hp),
        in_specs=[pl.BlockSpec((S, wblk), col(0)),
                  pl.BlockSpec((S, wblk), col(1)),
                  pl.BlockSpec((S, wblk), col(2)),
                  pl.BlockSpec((S, wblk), col(3)),
                  pl.BlockSpec((hp, 3, cw4.shape[2], LANES), lambda b, h: (h, 0, 0, 0)),
                  pl.BlockSpec((None, S, LANES), lambda b, h: (b, 0, 0)),
                  pl.BlockSpec((1, LANES), lambda b, h: (0, 0))],
        out_specs=pl.BlockSpec((S, wblk), lambda b, h: (b, h)),
        out_shape=jax.ShapeDtypeStruct((T, HB * LANES), BF16),
        scratch_shapes=[pltpu.VMEM((hp, S, LANES), F32)] * 3
        + [pltpu.VMEM((2, S + 2 * SUBLANES, LANES), F32),
           pltpu.VMEM((S, 2 * LANES), BF16),
           pltpu.VMEM((hp, 2 * LANES, 4 * LANES), BF16),
           pltpu.VMEM((hp, S, 2 * LANES), F32),
           pltpu.VMEM((hp, 2 * S, 2 * LANES), BF16),
           pltpu.VMEM((hp, S, 2 * LANES), BF16),
           pltpu.VMEM((hp, 2, LANES, S), BF16),
           pltpu.VMEM((hp, npair * 2 * SUBLANES, 2 * LANES), F32),
           pltpu.VMEM((hp, S, LANES), F32)],
        compiler_params=_cparams(("arbitrary", "arbitrary"), 56),
    )(proj, proj, proj, proj, cw4, pack3, onorm_g)


def _outproj_kernel(oa_ref, ob_ref, x_ref, gt_ref, w_ref, g_ref, sc_ref, sh_ref, wr_ref, br_ref,
                    x1_ref, hp_ref, lg_ref, z_ref, h_scr):
    z_ref[...] = jnp.zeros(z_ref.shape, U32)
    da = oa_ref.shape[1]
    y = _dot(oa_ref[...], w_ref[:da, :]) + _dot(ob_ref[...], w_ref[da:, :])
    x1 = x_ref[...] + gt_ref[...] * y
    x1_ref[...] = x1
    hn = x1 * lax.rsqrt(jnp.mean(x1 * x1, axis=-1, keepdims=True) + EPS) * g_ref[...]
    h = hn * (1.0 + sc_ref[...]) + sh_ref[...]
    hp_ref[...] = _pack_halves(h)
    h_scr[...] = h

    @pl.when(pl.program_id(0) < pl.num_programs(0))
    def _():
        lg_ref[...] = _dot_hilo(h_scr[...], wr_ref[...]) + br_ref[...]


def _outproj(oa, ob, x2, gt1, w_out_b, g2, sc2, sh2, wr, br, S, p_rows):
    T, D = x2.shape
    tm = ROW_TILE
    per_b = S // tm
    bmap = lambda i: (i // per_b, 0, 0)
    zrows = p_rows // (T // tm)
    assert zrows * (T // tm) == p_rows and zrows % SUBLANES == 0
    return pl.pallas_call(
        _outproj_kernel,
        grid=(T // tm,),
        in_specs=[pl.BlockSpec((tm, oa.shape[1]), lambda i: (i, 0)),
                  pl.BlockSpec((tm, ob.shape[1]), lambda i: (i, 0)),
                  pl.BlockSpec((tm, D), lambda i: (i, 0)),
                  pl.BlockSpec((None, 1, D), bmap),
                  pl.BlockSpec((D, D), lambda i: (0, 0)),
                  pl.BlockSpec((1, D), lambda i: (0, 0)),
                  pl.BlockSpec((None, 1, D), bmap),
                  pl.BlockSpec((None, 1, D), bmap),
                  pl.BlockSpec((D, 2 * LANES), lambda i: (0, 0)),
                  pl.BlockSpec((1, LANES), lambda i: (0, 0))],
        out_specs=[pl.BlockSpec((tm, D), lambda i: (i, 0)),
                   pl.BlockSpec((tm, D // 2), lambda i: (i, 0)),
                   pl.BlockSpec((tm, LANES), lambda i: (i, 0)),
                   pl.BlockSpec((zrows, D // 2), lambda i: (i, 0))],
        out_shape=[jax.ShapeDtypeStruct((T, D), F32), jax.ShapeDtypeStruct((T, D // 2), U32),
                   jax.ShapeDtypeStruct((T, LANES), F32), jax.ShapeDtypeStruct((p_rows, D // 2), U32)],
        scratch_shapes=[pltpu.VMEM((tm, D), F32)],
        compiler_params=_cparams(("arbitrary",), 48),
    )(oa, ob, x2, gt1, w_out_b, g2, sc2, sh2, wr, br)


def _route_kernel(lg_ref, o_ref, info_ref, run_scr, *, ne, ng):
    ph = pl.program_id(0)
    i = pl.program_id(1)

    @pl.when((ph == 0) & (i == 0))
    def _():
        run_scr[...] = jnp.zeros_like(run_scr)

    @pl.when((ph == 1) & (i == 0))
    def _():
        cnt = run_scr[...]
        padded = jnp.ceil(cnt * (1.0 / MOE_RB)) * MOE_RB
        k = lax.broadcasted_iota(I32, (LANES, LANES), 0)
        e = lax.broadcasted_iota(I32, (LANES, LANES), 1)
        start = _dot(padded, jnp.where(k < e, 1.0, 0.0), precision=HIGHEST)
        rowi = lax.broadcasted_iota(I32, cnt.shape, 0)
        info_ref[...] = jnp.where(rowi == 0, cnt, jnp.where(rowi == 1, padded, start))
        run_scr[...] = start

    lg = lg_ref[...]
    tm = lg.shape[0]
    epg = ne // ng
    lane_i = lax.broadcasted_iota(I32, lg.shape, 1)
    lane = lane_i.astype(F32)
    big = float(2 * LANES)
    is_g = (lane_i >= ne) & (lane_i < ne + ng)
    gl = jnp.where(is_g, lg, -jnp.inf)
    gmax = jnp.max(gl, axis=-1, keepdims=True)
    gidx = jnp.min(jnp.where(gl == gmax, lane, big), axis=-1, keepdims=True) - ne
    psel = 1.0 / jnp.sum(jnp.where(is_g, jnp.exp(gl - gmax), 0.0), axis=-1, keepdims=True)
    in_grp = (lane_i // epg).astype(F32) == gidx
    el = jnp.where(in_grp & (lane_i < ne), lg, -jnp.inf)
    m1 = jnp.max(el, axis=-1, keepdims=True)
    i1 = jnp.min(jnp.where(el == m1, lane, big), axis=-1, keepdims=True)
    el2 = jnp.where(lane == i1, -jnp.inf, el)
    m2 = jnp.max(el2, axis=-1, keepdims=True)
    i2 = jnp.min(jnp.where(el2 == m2, lane, big), axis=-1, keepdims=True)
    e21 = jnp.exp(m2 - m1)
    g1 = psel / (1.0 + e21)
    g2 = psel * e21 / (1.0 + e21)
    o1 = jnp.where(lane == i1, 1.0, 0.0)
    o2 = jnp.where(lane == i2, 1.0, 0.0)
    cnt = o1 + o2
    r = lax.broadcasted_iota(I32, (tm, tm), 0)
    c = lax.broadcasted_iota(I32, (tm, tm), 1)
    before = _dot(jnp.where(c < r, 1.0, 0.0).astype(BF16), cnt.astype(BF16)) + run_scr[0:1, :]
    d1 = jnp.sum(o1 * before, axis=-1, keepdims=True)
    d2 = jnp.sum(o2 * before, axis=-1, keepdims=True)
    run_scr[...] = run_scr[...] + jnp.sum(cnt, axis=0, keepdims=True)

    @pl.when(ph == 1)
    def _():
        out = jnp.zeros(lg.shape, F32)
        for j, val in enumerate((i1, i2, g1, g2, d1, d2)):
            out = jnp.where(lane_i == j, val, out)
        o_ref[...] = out


def _route(logits, ne, ng):
    T = logits.shape[0]
    tm = _pick_tile(T, (1024, 512, 256))
    return pl.pallas_call(
        functools.partial(_route_kernel, ne=ne, ng=ng),
        grid=(2, T // tm),
        in_specs=[pl.BlockSpec((tm, LANES), lambda p, i: (i, 0))],
        out_specs=[pl.BlockSpec((tm, LANES), lambda p, i: (i * p, 0)),
                   pl.BlockSpec((SUBLANES, LANES), lambda p, i: (0, 0))],
        out_shape=[jax.ShapeDtypeStruct((T, LANES), F32), jax.ShapeDtypeStruct((SUBLANES, LANES), F32)],
        scratch_shapes=[pltpu.VMEM((SUBLANES, LANES), F32)],
        compiler_params=_cparams(("arbitrary", "arbitrary"), 16),
    )(logits)


def _dispatch_kernel(dest_ref, h_hbm, xs_in, xs_hbm, sem):
    del xs_in
    tm = ROW_TILE
    base = pl.program_id(0) * tm

    def copy(tok, a):
        return pltpu.make_async_copy(h_hbm.at[pl.ds(tok, 1)], xs_hbm.at[pl.ds(dest_ref[a], 1)], sem)

    def issue(t, carry):
        for k in range(TOP_K):
            copy(base + t, (base + t) * TOP_K + k).start()
        return carry

    lax.fori_loop(0, tm, issue, 0, unroll=DMA_UNROLL)

    for k in range(TOP_K):
        pltpu.make_async_copy(h_hbm.at[pl.ds(0, tm)], xs_hbm.at[pl.ds(0, tm)], sem).wait()


def _dispatch(dest, hpk, xs_zero):
    T, dh = hpk.shape
    p_rows = xs_zero.shape[0]
    return pl.pallas_call(
        _dispatch_kernel,
        grid_spec=pltpu.PrefetchScalarGridSpec(
            num_scalar_prefetch=1,
            grid=(T // ROW_TILE,),
            in_specs=[pl.BlockSpec(memory_space=pl.ANY), pl.BlockSpec(memory_space=pl.ANY)],
            out_specs=pl.BlockSpec(memory_space=pl.ANY),
            scratch_shapes=[pltpu.SemaphoreType.DMA(())]),
        out_shape=jax.ShapeDtypeStruct((p_rows, dh), U32),
        input_output_aliases={2: 0},
        compiler_params=_cparams(("arbitrary",), 16),
    )(dest, hpk, xs_zero)


def _moe_kernel(we_ref, ws_ref, wn_ref, wt_ref, xs_hbm, w1_hbm, w3_hbm, w2_hbm, y_hbm,
                xbuf, xlo, xhi, yacc, ypk, w1buf, w3buf, w2buf, sem_in, sem_out, sem_w):
    w = pl.program_id(0)
    c = pl.program_id(1)
    nw = pl.num_programs(0)
    nc = pl.num_programs(1)
    nrows = wn_ref[w]
    start = ws_ref[w]
    nblk = nrows // MOE_RB
    dh = xbuf.shape[1]
    cw = w1buf.shape[-1]

    g = w * nc + c

    def w_copies(step, do):
        item = step // nc
        chunk = step % nc

        @pl.when((item < nw) & (wn_ref[jnp.minimum(item, nw - 1)] > 0))
        def _():
            e = we_ref[item]
            cols = pl.ds(pl.multiple_of(chunk * cw, cw), cw)
            slot = step % MOE_WSLOTS
            do(pltpu.make_async_copy(w1_hbm.at[e, :, cols], w1buf.at[slot], sem_w.at[slot]))
            do(pltpu.make_async_copy(w3_hbm.at[e, :, cols], w3buf.at[slot], sem_w.at[slot]))
            do(pltpu.make_async_copy(w2_hbm.at[e, cols, :], w2buf.at[slot], sem_w.at[slot]))

    @pl.when(g == 0)
    def _():
        for ahead in range(MOE_WSLOTS - 1):
            w_copies(ahead, lambda cp: cp.start())

    w_copies(g + MOE_WSLOTS - 1, lambda cp: cp.start())
    w_copies(g, lambda cp: cp.wait())
    wslot = g % MOE_WSLOTS
    w1_ref, w3_ref, w2_ref = w1buf.at[wslot], w3buf.at[wslot], w2buf.at[wslot]

    def blk_rows(rb):
        return pl.ds(pl.multiple_of(rb * MOE_RB, MOE_RB), MOE_RB)

    def hbm_rows(item_start, rb):
        return pl.ds(pl.multiple_of(item_start + rb * MOE_RB, MOE_RB), MOE_RB)

    def in_copy(item_start, rb):
        return pltpu.make_async_copy(xs_hbm.at[hbm_rows(item_start, rb)], xbuf.at[blk_rows(rb)], sem_in)

    def out_copy(item_start, rb):
        return pltpu.make_async_copy(ypk.at[blk_rows(rb)], y_hbm.at[hbm_rows(item_start, rb)], sem_out)

    def each_block(n, fn):
        def body(rb, carry):
            fn(rb)
            return carry
        lax.fori_loop(0, n, body, 0)

    @pl.when(c == 0)
    def _load():
        @pl.when(w == 0)
        def _():
            each_block(nblk, lambda rb: in_copy(start, rb).start())

        each_block(nblk, lambda rb: in_copy(start, rb).wait())

        def unpack(rb):
            lo, hi = _unpack_halves(xbuf[blk_rows(rb), :])
            xlo[blk_rows(rb), :] = lo.astype(BF16)
            xhi[blk_rows(rb), :] = hi.astype(BF16)
            yacc[blk_rows(rb), :] = jnp.zeros((MOE_RB, 2 * dh), F32)

        each_block(nblk, unpack)

        @pl.when(w + 1 < nw)
        def _():
            nxt = ws_ref[w + 1]
            each_block(wn_ref[w + 1] // MOE_RB, lambda rb: in_copy(nxt, rb).start())

    @pl.when(nrows > 0)
    def _compute():
        def rows_block(row0, nr):
            rows = pl.ds(row0, nr)
            xl = xlo[rows, :]
            xh = xhi[rows, :]
            h1 = _dot(xl, w1_ref[:dh, :].astype(BF16)) + _dot(xh, w1_ref[dh:, :].astype(BF16))
            h3 = _dot(xl, w3_ref[:dh, :].astype(BF16)) + _dot(xh, w3_ref[dh:, :].astype(BF16))
            hid = (_silu(h1) * h3).astype(BF16)
            yacc[rows, :] = yacc[rows, :] + _dot(hid, w2_ref[...].astype(BF16))

        ntall = nrows // MOE_TALL
        each_block(ntall, lambda i: rows_block(pl.multiple_of(i * MOE_TALL, MOE_TALL), MOE_TALL))

        rem = nrows - ntall * MOE_TALL
        for nr in range(MOE_RB, MOE_TALL, MOE_RB):
            @pl.when(rem == nr)
            def _(nr=nr):
                rows_block(pl.multiple_of(ntall * MOE_TALL, MOE_TALL), nr)

    @pl.when(c == nc - 1)
    def _store():
        @pl.when(w > 0)
        def _():
            prev = ws_ref[w - 1]
            each_block(wn_ref[w - 1] // MOE_RB, lambda rb: out_copy(prev, rb).wait())

        def pack(rb):
            ypk[blk_rows(rb), :] = _pack_halves(yacc[blk_rows(rb), :])

        each_block(nblk, pack)
        each_block(nblk, lambda rb: out_copy(start, rb).start())

        @pl.when(w == nw - 1)
        def _():
            each_block(nblk, lambda rb: out_copy(start, rb).wait())

    @pl.when((w == nw - 1) & (c == nc - 1))
    def _zero_tail():
        ypk[blk_rows(0), :] = jnp.zeros((MOE_RB, dh), U32)
        first = wt_ref[0] // MOE_RB

        def tail_copy(b):
            return pltpu.make_async_copy(ypk.at[blk_rows(0)],
                                         y_hbm.at[pl.ds(pl.multiple_of(b * MOE_RB, MOE_RB), MOE_RB)], sem_out)

        def start(b, carry):
            tail_copy(b).start()
            return carry

        def wait(b, carry):
            tail_copy(b).wait()
            return carry

        lax.fori_loop(first, y_hbm.shape[0] // MOE_RB, start, 0)
        lax.fori_loop(first, y_hbm.shape[0] // MOE_RB, wait, 0)


def _moe(we, ws, wn, wt, xs, w1, w3, w2):
    P, dh = xs.shape
    D = 2 * dh
    NE, _, DE = w1.shape
    cw = min(MOE_CW, DE)
    nc = DE // cw
    nw = we.shape[0]
    return pl.pallas_call(
        _moe_kernel,
        grid_spec=pltpu.PrefetchScalarGridSpec(
            num_scalar_prefetch=4,
            grid=(nw, nc),
            in_specs=[pl.BlockSpec(memory_space=pl.ANY)] * 4,
            out_specs=pl.BlockSpec(memory_space=pl.ANY),
            scratch_shapes=[pltpu.VMEM((MOE_RMAX, dh), U32),
                            pltpu.VMEM((MOE_RMAX, dh), BF16),
                            pltpu.VMEM((MOE_RMAX, dh), BF16),
                            pltpu.VMEM((MOE_RMAX, D), F32),
                            pltpu.VMEM((MOE_RMAX, dh), U32),
                            pltpu.VMEM((MOE_WSLOTS, D, cw), F32),
                            pltpu.VMEM((MOE_WSLOTS, D, cw), F32),
                            pltpu.VMEM((MOE_WSLOTS, cw, D), F32),
                            pltpu.SemaphoreType.DMA(()),
                            pltpu.SemaphoreType.DMA(()),
                            pltpu.SemaphoreType.DMA((MOE_WSLOTS,))]),
        out_shape=jax.ShapeDtypeStruct((P, dh), U32),
        compiler_params=_cparams(("arbitrary", "arbitrary"), 56),
    )(we, ws, wn, wt, xs, w1, w3, w2)


def _moe_schedule(info, NE, p_rows):
    padded = info[1, :NE].astype(I32)
    start_pad = info[2, :NE].astype(I32)
    items = (padded + MOE_RMAX - 1) // MOE_RMAX
    cum_items = jnp.cumsum(items)
    n_items = cum_items[-1]
    nw = (p_rows + NE * (MOE_RMAX - MOE_RB)) // MOE_RMAX
    wi = jnp.arange(nw, dtype=I32)
    valid = wi < n_items
    wi_c = jnp.minimum(wi, jnp.maximum(n_items - 1, 0))
    we = jnp.minimum(jnp.sum(cum_items[None, :] <= wi_c[:, None], axis=1), NE - 1).astype(I32)
    onehot = we[:, None] == jnp.arange(NE, dtype=I32)[None, :]
    pick = lambda v: jnp.sum(jnp.where(onehot, v[None, :], 0), axis=1)
    local = wi_c - (pick(cum_items) - pick(items))
    ws = (pick(start_pad) + local * MOE_RMAX).astype(I32)
    wn = jnp.where(valid, jnp.clip(pick(padded) - local * MOE_RMAX, 0, MOE_RMAX), 0).astype(I32)
    wt = (start_pad[NE - 1] + padded[NE - 1]).reshape(1)
    return we, ws, wn, wt


def _combine_kernel(dest_ref, x1_ref, rt_ref, gt_ref, y_hbm, o_ref, ybuf, sems):
    tm = ROW_TILE
    i = pl.program_id(0)
    n = pl.num_programs(0)
    dh = ybuf.shape[-1]

    def copy(tile, slot, t, k):
        a = (tile * tm + t) * TOP_K + k
        return pltpu.make_async_copy(y_hbm.at[pl.ds(dest_ref[a], 1)], ybuf.at[slot, k, pl.ds(t, 1)], sems.at[slot])

    def issue_tile(tile, slot):
        def body(t, carry):
            for k in range(TOP_K):
                copy(tile, slot, t, k).start()
            return carry
        lax.fori_loop(0, tm, body, 0, unroll=DMA_UNROLL)

    @pl.when(i == 0)
    def _():
        issue_tile(0, 0)

    @pl.when(i + 1 < n)
    def _():
        issue_tile(i + 1, (i + 1) % 2)

    slot = i % 2

    for k in range(TOP_K):
        pltpu.make_async_copy(y_hbm.at[pl.ds(0, tm)], ybuf.at[slot, k], sems.at[slot]).wait()

    rt = rt_ref[...]
    g1 = rt[:, 2:3]
    g2 = rt[:, 3:4]
    lo1, hi1 = _unpack_halves(ybuf[slot, 0])
    lo2, hi2 = _unpack_halves(ybuf[slot, 1])
    o_ref[:, :dh] = x1_ref[:, :dh] + gt_ref[:, :dh] * (g1 * lo1 + g2 * lo2)
    o_ref[:, dh:] = x1_ref[:, dh:] + gt_ref[:, dh:] * (g1 * hi1 + g2 * hi2)


def _combine(dest, x1, route, gt2, ypk, S):
    T, D = x1.shape
    tm = ROW_TILE
    per_b = S // tm
    return pl.pallas_call(
        _combine_kernel,
        grid_spec=pltpu.PrefetchScalarGridSpec(
            num_scalar_prefetch=1,
            grid=(T // tm,),
            in_specs=[pl.BlockSpec((tm, D), lambda i, d: (i, 0)),
                      pl.BlockSpec((tm, LANES), lambda i, d: (i, 0)),
                      pl.BlockSpec((None, 1, D), lambda i, d: (i // per_b, 0, 0)),
                      pl.BlockSpec(memory_space=pl.ANY)],
            out_specs=pl.BlockSpec((tm, D), lambda i, d: (i, 0)),
            scratch_shapes=[pltpu.VMEM((2, TOP_K, tm, D // 2), U32),
                            pltpu.SemaphoreType.DMA((2,))]),
        out_shape=jax.ShapeDtypeStruct((T, D), F32),
        compiler_params=_cparams(("arbitrary",), 32),
    )(dest, x1, route, gt2, ypk)


def kernel(x, c, w_ada, b_ada, norm1_g, norm2_g, w_in, qn_g, kn_g, rel_bias, conv_w, A_log, dt_bias,
           onorm_g, w_out, w_rg, b_rg, w_re, b_re, w1, w3, w2):
    B, S, D = x.shape
    depth = w_ada.shape[0]
    HA, hda = rel_bias.shape[1], qn_g.shape[-1]
    HB, hdb = A_log.shape[-1], onorm_g.shape[-1]
    DA, DB = HA * hda, HB * hdb
    NG, NE = w_rg.shape[-1], w_re.shape[-1]
    T = B * S
    assert hdb == LANES and 2 * hda == LANES and DA + DB == D and 16 * HB <= LANES
    assert NE + NG <= LANES and conv_w.shape[1] == 5 and S % ROW_TILE == 0
    n_main = 3 * DA + 4 * DB
    p_rows = TOP_K * T + NE * MOE_RB

    bias_prof = _attn_bias_profiles(rel_bias, S)
    w_in_t = jnp.swapaxes(w_in, 1, 2)
    x2 = x.reshape(T, D)
    for l in range(depth):
        mod = _ada(c, w_ada[l], b_ada[l]).reshape(B, 6, 1, D)
        sh1, sc1, gt1, sh2, sc2, gt2 = (mod[:, i] for i in range(6))

        w_gate = jnp.tile(w_in_t[l][n_main:, :].T, (1, 4))
        w_gate = _hilo_weights(jnp.pad(w_gate, ((0, 0), (0, LANES - w_gate.shape[1]))))
        proj, gat = _inproj(x2, norm1_g[l].reshape(1, D), sc1, sh1, w_in_t[l], n_main, w_gate, S)

        def gate_row(p):
            grp = jnp.concatenate([p.reshape(-1), jnp.zeros((2 * HB,), F32)])
            return jnp.pad(jnp.tile(grp, 4), (0, LANES - 16 * HB)).reshape(1, LANES)

        pack = _gating(gat, gate_row(A_log[l]), gate_row(dt_bias[l]), HB)

        oa = _attention(proj, jnp.tile(qn_g[l], 2).reshape(1, LANES), jnp.tile(kn_g[l], 2).reshape(1, LANES),
                        bias_prof, B, S, HA, hda)
        cw4 = jnp.transpose(conv_w[l].reshape(conv_w.shape[1], 3, HB, hdb), (2, 1, 0, 3))
        ob = _gdn(proj, cw4, pack.reshape(B, S, LANES), onorm_g[l].reshape(1, LANES), B, S, HB, 3 * DA // LANES)

        wr = _hilo_weights(jnp.pad(jnp.concatenate([w_re[l], w_rg[l]], axis=1), ((0, 0), (0, LANES - NE - NG))))
        br = jnp.pad(jnp.concatenate([b_re[l], b_rg[l]]), (0, LANES - NE - NG)).reshape(1, LANES)
        x1, hpk, logits, xs_zero = _outproj(oa, ob, x2, gt1, w_out[l].astype(BF16), norm2_g[l].reshape(1, D),
                                            sc2, sh2, wr, br, S, p_rows)
        route, info = _route(logits, NE, NG)
        dest = route[:, 4:4 + TOP_K].astype(I32).reshape(TOP_K * T)
        we, ws, wn, wt = _moe_schedule(info, NE, p_rows)
        xs = _dispatch(dest, hpk, xs_zero)
        ypk = _moe(we, ws, wn, wt, xs, w1[l], w3[l], w2[l])
        x2 = _combine(dest, x1, route, gt2, ypk, S)
    return x2.reshape(B, S, D)
```

```python
import functools

import numpy as np
import jax
import jax.numpy as jnp
from jax import lax
from jax.experimental import pallas as pl
from jax.experimental.pallas import tpu as pltpu

F32 = jnp.float32
BF16 = jnp.bfloat16
I32 = jnp.int32
U32 = jnp.uint32
HIGHEST = lax.Precision.HIGHEST

EPS = 1e-6
NEG = -1e30
DILATED_BRANCHES = ((128, 1), (512, 4), (2048, 16))
REL_MAX_DIST = 1024
CHUNK = 64
TOP_K = 2

LANES = 128
SUBLANES = 8
MIB = 1 << 20

ATT_QB = 128
ATT_KW = 256
ATT_UNROLL = 8
ATT_PROF_W = 512
GDN_HP = 2
GDN_PREP_UNROLL = 4
GDN_SEG = 512
MOE_RB = 128
MOE_TALL = 512
MOE_RMAX = 512
MOE_WSLOTS = 3
MOE_CW = 512
ROW_TILE = 256
DMA_UNROLL = 8


def _cparams(sem, vmem_mib):
    return pltpu.CompilerParams(dimension_semantics=sem, vmem_limit_bytes=vmem_mib * MIB)


def _dot(a, b, **kw):
    return jnp.dot(a, b, preferred_element_type=F32, **kw)


def _dot_nt(a, b):
    return lax.dot_general(a, b, (((1,), (1,)), ((), ())), preferred_element_type=F32)


def _pick_tile(n, prefs):
    for t in prefs:
        if n % t == 0:
            return t
    return n


def _pack_halves(x):
    half = x.shape[1] // 2
    lo = lax.bitcast_convert_type(x[:, :half].astype(BF16).astype(F32), U32)
    hi = lax.bitcast_convert_type(x[:, half:].astype(BF16).astype(F32), U32)
    return lax.shift_right_logical(lo, jnp.uint32(16)) | (hi & jnp.uint32(0xFFFF0000))


def _silu(x):
    h = 0.5 * x
    return h + h * jnp.tanh(h)


def _hilo_weights(w):
    hi = w.astype(BF16)
    lo = (w - hi.astype(F32)).astype(BF16)
    return jnp.concatenate([hi, lo], axis=1)


def _dot_hilo(x, w2):
    m, n = x.shape[0], w2.shape[1] // 2
    hi = x.astype(BF16)
    lo = (x - hi.astype(F32)).astype(BF16)
    r = _dot(jnp.concatenate([hi, lo], axis=0), w2)
    return r[:m, :n] + (r[:m, n:] + r[m:, :n])


def _unpack_halves(p):
    lo = lax.bitcast_convert_type(lax.shift_left(p, jnp.uint32(16)), F32)
    hi = lax.bitcast_convert_type(p & jnp.uint32(0xFFFF0000), F32)
    return lo, hi


def _ada_kernel(c_ref, w_ref, b_ref, o_ref):
    c = c_ref[...]
    s = _silu(c).astype(BF16)
    o_ref[...] = _dot(s, w_ref[...].astype(BF16)) + b_ref[...]


def _ada(c, w_ada, b_ada):
    B, D = c.shape
    N = w_ada.shape[1]
    tn = _pick_tile(N, (1024, 512, 256, 128))
    return pl.pallas_call(
        _ada_kernel,
        grid=(N // tn,),
        in_specs=[pl.BlockSpec((B, D), lambda j: (0, 0)),
                  pl.BlockSpec((D, tn), lambda j: (0, j)),
                  pl.BlockSpec((1, tn), lambda j: (0, j))],
        out_specs=pl.BlockSpec((B, tn), lambda j: (0, j)),
        out_shape=jax.ShapeDtypeStruct((B, N), F32),
        compiler_params=_cparams(("arbitrary",), 40),
    )(c, w_ada, b_ada.reshape(1, N))


def _prenorm_kernel(x_ref, g_ref, sc_ref, sh_ref, wg_ref, h_ref, og_ref):
    x = x_ref[...]
    y = x * lax.rsqrt(jnp.mean(x * x, axis=-1, keepdims=True) + EPS) * g_ref[...]
    h = y * (1.0 + sc_ref[...]) + sh_ref[...]
    h_ref[...] = h.astype(BF16)
    og_ref[...] = _dot_hilo(h, wg_ref[...])


def _inproj_kernel(h_ref, w_ref, o_ref, wb_scr):
    @pl.when(pl.program_id(1) == 0)
    def _():
        wb_scr[...] = w_ref[...].astype(BF16)

    o_ref[...] = _dot_nt(h_ref[...], wb_scr[...]).astype(o_ref.dtype)


def _inproj(x2, g, sc, sh, w_all, n_main, w_gate, S):
    T, D = x2.shape
    NM = n_main
    tp = _pick_tile(S, (512, 256, 128))
    per_b = S // tp
    h, gat = pl.pallas_call(
        _prenorm_kernel,
        grid=(T // tp,),
        in_specs=[pl.BlockSpec((tp, D), lambda i: (i, 0)),
                  pl.BlockSpec((1, D), lambda i: (0, 0)),
                  pl.BlockSpec((None, 1, D), lambda i: (i // per_b, 0, 0)),
                  pl.BlockSpec((None, 1, D), lambda i: (i // per_b, 0, 0)),
                  pl.BlockSpec((D, 2 * LANES), lambda i: (0, 0))],
        out_specs=[pl.BlockSpec((tp, D), lambda i: (i, 0)),
                   pl.BlockSpec((tp, LANES), lambda i: (i, 0))],
        out_shape=[jax.ShapeDtypeStruct((T, D), BF16), jax.ShapeDtypeStruct((T, LANES), F32)],
        compiler_params=_cparams(("arbitrary",), 40),
    )(x2, g, sc, sh, w_gate)
    tm = _pick_tile(T, (1024, 512, 256, 128))
    tn = _pick_tile(NM, (1024, 512, 256, 128))
    proj = pl.pallas_call(
        _inproj_kernel,
        grid=(NM // tn, T // tm),
        in_specs=[pl.BlockSpec((tm, D), lambda j, i: (i, 0)),
                  pl.BlockSpec((tn, D), lambda j, i: (j, 0))],
        out_specs=pl.BlockSpec((tm, tn), lambda j, i: (i, j)),
        out_shape=jax.ShapeDtypeStruct((T, NM), BF16),
        scratch_shapes=[pltpu.VMEM((tn, D), BF16)],
        compiler_params=_cparams(("arbitrary", "arbitrary"), 48),
    )(h, w_all)
    return proj, gat


def _gating_kernel(gat_ref, a_ref, dt_ref, o_ref, *, hb):
    gat = gat_ref[...]
    tm = gat.shape[0]
    gw = 4 * hb
    g = -jnp.exp(a_ref[...]) * jax.nn.softplus(gat + dt_ref[...])
    beta = jax.nn.sigmoid(gat)
    r = lax.broadcasted_iota(I32, (tm, tm), 0)
    c = lax.broadcasted_iota(I32, (tm, tm), 1)
    same = (r // CHUNK) == (c // CHUNK)
    pre = _dot(jnp.where(same & (c <= r), 1.0, 0.0), g, precision=HIGHEST)
    suf = _dot(jnp.where(same & (c >= r), 1.0, 0.0), g, precision=HIGHEST)
    lane = lax.broadcasted_iota(I32, gat.shape, 1)
    o_ref[...] = jnp.where(lane // gw == 0, jnp.where(lane % gw < hb, pre, suf), beta)


def _gating(gat, a_row, dt_row, hb):
    T = gat.shape[0]
    tm = ROW_TILE
    return pl.pallas_call(
        functools.partial(_gating_kernel, hb=hb),
        grid=(T // tm,),
        in_specs=[pl.BlockSpec((tm, LANES), lambda i: (i, 0)),
                  pl.BlockSpec((1, LANES), lambda i: (0, 0)),
                  pl.BlockSpec((1, LANES), lambda i: (0, 0))],
        out_specs=pl.BlockSpec((tm, LANES), lambda i: (i, 0)),
        out_shape=jax.ShapeDtypeStruct((T, LANES), F32),
        compiler_params=_cparams(("arbitrary",), 16),
    )(gat, a_row, dt_row)


def _t5_bucket(rel, n_buckets):
    half = n_buckets // 2
    max_exact = half // 2
    n = np.abs(rel)
    large = max_exact + (np.log(np.maximum(n, 1) / max_exact) / np.log(REL_MAX_DIST / max_exact)
                         * (half - max_exact)).astype(np.int32)
    large = np.minimum(large, half - 1)
    return (np.where(rel > 0, half, 0) + np.where(n < max_exact, n, large)).astype(np.int32)


def _attn_plan(S):
    plan, base = [], 0
    for window, dil in DILATED_BRANCHES:
        n = window // (2 * dil)
        L = S // dil
        assert L % ATT_QB == 0 and n * 2 == ATT_QB
        kw = min(ATT_KW, L)
        nbq = L // ATT_QB
        nvar = 1 if nbq == 1 else 3
        plan.append((dil, L, nbq, kw, base, nvar, n))
        base += nvar
    return tuple(plan), base


def _attn_bias_profiles(rel_bias, S):
    plan, nvar_total = _attn_plan(S)
    nbuckets, H = rel_bias.shape
    u = np.arange(ATT_PROF_W) - ATT_QB
    onehots, bands = [], []
    for dil, L, nbq, kw, base, nvar, n in plan:
        offs = [0] if nvar == 1 else [0, -n, -(kw - ATT_QB)]
        for off in offs:
            rel = off + u
            onehots.append(np.eye(nbuckets, dtype=np.float32)[_t5_bucket(rel * dil, nbuckets)])
            bands.append(np.abs(rel) <= n)
    onehot = jnp.asarray(np.stack(onehots))
    band = jnp.asarray(np.stack(bands))
    prof = jnp.einsum('vwn,nh->hvw', onehot, rel_bias.astype(F32), precision=HIGHEST)
    prof = jnp.where(band[None], prof, NEG)
    prof = prof.reshape(H // 2, 2, nvar_total, ATT_PROF_W)
    return jnp.transpose(prof, (0, 2, 1, 3)).reshape(H // 2, 2 * nvar_total, ATT_PROF_W)


def _attn_kernel(q_ref, k_ref, v_ref, qg_ref, kg_ref, prof_ref, o_ref,
                 qn_scr, kn_scr, v_scr, ob_scr, mb_scr, db_scr, bias_ref, *, plan, hd):
    S = q_ref.shape[0]
    lane = lax.broadcasted_iota(I32, (1, LANES), 1)
    left = lane < hd

    @pl.when(pl.program_id(1) == 0)
    def _():
        for row in range(prof_ref.shape[0]):
            rep = jnp.broadcast_to(prof_ref[row:row + 1, :], (ATT_QB, ATT_PROF_W))
            skew = pltpu.roll(rep, 0, 1, stride=1, stride_axis=0)
            bias_ref[row // 2, row % 2] = skew[:, ATT_QB:ATT_QB + ATT_KW]

    same_head = (lax.broadcasted_iota(I32, (2 * LANES, LANES), 0) % LANES) // hd == \
        lax.broadcasted_iota(I32, (2 * LANES, LANES), 1) // hd
    avg = jnp.where(same_head, 1.0 / hd, 0.0).astype(BF16)

    def headnorm(x, g):
        x2 = x * x
        hi = x2.astype(BF16)
        lo = (x2 - hi.astype(F32)).astype(BF16)
        ms = _dot(jnp.concatenate([hi, lo], axis=1), avg)
        return x * lax.rsqrt(ms + EPS) * g

    qn_scr[...] = headnorm(q_ref[...].astype(F32), qg_ref[...]) * (hd ** -0.5)
    kn_scr[...] = headnorm(k_ref[...].astype(F32), kg_ref[...])
    v_scr[...] = v_ref[...].astype(F32)

    for bi, (dil, L, nbq, kw, base, nvar, n) in enumerate(plan):
        ones = jnp.ones((kw, LANES), BF16)

        def body(t, carry, dil=dil, L=L, nbq=nbq, kw=kw, base=base, nvar=nvar, n=n, bi=bi, ones=ones):
            blocks = []
            for uu in range(ATT_UNROLL):
                idx = t * ATT_UNROLL + uu
                r = idx // nbq
                i = idx % nbq
                q0 = i * ATT_QB
                k0 = jnp.clip(q0 - n, 0, L - kw)
                var = base if nvar == 1 else base + jnp.where(i > 0, 1, 0) + jnp.where(i == nbq - 1, 1, 0)
                if dil == 1:
                    qrows = pl.ds(pl.multiple_of(q0, ATT_QB), ATT_QB)
                    krows = pl.ds(pl.multiple_of(k0, CHUNK), kw)
                else:
                    qrows = pl.ds(r + q0 * dil, ATT_QB, stride=dil)
                    krows = pl.ds(r + k0 * dil, kw, stride=dil)
                qb = qn_scr[qrows, :]
                q2 = jnp.concatenate([jnp.where(left, qb, 0.0), jnp.where(left, 0.0, qb)], axis=0).astype(BF16)
                blocks.append((qrows, krows, var, q2))
            def score(group):
                return [_dot_nt(q2, kn_scr[krows, :].astype(BF16)) for qrows, krows, var, q2 in group]

            def softmax(group, scores):
                probs, maxes = [], []
                for (qrows, krows, var, q2), s in zip(group, scores):
                    s = s + jnp.concatenate([bias_ref[var, 0][:, :kw], bias_ref[var, 1][:, :kw]], axis=0)
                    m = jnp.max(s, axis=-1, keepdims=True)
                    probs.append(jnp.exp(s - m).astype(BF16))
                    maxes.append(m)
                return probs, maxes

            def values(group, probs):
                return [_dot(p, jnp.concatenate([v_scr[krows, :].astype(BF16), ones], axis=1))
                        for (qrows, krows, var, q2), p in zip(group, probs)]

            def finish(group, maxes, outs):
                for (qrows, krows, var, q2), m, od in zip(group, maxes, outs):
                    mb = jnp.broadcast_to(m, (2 * ATT_QB, LANES))
                    ob_scr[bi, qrows, :] = jnp.where(left, od[:ATT_QB, :LANES], od[ATT_QB:, :LANES])
                    mb_scr[bi, qrows, :] = jnp.where(left, mb[:ATT_QB], mb[ATT_QB:])
                    db_scr[bi, qrows, :] = jnp.where(left, od[:ATT_QB, LANES:], od[ATT_QB:, LANES:])

            ga, gb = blocks[:ATT_UNROLL // 2], blocks[ATT_UNROLL // 2:]
            sa = score(ga)
            sb = score(gb)
            pa, ma = softmax(ga, sa)
            oa = values(ga, pa)
            pb, mbx = softmax(gb, sb)
            ob = values(gb, pb)
            finish(ga, ma, oa)
            finish(gb, mbx, ob)
            return carry

        assert (dil * nbq) % ATT_UNROLL == 0
        lax.fori_loop(0, dil * nbq // ATT_UNROLL, body, 0)

    nb = len(plan)
    mx = mb_scr[0]
    for bi in range(1, nb):
        mx = jnp.maximum(mx, mb_scr[bi])
    num = jnp.zeros((S, LANES), F32)
    den = jnp.zeros((S, LANES), F32)
    for bi in range(nb):
        w = jnp.exp(mb_scr[bi] - mx)
        num = num + w * ob_scr[bi]
        den = den + w * db_scr[bi]
    o_ref[...] = (num / den).astype(o_ref.dtype)


def _attention(proj, qg2, kg2, profiles, B, S, HA, hd):
    T = proj.shape[0]
    pairs = HA // 2
    da_blocks = HA * hd // LANES
    plan, nvar = _attn_plan(S)
    return pl.pallas_call(
        functools.partial(_attn_kernel, plan=plan, hd=hd),
        grid=(pairs, B),
        in_specs=[pl.BlockSpec((S, LANES), lambda p, b: (b, p)),
                  pl.BlockSpec((S, LANES), lambda p, b: (b, da_blocks + p)),
                  pl.BlockSpec((S, LANES), lambda p, b: (b, 2 * da_blocks + p)),
                  pl.BlockSpec((1, LANES), lambda p, b: (0, 0)),
                  pl.BlockSpec((1, LANES), lambda p, b: (0, 0)),
                  pl.BlockSpec((None, 2 * nvar, ATT_PROF_W), lambda p, b: (p, 0, 0))],
        out_specs=pl.BlockSpec((S, LANES), lambda p, b: (b, p)),
        out_shape=jax.ShapeDtypeStruct((T, HA * hd), BF16),
        scratch_shapes=[pltpu.VMEM((S, LANES), F32)] * 3 + [pltpu.VMEM((len(plan), S, LANES), F32)] * 3
        + [pltpu.VMEM((nvar, 2, ATT_QB, ATT_KW), F32)],
        compiler_params=_cparams(("arbitrary", "arbitrary"), 40),
    )(proj, proj, proj, qg2, kg2, profiles)


def _gdn_kernel(q_ref, k_ref, v_ref, z_ref, cw_ref, pack_ref, og_ref, o_ref,
                q_scr, k_scr, v_scr, xpad_scr, pk2_scr, sel_scr, u_scr, wq_scr, at_scr, kdt_scr, et_scr, oacc_scr,
                *, hb, hp):
    S = q_ref.shape[0]
    P2 = 2 * CHUNK
    W2 = 2 * LANES
    npair = S // P2
    hg = pl.program_id(1)
    gw = 4 * hb
    dk = LANES

    pad = SUBLANES
    for slot in range(2):
        xpad_scr[slot, 0:pad, :] = jnp.zeros((pad, LANES), F32)
        xpad_scr[slot, pad + S:, :] = jnp.zeros((pad, LANES), F32)

    seg = GDN_SEG if S % GDN_SEG == 0 else S

    def conv_silu_to(src_ref, lanes_j, j, which, dst_scr, l2norm, scale):
        xp = xpad_scr.at[(3 * j + which) % 2]
        for s0 in range(0, S, seg):
            xp[pad + s0:pad + s0 + seg, :] = src_ref[s0:s0 + seg, lanes_j].astype(F32)
        for s0 in range(0, S, seg):
            acc = xp[pad + s0:pad + s0 + seg, :] * cw_ref[j, which, 2:3, :]
            for d in (-2, -1, 1, 2):
                acc = acc + xp[pad + s0 + d:pad + s0 + d + seg, :] * cw_ref[j, which, 2 + d:3 + d, :]
            y = _silu(acc)
            if l2norm:
                y = y * (lax.rsqrt(jnp.sum(y * y, axis=-1, keepdims=True) + EPS) * scale)
            dst_scr[j, s0:s0 + seg, :] = y

    for s0 in range(0, S, seg):
        pk = pack_ref[s0:s0 + seg, :]
        p_hi = pk.astype(BF16)
        p_lo = (pk - p_hi.astype(F32)).astype(BF16)
        pk2_scr[s0:s0 + seg, :] = jnp.concatenate([p_hi, p_lo], axis=1)
    srow = lax.broadcasted_iota(I32, (W2, 4 * LANES), 0) % LANES
    scol = lax.broadcasted_iota(I32, (W2, 4 * LANES), 1) // LANES
    for j in range(hp):
        lanes_j = slice(j * LANES, (j + 1) * LANES)
        conv_silu_to(q_ref, lanes_j, j, 0, q_scr, True, dk ** -0.5)
        conv_silu_to(k_ref, lanes_j, j, 1, k_scr, True, 1.0)
        conv_silu_to(v_ref, lanes_j, j, 2, v_scr, False, 1.0)
        src = hg * hp + j + jnp.where(scol < 2, scol * hb, gw + 2 * hb + (scol - 2) * hb)
        sel_scr[j] = jnp.where(srow == src, 1.0, 0.0).astype(BF16)
        oacc_scr[j] = jnp.zeros((S, LANES), F32)

    r4 = lax.broadcasted_iota(I32, (CHUNK, W2), 0)
    l4 = lax.broadcasted_iota(I32, (CHUNK, W2), 1)
    c4 = l4 % CHUNK
    blk4 = l4 // CHUNK
    lo_half = (l4 % LANES) < CHUNK
    ahead = jnp.where(l4 >= LANES, r4 - c4, c4 - r4)
    incl = ahead <= 0
    strict = ahead < 0
    bd16 = (r4 // 16) == (c4 // 16)

    def squeeze(x):
        return jnp.where(lo_half, x[:CHUNK], x[CHUNK:])

    def unsqueeze(x):
        return jnp.concatenate([jnp.where(lo_half, x, jnp.zeros_like(x)),
                                jnp.where(lo_half, jnp.zeros_like(x), x)], axis=0)

    def mm4(a, b):
        rhs = jnp.concatenate([jnp.where(blk4 == g, b, 0.0) for g in range(4)], axis=0)
        return _dot(a.astype(BF16), rhs.astype(BF16))

    first = lax.broadcasted_iota(I32, (P2, 1), 0) < CHUNK
    zpair = jnp.zeros((P2, LANES), BF16)

    def bdiag(x):
        return jnp.concatenate([jnp.concatenate([x[:, :LANES], zpair], axis=1),
                                jnp.concatenate([zpair, x[:, LANES:]], axis=1)], axis=0)

    U = GDN_PREP_UNROLL if npair % GDN_PREP_UNROLL == 0 else 1

    def prep(t, carry):
        cx = []
        chains = [(t * U + u, j) for u in range(U) for j in range(hp)]
        bcs = [_dot(pk2_scr[pl.ds(pl.multiple_of(m * P2, P2), P2), :], sel_scr[j]) for m, j in chains]
        for (m, j), bc in zip(chains, bcs):
            rows = pl.ds(pl.multiple_of(m * P2, P2), P2)
            kp = k_scr[j, rows, :]
            qp = q_scr[j, rows, :]
            vp = v_scr[j, rows, :]
            gc2 = bc[:, 0:W2]
            beta2 = bc[:, W2:2 * W2]
            gcf, gcb = gc2[:, :LANES], gc2[:, LANES:]
            tot2 = jnp.concatenate([jnp.where(first, gcf[CHUNK - 1:CHUNK], gcf[P2 - 1:P2]),
                                    jnp.where(first, gcb[0:1], gcb[CHUNK:CHUNK + 1])], axis=1)
            egc2 = jnp.exp(gc2)
            k2 = jnp.concatenate([kp, kp], axis=1)
            kb2 = k2 * beta2
            vb2 = jnp.concatenate([vp, vp], axis=1) * beta2
            kbe2 = kb2 * egc2
            cx.append(dict(
                m=m, j=j, rows=rows, vb2=vb2, kbe2=kbe2,
                qeb=(jnp.concatenate([qp, qp], axis=1) * egc2).astype(BF16),
                kd2=k2 * jnp.exp(tot2 - gc2),
                et=jnp.exp(tot2),
                dec=jnp.exp(jnp.where(
                    incl, squeeze(gc2) - squeeze(jnp.concatenate([gcf.T, gcb.T], axis=1)), -jnp.inf)),
                stk=jnp.concatenate([kb2[:, :LANES], kb2[:, LANES:], qp], axis=0).astype(BF16),
                kpb=kp.astype(BF16)))

        g3s = [_dot_nt(c['stk'], c['kpb']) for c in cx]
        for c, g3 in zip(cx, g3s):
            lm = jnp.where(strict, squeeze(jnp.concatenate([g3[:P2], g3[P2:2 * P2]], axis=1)) * c['dec'], 0.0)
            attn = squeeze(jnp.concatenate([g3[2 * P2:], g3[2 * P2:]], axis=1)) * c['dec']
            c['attn2'] = unsqueeze(attn).astype(BF16)
            c['lbd'] = jnp.where(bd16, lm, 0.0)
            c['loff'] = lm - c['lbd']
        nn = [-c['lbd'] for c in cx]
        pw = [mm4(c['lbd'], c['lbd']) for c in cx]
        for rnd in range(3):
            prod = [mm4(a, p) for a, p in zip(nn, pw)]
            nxt = [mm4(p, p) for p in pw] if rnd < 2 else pw
            nn = [a + p + q for a, p, q in zip(nn, pw, prod)]
            pw = nxt
        mo = [c['loff'] + x for c, x in zip(cx, [mm4(a, c['loff']) for a, c in zip(nn, cx)])]
        m2 = [mm4(x, x) for x in mo]
        mn = [mm4(x, a) for x, a in zip(mo, nn)]
        xo = [a - x - y for a, x, y in zip(nn, mo, mn)]
        mx = [mm4(a, b) for a, b in zip(m2, xo)]
        toff = [a + b + q for a, b, q in zip(xo, m2, mx)]
        z4 = jnp.zeros((CHUNK, W2), BF16)
        tws = []
        for c, tf in zip(cx, toff):
            vbb, kbb = c['vb2'].astype(BF16), c['kbe2'].astype(BF16)
            blocks = []
            for g in range(4):
                rr = slice((g % 2) * CHUNK, (g % 2 + 1) * CHUNK)
                ll = slice((g // 2) * LANES, (g // 2 + 1) * LANES)
                blocks.append(jnp.concatenate([z4] * g + [vbb[rr, ll], kbb[rr, ll]] + [z4] * (3 - g), axis=1))
            tws.append(_dot(tf.astype(BF16), jnp.concatenate(blocks, axis=0)))
        for c, tw in zip(cx, tws):
            m, j, rows = c['m'], c['j'], c['rows']
            def pair_layout(off):
                piece = lambda g: tw[:, 2 * g * LANES + off:2 * g * LANES + off + LANES]
                return jnp.concatenate([jnp.concatenate([piece(0), piece(2)], axis=1),
                                        jnp.concatenate([piece(1), piece(3)], axis=1)], axis=0)

            u2 = c['vb2'] + pair_layout(0)
            w2 = (c['kbe2'] + pair_layout(LANES)).astype(BF16)
            qeb = c['qeb']
            u_scr[j, rows, :] = u2
            wq_scr[j, pl.ds(pl.multiple_of(m * 2 * P2, 2 * P2), 2 * P2), :] = jnp.concatenate(
                [w2[:CHUNK], qeb[:CHUNK], w2[CHUNK:], qeb[CHUNK:]], axis=0)
            at_scr[j, rows, :] = c['attn2']
            kdt_scr[j, 0, :, rows] = c['kd2'][:, :LANES].T.astype(BF16)
            kdt_scr[j, 1, :, rows] = c['kd2'][:, LANES:].T.astype(BF16)
            et_scr[j, pl.ds(pl.multiple_of(m * 2 * SUBLANES, 2 * SUBLANES), 2 * SUBLANES), :] = jnp.concatenate(
                [c['et'][:SUBLANES], c['et'][CHUNK:CHUNK + SUBLANES]], axis=0)
        return carry

    lax.fori_loop(0, npair // U, prep, 0)

    zc = jnp.zeros((CHUNK, LANES), F32)
    zp = jnp.zeros((P2, LANES), F32)

    def place(v, cpos):
        return jnp.concatenate([v, zc] if cpos == 0 else [zc, v], axis=0)

    def scan(m, states):
        pf = m
        pb = npair - 1 - m
        rows_f = pl.ds(pl.multiple_of(pf * P2, P2), P2)
        rows_b = pl.ds(pl.multiple_of(pb * P2, P2), P2)
        hx = []
        for j in range(hp):
            hx.append(dict(
                u_f=u_scr[j, rows_f, :LANES], u_b=u_scr[j, rows_b, LANES:],
                at_f=at_scr[j, rows_f, :LANES], at_b=at_scr[j, rows_b, LANES:],
                kdt=jnp.concatenate([kdt_scr[j, 0, :, rows_f], kdt_scr[j, 1, :, rows_b]], axis=1),
                wq_f=wq_scr[j, pl.ds(pl.multiple_of(pf * 2 * P2, 2 * P2), 2 * P2), :LANES],
                wq_b=wq_scr[j, pl.ds(pl.multiple_of(pb * 2 * P2, 2 * P2), 2 * P2), LANES:],
                et_f=et_scr[j, pl.ds(pl.multiple_of(pf * 2 * SUBLANES, 2 * SUBLANES), 2 * SUBLANES), :LANES],
                et_b=et_scr[j, pl.ds(pl.multiple_of(pb * 2 * SUBLANES, 2 * SUBLANES), 2 * SUBLANES), LANES:]))
        sts = list(states)
        for step in range(2):
            cf, cb = step, 1 - step
            rrs = [_dot(jnp.concatenate([c['wq_f'][cf * P2:(cf + 1) * P2], c['wq_b'][cb * P2:(cb + 1) * P2]], axis=1),
                        bdiag(st.astype(BF16))) for c, st in zip(hx, sts)]
            ress = []
            for c, rr in zip(hx, rrs):
                u2 = jnp.concatenate([c['u_f'][cf * CHUNK:(cf + 1) * CHUNK],
                                      c['u_b'][cb * CHUNK:(cb + 1) * CHUNK]], axis=1)
                v_new = u2 - rr[:CHUNK]
                rhs = jnp.concatenate(
                    [jnp.concatenate([place(v_new[:, :LANES], cf), zp], axis=1),
                     jnp.concatenate([zp, place(v_new[:, LANES:], cb)], axis=1)], axis=0).astype(BF16)
                lhs = jnp.concatenate(
                    [jnp.concatenate([c['at_f'][cf * CHUNK:(cf + 1) * CHUNK],
                                      c['at_b'][cb * CHUNK:(cb + 1) * CHUNK]], axis=1),
                     c['kdt']], axis=0)
                ress.append(_dot(lhs, rhs))
            for j, (c, rr, res) in enumerate(zip(hx, rrs, ress)):
                o2 = rr[CHUNK:] + res[:CHUNK]
                et2 = jnp.concatenate([c['et_f'][cf * SUBLANES:cf * SUBLANES + 1],
                                       c['et_b'][cb * SUBLANES:cb * SUBLANES + 1]], axis=1)
                sts[j] = sts[j] * et2 + res[CHUNK:]
                of_rows = pl.ds(pl.multiple_of(pf * P2 + cf * CHUNK, CHUNK), CHUNK)
                ob_rows = pl.ds(pl.multiple_of(pb * P2 + cb * CHUNK, CHUNK), CHUNK)
                oacc_scr[j, of_rows, :] = oacc_scr[j, of_rows, :] + o2[:, :LANES]
                oacc_scr[j, ob_rows, :] = oacc_scr[j, ob_rows, :] + o2[:, LANES:]
        return tuple(sts)

    s0 = jnp.zeros((dk, W2), F32)
    lax.fori_loop(0, npair, scan, (s0,) * hp)

    for j in range(hp):
        lanes_j = slice(j * LANES, (j + 1) * LANES)
        for s0 in range(0, S, seg):
            o = oacc_scr[j, s0:s0 + seg, :]
            y = o * lax.rsqrt(jnp.mean(o * o, axis=-1, keepdims=True) + EPS) * og_ref[...]
            z = z_ref[s0:s0 + seg, lanes_j].astype(F32)
            o_ref[s0:s0 + seg, lanes_j] = (y * _silu(z)).astype(o_ref.dtype)


def _gdn(proj, cw4, pack3, onorm_g, B, S, HB, base_blk):
    T = proj.shape[0]
    hp = GDN_HP if (HB % GDN_HP == 0 and base_blk % GDN_HP == 0) else 1
    wblk = hp * LANES
    npair = S // (2 * CHUNK)

    def col(k):
        off = (base_blk + k * HB) // hp
        return lambda b, h: (b, off + h)

    return pl.pallas_call(
        functools.partial(_gdn_kernel, hb=HB, hp=hp),
        grid=(B, HB // hp),
        in_specs=[pl.BlockSpec((S, wblk), col(0)),
                  pl.BlockSpec((S, wblk), col(1)),
                  pl.BlockSpec((S, wblk), col(2)),
                  pl.BlockSpec((S, wblk), col(3)),
                  pl.BlockSpec((hp, 3, cw4.shape[2], LANES), lambda b, h: (h, 0, 0, 0)),
                  pl.BlockSpec((None, S, LANES), lambda b, h: (b, 0, 0)),
                  pl.BlockSpec((1, LANES), lambda b, h: (0, 0))],
        out_specs=pl.BlockSpec((S, wblk), lambda b, h: (b, h)),
        out_shape=jax.ShapeDtypeStruct((T, HB * LANES), BF16),
        scratch_shapes=[pltpu.VMEM((hp, S, LANES), F32)] * 3
        + [pltpu.VMEM((2, S + 2 * SUBLANES, LANES), F32),
           pltpu.VMEM((S, 2 * LANES), BF16),
           pltpu.VMEM((hp, 2 * LANES, 4 * LANES), BF16),
           pltpu.VMEM((hp, S, 2 * LANES), F32),
           pltpu.VMEM((hp, 2 * S, 2 * LANES), BF16),
           pltpu.VMEM((hp, S, 2 * LANES), BF16),
           pltpu.VMEM((hp, 2, LANES, S), BF16),
           pltpu.VMEM((hp, npair * 2 * SUBLANES, 2 * LANES), F32),
           pltpu.VMEM((hp, S, LANES), F32)],
        compiler_params=_cparams(("arbitrary", "arbitrary"), 56),
    )(proj, proj, proj, proj, cw4, pack3, onorm_g)


def _outproj_kernel(oa_ref, ob_ref, x_ref, gt_ref, w_ref, g_ref, sc_ref, sh_ref, wr_ref, br_ref,
                    x1_ref, hp_ref, lg_ref, z_ref, h_scr):
    z_ref[...] = jnp.zeros(z_ref.shape, U32)
    da = oa_ref.shape[1]
    y = _dot(oa_ref[...], w_ref[:da, :]) + _dot(ob_ref[...], w_ref[da:, :])
    x1 = x_ref[...] + gt_ref[...] * y
    x1_ref[...] = x1
    hn = x1 * lax.rsqrt(jnp.mean(x1 * x1, axis=-1, keepdims=True) + EPS) * g_ref[...]
    h = hn * (1.0 + sc_ref[...]) + sh_ref[...]
    hp_ref[...] = _pack_halves(h)
    h_scr[...] = h

    @pl.when(pl.program_id(0) < pl.num_programs(0))
    def _():
        lg_ref[...] = _dot_hilo(h_scr[...], wr_ref[...]) + br_ref[...]


def _outproj(oa, ob, x2, gt1, w_out_b, g2, sc2, sh2, wr, br, S, p_rows):
    T, D = x2.shape
    tm = ROW_TILE
    per_b = S // tm
    bmap = lambda i: (i // per_b, 0, 0)
    zrows = p_rows // (T // tm)
    assert zrows * (T // tm) == p_rows and zrows % SUBLANES == 0
    return pl.pallas_call(
        _outproj_kernel,
        grid=(T // tm,),
        in_specs=[pl.BlockSpec((tm, oa.shape[1]), lambda i: (i, 0)),
                  pl.BlockSpec((tm, ob.shape[1]), lambda i: (i, 0)),
                  pl.BlockSpec((tm, D), lambda i: (i, 0)),
                  pl.BlockSpec((None, 1, D), bmap),
                  pl.BlockSpec((D, D), lambda i: (0, 0)),
                  pl.BlockSpec((1, D), lambda i: (0, 0)),
                  pl.BlockSpec((None, 1, D), bmap),
                  pl.BlockSpec((None, 1, D), bmap),
                  pl.BlockSpec((D, 2 * LANES), lambda i: (0, 0)),
                  pl.BlockSpec((1, LANES), lambda i: (0, 0))],
        out_specs=[pl.BlockSpec((tm, D), lambda i: (i, 0)),
                   pl.BlockSpec((tm, D // 2), lambda i: (i, 0)),
                   pl.BlockSpec((tm, LANES), lambda i: (i, 0)),
                   pl.BlockSpec((zrows, D // 2), lambda i: (i, 0))],
        out_shape=[jax.ShapeDtypeStruct((T, D), F32), jax.ShapeDtypeStruct((T, D // 2), U32),
                   jax.ShapeDtypeStruct((T, LANES), F32), jax.ShapeDtypeStruct((p_rows, D // 2), U32)],
        scratch_shapes=[pltpu.VMEM((tm, D), F32)],
        compiler_params=_cparams(("arbitrary",), 48),
    )(oa, ob, x2, gt1, w_out_b, g2, sc2, sh2, wr, br)


def _route_kernel(lg_ref, o_ref, info_ref, run_scr, *, ne, ng):
    ph = pl.program_id(0)
    i = pl.program_id(1)

    @pl.when((ph == 0) & (i == 0))
    def _():
        run_scr[...] = jnp.zeros_like(run_scr)

    @pl.when((ph == 1) & (i == 0))
    def _():
        cnt = run_scr[...]
        padded = jnp.ceil(cnt * (1.0 / MOE_RB)) * MOE_RB
        k = lax.broadcasted_iota(I32, (LANES, LANES), 0)
        e = lax.broadcasted_iota(I32, (LANES, LANES), 1)
        start = _dot(padded, jnp.where(k < e, 1.0, 0.0), precision=HIGHEST)
        rowi = lax.broadcasted_iota(I32, cnt.shape, 0)
        info_ref[...] = jnp.where(rowi == 0, cnt, jnp.where(rowi == 1, padded, start))
        run_scr[...] = start

    lg = lg_ref[...]
    tm = lg.shape[0]
    epg = ne // ng
    lane_i = lax.broadcasted_iota(I32, lg.shape, 1)
    lane = lane_i.astype(F32)
    big = float(2 * LANES)
    is_g = (lane_i >= ne) & (lane_i < ne + ng)
    gl = jnp.where(is_g, lg, -jnp.inf)
    gmax = jnp.max(gl, axis=-1, keepdims=True)
    gidx = jnp.min(jnp.where(gl == gmax, lane, big), axis=-1, keepdims=True) - ne
    psel = 1.0 / jnp.sum(jnp.where(is_g, jnp.exp(gl - gmax), 0.0), axis=-1, keepdims=True)
    in_grp = (lane_i // epg).astype(F32) == gidx
    el = jnp.where(in_grp & (lane_i < ne), lg, -jnp.inf)
    m1 = jnp.max(el, axis=-1, keepdims=True)
    i1 = jnp.min(jnp.where(el == m1, lane, big), axis=-1, keepdims=True)
    el2 = jnp.where(lane == i1, -jnp.inf, el)
    m2 = jnp.max(el2, axis=-1, keepdims=True)
    i2 = jnp.min(jnp.where(el2 == m2, lane, big), axis=-1, keepdims=True)
    e21 = jnp.exp(m2 - m1)
    g1 = psel / (1.0 + e21)
    g2 = psel * e21 / (1.0 + e21)
    o1 = jnp.where(lane == i1, 1.0, 0.0)
    o2 = jnp.where(lane == i2, 1.0, 0.0)
    cnt = o1 + o2
    r = lax.broadcasted_iota(I32, (tm, tm), 0)
    c = lax.broadcasted_iota(I32, (tm, tm), 1)
    before = _dot(jnp.where(c < r, 1.0, 0.0).astype(BF16), cnt.astype(BF16)) + run_scr[0:1, :]
    d1 = jnp.sum(o1 * before, axis=-1, keepdims=True)
    d2 = jnp.sum(o2 * before, axis=-1, keepdims=True)
    run_scr[...] = run_scr[...] + jnp.sum(cnt, axis=0, keepdims=True)

    @pl.when(ph == 1)
    def _():
        out = jnp.zeros(lg.shape, F32)
        for j, val in enumerate((i1, i2, g1, g2, d1, d2)):
            out = jnp.where(lane_i == j, val, out)
        o_ref[...] = out


def _route(logits, ne, ng):
    T = logits.shape[0]
    tm = _pick_tile(T, (1024, 512, 256))
    return pl.pallas_call(
        functools.partial(_route_kernel, ne=ne, ng=ng),
        grid=(2, T // tm),
        in_specs=[pl.BlockSpec((tm, LANES), lambda p, i: (i, 0))],
        out_specs=[pl.BlockSpec((tm, LANES), lambda p, i: (i * p, 0)),
                   pl.BlockSpec((SUBLANES, LANES), lambda p, i: (0, 0))],
        out_shape=[jax.ShapeDtypeStruct((T, LANES), F32), jax.ShapeDtypeStruct((SUBLANES, LANES), F32)],
        scratch_shapes=[pltpu.VMEM((SUBLANES, LANES), F32)],
        compiler_params=_cparams(("arbitrary", "arbitrary"), 16),
    )(logits)


def _dispatch_kernel(dest_ref, h_hbm, xs_in, xs_hbm, sem):
    del xs_in
    tm = ROW_TILE
    base = pl.program_id(0) * tm

    def copy(tok, a):
        return pltpu.make_async_copy(h_hbm.at[pl.ds(tok, 1)], xs_hbm.at[pl.ds(dest_ref[a], 1)], sem)

    def issue(t, carry):
        for k in range(TOP_K):
            copy(base + t, (base + t) * TOP_K + k).start()
        return carry

    lax.fori_loop(0, tm, issue, 0, unroll=DMA_UNROLL)

    for k in range(TOP_K):
        pltpu.make_async_copy(h_hbm.at[pl.ds(0, tm)], xs_hbm.at[pl.ds(0, tm)], sem).wait()


def _dispatch(dest, hpk, xs_zero):
    T, dh = hpk.shape
    p_rows = xs_zero.shape[0]
    return pl.pallas_call(
        _dispatch_kernel,
        grid_spec=pltpu.PrefetchScalarGridSpec(
            num_scalar_prefetch=1,
            grid=(T // ROW_TILE,),
            in_specs=[pl.BlockSpec(memory_space=pl.ANY), pl.BlockSpec(memory_space=pl.ANY)],
            out_specs=pl.BlockSpec(memory_space=pl.ANY),
            scratch_shapes=[pltpu.SemaphoreType.DMA(())]),
        out_shape=jax.ShapeDtypeStruct((p_rows, dh), U32),
        input_output_aliases={2: 0},
        compiler_params=_cparams(("arbitrary",), 16),
    )(dest, hpk, xs_zero)


def _moe_kernel(we_ref, ws_ref, wn_ref, wt_ref, xs_hbm, w1_hbm, w3_hbm, w2_hbm, y_hbm,
                xbuf, xlo, xhi, yacc, ypk, w1buf, w3buf, w2buf, sem_in, sem_out, sem_w):
    w = pl.program_id(0)
    c = pl.program_id(1)
    nw = pl.num_programs(0)
    nc = pl.num_programs(1)
    nrows = wn_ref[w]
    start = ws_ref[w]
    nblk = nrows // MOE_RB
    dh = xbuf.shape[1]
    cw = w1buf.shape[-1]

    g = w * nc + c

    def w_copies(step, do):
        item = step // nc
        chunk = step % nc

        @pl.when((item < nw) & (wn_ref[jnp.minimum(item, nw - 1)] > 0))
        def _():
            e = we_ref[item]
            cols = pl.ds(pl.multiple_of(chunk * cw, cw), cw)
            slot = step % MOE_WSLOTS
            do(pltpu.make_async_copy(w1_hbm.at[e, :, cols], w1buf.at[slot], sem_w.at[slot]))
            do(pltpu.make_async_copy(w3_hbm.at[e, :, cols], w3buf.at[slot], sem_w.at[slot]))
            do(pltpu.make_async_copy(w2_hbm.at[e, cols, :], w2buf.at[slot], sem_w.at[slot]))

    @pl.when(g == 0)
    def _():
        for ahead in range(MOE_WSLOTS - 1):
            w_copies(ahead, lambda cp: cp.start())

    w_copies(g + MOE_WSLOTS - 1, lambda cp: cp.start())
    w_copies(g, lambda cp: cp.wait())
    wslot = g % MOE_WSLOTS
    w1_ref, w3_ref, w2_ref = w1buf.at[wslot], w3buf.at[wslot], w2buf.at[wslot]

    def blk_rows(rb):
        return pl.ds(pl.multiple_of(rb * MOE_RB, MOE_RB), MOE_RB)

    def hbm_rows(item_start, rb):
        return pl.ds(pl.multiple_of(item_start + rb * MOE_RB, MOE_RB), MOE_RB)

    def in_copy(item_start, rb):
        return pltpu.make_async_copy(xs_hbm.at[hbm_rows(item_start, rb)], xbuf.at[blk_rows(rb)], sem_in)

    def out_copy(item_start, rb):
        return pltpu.make_async_copy(ypk.at[blk_rows(rb)], y_hbm.at[hbm_rows(item_start, rb)], sem_out)

    def each_block(n, fn):
        def body(rb, carry):
            fn(rb)
            return carry
        lax.fori_loop(0, n, body, 0)

    @pl.when(c == 0)
    def _load():
        @pl.when(w == 0)
        def _():
            each_block(nblk, lambda rb: in_copy(start, rb).start())

        each_block(nblk, lambda rb: in_copy(start, rb).wait())

        def unpack(rb):
            lo, hi = _unpack_halves(xbuf[blk_rows(rb), :])
            xlo[blk_rows(rb), :] = lo.astype(BF16)
            xhi[blk_rows(rb), :] = hi.astype(BF16)
            yacc[blk_rows(rb), :] = jnp.zeros((MOE_RB, 2 * dh), F32)

        each_block(nblk, unpack)

        @pl.when(w + 1 < nw)
        def _():
            nxt = ws_ref[w + 1]
            each_block(wn_ref[w + 1] // MOE_RB, lambda rb: in_copy(nxt, rb).start())

    @pl.when(nrows > 0)
    def _compute():
        def rows_block(row0, nr):
            rows = pl.ds(row0, nr)
            xl = xlo[rows, :]
            xh = xhi[rows, :]
            h1 = _dot(xl, w1_ref[:dh, :].astype(BF16)) + _dot(xh, w1_ref[dh:, :].astype(BF16))
            h3 = _dot(xl, w3_ref[:dh, :].astype(BF16)) + _dot(xh, w3_ref[dh:, :].astype(BF16))
            hid = (_silu(h1) * h3).astype(BF16)
            yacc[rows, :] = yacc[rows, :] + _dot(hid, w2_ref[...].astype(BF16))

        ntall = nrows // MOE_TALL
        each_block(ntall, lambda i: rows_block(pl.multiple_of(i * MOE_TALL, MOE_TALL), MOE_TALL))

        rem = nrows - ntall * MOE_TALL
        for nr in range(MOE_RB, MOE_TALL, MOE_RB):
            @pl.when(rem == nr)
            def _(nr=nr):
                rows_block(pl.multiple_of(ntall * MOE_TALL, MOE_TALL), nr)

    @pl.when(c == nc - 1)
    def _store():
        @pl.when(w > 0)
        def _():
            prev = ws_ref[w - 1]
            each_block(wn_ref[w - 1] // MOE_RB, lambda rb: out_copy(prev, rb).wait())

        def pack(rb):
            ypk[blk_rows(rb), :] = _pack_halves(yacc[blk_rows(rb), :])

        each_block(nblk, pack)
        each_block(nblk, lambda rb: out_copy(start, rb).start())

        @pl.when(w == nw - 1)
        def _():
            each_block(nblk, lambda rb: out_copy(start, rb).wait())

    @pl.when((w == nw - 1) & (c == nc - 1))
    def _zero_tail():
        ypk[blk_rows(0), :] = jnp.zeros((MOE_RB, dh), U32)
        first = wt_ref[0] // MOE_RB

        def tail_copy(b):
            return pltpu.make_async_copy(ypk.at[blk_rows(0)],
                                         y_hbm.at[pl.ds(pl.multiple_of(b * MOE_RB, MOE_RB), MOE_RB)], sem_out)

        def start(b, carry):
            tail_copy(b).start()
            return carry

        def wait(b, carry):
            tail_copy(b).wait()
            return carry

        lax.fori_loop(first, y_hbm.shape[0] // MOE_RB, start, 0)
        lax.fori_loop(first, y_hbm.shape[0] // MOE_RB, wait, 0)


def _moe(we, ws, wn, wt, xs, w1, w3, w2):
    P, dh = xs.shape
    D = 2 * dh
    NE, _, DE = w1.shape
    cw = min(MOE_CW, DE)
    nc = DE // cw
    nw = we.shape[0]
    return pl.pallas_call(
        _moe_kernel,
        grid_spec=pltpu.PrefetchScalarGridSpec(
            num_scalar_prefetch=4,
            grid=(nw, nc),
            in_specs=[pl.BlockSpec(memory_space=pl.ANY)] * 4,
            out_specs=pl.BlockSpec(memory_space=pl.ANY),
            scratch_shapes=[pltpu.VMEM((MOE_RMAX, dh), U32),
                            pltpu.VMEM((MOE_RMAX, dh), BF16),
                            pltpu.VMEM((MOE_RMAX, dh), BF16),
                            pltpu.VMEM((MOE_RMAX, D), F32),
                            pltpu.VMEM((MOE_RMAX, dh), U32),
                            pltpu.VMEM((MOE_WSLOTS, D, cw), F32),
                            pltpu.VMEM((MOE_WSLOTS, D, cw), F32),
                            pltpu.VMEM((MOE_WSLOTS, cw, D), F32),
                            pltpu.SemaphoreType.DMA(()),
                            pltpu.SemaphoreType.DMA(()),
                            pltpu.SemaphoreType.DMA((MOE_WSLOTS,))]),
        out_shape=jax.ShapeDtypeStruct((P, dh), U32),
        compiler_params=_cparams(("arbitrary", "arbitrary"), 56),
    )(we, ws, wn, wt, xs, w1, w3, w2)


def _moe_schedule(info, NE, p_rows):
    padded = info[1, :NE].astype(I32)
    start_pad = info[2, :NE].astype(I32)
    items = (padded + MOE_RMAX - 1) // MOE_RMAX
    cum_items = jnp.cumsum(items)
    n_items = cum_items[-1]
    nw = (p_rows + NE * (MOE_RMAX - MOE_RB)) // MOE_RMAX
    wi = jnp.arange(nw, dtype=I32)
    valid = wi < n_items
    wi_c = jnp.minimum(wi, jnp.maximum(n_items - 1, 0))
    we = jnp.minimum(jnp.sum(cum_items[None, :] <= wi_c[:, None], axis=1), NE - 1).astype(I32)
    onehot = we[:, None] == jnp.arange(NE, dtype=I32)[None, :]
    pick = lambda v: jnp.sum(jnp.where(onehot, v[None, :], 0), axis=1)
    local = wi_c - (pick(cum_items) - pick(items))
    ws = (pick(start_pad) + local * MOE_RMAX).astype(I32)
    wn = jnp.where(valid, jnp.clip(pick(padded) - local * MOE_RMAX, 0, MOE_RMAX), 0).astype(I32)
    wt = (start_pad[NE - 1] + padded[NE - 1]).reshape(1)
    return we, ws, wn, wt


def _combine_kernel(dest_ref, x1_ref, rt_ref, gt_ref, y_hbm, o_ref, ybuf, sems):
    tm = ROW_TILE
    i = pl.program_id(0)
    n = pl.num_programs(0)
    dh = ybuf.shape[-1]

    def copy(tile, slot, t, k):
        a = (tile * tm + t) * TOP_K + k
        return pltpu.make_async_copy(y_hbm.at[pl.ds(dest_ref[a], 1)], ybuf.at[slot, k, pl.ds(t, 1)], sems.at[slot])

    def issue_tile(tile, slot):
        def body(t, carry):
            for k in range(TOP_K):
                copy(tile, slot, t, k).start()
            return carry
        lax.fori_loop(0, tm, body, 0, unroll=DMA_UNROLL)

    @pl.when(i == 0)
    def _():
        issue_tile(0, 0)

    @pl.when(i + 1 < n)
    def _():
        issue_tile(i + 1, (i + 1) % 2)

    slot = i % 2

    for k in range(TOP_K):
        pltpu.make_async_copy(y_hbm.at[pl.ds(0, tm)], ybuf.at[slot, k], sems.at[slot]).wait()

    rt = rt_ref[...]
    g1 = rt[:, 2:3]
    g2 = rt[:, 3:4]
    lo1, hi1 = _unpack_halves(ybuf[slot, 0])
    lo2, hi2 = _unpack_halves(ybuf[slot, 1])
    o_ref[:, :dh] = x1_ref[:, :dh] + gt_ref[:, :dh] * (g1 * lo1 + g2 * lo2)
    o_ref[:, dh:] = x1_ref[:, dh:] + gt_ref[:, dh:] * (g1 * hi1 + g2 * hi2)


def _combine(dest, x1, route, gt2, ypk, S):
    T, D = x1.shape
    tm = ROW_TILE
    per_b = S // tm
    return pl.pallas_call(
        _combine_kernel,
        grid_spec=pltpu.PrefetchScalarGridSpec(
            num_scalar_prefetch=1,
            grid=(T // tm,),
            in_specs=[pl.BlockSpec((tm, D), lambda i, d: (i, 0)),
                      pl.BlockSpec((tm, LANES), lambda i, d: (i, 0)),
                      pl.BlockSpec((None, 1, D), lambda i, d: (i // per_b, 0, 0)),
                      pl.BlockSpec(memory_space=pl.ANY)],
            out_specs=pl.BlockSpec((tm, D), lambda i, d: (i, 0)),
            scratch_shapes=[pltpu.VMEM((2, TOP_K, tm, D // 2), U32),
                            pltpu.SemaphoreType.DMA((2,))]),
        out_shape=jax.ShapeDtypeStruct((T, D), F32),
        compiler_params=_cparams(("arbitrary",), 32),
    )(dest, x1, route, gt2, ypk)


def kernel(x, c, w_ada, b_ada, norm1_g, norm2_g, w_in, qn_g, kn_g, rel_bias, conv_w, A_log, dt_bias,
           onorm_g, w_out, w_rg, b_rg, w_re, b_re, w1, w3, w2):
    B, S, D = x.shape
    depth = w_ada.shape[0]
    HA, hda = rel_bias.shape[1], qn_g.shape[-1]
    HB, hdb = A_log.shape[-1], onorm_g.shape[-1]
    DA, DB = HA * hda, HB * hdb
    NG, NE = w_rg.shape[-1], w_re.shape[-1]
    T = B * S
    assert hdb == LANES and 2 * hda == LANES and DA + DB == D and 16 * HB <= LANES
    assert NE + NG <= LANES and conv_w.shape[1] == 5 and S % ROW_TILE == 0
    n_main = 3 * DA + 4 * DB
    p_rows = TOP_K * T + NE * MOE_RB

    bias_prof = _attn_bias_profiles(rel_bias, S)
    w_in_t = jnp.swapaxes(w_in, 1, 2)
    x2 = x.reshape(T, D)
    for l in range(depth):
        mod = _ada(c, w_ada[l], b_ada[l]).reshape(B, 6, 1, D)
        sh1, sc1, gt1, sh2, sc2, gt2 = (mod[:, i] for i in range(6))

        w_gate = jnp.tile(w_in_t[l][n_main:, :].T, (1, 4))
        w_gate = _hilo_weights(jnp.pad(w_gate, ((0, 0), (0, LANES - w_gate.shape[1]))))
        proj, gat = _inproj(x2, norm1_g[l].reshape(1, D), sc1, sh1, w_in_t[l], n_main, w_gate, S)

        def gate_row(p):
            grp = jnp.concatenate([p.reshape(-1), jnp.zeros((2 * HB,), F32)])
            return jnp.pad(jnp.tile(grp, 4), (0, LANES - 16 * HB)).reshape(1, LANES)

        pack = _gating(gat, gate_row(A_log[l]), gate_row(dt_bias[l]), HB)

        oa = _attention(proj, jnp.tile(qn_g[l], 2).reshape(1, LANES), jnp.tile(kn_g[l], 2).reshape(1, LANES),
                        bias_prof, B, S, HA, hda)
        cw4 = jnp.transpose(conv_w[l].reshape(conv_w.shape[1], 3, HB, hdb), (2, 1, 0, 3))
        ob = _gdn(proj, cw4, pack.reshape(B, S, LANES), onorm_g[l].reshape(1, LANES), B, S, HB, 3 * DA // LANES)

        wr = _hilo_weights(jnp.pad(jnp.concatenate([w_re[l], w_rg[l]], axis=1), ((0, 0), (0, LANES - NE - NG))))
        br = jnp.pad(jnp.concatenate([b_re[l], b_rg[l]]), (0, LANES - NE - NG)).reshape(1, LANES)
        x1, hpk, logits, xs_zero = _outproj(oa, ob, x2, gt1, w_out[l].astype(BF16), norm2_g[l].reshape(1, D),
                                            sc2, sh2, wr, br, S, p_rows)
        route, info = _route(logits, NE, NG)
        dest = route[:, 4:4 + TOP_K].astype(I32).reshape(TOP_K * T)
        we, ws, wn, wt = _moe_schedule(info, NE, p_rows)
        xs = _dispatch(dest, hpk, xs_zero)
        ypk = _moe(we, ws, wn, wt, xs, w1[l], w3[l], w2[l])
        x2 = _combine(dest, x1, route, gt2, ypk, S)
    return x2.reshape(B, S, D)
```

```python
import functools

import numpy as np
import jax
import jax.numpy as jnp
from jax import lax
from jax.experimental import pallas as pl
from jax.experimental.pallas import tpu as pltpu

F32 = jnp.float32
BF16 = jnp.bfloat16
I32 = jnp.int32
U32 = jnp.uint32
HIGHEST = lax.Precision.HIGHEST

EPS = 1e-6
NEG = -1e30
DILATED_BRANCHES = ((128, 1), (512, 4), (2048, 16))
REL_MAX_DIST = 1024
CHUNK = 64
TOP_K = 2

LANES = 128
SUBLANES = 8
MIB = 1 << 20

ATT_QB = 128
ATT_KW = 256
ATT_UNROLL = 8
ATT_PROF_W = 512
GDN_HP = 2
GDN_PREP_UNROLL = 4
GDN_SEG = 512
MOE_RB = 128
MOE_TALL = 512
MOE_RMAX = 512
MOE_WSLOTS = 3
MOE_CW = 512
ROW_TILE = 256
DMA_UNROLL = 8


def _cparams(sem, vmem_mib):
    return pltpu.CompilerParams(dimension_semantics=sem, vmem_limit_bytes=vmem_mib * MIB)


def _dot(a, b, **kw):
    return jnp.dot(a, b, preferred_element_type=F32, **kw)


def _dot_nt(a, b):
    return lax.dot_general(a, b, (((1,), (1,)), ((), ())), preferred_element_type=F32)


def _pick_tile(n, prefs):
    for t in prefs:
        if n % t == 0:
            return t
    return n


def _pack_halves(x):
    half = x.shape[1] // 2
    lo = lax.bitcast_convert_type(x[:, :half].astype(BF16).astype(F32), U32)
    hi = lax.bitcast_convert_type(x[:, half:].astype(BF16).astype(F32), U32)
    return lax.shift_right_logical(lo, jnp.uint32(16)) | (hi & jnp.uint32(0xFFFF0000))


def _silu(x):
    h = 0.5 * x
    return h + h * jnp.tanh(h)


def _hilo_weights(w):
    hi = w.astype(BF16)
    lo = (w - hi.astype(F32)).astype(BF16)
    return jnp.concatenate([hi, lo], axis=1)


def _dot_hilo(x, w2):
    m, n = x.shape[0], w2.shape[1] // 2
    hi = x.astype(BF16)
    lo = (x - hi.astype(F32)).astype(BF16)
    r = _dot(jnp.concatenate([hi, lo], axis=0), w2)
    return r[:m, :n] + (r[:m, n:] + r[m:, :n])


def _unpack_halves(p):
    lo = lax.bitcast_convert_type(lax.shift_left(p, jnp.uint32(16)), F32)
    hi = lax.bitcast_convert_type(p & jnp.uint32(0xFFFF0000), F32)
    return lo, hi


def _ada_kernel(c_ref, w_ref, b_ref, o_ref):
    c = c_ref[...]
    s = _silu(c).astype(BF16)
    o_ref[...] = _dot(s, w_ref[...].astype(BF16)) + b_ref[...]


def _ada(c, w_ada, b_ada):
    B, D = c.shape
    N = w_ada.shape[1]
    tn = _pick_tile(N, (1024, 512, 256, 128))
    return pl.pallas_call(
        _ada_kernel,
        grid=(N // tn,),
        in_specs=[pl.BlockSpec((B, D), lambda j: (0, 0)),
                  pl.BlockSpec((D, tn), lambda j: (0, j)),
                  pl.BlockSpec((1, tn), lambda j: (0, j))],
        out_specs=pl.BlockSpec((B, tn), lambda j: (0, j)),
        out_shape=jax.ShapeDtypeStruct((B, N), F32),
        compiler_params=_cparams(("arbitrary",), 40),
    )(c, w_ada, b_ada.reshape(1, N))


def _inproj_kernel(x_ref, g_ref, sc_ref, sh_ref, w_ref, wg_ref, o_ref, og_ref, h_scr):
    @pl.when(pl.program_id(1) == 0)
    def _():
        x = x_ref[...]
        y = x * lax.rsqrt(jnp.mean(x * x, axis=-1, keepdims=True) + EPS) * g_ref[...]
        h = y * (1.0 + sc_ref[...]) + sh_ref[...]
        h_scr[...] = h.astype(BF16)
        og_ref[...] = _dot_hilo(h, wg_ref[...])

    o_ref[...] = _dot_nt(h_scr[...], w_ref[...].astype(BF16)).astype(o_ref.dtype)


def _inproj(x2, g, sc, sh, w_all, n_main, w_gate, S):
    T, D = x2.shape
    NM = n_main
    tm = _pick_tile(S, (1024, 512, 256, 128))
    tn = _pick_tile(NM, (1024, 512, 256, 128))
    per_b = S // tm
    return pl.pallas_call(
        _inproj_kernel,
        grid=(T // tm, NM // tn),
        in_specs=[pl.BlockSpec((tm, D), lambda i, j: (i, 0)),
                  pl.BlockSpec((1, D), lambda i, j: (0, 0)),
                  pl.BlockSpec((None, 1, D), lambda i, j: (i // per_b, 0, 0)),
                  pl.BlockSpec((None, 1, D), lambda i, j: (i // per_b, 0, 0)),
                  pl.BlockSpec((tn, D), lambda i, j: (j, 0)),
                  pl.BlockSpec((D, 2 * LANES), lambda i, j: (0, 0))],
        out_specs=[pl.BlockSpec((tm, tn), lambda i, j: (i, j)),
                   pl.BlockSpec((tm, LANES), lambda i, j: (i, 0))],
        out_shape=[jax.ShapeDtypeStruct((T, NM), BF16), jax.ShapeDtypeStruct((T, LANES), F32)],
        scratch_shapes=[pltpu.VMEM((tm, D), BF16)],
        compiler_params=_cparams(("arbitrary", "arbitrary"), 56),
    )(x2, g, sc, sh, w_all, w_gate)


def _gating_kernel(gat_ref, a_ref, dt_ref, o_ref, *, hb):
    gat = gat_ref[...]
    tm = gat.shape[0]
    gw = 4 * hb
    g = -jnp.exp(a_ref[...]) * jax.nn.softplus(gat + dt_ref[...])
    beta = jax.nn.sigmoid(gat)
    r = lax.broadcasted_iota(I32, (tm, tm), 0)
    c = lax.broadcasted_iota(I32, (tm, tm), 1)
    same = (r // CHUNK) == (c // CHUNK)
    pre = _dot(jnp.where(same & (c <= r), 1.0, 0.0), g, precision=HIGHEST)
    suf = _dot(jnp.where(same & (c >= r), 1.0, 0.0), g, precision=HIGHEST)
    lane = lax.broadcasted_iota(I32, gat.shape, 1)
    o_ref[...] = jnp.where(lane // gw == 0, jnp.where(lane % gw < hb, pre, suf), beta)


def _gating(gat, a_row, dt_row, hb):
    T = gat.shape[0]
    tm = ROW_TILE
    return pl.pallas_call(
        functools.partial(_gating_kernel, hb=hb),
        grid=(T // tm,),
        in_specs=[pl.BlockSpec((tm, LANES), lambda i: (i, 0)),
                  pl.BlockSpec((1, LANES), lambda i: (0, 0)),
                  pl.BlockSpec((1, LANES), lambda i: (0, 0))],
        out_specs=pl.BlockSpec((tm, LANES), lambda i: (i, 0)),
        out_shape=jax.ShapeDtypeStruct((T, LANES), F32),
        compiler_params=_cparams(("arbitrary",), 16),
    )(gat, a_row, dt_row)


def _t5_bucket(rel, n_buckets):
    half = n_buckets // 2
    max_exact = half // 2
    n = np.abs(rel)
    large = max_exact + (np.log(np.maximum(n, 1) / max_exact) / np.log(REL_MAX_DIST / max_exact)
                         * (half - max_exact)).astype(np.int32)
    large = np.minimum(large, half - 1)
    return (np.where(rel > 0, half, 0) + np.where(n < max_exact, n, large)).astype(np.int32)


def _attn_plan(S):
    plan, base = [], 0
    for window, dil in DILATED_BRANCHES:
        n = window // (2 * dil)
        L = S // dil
        assert L % ATT_QB == 0 and n * 2 == ATT_QB
        kw = min(ATT_KW, L)
        nbq = L // ATT_QB
        nvar = 1 if nbq == 1 else 3
        plan.append((dil, L, nbq, kw, base, nvar, n))
        base += nvar
    return tuple(plan), base


def _attn_bias_profiles(rel_bias, S):
    plan, nvar_total = _attn_plan(S)
    nbuckets, H = rel_bias.shape
    u = np.arange(ATT_PROF_W) - ATT_QB
    onehots, bands = [], []
    for dil, L, nbq, kw, base, nvar, n in plan:
        offs = [0] if nvar == 1 else [0, -n, -(kw - ATT_QB)]
        for off in offs:
            rel = off + u
            onehots.append(np.eye(nbuckets, dtype=np.float32)[_t5_bucket(rel * dil, nbuckets)])
            bands.append(np.abs(rel) <= n)
    onehot = jnp.asarray(np.stack(onehots))
    band = jnp.asarray(np.stack(bands))
    prof = jnp.einsum('vwn,nh->hvw', onehot, rel_bias.astype(F32), precision=HIGHEST)
    prof = jnp.where(band[None], prof, NEG)
    prof = prof.reshape(H // 2, 2, nvar_total, ATT_PROF_W)
    return jnp.transpose(prof, (0, 2, 1, 3)).reshape(H // 2, 2 * nvar_total, ATT_PROF_W)


def _attn_kernel(q_ref, k_ref, v_ref, qg_ref, kg_ref, prof_ref, o_ref,
                 qn_scr, kn_scr, v_scr, ob_scr, mb_scr, db_scr, bias_ref, *, plan, hd):
    S = q_ref.shape[0]
    lane = lax.broadcasted_iota(I32, (1, LANES), 1)
    left = lane < hd

    @pl.when(pl.program_id(1) == 0)
    def _():
        for row in range(prof_ref.shape[0]):
            rep = jnp.broadcast_to(prof_ref[row:row + 1, :], (ATT_QB, ATT_PROF_W))
            skew = pltpu.roll(rep, 0, 1, stride=1, stride_axis=0)
            bias_ref[row // 2, row % 2] = skew[:, ATT_QB:ATT_QB + ATT_KW]

    same_head = (lax.broadcasted_iota(I32, (2 * LANES, LANES), 0) % LANES) // hd == \
        lax.broadcasted_iota(I32, (2 * LANES, LANES), 1) // hd
    avg = jnp.where(same_head, 1.0 / hd, 0.0).astype(BF16)

    def headnorm(x, g):
        x2 = x * x
        hi = x2.astype(BF16)
        lo = (x2 - hi.astype(F32)).astype(BF16)
        ms = _dot(jnp.concatenate([hi, lo], axis=1), avg)
        return x * lax.rsqrt(ms + EPS) * g

    qn_scr[...] = headnorm(q_ref[...].astype(F32), qg_ref[...]) * (hd ** -0.5)
    kn_scr[...] = headnorm(k_ref[...].astype(F32), kg_ref[...])
    v_scr[...] = v_ref[...].astype(F32)

    for bi, (dil, L, nbq, kw, base, nvar, n) in enumerate(plan):
        ones = jnp.ones((kw, LANES), BF16)

        def body(t, carry, dil=dil, L=L, nbq=nbq, kw=kw, base=base, nvar=nvar, n=n, bi=bi, ones=ones):
            blocks = []
            for uu in range(ATT_UNROLL):
                idx = t * ATT_UNROLL + uu
                r = idx // nbq
                i = idx % nbq
                q0 = i * ATT_QB
                k0 = jnp.clip(q0 - n, 0, L - kw)
                var = base if nvar == 1 else base + jnp.where(i > 0, 1, 0) + jnp.where(i == nbq - 1, 1, 0)
                if dil == 1:
                    qrows = pl.ds(pl.multiple_of(q0, ATT_QB), ATT_QB)
                    krows = pl.ds(pl.multiple_of(k0, CHUNK), kw)
                else:
                    qrows = pl.ds(r + q0 * dil, ATT_QB, stride=dil)
                    krows = pl.ds(r + k0 * dil, kw, stride=dil)
                qb = qn_scr[qrows, :]
                q2 = jnp.concatenate([jnp.where(left, qb, 0.0), jnp.where(left, 0.0, qb)], axis=0).astype(BF16)
                blocks.append((qrows, krows, var, q2))
            def score(group):
                return [_dot_nt(q2, kn_scr[krows, :].astype(BF16)) for qrows, krows, var, q2 in group]

            def softmax(group, scores):
                probs, maxes = [], []
                for (qrows, krows, var, q2), s in zip(group, scores):
                    s = s + jnp.concatenate([bias_ref[var, 0][:, :kw], bias_ref[var, 1][:, :kw]], axis=0)
                    m = jnp.max(s, axis=-1, keepdims=True)
                    probs.append(jnp.exp(s - m).astype(BF16))
                    maxes.append(m)
                return probs, maxes

            def values(group, probs):
                return [_dot(p, jnp.concatenate([v_scr[krows, :].astype(BF16), ones], axis=1))
                        for (qrows, krows, var, q2), p in zip(group, probs)]

            def finish(group, maxes, outs):
                for (qrows, krows, var, q2), m, od in zip(group, maxes, outs):
                    mb = jnp.broadcast_to(m, (2 * ATT_QB, LANES))
                    ob_scr[bi, qrows, :] = jnp.where(left, od[:ATT_QB, :LANES], od[ATT_QB:, :LANES])
                    mb_scr[bi, qrows, :] = jnp.where(left, mb[:ATT_QB], mb[ATT_QB:])
                    db_scr[bi, qrows, :] = jnp.where(left, od[:ATT_QB, LANES:], od[ATT_QB:, LANES:])

            ga, gb = blocks[:ATT_UNROLL // 2], blocks[ATT_UNROLL // 2:]
            sa = score(ga)
            sb = score(gb)
            pa, ma = softmax(ga, sa)
            oa = values(ga, pa)
            pb, mbx = softmax(gb, sb)
            ob = values(gb, pb)
            finish(ga, ma, oa)
            finish(gb, mbx, ob)
            return carry

        assert (dil * nbq) % ATT_UNROLL == 0
        lax.fori_loop(0, dil * nbq // ATT_UNROLL, body, 0)

    nb = len(plan)
    mx = mb_scr[0]
    for bi in range(1, nb):
        mx = jnp.maximum(mx, mb_scr[bi])
    num = jnp.zeros((S, LANES), F32)
    den = jnp.zeros((S, LANES), F32)
    for bi in range(nb):
        w = jnp.exp(mb_scr[bi] - mx)
        num = num + w * ob_scr[bi]
        den = den + w * db_scr[bi]
    o_ref[...] = (num / den).astype(o_ref.dtype)


def _attention(proj, qg2, kg2, profiles, B, S, HA, hd):
    T = proj.shape[0]
    pairs = HA // 2
    da_blocks = HA * hd // LANES
    plan, nvar = _attn_plan(S)
    return pl.pallas_call(
        functools.partial(_attn_kernel, plan=plan, hd=hd),
        grid=(pairs, B),
        in_specs=[pl.BlockSpec((S, LANES), lambda p, b: (b, p)),
                  pl.BlockSpec((S, LANES), lambda p, b: (b, da_blocks + p)),
                  pl.BlockSpec((S, LANES), lambda p, b: (b, 2 * da_blocks + p)),
                  pl.BlockSpec((1, LANES), lambda p, b: (0, 0)),
                  pl.BlockSpec((1, LANES), lambda p, b: (0, 0)),
                  pl.BlockSpec((None, 2 * nvar, ATT_PROF_W), lambda p, b: (p, 0, 0))],
        out_specs=pl.BlockSpec((S, LANES), lambda p, b: (b, p)),
        out_shape=jax.ShapeDtypeStruct((T, HA * hd), BF16),
        scratch_shapes=[pltpu.VMEM((S, LANES), F32)] * 3 + [pltpu.VMEM((len(plan), S, LANES), F32)] * 3
        + [pltpu.VMEM((nvar, 2, ATT_QB, ATT_KW), F32)],
        compiler_params=_cparams(("arbitrary", "arbitrary"), 40),
    )(proj, proj, proj, qg2, kg2, profiles)


def _gdn_kernel(q_ref, k_ref, v_ref, z_ref, cw_ref, pack_ref, og_ref, o_ref,
                q_scr, k_scr, v_scr, xpad_scr, pk2_scr, sel_scr, u_scr, wq_scr, at_scr, kdt_scr, et_scr, oacc_scr,
                *, hb, hp):
    S = q_ref.shape[0]
    P2 = 2 * CHUNK
    W2 = 2 * LANES
    npair = S // P2
    hg = pl.program_id(1)
    gw = 4 * hb
    dk = LANES

    pad = SUBLANES
    for slot in range(2):
        xpad_scr[slot, 0:pad, :] = jnp.zeros((pad, LANES), F32)
        xpad_scr[slot, pad + S:, :] = jnp.zeros((pad, LANES), F32)

    seg = GDN_SEG if S % GDN_SEG == 0 else S

    def conv_silu_to(src_ref, lanes_j, j, which, dst_scr, l2norm, scale):
        xp = xpad_scr.at[(3 * j + which) % 2]
        for s0 in range(0, S, seg):
            xp[pad + s0:pad + s0 + seg, :] = src_ref[s0:s0 + seg, lanes_j].astype(F32)
        for s0 in range(0, S, seg):
            acc = xp[pad + s0:pad + s0 + seg, :] * cw_ref[j, which, 2:3, :]
            for d in (-2, -1, 1, 2):
                acc = acc + xp[pad + s0 + d:pad + s0 + d + seg, :] * cw_ref[j, which, 2 + d:3 + d, :]
            y = _silu(acc)
            if l2norm:
                y = y * (lax.rsqrt(jnp.sum(y * y, axis=-1, keepdims=True) + EPS) * scale)
            dst_scr[j, s0:s0 + seg, :] = y

    for s0 in range(0, S, seg):
        pk = pack_ref[s0:s0 + seg, :]
        p_hi = pk.astype(BF16)
        p_lo = (pk - p_hi.astype(F32)).astype(BF16)
        pk2_scr[s0:s0 + seg, :] = jnp.concatenate([p_hi, p_lo], axis=1)
    srow = lax.broadcasted_iota(I32, (W2, 4 * LANES), 0) % LANES
    scol = lax.broadcasted_iota(I32, (W2, 4 * LANES), 1) // LANES
    for j in range(hp):
        lanes_j = slice(j * LANES, (j + 1) * LANES)
        conv_silu_to(q_ref, lanes_j, j, 0, q_scr, True, dk ** -0.5)
        conv_silu_to(k_ref, lanes_j, j, 1, k_scr, True, 1.0)
        conv_silu_to(v_ref, lanes_j, j, 2, v_scr, False, 1.0)
        src = hg * hp + j + jnp.where(scol < 2, scol * hb, gw + 2 * hb + (scol - 2) * hb)
        sel_scr[j] = jnp.where(srow == src, 1.0, 0.0).astype(BF16)
        oacc_scr[j] = jnp.zeros((S, LANES), F32)

    r4 = lax.broadcasted_iota(I32, (CHUNK, W2), 0)
    l4 = lax.broadcasted_iota(I32, (CHUNK, W2), 1)
    c4 = l4 % CHUNK
    blk4 = l4 // CHUNK
    lo_half = (l4 % LANES) < CHUNK
    ahead = jnp.where(l4 >= LANES, r4 - c4, c4 - r4)
    incl = ahead <= 0
    strict = ahead < 0
    bd16 = (r4 // 16) == (c4 // 16)

    def squeeze(x):
        return jnp.where(lo_half, x[:CHUNK], x[CHUNK:])

    def unsqueeze(x):
        return jnp.concatenate([jnp.where(lo_half, x, jnp.zeros_like(x)),
                                jnp.where(lo_half, jnp.zeros_like(x), x)], axis=0)

    def mm4(a, b):
        rhs = jnp.concatenate([jnp.where(blk4 == g, b, 0.0) for g in range(4)], axis=0)
        return _dot(a.astype(BF16), rhs.astype(BF16))

    first = lax.broadcasted_iota(I32, (P2, 1), 0) < CHUNK
    zpair = jnp.zeros((P2, LANES), BF16)

    def bdiag(x):
        return jnp.concatenate([jnp.concatenate([x[:, :LANES], zpair], axis=1),
                                jnp.concatenate([zpair, x[:, LANES:]], axis=1)], axis=0)

    U = GDN_PREP_UNROLL if npair % GDN_PREP_UNROLL == 0 else 1

    def prep(t, carry):
        cx = []
        chains = [(t * U + u, j) for u in range(U) for j in range(hp)]
        bcs = [_dot(pk2_scr[pl.ds(pl.multiple_of(m * P2, P2), P2), :], sel_scr[j]) for m, j in chains]
        for (m, j), bc in zip(chains, bcs):
            rows = pl.ds(pl.multiple_of(m * P2, P2), P2)
            kp = k_scr[j, rows, :]
            qp = q_scr[j, rows, :]
            vp = v_scr[j, rows, :]
            gc2 = bc[:, 0:W2]
            beta2 = bc[:, W2:2 * W2]
            gcf, gcb = gc2[:, :LANES], gc2[:, LANES:]
            tot2 = jnp.concatenate([jnp.where(first, gcf[CHUNK - 1:CHUNK], gcf[P2 - 1:P2]),
                                    jnp.where(first, gcb[0:1], gcb[CHUNK:CHUNK + 1])], axis=1)
            egc2 = jnp.exp(gc2)
            k2 = jnp.concatenate([kp, kp], axis=1)
            kb2 = k2 * beta2
            vb2 = jnp.concatenate([vp, vp], axis=1) * beta2
            kbe2 = kb2 * egc2
            cx.append(dict(
                m=m, j=j, rows=rows, vb2=vb2, kbe2=kbe2,
                qeb=(jnp.concatenate([qp, qp], axis=1) * egc2).astype(BF16),
                kd2=k2 * jnp.exp(tot2 - gc2),
                et=jnp.exp(tot2),
                dec=jnp.exp(jnp.where(
                    incl, squeeze(gc2) - squeeze(jnp.concatenate([gcf.T, gcb.T], axis=1)), -jnp.inf)),
                stk=jnp.concatenate([kb2[:, :LANES], kb2[:, LANES:], qp], axis=0).astype(BF16),
                kpb=kp.astype(BF16)))

        g3s = [_dot_nt(c['stk'], c['kpb']) for c in cx]
        for c, g3 in zip(cx, g3s):
            lm = jnp.where(strict, squeeze(jnp.concatenate([g3[:P2], g3[P2:2 * P2]], axis=1)) * c['dec'], 0.0)
            attn = squeeze(jnp.concatenate([g3[2 * P2:], g3[2 * P2:]], axis=1)) * c['dec']
            c['attn2'] = unsqueeze(attn).astype(BF16)
            c['lbd'] = jnp.where(bd16, lm, 0.0)
            c['loff'] = lm - c['lbd']
        nn = [-c['lbd'] for c in cx]
        pw = [mm4(c['lbd'], c['lbd']) for c in cx]
        for rnd in range(3):
            prod = [mm4(a, p) for a, p in zip(nn, pw)]
            nxt = [mm4(p, p) for p in pw] if rnd < 2 else pw
            nn = [a + p + q for a, p, q in zip(nn, pw, prod)]
            pw = nxt
        mo = [c['loff'] + x for c, x in zip(cx, [mm4(a, c['loff']) for a, c in zip(nn, cx)])]
        m2 = [mm4(x, x) for x in mo]
        mn = [mm4(x, a) for x, a in zip(mo, nn)]
        xo = [a - x - y for a, x, y in zip(nn, mo, mn)]
        mx = [mm4(a, b) for a, b in zip(m2, xo)]
        toff = [a + b + q for a, b, q in zip(xo, m2, mx)]
        z4 = jnp.zeros((CHUNK, W2), BF16)
        tws = []
        for c, tf in zip(cx, toff):
            vbb, kbb = c['vb2'].astype(BF16), c['kbe2'].astype(BF16)
            blocks = []
            for g in range(4):
                rr = slice((g % 2) * CHUNK, (g % 2 + 1) * CHUNK)
                ll = slice((g // 2) * LANES, (g // 2 + 1) * LANES)
                blocks.append(jnp.concatenate([z4] * g + [vbb[rr, ll], kbb[rr, ll]] + [z4] * (3 - g), axis=1))
            tws.append(_dot(tf.astype(BF16), jnp.concatenate(blocks, axis=0)))
        for c, tw in zip(cx, tws):
            m, j, rows = c['m'], c['j'], c['rows']
            def pair_layout(off):
                piece = lambda g: tw[:, 2 * g * LANES + off:2 * g * LANES + off + LANES]
                return jnp.concatenate([jnp.concatenate([piece(0), piece(2)], axis=1),
                                        jnp.concatenate([piece(1), piece(3)], axis=1)], axis=0)

            u2 = c['vb2'] + pair_layout(0)
            w2 = (c['kbe2'] + pair_layout(LANES)).astype(BF16)
            qeb = c['qeb']
            u_scr[j, rows, :] = u2
            wq_scr[j, pl.ds(pl.multiple_of(m * 2 * P2, 2 * P2), 2 * P2), :] = jnp.concatenate(
                [w2[:CHUNK], qeb[:CHUNK], w2[CHUNK:], qeb[CHUNK:]], axis=0)
            at_scr[j, rows, :] = c['attn2']
            kdt_scr[j, 0, :, rows] = c['kd2'][:, :LANES].T.astype(BF16)
            kdt_scr[j, 1, :, rows] = c['kd2'][:, LANES:].T.astype(BF16)
            et_scr[j, pl.ds(pl.multiple_of(m * 2 * SUBLANES, 2 * SUBLANES), 2 * SUBLANES), :] = jnp.concatenate(
                [c['et'][:SUBLANES], c['et'][CHUNK:CHUNK + SUBLANES]], axis=0)
        return carry

    lax.fori_loop(0, npair // U, prep, 0)

    zc = jnp.zeros((CHUNK, LANES), F32)
    zp = jnp.zeros((P2, LANES), F32)

    def place(v, cpos):
        return jnp.concatenate([v, zc] if cpos == 0 else [zc, v], axis=0)

    def scan(m, states):
        pf = m
        pb = npair - 1 - m
        rows_f = pl.ds(pl.multiple_of(pf * P2, P2), P2)
        rows_b = pl.ds(pl.multiple_of(pb * P2, P2), P2)
        hx = []
        for j in range(hp):
            hx.append(dict(
                u_f=u_scr[j, rows_f, :LANES], u_b=u_scr[j, rows_b, LANES:],
                at_f=at_scr[j, rows_f, :LANES], at_b=at_scr[j, rows_b, LANES:],
                kdt=jnp.concatenate([kdt_scr[j, 0, :, rows_f], kdt_scr[j, 1, :, rows_b]], axis=1),
                wq_f=wq_scr[j, pl.ds(pl.multiple_of(pf * 2 * P2, 2 * P2), 2 * P2), :LANES],
                wq_b=wq_scr[j, pl.ds(pl.multiple_of(pb * 2 * P2, 2 * P2), 2 * P2), LANES:],
                et_f=et_scr[j, pl.ds(pl.multiple_of(pf * 2 * SUBLANES, 2 * SUBLANES), 2 * SUBLANES), :LANES],
                et_b=et_scr[j, pl.ds(pl.multiple_of(pb * 2 * SUBLANES, 2 * SUBLANES), 2 * SUBLANES), LANES:]))
        sts = list(states)
        for step in range(2):
            cf, cb = step, 1 - step
            rrs = [_dot(jnp.concatenate([c['wq_f'][cf * P2:(cf + 1) * P2], c['wq_b'][cb * P2:(cb + 1) * P2]], axis=1),
                        bdiag(st.astype(BF16))) for c, st in zip(hx, sts)]
            ress = []
            for c, rr in zip(hx, rrs):
                u2 = jnp.concatenate([c['u_f'][cf * CHUNK:(cf + 1) * CHUNK],
                                      c['u_b'][cb * CHUNK:(cb + 1) * CHUNK]], axis=1)
                v_new = u2 - rr[:CHUNK]
                rhs = jnp.concatenate(
                    [jnp.concatenate([place(v_new[:, :LANES], cf), zp], axis=1),
                     jnp.concatenate([zp, place(v_new[:, LANES:], cb)], axis=1)], axis=0).astype(BF16)
                lhs = jnp.concatenate(
                    [jnp.concatenate([c['at_f'][cf * CHUNK:(cf + 1) * CHUNK],
                                      c['at_b'][cb * CHUNK:(cb + 1) * CHUNK]], axis=1),
                     c['kdt']], axis=0)
                ress.append(_dot(lhs, rhs))
            for j, (c, rr, res) in enumerate(zip(hx, rrs, ress)):
                o2 = rr[CHUNK:] + res[:CHUNK]
                et2 = jnp.concatenate([c['et_f'][cf * SUBLANES:cf * SUBLANES + 1],
                                       c['et_b'][cb * SUBLANES:cb * SUBLANES + 1]], axis=1)
                sts[j] = sts[j] * et2 + res[CHUNK:]
                of_rows = pl.ds(pl.multiple_of(pf * P2 + cf * CHUNK, CHUNK), CHUNK)
                ob_rows = pl.ds(pl.multiple_of(pb * P2 + cb * CHUNK, CHUNK), CHUNK)
                oacc_scr[j, of_rows, :] = oacc_scr[j, of_rows, :] + o2[:, :LANES]
                oacc_scr[j, ob_rows, :] = oacc_scr[j, ob_rows, :] + o2[:, LANES:]
        return tuple(sts)

    s0 = jnp.zeros((dk, W2), F32)
    lax.fori_loop(0, npair, scan, (s0,) * hp)

    for j in range(hp):
        lanes_j = slice(j * LANES, (j + 1) * LANES)
        for s0 in range(0, S, seg):
            o = oacc_scr[j, s0:s0 + seg, :]
            y = o * lax.rsqrt(jnp.mean(o * o, axis=-1, keepdims=True) + EPS) * og_ref[...]
            z = z_ref[s0:s0 + seg, lanes_j].astype(F32)
            o_ref[s0:s0 + seg, lanes_j] = (y * _silu(z)).astype(o_ref.dtype)


def _gdn(proj, cw4, pack3, onorm_g, B, S, HB, base_blk):
    T = proj.shape[0]
    hp = GDN_HP if (HB % GDN_HP == 0 and base_blk % GDN_HP == 0) else 1
    wblk = hp * LANES
    npair = S // (2 * CHUNK)

    def col(k):
        off = (base_blk + k * HB) // hp
        return lambda b, h: (b, off + h)

    return pl.pallas_call(
        functools.partial(_gdn_kernel, hb=HB, hp=hp),
        grid=(B, HB // hp),
        in_specs=[pl.BlockSpec((S, wblk), col(0)),
                  pl.BlockSpec((S, wblk), col(1)),
                  pl.BlockSpec((S, wblk), col(2)),
                  pl.BlockSpec((S, wblk), col(3)),
                  pl.BlockSpec((hp, 3, cw4.shape[2], LANES), lambda b, h: (h, 0, 0, 0)),
                  pl.BlockSpec((None, S, LANES), lambda b, h: (b, 0, 0)),
                  pl.BlockSpec((1, LANES), lambda b, h: (0, 0))],
        out_specs=pl.BlockSpec((S, wblk), lambda b, h: (b, h)),
        out_shape=jax.ShapeDtypeStruct((T, HB * LANES), BF16),
        scratch_shapes=[pltpu.VMEM((hp, S, LANES), F32)] * 3
        + [pltpu.VMEM((2, S + 2 * SUBLANES, LANES), F32),
           pltpu.VMEM((S, 2 * LANES), BF16),
           pltpu.VMEM((hp, 2 * LANES, 4 * LANES), BF16),
           pltpu.VMEM((hp, S, 2 * LANES), F32),
           pltpu.VMEM((hp, 2 * S, 2 * LANES), BF16),
           pltpu.VMEM((hp, S, 2 * LANES), BF16),
           pltpu.VMEM((hp, 2, LANES, S), BF16),
           pltpu.VMEM((hp, npair * 2 * SUBLANES, 2 * LANES), F32),
           pltpu.VMEM((hp, S, LANES), F32)],
        compiler_params=_cparams(("arbitrary", "arbitrary"), 56),
    )(proj, proj, proj, proj, cw4, pack3, onorm_g)


def _outproj_kernel(oa_ref, ob_ref, x_ref, gt_ref, w_ref, g_ref, sc_ref, sh_ref, wr_ref, br_ref,
                    x1_ref, hp_ref, lg_ref, z_ref, h_scr):
    z_ref[...] = jnp.zeros(z_ref.shape, U32)
    da = oa_ref.shape[1]
    y = _dot(oa_ref[...], w_ref[:da, :]) + _dot(ob_ref[...], w_ref[da:, :])
    x1 = x_ref[...] + gt_ref[...] * y
    x1_ref[...] = x1
    hn = x1 * lax.rsqrt(jnp.mean(x1 * x1, axis=-1, keepdims=True) + EPS) * g_ref[...]
    h = hn * (1.0 + sc_ref[...]) + sh_ref[...]
    hp_ref[...] = _pack_halves(h)
    h_scr[...] = h

    @pl.when(pl.program_id(0) < pl.num_programs(0))
    def _():
        lg_ref[...] = _dot_hilo(h_scr[...], wr_ref[...]) + br_ref[...]


def _outproj(oa, ob, x2, gt1, w_out_b, g2, sc2, sh2, wr, br, S, p_rows):
    T, D = x2.shape
    tm = ROW_TILE
    per_b = S // tm
    bmap = lambda i: (i // per_b, 0, 0)
    zrows = p_rows // (T // tm)
    assert zrows * (T // tm) == p_rows and zrows % SUBLANES == 0
    return pl.pallas_call(
        _outproj_kernel,
        grid=(T // tm,),
        in_specs=[pl.BlockSpec((tm, oa.shape[1]), lambda i: (i, 0)),
                  pl.BlockSpec((tm, ob.shape[1]), lambda i: (i, 0)),
                  pl.BlockSpec((tm, D), lambda i: (i, 0)),
                  pl.BlockSpec((None, 1, D), bmap),
                  pl.BlockSpec((D, D), lambda i: (0, 0)),
                  pl.BlockSpec((1, D), lambda i: (0, 0)),
                  pl.BlockSpec((None, 1, D), bmap),
                  pl.BlockSpec((None, 1, D), bmap),
                  pl.BlockSpec((D, 2 * LANES), lambda i: (0, 0)),
                  pl.BlockSpec((1, LANES), lambda i: (0, 0))],
        out_specs=[pl.BlockSpec((tm, D), lambda i: (i, 0)),
                   pl.BlockSpec((tm, D // 2), lambda i: (i, 0)),
                   pl.BlockSpec((tm, LANES), lambda i: (i, 0)),
                   pl.BlockSpec((zrows, D // 2), lambda i: (i, 0))],
        out_shape=[jax.ShapeDtypeStruct((T, D), F32), jax.ShapeDtypeStruct((T, D // 2), U32),
                   jax.ShapeDtypeStruct((T, LANES), F32), jax.ShapeDtypeStruct((p_rows, D // 2), U32)],
        scratch_shapes=[pltpu.VMEM((tm, D), F32)],
        compiler_params=_cparams(("arbitrary",), 48),
    )(oa, ob, x2, gt1, w_out_b, g2, sc2, sh2, wr, br)


def _route_kernel(lg_ref, o_ref, info_ref, run_scr, *, ne, ng):
    ph = pl.program_id(0)
    i = pl.program_id(1)

    @pl.when((ph == 0) & (i == 0))
    def _():
        run_scr[...] = jnp.zeros_like(run_scr)

    @pl.when((ph == 1) & (i == 0))
    def _():
        cnt = run_scr[...]
        padded = jnp.ceil(cnt * (1.0 / MOE_RB)) * MOE_RB
        k = lax.broadcasted_iota(I32, (LANES, LANES), 0)
        e = lax.broadcasted_iota(I32, (LANES, LANES), 1)
        start = _dot(padded, jnp.where(k < e, 1.0, 0.0), precision=HIGHEST)
        rowi = lax.broadcasted_iota(I32, cnt.shape, 0)
        info_ref[...] = jnp.where(rowi == 0, cnt, jnp.where(rowi == 1, padded, start))
        run_scr[...] = start

    lg = lg_ref[...]
    tm = lg.shape[0]
    epg = ne // ng
    lane_i = lax.broadcasted_iota(I32, lg.shape, 1)
    lane = lane_i.astype(F32)
    big = float(2 * LANES)
    is_g = (lane_i >= ne) & (lane_i < ne + ng)
    gl = jnp.where(is_g, lg, -jnp.inf)
    gmax = jnp.max(gl, axis=-1, keepdims=True)
    gidx = jnp.min(jnp.where(gl == gmax, lane, big), axis=-1, keepdims=True) - ne
    psel = 1.0 / jnp.sum(jnp.where(is_g, jnp.exp(gl - gmax), 0.0), axis=-1, keepdims=True)
    in_grp = (lane_i // epg).astype(F32) == gidx
    el = jnp.where(in_grp & (lane_i < ne), lg, -jnp.inf)
    m1 = jnp.max(el, axis=-1, keepdims=True)
    i1 = jnp.min(jnp.where(el == m1, lane, big), axis=-1, keepdims=True)
    el2 = jnp.where(lane == i1, -jnp.inf, el)
    m2 = jnp.max(el2, axis=-1, keepdims=True)
    i2 = jnp.min(jnp.where(el2 == m2, lane, big), axis=-1, keepdims=True)
    e21 = jnp.exp(m2 - m1)
    g1 = psel / (1.0 + e21)
    g2 = psel * e21 / (1.0 + e21)
    o1 = jnp.where(lane == i1, 1.0, 0.0)
    o2 = jnp.where(lane == i2, 1.0, 0.0)
    cnt = o1 + o2
    r = lax.broadcasted_iota(I32, (tm, tm), 0)
    c = lax.broadcasted_iota(I32, (tm, tm), 1)
    before = _dot(jnp.where(c < r, 1.0, 0.0).astype(BF16), cnt.astype(BF16)) + run_scr[0:1, :]
    d1 = jnp.sum(o1 * before, axis=-1, keepdims=True)
    d2 = jnp.sum(o2 * before, axis=-1, keepdims=True)
    run_scr[...] = run_scr[...] + jnp.sum(cnt, axis=0, keepdims=True)

    @pl.when(ph == 1)
    def _():
        out = jnp.zeros(lg.shape, F32)
        for j, val in enumerate((i1, i2, g1, g2, d1, d2)):
            out = jnp.where(lane_i == j, val, out)
        o_ref[...] = out


def _route(logits, ne, ng):
    T = logits.shape[0]
    tm = _pick_tile(T, (1024, 512, 256))
    return pl.pallas_call(
        functools.partial(_route_kernel, ne=ne, ng=ng),
        grid=(2, T // tm),
        in_specs=[pl.BlockSpec((tm, LANES), lambda p, i: (i, 0))],
        out_specs=[pl.BlockSpec((tm, LANES), lambda p, i: (i * p, 0)),
                   pl.BlockSpec((SUBLANES, LANES), lambda p, i: (0, 0))],
        out_shape=[jax.ShapeDtypeStruct((T, LANES), F32), jax.ShapeDtypeStruct((SUBLANES, LANES), F32)],
        scratch_shapes=[pltpu.VMEM((SUBLANES, LANES), F32)],
        compiler_params=_cparams(("arbitrary", "arbitrary"), 16),
    )(logits)


def _dispatch_kernel(dest_ref, h_hbm, xs_in, xs_hbm, sem):
    del xs_in
    tm = ROW_TILE
    base = pl.program_id(0) * tm

    def copy(tok, a):
        return pltpu.make_async_copy(h_hbm.at[pl.ds(tok, 1)], xs_hbm.at[pl.ds(dest_ref[a], 1)], sem)

    def issue(t, carry):
        for k in range(TOP_K):
            copy(base + t, (base + t) * TOP_K + k).start()
        return carry

    lax.fori_loop(0, tm, issue, 0, unroll=DMA_UNROLL)

    for k in range(TOP_K):
        pltpu.make_async_copy(h_hbm.at[pl.ds(0, tm)], xs_hbm.at[pl.ds(0, tm)], sem).wait()


def _dispatch(dest, hpk, xs_zero):
    T, dh = hpk.shape
    p_rows = xs_zero.shape[0]
    return pl.pallas_call(
        _dispatch_kernel,
        grid_spec=pltpu.PrefetchScalarGridSpec(
            num_scalar_prefetch=1,
            grid=(T // ROW_TILE,),
            in_specs=[pl.BlockSpec(memory_space=pl.ANY), pl.BlockSpec(memory_space=pl.ANY)],
            out_specs=pl.BlockSpec(memory_space=pl.ANY),
            scratch_shapes=[pltpu.SemaphoreType.DMA(())]),
        out_shape=jax.ShapeDtypeStruct((p_rows, dh), U32),
        input_output_aliases={2: 0},
        compiler_params=_cparams(("arbitrary",), 16),
    )(dest, hpk, xs_zero)


def _moe_kernel(*refs):
    w1_hbm, w1buf = refs[5], refs[14]
    nc = w1_hbm.shape[2] // w1buf.shape[-1]

    def chunk(c, carry):
        _moe_chunk(c, nc, *refs)
        return carry

    lax.fori_loop(0, nc, chunk, 0)


def _moe_chunk(c, nc, we_ref, ws_ref, wn_ref, wt_ref, xs_hbm, w1_hbm, w3_hbm, w2_hbm, y_hbm,
               xbuf, xlo, xhi, yacc, ypk, w1buf, w3buf, w2buf, sem_in, sem_out, sem_w):
    w = pl.program_id(0)
    nw = pl.num_programs(0)
    nrows = wn_ref[w]
    start = ws_ref[w]
    nblk = nrows // MOE_RB
    dh = xbuf.shape[1]
    cw = w1buf.shape[-1]

    g = w * nc + c

    def w_copies(step, do):
        item = step // nc
        chunk = step % nc

        @pl.when((item < nw) & (wn_ref[jnp.minimum(item, nw - 1)] > 0))
        def _():
            e = we_ref[item]
            cols = pl.ds(pl.multiple_of(chunk * cw, cw), cw)
            slot = step % MOE_WSLOTS
            do(pltpu.make_async_copy(w1_hbm.at[e, :, cols], w1buf.at[slot], sem_w.at[slot]))
            do(pltpu.make_async_copy(w3_hbm.at[e, :, cols], w3buf.at[slot], sem_w.at[slot]))
            do(pltpu.make_async_copy(w2_hbm.at[e, cols, :], w2buf.at[slot], sem_w.at[slot]))

    @pl.when(g == 0)
    def _():
        for ahead in range(MOE_WSLOTS - 1):
            w_copies(ahead, lambda cp: cp.start())

    w_copies(g + MOE_WSLOTS - 1, lambda cp: cp.start())
    w_copies(g, lambda cp: cp.wait())
    wslot = g % MOE_WSLOTS
    w1_ref, w3_ref, w2_ref = w1buf.at[wslot], w3buf.at[wslot], w2buf.at[wslot]

    def blk_rows(rb):
        return pl.ds(pl.multiple_of(rb * MOE_RB, MOE_RB), MOE_RB)

    def hbm_rows(item_start, rb):
        return pl.ds(pl.multiple_of(item_start + rb * MOE_RB, MOE_RB), MOE_RB)

    def in_copy(item_start, rb):
        return pltpu.make_async_copy(xs_hbm.at[hbm_rows(item_start, rb)], xbuf.at[blk_rows(rb)], sem_in)

    def out_copy(item_start, rb):
        return pltpu.make_async_copy(ypk.at[blk_rows(rb)], y_hbm.at[hbm_rows(item_start, rb)], sem_out)

    def each_block(n, fn):
        def body(rb, carry):
            fn(rb)
            return carry
        lax.fori_loop(0, n, body, 0)

    @pl.when(c == 0)
    def _load():
        @pl.when(w == 0)
        def _():
            each_block(nblk, lambda rb: in_copy(start, rb).start())

        each_block(nblk, lambda rb: in_copy(start, rb).wait())

        def unpack(rb):
            lo, hi = _unpack_halves(xbuf[blk_rows(rb), :])
            xlo[blk_rows(rb), :] = lo.astype(BF16)
            xhi[blk_rows(rb), :] = hi.astype(BF16)
            yacc[blk_rows(rb), :] = jnp.zeros((MOE_RB, 2 * dh), F32)

        each_block(nblk, unpack)

        @pl.when(w + 1 < nw)
        def _():
            nxt = ws_ref[w + 1]
            each_block(wn_ref[w + 1] // MOE_RB, lambda rb: in_copy(nxt, rb).start())

    @pl.when(nrows > 0)
    def _compute():
        def rows_block(row0, nr):
            rows = pl.ds(row0, nr)
            xl = xlo[rows, :]
            xh = xhi[rows, :]
            h1 = _dot(xl, w1_ref[:dh, :].astype(BF16)) + _dot(xh, w1_ref[dh:, :].astype(BF16))
            h3 = _dot(xl, w3_ref[:dh, :].astype(BF16)) + _dot(xh, w3_ref[dh:, :].astype(BF16))
            hid = (_silu(h1) * h3).astype(BF16)
            yacc[rows, :] = yacc[rows, :] + _dot(hid, w2_ref[...].astype(BF16))

        ntall = nrows // MOE_TALL
        each_block(ntall, lambda i: rows_block(pl.multiple_of(i * MOE_TALL, MOE_TALL), MOE_TALL))

        rem = nrows - ntall * MOE_TALL
        for nr in range(MOE_RB, MOE_TALL, MOE_RB):
            @pl.when(rem == nr)
            def _(nr=nr):
                rows_block(pl.multiple_of(ntall * MOE_TALL, MOE_TALL), nr)

    @pl.when(c == nc - 1)
    def _store():
        @pl.when(w > 0)
        def _():
            prev = ws_ref[w - 1]
            each_block(wn_ref[w - 1] // MOE_RB, lambda rb: out_copy(prev, rb).wait())

        def pack(rb):
            ypk[blk_rows(rb), :] = _pack_halves(yacc[blk_rows(rb), :])

        each_block(nblk, pack)
        each_block(nblk, lambda rb: out_copy(start, rb).start())

        @pl.when(w == nw - 1)
        def _():
            each_block(nblk, lambda rb: out_copy(start, rb).wait())

    @pl.when((w == nw - 1) & (c == nc - 1))
    def _zero_tail():
        ypk[blk_rows(0), :] = jnp.zeros((MOE_RB, dh), U32)
        first = wt_ref[0] // MOE_RB

        def tail_copy(b):
            return pltpu.make_async_copy(ypk.at[blk_rows(0)],
                                         y_hbm.at[pl.ds(pl.multiple_of(b * MOE_RB, MOE_RB), MOE_RB)], sem_out)

        def start(b, carry):
            tail_copy(b).start()
            return carry

        def wait(b, carry):
            tail_copy(b).wait()
            return carry

        lax.fori_loop(first, y_hbm.shape[0] // MOE_RB, start, 0)
        lax.fori_loop(first, y_hbm.shape[0] // MOE_RB, wait, 0)


def _moe(we, ws, wn, wt, xs, w1, w3, w2):
    P, dh = xs.shape
    D = 2 * dh
    NE, _, DE = w1.shape
    cw = min(MOE_CW, DE)
    nw = we.shape[0]
    return pl.pallas_call(
        _moe_kernel,
        grid_spec=pltpu.PrefetchScalarGridSpec(
            num_scalar_prefetch=4,
            grid=(nw,),
            in_specs=[pl.BlockSpec(memory_space=pl.ANY)] * 4,
            out_specs=pl.BlockSpec(memory_space=pl.ANY),
            scratch_shapes=[pltpu.VMEM((MOE_RMAX, dh), U32),
                            pltpu.VMEM((MOE_RMAX, dh), BF16),
                            pltpu.VMEM((MOE_RMAX, dh), BF16),
                            pltpu.VMEM((MOE_RMAX, D), F32),
                            pltpu.VMEM((MOE_RMAX, dh), U32),
                            pltpu.VMEM((MOE_WSLOTS, D, cw), F32),
                            pltpu.VMEM((MOE_WSLOTS, D, cw), F32),
                            pltpu.VMEM((MOE_WSLOTS, cw, D), F32),
                            pltpu.SemaphoreType.DMA(()),
                            pltpu.SemaphoreType.DMA(()),
                            pltpu.SemaphoreType.DMA((MOE_WSLOTS,))]),
        out_shape=jax.ShapeDtypeStruct((P, dh), U32),
        compiler_params=_cparams(("arbitrary",), 56),
    )(we, ws, wn, wt, xs, w1, w3, w2)


def _moe_schedule(info, NE, p_rows):
    padded = info[1, :NE].astype(I32)
    start_pad = info[2, :NE].astype(I32)
    items = (padded + MOE_RMAX - 1) // MOE_RMAX
    cum_items = jnp.cumsum(items)
    n_items = cum_items[-1]
    nw = (p_rows + NE * (MOE_RMAX - MOE_RB)) // MOE_RMAX
    wi = jnp.arange(nw, dtype=I32)
    valid = wi < n_items
    wi_c = jnp.minimum(wi, jnp.maximum(n_items - 1, 0))
    we = jnp.minimum(jnp.sum(cum_items[None, :] <= wi_c[:, None], axis=1), NE - 1).astype(I32)
    onehot = we[:, None] == jnp.arange(NE, dtype=I32)[None, :]
    pick = lambda v: jnp.sum(jnp.where(onehot, v[None, :], 0), axis=1)
    local = wi_c - (pick(cum_items) - pick(items))
    ws = (pick(start_pad) + local * MOE_RMAX).astype(I32)
    wn = jnp.where(valid, jnp.clip(pick(padded) - local * MOE_RMAX, 0, MOE_RMAX), 0).astype(I32)
    wt = (start_pad[NE - 1] + padded[NE - 1]).reshape(1)
    return we, ws, wn, wt


def _combine_kernel(dest_ref, x1_ref, rt_ref, gt_ref, y_hbm, o_ref, ybuf, sems):
    tm = ROW_TILE
    i = pl.program_id(0)
    n = pl.num_programs(0)
    dh = ybuf.shape[-1]

    def copy(tile, slot, t, k):
        a = (tile * tm + t) * TOP_K + k
        return pltpu.make_async_copy(y_hbm.at[pl.ds(dest_ref[a], 1)], ybuf.at[slot, k, pl.ds(t, 1)], sems.at[slot])

    def issue_tile(tile, slot):
        def body(t, carry):
            for k in range(TOP_K):
                copy(tile, slot, t, k).start()
            return carry
        lax.fori_loop(0, tm, body, 0, unroll=DMA_UNROLL)

    @pl.when(i == 0)
    def _():
        issue_tile(0, 0)

    @pl.when(i + 1 < n)
    def _():
        issue_tile(i + 1, (i + 1) % 2)

    slot = i % 2

    for k in range(TOP_K):
        pltpu.make_async_copy(y_hbm.at[pl.ds(0, tm)], ybuf.at[slot, k], sems.at[slot]).wait()

    rt = rt_ref[...]
    g1 = rt[:, 2:3]
    g2 = rt[:, 3:4]
    lo1, hi1 = _unpack_halves(ybuf[slot, 0])
    lo2, hi2 = _unpack_halves(ybuf[slot, 1])
    o_ref[:, :dh] = x1_ref[:, :dh] + gt_ref[:, :dh] * (g1 * lo1 + g2 * lo2)
    o_ref[:, dh:] = x1_ref[:, dh:] + gt_ref[:, dh:] * (g1 * hi1 + g2 * hi2)


def _combine(dest, x1, route, gt2, ypk, S):
    T, D = x1.shape
    tm = ROW_TILE
    per_b = S // tm
    return pl.pallas_call(
        _combine_kernel,
        grid_spec=pltpu.PrefetchScalarGridSpec(
            num_scalar_prefetch=1,
            grid=(T // tm,),
            in_specs=[pl.BlockSpec((tm, D), lambda i, d: (i, 0)),
                      pl.BlockSpec((tm, LANES), lambda i, d: (i, 0)),
                      pl.BlockSpec((None, 1, D), lambda i, d: (i // per_b, 0, 0)),
                      pl.BlockSpec(memory_space=pl.ANY)],
            out_specs=pl.BlockSpec((tm, D), lambda i, d: (i, 0)),
            scratch_shapes=[pltpu.VMEM((2, TOP_K, tm, D // 2), U32),
                            pltpu.SemaphoreType.DMA((2,))]),
        out_shape=jax.ShapeDtypeStruct((T, D), F32),
        compiler_params=_cparams(("arbitrary",), 32),
    )(dest, x1, route, gt2, ypk)


def kernel(x, c, w_ada, b_ada, norm1_g, norm2_g, w_in, qn_g, kn_g, rel_bias, conv_w, A_log, dt_bias,
           onorm_g, w_out, w_rg, b_rg, w_re, b_re, w1, w3, w2):
    B, S, D = x.shape
    depth = w_ada.shape[0]
    HA, hda = rel_bias.shape[1], qn_g.shape[-1]
    HB, hdb = A_log.shape[-1], onorm_g.shape[-1]
    DA, DB = HA * hda, HB * hdb
    NG, NE = w_rg.shape[-1], w_re.shape[-1]
    T = B * S
    assert hdb == LANES and 2 * hda == LANES and DA + DB == D and 16 * HB <= LANES
    assert NE + NG <= LANES and conv_w.shape[1] == 5 and S % ROW_TILE == 0
    n_main = 3 * DA + 4 * DB
    p_rows = TOP_K * T + NE * MOE_RB

    bias_prof = _attn_bias_profiles(rel_bias, S)
    w_in_t = jnp.swapaxes(w_in, 1, 2)
    x2 = x.reshape(T, D)
    for l in range(depth):
        mod = _ada(c, w_ada[l], b_ada[l]).reshape(B, 6, 1, D)
        sh1, sc1, gt1, sh2, sc2, gt2 = (mod[:, i] for i in range(6))

        w_gate = jnp.tile(w_in_t[l][n_main:, :].T, (1, 4))
        w_gate = _hilo_weights(jnp.pad(w_gate, ((0, 0), (0, LANES - w_gate.shape[1]))))
        proj, gat = _inproj(x2, norm1_g[l].reshape(1, D), sc1, sh1, w_in_t[l], n_main, w_gate, S)

        def gate_row(p):
            grp = jnp.concatenate([p.reshape(-1), jnp.zeros((2 * HB,), F32)])
            return jnp.pad(jnp.tile(grp, 4), (0, LANES - 16 * HB)).reshape(1, LANES)

        pack = _gating(gat, gate_row(A_log[l]), gate_row(dt_bias[l]), HB)

        oa = _attention(proj, jnp.tile(qn_g[l], 2).reshape(1, LANES), jnp.tile(kn_g[l], 2).reshape(1, LANES),
                        bias_prof, B, S, HA, hda)
        cw4 = jnp.transpose(conv_w[l].reshape(conv_w.shape[1], 3, HB, hdb), (2, 1, 0, 3))
        ob = _gdn(proj, cw4, pack.reshape(B, S, LANES), onorm_g[l].reshape(1, LANES), B, S, HB, 3 * DA // LANES)

        wr = _hilo_weights(jnp.pad(jnp.concatenate([w_re[l], w_rg[l]], axis=1), ((0, 0), (0, LANES - NE - NG))))
        br = jnp.pad(jnp.concatenate([b_re[l], b_rg[l]]), (0, LANES - NE - NG)).reshape(1, LANES)
        x1, hpk, logits, xs_zero = _outproj(oa, ob, x2, gt1, w_out[l].astype(BF16), norm2_g[l].reshape(1, D),
                                            sc2, sh2, wr, br, S, p_rows)
        route, info = _route(logits, NE, NG)
        dest = route[:, 4:4 + TOP_K].astype(I32).reshape(TOP_K * T)
        we, ws, wn, wt = _moe_schedule(info, NE, p_rows)
        xs = _dispatch(dest, hpk, xs_zero)
        ypk = _moe(we, ws, wn, wt, xs, w1[l], w3[l], w2[l])
        x2 = _combine(dest, x1, route, gt2, ypk, S)
    return x2.reshape(B, S, D)
```

```python
import functools

import numpy as np
import jax
import jax.numpy as jnp
from jax import lax
from jax.experimental import pallas as pl
from jax.experimental.pallas import tpu as pltpu

F32 = jnp.float32
BF16 = jnp.bfloat16
I32 = jnp.int32
U32 = jnp.uint32
HIGHEST = lax.Precision.HIGHEST

EPS = 1e-6
NEG = -1e30
DILATED_BRANCHES = ((128, 1), (512, 4), (2048, 16))
REL_MAX_DIST = 1024
CHUNK = 64
TOP_K = 2

LANES = 128
SUBLANES = 8
MIB = 1 << 20

ATT_QB = 128
ATT_KW = 256
ATT_UNROLL = 8
ATT_PROF_W = 512
GDN_HP = 2
GDN_PREP_UNROLL = 4
GDN_SEG = 512
MOE_RB = 128
MOE_TALL = 512
MOE_RMAX = 512
MOE_WSLOTS = 3
MOE_CW = 512
ROW_TILE = 256
DMA_UNROLL = 8


def _cparams(sem, vmem_mib):
    return pltpu.CompilerParams(dimension_semantics=sem, vmem_limit_bytes=vmem_mib * MIB)


def _dot(a, b, **kw):
    return jnp.dot(a, b, preferred_element_type=F32, **kw)


def _dot_nt(a, b):
    return lax.dot_general(a, b, (((1,), (1,)), ((), ())), preferred_element_type=F32)


def _pick_tile(n, prefs):
    for t in prefs:
        if n % t == 0:
            return t
    return n


def _pack_halves(x):
    half = x.shape[1] // 2
    lo = lax.bitcast_convert_type(x[:, :half].astype(BF16).astype(F32), U32)
    hi = lax.bitcast_convert_type(x[:, half:].astype(BF16).astype(F32), U32)
    return lax.shift_right_logical(lo, jnp.uint32(16)) | (hi & jnp.uint32(0xFFFF0000))


def _silu(x):
    h = 0.5 * x
    return h + h * jnp.tanh(h)


def _hilo_weights(w):
    hi = w.astype(BF16)
    lo = (w - hi.astype(F32)).astype(BF16)
    return jnp.concatenate([hi, lo], axis=1)


def _dot_hilo(x, w2):
    m, n = x.shape[0], w2.shape[1] // 2
    hi = x.astype(BF16)
    lo = (x - hi.astype(F32)).astype(BF16)
    r = _dot(jnp.concatenate([hi, lo], axis=0), w2)
    return r[:m, :n] + (r[:m, n:] + r[m:, :n])


def _unpack_halves(p):
    lo = lax.bitcast_convert_type(lax.shift_left(p, jnp.uint32(16)), F32)
    hi = lax.bitcast_convert_type(p & jnp.uint32(0xFFFF0000), F32)
    return lo, hi


def _ada_kernel(c_ref, w_ref, b_ref, o_ref):
    c = c_ref[...]
    s = _silu(c).astype(BF16)
    o_ref[...] = _dot(s, w_ref[...].astype(BF16)) + b_ref[...]


def _ada(c, w_ada, b_ada):
    B, D = c.shape
    N = w_ada.shape[1]
    tn = _pick_tile(N, (1024, 512, 256, 128))
    return pl.pallas_call(
        _ada_kernel,
        grid=(N // tn,),
        in_specs=[pl.BlockSpec((B, D), lambda j: (0, 0)),
                  pl.BlockSpec((D, tn), lambda j: (0, j)),
                  pl.BlockSpec((1, tn), lambda j: (0, j))],
        out_specs=pl.BlockSpec((B, tn), lambda j: (0, j)),
        out_shape=jax.ShapeDtypeStruct((B, N), F32),
        compiler_params=_cparams(("arbitrary",), 40),
    )(c, w_ada, b_ada.reshape(1, N))


def _inproj_kernel(x_ref, g_ref, sc_ref, sh_ref, w_ref, wg_ref, o_ref, og_ref, h_scr):
    @pl.when(pl.program_id(1) == 0)
    def _():
        x = x_ref[...]
        y = x * lax.rsqrt(jnp.mean(x * x, axis=-1, keepdims=True) + EPS) * g_ref[...]
        h = y * (1.0 + sc_ref[...]) + sh_ref[...]
        h_scr[...] = h.astype(BF16)
        og_ref[...] = _dot_hilo(h, wg_ref[...])

    o_ref[...] = _dot_nt(h_scr[...], w_ref[...].astype(BF16)).astype(o_ref.dtype)


def _inproj(x2, g, sc, sh, w_all, n_main, w_gate, S):
    T, D = x2.shape
    NM = n_main
    tm = _pick_tile(S, (1024, 512, 256, 128))
    tn = _pick_tile(NM, (1024, 512, 256, 128))
    per_b = S // tm
    return pl.pallas_call(
        _inproj_kernel,
        grid=(T // tm, NM // tn),
        in_specs=[pl.BlockSpec((tm, D), lambda i, j: (i, 0)),
                  pl.BlockSpec((1, D), lambda i, j: (0, 0)),
                  pl.BlockSpec((None, 1, D), lambda i, j: (i // per_b, 0, 0)),
                  pl.BlockSpec((None, 1, D), lambda i, j: (i // per_b, 0, 0)),
                  pl.BlockSpec((tn, D), lambda i, j: (j, 0)),
                  pl.BlockSpec((D, 2 * LANES), lambda i, j: (0, 0))],
        out_specs=[pl.BlockSpec((tm, tn), lambda i, j: (i, j)),
                   pl.BlockSpec((tm, LANES), lambda i, j: (i, 0))],
        out_shape=[jax.ShapeDtypeStruct((T, NM), BF16), jax.ShapeDtypeStruct((T, LANES), F32)],
        scratch_shapes=[pltpu.VMEM((tm, D), BF16)],
        compiler_params=_cparams(("arbitrary", "arbitrary"), 56),
    )(x2, g, sc, sh, w_all, w_gate)


def _gating_kernel(gat_ref, a_ref, dt_ref, o_ref, *, hb):
    gat = gat_ref[...]
    tm = gat.shape[0]
    gw = 4 * hb
    g = -jnp.exp(a_ref[...]) * jax.nn.softplus(gat + dt_ref[...])
    beta = jax.nn.sigmoid(gat)
    r = lax.broadcasted_iota(I32, (tm, tm), 0)
    c = lax.broadcasted_iota(I32, (tm, tm), 1)
    same = (r // CHUNK) == (c // CHUNK)
    pre = _dot(jnp.where(same & (c <= r), 1.0, 0.0), g, precision=HIGHEST)
    suf = _dot(jnp.where(same & (c >= r), 1.0, 0.0), g, precision=HIGHEST)
    lane = lax.broadcasted_iota(I32, gat.shape, 1)
    o_ref[...] = jnp.where(lane // gw == 0, jnp.where(lane % gw < hb, pre, suf), beta)


def _gating(gat, a_row, dt_row, hb):
    T = gat.shape[0]
    tm = ROW_TILE
    return pl.pallas_call(
        functools.partial(_gating_kernel, hb=hb),
        grid=(T // tm,),
        in_specs=[pl.BlockSpec((tm, LANES), lambda i: (i, 0)),
                  pl.BlockSpec((1, LANES), lambda i: (0, 0)),
                  pl.BlockSpec((1, LANES), lambda i: (0, 0))],
        out_specs=pl.BlockSpec((tm, LANES), lambda i: (i, 0)),
        out_shape=jax.ShapeDtypeStruct((T, LANES), F32),
        compiler_params=_cparams(("arbitrary",), 16),
    )(gat, a_row, dt_row)


def _t5_bucket(rel, n_buckets):
    half = n_buckets // 2
    max_exact = half // 2
    n = np.abs(rel)
    large = max_exact + (np.log(np.maximum(n, 1) / max_exact) / np.log(REL_MAX_DIST / max_exact)
                         * (half - max_exact)).astype(np.int32)
    large = np.minimum(large, half - 1)
    return (np.where(rel > 0, half, 0) + np.where(n < max_exact, n, large)).astype(np.int32)


def _attn_plan(S):
    plan, base = [], 0
    for window, dil in DILATED_BRANCHES:
        n = window // (2 * dil)
        L = S // dil
        assert L % ATT_QB == 0 and n * 2 == ATT_QB
        kw = min(ATT_KW, L)
        nbq = L // ATT_QB
        nvar = 1 if nbq == 1 else 3
        plan.append((dil, L, nbq, kw, base, nvar, n))
        base += nvar
    return tuple(plan), base


def _attn_bias_profiles(rel_bias, S):
    plan, nvar_total = _attn_plan(S)
    nbuckets, H = rel_bias.shape
    u = np.arange(ATT_PROF_W) - ATT_QB
    onehots, bands = [], []
    for dil, L, nbq, kw, base, nvar, n in plan:
        offs = [0] if nvar == 1 else [0, -n, -(kw - ATT_QB)]
        for off in offs:
            rel = off + u
            onehots.append(np.eye(nbuckets, dtype=np.float32)[_t5_bucket(rel * dil, nbuckets)])
            bands.append(np.abs(rel) <= n)
    onehot = jnp.asarray(np.stack(onehots))
    band = jnp.asarray(np.stack(bands))
    prof = jnp.einsum('vwn,nh->hvw', onehot, rel_bias.astype(F32), precision=HIGHEST)
    prof = jnp.where(band[None], prof, NEG)
    prof = prof.reshape(H // 2, 2, nvar_total, ATT_PROF_W)
    return jnp.transpose(prof, (0, 2, 1, 3)).reshape(H // 2, 2 * nvar_total, ATT_PROF_W)


def _attn_kernel(q_ref, k_ref, v_ref, qg_ref, kg_ref, prof_ref, o_ref,
                 qn_scr, kn_scr, v_scr, ob_scr, mb_scr, db_scr, bias_ref, *, plan, hd):
    S = q_ref.shape[0]
    lane = lax.broadcasted_iota(I32, (1, LANES), 1)
    left = lane < hd

    @pl.when(pl.program_id(1) == 0)
    def _():
        for row in range(prof_ref.shape[0]):
            rep = jnp.broadcast_to(prof_ref[row:row + 1, :], (ATT_QB, ATT_PROF_W))
            skew = pltpu.roll(rep, 0, 1, stride=1, stride_axis=0)
            bias_ref[row // 2, row % 2] = skew[:, ATT_QB:ATT_QB + ATT_KW]

    same_head = (lax.broadcasted_iota(I32, (2 * LANES, LANES), 0) % LANES) // hd == \
        lax.broadcasted_iota(I32, (2 * LANES, LANES), 1) // hd
    avg = jnp.where(same_head, 1.0 / hd, 0.0).astype(BF16)

    def headnorm(x, g):
        x2 = x * x
        hi = x2.astype(BF16)
        lo = (x2 - hi.astype(F32)).astype(BF16)
        ms = _dot(jnp.concatenate([hi, lo], axis=1), avg)
        return x * lax.rsqrt(ms + EPS) * g

    qn_scr[...] = headnorm(q_ref[...].astype(F32), qg_ref[...]) * (hd ** -0.5)
    kn_scr[...] = headnorm(k_ref[...].astype(F32), kg_ref[...])
    v_scr[...] = v_ref[...].astype(F32)

    for bi, (dil, L, nbq, kw, base, nvar, n) in enumerate(plan):
        ones = jnp.ones((kw, LANES), BF16)

        def body(t, carry, dil=dil, L=L, nbq=nbq, kw=kw, base=base, nvar=nvar, n=n, bi=bi, ones=ones):
            blocks = []
            for uu in range(ATT_UNROLL):
                idx = t * ATT_UNROLL + uu
                r = idx // nbq
                i = idx % nbq
                q0 = i * ATT_QB
                k0 = jnp.clip(q0 - n, 0, L - kw)
                var = base if nvar == 1 else base + jnp.where(i > 0, 1, 0) + jnp.where(i == nbq - 1, 1, 0)
                if dil == 1:
                    qrows = pl.ds(pl.multiple_of(q0, ATT_QB), ATT_QB)
                    krows = pl.ds(pl.multiple_of(k0, CHUNK), kw)
                else:
                    qrows = pl.ds(r + q0 * dil, ATT_QB, stride=dil)
                    krows = pl.ds(r + k0 * dil, kw, stride=dil)
                qb = qn_scr[qrows, :]
                q2 = jnp.concatenate([jnp.where(left, qb, 0.0), jnp.where(left, 0.0, qb)], axis=0).astype(BF16)
                blocks.append((qrows, krows, var, q2))
            def score(group):
                return [_dot_nt(q2, kn_scr[krows, :].astype(BF16)) for qrows, krows, var, q2 in group]

            def softmax(group, scores):
                probs, maxes = [], []
                for (qrows, krows, var, q2), s in zip(group, scores):
                    s = s + jnp.concatenate([bias_ref[var, 0][:, :kw], bias_ref[var, 1][:, :kw]], axis=0)
                    m = jnp.max(s, axis=-1, keepdims=True)
                    probs.append(jnp.exp(s - m).astype(BF16))
                    maxes.append(m)
                return probs, maxes

            def values(group, probs):
                return [_dot(p, jnp.concatenate([v_scr[krows, :].astype(BF16), ones], axis=1))
                        for (qrows, krows, var, q2), p in zip(group, probs)]

            def finish(group, maxes, outs):
                for (qrows, krows, var, q2), m, od in zip(group, maxes, outs):
                    mb = jnp.broadcast_to(m, (2 * ATT_QB, LANES))
                    ob_scr[bi, qrows, :] = jnp.where(left, od[:ATT_QB, :LANES], od[ATT_QB:, :LANES])
                    mb_scr[bi, qrows, :] = jnp.where(left, mb[:ATT_QB], mb[ATT_QB:])
                    db_scr[bi, qrows, :] = jnp.where(left, od[:ATT_QB, LANES:], od[ATT_QB:, LANES:])

            ga, gb = blocks[:ATT_UNROLL // 2], blocks[ATT_UNROLL // 2:]
            sa = score(ga)
            sb = score(gb)
            pa, ma = softmax(ga, sa)
            oa = values(ga, pa)
            pb, mbx = softmax(gb, sb)
            ob = values(gb, pb)
            finish(ga, ma, oa)
            finish(gb, mbx, ob)
            return carry

        assert (dil * nbq) % ATT_UNROLL == 0
        lax.fori_loop(0, dil * nbq // ATT_UNROLL, body, 0)

    nb = len(plan)
    mx = mb_scr[0]
    for bi in range(1, nb):
        mx = jnp.maximum(mx, mb_scr[bi])
    num = jnp.zeros((S, LANES), F32)
    den = jnp.zeros((S, LANES), F32)
    for bi in range(nb):
        w = jnp.exp(mb_scr[bi] - mx)
        num = num + w * ob_scr[bi]
        den = den + w * db_scr[bi]
    o_ref[...] = (num / den).astype(o_ref.dtype)


def _attention(proj, qg2, kg2, profiles, B, S, HA, hd):
    T = proj.shape[0]
    pairs = HA // 2
    da_blocks = HA * hd // LANES
    plan, nvar = _attn_plan(S)
    return pl.pallas_call(
        functools.partial(_attn_kernel, plan=plan, hd=hd),
        grid=(pairs, B),
        in_specs=[pl.BlockSpec((S, LANES), lambda p, b: (b, p)),
                  pl.BlockSpec((S, LANES), lambda p, b: (b, da_blocks + p)),
                  pl.BlockSpec((S, LANES), lambda p, b: (b, 2 * da_blocks + p)),
                  pl.BlockSpec((1, LANES), lambda p, b: (0, 0)),
                  pl.BlockSpec((1, LANES), lambda p, b: (0, 0)),
                  pl.BlockSpec((None, 2 * nvar, ATT_PROF_W), lambda p, b: (p, 0, 0))],
        out_specs=pl.BlockSpec((S, LANES), lambda p, b: (b, p)),
        out_shape=jax.ShapeDtypeStruct((T, HA * hd), BF16),
        scratch_shapes=[pltpu.VMEM((S, LANES), F32)] * 3 + [pltpu.VMEM((len(plan), S, LANES), F32)] * 3
        + [pltpu.VMEM((nvar, 2, ATT_QB, ATT_KW), F32)],
        compiler_params=_cparams(("arbitrary", "arbitrary"), 40),
    )(proj, proj, proj, qg2, kg2, profiles)


def _gdn_kernel(q_ref, k_ref, v_ref, z_ref, cw_ref, pack_ref, og_ref, o_ref,
                q_scr, k_scr, v_scr, xpad_scr, pk2_scr, sel_scr, u_scr, wq_scr, at_scr, kdt_scr, et_scr, oacc_scr,
                *, hb, hp):
    S = q_ref.shape[0]
    P2 = 2 * CHUNK
    W2 = 2 * LANES
    npair = S // P2
    hg = pl.program_id(1)
    gw = 4 * hb
    dk = LANES

    pad = SUBLANES
    for slot in range(2):
        xpad_scr[slot, 0:pad, :] = jnp.zeros((pad, LANES), F32)
        xpad_scr[slot, pad + S:, :] = jnp.zeros((pad, LANES), F32)

    seg = GDN_SEG if S % GDN_SEG == 0 else S

    def conv_silu_to(src_ref, lanes_j, j, which, dst_scr, l2norm, scale):
        xp = xpad_scr.at[(3 * j + which) % 2]
        for s0 in range(0, S, seg):
            xp[pad + s0:pad + s0 + seg, :] = src_ref[s0:s0 + seg, lanes_j].astype(F32)
        for s0 in range(0, S, seg):
            acc = xp[pad + s0:pad + s0 + seg, :] * cw_ref[j, which, 2:3, :]
            for d in (-2, -1, 1, 2):
                acc = acc + xp[pad + s0 + d:pad + s0 + d + seg, :] * cw_ref[j, which, 2 + d:3 + d, :]
            y = _silu(acc)
            if l2norm:
                y = y * (lax.rsqrt(jnp.sum(y * y, axis=-1, keepdims=True) + EPS) * scale)
            dst_scr[j, s0:s0 + seg, :] = y

    for s0 in range(0, S, seg):
        pk = pack_ref[s0:s0 + seg, :]
        p_hi = pk.astype(BF16)
        p_lo = (pk - p_hi.astype(F32)).astype(BF16)
        pk2_scr[s0:s0 + seg, :] = jnp.concatenate([p_hi, p_lo], axis=1)
    srow = lax.broadcasted_iota(I32, (W2, 4 * LANES), 0) % LANES
    scol = lax.broadcasted_iota(I32, (W2, 4 * LANES), 1) // LANES
    for j in range(hp):
        lanes_j = slice(j * LANES, (j + 1) * LANES)
        conv_silu_to(q_ref, lanes_j, j, 0, q_scr, True, dk ** -0.5)
        conv_silu_to(k_ref, lanes_j, j, 1, k_scr, True, 1.0)
        conv_silu_to(v_ref, lanes_j, j, 2, v_scr, False, 1.0)
        src = hg * hp + j + jnp.where(scol < 2, scol * hb, gw + 2 * hb + (scol - 2) * hb)
        sel_scr[j] = jnp.where(srow == src, 1.0, 0.0).astype(BF16)
        oacc_scr[j] = jnp.zeros((S, LANES), F32)

    r4 = lax.broadcasted_iota(I32, (CHUNK, W2), 0)
    l4 = lax.broadcasted_iota(I32, (CHUNK, W2), 1)
    c4 = l4 % CHUNK
    blk4 = l4 // CHUNK
    lo_half = (l4 % LANES) < CHUNK
    ahead = jnp.where(l4 >= LANES, r4 - c4, c4 - r4)
    incl = ahead <= 0
    strict = ahead < 0
    bd16 = (r4 // 16) == (c4 // 16)

    def squeeze(x):
        return jnp.where(lo_half, x[:CHUNK], x[CHUNK:])

    def unsqueeze(x):
        return jnp.concatenate([jnp.where(lo_half, x, jnp.zeros_like(x)),
                                jnp.where(lo_half, jnp.zeros_like(x), x)], axis=0)

    def mm4(a, b):
        rhs = jnp.concatenate([jnp.where(blk4 == g, b, 0.0) for g in range(4)], axis=0)
        return _dot(a.astype(BF16), rhs.astype(BF16))

    first = lax.broadcasted_iota(I32, (P2, 1), 0) < CHUNK
    zpair = jnp.zeros((P2, LANES), BF16)

    def bdiag(x):
        return jnp.concatenate([jnp.concatenate([x[:, :LANES], zpair], axis=1),
                                jnp.concatenate([zpair, x[:, LANES:]], axis=1)], axis=0)

    U = GDN_PREP_UNROLL if npair % GDN_PREP_UNROLL == 0 else 1

    def prep(t, carry):
        cx = []
        chains = [(t * U + u, j) for u in range(U) for j in range(hp)]
        bcs = [_dot(pk2_scr[pl.ds(pl.multiple_of(m * P2, P2), P2), :], sel_scr[j]) for m, j in chains]
        for (m, j), bc in zip(chains, bcs):
            rows = pl.ds(pl.multiple_of(m * P2, P2), P2)
            kp = k_scr[j, rows, :]
            qp = q_scr[j, rows, :]
            vp = v_scr[j, rows, :]
            gc2 = bc[:, 0:W2]
            beta2 = bc[:, W2:2 * W2]
            gcf, gcb = gc2[:, :LANES], gc2[:, LANES:]
            tot2 = jnp.concatenate([jnp.where(first, gcf[CHUNK - 1:CHUNK], gcf[P2 - 1:P2]),
                                    jnp.where(first, gcb[0:1], gcb[CHUNK:CHUNK + 1])], axis=1)
            egc2 = jnp.exp(gc2)
            k2 = jnp.concatenate([kp, kp], axis=1)
            kb2 = k2 * beta2
            vb2 = jnp.concatenate([vp, vp], axis=1) * beta2
            kbe2 = kb2 * egc2
            cx.append(dict(
                m=m, j=j, rows=rows, vb2=vb2, kbe2=kbe2,
                qeb=(jnp.concatenate([qp, qp], axis=1) * egc2).astype(BF16),
                kd2=k2 * jnp.exp(tot2 - gc2),
                et=jnp.exp(tot2),
                dec=jnp.exp(jnp.where(
                    incl, squeeze(gc2) - squeeze(jnp.concatenate([gcf.T, gcb.T], axis=1)), -jnp.inf)),
                stk=jnp.concatenate([kb2[:, :LANES], kb2[:, LANES:], qp], axis=0).astype(BF16),
                kpb=kp.astype(BF16)))

        g3s = [_dot_nt(c['stk'], c['kpb']) for c in cx]
        for c, g3 in zip(cx, g3s):
            lm = jnp.where(strict, squeeze(jnp.concatenate([g3[:P2], g3[P2:2 * P2]], axis=1)) * c['dec'], 0.0)
            attn = squeeze(jnp.concatenate([g3[2 * P2:], g3[2 * P2:]], axis=1)) * c['dec']
            c['attn2'] = unsqueeze(attn).astype(BF16)
            c['lbd'] = jnp.where(bd16, lm, 0.0)
            c['loff'] = lm - c['lbd']
        nn = [-c['lbd'] for c in cx]
        pw = [mm4(c['lbd'], c['lbd']) for c in cx]
        for rnd in range(3):
            prod = [mm4(a, p) for a, p in zip(nn, pw)]
            nxt = [mm4(p, p) for p in pw] if rnd < 2 else pw
            nn = [a + p + q for a, p, q in zip(nn, pw, prod)]
            pw = nxt
        mo = [c['loff'] + x for c, x in zip(cx, [mm4(a, c['loff']) for a, c in zip(nn, cx)])]
        m2 = [mm4(x, x) for x in mo]
        mn = [mm4(x, a) for x, a in zip(mo, nn)]
        xo = [a - x - y for a, x, y in zip(nn, mo, mn)]
        mx = [mm4(a, b) for a, b in zip(m2, xo)]
        toff = [a + b + q for a, b, q in zip(xo, m2, mx)]
        z4 = jnp.zeros((CHUNK, W2), BF16)
        tws = []
        for c, tf in zip(cx, toff):
            vbb, kbb = c['vb2'].astype(BF16), c['kbe2'].astype(BF16)
            blocks = []
            for g in range(4):
                rr = slice((g % 2) * CHUNK, (g % 2 + 1) * CHUNK)
                ll = slice((g // 2) * LANES, (g // 2 + 1) * LANES)
                blocks.append(jnp.concatenate([z4] * g + [vbb[rr, ll], kbb[rr, ll]] + [z4] * (3 - g), axis=1))
            tws.append(_dot(tf.astype(BF16), jnp.concatenate(blocks, axis=0)))
        for c, tw in zip(cx, tws):
            m, j, rows = c['m'], c['j'], c['rows']
            def pair_layout(off):
                piece = lambda g: tw[:, 2 * g * LANES + off:2 * g * LANES + off + LANES]
                return jnp.concatenate([jnp.concatenate([piece(0), piece(2)], axis=1),
                                        jnp.concatenate([piece(1), piece(3)], axis=1)], axis=0)

            u2 = c['vb2'] + pair_layout(0)
            w2 = (c['kbe2'] + pair_layout(LANES)).astype(BF16)
            qeb = c['qeb']
            u_scr[j, rows, :] = u2
            wq_scr[j, pl.ds(pl.multiple_of(m * 2 * P2, 2 * P2), 2 * P2), :] = jnp.concatenate(
                [w2[:CHUNK], qeb[:CHUNK], w2[CHUNK:], qeb[CHUNK:]], axis=0)
            at_scr[j, rows, :] = c['attn2']
            kdt_scr[j, 0, :, rows] = c['kd2'][:, :LANES].T.astype(BF16)
            kdt_scr[j, 1, :, rows] = c['kd2'][:, LANES:].T.astype(BF16)
            et_scr[j, pl.ds(pl.multiple_of(m * 2 * SUBLANES, 2 * SUBLANES), 2 * SUBLANES), :] = jnp.concatenate(
                [c['et'][:SUBLANES], c['et'][CHUNK:CHUNK + SUBLANES]], axis=0)
        return carry

    lax.fori_loop(0, npair // U, prep, 0)

    zc = jnp.zeros((CHUNK, LANES), F32)
    zp = jnp.zeros((P2, LANES), F32)

    def place(v, cpos):
        return jnp.concatenate([v, zc] if cpos == 0 else [zc, v], axis=0)

    def scan(m, states):
        pf = m
        pb = npair - 1 - m
        rows_f = pl.ds(pl.multiple_of(pf * P2, P2), P2)
        rows_b = pl.ds(pl.multiple_of(pb * P2, P2), P2)
        hx = []
        for j in range(hp):
            hx.append(dict(
                u_f=u_scr[j, rows_f, :LANES], u_b=u_scr[j, rows_b, LANES:],
                at_f=at_scr[j, rows_f, :LANES], at_b=at_scr[j, rows_b, LANES:],
                kdt=jnp.concatenate([kdt_scr[j, 0, :, rows_f], kdt_scr[j, 1, :, rows_b]], axis=1),
                wq_f=wq_scr[j, pl.ds(pl.multiple_of(pf * 2 * P2, 2 * P2), 2 * P2), :LANES],
                wq_b=wq_scr[j, pl.ds(pl.multiple_of(pb * 2 * P2, 2 * P2), 2 * P2), LANES:],
                et_f=et_scr[j, pl.ds(pl.multiple_of(pf * 2 * SUBLANES, 2 * SUBLANES), 2 * SUBLANES), :LANES],
                et_b=et_scr[j, pl.ds(pl.multiple_of(pb * 2 * SUBLANES, 2 * SUBLANES), 2 * SUBLANES), LANES:]))
        sts = list(states)
        for step in range(2):
            cf, cb = step, 1 - step
            rrs = [_dot(jnp.concatenate([c['wq_f'][cf * P2:(cf + 1) * P2], c['wq_b'][cb * P2:(cb + 1) * P2]], axis=1),
                        bdiag(st.astype(BF16))) for c, st in zip(hx, sts)]
            ress = []
            for c, rr in zip(hx, rrs):
                u2 = jnp.concatenate([c['u_f'][cf * CHUNK:(cf + 1) * CHUNK],
                                      c['u_b'][cb * CHUNK:(cb + 1) * CHUNK]], axis=1)
                v_new = u2 - rr[:CHUNK]
                rhs = jnp.concatenate(
                    [jnp.concatenate([place(v_new[:, :LANES], cf), zp], axis=1),
                     jnp.concatenate([zp, place(v_new[:, LANES:], cb)], axis=1)], axis=0).astype(BF16)
                lhs = jnp.concatenate(
                    [jnp.concatenate([c['at_f'][cf * CHUNK:(cf + 1) * CHUNK],
                                      c['at_b'][cb * CHUNK:(cb + 1) * CHUNK]], axis=1),
                     c['kdt']], axis=0)
                ress.append(_dot(lhs, rhs))
            for j, (c, rr, res) in enumerate(zip(hx, rrs, ress)):
                o2 = rr[CHUNK:] + res[:CHUNK]
                et2 = jnp.concatenate([c['et_f'][cf * SUBLANES:cf * SUBLANES + 1],
                                       c['et_b'][cb * SUBLANES:cb * SUBLANES + 1]], axis=1)
                sts[j] = sts[j] * et2 + res[CHUNK:]
                of_rows = pl.ds(pl.multiple_of(pf * P2 + cf * CHUNK, CHUNK), CHUNK)
                ob_rows = pl.ds(pl.multiple_of(pb * P2 + cb * CHUNK, CHUNK), CHUNK)
                oacc_scr[j, of_rows, :] = oacc_scr[j, of_rows, :] + o2[:, :LANES]
                oacc_scr[j, ob_rows, :] = oacc_scr[j, ob_rows, :] + o2[:, LANES:]
        return tuple(sts)

    s0 = jnp.zeros((dk, W2), F32)
    lax.fori_loop(0, npair, scan, (s0,) * hp)

    for j in range(hp):
        lanes_j = slice(j * LANES, (j + 1) * LANES)
        for s0 in range(0, S, seg):
            o = oacc_scr[j, s0:s0 + seg, :]
            y = o * lax.rsqrt(jnp.mean(o * o, axis=-1, keepdims=True) + EPS) * og_ref[...]
            z = z_ref[s0:s0 + seg, lanes_j].astype(F32)
            o_ref[s0:s0 + seg, lanes_j] = (y * _silu(z)).astype(o_ref.dtype)


def _gdn(proj, cw4, pack3, onorm_g, B, S, HB, base_blk):
    T = proj.shape[0]
    hp = GDN_HP if (HB % GDN_HP == 0 and base_blk % GDN_HP == 0) else 1
    wblk = hp * LANES
    npair = S // (2 * CHUNK)

    def col(k):
        off = (base_blk + k * HB) // hp
        return lambda b, h: (b, off + h)

    return pl.pallas_call(
        functools.partial(_gdn_kernel, hb=HB, hp=hp),
        grid=(B, HB // hp),
        in_specs=[pl.BlockSpec((S, wblk), col(0)),
                  pl.BlockSpec((S, wblk), col(1)),
                  pl.BlockSpec((S, wblk), col(2)),
                  pl.BlockSpec((S, wblk), col(3)),
                  pl.BlockSpec((hp, 3, cw4.shape[2], LANES), lambda b, h: (h, 0, 0, 0)),
                  pl.BlockSpec((None, S, LANES), lambda b, h: (b, 0, 0)),
                  pl.BlockSpec((1, LANES), lambda b, h: (0, 0))],
        out_specs=pl.BlockSpec((S, wblk), lambda b, h: (b, h)),
        out_shape=jax.ShapeDtypeStruct((T, HB * LANES), BF16),
        scratch_shapes=[pltpu.VMEM((hp, S, LANES), F32)] * 3
        + [pltpu.VMEM((2, S + 2 * SUBLANES, LANES), F32),
           pltpu.VMEM((S, 2 * LANES), BF16),
           pltpu.VMEM((hp, 2 * LANES, 4 * LANES), BF16),
           pltpu.VMEM((hp, S, 2 * LANES), F32),
           pltpu.VMEM((hp, 2 * S, 2 * LANES), BF16),
           pltpu.VMEM((hp, S, 2 * LANES), BF16),
           pltpu.VMEM((hp, 2, LANES, S), BF16),
           pltpu.VMEM((hp, npair * 2 * SUBLANES, 2 * LANES), F32),
           pltpu.VMEM((hp, S, LANES), F32)],
        compiler_params=_cparams(("arbitrary", "arbitrary"), 56),
    )(proj, proj, proj, proj, cw4, pack3, onorm_g)


def _outproj_kernel(oa_ref, ob_ref, x_ref, gt_ref, w_ref, g_ref, sc_ref, sh_ref, wr_ref, br_ref,
                    x1_ref, hp_ref, lg_ref, z_ref, h_scr):
    z_ref[...] = jnp.zeros(z_ref.shape, U32)
    da = oa_ref.shape[1]
    y = _dot(oa_ref[...], w_ref[:da, :]) + _dot(ob_ref[...], w_ref[da:, :])
    x1 = x_ref[...] + gt_ref[...] * y
    x1_ref[...] = x1
    hn = x1 * lax.rsqrt(jnp.mean(x1 * x1, axis=-1, keepdims=True) + EPS) * g_ref[...]
    h = hn * (1.0 + sc_ref[...]) + sh_ref[...]
    hp_ref[...] = _pack_halves(h)
    h_scr[...] = h

    @pl.when(pl.program_id(0) < pl.num_programs(0))
    def _():
        lg_ref[...] = _dot_hilo(h_scr[...], wr_ref[...]) + br_ref[...]


def _outproj(oa, ob, x2, gt1, w_out_b, g2, sc2, sh2, wr, br, S, p_rows):
    T, D = x2.shape
    tm = ROW_TILE
    per_b = S // tm
    bmap = lambda i: (i // per_b, 0, 0)
    zrows = p_rows // (T // tm)
    assert zrows * (T // tm) == p_rows and zrows % SUBLANES == 0
    return pl.pallas_call(
        _outproj_kernel,
        grid=(T // tm,),
        in_specs=[pl.BlockSpec((tm, oa.shape[1]), lambda i: (i, 0)),
                  pl.BlockSpec((tm, ob.shape[1]), lambda i: (i, 0)),
                  pl.BlockSpec((tm, D), lambda i: (i, 0)),
                  pl.BlockSpec((None, 1, D), bmap),
                  pl.BlockSpec((D, D), lambda i: (0, 0)),
                  pl.BlockSpec((1, D), lambda i: (0, 0)),
                  pl.BlockSpec((None, 1, D), bmap),
                  pl.BlockSpec((None, 1, D), bmap),
                  pl.BlockSpec((D, 2 * LANES), lambda i: (0, 0)),
                  pl.BlockSpec((1, LANES), lambda i: (0, 0))],
        out_specs=[pl.BlockSpec((tm, D), lambda i: (i, 0)),
                   pl.BlockSpec((tm, D // 2), lambda i: (i, 0)),
                   pl.BlockSpec((tm, LANES), lambda i: (i, 0)),
                   pl.BlockSpec((zrows, D // 2), lambda i: (i, 0))],
        out_shape=[jax.ShapeDtypeStruct((T, D), F32), jax.ShapeDtypeStruct((T, D // 2), U32),
                   jax.ShapeDtypeStruct((T, LANES), F32), jax.ShapeDtypeStruct((p_rows, D // 2), U32)],
        scratch_shapes=[pltpu.VMEM((tm, D), F32)],
        compiler_params=_cparams(("arbitrary",), 48),
    )(oa, ob, x2, gt1, w_out_b, g2, sc2, sh2, wr, br)


def _route_kernel(lg_ref, o_ref, info_ref, run_scr, *, ne, ng):
    ph = pl.program_id(0)
    i = pl.program_id(1)

    @pl.when((ph == 0) & (i == 0))
    def _():
        run_scr[...] = jnp.zeros_like(run_scr)

    @pl.when((ph == 1) & (i == 0))
    def _():
        cnt = run_scr[...]
        padded = jnp.ceil(cnt * (1.0 / MOE_RB)) * MOE_RB
        k = lax.broadcasted_iota(I32, (LANES, LANES), 0)
        e = lax.broadcasted_iota(I32, (LANES, LANES), 1)
        start = _dot(padded, jnp.where(k < e, 1.0, 0.0), precision=HIGHEST)
        rowi = lax.broadcasted_iota(I32, cnt.shape, 0)
        info_ref[...] = jnp.where(rowi == 0, cnt, jnp.where(rowi == 1, padded, start))
        run_scr[...] = start

    lg = lg_ref[...]
    tm = lg.shape[0]
    epg = ne // ng
    lane_i = lax.broadcasted_iota(I32, lg.shape, 1)
    lane = lane_i.astype(F32)
    big = float(2 * LANES)
    is_g = (lane_i >= ne) & (lane_i < ne + ng)
    gl = jnp.where(is_g, lg, -jnp.inf)
    gmax = jnp.max(gl, axis=-1, keepdims=True)
    gidx = jnp.min(jnp.where(gl == gmax, lane, big), axis=-1, keepdims=True) - ne
    psel = 1.0 / jnp.sum(jnp.where(is_g, jnp.exp(gl - gmax), 0.0), axis=-1, keepdims=True)
    in_grp = (lane_i // epg).astype(F32) == gidx
    el = jnp.where(in_grp & (lane_i < ne), lg, -jnp.inf)
    m1 = jnp.max(el, axis=-1, keepdims=True)
    i1 = jnp.min(jnp.where(el == m1, lane, big), axis=-1, keepdims=True)
    el2 = jnp.where(lane == i1, -jnp.inf, el)
    m2 = jnp.max(el2, axis=-1, keepdims=True)
    i2 = jnp.min(jnp.where(el2 == m2, lane, big), axis=-1, keepdims=True)
    e21 = jnp.exp(m2 - m1)
    g1 = psel / (1.0 + e21)
    g2 = psel * e21 / (1.0 + e21)
    o1 = jnp.where(lane == i1, 1.0, 0.0)
    o2 = jnp.where(lane == i2, 1.0, 0.0)
    cnt = o1 + o2
    r = lax.broadcasted_iota(I32, (tm, tm), 0)
    c = lax.broadcasted_iota(I32, (tm, tm), 1)
    before = _dot(jnp.where(c < r, 1.0, 0.0).astype(BF16), cnt.astype(BF16)) + run_scr[0:1, :]
    d1 = jnp.sum(o1 * before, axis=-1, keepdims=True)
    d2 = jnp.sum(o2 * before, axis=-1, keepdims=True)
    run_scr[...] = run_scr[...] + jnp.sum(cnt, axis=0, keepdims=True)

    @pl.when(ph == 1)
    def _():
        out = jnp.zeros(lg.shape, F32)
        for j, val in enumerate((i1, i2, g1, g2, d1, d2)):
            out = jnp.where(lane_i == j, val, out)
        o_ref[...] = out


def _route(logits, ne, ng):
    T = logits.shape[0]
    tm = _pick_tile(T, (1024, 512, 256))
    return pl.pallas_call(
        functools.partial(_route_kernel, ne=ne, ng=ng),
        grid=(2, T // tm),
        in_specs=[pl.BlockSpec((tm, LANES), lambda p, i: (i, 0))],
        out_specs=[pl.BlockSpec((tm, LANES), lambda p, i: (i * p, 0)),
                   pl.BlockSpec((SUBLANES, LANES), lambda p, i: (0, 0))],
        out_shape=[jax.ShapeDtypeStruct((T, LANES), F32), jax.ShapeDtypeStruct((SUBLANES, LANES), F32)],
        scratch_shapes=[pltpu.VMEM((SUBLANES, LANES), F32)],
        compiler_params=_cparams(("arbitrary", "arbitrary"), 16),
    )(logits)


def _dispatch_kernel(dest_ref, h_hbm, xs_in, xs_hbm, sem):
    del xs_in
    tm = ROW_TILE
    base = pl.program_id(0) * tm

    def copy(tok, a):
        return pltpu.make_async_copy(h_hbm.at[pl.ds(tok, 1)], xs_hbm.at[pl.ds(dest_ref[a], 1)], sem)

    def issue(t, carry):
        for k in range(TOP_K):
            copy(base + t, (base + t) * TOP_K + k).start()
        return carry

    lax.fori_loop(0, tm, issue, 0, unroll=DMA_UNROLL)

    def retire_step():
        for k in range(TOP_K):
            pltpu.make_async_copy(h_hbm.at[pl.ds(0, tm)], xs_hbm.at[pl.ds(0, tm)], sem).wait()

    @pl.when(pl.program_id(0) > 0)
    def _():
        retire_step()

    @pl.when(pl.program_id(0) == pl.num_programs(0) - 1)
    def _():
        retire_step()


def _dispatch(dest, hpk, xs_zero):
    T, dh = hpk.shape
    p_rows = xs_zero.shape[0]
    return pl.pallas_call(
        _dispatch_kernel,
        grid_spec=pltpu.PrefetchScalarGridSpec(
            num_scalar_prefetch=1,
            grid=(T // ROW_TILE,),
            in_specs=[pl.BlockSpec(memory_space=pl.ANY), pl.BlockSpec(memory_space=pl.ANY)],
            out_specs=pl.BlockSpec(memory_space=pl.ANY),
            scratch_shapes=[pltpu.SemaphoreType.DMA(())]),
        out_shape=jax.ShapeDtypeStruct((p_rows, dh), U32),
        input_output_aliases={2: 0},
        compiler_params=_cparams(("arbitrary",), 16),
    )(dest, hpk, xs_zero)


def _moe_kernel(we_ref, ws_ref, wn_ref, wt_ref, xs_hbm, w1_hbm, w3_hbm, w2_hbm, y_hbm,
                xbuf, xlo, xhi, yacc, ypk, w1buf, w3buf, w2buf, sem_in, sem_out, sem_w):
    w = pl.program_id(0)
    c = pl.program_id(1)
    nw = pl.num_programs(0)
    nc = pl.num_programs(1)
    nrows = wn_ref[w]
    start = ws_ref[w]
    nblk = nrows // MOE_RB
    dh = xbuf.shape[1]
    cw = w1buf.shape[-1]

    g = w * nc + c

    def w_copies(step, do):
        item = step // nc
        chunk = step % nc

        @pl.when((item < nw) & (wn_ref[jnp.minimum(item, nw - 1)] > 0))
        def _():
            e = we_ref[item]
            cols = pl.ds(pl.multiple_of(chunk * cw, cw), cw)
            slot = step % MOE_WSLOTS
            do(pltpu.make_async_copy(w1_hbm.at[e, :, cols], w1buf.at[slot], sem_w.at[slot]))
            do(pltpu.make_async_copy(w3_hbm.at[e, :, cols], w3buf.at[slot], sem_w.at[slot]))
            do(pltpu.make_async_copy(w2_hbm.at[e, cols, :], w2buf.at[slot], sem_w.at[slot]))

    @pl.when(g == 0)
    def _():
        for ahead in range(MOE_WSLOTS - 1):
            w_copies(ahead, lambda cp: cp.start())

    w_copies(g + MOE_WSLOTS - 1, lambda cp: cp.start())
    w_copies(g, lambda cp: cp.wait())
    wslot = g % MOE_WSLOTS
    w1_ref, w3_ref, w2_ref = w1buf.at[wslot], w3buf.at[wslot], w2buf.at[wslot]

    def blk_rows(rb):
        return pl.ds(pl.multiple_of(rb * MOE_RB, MOE_RB), MOE_RB)

    def hbm_rows(item_start, rb):
        return pl.ds(pl.multiple_of(item_start + rb * MOE_RB, MOE_RB), MOE_RB)

    def in_copy(item_start, rb):
        return pltpu.make_async_copy(xs_hbm.at[hbm_rows(item_start, rb)], xbuf.at[blk_rows(rb)], sem_in)

    def out_copy(item_start, rb):
        return pltpu.make_async_copy(ypk.at[blk_rows(rb)], y_hbm.at[hbm_rows(item_start, rb)], sem_out)

    def each_block(n, fn):
        def body(rb, carry):
            fn(rb)
            return carry
        lax.fori_loop(0, n, body, 0)

    @pl.when(c == 0)
    def _load():
        @pl.when(w == 0)
        def _():
            each_block(nblk, lambda rb: in_copy(start, rb).start())

        each_block(nblk, lambda rb: in_copy(start, rb).wait())

        def unpack(rb):
            lo, hi = _unpack_halves(xbuf[blk_rows(rb), :])
            xlo[blk_rows(rb), :] = lo.astype(BF16)
            xhi[blk_rows(rb), :] = hi.astype(BF16)
            yacc[blk_rows(rb), :] = jnp.zeros((MOE_RB, 2 * dh), F32)

        each_block(nblk, unpack)

        @pl.when(w + 1 < nw)
        def _():
            nxt = ws_ref[w + 1]
            each_block(wn_ref[w + 1] // MOE_RB, lambda rb: in_copy(nxt, rb).start())

    @pl.when(nrows > 0)
    def _compute():
        def rows_block(row0, nr):
            rows = pl.ds(row0, nr)
            xl = xlo[rows, :]
            xh = xhi[rows, :]
            h1 = _dot(xl, w1_ref[:dh, :].astype(BF16)) + _dot(xh, w1_ref[dh:, :].astype(BF16))
            h3 = _dot(xl, w3_ref[:dh, :].astype(BF16)) + _dot(xh, w3_ref[dh:, :].astype(BF16))
            hid = (_silu(h1) * h3).astype(BF16)
            yacc[rows, :] = yacc[rows, :] + _dot(hid, w2_ref[...].astype(BF16))

        ntall = nrows // MOE_TALL
        each_block(ntall, lambda i: rows_block(pl.multiple_of(i * MOE_TALL, MOE_TALL), MOE_TALL))

        rem = nrows - ntall * MOE_TALL
        for nr in range(MOE_RB, MOE_TALL, MOE_RB):
            @pl.when(rem == nr)
            def _(nr=nr):
                rows_block(pl.multiple_of(ntall * MOE_TALL, MOE_TALL), nr)

    @pl.when(c == nc - 1)
    def _store():
        @pl.when(w > 0)
        def _():
            prev = ws_ref[w - 1]
            each_block(wn_ref[w - 1] // MOE_RB, lambda rb: out_copy(prev, rb).wait())

        def pack(rb):
            ypk[blk_rows(rb), :] = _pack_halves(yacc[blk_rows(rb), :])

        each_block(nblk, pack)
        each_block(nblk, lambda rb: out_copy(start, rb).start())

        @pl.when(w == nw - 1)
        def _():
            each_block(nblk, lambda rb: out_copy(start, rb).wait())

    @pl.when((w == nw - 1) & (c == nc - 1))
    def _zero_tail():
        ypk[blk_rows(0), :] = jnp.zeros((MOE_RB, dh), U32)
        first = wt_ref[0] // MOE_RB

        def tail_copy(b):
            return pltpu.make_async_copy(ypk.at[blk_rows(0)],
                                         y_hbm.at[pl.ds(pl.multiple_of(b * MOE_RB, MOE_RB), MOE_RB)], sem_out)

        def start(b, carry):
            tail_copy(b).start()
            return carry

        def wait(b, carry):
            tail_copy(b).wait()
            return carry

        lax.fori_loop(first, y_hbm.shape[0] // MOE_RB, start, 0)
        lax.fori_loop(first, y_hbm.shape[0] // MOE_RB, wait, 0)


def _moe(we, ws, wn, wt, xs, w1, w3, w2):
    P, dh = xs.shape
    D = 2 * dh
    NE, _, DE = w1.shape
    cw = min(MOE_CW, DE)
    nc = DE // cw
    nw = we.shape[0]
    return pl.pallas_call(
        _moe_kernel,
        grid_spec=pltpu.PrefetchScalarGridSpec(
            num_scalar_prefetch=4,
            grid=(nw, nc),
            in_specs=[pl.BlockSpec(memory_space=pl.ANY)] * 4,
            out_specs=pl.BlockSpec(memory_space=pl.ANY),
            scratch_shapes=[pltpu.VMEM((MOE_RMAX, dh), U32),
                            pltpu.VMEM((MOE_RMAX, dh), BF16),
                            pltpu.VMEM((MOE_RMAX, dh), BF16),
                            pltpu.VMEM((MOE_RMAX, D), F32),
                            pltpu.VMEM((MOE_RMAX, dh), U32),
                            pltpu.VMEM((MOE_WSLOTS, D, cw), F32),
                            pltpu.VMEM((MOE_WSLOTS, D, cw), F32),
                            pltpu.VMEM((MOE_WSLOTS, cw, D), F32),
                            pltpu.SemaphoreType.DMA(()),
                            pltpu.SemaphoreType.DMA(()),
                            pltpu.SemaphoreType.DMA((MOE_WSLOTS,))]),
        out_shape=jax.ShapeDtypeStruct((P, dh), U32),
        compiler_params=_cparams(("arbitrary", "arbitrary"), 56),
    )(we, ws, wn, wt, xs, w1, w3, w2)


def _moe_schedule(info, NE, p_rows):
    padded = info[1, :NE].astype(I32)
    start_pad = info[2, :NE].astype(I32)
    items = (padded + MOE_RMAX - 1) // MOE_RMAX
    cum_items = jnp.cumsum(items)
    n_items = cum_items[-1]
    nw = (p_rows + NE * (MOE_RMAX - MOE_RB)) // MOE_RMAX
    wi = jnp.arange(nw, dtype=I32)
    valid = wi < n_items
    wi_c = jnp.minimum(wi, jnp.maximum(n_items - 1, 0))
    we = jnp.minimum(jnp.sum(cum_items[None, :] <= wi_c[:, None], axis=1), NE - 1).astype(I32)
    onehot = we[:, None] == jnp.arange(NE, dtype=I32)[None, :]
    pick = lambda v: jnp.sum(jnp.where(onehot, v[None, :], 0), axis=1)
    local = wi_c - (pick(cum_items) - pick(items))
    ws = (pick(start_pad) + local * MOE_RMAX).astype(I32)
    wn = jnp.where(valid, jnp.clip(pick(padded) - local * MOE_RMAX, 0, MOE_RMAX), 0).astype(I32)
    wt = (start_pad[NE - 1] + padded[NE - 1]).reshape(1)
    return we, ws, wn, wt


def _combine_kernel(dest_ref, x1_ref, rt_ref, gt_ref, y_hbm, o_ref, ybuf, sems):
    tm = ROW_TILE
    i = pl.program_id(0)
    n = pl.num_programs(0)
    dh = ybuf.shape[-1]

    def copy(tile, slot, t, k):
        a = (tile * tm + t) * TOP_K + k
        return pltpu.make_async_copy(y_hbm.at[pl.ds(dest_ref[a], 1)], ybuf.at[slot, k, pl.ds(t, 1)], sems.at[slot])

    def issue_tile(tile, slot):
        def body(t, carry):
            for k in range(TOP_K):
                copy(tile, slot, t, k).start()
            return carry
        lax.fori_loop(0, tm, body, 0, unroll=DMA_UNROLL)

    @pl.when(i == 0)
    def _():
        issue_tile(0, 0)

    @pl.when(i + 1 < n)
    def _():
        issue_tile(i + 1, (i + 1) % 2)

    slot = i % 2

    for k in range(TOP_K):
        pltpu.make_async_copy(y_hbm.at[pl.ds(0, tm)], ybuf.at[slot, k], sems.at[slot]).wait()

    rt = rt_ref[...]
    g1 = rt[:, 2:3]
    g2 = rt[:, 3:4]
    lo1, hi1 = _unpack_halves(ybuf[slot, 0])
    lo2, hi2 = _unpack_halves(ybuf[slot, 1])
    o_ref[:, :dh] = x1_ref[:, :dh] + gt_ref[:, :dh] * (g1 * lo1 + g2 * lo2)
    o_ref[:, dh:] = x1_ref[:, dh:] + gt_ref[:, dh:] * (g1 * hi1 + g2 * hi2)


def _combine(dest, x1, route, gt2, ypk, S):
    T, D = x1.shape
    tm = ROW_TILE
    per_b = S // tm
    return pl.pallas_call(
        _combine_kernel,
        grid_spec=pltpu.PrefetchScalarGridSpec(
            num_scalar_prefetch=1,
            grid=(T // tm,),
            in_specs=[pl.BlockSpec((tm, D), lambda i, d: (i, 0)),
                      pl.BlockSpec((tm, LANES), lambda i, d: (i, 0)),
                      pl.BlockSpec((None, 1, D), lambda i, d: (i // per_b, 0, 0)),
                      pl.BlockSpec(memory_space=pl.ANY)],
            out_specs=pl.BlockSpec((tm, D), lambda i, d: (i, 0)),
            scratch_shapes=[pltpu.VMEM((2, TOP_K, tm, D // 2), U32),
                            pltpu.SemaphoreType.DMA((2,))]),
        out_shape=jax.ShapeDtypeStruct((T, D), F32),
        compiler_params=_cparams(("arbitrary",), 32),
    )(dest, x1, route, gt2, ypk)


def kernel(x, c, w_ada, b_ada, norm1_g, norm2_g, w_in, qn_g, kn_g, rel_bias, conv_w, A_log, dt_bias,
           onorm_g, w_out, w_rg, b_rg, w_re, b_re, w1, w3, w2):
    B, S, D = x.shape
    depth = w_ada.shape[0]
    HA, hda = rel_bias.shape[1], qn_g.shape[-1]
    HB, hdb = A_log.shape[-1], onorm_g.shape[-1]
    DA, DB = HA * hda, HB * hdb
    NG, NE = w_rg.shape[-1], w_re.shape[-1]
    T = B * S
    assert hdb == LANES and 2 * hda == LANES and DA + DB == D and 16 * HB <= LANES
    assert NE + NG <= LANES and conv_w.shape[1] == 5 and S % ROW_TILE == 0
    n_main = 3 * DA + 4 * DB
    p_rows = TOP_K * T + NE * MOE_RB

    bias_prof = _attn_bias_profiles(rel_bias, S)
    w_in_t = jnp.swapaxes(w_in, 1, 2)
    x2 = x.reshape(T, D)
    for l in range(depth):
        mod = _ada(c, w_ada[l], b_ada[l]).reshape(B, 6, 1, D)
        sh1, sc1, gt1, sh2, sc2, gt2 = (mod[:, i] for i in range(6))

        w_gate = jnp.tile(w_in_t[l][n_main:, :].T, (1, 4))
        w_gate = _hilo_weights(jnp.pad(w_gate, ((0, 0), (0, LANES - w_gate.shape[1]))))
        proj, gat = _inproj(x2, norm1_g[l].reshape(1, D), sc1, sh1, w_in_t[l], n_main, w_gate, S)

        def gate_row(p):
            grp = jnp.concatenate([p.reshape(-1), jnp.zeros((2 * HB,), F32)])
            return jnp.pad(jnp.tile(grp, 4), (0, LANES - 16 * HB)).reshape(1, LANES)

        pack = _gating(gat, gate_row(A_log[l]), gate_row(dt_bias[l]), HB)

        oa = _attention(proj, jnp.tile(qn_g[l], 2).reshape(1, LANES), jnp.tile(kn_g[l], 2).reshape(1, LANES),
                        bias_prof, B, S, HA, hda)
        cw4 = jnp.transpose(conv_w[l].reshape(conv_w.shape[1], 3, HB, hdb), (2, 1, 0, 3))
        ob = _gdn(proj, cw4, pack.reshape(B, S, LANES), onorm_g[l].reshape(1, LANES), B, S, HB, 3 * DA // LANES)

        wr = _hilo_weights(jnp.pad(jnp.concatenate([w_re[l], w_rg[l]], axis=1), ((0, 0), (0, LANES - NE - NG))))
        br = jnp.pad(jnp.concatenate([b_re[l], b_rg[l]]), (0, LANES - NE - NG)).reshape(1, LANES)
        x1, hpk, logits, xs_zero = _outproj(oa, ob, x2, gt1, w_out[l].astype(BF16), norm2_g[l].reshape(1, D),
                                            sc2, sh2, wr, br, S, p_rows)
        route, info = _route(logits, NE, NG)
        dest = route[:, 4:4 + TOP_K].astype(I32).reshape(TOP_K * T)
        we, ws, wn, wt = _moe_schedule(info, NE, p_rows)
        xs = _dispatch(dest, hpk, xs_zero)
        ypk = _moe(we, ws, wn, wt, xs, w1[l], w3[l], w2[l])
        x2 = _combine(dest, x1, route, gt2, ypk, S)
    return x2.reshape(B, S, D)
```

```python
import functools

import numpy as np
import jax
import jax.numpy as jnp
from jax import lax
from jax.experimental import pallas as pl
from jax.experimental.pallas import tpu as pltpu

F32 = jnp.float32
BF16 = jnp.bfloat16
I32 = jnp.int32
U32 = jnp.uint32
HIGHEST = lax.Precision.HIGHEST

EPS = 1e-6
NEG = -1e30
DILATED_BRANCHES = ((128, 1), (512, 4), (2048, 16))
REL_MAX_DIST = 1024
CHUNK = 64
TOP_K = 2

LANES = 128
SUBLANES = 8
MIB = 1 << 20

ATT_QB = 128
ATT_KW = 256
ATT_UNROLL = 8
ATT_PROF_W = 512
GDN_HP = 2
GDN_PREP_UNROLL = 4
GDN_SEG = 512
MOE_RB = 128
MOE_TALL = 512
MOE_RMAX = 512
MOE_WSLOTS = 3
MOE_CW = 512
ROW_TILE = 256
DMA_UNROLL = 8


def _cparams(sem, vmem_mib):
    return pltpu.CompilerParams(dimension_semantics=sem, vmem_limit_bytes=vmem_mib * MIB)


def _dot(a, b, **kw):
    return jnp.dot(a, b, preferred_element_type=F32, **kw)


def _dot_nt(a, b):
    return lax.dot_general(a, b, (((1,), (1,)), ((), ())), preferred_element_type=F32)


def _pick_tile(n, prefs):
    for t in prefs:
        if n % t == 0:
            return t
    return n


def _pack_halves(x):
    half = x.shape[1] // 2
    lo = lax.bitcast_convert_type(x[:, :half].astype(BF16).astype(F32), U32)
    hi = lax.bitcast_convert_type(x[:, half:].astype(BF16).astype(F32), U32)
    return lax.shift_right_logical(lo, jnp.uint32(16)) | (hi & jnp.uint32(0xFFFF0000))


def _silu(x):
    h = 0.5 * x
    return h + h * jnp.tanh(h)


def _hilo_weights(w):
    hi = w.astype(BF16)
    lo = (w - hi.astype(F32)).astype(BF16)
    return jnp.concatenate([hi, lo], axis=1)


def _dot_hilo(x, w2):
    m, n = x.shape[0], w2.shape[1] // 2
    hi = x.astype(BF16)
    lo = (x - hi.astype(F32)).astype(BF16)
    r = _dot(jnp.concatenate([hi, lo], axis=0), w2)
    return r[:m, :n] + (r[:m, n:] + r[m:, :n])


def _unpack_halves(p):
    lo = lax.bitcast_convert_type(lax.shift_left(p, jnp.uint32(16)), F32)
    hi = lax.bitcast_convert_type(p & jnp.uint32(0xFFFF0000), F32)
    return lo, hi


def _ada_kernel(c_ref, w_ref, b_ref, o_ref):
    c = c_ref[...]
    s = _silu(c).astype(BF16)
    o_ref[...] = _dot(s, w_ref[...].astype(BF16)) + b_ref[...]


def _ada(c, w_ada, b_ada):
    B, D = c.shape
    N = w_ada.shape[1]
    tn = _pick_tile(N, (1024, 512, 256, 128))
    return pl.pallas_call(
        _ada_kernel,
        grid=(N // tn,),
        in_specs=[pl.BlockSpec((B, D), lambda j: (0, 0)),
                  pl.BlockSpec((D, tn), lambda j: (0, j)),
                  pl.BlockSpec((1, tn), lambda j: (0, j))],
        out_specs=pl.BlockSpec((B, tn), lambda j: (0, j)),
        out_shape=jax.ShapeDtypeStruct((B, N), F32),
        compiler_params=_cparams(("arbitrary",), 40),
    )(c, w_ada, b_ada.reshape(1, N))


def _inproj_kernel(x_ref, g_ref, sc_ref, sh_ref, w_ref, wg_ref, o_ref, og_ref, h_scr):
    @pl.when(pl.program_id(1) == 0)
    def _():
        x = x_ref[...]
        y = x * lax.rsqrt(jnp.mean(x * x, axis=-1, keepdims=True) + EPS) * g_ref[...]
        h = y * (1.0 + sc_ref[...]) + sh_ref[...]
        h_scr[...] = h.astype(BF16)
        og_ref[...] = _dot_hilo(h, wg_ref[...])

    o_ref[...] = _dot_nt(h_scr[...], w_ref[...].astype(BF16)).astype(o_ref.dtype)


def _inproj(x2, g, sc, sh, w_all, n_main, w_gate, S):
    T, D = x2.shape
    NM = n_main
    tm = _pick_tile(S, (1024, 512, 256, 128))
    tn = _pick_tile(NM, (1024, 512, 256, 128))
    per_b = S // tm
    return pl.pallas_call(
        _inproj_kernel,
        grid=(T // tm, NM // tn),
        in_specs=[pl.BlockSpec((tm, D), lambda i, j: (i, 0)),
                  pl.BlockSpec((1, D), lambda i, j: (0, 0)),
                  pl.BlockSpec((None, 1, D), lambda i, j: (i // per_b, 0, 0)),
                  pl.BlockSpec((None, 1, D), lambda i, j: (i // per_b, 0, 0)),
                  pl.BlockSpec((tn, D), lambda i, j: (j, 0)),
                  pl.BlockSpec((D, 2 * LANES), lambda i, j: (0, 0))],
        out_specs=[pl.BlockSpec((tm, tn), lambda i, j: (i, j)),
                   pl.BlockSpec((tm, LANES), lambda i, j: (i, 0))],
        out_shape=[jax.ShapeDtypeStruct((T, NM), BF16), jax.ShapeDtypeStruct((T, LANES), F32)],
        scratch_shapes=[pltpu.VMEM((tm, D), BF16)],
        compiler_params=_cparams(("arbitrary", "arbitrary"), 56),
    )(x2, g, sc, sh, w_all, w_gate)


def _gating_kernel(gat_ref, a_ref, dt_ref, o_ref, *, hb):
    gat = gat_ref[...]
    tm = gat.shape[0]
    gw = 4 * hb
    g = -jnp.exp(a_ref[...]) * jax.nn.softplus(gat + dt_ref[...])
    beta = jax.nn.sigmoid(gat)
    r = lax.broadcasted_iota(I32, (tm, tm), 0)
    c = lax.broadcasted_iota(I32, (tm, tm), 1)
    same = (r // CHUNK) == (c // CHUNK)
    pre = _dot(jnp.where(same & (c <= r), 1.0, 0.0), g, precision=HIGHEST)
    suf = _dot(jnp.where(same & (c >= r), 1.0, 0.0), g, precision=HIGHEST)
    lane = lax.broadcasted_iota(I32, gat.shape, 1)
    o_ref[...] = jnp.where(lane // gw == 0, jnp.where(lane % gw < hb, pre, suf), beta)


def _gating(gat, a_row, dt_row, hb):
    T = gat.shape[0]
    tm = ROW_TILE
    return pl.pallas_call(
        functools.partial(_gating_kernel, hb=hb),
        grid=(T // tm,),
        in_specs=[pl.BlockSpec((tm, LANES), lambda i: (i, 0)),
                  pl.BlockSpec((1, LANES), lambda i: (0, 0)),
                  pl.BlockSpec((1, LANES), lambda i: (0, 0))],
        out_specs=pl.BlockSpec((tm, LANES), lambda i: (i, 0)),
        out_shape=jax.ShapeDtypeStruct((T, LANES), F32),
        compiler_params=_cparams(("arbitrary",), 16),
    )(gat, a_row, dt_row)


def _t5_bucket(rel, n_buckets):
    half = n_buckets // 2
    max_exact = half // 2
    n = np.abs(rel)
    large = max_exact + (np.log(np.maximum(n, 1) / max_exact) / np.log(REL_MAX_DIST / max_exact)
                         * (half - max_exact)).astype(np.int32)
    large = np.minimum(large, half - 1)
    return (np.where(rel > 0, half, 0) + np.where(n < max_exact, n, large)).astype(np.int32)


def _attn_plan(S):
    plan, base = [], 0
    for window, dil in DILATED_BRANCHES:
        n = window // (2 * dil)
        L = S // dil
        assert L % ATT_QB == 0 and n * 2 == ATT_QB
        kw = min(ATT_KW, L)
        nbq = L // ATT_QB
        nvar = 1 if nbq == 1 else 3
        plan.append((dil, L, nbq, kw, base, nvar, n))
        base += nvar
    return tuple(plan), base


def _attn_bias_profiles(rel_bias, S):
    plan, nvar_total = _attn_plan(S)
    nbuckets, H = rel_bias.shape
    u = np.arange(ATT_PROF_W) - ATT_QB
    onehots, bands = [], []
    for dil, L, nbq, kw, base, nvar, n in plan:
        offs = [0] if nvar == 1 else [0, -n, -(kw - ATT_QB)]
        for off in offs:
            rel = off + u
            onehots.append(np.eye(nbuckets, dtype=np.float32)[_t5_bucket(rel * dil, nbuckets)])
            bands.append(np.abs(rel) <= n)
    onehot = jnp.asarray(np.stack(onehots))
    band = jnp.asarray(np.stack(bands))
    prof = jnp.einsum('vwn,nh->hvw', onehot, rel_bias.astype(F32), precision=HIGHEST)
    prof = jnp.where(band[None], prof, NEG)
    prof = prof.reshape(H // 2, 2, nvar_total, ATT_PROF_W)
    return jnp.transpose(prof, (0, 2, 1, 3)).reshape(H // 2, 2 * nvar_total, ATT_PROF_W)


def _attn_kernel(q_ref, k_ref, v_ref, qg_ref, kg_ref, prof_ref, o_ref,
                 qn_scr, kn_scr, v_scr, ob_scr, mb_scr, db_scr, bias_ref, *, plan, hd):
    S = q_ref.shape[0]
    lane = lax.broadcasted_iota(I32, (1, LANES), 1)
    left = lane < hd

    @pl.when(pl.program_id(1) == 0)
    def _():
        for row in range(prof_ref.shape[0]):
            rep = jnp.broadcast_to(prof_ref[row:row + 1, :], (ATT_QB, ATT_PROF_W))
            skew = pltpu.roll(rep, 0, 1, stride=1, stride_axis=0)
            bias_ref[row // 2, row % 2] = skew[:, ATT_QB:ATT_QB + ATT_KW]

    same_head = (lax.broadcasted_iota(I32, (2 * LANES, LANES), 0) % LANES) // hd == \
        lax.broadcasted_iota(I32, (2 * LANES, LANES), 1) // hd
    avg = jnp.where(same_head, 1.0 / hd, 0.0).astype(BF16)

    def headnorm(x, g):
        x2 = x * x
        hi = x2.astype(BF16)
        lo = (x2 - hi.astype(F32)).astype(BF16)
        ms = _dot(jnp.concatenate([hi, lo], axis=1), avg)
        return x * lax.rsqrt(ms + EPS) * g

    qn_scr[...] = headnorm(q_ref[...].astype(F32), qg_ref[...]) * (hd ** -0.5)
    kn_scr[...] = headnorm(k_ref[...].astype(F32), kg_ref[...])
    v_scr[...] = v_ref[...].astype(F32)

    for bi, (dil, L, nbq, kw, base, nvar, n) in enumerate(plan):
        ones = jnp.ones((kw, LANES), BF16)

        def body(t, carry, dil=dil, L=L, nbq=nbq, kw=kw, base=base, nvar=nvar, n=n, bi=bi, ones=ones):
            blocks = []
            for uu in range(ATT_UNROLL):
                idx = t * ATT_UNROLL + uu
                r = idx // nbq
                i = idx % nbq
                q0 = i * ATT_QB
                k0 = jnp.clip(q0 - n, 0, L - kw)
                var = base if nvar == 1 else base + jnp.where(i > 0, 1, 0) + jnp.where(i == nbq - 1, 1, 0)
                if dil == 1:
                    qrows = pl.ds(pl.multiple_of(q0, ATT_QB), ATT_QB)
                    krows = pl.ds(pl.multiple_of(k0, CHUNK), kw)
                else:
                    qrows = pl.ds(r + q0 * dil, ATT_QB, stride=dil)
                    krows = pl.ds(r + k0 * dil, kw, stride=dil)
                qb = qn_scr[qrows, :]
                q2 = jnp.concatenate([jnp.where(left, qb, 0.0), jnp.where(left, 0.0, qb)], axis=0).astype(BF16)
                blocks.append((qrows, krows, var, q2))
            def score(group):
                return [_dot_nt(q2, kn_scr[krows, :].astype(BF16)) for qrows, krows, var, q2 in group]

            def softmax(group, scores):
                probs, maxes = [], []
                for (qrows, krows, var, q2), s in zip(group, scores):
                    s = s + jnp.concatenate([bias_ref[var, 0][:, :kw], bias_ref[var, 1][:, :kw]], axis=0)
                    m = jnp.max(s, axis=-1, keepdims=True)
                    probs.append(jnp.exp(s - m).astype(BF16))
                    maxes.append(m)
                return probs, maxes

            def values(group, probs):
                return [_dot(p, jnp.concatenate([v_scr[krows, :].astype(BF16), ones], axis=1))
                        for (qrows, krows, var, q2), p in zip(group, probs)]

            def finish(group, maxes, outs):
                for (qrows, krows, var, q2), m, od in zip(group, maxes, outs):
                    mb = jnp.broadcast_to(m, (2 * ATT_QB, LANES))
                    ob_scr[bi, qrows, :] = jnp.where(left, od[:ATT_QB, :LANES], od[ATT_QB:, :LANES])
                    mb_scr[bi, qrows, :] = jnp.where(left, mb[:ATT_QB], mb[ATT_QB:])
                    db_scr[bi, qrows, :] = jnp.where(left, od[:ATT_QB, LANES:], od[ATT_QB:, LANES:])

            ga, gb = blocks[:ATT_UNROLL // 2], blocks[ATT_UNROLL // 2:]
            sa = score(ga)
            sb = score(gb)
            pa, ma = softmax(ga, sa)
            oa = values(ga, pa)
            pb, mbx = softmax(gb, sb)
            ob = values(gb, pb)
            finish(ga, ma, oa)
            finish(gb, mbx, ob)
            return carry

        assert (dil * nbq) % ATT_UNROLL == 0
        lax.fori_loop(0, dil * nbq // ATT_UNROLL, body, 0)

    nb = len(plan)
    mx = mb_scr[0]
    for bi in range(1, nb):
        mx = jnp.maximum(mx, mb_scr[bi])
    num = jnp.zeros((S, LANES), F32)
    den = jnp.zeros((S, LANES), F32)
    for bi in range(nb):
        w = jnp.exp(mb_scr[bi] - mx)
        num = num + w * ob_scr[bi]
        den = den + w * db_scr[bi]
    o_ref[...] = (num / den).astype(o_ref.dtype)


def _attention(proj, qg2, kg2, profiles, B, S, HA, hd):
    T = proj.shape[0]
    pairs = HA // 2
    da_blocks = HA * hd // LANES
    plan, nvar = _attn_plan(S)
    return pl.pallas_call(
        functools.partial(_attn_kernel, plan=plan, hd=hd),
        grid=(pairs, B),
        in_specs=[pl.BlockSpec((S, LANES), lambda p, b: (b, p)),
                  pl.BlockSpec((S, LANES), lambda p, b: (b, da_blocks + p)),
                  pl.BlockSpec((S, LANES), lambda p, b: (b, 2 * da_blocks + p)),
                  pl.BlockSpec((1, LANES), lambda p, b: (0, 0)),
                  pl.BlockSpec((1, LANES), lambda p, b: (0, 0)),
                  pl.BlockSpec((None, 2 * nvar, ATT_PROF_W), lambda p, b: (p, 0, 0))],
        out_specs=pl.BlockSpec((S, LANES), lambda p, b: (b, p)),
        out_shape=jax.ShapeDtypeStruct((T, HA * hd), BF16),
        scratch_shapes=[pltpu.VMEM((S, LANES), F32)] * 3 + [pltpu.VMEM((len(plan), S, LANES), F32)] * 3
        + [pltpu.VMEM((nvar, 2, ATT_QB, ATT_KW), F32)],
        compiler_params=_cparams(("arbitrary", "arbitrary"), 40),
    )(proj, proj, proj, qg2, kg2, profiles)


def _gdn_kernel(q_ref, k_ref, v_ref, z_ref, cw_ref, pack_ref, og_ref, o_ref,
                q_scr, k_scr, v_scr, xpad_scr, pk2_scr, sel_scr, u_scr, wq_scr, at_scr, kdt_scr, et_scr, oacc_scr,
                *, hb, hp):
    S = q_ref.shape[0]
    P2 = 2 * CHUNK
    W2 = 2 * LANES
    npair = S // P2
    hg = pl.program_id(1)
    gw = 4 * hb
    dk = LANES

    pad = SUBLANES
    for slot in range(2):
        xpad_scr[slot, 0:pad, :] = jnp.zeros((pad, LANES), F32)
        xpad_scr[slot, pad + S:, :] = jnp.zeros((pad, LANES), F32)

    seg = GDN_SEG if S % GDN_SEG == 0 else S

    def conv_silu_to(src_ref, lanes_j, j, which, dst_scr, l2norm, scale):
        xp = xpad_scr.at[(3 * j + which) % 2]
        for s0 in range(0, S, seg):
            xp[pad + s0:pad + s0 + seg, :] = src_ref[s0:s0 + seg, lanes_j].astype(F32)
        for s0 in range(0, S, seg):
            acc = xp[pad + s0:pad + s0 + seg, :] * cw_ref[j, which, 2:3, :]
            for d in (-2, -1, 1, 2):
                acc = acc + xp[pad + s0 + d:pad + s0 + d + seg, :] * cw_ref[j, which, 2 + d:3 + d, :]
            y = _silu(acc)
            if l2norm:
                y = y * (lax.rsqrt(jnp.sum(y * y, axis=-1, keepdims=True) + EPS) * scale)
            dst_scr[j, s0:s0 + seg, :] = y

    for s0 in range(0, S, seg):
        pk = pack_ref[s0:s0 + seg, :]
        p_hi = pk.astype(BF16)
        p_lo = (pk - p_hi.astype(F32)).astype(BF16)
        pk2_scr[s0:s0 + seg, :] = jnp.concatenate([p_hi, p_lo], axis=1)
    srow = lax.broadcasted_iota(I32, (W2, 4 * LANES), 0) % LANES
    scol = lax.broadcasted_iota(I32, (W2, 4 * LANES), 1) // LANES
    for j in range(hp):
        lanes_j = slice(j * LANES, (j + 1) * LANES)
        conv_silu_to(q_ref, lanes_j, j, 0, q_scr, True, dk ** -0.5)
        conv_silu_to(k_ref, lanes_j, j, 1, k_scr, True, 1.0)
        conv_silu_to(v_ref, lanes_j, j, 2, v_scr, False, 1.0)
        src = hg * hp + j + jnp.where(scol < 2, scol * hb, gw + 2 * hb + (scol - 2) * hb)
        sel_scr[j] = jnp.where(srow == src, 1.0, 0.0).astype(BF16)
        oacc_scr[j] = jnp.zeros((S, LANES), F32)

    r4 = lax.broadcasted_iota(I32, (CHUNK, W2), 0)
    l4 = lax.broadcasted_iota(I32, (CHUNK, W2), 1)
    c4 = l4 % CHUNK
    blk4 = l4 // CHUNK
    lo_half = (l4 % LANES) < CHUNK
    ahead = jnp.where(l4 >= LANES, r4 - c4, c4 - r4)
    incl = ahead <= 0
    strict = ahead < 0
    bd16 = (r4 // 16) == (c4 // 16)

    def squeeze(x):
        return jnp.where(lo_half, x[:CHUNK], x[CHUNK:])

    def unsqueeze(x):
        return jnp.concatenate([jnp.where(lo_half, x, jnp.zeros_like(x)),
                                jnp.where(lo_half, jnp.zeros_like(x), x)], axis=0)

    def mm4(a, b):
        rhs = jnp.concatenate([jnp.where(blk4 == g, b, 0.0) for g in range(4)], axis=0)
        return _dot(a.astype(BF16), rhs.astype(BF16))

    first = lax.broadcasted_iota(I32, (P2, 1), 0) < CHUNK
    zpair = jnp.zeros((P2, LANES), BF16)

    def bdiag(x):
        return jnp.concatenate([jnp.concatenate([x[:, :LANES], zpair], axis=1),
                                jnp.concatenate([zpair, x[:, LANES:]], axis=1)], axis=0)

    U = GDN_PREP_UNROLL if npair % GDN_PREP_UNROLL == 0 else 1

    def prep(t, carry):
        cx = []
        chains = [(t * U + u, j) for u in range(U) for j in range(hp)]
        bcs = [_dot(pk2_scr[pl.ds(pl.multiple_of(m * P2, P2), P2), :], sel_scr[j]) for m, j in chains]
        for (m, j), bc in zip(chains, bcs):
            rows = pl.ds(pl.multiple_of(m * P2, P2), P2)
            kp = k_scr[j, rows, :]
            qp = q_scr[j, rows, :]
            vp = v_scr[j, rows, :]
            gc2 = bc[:, 0:W2]
            beta2 = bc[:, W2:2 * W2]
            gcf, gcb = gc2[:, :LANES], gc2[:, LANES:]
            tot2 = jnp.concatenate([jnp.where(first, gcf[CHUNK - 1:CHUNK], gcf[P2 - 1:P2]),
                                    jnp.where(first, gcb[0:1], gcb[CHUNK:CHUNK + 1])], axis=1)
            egc2 = jnp.exp(gc2)
            k2 = jnp.concatenate([kp, kp], axis=1)
            kb2 = k2 * beta2
            vb2 = jnp.concatenate([vp, vp], axis=1) * beta2
            kbe2 = kb2 * egc2
            cx.append(dict(
                m=m, j=j, rows=rows, vb2=vb2, kbe2=kbe2,
                qeb=(jnp.concatenate([qp, qp], axis=1) * egc2).astype(BF16),
                kd2=k2 * jnp.exp(tot2 - gc2),
                et=jnp.exp(tot2),
                dec=jnp.exp(jnp.where(
                    incl, squeeze(gc2) - squeeze(jnp.concatenate([gcf.T, gcb.T], axis=1)), -jnp.inf)),
                stk=jnp.concatenate([kb2[:, :LANES], kb2[:, LANES:], qp], axis=0).astype(BF16),
                kpb=kp.astype(BF16)))

        g3s = [_dot_nt(c['stk'], c['kpb']) for c in cx]
        for c, g3 in zip(cx, g3s):
            lm = jnp.where(strict, squeeze(jnp.concatenate([g3[:P2], g3[P2:2 * P2]], axis=1)) * c['dec'], 0.0)
            attn = squeeze(jnp.concatenate([g3[2 * P2:], g3[2 * P2:]], axis=1)) * c['dec']
            c['attn2'] = unsqueeze(attn).astype(BF16)
            c['lbd'] = jnp.where(bd16, lm, 0.0)
            c['loff'] = lm - c['lbd']
        nn = [-c['lbd'] for c in cx]
        pw = [mm4(c['lbd'], c['lbd']) for c in cx]
        for rnd in range(3):
            prod = [mm4(a, p) for a, p in zip(nn, pw)]
            nxt = [mm4(p, p) for p in pw] if rnd < 2 else pw
            nn = [a + p + q for a, p, q in zip(nn, pw, prod)]
            pw = nxt
        mo = [c['loff'] + x for c, x in zip(cx, [mm4(a, c['loff']) for a, c in zip(nn, cx)])]
        m2 = [mm4(x, x) for x in mo]
        mn = [mm4(x, a) for x, a in zip(mo, nn)]
        xo = [a - x - y for a, x, y in zip(nn, mo, mn)]
        mx = [mm4(a, b) for a, b in zip(m2, xo)]
        toff = [a + b + q for a, b, q in zip(xo, m2, mx)]
        z4 = jnp.zeros((CHUNK, W2), BF16)
        tws = []
        for c, tf in zip(cx, toff):
            vbb, kbb = c['vb2'].astype(BF16), c['kbe2'].astype(BF16)
            blocks = []
            for g in range(4):
                rr = slice((g % 2) * CHUNK, (g % 2 + 1) * CHUNK)
                ll = slice((g // 2) * LANES, (g // 2 + 1) * LANES)
                blocks.append(jnp.concatenate([z4] * g + [vbb[rr, ll], kbb[rr, ll]] + [z4] * (3 - g), axis=1))
            tws.append(_dot(tf.astype(BF16), jnp.concatenate(blocks, axis=0)))
        for c, tw in zip(cx, tws):
            m, j, rows = c['m'], c['j'], c['rows']
            def pair_layout(off):
                piece = lambda g: tw[:, 2 * g * LANES + off:2 * g * LANES + off + LANES]
                return jnp.concatenate([jnp.concatenate([piece(0), piece(2)], axis=1),
                                        jnp.concatenate([piece(1), piece(3)], axis=1)], axis=0)

            u2 = c['vb2'] + pair_layout(0)
            w2 = (c['kbe2'] + pair_layout(LANES)).astype(BF16)
            qeb = c['qeb']
            u_scr[j, rows, :] = u2
            wq_scr[j, pl.ds(pl.multiple_of(m * 2 * P2, 2 * P2), 2 * P2), :] = jnp.concatenate(
                [w2[:CHUNK], qeb[:CHUNK], w2[CHUNK:], qeb[CHUNK:]], axis=0)
            at_scr[j, rows, :] = c['attn2']
            kdt_scr[j, 0, :, rows] = c['kd2'][:, :LANES].T.astype(BF16)
            kdt_scr[j, 1, :, rows] = c['kd2'][:, LANES:].T.astype(BF16)
            et_scr[j, pl.ds(pl.multiple_of(m * 2 * SUBLANES, 2 * SUBLANES), 2 * SUBLANES), :] = jnp.concatenate(
                [c['et'][:SUBLANES], c['et'][CHUNK:CHUNK + SUBLANES]], axis=0)
        return carry

    lax.fori_loop(0, npair // U, prep, 0)

    zc = jnp.zeros((CHUNK, LANES), F32)
    zp = jnp.zeros((P2, LANES), F32)

    def place(v, cpos):
        return jnp.concatenate([v, zc] if cpos == 0 else [zc, v], axis=0)

    def scan(m, states):
        pf = m
        pb = npair - 1 - m
        rows_f = pl.ds(pl.multiple_of(pf * P2, P2), P2)
        rows_b = pl.ds(pl.multiple_of(pb * P2, P2), P2)
        hx = []
        for j in range(hp):
            hx.append(dict(
                u_f=u_scr[j, rows_f, :LANES], u_b=u_scr[j, rows_b, LANES:],
                at_f=at_scr[j, rows_f, :LANES], at_b=at_scr[j, rows_b, LANES:],
                kdt=jnp.concatenate([kdt_scr[j, 0, :, rows_f], kdt_scr[j, 1, :, rows_b]], axis=1),
                wq_f=wq_scr[j, pl.ds(pl.multiple_of(pf * 2 * P2, 2 * P2), 2 * P2), :LANES],
                wq_b=wq_scr[j, pl.ds(pl.multiple_of(pb * 2 * P2, 2 * P2), 2 * P2), LANES:],
                et_f=et_scr[j, pl.ds(pl.multiple_of(pf * 2 * SUBLANES, 2 * SUBLANES), 2 * SUBLANES), :LANES],
                et_b=et_scr[j, pl.ds(pl.multiple_of(pb * 2 * SUBLANES, 2 * SUBLANES), 2 * SUBLANES), LANES:]))
        sts = list(states)
        for step in range(2):
            cf, cb = step, 1 - step
            rrs = [_dot(jnp.concatenate([c['wq_f'][cf * P2:(cf + 1) * P2], c['wq_b'][cb * P2:(cb + 1) * P2]], axis=1),
                        bdiag(st.astype(BF16))) for c, st in zip(hx, sts)]
            ress = []
            for c, rr in zip(hx, rrs):
                u2 = jnp.concatenate([c['u_f'][cf * CHUNK:(cf + 1) * CHUNK],
                                      c['u_b'][cb * CHUNK:(cb + 1) * CHUNK]], axis=1)
                v_new = u2 - rr[:CHUNK]
                rhs = jnp.concatenate(
                    [jnp.concatenate([place(v_new[:, :LANES], cf), zp], axis=1),
                     jnp.concatenate([zp, place(v_new[:, LANES:], cb)], axis=1)], axis=0).astype(BF16)
                lhs = jnp.concatenate(
                    [jnp.concatenate([c['at_f'][cf * CHUNK:(cf + 1) * CHUNK],
                                      c['at_b'][cb * CHUNK:(cb + 1) * CHUNK]], axis=1),
                     c['kdt']], axis=0)
                ress.append(_dot(lhs, rhs))
            for j, (c, rr, res) in enumerate(zip(hx, rrs, ress)):
                o2 = rr[CHUNK:] + res[:CHUNK]
                et2 = jnp.concatenate([c['et_f'][cf * SUBLANES:cf * SUBLANES + 1],
                                       c['et_b'][cb * SUBLANES:cb * SUBLANES + 1]], axis=1)
                sts[j] = sts[j] * et2 + res[CHUNK:]
                of_rows = pl.ds(pl.multiple_of(pf * P2 + cf * CHUNK, CHUNK), CHUNK)
                ob_rows = pl.ds(pl.multiple_of(pb * P2 + cb * CHUNK, CHUNK), CHUNK)
                oacc_scr[j, of_rows, :] = oacc_scr[j, of_rows, :] + o2[:, :LANES]
                oacc_scr[j, ob_rows, :] = oacc_scr[j, ob_rows, :] + o2[:, LANES:]
        return tuple(sts)

    s0 = jnp.zeros((dk, W2), F32)
    lax.fori_loop(0, npair, scan, (s0,) * hp)

    for j in range(hp):
        lanes_j = slice(j * LANES, (j + 1) * LANES)
        for s0 in range(0, S, seg):
            o = oacc_scr[j, s0:s0 + seg, :]
            y = o * lax.rsqrt(jnp.mean(o * o, axis=-1, keepdims=True) + EPS) * og_ref[...]
            z = z_ref[s0:s0 + seg, lanes_j].astype(F32)
            o_ref[s0:s0 + seg, lanes_j] = (y * _silu(z)).astype(o_ref.dtype)


def _gdn(proj, cw4, pack3, onorm_g, B, S, HB, base_blk):
    T = proj.shape[0]
    hp = GDN_HP if (HB % GDN_HP == 0 and base_blk % GDN_HP == 0) else 1
    wblk = hp * LANES
    npair = S // (2 * CHUNK)

    def col(k):
        off = (base_blk + k * HB) // hp
        return lambda b, h: (b, off + h)

    return pl.pallas_call(
        functools.partial(_gdn_kernel, hb=HB, hp=hp),
        grid=(B, HB // hp),
        in_specs=[pl.BlockSpec((S, wblk), col(0)),
                  pl.BlockSpec((S, wblk), col(1)),
                  pl.BlockSpec((S, wblk), col(2)),
                  pl.BlockSpec((S, wblk), col(3)),
                  pl.BlockSpec((hp, 3, cw4.shape[2], LANES), lambda b, h: (h, 0, 0, 0)),
                  pl.BlockSpec((None, S, LANES), lambda b, h: (b, 0, 0)),
                  pl.BlockSpec((1, LANES), lambda b, h: (0, 0))],
        out_specs=pl.BlockSpec((S, wblk), lambda b, h: (b, h)),
        out_shape=jax.ShapeDtypeStruct((T, HB * LANES), BF16),
        scratch_shapes=[pltpu.VMEM((hp, S, LANES), F32)] * 3
        + [pltpu.VMEM((2, S + 2 * SUBLANES, LANES), F32),
           pltpu.VMEM((S, 2 * LANES), BF16),
           pltpu.VMEM((hp, 2 * LANES, 4 * LANES), BF16),
           pltpu.VMEM((hp, S, 2 * LANES), F32),
           pltpu.VMEM((hp, 2 * S, 2 * LANES), BF16),
           pltpu.VMEM((hp, S, 2 * LANES), BF16),
           pltpu.VMEM((hp, 2, LANES, S), BF16),
           pltpu.VMEM((hp, npair * 2 * SUBLANES, 2 * LANES), F32),
           pltpu.VMEM((hp, S, LANES), F32)],
        compiler_params=_cparams(("arbitrary", "arbitrary"), 56),
    )(proj, proj, proj, proj, cw4, pack3, onorm_g)


def _outproj_kernel(oa_ref, ob_ref, x_ref, gt_ref, w_ref, g_ref, sc_ref, sh_ref, wr_ref, br_ref,
                    x1_ref, hp_ref, lg_ref, z_ref, h_scr):
    z_ref[...] = jnp.zeros(z_ref.shape, U32)
    da = oa_ref.shape[1]
    y = _dot(oa_ref[...], w_ref[:da, :]) + _dot(ob_ref[...], w_ref[da:, :])
    x1 = x_ref[...] + gt_ref[...] * y
    x1_ref[...] = x1
    hn = x1 * lax.rsqrt(jnp.mean(x1 * x1, axis=-1, keepdims=True) + EPS) * g_ref[...]
    h = hn * (1.0 + sc_ref[...]) + sh_ref[...]
    hp_ref[...] = _pack_halves(h)
    h_scr[...] = h

    @pl.when(pl.program_id(0) < pl.num_programs(0))
    def _():
        lg_ref[...] = _dot_hilo(h_scr[...], wr_ref[...]) + br_ref[...]


def _outproj(oa, ob, x2, gt1, w_out_b, g2, sc2, sh2, wr, br, S, p_rows):
    T, D = x2.shape
    tm = ROW_TILE
    per_b = S // tm
    bmap = lambda i: (i // per_b, 0, 0)
    zrows = p_rows // (T // tm)
    assert zrows * (T // tm) == p_rows and zrows % SUBLANES == 0
    return pl.pallas_call(
        _outproj_kernel,
        grid=(T // tm,),
        in_specs=[pl.BlockSpec((tm, oa.shape[1]), lambda i: (i, 0)),
                  pl.BlockSpec((tm, ob.shape[1]), lambda i: (i, 0)),
                  pl.BlockSpec((tm, D), lambda i: (i, 0)),
                  pl.BlockSpec((None, 1, D), bmap),
                  pl.BlockSpec((D, D), lambda i: (0, 0)),
                  pl.BlockSpec((1, D), lambda i: (0, 0)),
                  pl.BlockSpec((None, 1, D), bmap),
                  pl.BlockSpec((None, 1, D), bmap),
                  pl.BlockSpec((D, 2 * LANES), lambda i: (0, 0)),
                  pl.BlockSpec((1, LANES), lambda i: (0, 0))],
        out_specs=[pl.BlockSpec((tm, D), lambda i: (i, 0)),
                   pl.BlockSpec((tm, D // 2), lambda i: (i, 0)),
                   pl.BlockSpec((tm, LANES), lambda i: (i, 0)),
                   pl.BlockSpec((zrows, D // 2), lambda i: (i, 0))],
        out_shape=[jax.ShapeDtypeStruct((T, D), F32), jax.ShapeDtypeStruct((T, D // 2), U32),
                   jax.ShapeDtypeStruct((T, LANES), F32), jax.ShapeDtypeStruct((p_rows, D // 2), U32)],
        scratch_shapes=[pltpu.VMEM((tm, D), F32)],
        compiler_params=_cparams(("arbitrary",), 48),
    )(oa, ob, x2, gt1, w_out_b, g2, sc2, sh2, wr, br)


def _route_kernel(lg_ref, o_ref, info_ref, run_scr, *, ne, ng):
    ph = pl.program_id(0)
    i = pl.program_id(1)

    @pl.when((ph == 0) & (i == 0))
    def _():
        run_scr[...] = jnp.zeros_like(run_scr)

    @pl.when((ph == 1) & (i == 0))
    def _():
        cnt = run_scr[...]
        padded = jnp.ceil(cnt * (1.0 / MOE_RB)) * MOE_RB
        k = lax.broadcasted_iota(I32, (LANES, LANES), 0)
        e = lax.broadcasted_iota(I32, (LANES, LANES), 1)
        start = _dot(padded, jnp.where(k < e, 1.0, 0.0), precision=HIGHEST)
        rowi = lax.broadcasted_iota(I32, cnt.shape, 0)
        info_ref[...] = jnp.where(rowi == 0, cnt, jnp.where(rowi == 1, padded, start))
        run_scr[...] = start

    lg = lg_ref[...]
    tm = lg.shape[0]
    epg = ne // ng
    lane_i = lax.broadcasted_iota(I32, lg.shape, 1)
    lane = lane_i.astype(F32)
    big = float(2 * LANES)
    is_g = (lane_i >= ne) & (lane_i < ne + ng)
    gl = jnp.where(is_g, lg, -jnp.inf)
    gmax = jnp.max(gl, axis=-1, keepdims=True)
    gidx = jnp.min(jnp.where(gl == gmax, lane, big), axis=-1, keepdims=True) - ne
    psel = 1.0 / jnp.sum(jnp.where(is_g, jnp.exp(gl - gmax), 0.0), axis=-1, keepdims=True)
    in_grp = (lane_i // epg).astype(F32) == gidx
    el = jnp.where(in_grp & (lane_i < ne), lg, -jnp.inf)
    m1 = jnp.max(el, axis=-1, keepdims=True)
    i1 = jnp.min(jnp.where(el == m1, lane, big), axis=-1, keepdims=True)
    el2 = jnp.where(lane == i1, -jnp.inf, el)
    m2 = jnp.max(el2, axis=-1, keepdims=True)
    i2 = jnp.min(jnp.where(el2 == m2, lane, big), axis=-1, keepdims=True)
    e21 = jnp.exp(m2 - m1)
    g1 = psel / (1.0 + e21)
    g2 = psel * e21 / (1.0 + e21)
    o1 = jnp.where(lane == i1, 1.0, 0.0)
    o2 = jnp.where(lane == i2, 1.0, 0.0)
    cnt = o1 + o2
    r = lax.broadcasted_iota(I32, (tm, tm), 0)
    c = lax.broadcasted_iota(I32, (tm, tm), 1)
    before = _dot(jnp.where(c < r, 1.0, 0.0).astype(BF16), cnt.astype(BF16)) + run_scr[0:1, :]
    d1 = jnp.sum(o1 * before, axis=-1, keepdims=True)
    d2 = jnp.sum(o2 * before, axis=-1, keepdims=True)
    run_scr[...] = run_scr[...] + jnp.sum(cnt, axis=0, keepdims=True)

    @pl.when(ph == 1)
    def _():
        out = jnp.zeros(lg.shape, F32)
        for j, val in enumerate((i1, i2, g1, g2, d1, d2)):
            out = jnp.where(lane_i == j, val, out)
        o_ref[...] = out


def _route(logits, ne, ng):
    T = logits.shape[0]
    tm = _pick_tile(T, (1024, 512, 256))
    return pl.pallas_call(
        functools.partial(_route_kernel, ne=ne, ng=ng),
        grid=(2, T // tm),
        in_specs=[pl.BlockSpec((tm, LANES), lambda p, i: (i, 0))],
        out_specs=[pl.BlockSpec((tm, LANES), lambda p, i: (i * p, 0)),
                   pl.BlockSpec((SUBLANES, LANES), lambda p, i: (0, 0))],
        out_shape=[jax.ShapeDtypeStruct((T, LANES), F32), jax.ShapeDtypeStruct((SUBLANES, LANES), F32)],
        scratch_shapes=[pltpu.VMEM((SUBLANES, LANES), F32)],
        compiler_params=_cparams(("arbitrary", "arbitrary"), 16),
    )(logits)


def _dispatch_kernel(dest_ref, h_hbm, xs_in, xs_hbm, sem):
    del xs_in
    tm = ROW_TILE
    base = pl.program_id(0) * tm

    def copy(tok, a):
        return pltpu.make_async_copy(h_hbm.at[pl.ds(tok, 1)], xs_hbm.at[pl.ds(dest_ref[a], 1)], sem)

    def issue(t, carry):
        for k in range(TOP_K):
            copy(base + t, (base + t) * TOP_K + k).start(priority=k % 2)
        return carry

    lax.fori_loop(0, tm, issue, 0, unroll=DMA_UNROLL)

    def retire_step():
        for k in range(TOP_K):
            pltpu.make_async_copy(h_hbm.at[pl.ds(0, tm)], xs_hbm.at[pl.ds(0, tm)], sem).wait()

    @pl.when(pl.program_id(0) > 0)
    def _():
        retire_step()

    @pl.when(pl.program_id(0) == pl.num_programs(0) - 1)
    def _():
        retire_step()


def _dispatch(dest, hpk, xs_zero):
    T, dh = hpk.shape
    p_rows = xs_zero.shape[0]
    return pl.pallas_call(
        _dispatch_kernel,
        grid_spec=pltpu.PrefetchScalarGridSpec(
            num_scalar_prefetch=1,
            grid=(T // ROW_TILE,),
            in_specs=[pl.BlockSpec(memory_space=pl.ANY), pl.BlockSpec(memory_space=pl.ANY)],
            out_specs=pl.BlockSpec(memory_space=pl.ANY),
            scratch_shapes=[pltpu.SemaphoreType.DMA(())]),
        out_shape=jax.ShapeDtypeStruct((p_rows, dh), U32),
        input_output_aliases={2: 0},
        compiler_params=_cparams(("arbitrary",), 16),
    )(dest, hpk, xs_zero)


def _moe_kernel(we_ref, ws_ref, wn_ref, wt_ref, xs_hbm, w1_hbm, w3_hbm, w2_hbm, y_hbm,
                xbuf, xlo, xhi, yacc, ypk, w1buf, w3buf, w2buf, sem_in, sem_out, sem_w):
    w = pl.program_id(0)
    c = pl.program_id(1)
    nw = pl.num_programs(0)
    nc = pl.num_programs(1)
    nrows = wn_ref[w]
    start = ws_ref[w]
    nblk = nrows // MOE_RB
    dh = xbuf.shape[1]
    cw = w1buf.shape[-1]

    g = w * nc + c

    def w_copies(step, do):
        item = step // nc
        chunk = step % nc

        @pl.when((item < nw) & (wn_ref[jnp.minimum(item, nw - 1)] > 0))
        def _():
            e = we_ref[item]
            cols = pl.ds(pl.multiple_of(chunk * cw, cw), cw)
            slot = step % MOE_WSLOTS
            do(pltpu.make_async_copy(w1_hbm.at[e, :, cols], w1buf.at[slot], sem_w.at[slot]))
            do(pltpu.make_async_copy(w3_hbm.at[e, :, cols], w3buf.at[slot], sem_w.at[slot]))
            do(pltpu.make_async_copy(w2_hbm.at[e, cols, :], w2buf.at[slot], sem_w.at[slot]))

    @pl.when(g == 0)
    def _():
        for ahead in range(MOE_WSLOTS - 1):
            w_copies(ahead, lambda cp: cp.start())

    w_copies(g + MOE_WSLOTS - 1, lambda cp: cp.start())
    w_copies(g, lambda cp: cp.wait())
    wslot = g % MOE_WSLOTS
    w1_ref, w3_ref, w2_ref = w1buf.at[wslot], w3buf.at[wslot], w2buf.at[wslot]

    def blk_rows(rb):
        return pl.ds(pl.multiple_of(rb * MOE_RB, MOE_RB), MOE_RB)

    def hbm_rows(item_start, rb):
        return pl.ds(pl.multiple_of(item_start + rb * MOE_RB, MOE_RB), MOE_RB)

    def in_copy(item_start, rb):
        return pltpu.make_async_copy(xs_hbm.at[hbm_rows(item_start, rb)], xbuf.at[blk_rows(rb)], sem_in)

    def out_copy(item_start, rb):
        return pltpu.make_async_copy(ypk.at[blk_rows(rb)], y_hbm.at[hbm_rows(item_start, rb)], sem_out)

    def each_block(n, fn):
        def body(rb, carry):
            fn(rb)
            return carry
        lax.fori_loop(0, n, body, 0)

    @pl.when(c == 0)
    def _load():
        @pl.when(w == 0)
        def _():
            each_block(nblk, lambda rb: in_copy(start, rb).start())

        each_block(nblk, lambda rb: in_copy(start, rb).wait())

        def unpack(rb):
            lo, hi = _unpack_halves(xbuf[blk_rows(rb), :])
            xlo[blk_rows(rb), :] = lo.astype(BF16)
            xhi[blk_rows(rb), :] = hi.astype(BF16)
            yacc[blk_rows(rb), :] = jnp.zeros((MOE_RB, 2 * dh), F32)

        each_block(nblk, unpack)

        @pl.when(w + 1 < nw)
        def _():
            nxt = ws_ref[w + 1]
            each_block(wn_ref[w + 1] // MOE_RB, lambda rb: in_copy(nxt, rb).start())

    @pl.when(nrows > 0)
    def _compute():
        def rows_block(row0, nr):
            rows = pl.ds(row0, nr)
            xl = xlo[rows, :]
            xh = xhi[rows, :]
            h1 = _dot(xl, w1_ref[:dh, :].astype(BF16)) + _dot(xh, w1_ref[dh:, :].astype(BF16))
            h3 = _dot(xl, w3_ref[:dh, :].astype(BF16)) + _dot(xh, w3_ref[dh:, :].astype(BF16))
            hid = (_silu(h1) * h3).astype(BF16)
            yacc[rows, :] = yacc[rows, :] + _dot(hid, w2_ref[...].astype(BF16))

        ntall = nrows // MOE_TALL
        each_block(ntall, lambda i: rows_block(pl.multiple_of(i * MOE_TALL, MOE_TALL), MOE_TALL))

        rem = nrows - ntall * MOE_TALL
        for nr in range(MOE_RB, MOE_TALL, MOE_RB):
            @pl.when(rem == nr)
            def _(nr=nr):
                rows_block(pl.multiple_of(ntall * MOE_TALL, MOE_TALL), nr)

    @pl.when(c == nc - 1)
    def _store():
        @pl.when(w > 0)
        def _():
            prev = ws_ref[w - 1]
            each_block(wn_ref[w - 1] // MOE_RB, lambda rb: out_copy(prev, rb).wait())

        def pack(rb):
            ypk[blk_rows(rb), :] = _pack_halves(yacc[blk_rows(rb), :])

        each_block(nblk, pack)
        each_block(nblk, lambda rb: out_copy(start, rb).start())

        @pl.when(w == nw - 1)
        def _():
            each_block(nblk, lambda rb: out_copy(start, rb).wait())

    @pl.when((w == nw - 1) & (c == nc - 1))
    def _zero_tail():
        ypk[blk_rows(0), :] = jnp.zeros((MOE_RB, dh), U32)
        first = wt_ref[0] // MOE_RB

        def tail_copy(b):
            return pltpu.make_async_copy(ypk.at[blk_rows(0)],
                                         y_hbm.at[pl.ds(pl.multiple_of(b * MOE_RB, MOE_RB), MOE_RB)], sem_out)

        def start(b, carry):
            tail_copy(b).start()
            return carry

        def wait(b, carry):
            tail_copy(b).wait()
            return carry

        lax.fori_loop(first, y_hbm.shape[0] // MOE_RB, start, 0)
        lax.fori_loop(first, y_hbm.shape[0] // MOE_RB, wait, 0)


def _moe(we, ws, wn, wt, xs, w1, w3, w2):
    P, dh = xs.shape
    D = 2 * dh
    NE, _, DE = w1.shape
    cw = min(MOE_CW, DE)
    nc = DE // cw
    nw = we.shape[0]
    return pl.pallas_call(
        _moe_kernel,
        grid_spec=pltpu.PrefetchScalarGridSpec(
            num_scalar_prefetch=4,
            grid=(nw, nc),
            in_specs=[pl.BlockSpec(memory_space=pl.ANY)] * 4,
            out_specs=pl.BlockSpec(memory_space=pl.ANY),
            scratch_shapes=[pltpu.VMEM((MOE_RMAX, dh), U32),
                            pltpu.VMEM((MOE_RMAX, dh), BF16),
                            pltpu.VMEM((MOE_RMAX, dh), BF16),
                            pltpu.VMEM((MOE_RMAX, D), F32),
                            pltpu.VMEM((MOE_RMAX, dh), U32),
                            pltpu.VMEM((MOE_WSLOTS, D, cw), F32),
                            pltpu.VMEM((MOE_WSLOTS, D, cw), F32),
                            pltpu.VMEM((MOE_WSLOTS, cw, D), F32),
                            pltpu.SemaphoreType.DMA(()),
                            pltpu.SemaphoreType.DMA(()),
                            pltpu.SemaphoreType.DMA((MOE_WSLOTS,))]),
        out_shape=jax.ShapeDtypeStruct((P, dh), U32),
        compiler_params=_cparams(("arbitrary", "arbitrary"), 56),
    )(we, ws, wn, wt, xs, w1, w3, w2)


def _moe_schedule(info, NE, p_rows):
    padded = info[1, :NE].astype(I32)
    start_pad = info[2, :NE].astype(I32)
    items = (padded + MOE_RMAX - 1) // MOE_RMAX
    cum_items = jnp.cumsum(items)
    n_items = cum_items[-1]
    nw = (p_rows + NE * (MOE_RMAX - MOE_RB)) // MOE_RMAX
    wi = jnp.arange(nw, dtype=I32)
    valid = wi < n_items
    wi_c = jnp.minimum(wi, jnp.maximum(n_items - 1, 0))
    we = jnp.minimum(jnp.sum(cum_items[None, :] <= wi_c[:, None], axis=1), NE - 1).astype(I32)
    onehot = we[:, None] == jnp.arange(NE, dtype=I32)[None, :]
    pick = lambda v: jnp.sum(jnp.where(onehot, v[None, :], 0), axis=1)
    local = wi_c - (pick(cum_items) - pick(items))
    ws = (pick(start_pad) + local * MOE_RMAX).astype(I32)
    wn = jnp.where(valid, jnp.clip(pick(padded) - local * MOE_RMAX, 0, MOE_RMAX), 0).astype(I32)
    wt = (start_pad[NE - 1] + padded[NE - 1]).reshape(1)
    return we, ws, wn, wt


def _combine_kernel(dest_ref, x1_ref, rt_ref, gt_ref, y_hbm, o_ref, ybuf, sems):
    tm = ROW_TILE
    i = pl.program_id(0)
    n = pl.num_programs(0)
    dh = ybuf.shape[-1]

    def copy(tile, slot, t, k):
        a = (tile * tm + t) * TOP_K + k
        return pltpu.make_async_copy(y_hbm.at[pl.ds(dest_ref[a], 1)], ybuf.at[slot, k, pl.ds(t, 1)], sems.at[slot])

    def issue_tile(tile, slot):
        def body(t, carry):
            for k in range(TOP_K):
                copy(tile, slot, t, k).start(priority=k % 2)
            return carry
        lax.fori_loop(0, tm, body, 0, unroll=DMA_UNROLL)

    @pl.when(i == 0)
    def _():
        issue_tile(0, 0)

    @pl.when(i + 1 < n)
    def _():
        issue_tile(i + 1, (i + 1) % 2)

    slot = i % 2

    for k in range(TOP_K):
        pltpu.make_async_copy(y_hbm.at[pl.ds(0, tm)], ybuf.at[slot, k], sems.at[slot]).wait()

    rt = rt_ref[...]
    g1 = rt[:, 2:3]
    g2 = rt[:, 3:4]
    lo1, hi1 = _unpack_halves(ybuf[slot, 0])
    lo2, hi2 = _unpack_halves(ybuf[slot, 1])
    o_ref[:, :dh] = x1_ref[:, :dh] + gt_ref[:, :dh] * (g1 * lo1 + g2 * lo2)
    o_ref[:, dh:] = x1_ref[:, dh:] + gt_ref[:, dh:] * (g1 * hi1 + g2 * hi2)


def _combine(dest, x1, route, gt2, ypk, S):
    T, D = x1.shape
    tm = ROW_TILE
    per_b = S // tm
    return pl.pallas_call(
        _combine_kernel,
        grid_spec=pltpu.PrefetchScalarGridSpec(
            num_scalar_prefetch=1,
            grid=(T // tm,),
            in_specs=[pl.BlockSpec((tm, D), lambda i, d: (i, 0)),
                      pl.BlockSpec((tm, LANES), lambda i, d: (i, 0)),
                      pl.BlockSpec((None, 1, D), lambda i, d: (i // per_b, 0, 0)),
                      pl.BlockSpec(memory_space=pl.ANY)],
            out_specs=pl.BlockSpec((tm, D), lambda i, d: (i, 0)),
            scratch_shapes=[pltpu.VMEM((2, TOP_K, tm, D // 2), U32),
                            pltpu.SemaphoreType.DMA((2,))]),
        out_shape=jax.ShapeDtypeStruct((T, D), F32),
        compiler_params=_cparams(("arbitrary",), 32),
    )(dest, x1, route, gt2, ypk)


def kernel(x, c, w_ada, b_ada, norm1_g, norm2_g, w_in, qn_g, kn_g, rel_bias, conv_w, A_log, dt_bias,
           onorm_g, w_out, w_rg, b_rg, w_re, b_re, w1, w3, w2):
    B, S, D = x.shape
    depth = w_ada.shape[0]
    HA, hda = rel_bias.shape[1], qn_g.shape[-1]
    HB, hdb = A_log.shape[-1], onorm_g.shape[-1]
    DA, DB = HA * hda, HB * hdb
    NG, NE = w_rg.shape[-1], w_re.shape[-1]
    T = B * S
    assert hdb == LANES and 2 * hda == LANES and DA + DB == D and 16 * HB <= LANES
    assert NE + NG <= LANES and conv_w.shape[1] == 5 and S % ROW_TILE == 0
    n_main = 3 * DA + 4 * DB
    p_rows = TOP_K * T + NE * MOE_RB

    bias_prof = _attn_bias_profiles(rel_bias, S)
    w_in_t = jnp.swapaxes(w_in, 1, 2)
    x2 = x.reshape(T, D)
    for l in range(depth):
        mod = _ada(c, w_ada[l], b_ada[l]).reshape(B, 6, 1, D)
        sh1, sc1, gt1, sh2, sc2, gt2 = (mod[:, i] for i in range(6))

        w_gate = jnp.tile(w_in_t[l][n_main:, :].T, (1, 4))
        w_gate = _hilo_weights(jnp.pad(w_gate, ((0, 0), (0, LANES - w_gate.shape[1]))))
        proj, gat = _inproj(x2, norm1_g[l].reshape(1, D), sc1, sh1, w_in_t[l], n_main, w_gate, S)

        def gate_row(p):
            grp = jnp.concatenate([p.reshape(-1), jnp.zeros((2 * HB,), F32)])
            return jnp.pad(jnp.tile(grp, 4), (0, LANES - 16 * HB)).reshape(1, LANES)

        pack = _gating(gat, gate_row(A_log[l]), gate_row(dt_bias[l]), HB)

        oa = _attention(proj, jnp.tile(qn_g[l], 2).reshape(1, LANES), jnp.tile(kn_g[l], 2).reshape(1, LANES),
                        bias_prof, B, S, HA, hda)
        cw4 = jnp.transpose(conv_w[l].reshape(conv_w.shape[1], 3, HB, hdb), (2, 1, 0, 3))
        ob = _gdn(proj, cw4, pack.reshape(B, S, LANES), onorm_g[l].reshape(1, LANES), B, S, HB, 3 * DA // LANES)

        wr = _hilo_weights(jnp.pad(jnp.concatenate([w_re[l], w_rg[l]], axis=1), ((0, 0), (0, LANES - NE - NG))))
        br = jnp.pad(jnp.concatenate([b_re[l], b_rg[l]]), (0, LANES - NE - NG)).reshape(1, LANES)
        x1, hpk, logits, xs_zero = _outproj(oa, ob, x2, gt1, w_out[l].astype(BF16), norm2_g[l].reshape(1, D),
                                            sc2, sh2, wr, br, S, p_rows)
        route, info = _route(logits, NE, NG)
        dest = route[:, 4:4 + TOP_K].astype(I32).reshape(TOP_K * T)
        we, ws, wn, wt = _moe_schedule(info, NE, p_rows)
        xs = _dispatch(dest, hpk, xs_zero)
        ypk = _moe(we, ws, wn, wt, xs, w1[l], w3[l], w2[l])
        x2 = _combine(dest, x1, route, gt2, ypk, S)
    return x2.reshape(B, S, D)
```
